```python
import numpy as np
import jax
import jax.numpy as jnp
from jax import lax

D_MODEL = 1024
BATCH = 8
SEQ = 2048
DEPTH = 1
DEC_BATCH = 32
DEC_SEQ = 4
PAST_LEN = 16384
PAGE_SIZE = 128

MLA_HEADS = 8
MLA_Q_RANK = 384
MLA_KV_RANK = 256
MLA_D_NOPE = 64
MLA_D_ROPE = 32
MLA_D_V = 64
ROPE_THETA = 10000.0
MLA_SCALE = (MLA_D_NOPE + MLA_D_ROPE) ** -0.5
NSA_HEADS = 8
NSA_GROUPS = 2
NSA_HPG = NSA_HEADS // NSA_GROUPS
NSA_DH = 64
NSA_SCALE = NSA_DH ** -0.5
CMP_BLOCK = 32
CMP_STRIDE = 16
CMP_HID = 64
SLC_BLOCK = 64
SLC_TOP_N = 16
WINDOW = 512
D_FF = 2816
CONV_W = 3
Q_BLOCK = 128
SLC_Q_BLOCK = 64
EPS = 1e-6
NEG = -1e30
FORCE = 1e9
IN_TOTAL = (MLA_Q_RANK + MLA_KV_RANK + MLA_D_ROPE + NSA_HEADS * NSA_DH
            + 3 * (2 * NSA_GROUPS * NSA_DH) + 3 * NSA_HEADS + 2 * D_MODEL)

kernel_name = 'hybrid_mla_nsa_convffn_step'


def rms_norm(x, g):
    xf = x.astype(jnp.float32)
    y = xf * lax.rsqrt(jnp.mean(xf * xf, axis=-1, keepdims=True) + EPS)
    return (y * g.astype(jnp.float32)).astype(x.dtype)


def rope(x, pos):
    d = x.shape[-1]
    inv = ROPE_THETA ** (-jnp.arange(0, d, 2, dtype=jnp.float32) / d)
    ang = pos.astype(jnp.float32)[:, None] * inv[None, :]
    cos = jnp.cos(ang)[None, :, None, :]
    sin = jnp.sin(ang)[None, :, None, :]
    x1, x2 = jnp.split(x.astype(jnp.float32), 2, axis=-1)
    return jnp.concatenate([x1 * cos - x2 * sin, x1 * sin + x2 * cos], axis=-1).astype(x.dtype)


def alibi_slopes():
    h = jnp.arange(1, NSA_HEADS + 1, dtype=jnp.float32)
    return (2.0 ** (-8.0 * h / NSA_HEADS)).reshape(NSA_GROUPS, NSA_HPG)


def masked_softmax(s, mask, axis=-1):
    p = jax.nn.softmax(jnp.where(mask, s, NEG), axis=axis)
    return jnp.where(mask, p, 0.0)


def to_blocks(x, blk):
    b, t = x.shape[:2]
    return jnp.moveaxis(x.reshape((b, t // blk, blk) + x.shape[2:]), 1, 0)


def from_blocks(x):
    n, b, blk = x.shape[:3]
    return jnp.moveaxis(x, 0, 1).reshape((b, n * blk) + x.shape[3:])


def mixer_inputs(h, pos, lp):
    b, t, _ = h.shape
    sizes = [MLA_Q_RANK, MLA_KV_RANK, MLA_D_ROPE, NSA_HEADS * NSA_DH,
             2 * NSA_GROUPS * NSA_DH, 2 * NSA_GROUPS * NSA_DH, 2 * NSA_GROUPS * NSA_DH,
             3 * NSA_HEADS, D_MODEL, D_MODEL]
    cuts = np.cumsum(sizes)[:-1].tolist()
    cq, ckv, kr, qn, kvc, kvs, kvw, gn, ga, gb = jnp.split(h @ lp['w_in'], cuts, axis=-1)
    cq = rms_norm(cq, lp['q_norm_g'])
    q = jnp.einsum('btr,rhd->bthd', cq, lp['w_uq'])
    q_nope = q[..., :MLA_D_NOPE]
    q_rope = rope(q[..., MLA_D_NOPE:], pos)
    q_abs = jnp.einsum('bthn,rhn->bthr', q_nope, lp['w_uk'])
    ckv = rms_norm(ckv, lp['kv_norm_g'])
    krope = rope(kr[:, :, None, :], pos)[:, :, 0, :]
    qn = qn.reshape(b, t, NSA_HEADS, NSA_DH)

    def kv(z):
        z = z.reshape(b, t, 2, NSA_GROUPS, NSA_DH)
        return z[:, :, 0], z[:, :, 1]

    kc, vc = kv(kvc)
    ks, vs = kv(kvs)
    kw, vw = kv(kvw)
    gn = jax.nn.sigmoid(gn.reshape(b, t, NSA_HEADS, 3))
    return (q_abs, q_rope, ckv, krope, qn, kc, vc, ks, vs, kw, vw, gn,
            jax.nn.sigmoid(ga), jax.nn.sigmoid(gb))


def mla_core(q_abs, q_rope, ckv, krope, q_pos, k_pos):
    s = (jnp.einsum('bqhr,bkr->bhqk', q_abs, ckv).astype(jnp.float32)
         + jnp.einsum('bqhd,bkd->bhqk', q_rope, krope).astype(jnp.float32)) * MLA_SCALE
    mask = (k_pos[None, :] <= q_pos[:, None])[None, None]
    p = masked_softmax(s, mask)
    return jnp.einsum('bhqk,bkr->bqhr', p.astype(ckv.dtype), ckv)


def compress(k, pos_emb, w1, w2):
    b, t = k.shape[:2]
    n_chunks = t // CMP_STRIDE
    ratio = CMP_BLOCK // CMP_STRIDE
    n_cmp = n_chunks - ratio + 1
    kc = k[:, :n_chunks * CMP_STRIDE].reshape(b, n_chunks, CMP_STRIDE, NSA_GROUPS, NSA_DH)
    blocks = jnp.concatenate([kc[:, j:j + n_cmp] for j in range(ratio)], axis=2)
    blocks = blocks + pos_emb[None, None, :, None, :]
    flat = jnp.moveaxis(blocks, 3, 2).reshape(b, n_cmp, NSA_GROUPS, CMP_BLOCK * NSA_DH)
    return jax.nn.gelu(flat @ w1) @ w2


def cmp_attend(q, kc, vc, q_pos, slopes):
    b, tq = q.shape[:2]
    n_cmp = kc.shape[1]
    end = jnp.arange(n_cmp) * CMP_STRIDE + CMP_BLOCK - 1
    qg = q.reshape(b, tq, NSA_GROUPS, NSA_HPG, NSA_DH)
    dist = (q_pos[:, None] - end[None, :]).astype(jnp.float32)
    s = jnp.einsum('bqgpd,bcgd->bgpqc', qg, kc).astype(jnp.float32) * NSA_SCALE
    s = s - slopes[None, :, :, None, None] * dist
    p = masked_softmax(s, dist >= 0)
    o = jnp.einsum('bgpqc,bcgd->bqgpd', p.astype(vc.dtype), vc).reshape(b, tq, NSA_HEADS, NSA_DH)
    return o, p.sum(axis=2)


def select_blocks(imp, q_pos, n_slc):
    n_cmp = imp.shape[-1]
    i = jnp.arange(n_cmp)[:, None]
    j = jnp.arange(n_slc)[None, :]
    overlap = ((i * CMP_STRIDE < (j + 1) * SLC_BLOCK)
               & (i * CMP_STRIDE + CMP_BLOCK > j * SLC_BLOCK)).astype(jnp.float32)
    score = jnp.einsum('bgqc,cs->bqgs', imp, overlap)
    cur = (q_pos // SLC_BLOCK)[:, None, None]
    jj = jnp.arange(n_slc)[None, None, :]
    forced = (jj == 0) | (jj == cur) | (jj == cur - 1)
    score = jnp.where(forced, FORCE, score)
    score = jnp.where(jj <= cur, score, NEG)
    _, idx = lax.top_k(score, min(SLC_TOP_N, n_slc))
    valid = idx <= (q_pos // SLC_BLOCK)[None, :, None, None]
    return idx, valid


def to_slc_blocks(k, n_slc):
    b, t = k.shape[:2]
    kp = jnp.pad(k, ((0, 0), (0, n_slc * SLC_BLOCK - t), (0, 0), (0, 0)))
    return jnp.transpose(kp.reshape(b, n_slc, SLC_BLOCK, NSA_GROUPS, NSA_DH), (0, 3, 1, 2, 4))


def slc_attend(q, kb, vb, idx, valid, q_pos, slopes):
    b, tq = q.shape[:2]
    bi = jnp.arange(b)[:, None, None, None]
    gi = jnp.arange(NSA_GROUPS)[None, None, :, None]
    ks = kb[bi, gi, idx]
    vs = vb[bi, gi, idx]
    kpos = idx[..., None] * SLC_BLOCK + jnp.arange(SLC_BLOCK)
    dist = (q_pos[None, :, None, None, None] - kpos).astype(jnp.float32)
    qg = q.reshape(b, tq, NSA_GROUPS, NSA_HPG, NSA_DH)
    s = jnp.einsum('bqgpd,bqgnsd->bqgpns', qg, ks).astype(jnp.float32) * NSA_SCALE
    s = s - slopes[None, None, :, :, None, None] * dist[:, :, :, None]
    mask = (valid[..., None] & (dist >= 0))[:, :, :, None]
    p = masked_softmax(s, mask, axis=(-2, -1))
    return jnp.einsum('bqgpns,bqgnsd->bqgpd', p.astype(vs.dtype), vs).reshape(b, tq, NSA_HEADS, NSA_DH)


def band_attend(qb, kb, vb, qpos, kpos, slopes):
    b, n, tq = qb.shape[:3]
    qg = qb.reshape(b, n, tq, NSA_GROUPS, NSA_HPG, NSA_DH)
    dist = (qpos[:, :, None] - kpos[:, None, :]).astype(jnp.float32)
    s = jnp.einsum('bnqgpd,bnkgd->bgpnqk', qg, kb).astype(jnp.float32) * NSA_SCALE
    s = s - slopes[None, :, :, None, None, None] * dist
    mask = (dist >= 0) & (dist < WINDOW) & (kpos[:, None, :] >= 0)
    p = masked_softmax(s, mask)
    return jnp.einsum('bgpnqk,bnkgd->bnqgpd', p.astype(vb.dtype), vb).reshape(b, n, tq, NSA_HEADS, NSA_DH)


def finish(x, o_lat, o_cmp, o_slc, o_win, gn, ga, gb, conv_prev, lp):
    b, t, _ = x.shape
    o_mla = jnp.einsum('bthr,rhv->bthv', o_lat, lp['w_uv']).reshape(b, t, MLA_HEADS * MLA_D_V)
    o_nsa = (gn[..., 0:1] * o_cmp + gn[..., 1:2] * o_slc + gn[..., 2:3] * o_win).reshape(b, t, NSA_HEADS * NSA_DH)
    merged = ga * (o_mla @ lp['w_proj_mla']) + gb * (o_nsa @ lp['w_proj_nsa'])
    x1 = x + merged @ lp['w_out']
    h2 = rms_norm(x1, lp['norm2_g'])
    g = h2 @ lp['w_gate']
    u = h2 @ lp['w_up']
    gp = jnp.concatenate([conv_prev, g], axis=1)
    conv = lp['conv_b']
    for j in range(CONV_W):
        conv = conv + lp['conv_w'][j] * gp[:, j:j + t]
    x2 = x1 + (jax.nn.silu(conv) * u) @ lp['w_down']
    return x2, gp[:, -(CONV_W - 1):]


def layer_prompt(x, lp, slopes):
    b, t, _ = x.shape
    pos = jnp.arange(t, dtype=jnp.int32)
    h = rms_norm(x, lp['norm1_g'])
    (q_abs, q_rope, ckv, krope, qn, kc, vc, ks, vs, kw, vw, gn, ga, gb) = mixer_inputs(h, pos, lp)
    o_lat = from_blocks(lax.map(
        lambda a: mla_core(a[0], a[1], ckv, krope, a[2], pos),
        (to_blocks(q_abs, Q_BLOCK), to_blocks(q_rope, Q_BLOCK), pos.reshape(-1, Q_BLOCK))))
    kcc = compress(kc, lp['cmp_pos_k'], lp['cmp_w1_k'], lp['cmp_w2_k'])
    vcc = compress(vc, lp['cmp_pos_v'], lp['cmp_w1_v'], lp['cmp_w2_v'])
    o_cmp, imp = cmp_attend(qn, kcc, vcc, pos, slopes)
    n_slc = -(-t // SLC_BLOCK)
    idx, valid = select_blocks(imp, pos, n_slc)
    kb = to_slc_blocks(ks, n_slc)
    vb = to_slc_blocks(vs, n_slc)
    o_slc = from_blocks(lax.map(
        lambda a: slc_attend(a[0], kb, vb, a[1], a[2], a[3], slopes),
        (to_blocks(qn, SLC_Q_BLOCK), to_blocks(idx, SLC_Q_BLOCK), to_blocks(valid, SLC_Q_BLOCK),
         pos.reshape(-1, SLC_Q_BLOCK))))
    nqb = t // Q_BLOCK
    band = WINDOW + Q_BLOCK
    gidx = jnp.arange(nqb)[:, None] * Q_BLOCK + jnp.arange(band)[None, :]
    kp = jnp.pad(kw, ((0, 0), (WINDOW, 0), (0, 0), (0, 0)))
    vp = jnp.pad(vw, ((0, 0), (WINDOW, 0), (0, 0), (0, 0)))
    o_win = band_attend(qn.reshape(b, nqb, Q_BLOCK, NSA_HEADS, NSA_DH), kp[:, gidx], vp[:, gidx],
                        pos.reshape(nqb, Q_BLOCK), gidx - WINDOW, slopes).reshape(b, t, NSA_HEADS, NSA_DH)
    conv0 = jnp.zeros((b, CONV_W - 1, D_FF), x.dtype)
    y, conv_state = finish(x, o_lat, o_cmp, o_slc, o_win, gn, ga, gb, conv0, lp)
    n_keep = min(WINDOW, t)
    return y, (ckv, krope, kc, vc, ks, vs, kw[:, -n_keep:], vw[:, -n_keep:], conv_state)


def layer_sample(x, lc, page_table, lp, slopes):
    (c_ckv, c_krope, c_cmp_k, c_cmp_v, c_slc_k, c_slc_v, s_win_k, s_win_v, s_conv) = lc
    b, t, _ = x.shape
    past = page_table.shape[1] * PAGE_SIZE
    pos = past + jnp.arange(t, dtype=jnp.int32)
    all_pos = jnp.arange(past + t, dtype=jnp.int32)
    h = rms_norm(x, lp['norm1_g'])
    (q_abs, q_rope, ckv, krope, qn, kc, vc, ks, vs, kw, vw, gn, ga, gb) = mixer_inputs(h, pos, lp)

    def gather(pool, new):
        rows = pool[page_table].reshape((b, past) + pool.shape[2:])
        return jnp.concatenate([rows, new], axis=1)

    o_lat = mla_core(q_abs, q_rope, gather(c_ckv, ckv), gather(c_krope, krope), pos, all_pos)
    kc_all = gather(c_cmp_k, kc)
    vc_all = gather(c_cmp_v, vc)
    kcc = compress(kc_all, lp['cmp_pos_k'], lp['cmp_w1_k'], lp['cmp_w2_k'])
    vcc = compress(vc_all, lp['cmp_pos_v'], lp['cmp_w1_v'], lp['cmp_w2_v'])
    o_cmp, imp = cmp_attend(qn, kcc, vcc, pos, slopes)
    n_slc = -(-(past + t) // SLC_BLOCK)
    idx, valid = select_blocks(imp, pos, n_slc)
    o_slc = slc_attend(qn, to_slc_blocks(gather(c_slc_k, ks), n_slc), to_slc_blocks(gather(c_slc_v, vs), n_slc),
                       idx, valid, pos, slopes)
    n_buf = s_win_k.shape[1]
    kw_all = jnp.concatenate([s_win_k, kw], axis=1)
    vw_all = jnp.concatenate([s_win_v, vw], axis=1)
    kpos = past - n_buf + jnp.arange(n_buf + t, dtype=jnp.int32)
    o_win = band_attend(qn[:, None], kw_all[:, None], vw_all[:, None], pos[None], kpos[None], slopes)[:, 0]
    y, conv_state = finish(x, o_lat, o_cmp, o_slc, o_win, gn, ga, gb, s_conv, lp)
    return y, (ckv, krope, kc, vc, ks, vs, kw_all[:, -n_buf:], vw_all[:, -n_buf:], conv_state)


def setup_inputs(seed: int = 0) -> dict:
    key = jax.random.key(seed)
    keys = iter(jax.random.split(key, 64))

    def nrm(shape, scale):
        return jax.random.normal(next(keys), shape, jnp.float32) * scale

    def gain(shape):
        return 1.0 + nrm(shape, 0.02)

    n_pages = PAST_LEN // PAGE_SIZE
    n_pool = (DEC_BATCH * n_pages * 5) // 4
    win_buf = min(WINDOW, PAST_LEN)
    L = DEPTH
    page_table = jax.random.permutation(next(keys), n_pool)[:DEC_BATCH * n_pages]
    page_table = page_table.reshape(DEC_BATCH, n_pages).astype(jnp.int32)
    kvp = (L, n_pool, PAGE_SIZE, NSA_GROUPS, NSA_DH)
    return {
        'x_prompt': nrm((BATCH, SEQ, D_MODEL), 1.0),
        'x_sample': nrm((DEC_BATCH, DEC_SEQ, D_MODEL), 1.0),
        'cache_mla_ckv': nrm((L, n_pool, PAGE_SIZE, MLA_KV_RANK), 1.0),
        'cache_mla_krope': nrm((L, n_pool, PAGE_SIZE, MLA_D_ROPE), 1.0),
        'cache_nsa_cmp_k': nrm(kvp, 1.0),
        'cache_nsa_cmp_v': nrm(kvp, 1.0),
        'cache_nsa_slc_k': nrm(kvp, 1.0),
        'cache_nsa_slc_v': nrm(kvp, 1.0),
        'state_win_k': nrm((L, DEC_BATCH, win_buf, NSA_GROUPS, NSA_DH), 1.0),
        'state_win_v': nrm((L, DEC_BATCH, win_buf, NSA_GROUPS, NSA_DH), 1.0),
        'state_ffn_conv': nrm((L, DEC_BATCH, CONV_W - 1, D_FF), 1.0),
        'page_table': page_table,
        'norm1_g': gain((L, D_MODEL)),
        'w_in': nrm((L, D_MODEL, IN_TOTAL), D_MODEL ** -0.5),
        'q_norm_g': gain((L, MLA_Q_RANK)),
        'kv_norm_g': gain((L, MLA_KV_RANK)),
        'w_uq': nrm((L, MLA_Q_RANK, MLA_HEADS, MLA_D_NOPE + MLA_D_ROPE), MLA_Q_RANK ** -0.5),
        'w_uk': nrm((L, MLA_KV_RANK, MLA_HEADS, MLA_D_NOPE), MLA_KV_RANK ** -0.5),
        'w_uv': nrm((L, MLA_KV_RANK, MLA_HEADS, MLA_D_V), MLA_KV_RANK ** -0.5),
        'cmp_pos_k': nrm((L, CMP_BLOCK, NSA_DH), 0.1),
        'cmp_w1_k': nrm((L, CMP_BLOCK * NSA_DH, CMP_HID), (CMP_BLOCK * NSA_DH) ** -0.5),
        'cmp_w2_k': nrm((L, CMP_HID, NSA_DH), CMP_HID ** -0.5),
        'cmp_pos_v': nrm((L, CMP_BLOCK, NSA_DH), 0.1),
        'cmp_w1_v': nrm((L, CMP_BLOCK * NSA_DH, CMP_HID), (CMP_BLOCK * NSA_DH) ** -0.5),
        'cmp_w2_v': nrm((L, CMP_HID, NSA_DH), CMP_HID ** -0.5),
        'w_proj_mla': nrm((L, MLA_HEADS * MLA_D_V, D_MODEL), (MLA_HEADS * MLA_D_V) ** -0.5),
        'w_proj_nsa': nrm((L, NSA_HEADS * NSA_DH, D_MODEL), (NSA_HEADS * NSA_DH) ** -0.5),
        'w_out': nrm((L, D_MODEL, D_MODEL), D_MODEL ** -0.5),
        'norm2_g': gain((L, D_MODEL)),
        'w_gate': nrm((L, D_MODEL, D_FF), D_MODEL ** -0.5),
        'w_up': nrm((L, D_MODEL, D_FF), D_MODEL ** -0.5),
        'conv_w': nrm((L, CONV_W, D_FF), CONV_W ** -0.5),
        'conv_b': nrm((L, D_FF), 0.02),
        'w_down': nrm((L, D_FF, D_MODEL), D_FF ** -0.5),
        'norm_f_g': gain((D_MODEL,)),
    }


def reference(x_prompt, x_sample, cache_mla_ckv, cache_mla_krope, cache_nsa_cmp_k, cache_nsa_cmp_v,
              cache_nsa_slc_k, cache_nsa_slc_v, state_win_k, state_win_v, state_ffn_conv, page_table,
              norm1_g, w_in, q_norm_g, kv_norm_g, w_uq, w_uk, w_uv,
              cmp_pos_k, cmp_w1_k, cmp_w2_k, cmp_pos_v, cmp_w1_v, cmp_w2_v,
              w_proj_mla, w_proj_nsa, w_out, norm2_g, w_gate, w_up, conv_w, conv_b, w_down, norm_f_g):
    slopes = alibi_slopes()
    yp, ys = x_prompt, x_sample
    p_states, s_states = [], []
    for l in range(DEPTH):
        lp = {'norm1_g': norm1_g[l], 'w_in': w_in[l], 'q_norm_g': q_norm_g[l], 'kv_norm_g': kv_norm_g[l],
              'w_uq': w_uq[l], 'w_uk': w_uk[l], 'w_uv': w_uv[l],
              'cmp_pos_k': cmp_pos_k[l], 'cmp_w1_k': cmp_w1_k[l], 'cmp_w2_k': cmp_w2_k[l],
              'cmp_pos_v': cmp_pos_v[l], 'cmp_w1_v': cmp_w1_v[l], 'cmp_w2_v': cmp_w2_v[l],
              'w_proj_mla': w_proj_mla[l], 'w_proj_nsa': w_proj_nsa[l], 'w_out': w_out[l],
              'norm2_g': norm2_g[l], 'w_gate': w_gate[l], 'w_up': w_up[l],
              'conv_w': conv_w[l], 'conv_b': conv_b[l], 'w_down': w_down[l]}
        lc = (cache_mla_ckv[l], cache_mla_krope[l], cache_nsa_cmp_k[l], cache_nsa_cmp_v[l],
              cache_nsa_slc_k[l], cache_nsa_slc_v[l], state_win_k[l], state_win_v[l], state_ffn_conv[l])
        yp, ps = layer_prompt(yp, lp, slopes)
        ys, ss = layer_sample(ys, lc, page_table, lp, slopes)
        p_states.append(ps)
        s_states.append(ss)
    (p_ckv, p_krope, p_cmp_k, p_cmp_v, p_slc_k, p_slc_v, p_win_k, p_win_v, p_conv) = [
        jnp.stack(s) for s in zip(*p_states)]
    (s_ckv, s_krope, s_cmp_k, s_cmp_v, s_slc_k, s_slc_v, s_win_k, s_win_v, s_conv) = [
        jnp.stack(s) for s in zip(*s_states)]
    y_prompt = rms_norm(yp, norm_f_g)
    y_sample = rms_norm(ys, norm_f_g)
    return (y_prompt, y_sample, p_ckv, s_ckv, p_krope, s_krope, p_cmp_k, s_cmp_k, p_cmp_v, s_cmp_v,
            p_slc_k, s_slc_k, p_slc_v, s_slc_v, p_win_k, s_win_k, p_win_v, s_win_v, p_conv, s_conv)
```

```python
import functools

import numpy as np
import jax
import jax.numpy as jnp
from jax import lax
from jax.experimental import pallas as pl
from jax.experimental.pallas import tpu as pltpu

MLA_HEADS = 8
MLA_D_NOPE = 64
MLA_D_ROPE = 32
MLA_D_V = 64
ROPE_THETA = 10000.0
MLA_SCALE = (MLA_D_NOPE + MLA_D_ROPE) ** -0.5
NSA_HEADS = 8
NSA_GROUPS = 2
NSA_HPG = NSA_HEADS // NSA_GROUPS
NSA_DH = 64
NSA_SCALE = NSA_DH ** -0.5
CMP_BLOCK = 32
CMP_STRIDE = 16
SLC_BLOCK = 64
SLC_TOP_N = 16
WINDOW = 512
CONV_W = 3
PAGE_SIZE = 128
EPS = 1e-6
NEG = -1e30
FORCE = 1e9

LANE = 128
HEAD_PAD = 128
CHUNK_FEATS = CMP_STRIDE * NSA_GROUPS * NSA_DH
SLOPES = tuple(float(2.0 ** (-8.0 * (h + 1) / NSA_HEADS)) for h in range(NSA_HEADS))
VMEM_LIMIT = 56 * 1024 * 1024

F32 = jnp.float32
BF16 = jnp.bfloat16
_NT = (((1,), (1,)), ((), ()))


def _cparams(*sem):
    return pltpu.CompilerParams(dimension_semantics=sem, vmem_limit_bytes=VMEM_LIMIT)


def _rms(x, g):
    return x * lax.rsqrt(jnp.mean(x * x, axis=-1, keepdims=True) + EPS) * g


def _dot(a, b):
    return jnp.dot(a, b, preferred_element_type=F32)


def _dot_nt(a, b):
    return lax.dot_general(a, b, _NT, preferred_element_type=F32)


def _dot_exact(a, b):
    return jnp.dot(a, b, preferred_element_type=F32, precision=lax.Precision.HIGHEST)


def _iota(shape, dim):
    return lax.broadcasted_iota(jnp.int32, shape, dim)


def _log2(n):
    assert n > 0 and n & (n - 1) == 0, n
    return n.bit_length() - 1


def _vdiv(x, n):
    return lax.shift_right_logical(x, jnp.full(x.shape, _log2(n), jnp.int32))


def _vmod(x, n):
    assert n & (n - 1) == 0, n
    return x & (n - 1)


_O_CQ, _O_CKV, _O_QN, _O_K6, _O_GA = 0, 384, 640, 1152, 1920


def _inproj_kernel(x_ref, g1_ref, w_ref, qg_ref, kvg_ref, wuq_ref, wk_ref, wv_ref, cos_ref, sin_ref,
                   *outs, sample, q_rank, kv_rank, d_model):
    o_gb = _O_GA + d_model
    o_kr = o_gb + d_model
    x = x_ref[...]
    hn = _rms(x, g1_ref[...])
    y = _dot(hn.astype(BF16), w_ref[...])
    cosq = cos_ref[...]
    sinq = sin_ref[...]
    nq = MLA_HEADS * HEAD_PAD
    cqn = _rms(y[:, _O_CQ:_O_CQ + q_rank], qg_ref[...])
    q2 = _dot(cqn.astype(BF16), wuq_ref[...])
    ckv = _rms(y[:, _O_CKV:_O_CKV + kv_rank], kvg_ref[...])
    kr = y[:, o_kr:o_kr + LANE] * cosq + y[:, o_kr + LANE:o_kr + 2 * LANE] * sinq
    ckv_b = ckv.astype(BF16)
    it = iter(outs)
    if sample:
        qrot_ref, qabs_ref = next(it), next(it)
        for h in range(MLA_HEADS):
            sl = slice(h * HEAD_PAD, (h + 1) * HEAD_PAD)
            qh = (q2[:, sl] * cosq + q2[:, nq + h * HEAD_PAD:nq + (h + 1) * HEAD_PAD] * sinq) * MLA_SCALE
            qrot_ref[:, sl] = qh
            qabs_ref[:, h * kv_rank:(h + 1) * kv_rank] = _dot(qh.astype(BF16), wk_ref[h])
    else:
        q_ref, k_ref, v_ref = next(it), next(it), next(it)
        knp = _dot(ckv_b, wk_ref[...])
        for h in range(MLA_HEADS):
            sl = slice(h * HEAD_PAD, (h + 1) * HEAD_PAD)
            qh = (q2[:, sl] * cosq + q2[:, nq + h * HEAD_PAD:nq + (h + 1) * HEAD_PAD] * sinq) * MLA_SCALE
            q_ref[:, sl] = qh.astype(BF16)
            k_ref[:, sl] = (knp[:, sl] + kr).astype(BF16)
        v_ref[...] = _dot(ckv_b, wv_ref[...]).astype(BF16)
    ckv_ref, kr_ref, qn_ref = next(it), next(it), next(it)
    ckv_ref[...] = ckv
    kr_ref[...] = kr
    qn_ref[...] = (y[:, _O_QN:_O_QN + NSA_HEADS * NSA_DH] * NSA_SCALE).astype(BF16)
    for j in range(6):
        next(it)[...] = y[:, _O_K6 + j * LANE:_O_K6 + (j + 1) * LANE]
    if not sample:
        for j in range(2, 6):
            next(it)[...] = y[:, _O_K6 + j * LANE:_O_K6 + (j + 1) * LANE].astype(BF16)
    gn_ref, ga_ref, gb_ref = next(it), next(it), next(it)
    gn_ref[...] = jax.nn.sigmoid(y[:, o_kr + 2 * LANE:o_kr + 3 * LANE])
    ga_ref[...] = jax.nn.sigmoid(y[:, _O_GA:_O_GA + d_model])
    gb_ref[...] = jax.nn.sigmoid(y[:, o_gb:o_gb + d_model])


def _inproj(x, wts, cosq, sinq, *, sample, tm, tab_blocks):
    n, d = x.shape
    q_rank, kv_rank = wts['q_norm_g'].shape[1], wts['kv_norm_g'].shape[1]
    nq = MLA_HEADS * HEAD_PAD
    wk = wts['w_ukT'] if sample else wts['w_ukp']
    row = lambda i: (i, 0)
    const2 = lambda i: (0, 0)
    tab_map = (lambda i: (i % tab_blocks, 0))
    in_specs = [
        pl.BlockSpec((tm, d), row),
        pl.BlockSpec((1, d), const2),
        pl.BlockSpec(wts['w_in'].shape, const2),
        pl.BlockSpec((1, q_rank), const2),
        pl.BlockSpec((1, kv_rank), const2),
        pl.BlockSpec(wts['w_uq2'].shape, const2),
        pl.BlockSpec(wk.shape, (lambda i: (0, 0, 0)) if sample else const2),
        pl.BlockSpec(wts['w_uvf'].shape, const2),
        pl.BlockSpec((tm, LANE), tab_map),
        pl.BlockSpec((tm, LANE), tab_map),
    ]
    shapes = []
    if sample:
        shapes += [(nq, F32), (MLA_HEADS * kv_rank, F32)]
    else:
        shapes += [(nq, BF16), (nq, BF16), (MLA_HEADS * MLA_D_V, BF16)]
    shapes += [(kv_rank, F32), (LANE, F32), (NSA_HEADS * NSA_DH, BF16)]
    shapes += [(LANE, F32)] * 6
    if not sample:
        shapes += [(LANE, BF16)] * 4
    shapes += [(LANE, F32), (d, F32), (d, F32)]
    out_shape = [jax.ShapeDtypeStruct((n, w), dt) for w, dt in shapes]
    out_specs = [pl.BlockSpec((tm, w), row) for w, _ in shapes]
    kern = functools.partial(_inproj_kernel, sample=sample, q_rank=q_rank, kv_rank=kv_rank, d_model=d)
    return pl.pallas_call(
        kern, grid=(n // tm,), in_specs=in_specs, out_specs=out_specs, out_shape=out_shape,
        compiler_params=_cparams("parallel"), name="inproj_sample" if sample else "inproj_prompt",
    )(x, wts['norm1_g'], wts['w_in'], wts['q_norm_g'], wts['kv_norm_g'], wts['w_uq2'], wk, wts['w_uvf'],
      cosq, sinq)


def _flash_kernel(*refs, tq, tk, groups, dq, dk, dv, window, has_sel, n_sel):
    if has_sel:
        q_ref, k_ref, v_ref, sel_ref, o_ref = refs
    else:
        q_ref, k_ref, v_ref, o_ref = refs
    q_start = pl.program_id(1) * tq
    n_hi = (q_start + tq - 1) // tk + 1
    n_lo = jnp.maximum(q_start - (window - 1), 0) // tk if window else 0
    for gi, (heads, kcol, slopes) in enumerate(groups):
        nh = len(heads)
        rows = nh * tq
        if nh == 1:
            qg = q_ref[:, heads[0] * dq:(heads[0] + 1) * dq]
        else:
            qg = jnp.concatenate([q_ref[:, h * dq:(h + 1) * dq] for h in heads], axis=0)
        qpos = q_start + (_iota((rows, 1), 0) & (tq - 1))
        use_alibi = any(s != 0.0 for s in slopes)
        if use_alibi:
            hrow = _vdiv(_iota((rows, 1), 0), tq)
            slope = jnp.zeros((rows, 1), F32)
            for hh, s in enumerate(slopes):
                slope = jnp.where(hrow == hh, s, slope)
        if has_sel:
            selg = sel_ref[:, gi * n_sel:(gi + 1) * n_sel].astype(BF16)

        def body(j, carry):
            m, l, acc = carry
            k0 = pl.multiple_of(j * tk, tk)
            kt = k_ref[pl.ds(k0, tk), kcol * dk:(kcol + 1) * dk]
            vt = v_ref[pl.ds(k0, tk), kcol * dv:(kcol + 1) * dv]
            s = _dot_nt(qg, kt)
            dist = qpos - (k0 + _iota((1, tk), 1))
            mask = dist >= 0
            if window:
                mask = mask & (dist < window)
            if use_alibi:
                s = s - slope * dist.astype(F32)
            if has_sel:
                blk = _vdiv(k0 + _iota((n_sel, tk), 1), SLC_BLOCK)
                expand = (blk == _iota((n_sel, tk), 0)).astype(BF16)
                selx = _dot(selg, expand) > 0.5
                if nh > 1:
                    selx = jnp.concatenate([selx] * nh, axis=0)
                mask = mask & selx
            s = jnp.where(mask, s, NEG)
            m_new = jnp.maximum(m, jnp.max(s, axis=-1, keepdims=True))
            p = jnp.exp(s - m_new)
            alpha = jnp.exp(m - m_new)
            l = alpha * l + jnp.sum(p, axis=-1, keepdims=True)
            acc = alpha * acc + _dot(p.astype(BF16), vt)
            return m_new, l, acc

        init = (jnp.full((rows, 1), NEG, F32), jnp.zeros((rows, 1), F32), jnp.zeros((rows, dv), F32))
        m, l, acc = lax.fori_loop(n_lo, n_hi, body, init)
        o = acc / l
        for hh, h in enumerate(heads):
            o_ref[:, h * dv:(h + 1) * dv] = o[hh * tq:(hh + 1) * tq].astype(o_ref.dtype)


def _flash(q, k, v, sel, *, batch, seq, tq, tk, groups, dq, dk, dv, window, name):
    n = q.shape[0]
    has_sel = sel is not None
    n_heads = sum(len(g[0]) for g in groups)
    qrow = lambda b, i: (b * (seq // tq) + i, 0)
    kv = lambda b, i: (b, 0)
    in_specs = [pl.BlockSpec((tq, q.shape[1]), qrow), pl.BlockSpec((seq, k.shape[1]), kv),
                pl.BlockSpec((seq, v.shape[1]), kv)]
    args = [q, k, v]
    n_sel = 0
    if has_sel:
        n_sel = sel.shape[1] // len(groups)
        in_specs.append(pl.BlockSpec((tq, sel.shape[1]), qrow))
        args.append(sel)
    kern = functools.partial(_flash_kernel, tq=tq, tk=tk, groups=groups, dq=dq, dk=dk, dv=dv, window=window,
                             has_sel=has_sel, n_sel=n_sel)
    return pl.pallas_call(
        kern, grid=(batch, seq // tq), in_specs=in_specs,
        out_specs=pl.BlockSpec((tq, n_heads * dv), qrow),
        out_shape=jax.ShapeDtypeStruct((n, n_heads * dv), BF16),
        compiler_params=_cparams("parallel", "arbitrary"), name=name,
    )(*args)


def _compress_rows(xk, xv, w1k_ref, w1v_ref, posk_ref, posv_ref):
    yk = _dot(xk.astype(BF16), w1k_ref[...])
    yv = _dot(xv.astype(BF16), w1v_ref[...])
    return yk, yv


def _compress_finish(y, posy, w2_ref):
    rows = y.shape[0]
    a = y[:, :LANE]
    b = pltpu.roll(y[:, LANE:], rows - 1, 0)
    pos = posy[0:1, :LANE] + posy[1:2, LANE:]
    hid = jax.nn.gelu(a + b + pos)
    return _dot(hid.astype(BF16), w2_ref[...])


def _compress_prompt_kernel(xk_ref, xv_ref, w1k_ref, w1v_ref, pk_ref, pv_ref, w2k_ref, w2v_ref, ok_ref, ov_ref):
    yk, yv = _compress_rows(xk_ref[...], xv_ref[...], w1k_ref, w1v_ref, pk_ref, pv_ref)
    ok_ref[...] = _compress_finish(yk, _dot(pk_ref[...], w1k_ref[...]), w2k_ref).astype(BF16)
    ov_ref[...] = _compress_finish(yv, _dot(pv_ref[...], w1v_ref[...]), w2v_ref).astype(BF16)


def _compress_prompt(kc, vc, wts, *, batch, seq):
    nch = seq // CMP_STRIDE
    xk = kc.reshape(batch * nch, CHUNK_FEATS)
    xv = vc.reshape(batch * nch, CHUNK_FEATS)
    row = lambda b: (b, 0)
    c2 = lambda b: (0, 0)
    wspec = pl.BlockSpec((CHUNK_FEATS, 2 * LANE), c2)
    pspec = pl.BlockSpec((8, CHUNK_FEATS), c2)
    w2spec = pl.BlockSpec((LANE, LANE), c2)
    return pl.pallas_call(
        _compress_prompt_kernel, grid=(batch,),
        in_specs=[pl.BlockSpec((nch, CHUNK_FEATS), row), pl.BlockSpec((nch, CHUNK_FEATS), row),
                  wspec, wspec, pspec, pspec, w2spec, w2spec],
        out_specs=[pl.BlockSpec((nch, LANE), row)] * 2,
        out_shape=[jax.ShapeDtypeStruct((batch * nch, LANE), BF16)] * 2,
        compiler_params=_cparams("parallel"), name="compress_prompt",
    )(xk, xv, wts['cmp_w1k'], wts['cmp_w1v'], wts['cmp_posk'], wts['cmp_posv'], wts['cmp_w2k'], wts['cmp_w2v'])


def _overlap(n_rows, n_sel):
    c = _iota((n_rows, n_sel), 0) * CMP_STRIDE
    j = _iota((n_rows, n_sel), 1) * SLC_BLOCK
    return ((c < j + SLC_BLOCK) & (c + CMP_BLOCK > j)).astype(F32)


def _force_scores(score, cur, jj):
    forced = (jj == 0) | (jj == cur) | (jj == cur - 1)
    score = jnp.where(forced, FORCE, score)
    return jnp.where(jj <= cur, score, NEG)


def _cmp_prompt_kernel(q_ref, k_ref, v_ref, o_ref, sel_ref, *, tq, n_cmp, n_sel):
    q_start = pl.program_id(1) * tq
    ncp = k_ref.shape[0]
    rows = NSA_HPG * tq
    qpos = q_start + (_iota((rows, 1), 0) & (tq - 1))
    cidx = _iota((1, ncp), 1)
    dist = qpos - (cidx * CMP_STRIDE + CMP_BLOCK - 1)
    mask = (dist >= 0) & (cidx < n_cmp)
    distf = dist.astype(F32)
    hrow = _vdiv(_iota((rows, 1), 0), tq)
    qp1 = q_start + _iota((tq, 1), 0)
    cur = _vdiv(qp1, SLC_BLOCK)
    jj = _iota((tq, n_sel), 1)
    ov = _overlap(ncp, n_sel)
    for g in range(NSA_GROUPS):
        heads = range(g * NSA_HPG, (g + 1) * NSA_HPG)
        qg = jnp.concatenate([q_ref[:, h * NSA_DH:(h + 1) * NSA_DH] for h in heads], axis=0)
        slope = jnp.zeros((rows, 1), F32)
        for hh, h in enumerate(heads):
            slope = jnp.where(hrow == hh, SLOPES[h], slope)
        s = _dot_nt(qg, k_ref[:, g * NSA_DH:(g + 1) * NSA_DH]) - slope * distf
        s = jnp.where(mask, s, NEG)
        m = jnp.max(s, axis=-1, keepdims=True)
        p = jnp.where(mask, jnp.exp(s - m), 0.0)
        l = jnp.sum(p, axis=-1, keepdims=True)
        p = p / jnp.where(l > 0.0, l, 1.0)
        o = _dot(p.astype(BF16), v_ref[:, g * NSA_DH:(g + 1) * NSA_DH])
        imp = p[0:tq]
        for hh in range(1, NSA_HPG):
            imp = imp + p[hh * tq:(hh + 1) * tq]
            o_ref[:, (g * NSA_HPG + hh) * NSA_DH:(g * NSA_HPG + hh + 1) * NSA_DH] = o[hh * tq:(hh + 1) * tq].astype(BF16)
        o_ref[:, g * NSA_HPG * NSA_DH:(g * NSA_HPG + 1) * NSA_DH] = o[0:tq].astype(BF16)
        score = _force_scores(_dot_exact(imp, ov), cur, jj)
        rank = jnp.zeros((tq, n_sel), F32)
        for i in range(n_sel):
            ci = score[:, i:i + 1]
            beats = (ci > score) | ((ci == score) & (i < jj))
            rank = rank + beats.astype(F32)
        sel = (rank < float(min(SLC_TOP_N, n_sel))) & (jj <= cur)
        sel_ref[:, g * n_sel:(g + 1) * n_sel] = sel.astype(F32)


def _cmp_prompt(qn, kcc, vcc, *, batch, seq, tq):
    n = qn.shape[0]
    nch = seq // CMP_STRIDE
    n_cmp = nch - CMP_BLOCK // CMP_STRIDE + 1
    n_sel = -(-seq // SLC_BLOCK)
    qrow = lambda b, i: (b * (seq // tq) + i, 0)
    kv = lambda b, i: (b, 0)
    kern = functools.partial(_cmp_prompt_kernel, tq=tq, n_cmp=n_cmp, n_sel=n_sel)
    return pl.pallas_call(
        kern, grid=(batch, seq // tq),
        in_specs=[pl.BlockSpec((tq, qn.shape[1]), qrow), pl.BlockSpec((nch, LANE), kv), pl.BlockSpec((nch, LANE), kv)],
        out_specs=[pl.BlockSpec((tq, NSA_HEADS * NSA_DH), qrow), pl.BlockSpec((tq, NSA_GROUPS * n_sel), qrow)],
        out_shape=[jax.ShapeDtypeStruct((n, NSA_HEADS * NSA_DH), BF16),
                   jax.ShapeDtypeStruct((n, NSA_GROUPS * n_sel), F32)],
        compiler_params=_cparams("parallel", "arbitrary"), name="cmp_prompt",
    )(qn, kcc, vcc)


def _finish_kernel(*refs, period, latent, has_state):
    it = iter(refs)
    x_ref, omla_ref, ocmp_ref, oslc_ref, owin_ref, gn_ref, ga_ref, gb_ref = (next(it) for _ in range(8))
    prev1_ref, prev2_ref = (next(it), next(it)) if has_state else (None, None)
    gx_ref = next(it)
    wuv_ref = next(it) if latent else None
    (wpm_ref, wpn_ref, wo_ref, g2_ref, wg_ref, wu_ref, cw_ref, cb_ref, wd_ref, gf_ref,
     y_ref, gout_ref, carry_ref) = it
    tm = x_ref.shape[0]
    gexp = _dot_exact(gn_ref[...], gx_ref[...])
    w = NSA_HEADS * NSA_DH
    o_nsa = (gexp[:, 0:w] * ocmp_ref[...].astype(F32) + gexp[:, w:2 * w] * oslc_ref[...].astype(F32)
             + gexp[:, 2 * w:3 * w] * owin_ref[...].astype(F32))
    o_mla = omla_ref[...].astype(BF16)
    if latent:
        o_mla = _dot(o_mla, wuv_ref[...]).astype(BF16)
    merged = (ga_ref[...] * _dot(o_mla, wpm_ref[...])
              + gb_ref[...] * _dot(o_nsa.astype(BF16), wpn_ref[...]))
    x1 = x_ref[...] + _dot(merged.astype(BF16), wo_ref[...])
    h2 = _rms(x1, g2_ref[...]).astype(BF16)
    g = _dot(h2, wg_ref[...])
    u = _dot(h2, wu_ref[...])
    row = _iota((tm, 1), 0)
    i = pl.program_id(0)
    t = _vmod(i * tm + row, period)
    g1 = pltpu.roll(g, 1, 0)
    g2 = pltpu.roll(g, 2, 0)
    if period > tm:
        @pl.when(i == 0)
        def _():
            carry_ref[...] = jnp.zeros_like(carry_ref)
        c = carry_ref[...]
        g1 = jnp.where(row == 0, c[7:8], g1)
        g2 = jnp.where(row == 0, c[6:7], jnp.where(row == 1, c[7:8], g2))
        carry_ref[...] = g[tm - 8:tm]
    g1 = jnp.where(t >= 1, g1, prev1_ref[...] if has_state else 0.0)
    g2 = jnp.where(t >= 2, g2, prev2_ref[...] if has_state else 0.0)
    cw = cw_ref[...]
    conv = cb_ref[...] + cw[0:1] * g2 + cw[1:2] * g1 + cw[2:3] * g
    act = (jax.nn.silu(conv) * u).astype(BF16)
    x2 = x1 + _dot(act, wd_ref[...])
    y_ref[...] = _rms(x2, gf_ref[...])
    gout_ref[...] = g[tm - 8:tm] if gout_ref.shape[0] == 8 else g


def _finish(x, omla, ocmp, oslc, owin, gn, ga, gb, state_rows, wts, *, tm, period, full_g, latent):
    n, d = x.shape
    dff = wts['w_gate'].shape[1]
    row = lambda i: (i, 0)
    c2 = lambda i: (0, 0)
    acts = [x, omla, ocmp, oslc, owin, gn, ga, gb] + (list(state_rows) if state_rows is not None else [])
    consts = [wts['gate_expand']] + ([wts['w_uvbd']] if latent else []) + [
        wts['w_proj_mla'], wts['w_proj_nsa'], wts['w_out'], wts['norm2_g'],
        wts['w_gate'], wts['w_up'], wts['conv_w'], wts['conv_b'], wts['w_down'], wts['norm_f_g']]
    ins = acts + consts
    in_specs = [pl.BlockSpec((tm, a.shape[1]), row) for a in acts] + [pl.BlockSpec(a.shape, c2) for a in consts]
    g_rows = n if full_g else (n // tm) * 8
    g_blk = tm if full_g else 8
    kern = functools.partial(_finish_kernel, period=period, latent=latent, has_state=state_rows is not None)
    return pl.pallas_call(
        kern, grid=(n // tm,), in_specs=in_specs,
        out_specs=[pl.BlockSpec((tm, d), row), pl.BlockSpec((g_blk, dff), row)],
        out_shape=[jax.ShapeDtypeStruct((n, d), F32), jax.ShapeDtypeStruct((g_rows, dff), F32)],
        scratch_shapes=[pltpu.VMEM((8, dff), F32)],
        compiler_params=_cparams("arbitrary"), name="finish_full" if full_g else "finish_tiled",
    )(*ins)


def _swap_halves(w):
    hlf = w.shape[-1] // 2
    return jnp.concatenate([-w[..., hlf:], w[..., :hlf]], axis=-1)


def _prep_weights(p):
    d = p['w_in'].shape[0]
    q_rank, kv_rank = p['q_norm_g'].shape[-1], p['kv_norm_g'].shape[-1]
    sizes = [q_rank, kv_rank, MLA_D_ROPE, NSA_HEADS * NSA_DH] + [2 * NSA_GROUPS * NSA_DH] * 3 + [3 * NSA_HEADS, d, d]
    cuts = np.cumsum(sizes)[:-1].tolist()
    cq, ckv, kr, qn, kvc, kvs, kvw, gn, ga, gb = jnp.split(p['w_in'], cuts, axis=-1)
    assert _O_CKV == q_rank and _O_QN == q_rank + kv_rank
    lo, hi = MLA_D_NOPE, HEAD_PAD - MLA_D_NOPE - MLA_D_ROPE
    place = lambda w: jnp.pad(w, ((0, 0), (lo, hi)))
    gnp = jnp.pad(gn, ((0, 0), (0, LANE - gn.shape[1])))
    w_in = jnp.concatenate([cq, ckv, qn, kvc, kvs, kvw, ga, gb, place(kr), place(_swap_halves(kr)), gnp], axis=1)
    w = {'w_in': w_in.astype(BF16)}
    for k in ('norm1_g', 'q_norm_g', 'kv_norm_g', 'norm2_g', 'conv_b'):
        w[k] = p[k].reshape(1, -1)
    w['norm_f_g'] = p['norm_f_g'].reshape(1, -1)
    w['conv_w'] = jnp.pad(p['conv_w'], ((0, 8 - CONV_W), (0, 0)))
    uq = p['w_uq']
    hpad = ((0, 0), (0, 0), (0, HEAD_PAD - uq.shape[-1]))
    uq_a = jnp.pad(uq, hpad)
    uq_b = jnp.pad(jnp.concatenate([jnp.zeros_like(uq[..., :MLA_D_NOPE]), _swap_halves(uq[..., MLA_D_NOPE:])], -1), hpad)
    w['w_uq2'] = jnp.concatenate([uq_a.reshape(q_rank, -1), uq_b.reshape(q_rank, -1)], axis=1).astype(BF16)
    uk = p['w_uk']
    w['w_ukp'] = jnp.pad(uk, ((0, 0), (0, 0), (0, HEAD_PAD - MLA_D_NOPE))).reshape(kv_rank, -1).astype(BF16)
    w['w_ukT'] = jnp.pad(jnp.transpose(uk, (1, 2, 0)), ((0, 0), (0, HEAD_PAD - MLA_D_NOPE), (0, 0))).astype(BF16)
    w['w_uvf'] = p['w_uv'].reshape(kv_rank, -1).astype(BF16)
    eye_h = jnp.eye(MLA_HEADS, dtype=F32)
    w['w_uvbd'] = jnp.einsum('rhv,hk->hrkv', p['w_uv'], eye_h).reshape(MLA_HEADS * kv_rank, -1).astype(BF16)
    eye_g = jnp.eye(NSA_GROUPS, dtype=F32)
    for nm in ('k', 'v'):
        w1 = p['cmp_w1_' + nm].reshape(2, CMP_STRIDE, NSA_DH, -1)
        big = jnp.einsum('ajdh,gk->jgdakh', w1, eye_g)
        w['cmp_w1' + nm] = big.reshape(CHUNK_FEATS, -1).astype(BF16)
        pos = p['cmp_pos_' + nm].reshape(2, CMP_STRIDE, 1, NSA_DH)
        pos = jnp.broadcast_to(pos, (2, CMP_STRIDE, NSA_GROUPS, NSA_DH)).reshape(2, CHUNK_FEATS)
        w['cmp_pos' + nm] = jnp.pad(pos, ((0, 6), (0, 0))).astype(BF16)
        w2 = p['cmp_w2_' + nm]
        w['cmp_w2' + nm] = jnp.einsum('hd,gk->ghkd', w2, eye_g).reshape(NSA_GROUPS * w2.shape[0], -1).astype(BF16)
    ge = np.zeros((LANE, 3 * NSA_HEADS * NSA_DH), np.float32)
    for h in range(NSA_HEADS):
        for i in range(3):
            ge[h * 3 + i, i * NSA_HEADS * NSA_DH + h * NSA_DH:i * NSA_HEADS * NSA_DH + (h + 1) * NSA_DH] = 1.0
    w['gate_expand'] = jnp.asarray(ge)
    for k in ('w_proj_mla', 'w_proj_nsa', 'w_out', 'w_gate', 'w_up', 'w_down'):
        w[k] = p[k].astype(BF16)
    return w


def _rope_tables(pos):
    inv = ROPE_THETA ** (-jnp.arange(0, MLA_D_ROPE, 2, dtype=F32) / MLA_D_ROPE)
    ang = pos.astype(F32)[:, None] * inv[None, :]
    cos, sin = jnp.cos(ang), jnp.sin(ang)
    n = pos.shape[0]
    pad = jnp.zeros((n, HEAD_PAD - MLA_D_NOPE - MLA_D_ROPE), F32)
    cosq = jnp.concatenate([jnp.ones((n, MLA_D_NOPE), F32), cos, cos, pad], axis=1)
    sinq = jnp.concatenate([jnp.zeros((n, MLA_D_NOPE), F32), sin, sin, pad], axis=1)
    return cosq, sinq


_NSA_GROUPS_SPEC = tuple((tuple(range(g * NSA_HPG, (g + 1) * NSA_HPG)), g, SLOPES[g * NSA_HPG:(g + 1) * NSA_HPG])
                         for g in range(NSA_GROUPS))
_MLA_GROUPS_SPEC = tuple(((h,), h, (0.0,)) for h in range(MLA_HEADS))


def _prompt(x_prompt, w):
    b, t, d = x_prompt.shape
    n = b * t
    x = x_prompt.reshape(n, d)
    tm = 256
    cosq, sinq = _rope_tables(jnp.arange(t, dtype=jnp.int32))
    (q_mla, k_mla, v_mla, ckv, krp, qn, kc, vc, ks, vs, kw, vw, ks_b, vs_b, kw_b, vw_b, gn, ga, gb) = _inproj(
        x, w, cosq, sinq, sample=False, tm=tm, tab_blocks=t // tm)
    o_mla = _flash(q_mla, k_mla, v_mla, None, batch=b, seq=t, tq=256, tk=512, groups=_MLA_GROUPS_SPEC,
                   dq=HEAD_PAD, dk=HEAD_PAD, dv=MLA_D_V, window=0, name="mla_prompt")
    kcc, vcc = _compress_prompt(kc, vc, w, batch=b, seq=t)
    o_cmp, sel = _cmp_prompt(qn, kcc, vcc, batch=b, seq=t, tq=128)
    o_slc = _flash(qn, ks_b, vs_b, sel, batch=b, seq=t, tq=128, tk=512, groups=_NSA_GROUPS_SPEC,
                   dq=NSA_DH, dk=NSA_DH, dv=NSA_DH, window=0, name="slc_prompt")
    o_win = _flash(qn, kw_b, vw_b, None, batch=b, seq=t, tq=128, tk=512, groups=_NSA_GROUPS_SPEC,
                   dq=NSA_DH, dk=NSA_DH, dv=NSA_DH, window=WINDOW, name="win_prompt")
    dff = w['w_gate'].shape[1]
    y, gtail = _finish(x, o_mla, o_cmp, o_slc, o_win, gn, ga, gb, None, w, tm=tm, period=t, full_g=False,
                       latent=False)
    kv4 = lambda a: a.reshape(1, b, t, NSA_GROUPS, NSA_DH)
    n_keep = min(WINDOW, t)
    conv_state = gtail.reshape(b, t // tm, 8, dff)[:, -1, 8 - (CONV_W - 1):, :]
    states = (ckv.reshape(1, b, t, -1), krp[:, MLA_D_NOPE:MLA_D_NOPE + MLA_D_ROPE].reshape(1, b, t, MLA_D_ROPE),
              kv4(kc), kv4(vc), kv4(ks), kv4(vs), kv4(kw)[:, :, -n_keep:], kv4(vw)[:, :, -n_keep:],
              conv_state[None])
    return y.reshape(b, t, d), states


def _page_specs(block, n_pages, pp):
    zeros = (0,) * (len(block) - 1)
    return [pl.BlockSpec(block, (lambda b, s, pt, k=k: (pt[b * n_pages + s * pp + k],) + zeros)) for k in range(pp)]


def _softmax_update(sc, v, m_scr, l_scr, acc_scr):
    m_old = m_scr[...]
    m_new = jnp.maximum(m_old, jnp.max(sc, axis=-1, keepdims=True))
    p = jnp.exp(sc - m_new)
    alpha = jnp.exp(m_old - m_new)
    l_scr[...] = alpha * l_scr[...] + jnp.sum(p, axis=-1, keepdims=True)
    acc_scr[...] = alpha * acc_scr[...] + _dot(p.astype(BF16), v)
    m_scr[...] = m_new


def _mla_decode_kernel(pt_ref, q_ref, knew_ref, *rest, pp, td, kv_rank):
    ckv_pages, kr_pages = rest[:pp], rest[pp:2 * pp]
    o_ref, kscr, m_scr, l_scr, acc_scr = rest[2 * pp:]
    s = pl.program_id(1)
    q = q_ref[...]
    rows = q.shape[0]

    @pl.when(s == 0)
    def _():
        kscr[...] = jnp.zeros_like(kscr)
        m_scr[...] = jnp.full_like(m_scr, NEG)
        l_scr[...] = jnp.zeros_like(l_scr)
        acc_scr[...] = jnp.zeros_like(acc_scr)
        kn = knew_ref[...]
        trow = _vdiv(_iota((rows, 1), 0), MLA_HEADS)
        col = _iota((1, kn.shape[0]), 1)
        sc = jnp.where((col <= trow) & (col < td), _dot_nt(q, kn), NEG)
        _softmax_update(sc, kn[:, :kv_rank], m_scr, l_scr, acc_scr)

    for k in range(pp):
        kscr[k * PAGE_SIZE:(k + 1) * PAGE_SIZE, 0:kv_rank] = ckv_pages[k][...].astype(BF16)
        kscr[k * PAGE_SIZE:(k + 1) * PAGE_SIZE, kv_rank:kv_rank + MLA_D_ROPE] = kr_pages[k][...].astype(BF16)
    kt = kscr[...]
    _softmax_update(_dot_nt(q, kt), kt[:, :kv_rank], m_scr, l_scr, acc_scr)

    @pl.when(s == pl.num_programs(1) - 1)
    def _():
        o_ref[...] = acc_scr[...] / l_scr[...]


def _mla_decode(pt, qd, knew, ckv_pool, kr_pool, *, td, pp):
    bd, rows, qw = qd.shape
    n_pages = pt.shape[0] // bd
    kv_rank = ckv_pool.shape[-1]
    per_b = lambda b, s, pt: (b, 0, 0)
    in_specs = ([pl.BlockSpec((None, rows, qw), per_b), pl.BlockSpec((None,) + knew.shape[1:], per_b)]
                + _page_specs((None, PAGE_SIZE, kv_rank), n_pages, pp)
                + _page_specs((None, PAGE_SIZE, MLA_D_ROPE), n_pages, pp))
    kern = functools.partial(_mla_decode_kernel, pp=pp, td=td, kv_rank=kv_rank)
    return pl.pallas_call(
        kern,
        grid_spec=pltpu.PrefetchScalarGridSpec(
            num_scalar_prefetch=1, grid=(bd, n_pages // pp), in_specs=in_specs,
            out_specs=pl.BlockSpec((None, rows, kv_rank), per_b),
            scratch_shapes=[pltpu.VMEM((pp * PAGE_SIZE, qw), BF16), pltpu.VMEM((rows, 1), F32),
                            pltpu.VMEM((rows, 1), F32), pltpu.VMEM((rows, kv_rank), F32)]),
        out_shape=jax.ShapeDtypeStruct((bd, rows, kv_rank), F32),
        compiler_params=_cparams("parallel", "arbitrary"), name="mla_decode",
    )(pt, qd, knew, *([ckv_pool] * pp), *([kr_pool] * pp))


def _alibi_rows(rows, td):
    r = _iota((rows, 1), 0)
    return _vdiv(r, td), _vmod(r, td)


def _slope_rows(hrow, g):
    slope = jnp.zeros(hrow.shape, F32)
    for hh in range(NSA_HPG):
        slope = jnp.where(hrow == hh, SLOPES[g * NSA_HPG + hh], slope)
    return slope


def _cmp_decode_kernel(pt_ref, q_ref, *rest, pp, td, past, n_cmp, n_sel_pad):
    k_pages, v_pages = rest[:pp], rest[pp:2 * pp]
    (w1k_ref, w1v_ref, pk_ref, pv_ref, w2k_ref, w2v_ref, o_ref, score_ref, yk_scr, yv_scr) = rest[2 * pp:]
    s = pl.program_id(1)
    cpp = PAGE_SIZE // CMP_STRIDE
    xk = jnp.concatenate([r[...] for r in k_pages], axis=0).astype(BF16)
    xv = jnp.concatenate([r[...] for r in v_pages], axis=0).astype(BF16)
    r0 = pl.multiple_of(s * (pp * cpp), pp * cpp)
    yk_scr[pl.ds(r0, pp * cpp), :] = _dot(xk, w1k_ref[...])
    yv_scr[pl.ds(r0, pp * cpp), :] = _dot(xv, w1v_ref[...])

    @pl.when(s == pl.num_programs(1) - 1)
    def _():
        kcc = _compress_finish(yk_scr[...], _dot(pk_ref[...], w1k_ref[...]), w2k_ref).astype(BF16)
        vcc = _compress_finish(yv_scr[...], _dot(pv_ref[...], w1v_ref[...]), w2v_ref).astype(BF16)
        ncp = kcc.shape[0]
        rows = NSA_HPG * td
        hrow, trow = _alibi_rows(rows, td)
        cidx = _iota((1, ncp), 1)
        dist = (past + trow) - (cidx * CMP_STRIDE + CMP_BLOCK - 1)
        mask = (dist >= 0) & (cidx < n_cmp)
        distf = dist.astype(F32)
        tsum = (_vmod(_iota((8, rows), 1), td) == _iota((8, rows), 0)).astype(F32)
        ov = _overlap(ncp, n_sel_pad)
        t8 = _iota((8, 1), 0)
        cur = _vdiv(past + t8, SLC_BLOCK)
        jj = _iota((8, n_sel_pad), 1)
        for g in range(NSA_GROUPS):
            sc = _dot_nt(q_ref[g], kcc) - _slope_rows(hrow, g) * distf
            sc = jnp.where(mask, sc, NEG)
            m = jnp.max(sc, axis=-1, keepdims=True)
            p = jnp.where(mask, jnp.exp(sc - m), 0.0)
            l = jnp.sum(p, axis=-1, keepdims=True)
            p = p / jnp.where(l > 0.0, l, 1.0)
            o_ref[g] = _dot(p.astype(BF16), vcc)
            imp = _dot_exact(tsum, p)
            score_ref[g] = _force_scores(_dot_exact(imp, ov), cur, jj)


def _cmp_decode(pt, qd, k_pool, v_pool, wts, *, td, pp, past, n_sel_pad):
    bd = qd.shape[0]
    n_pages = pt.shape[0] // bd
    cpp = PAGE_SIZE // CMP_STRIDE
    nch = n_pages * cpp
    n_cmp = (past + td) // CMP_STRIDE - CMP_BLOCK // CMP_STRIDE + 1
    assert (past + td) // CMP_STRIDE == nch, "new rows must not complete a chunk"
    per_b = lambda b, s, pt: (b, 0, 0, 0)
    c2 = lambda b, s, pt: (0, 0)
    rows = qd.shape[2]
    in_specs = ([pl.BlockSpec((None,) + qd.shape[1:], per_b)]
                + _page_specs((None, cpp, CHUNK_FEATS), n_pages, pp) + _page_specs((None, cpp, CHUNK_FEATS), n_pages, pp)
                + [pl.BlockSpec((CHUNK_FEATS, 2 * LANE), c2) for _ in range(2)]
                + [pl.BlockSpec((8, CHUNK_FEATS), c2) for _ in range(2)]
                + [pl.BlockSpec((LANE, LANE), c2) for _ in range(2)])
    kern = functools.partial(_cmp_decode_kernel, pp=pp, td=td, past=past, n_cmp=n_cmp, n_sel_pad=n_sel_pad)
    return pl.pallas_call(
        kern,
        grid_spec=pltpu.PrefetchScalarGridSpec(
            num_scalar_prefetch=1, grid=(bd, n_pages // pp), in_specs=in_specs,
            out_specs=[pl.BlockSpec((None, NSA_GROUPS, rows, LANE), per_b),
                       pl.BlockSpec((None, NSA_GROUPS, 8, n_sel_pad), per_b)],
            scratch_shapes=[pltpu.VMEM((nch, 2 * LANE), F32) for _ in range(2)]),
        out_shape=[jax.ShapeDtypeStruct((bd, NSA_GROUPS, rows, LANE), F32),
                   jax.ShapeDtypeStruct((bd, NSA_GROUPS, 8, n_sel_pad), F32)],
        compiler_params=_cparams("parallel", "arbitrary"), name="cmp_decode",
    )(pt, qd, *([k_pool] * pp), *([v_pool] * pp), wts['cmp_w1k'], wts['cmp_w1v'], wts['cmp_posk'], wts['cmp_posv'],
      wts['cmp_w2k'], wts['cmp_w2v'])


def _rank_kernel(score_ref, cur_ref, sel_ref, *, n_sel):
    sc = score_ref[...]
    jj = _iota(sc.shape, 0)

    def body(i, rank):
        ri = score_ref[pl.ds(i, 1), :]
        beats = (ri > sc) | ((ri == sc) & (i < jj))
        return rank + beats.astype(F32)

    rank = lax.fori_loop(0, n_sel, body, jnp.zeros(sc.shape, F32))
    sel = (rank < float(min(SLC_TOP_N, n_sel))) & (jj <= cur_ref[...])
    sel_ref[...] = sel.astype(F32)


def _rank(score_t, cur, *, n_sel):
    full = lambda a: pl.BlockSpec(a.shape, lambda: (0,) * a.ndim)
    return pl.pallas_call(
        functools.partial(_rank_kernel, n_sel=n_sel), in_specs=[full(score_t), full(cur)],
        out_specs=full(score_t), out_shape=jax.ShapeDtypeStruct(score_t.shape, F32), name="rank_decode",
    )(score_t, cur)


def _slc_decode_kernel(pt_ref, q_ref, sel_ref, selnew_ref, knew_ref, vnew_ref, *rest, pp, td, past):
    k_pages, v_pages = rest[:pp], rest[pp:2 * pp]
    o_ref, kscr, vscr, m_scr, l_scr, acc_scr = rest[2 * pp:]
    s = pl.program_id(1)
    rows = q_ref.shape[1]
    hrow, trow = _alibi_rows(rows, td)
    tk = pp * PAGE_SIZE
    nblk = tk // SLC_BLOCK

    @pl.when(s == 0)
    def _():
        m_scr[...] = jnp.full_like(m_scr, NEG)
        l_scr[...] = jnp.zeros_like(l_scr)
        acc_scr[...] = jnp.zeros_like(acc_scr)
        kn = knew_ref[...]
        vn = vnew_ref[...]
        col = _iota((1, kn.shape[0]), 1)
        dist = trow - col
        for g in range(NSA_GROUPS):
            sc = _dot_nt(q_ref[g], kn) - _slope_rows(hrow, g) * dist.astype(F32)
            mask = (dist >= 0) & (col < td) & (selnew_ref[g] > 0.5)
            _softmax_update(jnp.where(mask, sc, NEG), vn, m_scr.at[g], l_scr.at[g], acc_scr.at[g])

    for k in range(pp):
        kscr[k * PAGE_SIZE:(k + 1) * PAGE_SIZE, :] = k_pages[k][...].astype(BF16)
        vscr[k * PAGE_SIZE:(k + 1) * PAGE_SIZE, :] = v_pages[k][...].astype(BF16)
    kt = kscr[...]
    vt = vscr[...]
    kpos = s * tk + _iota((1, tk), 1)
    distf = ((past + trow) - kpos).astype(F32)
    expand = (_vdiv(_iota((nblk, tk), 1), SLC_BLOCK) == _iota((nblk, tk), 0)).astype(BF16)
    for g in range(NSA_GROUPS):
        sc = _dot_nt(q_ref[g], kt) - _slope_rows(hrow, g) * distf
        selx = _dot(sel_ref[g], expand) > 0.5
        _softmax_update(jnp.where(selx, sc, NEG), vt, m_scr.at[g], l_scr.at[g], acc_scr.at[g])

    @pl.when(s == pl.num_programs(1) - 1)
    def _():
        o_ref[...] = acc_scr[...] / l_scr[...]


def _slc_decode(pt, qd, sel16, selnew, knew, vnew, k_pool, v_pool, *, td, pp, past):
    bd = qd.shape[0]
    rows = qd.shape[2]
    n_pages = pt.shape[0] // bd
    per_b = lambda b, s, pt: (b, 0, 0, 0)
    per_b3 = lambda b, s, pt: (b, 0, 0)
    in_specs = ([pl.BlockSpec((None,) + qd.shape[1:], per_b),
                 pl.BlockSpec((None, None) + sel16.shape[2:], lambda b, s, pt: (b, s, 0, 0, 0)),
                 pl.BlockSpec((None,) + selnew.shape[1:], per_b),
                 pl.BlockSpec((None,) + knew.shape[1:], per_b3), pl.BlockSpec((None,) + vnew.shape[1:], per_b3)]
                + _page_specs((None, PAGE_SIZE, LANE), n_pages, pp) + _page_specs((None, PAGE_SIZE, LANE), n_pages, pp))
    kern = functools.partial(_slc_decode_kernel, pp=pp, td=td, past=past)
    return pl.pallas_call(
        kern,
        grid_spec=pltpu.PrefetchScalarGridSpec(
            num_scalar_prefetch=1, grid=(bd, n_pages // pp), in_specs=in_specs,
            out_specs=pl.BlockSpec((None, NSA_GROUPS, rows, LANE), per_b),
            scratch_shapes=[pltpu.VMEM((pp * PAGE_SIZE, LANE), BF16)] * 2
            + [pltpu.VMEM((NSA_GROUPS, rows, 1), F32)] * 2 + [pltpu.VMEM((NSA_GROUPS, rows, LANE), F32)]),
        out_shape=jax.ShapeDtypeStruct((bd, NSA_GROUPS, rows, LANE), F32),
        compiler_params=_cparams("parallel", "arbitrary"), name="slc_decode",
    )(pt, qd, sel16, selnew, knew, vnew, *([k_pool] * pp), *([v_pool] * pp))


def _win_decode_kernel(q_ref, kst_ref, vst_ref, knew_ref, vnew_ref, o_ref, *, td, past):
    rows = q_ref.shape[1]
    hrow, trow = _alibi_rows(rows, td)
    nbuf = kst_ref.shape[0]
    kst = kst_ref[...].astype(BF16)
    vst = vst_ref[...].astype(BF16)
    kn = knew_ref[...]
    vn = vnew_ref[...]
    kpos = past - nbuf + _iota((1, nbuf), 1)
    d1 = (past + trow) - kpos
    m1 = (d1 >= 0) & (d1 < WINDOW) & (kpos >= 0)
    col = _iota((1, kn.shape[0]), 1)
    d2 = trow - col
    m2 = (d2 >= 0) & (d2 < WINDOW) & (col < td)
    for g in range(NSA_GROUPS):
        slope = _slope_rows(hrow, g)
        s1 = jnp.where(m1, _dot_nt(q_ref[g], kst) - slope * d1.astype(F32), NEG)
        s2 = jnp.where(m2, _dot_nt(q_ref[g], kn) - slope * d2.astype(F32), NEG)
        m = jnp.maximum(jnp.max(s1, axis=-1, keepdims=True), jnp.max(s2, axis=-1, keepdims=True))
        p1 = jnp.exp(s1 - m)
        p2 = jnp.exp(s2 - m)
        l = jnp.sum(p1, axis=-1, keepdims=True) + jnp.sum(p2, axis=-1, keepdims=True)
        o_ref[g] = (_dot(p1.astype(BF16), vst) + _dot(p2.astype(BF16), vn)) / l


def _win_decode(qd, kst, vst, knew, vnew, *, td, past):
    bd = qd.shape[0]
    rows = qd.shape[2]
    per_b = lambda b: (b, 0, 0, 0)
    per_b3 = lambda b: (b, 0, 0)
    blk3 = lambda a: pl.BlockSpec((None,) + a.shape[1:], per_b3)
    return pl.pallas_call(
        functools.partial(_win_decode_kernel, td=td, past=past), grid=(bd,),
        in_specs=[pl.BlockSpec((None,) + qd.shape[1:], per_b), blk3(kst), blk3(vst), blk3(knew), blk3(vnew)],
        out_specs=pl.BlockSpec((None, NSA_GROUPS, rows, LANE), per_b),
        out_shape=jax.ShapeDtypeStruct((bd, NSA_GROUPS, rows, LANE), F32),
        compiler_params=_cparams("parallel"), name="win_decode",
    )(qd, kst, vst, knew, vnew)


def _pad_rows(a, rows):
    return jnp.pad(a, ((0, 0), (0, rows - a.shape[1]), (0, 0)))


def _sample(x_sample, caches, page_table, w):
    (c_ckv, c_krope, c_cmp_k, c_cmp_v, c_slc_k, c_slc_v, s_win_k, s_win_v, s_conv) = caches
    bd, td, d = x_sample.shape
    n = bd * td
    n_pages = page_table.shape[1]
    past = n_pages * PAGE_SIZE
    n_pool = c_ckv.shape[0]
    kv_rank = c_ckv.shape[-1]
    x = x_sample.reshape(n, d)
    pos = past + jnp.arange(td, dtype=jnp.int32)
    cosq, sinq = _rope_tables(jnp.tile(pos, bd))
    (qrot, qabs, ckv, krp, qn, kc, vc, ks, vs, kw, vw, gn, ga, gb) = _inproj(
        x, w, cosq, sinq, sample=True, tm=n, tab_blocks=1)
    pt = page_table.reshape(-1)
    krope = krp[:, MLA_D_NOPE:MLA_D_NOPE + MLA_D_ROPE]
    qrope = qrot.reshape(bd, td, MLA_HEADS, HEAD_PAD)[..., MLA_D_NOPE:MLA_D_NOPE + MLA_D_ROPE]
    qpad = LANE - MLA_D_ROPE
    qd_mla = jnp.concatenate([qabs.reshape(bd, td, MLA_HEADS, kv_rank), qrope,
                              jnp.zeros((bd, td, MLA_HEADS, qpad), F32)], axis=-1)
    qd_mla = qd_mla.reshape(bd, td * MLA_HEADS, kv_rank + LANE).astype(BF16)
    knew = jnp.concatenate([ckv, krope, jnp.zeros((n, qpad), F32)], axis=-1).reshape(bd, td, -1)
    knew = _pad_rows(knew, 8).astype(BF16)
    o_lat = _mla_decode(pt, qd_mla, knew, c_ckv, c_krope, td=td, pp=min(16, n_pages))
    qg = qn.astype(F32).reshape(bd, td, NSA_GROUPS, NSA_HPG, NSA_DH).transpose(0, 2, 3, 1, 4)
    qg = qg.reshape(bd, NSA_GROUPS, NSA_HPG * td, NSA_DH)
    lane_g = (jnp.arange(LANE) // NSA_DH)[None, :] == jnp.arange(NSA_GROUPS)[:, None]
    qd = jnp.where(lane_g[None, :, None, :], jnp.tile(qg, (1, 1, 1, NSA_GROUPS)), 0.0).astype(BF16)
    n_sel = -(-(past + td) // SLC_BLOCK)
    n_sel_pad = -(-n_sel // LANE) * LANE
    o_cmp, score = _cmp_decode(pt, qd, c_cmp_k.reshape(n_pool, -1, CHUNK_FEATS), c_cmp_v.reshape(n_pool, -1, CHUNK_FEATS),
                               w, td=td, pp=min(32, n_pages), past=past, n_sel_pad=n_sel_pad)
    cur = jnp.broadcast_to(((past + jnp.arange(8)) // SLC_BLOCK).astype(jnp.int32), (bd, NSA_GROUPS, 8)).reshape(1, -1)
    sel_t = _rank(score.reshape(-1, n_sel_pad).T, cur, n_sel=n_sel)
    sel = sel_t.T.reshape(bd, NSA_GROUPS, 8, n_sel_pad)[:, :, :td]
    pp_slc = min(16, n_pages)
    bps = pp_slc * PAGE_SIZE // SLC_BLOCK
    n_past_blk = past // SLC_BLOCK
    sel_past = sel[..., :n_past_blk].reshape(bd, NSA_GROUPS, td, n_past_blk // bps, bps).transpose(0, 3, 1, 2, 4)
    sel16 = jnp.tile(sel_past, (1, 1, 1, NSA_HPG, 1)).astype(BF16)
    selnew = jnp.tile(jnp.broadcast_to(sel[..., n_past_blk:n_past_blk + 1], (bd, NSA_GROUPS, td, 8)), (1, 1, NSA_HPG, 1))
    new8 = lambda a: _pad_rows(a.reshape(bd, td, LANE), 8).astype(BF16)
    o_slc = _slc_decode(pt, qd, sel16, selnew, new8(ks), new8(vs), c_slc_k.reshape(n_pool, PAGE_SIZE, LANE),
                        c_slc_v.reshape(n_pool, PAGE_SIZE, LANE), td=td, pp=pp_slc, past=past)
    nbuf = s_win_k.shape[1]
    o_win = _win_decode(qd, s_win_k.reshape(bd, nbuf, LANE), s_win_v.reshape(bd, nbuf, LANE), new8(kw), new8(vw),
                        td=td, past=past)

    def heads_out(o):
        o = o.reshape(bd, NSA_GROUPS, NSA_HPG, td, NSA_GROUPS, NSA_DH)
        o = jnp.stack([o[:, g, :, :, g] for g in range(NSA_GROUPS)], axis=1)
        return o.transpose(0, 3, 1, 2, 4).reshape(n, NSA_HEADS * NSA_DH)

    dff = w['w_gate'].shape[1]
    prev1 = jnp.zeros((bd, td, dff), F32).at[:, 0].set(s_conv[:, 1])
    prev2 = jnp.zeros((bd, td, dff), F32).at[:, 0].set(s_conv[:, 0]).at[:, 1].set(s_conv[:, 1])
    y, g = _finish(x, o_lat.reshape(n, MLA_HEADS * kv_rank), heads_out(o_cmp), heads_out(o_slc), heads_out(o_win),
                   gn, ga, gb, (prev1.reshape(n, dff), prev2.reshape(n, dff)), w, tm=n, period=td, full_g=True,
                   latent=True)
    kv4 = lambda a: a.reshape(1, bd, td, NSA_GROUPS, NSA_DH)
    win = lambda st, new: jnp.concatenate([st, new.reshape(bd, td, NSA_GROUPS, NSA_DH)], axis=1)[None, :, -nbuf:]
    conv_state = jnp.concatenate([s_conv, g.reshape(bd, td, dff)], axis=1)[None, :, -(CONV_W - 1):]
    states = (ckv.reshape(1, bd, td, kv_rank), krope.reshape(1, bd, td, MLA_D_ROPE), kv4(kc), kv4(vc), kv4(ks), kv4(vs),
              win(s_win_k, kw), win(s_win_v, vw), conv_state)
    return y.reshape(bd, td, d), states


def kernel(x_prompt, x_sample, cache_mla_ckv, cache_mla_krope, cache_nsa_cmp_k, cache_nsa_cmp_v, cache_nsa_slc_k, cache_nsa_slc_v, state_win_k, state_win_v, state_ffn_conv, page_table, norm1_g, w_in, q_norm_g, kv_norm_g, w_uq, w_uk, w_uv, cmp_pos_k, cmp_w1_k, cmp_w2_k, cmp_pos_v, cmp_w1_v, cmp_w2_v, w_proj_mla, w_proj_nsa, w_out, norm2_g, w_gate, w_up, conv_w, conv_b, w_down, norm_f_g):
    assert norm1_g.shape[0] == 1, "single-layer trunk"
    p = dict(norm1_g=norm1_g[0], w_in=w_in[0], q_norm_g=q_norm_g[0], kv_norm_g=kv_norm_g[0], w_uq=w_uq[0],
             w_uk=w_uk[0], w_uv=w_uv[0], cmp_pos_k=cmp_pos_k[0], cmp_w1_k=cmp_w1_k[0], cmp_w2_k=cmp_w2_k[0],
             cmp_pos_v=cmp_pos_v[0], cmp_w1_v=cmp_w1_v[0], cmp_w2_v=cmp_w2_v[0], w_proj_mla=w_proj_mla[0],
             w_proj_nsa=w_proj_nsa[0], w_out=w_out[0], norm2_g=norm2_g[0], w_gate=w_gate[0], w_up=w_up[0],
             conv_w=conv_w[0], conv_b=conv_b[0], w_down=w_down[0], norm_f_g=norm_f_g)
    w = _prep_weights(p)
    y_p, ps = _prompt(x_prompt, w)
    caches = (cache_mla_ckv[0], cache_mla_krope[0], cache_nsa_cmp_k[0], cache_nsa_cmp_v[0], cache_nsa_slc_k[0],
              cache_nsa_slc_v[0], state_win_k[0], state_win_v[0], state_ffn_conv[0])
    y_s, ss = _sample(x_sample, caches, page_table, w)
    out = [y_p, y_s]
    for a, b in zip(ps, ss):
        out += [a, b]
    return tuple(out)
```

```python
import functools

import numpy as np
import jax
import jax.numpy as jnp
from jax import lax
from jax.experimental import pallas as pl
from jax.experimental.pallas import tpu as pltpu

MLA_HEADS = 8
MLA_D_NOPE = 64
MLA_D_ROPE = 32
MLA_D_V = 64
ROPE_THETA = 10000.0
MLA_SCALE = (MLA_D_NOPE + MLA_D_ROPE) ** -0.5
NSA_HEADS = 8
NSA_GROUPS = 2
NSA_HPG = NSA_HEADS // NSA_GROUPS
NSA_DH = 64
NSA_SCALE = NSA_DH ** -0.5
CMP_BLOCK = 32
CMP_STRIDE = 16
SLC_BLOCK = 64
SLC_TOP_N = 16
WINDOW = 512
CONV_W = 3
PAGE_SIZE = 128
EPS = 1e-6
NEG = -1e30
FORCE = 1e9

LANE = 128
HEAD_PAD = 128
CHUNK_FEATS = CMP_STRIDE * NSA_GROUPS * NSA_DH
SLOPES = tuple(float(2.0 ** (-8.0 * (h + 1) / NSA_HEADS)) for h in range(NSA_HEADS))
VMEM_LIMIT = 56 * 1024 * 1024

F32 = jnp.float32
BF16 = jnp.bfloat16
_NT = (((1,), (1,)), ((), ()))


def _cparams(*sem):
    return pltpu.CompilerParams(dimension_semantics=sem, vmem_limit_bytes=VMEM_LIMIT)


def _rms(x, g):
    return x * lax.rsqrt(jnp.mean(x * x, axis=-1, keepdims=True) + EPS) * g


def _dot(a, b):
    return jnp.dot(a, b, preferred_element_type=F32)


def _dot_nt(a, b):
    return lax.dot_general(a, b, _NT, preferred_element_type=F32)


def _dot_exact(a, b):
    return jnp.dot(a, b, preferred_element_type=F32, precision=lax.Precision.HIGHEST)


def _iota(shape, dim):
    return lax.broadcasted_iota(jnp.int32, shape, dim)


def _log2(n):
    assert n > 0 and n & (n - 1) == 0, n
    return n.bit_length() - 1


def _vdiv(x, n):
    return lax.shift_right_logical(x, jnp.full(x.shape, _log2(n), jnp.int32))


def _vmod(x, n):
    assert n & (n - 1) == 0, n
    return x & (n - 1)


_O_CQ, _O_CKV, _O_QN, _O_K6, _O_GA = 0, 384, 640, 1152, 1920


def _inproj_kernel(x_ref, g1_ref, w_ref, qg_ref, kvg_ref, wuq_ref, wk_ref, wv_ref, cos_ref, sin_ref,
                   *outs, sample, q_rank, kv_rank, d_model):
    o_gb = _O_GA + d_model
    o_kr = o_gb + d_model
    x = x_ref[...]
    hn = _rms(x, g1_ref[...])
    y = _dot(hn.astype(BF16), w_ref[...])
    cosq = cos_ref[...]
    sinq = sin_ref[...]
    nq = MLA_HEADS * HEAD_PAD
    cqn = _rms(y[:, _O_CQ:_O_CQ + q_rank], qg_ref[...])
    q2 = _dot(cqn.astype(BF16), wuq_ref[...])
    ckv = _rms(y[:, _O_CKV:_O_CKV + kv_rank], kvg_ref[...])
    kr = y[:, o_kr:o_kr + LANE] * cosq + y[:, o_kr + LANE:o_kr + 2 * LANE] * sinq
    ckv_b = ckv.astype(BF16)
    it = iter(outs)
    if sample:
        qrot_ref, qabs_ref = next(it), next(it)
        for h in range(MLA_HEADS):
            sl = slice(h * HEAD_PAD, (h + 1) * HEAD_PAD)
            qh = (q2[:, sl] * cosq + q2[:, nq + h * HEAD_PAD:nq + (h + 1) * HEAD_PAD] * sinq) * MLA_SCALE
            qrot_ref[:, sl] = qh
            qabs_ref[:, h * kv_rank:(h + 1) * kv_rank] = _dot(qh.astype(BF16), wk_ref[h])
    else:
        q_ref, k_ref, v_ref = next(it), next(it), next(it)
        knp = _dot(ckv_b, wk_ref[...])
        for h in range(MLA_HEADS):
            sl = slice(h * HEAD_PAD, (h + 1) * HEAD_PAD)
            qh = (q2[:, sl] * cosq + q2[:, nq + h * HEAD_PAD:nq + (h + 1) * HEAD_PAD] * sinq) * MLA_SCALE
            q_ref[:, sl] = qh.astype(BF16)
            k_ref[:, sl] = (knp[:, sl] + kr).astype(BF16)
        v_ref[...] = _dot(ckv_b, wv_ref[...]).astype(BF16)
    ckv_ref, kr_ref, qn_ref = next(it), next(it), next(it)
    ckv_ref[...] = ckv
    kr_ref[...] = kr
    qn_ref[...] = (y[:, _O_QN:_O_QN + NSA_HEADS * NSA_DH] * NSA_SCALE).astype(BF16)
    for j in range(6):
        yj = y[:, _O_K6 + j * LANE:_O_K6 + (j + 1) * LANE]
        if sample:
            next(it)[...] = yj
        else:
            next(it)[...] = yj.T
    if not sample:
        for j in range(6):
            next(it)[...] = y[:, _O_K6 + j * LANE:_O_K6 + (j + 1) * LANE].astype(BF16)
    gn_ref, ga_ref, gb_ref = next(it), next(it), next(it)
    gn_ref[...] = jax.nn.sigmoid(y[:, o_kr + 2 * LANE:o_kr + 3 * LANE])
    ga_ref[...] = jax.nn.sigmoid(y[:, _O_GA:_O_GA + d_model])
    gb_ref[...] = jax.nn.sigmoid(y[:, o_gb:o_gb + d_model])


def _inproj(x, wts, cosq, sinq, *, sample, tm, tab_blocks):
    n, d = x.shape
    q_rank, kv_rank = wts['q_norm_g'].shape[1], wts['kv_norm_g'].shape[1]
    nq = MLA_HEADS * HEAD_PAD
    wk = wts['w_ukT'] if sample else wts['w_ukp']
    row = lambda i: (i, 0)
    const2 = lambda i: (0, 0)
    tab_map = (lambda i: (i % tab_blocks, 0))
    in_specs = [
        pl.BlockSpec((tm, d), row),
        pl.BlockSpec((1, d), const2),
        pl.BlockSpec(wts['w_in'].shape, const2),
        pl.BlockSpec((1, q_rank), const2),
        pl.BlockSpec((1, kv_rank), const2),
        pl.BlockSpec(wts['w_uq2'].shape, const2),
        pl.BlockSpec(wk.shape, (lambda i: (0, 0, 0)) if sample else const2),
        pl.BlockSpec(wts['w_uvf'].shape, const2),
        pl.BlockSpec((tm, LANE), tab_map),
        pl.BlockSpec((tm, LANE), tab_map),
    ]
    shapes = []
    if sample:
        shapes += [(nq, F32), (MLA_HEADS * kv_rank, F32)]
    else:
        shapes += [(nq, BF16), (nq, BF16), (MLA_HEADS * MLA_D_V, BF16)]
    shapes += [(kv_rank, F32), (LANE, F32), (NSA_HEADS * NSA_DH, BF16)]
    n_lead = len(shapes)
    shapes += [(LANE, F32)] * 6
    if not sample:
        shapes += [(LANE, BF16)] * 6
    shapes += [(LANE, F32), (d, F32), (d, F32)]
    out_shape = [jax.ShapeDtypeStruct((n, w), dt) for w, dt in shapes]
    out_specs = [pl.BlockSpec((tm, w), row) for w, _ in shapes]
    if not sample:
        tpb = tab_blocks
        for j in range(n_lead, n_lead + 6):
            out_shape[j] = jax.ShapeDtypeStruct((n // (tpb * tm), LANE, tpb * tm), F32)
            out_specs[j] = pl.BlockSpec((None, LANE, tm), lambda i: (i // tpb, 0, i % tpb))
    kern = functools.partial(_inproj_kernel, sample=sample, q_rank=q_rank, kv_rank=kv_rank, d_model=d)
    return pl.pallas_call(
        kern, grid=(n // tm,), in_specs=in_specs, out_specs=out_specs, out_shape=out_shape,
        compiler_params=_cparams("parallel"), name="inproj_sample" if sample else "inproj_prompt",
    )(x, wts['norm1_g'], wts['w_in'], wts['q_norm_g'], wts['kv_norm_g'], wts['w_uq2'], wk, wts['w_uvf'],
      cosq, sinq)


def _flash_kernel(*refs, tq, tk, groups, dq, dk, dv, window, has_sel, n_sel):
    if has_sel:
        q_ref, k_ref, v_ref, sel_ref, o_ref = refs
    else:
        q_ref, k_ref, v_ref, o_ref = refs
    q_start = pl.program_id(1) * tq
    n_hi = (q_start + tq - 1) // tk + 1
    n_lo = jnp.maximum(q_start - (window - 1), 0) // tk if window else 0
    for gi, (heads, kcol, slopes) in enumerate(groups):
        nh = len(heads)
        rows = nh * tq
        if nh == 1:
            qg = q_ref[:, heads[0] * dq:(heads[0] + 1) * dq]
        else:
            qg = jnp.concatenate([q_ref[:, h * dq:(h + 1) * dq] for h in heads], axis=0)
        qpos = q_start + (_iota((rows, 1), 0) & (tq - 1))
        use_alibi = any(s != 0.0 for s in slopes)
        if use_alibi:
            hrow = _vdiv(_iota((rows, 1), 0), tq)
            slope = jnp.zeros((rows, 1), F32)
            for hh, s in enumerate(slopes):
                slope = jnp.where(hrow == hh, s, slope)
        if has_sel:
            selg = sel_ref[:, gi * n_sel:(gi + 1) * n_sel].astype(BF16)

        def body(j, carry):
            m, l, acc = carry
            k0 = pl.multiple_of(j * tk, tk)
            kt = k_ref[pl.ds(k0, tk), kcol * dk:(kcol + 1) * dk]
            vt = v_ref[pl.ds(k0, tk), kcol * dv:(kcol + 1) * dv]
            s = _dot_nt(qg, kt)
            dist = qpos - (k0 + _iota((1, tk), 1))
            mask = dist >= 0
            if window:
                mask = mask & (dist < window)
            if use_alibi:
                s = s - slope * dist.astype(F32)
            if has_sel:
                blk = _vdiv(k0 + _iota((n_sel, tk), 1), SLC_BLOCK)
                expand = (blk == _iota((n_sel, tk), 0)).astype(BF16)
                selx = _dot(selg, expand) > 0.5
                if nh > 1:
                    selx = jnp.concatenate([selx] * nh, axis=0)
                mask = mask & selx
            s = jnp.where(mask, s, NEG)
            m_new = jnp.maximum(m, jnp.max(s, axis=-1, keepdims=True))
            p = jnp.exp(s - m_new)
            alpha = jnp.exp(m - m_new)
            l = alpha * l + jnp.sum(p, axis=-1, keepdims=True)
            acc = alpha * acc + _dot(p.astype(BF16), vt)
            return m_new, l, acc

        init = (jnp.full((rows, 1), NEG, F32), jnp.zeros((rows, 1), F32), jnp.zeros((rows, dv), F32))
        m, l, acc = lax.fori_loop(n_lo, n_hi, body, init)
        o = acc / l
        for hh, h in enumerate(heads):
            o_ref[:, h * dv:(h + 1) * dv] = o[hh * tq:(hh + 1) * tq].astype(o_ref.dtype)


def _flash(q, k, v, sel, *, batch, seq, tq, tk, groups, dq, dk, dv, window, name):
    n = q.shape[0]
    has_sel = sel is not None
    n_heads = sum(len(g[0]) for g in groups)
    qrow = lambda b, i: (b * (seq // tq) + i, 0)
    kv = lambda b, i: (b, 0)
    in_specs = [pl.BlockSpec((tq, q.shape[1]), qrow), pl.BlockSpec((seq, k.shape[1]), kv),
                pl.BlockSpec((seq, v.shape[1]), kv)]
    args = [q, k, v]
    n_sel = 0
    if has_sel:
        n_sel = sel.shape[1] // len(groups)
        in_specs.append(pl.BlockSpec((tq, sel.shape[1]), qrow))
        args.append(sel)
    kern = functools.partial(_flash_kernel, tq=tq, tk=tk, groups=groups, dq=dq, dk=dk, dv=dv, window=window,
                             has_sel=has_sel, n_sel=n_sel)
    return pl.pallas_call(
        kern, grid=(batch, seq // tq), in_specs=in_specs,
        out_specs=pl.BlockSpec((tq, n_heads * dv), qrow),
        out_shape=jax.ShapeDtypeStruct((n, n_heads * dv), BF16),
        compiler_params=_cparams("parallel", "arbitrary"), name=name,
    )(*args)


def _compress_rows(xk, xv, w1k_ref, w1v_ref, posk_ref, posv_ref):
    yk = _dot(xk.astype(BF16), w1k_ref[...])
    yv = _dot(xv.astype(BF16), w1v_ref[...])
    return yk, yv


def _compress_finish(y, posy, w2_ref):
    rows = y.shape[0]
    a = y[:, :LANE]
    b = pltpu.roll(y[:, LANE:], rows - 1, 0)
    pos = posy[0:1, :LANE] + posy[1:2, LANE:]
    hid = jax.nn.gelu(a + b + pos)
    return _dot(hid.astype(BF16), w2_ref[...])


def _compress_prompt_kernel(xk_ref, xv_ref, w1k_ref, w1v_ref, pk_ref, pv_ref, w2k_ref, w2v_ref, ok_ref, ov_ref):
    yk, yv = _compress_rows(xk_ref[...], xv_ref[...], w1k_ref, w1v_ref, pk_ref, pv_ref)
    ok_ref[...] = _compress_finish(yk, _dot(pk_ref[...], w1k_ref[...]), w2k_ref).astype(BF16)
    ov_ref[...] = _compress_finish(yv, _dot(pv_ref[...], w1v_ref[...]), w2v_ref).astype(BF16)


def _compress_prompt(kc, vc, wts, *, batch, seq):
    nch = seq // CMP_STRIDE
    xk = kc.reshape(batch * nch, CHUNK_FEATS)
    xv = vc.reshape(batch * nch, CHUNK_FEATS)
    row = lambda b: (b, 0)
    c2 = lambda b: (0, 0)
    wspec = pl.BlockSpec((CHUNK_FEATS, 2 * LANE), c2)
    pspec = pl.BlockSpec((8, CHUNK_FEATS), c2)
    w2spec = pl.BlockSpec((LANE, LANE), c2)
    return pl.pallas_call(
        _compress_prompt_kernel, grid=(batch,),
        in_specs=[pl.BlockSpec((nch, CHUNK_FEATS), row), pl.BlockSpec((nch, CHUNK_FEATS), row),
                  wspec, wspec, pspec, pspec, w2spec, w2spec],
        out_specs=[pl.BlockSpec((nch, LANE), row)] * 2,
        out_shape=[jax.ShapeDtypeStruct((batch * nch, LANE), BF16)] * 2,
        compiler_params=_cparams("parallel"), name="compress_prompt",
    )(xk, xv, wts['cmp_w1k'], wts['cmp_w1v'], wts['cmp_posk'], wts['cmp_posv'], wts['cmp_w2k'], wts['cmp_w2v'])


def _overlap(n_rows, n_sel):
    c = _iota((n_rows, n_sel), 0) * CMP_STRIDE
    j = _iota((n_rows, n_sel), 1) * SLC_BLOCK
    return ((c < j + SLC_BLOCK) & (c + CMP_BLOCK > j)).astype(F32)


def _force_scores(score, cur, jj):
    forced = (jj == 0) | (jj == cur) | (jj == cur - 1)
    score = jnp.where(forced, FORCE, score)
    return jnp.where(jj <= cur, score, NEG)


def _cmp_prompt_kernel(q_ref, k_ref, v_ref, o_ref, sel_ref, *, tq, n_cmp, n_sel):
    q_start = pl.program_id(1) * tq
    ncp = k_ref.shape[0]
    rows = NSA_HPG * tq
    qpos = q_start + (_iota((rows, 1), 0) & (tq - 1))
    cidx = _iota((1, ncp), 1)
    dist = qpos - (cidx * CMP_STRIDE + CMP_BLOCK - 1)
    mask = (dist >= 0) & (cidx < n_cmp)
    distf = dist.astype(F32)
    hrow = _vdiv(_iota((rows, 1), 0), tq)
    qp1 = q_start + _iota((tq, 1), 0)
    cur = _vdiv(qp1, SLC_BLOCK)
    jj = _iota((tq, n_sel), 1)
    ov = _overlap(ncp, n_sel)
    for g in range(NSA_GROUPS):
        heads = range(g * NSA_HPG, (g + 1) * NSA_HPG)
        qg = jnp.concatenate([q_ref[:, h * NSA_DH:(h + 1) * NSA_DH] for h in heads], axis=0)
        slope = jnp.zeros((rows, 1), F32)
        for hh, h in enumerate(heads):
            slope = jnp.where(hrow == hh, SLOPES[h], slope)
        s = _dot_nt(qg, k_ref[:, g * NSA_DH:(g + 1) * NSA_DH]) - slope * distf
        s = jnp.where(mask, s, NEG)
        m = jnp.max(s, axis=-1, keepdims=True)
        p = jnp.where(mask, jnp.exp(s - m), 0.0)
        l = jnp.sum(p, axis=-1, keepdims=True)
        p = p / jnp.where(l > 0.0, l, 1.0)
        o = _dot(p.astype(BF16), v_ref[:, g * NSA_DH:(g + 1) * NSA_DH])
        imp = p[0:tq]
        for hh in range(1, NSA_HPG):
            imp = imp + p[hh * tq:(hh + 1) * tq]
            o_ref[:, (g * NSA_HPG + hh) * NSA_DH:(g * NSA_HPG + hh + 1) * NSA_DH] = o[hh * tq:(hh + 1) * tq].astype(BF16)
        o_ref[:, g * NSA_HPG * NSA_DH:(g * NSA_HPG + 1) * NSA_DH] = o[0:tq].astype(BF16)
        score = _force_scores(_dot_exact(imp, ov), cur, jj)
        rank = jnp.zeros((tq, n_sel), F32)
        for i in range(n_sel):
            ci = score[:, i:i + 1]
            beats = (ci > score) | ((ci == score) & (i < jj))
            rank = rank + beats.astype(F32)
        sel = (rank < float(min(SLC_TOP_N, n_sel))) & (jj <= cur)
        sel_ref[:, g * n_sel:(g + 1) * n_sel] = sel.astype(F32)


def _cmp_prompt(qn, kcc, vcc, *, batch, seq, tq):
    n = qn.shape[0]
    nch = seq // CMP_STRIDE
    n_cmp = nch - CMP_BLOCK // CMP_STRIDE + 1
    n_sel = -(-seq // SLC_BLOCK)
    qrow = lambda b, i: (b * (seq // tq) + i, 0)
    kv = lambda b, i: (b, 0)
    kern = functools.partial(_cmp_prompt_kernel, tq=tq, n_cmp=n_cmp, n_sel=n_sel)
    return pl.pallas_call(
        kern, grid=(batch, seq // tq),
        in_specs=[pl.BlockSpec((tq, qn.shape[1]), qrow), pl.BlockSpec((nch, LANE), kv), pl.BlockSpec((nch, LANE), kv)],
        out_specs=[pl.BlockSpec((tq, NSA_HEADS * NSA_DH), qrow), pl.BlockSpec((tq, NSA_GROUPS * n_sel), qrow)],
        out_shape=[jax.ShapeDtypeStruct((n, NSA_HEADS * NSA_DH), BF16),
                   jax.ShapeDtypeStruct((n, NSA_GROUPS * n_sel), F32)],
        compiler_params=_cparams("parallel", "arbitrary"), name="cmp_prompt",
    )(qn, kcc, vcc)


def _finish_kernel(*refs, period, latent, has_state):
    it = iter(refs)
    x_ref, omla_ref, ocmp_ref, oslc_ref, owin_ref, gn_ref, ga_ref, gb_ref = (next(it) for _ in range(8))
    prev1_ref, prev2_ref = (next(it), next(it)) if has_state else (None, None)
    gx_ref = next(it)
    wuv_ref = next(it) if latent else None
    (wpm_ref, wpn_ref, wo_ref, g2_ref, wg_ref, wu_ref, cw_ref, cb_ref, wd_ref, gf_ref,
     y_ref, gout_ref, carry_ref) = it
    tm = x_ref.shape[0]
    gexp = _dot_exact(gn_ref[...], gx_ref[...])
    w = NSA_HEADS * NSA_DH
    o_nsa = (gexp[:, 0:w] * ocmp_ref[...].astype(F32) + gexp[:, w:2 * w] * oslc_ref[...].astype(F32)
             + gexp[:, 2 * w:3 * w] * owin_ref[...].astype(F32))
    o_mla = omla_ref[...].astype(BF16)
    if latent:
        o_mla = _dot(o_mla, wuv_ref[...]).astype(BF16)
    merged = (ga_ref[...] * _dot(o_mla, wpm_ref[...])
              + gb_ref[...] * _dot(o_nsa.astype(BF16), wpn_ref[...]))
    x1 = x_ref[...] + _dot(merged.astype(BF16), wo_ref[...])
    h2 = _rms(x1, g2_ref[...]).astype(BF16)
    g = _dot(h2, wg_ref[...])
    u = _dot(h2, wu_ref[...])
    row = _iota((tm, 1), 0)
    i = pl.program_id(0)
    t = _vmod(i * tm + row, period)
    g1 = pltpu.roll(g, 1, 0)
    g2 = pltpu.roll(g, 2, 0)
    if period > tm:
        @pl.when(i == 0)
        def _():
            carry_ref[...] = jnp.zeros_like(carry_ref)
        c = carry_ref[...]
        g1 = jnp.where(row == 0, c[7:8], g1)
        g2 = jnp.where(row == 0, c[6:7], jnp.where(row == 1, c[7:8], g2))
        carry_ref[...] = g[tm - 8:tm]
    g1 = jnp.where(t >= 1, g1, prev1_ref[...] if has_state else 0.0)
    g2 = jnp.where(t >= 2, g2, prev2_ref[...] if has_state else 0.0)
    cw = cw_ref[...]
    conv = cb_ref[...] + cw[0:1] * g2 + cw[1:2] * g1 + cw[2:3] * g
    act = (jax.nn.silu(conv) * u).astype(BF16)
    x2 = x1 + _dot(act, wd_ref[...])
    y_ref[...] = _rms(x2, gf_ref[...])
    gout_ref[...] = g[tm - 8:tm] if gout_ref.shape[0] == 8 else g


def _finish(x, omla, ocmp, oslc, owin, gn, ga, gb, state_rows, wts, *, tm, period, full_g, latent):
    n, d = x.shape
    dff = wts['w_gate'].shape[1]
    row = lambda i: (i, 0)
    c2 = lambda i: (0, 0)
    acts = [x, omla, ocmp, oslc, owin, gn, ga, gb] + (list(state_rows) if state_rows is not None else [])
    consts = [wts['gate_expand']] + ([wts['w_uvbd']] if latent else []) + [
        wts['w_proj_mla'], wts['w_proj_nsa'], wts['w_out'], wts['norm2_g'],
        wts['w_gate'], wts['w_up'], wts['conv_w'], wts['conv_b'], wts['w_down'], wts['norm_f_g']]
    ins = acts + consts
    in_specs = [pl.BlockSpec((tm, a.shape[1]), row) for a in acts] + [pl.BlockSpec(a.shape, c2) for a in consts]
    g_rows = n if full_g else (n // tm) * 8
    g_blk = tm if full_g else 8
    kern = functools.partial(_finish_kernel, period=period, latent=latent, has_state=state_rows is not None)
    return pl.pallas_call(
        kern, grid=(n // tm,), in_specs=in_specs,
        out_specs=[pl.BlockSpec((tm, d), row), pl.BlockSpec((g_blk, dff), row)],
        out_shape=[jax.ShapeDtypeStruct((n, d), F32), jax.ShapeDtypeStruct((g_rows, dff), F32)],
        scratch_shapes=[pltpu.VMEM((8, dff), F32)],
        compiler_params=_cparams("arbitrary"), name="finish_full" if full_g else "finish_tiled",
    )(*ins)


def _swap_halves(w):
    hlf = w.shape[-1] // 2
    return jnp.concatenate([-w[..., hlf:], w[..., :hlf]], axis=-1)


def _prep_weights(p):
    d = p['w_in'].shape[0]
    q_rank, kv_rank = p['q_norm_g'].shape[-1], p['kv_norm_g'].shape[-1]
    sizes = [q_rank, kv_rank, MLA_D_ROPE, NSA_HEADS * NSA_DH] + [2 * NSA_GROUPS * NSA_DH] * 3 + [3 * NSA_HEADS, d, d]
    cuts = np.cumsum(sizes)[:-1].tolist()
    cq, ckv, kr, qn, kvc, kvs, kvw, gn, ga, gb = jnp.split(p['w_in'], cuts, axis=-1)
    assert _O_CKV == q_rank and _O_QN == q_rank + kv_rank
    lo, hi = MLA_D_NOPE, HEAD_PAD - MLA_D_NOPE - MLA_D_ROPE
    place = lambda w: jnp.pad(w, ((0, 0), (lo, hi)))
    gnp = jnp.pad(gn, ((0, 0), (0, LANE - gn.shape[1])))
    w_in = jnp.concatenate([cq, ckv, qn, kvc, kvs, kvw, ga, gb, place(kr), place(_swap_halves(kr)), gnp], axis=1)
    w = {'w_in': w_in.astype(BF16)}
    for k in ('norm1_g', 'q_norm_g', 'kv_norm_g', 'norm2_g', 'conv_b'):
        w[k] = p[k].reshape(1, -1)
    w['norm_f_g'] = p['norm_f_g'].reshape(1, -1)
    w['conv_w'] = jnp.pad(p['conv_w'], ((0, 8 - CONV_W), (0, 0)))
    uq = p['w_uq']
    hpad = ((0, 0), (0, 0), (0, HEAD_PAD - uq.shape[-1]))
    uq_a = jnp.pad(uq, hpad)
    uq_b = jnp.pad(jnp.concatenate([jnp.zeros_like(uq[..., :MLA_D_NOPE]), _swap_halves(uq[..., MLA_D_NOPE:])], -1), hpad)
    w['w_uq2'] = jnp.concatenate([uq_a.reshape(q_rank, -1), uq_b.reshape(q_rank, -1)], axis=1).astype(BF16)
    uk = p['w_uk']
    w['w_ukp'] = jnp.pad(uk, ((0, 0), (0, 0), (0, HEAD_PAD - MLA_D_NOPE))).reshape(kv_rank, -1).astype(BF16)
    w['w_ukT'] = jnp.pad(jnp.transpose(uk, (1, 2, 0)), ((0, 0), (0, HEAD_PAD - MLA_D_NOPE), (0, 0))).astype(BF16)
    w['w_uvf'] = p['w_uv'].reshape(kv_rank, -1).astype(BF16)
    eye_h = jnp.eye(MLA_HEADS, dtype=F32)
    w['w_uvbd'] = jnp.einsum('rhv,hk->hrkv', p['w_uv'], eye_h).reshape(MLA_HEADS * kv_rank, -1).astype(BF16)
    eye_g = jnp.eye(NSA_GROUPS, dtype=F32)
    for nm in ('k', 'v'):
        w1 = p['cmp_w1_' + nm].reshape(2, CMP_STRIDE, NSA_DH, -1)
        big = jnp.einsum('ajdh,gk->jgdakh', w1, eye_g)
        w['cmp_w1' + nm] = big.reshape(CHUNK_FEATS, -1).astype(BF16)
        pos = p['cmp_pos_' + nm].reshape(2, CMP_STRIDE, 1, NSA_DH)
        pos = jnp.broadcast_to(pos, (2, CMP_STRIDE, NSA_GROUPS, NSA_DH)).reshape(2, CHUNK_FEATS)
        w['cmp_pos' + nm] = jnp.pad(pos, ((0, 6), (0, 0))).astype(BF16)
        w2 = p['cmp_w2_' + nm]
        w['cmp_w2' + nm] = jnp.einsum('hd,gk->ghkd', w2, eye_g).reshape(NSA_GROUPS * w2.shape[0], -1).astype(BF16)
    ge = np.zeros((LANE, 3 * NSA_HEADS * NSA_DH), np.float32)
    for h in range(NSA_HEADS):
        for i in range(3):
            ge[h * 3 + i, i * NSA_HEADS * NSA_DH + h * NSA_DH:i * NSA_HEADS * NSA_DH + (h + 1) * NSA_DH] = 1.0
    w['gate_expand'] = jnp.asarray(ge)
    for k in ('w_proj_mla', 'w_proj_nsa', 'w_out', 'w_gate', 'w_up', 'w_down'):
        w[k] = p[k].astype(BF16)
    return w


def _rope_tables(pos):
    inv = ROPE_THETA ** (-jnp.arange(0, MLA_D_ROPE, 2, dtype=F32) / MLA_D_ROPE)
    ang = pos.astype(F32)[:, None] * inv[None, :]
    cos, sin = jnp.cos(ang), jnp.sin(ang)
    n = pos.shape[0]
    pad = jnp.zeros((n, HEAD_PAD - MLA_D_NOPE - MLA_D_ROPE), F32)
    cosq = jnp.concatenate([jnp.ones((n, MLA_D_NOPE), F32), cos, cos, pad], axis=1)
    sinq = jnp.concatenate([jnp.zeros((n, MLA_D_NOPE), F32), sin, sin, pad], axis=1)
    return cosq, sinq


_NSA_GROUPS_SPEC = tuple((tuple(range(g * NSA_HPG, (g + 1) * NSA_HPG)), g, SLOPES[g * NSA_HPG:(g + 1) * NSA_HPG])
                         for g in range(NSA_GROUPS))
_MLA_GROUPS_SPEC = tuple(((h,), h, (0.0,)) for h in range(MLA_HEADS))


def _prompt(x_prompt, w):
    b, t, d = x_prompt.shape
    n = b * t
    x = x_prompt.reshape(n, d)
    tm = 256
    cosq, sinq = _rope_tables(jnp.arange(t, dtype=jnp.int32))
    (q_mla, k_mla, v_mla, ckv, krp, qn, kc, vc, ks, vs, kw, vw, kc_b, vc_b, ks_b, vs_b, kw_b, vw_b, gn, ga, gb) = _inproj(
        x, w, cosq, sinq, sample=False, tm=tm, tab_blocks=t // tm)
    o_mla = _flash(q_mla, k_mla, v_mla, None, batch=b, seq=t, tq=256, tk=512, groups=_MLA_GROUPS_SPEC,
                   dq=HEAD_PAD, dk=HEAD_PAD, dv=MLA_D_V, window=0, name="mla_prompt")
    kcc, vcc = _compress_prompt(kc_b, vc_b, w, batch=b, seq=t)
    o_cmp, sel = _cmp_prompt(qn, kcc, vcc, batch=b, seq=t, tq=128)
    o_slc = _flash(qn, ks_b, vs_b, sel, batch=b, seq=t, tq=128, tk=512, groups=_NSA_GROUPS_SPEC,
                   dq=NSA_DH, dk=NSA_DH, dv=NSA_DH, window=0, name="slc_prompt")
    o_win = _flash(qn, kw_b, vw_b, None, batch=b, seq=t, tq=128, tk=512, groups=_NSA_GROUPS_SPEC,
                   dq=NSA_DH, dk=NSA_DH, dv=NSA_DH, window=WINDOW, name="win_prompt")
    dff = w['w_gate'].shape[1]
    y, gtail = _finish(x, o_mla, o_cmp, o_slc, o_win, gn, ga, gb, None, w, tm=tm, period=t, full_g=False,
                       latent=False)
    kv4 = lambda a: a.reshape(1, b, NSA_GROUPS, NSA_DH, a.shape[-1]).transpose(0, 1, 4, 2, 3)
    n_keep = min(WINDOW, t)
    kw, vw = kw[:, :, t - n_keep:], vw[:, :, t - n_keep:]
    conv_state = gtail.reshape(b, t // tm, 8, dff)[:, -1, 8 - (CONV_W - 1):, :]
    states = (ckv.reshape(1, b, t, -1), krp[:, MLA_D_NOPE:MLA_D_NOPE + MLA_D_ROPE].reshape(1, b, t, MLA_D_ROPE),
              kv4(kc), kv4(vc), kv4(ks), kv4(vs), kv4(kw), kv4(vw), conv_state[None])
    return y.reshape(b, t, d), states


def _page_specs(block, n_pages, pp):
    zeros = (0,) * (len(block) - 1)
    return [pl.BlockSpec(block, (lambda b, s, pt, k=k: (pt[b * n_pages + s * pp + k],) + zeros)) for k in range(pp)]


def _softmax_update(sc, v, m_scr, l_scr, acc_scr, v_transposed=False):
    m_old = m_scr[...]
    m_new = jnp.maximum(m_old, jnp.max(sc, axis=-1, keepdims=True))
    p = jnp.exp(sc - m_new)
    alpha = jnp.exp(m_old - m_new)
    l_scr[...] = alpha * l_scr[...] + jnp.sum(p, axis=-1, keepdims=True)
    pv = _dot_nt(p.astype(BF16), v) if v_transposed else _dot(p.astype(BF16), v)
    acc_scr[...] = alpha * acc_scr[...] + pv
    m_scr[...] = m_new


def _mla_decode_kernel(pt_ref, q_ref, knew_ref, *rest, pp, td, kv_rank):
    ckv_pages, kr_pages = rest[:pp], rest[pp:2 * pp]
    o_ref, kscr, krscr, m_scr, l_scr, acc_scr = rest[2 * pp:]
    s = pl.program_id(1)
    q = q_ref[...]
    rows = q.shape[0]

    @pl.when(s == 0)
    def _():
        m_scr[...] = jnp.full_like(m_scr, NEG)
        l_scr[...] = jnp.zeros_like(l_scr)
        acc_scr[...] = jnp.zeros_like(acc_scr)
        kn = knew_ref[...]
        trow = _vdiv(_iota((rows, 1), 0), MLA_HEADS)
        col = _iota((1, kn.shape[0]), 1)
        sc = jnp.where((col <= trow) & (col < td), _dot_nt(q, kn), NEG)
        _softmax_update(sc, kn[:, :kv_rank], m_scr, l_scr, acc_scr)

    for k in range(pp):
        kscr[k * PAGE_SIZE:(k + 1) * PAGE_SIZE, :] = ckv_pages[k][...].astype(BF16)
        krscr[:, k * PAGE_SIZE:(k + 1) * PAGE_SIZE] = kr_pages[k][...].astype(BF16)
    kt = kscr[...]
    sc = _dot_nt(q[:, :kv_rank], kt) + _dot(q[:, kv_rank:kv_rank + MLA_D_ROPE], krscr[...])
    _softmax_update(sc, kt, m_scr, l_scr, acc_scr)

    @pl.when(s == pl.num_programs(1) - 1)
    def _():
        o_ref[...] = acc_scr[...] / l_scr[...]


def _mla_decode(pt, qd, knew, ckv_pool, kr_pool, *, td, pp):
    bd, rows, qw = qd.shape
    n_pages = pt.shape[0] // bd
    kv_rank = ckv_pool.shape[-1]
    per_b = lambda b, s, pt: (b, 0, 0)
    in_specs = ([pl.BlockSpec((None, rows, qw), per_b), pl.BlockSpec((None,) + knew.shape[1:], per_b)]
                + _page_specs((None, PAGE_SIZE, kv_rank), n_pages, pp)
                + _page_specs((None, MLA_D_ROPE, PAGE_SIZE), n_pages, pp))
    kern = functools.partial(_mla_decode_kernel, pp=pp, td=td, kv_rank=kv_rank)
    return pl.pallas_call(
        kern,
        grid_spec=pltpu.PrefetchScalarGridSpec(
            num_scalar_prefetch=1, grid=(bd, n_pages // pp), in_specs=in_specs,
            out_specs=pl.BlockSpec((None, rows, kv_rank), per_b),
            scratch_shapes=[pltpu.VMEM((pp * PAGE_SIZE, kv_rank), BF16), pltpu.VMEM((MLA_D_ROPE, pp * PAGE_SIZE), BF16),
                            pltpu.VMEM((rows, 1), F32), pltpu.VMEM((rows, 1), F32), pltpu.VMEM((rows, kv_rank), F32)]),
        out_shape=jax.ShapeDtypeStruct((bd, rows, kv_rank), F32),
        compiler_params=_cparams("parallel", "arbitrary"), name="mla_decode",
    )(pt, qd, knew, *([ckv_pool] * pp), *([kr_pool] * pp))


def _alibi_rows(rows, td):
    r = _iota((rows, 1), 0)
    return _vdiv(r, td), _vmod(r, td)


def _slope_rows(hrow, g):
    slope = jnp.zeros(hrow.shape, F32)
    for hh in range(NSA_HPG):
        slope = jnp.where(hrow == hh, SLOPES[g * NSA_HPG + hh], slope)
    return slope


def _cmp_decode_kernel(pt_ref, q_ref, *rest, pp, td, past, n_cmp, n_sel_pad):
    k_pages, v_pages = rest[:pp], rest[pp:2 * pp]
    (w1k_ref, w1v_ref, pk_ref, pv_ref, w2k_ref, w2v_ref, o_ref, score_ref, kp_scr, vp_scr, yk_scr, yv_scr) = rest[2 * pp:]
    s = pl.program_id(1)
    cpp = PAGE_SIZE // CMP_STRIDE
    for k in range(pp):
        kp_scr[k * PAGE_SIZE:(k + 1) * PAGE_SIZE, :] = k_pages[k][...].T
        vp_scr[k * PAGE_SIZE:(k + 1) * PAGE_SIZE, :] = v_pages[k][...].T
    chunk_rows = lambda scr: jnp.concatenate(
        [scr[pl.ds(j, pp * cpp, stride=CMP_STRIDE), :] for j in range(CMP_STRIDE)], axis=1).astype(BF16)
    xk = chunk_rows(kp_scr)
    xv = chunk_rows(vp_scr)
    r0 = pl.multiple_of(s * (pp * cpp), pp * cpp)
    yk_scr[pl.ds(r0, pp * cpp), :] = _dot(xk, w1k_ref[...])
    yv_scr[pl.ds(r0, pp * cpp), :] = _dot(xv, w1v_ref[...])

    @pl.when(s == pl.num_programs(1) - 1)
    def _():
        kcc = _compress_finish(yk_scr[...], _dot(pk_ref[...], w1k_ref[...]), w2k_ref).astype(BF16)
        vcc = _compress_finish(yv_scr[...], _dot(pv_ref[...], w1v_ref[...]), w2v_ref).astype(BF16)
        ncp = kcc.shape[0]
        rows = NSA_HPG * td
        hrow, trow = _alibi_rows(rows, td)
        cidx = _iota((1, ncp), 1)
        dist = (past + trow) - (cidx * CMP_STRIDE + CMP_BLOCK - 1)
        mask = (dist >= 0) & (cidx < n_cmp)
        distf = dist.astype(F32)
        tsum = (_vmod(_iota((8, rows), 1), td) == _iota((8, rows), 0)).astype(F32)
        ov = _overlap(ncp, n_sel_pad)
        t8 = _iota((8, 1), 0)
        cur = _vdiv(past + t8, SLC_BLOCK)
        jj = _iota((8, n_sel_pad), 1)
        for g in range(NSA_GROUPS):
            sc = _dot_nt(q_ref[g], kcc) - _slope_rows(hrow, g) * distf
            sc = jnp.where(mask, sc, NEG)
            m = jnp.max(sc, axis=-1, keepdims=True)
            p = jnp.where(mask, jnp.exp(sc - m), 0.0)
            l = jnp.sum(p, axis=-1, keepdims=True)
            p = p / jnp.where(l > 0.0, l, 1.0)
            o_ref[g] = _dot(p.astype(BF16), vcc)
            imp = _dot_exact(tsum, p)
            score_ref[g] = _force_scores(_dot_exact(imp, ov), cur, jj)


def _cmp_decode(pt, qd, k_pool, v_pool, wts, *, td, pp, past, n_sel_pad):
    bd = qd.shape[0]
    n_pages = pt.shape[0] // bd
    cpp = PAGE_SIZE // CMP_STRIDE
    nch = n_pages * cpp
    n_cmp = (past + td) // CMP_STRIDE - CMP_BLOCK // CMP_STRIDE + 1
    assert (past + td) // CMP_STRIDE == nch, "new rows must not complete a chunk"
    per_b = lambda b, s, pt: (b, 0, 0, 0)
    c2 = lambda b, s, pt: (0, 0)
    rows = qd.shape[2]
    in_specs = ([pl.BlockSpec((None,) + qd.shape[1:], per_b)]
                + _page_specs((None, LANE, PAGE_SIZE), n_pages, pp) + _page_specs((None, LANE, PAGE_SIZE), n_pages, pp)
                + [pl.BlockSpec((CHUNK_FEATS, 2 * LANE), c2) for _ in range(2)]
                + [pl.BlockSpec((8, CHUNK_FEATS), c2) for _ in range(2)]
                + [pl.BlockSpec((LANE, LANE), c2) for _ in range(2)])
    kern = functools.partial(_cmp_decode_kernel, pp=pp, td=td, past=past, n_cmp=n_cmp, n_sel_pad=n_sel_pad)
    return pl.pallas_call(
        kern,
        grid_spec=pltpu.PrefetchScalarGridSpec(
            num_scalar_prefetch=1, grid=(bd, n_pages // pp), in_specs=in_specs,
            out_specs=[pl.BlockSpec((None, NSA_GROUPS, rows, LANE), per_b),
                       pl.BlockSpec((None, NSA_GROUPS, 8, n_sel_pad), per_b)],
            scratch_shapes=[pltpu.VMEM((pp * PAGE_SIZE, LANE), F32) for _ in range(2)]
            + [pltpu.VMEM((nch, 2 * LANE), F32) for _ in range(2)]),
        out_shape=[jax.ShapeDtypeStruct((bd, NSA_GROUPS, rows, LANE), F32),
                   jax.ShapeDtypeStruct((bd, NSA_GROUPS, 8, n_sel_pad), F32)],
        compiler_params=_cparams("parallel", "arbitrary"), name="cmp_decode",
    )(pt, qd, *([k_pool] * pp), *([v_pool] * pp), wts['cmp_w1k'], wts['cmp_w1v'], wts['cmp_posk'], wts['cmp_posv'],
      wts['cmp_w2k'], wts['cmp_w2v'])


def _rank_kernel(score_ref, cur_ref, sel_ref, *, n_sel):
    sc = score_ref[...]
    jj = _iota(sc.shape, 0)

    def body(i, rank):
        ri = score_ref[pl.ds(i, 1), :]
        beats = (ri > sc) | ((ri == sc) & (i < jj))
        return rank + beats.astype(F32)

    rank = lax.fori_loop(0, n_sel, body, jnp.zeros(sc.shape, F32))
    sel = (rank < float(min(SLC_TOP_N, n_sel))) & (jj <= cur_ref[...])
    sel_ref[...] = sel.astype(F32)


def _rank(score_t, cur, *, n_sel):
    full = lambda a: pl.BlockSpec(a.shape, lambda: (0,) * a.ndim)
    return pl.pallas_call(
        functools.partial(_rank_kernel, n_sel=n_sel), in_specs=[full(score_t), full(cur)],
        out_specs=full(score_t), out_shape=jax.ShapeDtypeStruct(score_t.shape, F32), name="rank_decode",
    )(score_t, cur)


def _slc_decode_kernel(pt_ref, q_ref, sel_ref, selnew_ref, knew_ref, vnew_ref, *rest, pp, td, past):
    k_pages, v_pages = rest[:pp], rest[pp:2 * pp]
    o_ref, kscr, vscr, m_scr, l_scr, acc_scr = rest[2 * pp:]
    s = pl.program_id(1)
    rows = q_ref.shape[1]
    hrow, trow = _alibi_rows(rows, td)
    tk = pp * PAGE_SIZE
    nblk = tk // SLC_BLOCK

    @pl.when(s == 0)
    def _():
        m_scr[...] = jnp.full_like(m_scr, NEG)
        l_scr[...] = jnp.zeros_like(l_scr)
        acc_scr[...] = jnp.zeros_like(acc_scr)
        kn = knew_ref[...]
        vn = vnew_ref[...]
        col = _iota((1, kn.shape[0]), 1)
        dist = trow - col
        for g in range(NSA_GROUPS):
            sc = _dot_nt(q_ref[g], kn) - _slope_rows(hrow, g) * dist.astype(F32)
            mask = (dist >= 0) & (col < td) & (selnew_ref[g] > 0.5)
            _softmax_update(jnp.where(mask, sc, NEG), vn, m_scr.at[g], l_scr.at[g], acc_scr.at[g])

    for k in range(pp):
        kscr[:, k * PAGE_SIZE:(k + 1) * PAGE_SIZE] = k_pages[k][...].astype(BF16)
        vscr[:, k * PAGE_SIZE:(k + 1) * PAGE_SIZE] = v_pages[k][...].astype(BF16)
    kt = kscr[...]
    vt = vscr[...]
    kpos = s * tk + _iota((1, tk), 1)
    distf = ((past + trow) - kpos).astype(F32)
    expand = (_vdiv(_iota((nblk, tk), 1), SLC_BLOCK) == _iota((nblk, tk), 0)).astype(BF16)
    for g in range(NSA_GROUPS):
        sc = _dot(q_ref[g], kt) - _slope_rows(hrow, g) * distf
        selx = _dot(sel_ref[g], expand) > 0.5
        _softmax_update(jnp.where(selx, sc, NEG), vt, m_scr.at[g], l_scr.at[g], acc_scr.at[g], v_transposed=True)

    @pl.when(s == pl.num_programs(1) - 1)
    def _():
        o_ref[...] = acc_scr[...] / l_scr[...]


def _slc_decode(pt, qd, sel16, selnew, knew, vnew, k_pool, v_pool, *, td, pp, past):
    bd = qd.shape[0]
    rows = qd.shape[2]
    n_pages = pt.shape[0] // bd
    per_b = lambda b, s, pt: (b, 0, 0, 0)
    per_b3 = lambda b, s, pt: (b, 0, 0)
    in_specs = ([pl.BlockSpec((None,) + qd.shape[1:], per_b),
                 pl.BlockSpec((None, None) + sel16.shape[2:], lambda b, s, pt: (b, s, 0, 0, 0)),
                 pl.BlockSpec((None,) + selnew.shape[1:], per_b),
                 pl.BlockSpec((None,) + knew.shape[1:], per_b3), pl.BlockSpec((None,) + vnew.shape[1:], per_b3)]
                + _page_specs((None, LANE, PAGE_SIZE), n_pages, pp) + _page_specs((None, LANE, PAGE_SIZE), n_pages, pp))
    kern = functools.partial(_slc_decode_kernel, pp=pp, td=td, past=past)
    return pl.pallas_call(
        kern,
        grid_spec=pltpu.PrefetchScalarGridSpec(
            num_scalar_prefetch=1, grid=(bd, n_pages // pp), in_specs=in_specs,
            out_specs=pl.BlockSpec((None, NSA_GROUPS, rows, LANE), per_b),
            scratch_shapes=[pltpu.VMEM((LANE, pp * PAGE_SIZE), BF16) for _ in range(2)]
            + [pltpu.VMEM((NSA_GROUPS, rows, 1), F32) for _ in range(2)] + [pltpu.VMEM((NSA_GROUPS, rows, LANE), F32)]),
        out_shape=jax.ShapeDtypeStruct((bd, NSA_GROUPS, rows, LANE), F32),
        compiler_params=_cparams("parallel", "arbitrary"), name="slc_decode",
    )(pt, qd, sel16, selnew, knew, vnew, *([k_pool] * pp), *([v_pool] * pp))


def _win_decode_kernel(q_ref, kst_ref, vst_ref, knew_ref, vnew_ref, o_ref, *, td, past):
    rows = q_ref.shape[1]
    hrow, trow = _alibi_rows(rows, td)
    nbuf = kst_ref.shape[1]
    kst = kst_ref[...].astype(BF16)
    vst = vst_ref[...].astype(BF16)
    kn = knew_ref[...]
    vn = vnew_ref[...]
    kpos = past - nbuf + _iota((1, nbuf), 1)
    d1 = (past + trow) - kpos
    m1 = (d1 >= 0) & (d1 < WINDOW) & (kpos >= 0)
    col = _iota((1, kn.shape[0]), 1)
    d2 = trow - col
    m2 = (d2 >= 0) & (d2 < WINDOW) & (col < td)
    for g in range(NSA_GROUPS):
        slope = _slope_rows(hrow, g)
        s1 = jnp.where(m1, _dot(q_ref[g], kst) - slope * d1.astype(F32), NEG)
        s2 = jnp.where(m2, _dot_nt(q_ref[g], kn) - slope * d2.astype(F32), NEG)
        m = jnp.maximum(jnp.max(s1, axis=-1, keepdims=True), jnp.max(s2, axis=-1, keepdims=True))
        p1 = jnp.exp(s1 - m)
        p2 = jnp.exp(s2 - m)
        l = jnp.sum(p1, axis=-1, keepdims=True) + jnp.sum(p2, axis=-1, keepdims=True)
        o_ref[g] = (_dot_nt(p1.astype(BF16), vst) + _dot(p2.astype(BF16), vn)) / l


def _win_decode(qd, kst, vst, knew, vnew, *, td, past):
    bd = qd.shape[0]
    rows = qd.shape[2]
    per_b = lambda b: (b, 0, 0, 0)
    per_b3 = lambda b: (b, 0, 0)
    blk3 = lambda a: pl.BlockSpec((None,) + a.shape[1:], per_b3)
    return pl.pallas_call(
        functools.partial(_win_decode_kernel, td=td, past=past), grid=(bd,),
        in_specs=[pl.BlockSpec((None,) + qd.shape[1:], per_b), blk3(kst), blk3(vst), blk3(knew), blk3(vnew)],
        out_specs=pl.BlockSpec((None, NSA_GROUPS, rows, LANE), per_b),
        out_shape=jax.ShapeDtypeStruct((bd, NSA_GROUPS, rows, LANE), F32),
        compiler_params=_cparams("parallel"), name="win_decode",
    )(qd, kst, vst, knew, vnew)


def _pad_rows(a, rows):
    return jnp.pad(a, ((0, 0), (0, rows - a.shape[1]), (0, 0)))


def _sample(x_sample, caches, page_table, w):
    (c_ckv, c_krope, c_cmp_k, c_cmp_v, c_slc_k, c_slc_v, s_win_k, s_win_v, s_conv) = caches
    bd, td, d = x_sample.shape
    n = bd * td
    n_pages = page_table.shape[1]
    past = n_pages * PAGE_SIZE
    n_pool = c_ckv.shape[0]
    kv_rank = c_ckv.shape[-1]
    x = x_sample.reshape(n, d)
    pos = past + jnp.arange(td, dtype=jnp.int32)
    cosq, sinq = _rope_tables(jnp.tile(pos, bd))
    (qrot, qabs, ckv, krp, qn, kc, vc, ks, vs, kw, vw, gn, ga, gb) = _inproj(
        x, w, cosq, sinq, sample=True, tm=n, tab_blocks=1)
    pt = page_table.reshape(-1)
    krope = krp[:, MLA_D_NOPE:MLA_D_NOPE + MLA_D_ROPE]
    qrope = qrot.reshape(bd, td, MLA_HEADS, HEAD_PAD)[..., MLA_D_NOPE:MLA_D_NOPE + MLA_D_ROPE]
    qpad = LANE - MLA_D_ROPE
    qd_mla = jnp.concatenate([qabs.reshape(bd, td, MLA_HEADS, kv_rank), qrope,
                              jnp.zeros((bd, td, MLA_HEADS, qpad), F32)], axis=-1)
    qd_mla = qd_mla.reshape(bd, td * MLA_HEADS, kv_rank + LANE).astype(BF16)
    knew = jnp.concatenate([ckv, krope, jnp.zeros((n, qpad), F32)], axis=-1).reshape(bd, td, -1)
    knew = _pad_rows(knew, 8).astype(BF16)
    pos_minor = lambda c: jnp.moveaxis(c, 1, -1).reshape(c.shape[0], -1, c.shape[1])
    o_lat = _mla_decode(pt, qd_mla, knew, c_ckv, pos_minor(c_krope), td=td, pp=min(16, n_pages))
    qg = qn.astype(F32).reshape(bd, td, NSA_GROUPS, NSA_HPG, NSA_DH).transpose(0, 2, 3, 1, 4)
    qg = qg.reshape(bd, NSA_GROUPS, NSA_HPG * td, NSA_DH)
    lane_g = (jnp.arange(LANE) // NSA_DH)[None, :] == jnp.arange(NSA_GROUPS)[:, None]
    qd = jnp.where(lane_g[None, :, None, :], jnp.tile(qg, (1, 1, 1, NSA_GROUPS)), 0.0).astype(BF16)
    n_sel = -(-(past + td) // SLC_BLOCK)
    n_sel_pad = -(-n_sel // LANE) * LANE
    o_cmp, score = _cmp_decode(pt, qd, pos_minor(c_cmp_k), pos_minor(c_cmp_v),
                               w, td=td, pp=min(32, n_pages), past=past, n_sel_pad=n_sel_pad)
    cur = jnp.broadcast_to(((past + jnp.arange(8)) // SLC_BLOCK).astype(jnp.int32), (bd, NSA_GROUPS, 8)).reshape(1, -1)
    sel_t = _rank(score.reshape(-1, n_sel_pad).T, cur, n_sel=n_sel)
    sel = sel_t.T.reshape(bd, NSA_GROUPS, 8, n_sel_pad)[:, :, :td]
    pp_slc = min(16, n_pages)
    bps = pp_slc * PAGE_SIZE // SLC_BLOCK
    n_past_blk = past // SLC_BLOCK
    sel_past = sel[..., :n_past_blk].reshape(bd, NSA_GROUPS, td, n_past_blk // bps, bps).transpose(0, 3, 1, 2, 4)
    sel16 = jnp.tile(sel_past, (1, 1, 1, NSA_HPG, 1)).astype(BF16)
    selnew = jnp.tile(jnp.broadcast_to(sel[..., n_past_blk:n_past_blk + 1], (bd, NSA_GROUPS, td, 8)), (1, 1, NSA_HPG, 1))
    new8 = lambda a: _pad_rows(a.reshape(bd, td, LANE), 8).astype(BF16)
    o_slc = _slc_decode(pt, qd, sel16, selnew, new8(ks), new8(vs), pos_minor(c_slc_k), pos_minor(c_slc_v),
                        td=td, pp=pp_slc, past=past)
    nbuf = s_win_k.shape[1]
    o_win = _win_decode(qd, pos_minor(s_win_k), pos_minor(s_win_v), new8(kw), new8(vw),
                        td=td, past=past)

    def heads_out(o):
        o = o.reshape(bd, NSA_GROUPS, NSA_HPG, td, NSA_GROUPS, NSA_DH)
        o = jnp.stack([o[:, g, :, :, g] for g in range(NSA_GROUPS)], axis=1)
        return o.transpose(0, 3, 1, 2, 4).reshape(n, NSA_HEADS * NSA_DH)

    dff = w['w_gate'].shape[1]
    prev1 = jnp.zeros((bd, td, dff), F32).at[:, 0].set(s_conv[:, 1])
    prev2 = jnp.zeros((bd, td, dff), F32).at[:, 0].set(s_conv[:, 0]).at[:, 1].set(s_conv[:, 1])
    y, g = _finish(x, o_lat.reshape(n, MLA_HEADS * kv_rank), heads_out(o_cmp), heads_out(o_slc), heads_out(o_win),
                   gn, ga, gb, (prev1.reshape(n, dff), prev2.reshape(n, dff)), w, tm=n, period=td, full_g=True,
                   latent=True)
    kv4 = lambda a: a.reshape(1, bd, td, NSA_GROUPS, NSA_DH)
    win = lambda st, new: jnp.concatenate([st, new.reshape(bd, td, NSA_GROUPS, NSA_DH)], axis=1)[None, :, -nbuf:]
    conv_state = jnp.concatenate([s_conv, g.reshape(bd, td, dff)], axis=1)[None, :, -(CONV_W - 1):]
    states = (ckv.reshape(1, bd, td, kv_rank), krope.reshape(1, bd, td, MLA_D_ROPE), kv4(kc), kv4(vc), kv4(ks), kv4(vs),
              win(s_win_k, kw), win(s_win_v, vw), conv_state)
    return y.reshape(bd, td, d), states


def kernel(x_prompt, x_sample, cache_mla_ckv, cache_mla_krope, cache_nsa_cmp_k, cache_nsa_cmp_v, cache_nsa_slc_k, cache_nsa_slc_v, state_win_k, state_win_v, state_ffn_conv, page_table, norm1_g, w_in, q_norm_g, kv_norm_g, w_uq, w_uk, w_uv, cmp_pos_k, cmp_w1_k, cmp_w2_k, cmp_pos_v, cmp_w1_v, cmp_w2_v, w_proj_mla, w_proj_nsa, w_out, norm2_g, w_gate, w_up, conv_w, conv_b, w_down, norm_f_g):
    assert norm1_g.shape[0] == 1, "single-layer trunk"
    p = dict(norm1_g=norm1_g[0], w_in=w_in[0], q_norm_g=q_norm_g[0], kv_norm_g=kv_norm_g[0], w_uq=w_uq[0],
             w_uk=w_uk[0], w_uv=w_uv[0], cmp_pos_k=cmp_pos_k[0], cmp_w1_k=cmp_w1_k[0], cmp_w2_k=cmp_w2_k[0],
             cmp_pos_v=cmp_pos_v[0], cmp_w1_v=cmp_w1_v[0], cmp_w2_v=cmp_w2_v[0], w_proj_mla=w_proj_mla[0],
             w_proj_nsa=w_proj_nsa[0], w_out=w_out[0], norm2_g=norm2_g[0], w_gate=w_gate[0], w_up=w_up[0],
             conv_w=conv_w[0], conv_b=conv_b[0], w_down=w_down[0], norm_f_g=norm_f_g)
    w = _prep_weights(p)
    y_p, ps = _prompt(x_prompt, w)
    caches = (cache_mla_ckv[0], cache_mla_krope[0], cache_nsa_cmp_k[0], cache_nsa_cmp_v[0], cache_nsa_slc_k[0],
              cache_nsa_slc_v[0], state_win_k[0], state_win_v[0], state_ffn_conv[0])
    y_s, ss = _sample(x_sample, caches, page_table, w)
    out = [y_p, y_s]
    for a, b in zip(ps, ss):
        out += [a, b]
    return tuple(out)
```

```python
import functools

import numpy as np
import jax
import jax.numpy as jnp
from jax import lax
from jax.experimental import pallas as pl
from jax.experimental.pallas import tpu as pltpu

MLA_HEADS = 8
MLA_D_NOPE = 64
MLA_D_ROPE = 32
MLA_D_V = 64
ROPE_THETA = 10000.0
MLA_SCALE = (MLA_D_NOPE + MLA_D_ROPE) ** -0.5
NSA_HEADS = 8
NSA_GROUPS = 2
NSA_HPG = NSA_HEADS // NSA_GROUPS
NSA_DH = 64
NSA_SCALE = NSA_DH ** -0.5
CMP_BLOCK = 32
CMP_STRIDE = 16
SLC_BLOCK = 64
SLC_TOP_N = 16
WINDOW = 512
CONV_W = 3
PAGE_SIZE = 128
EPS = 1e-6
NEG = -1e30
FORCE = 1e9

LOG2E = 1.4426950408889634
LANE = 128
AUX_POS_HI, AUX_POS_LO, AUX_BLK0 = 64, 65, 72
HEAD_PAD = 128
CHUNK_FEATS = CMP_STRIDE * NSA_GROUPS * NSA_DH
SLOPES = tuple(float(2.0 ** (-8.0 * (h + 1) / NSA_HEADS)) for h in range(NSA_HEADS))
VMEM_LIMIT = 56 * 1024 * 1024

F32 = jnp.float32
BF16 = jnp.bfloat16
_NT = (((1,), (1,)), ((), ()))


def _cparams(*sem):
    return pltpu.CompilerParams(dimension_semantics=sem, vmem_limit_bytes=VMEM_LIMIT)


def _rms(x, g):
    return x * lax.rsqrt(jnp.mean(x * x, axis=-1, keepdims=True) + EPS) * g


def _dot(a, b):
    return jnp.dot(a, b, preferred_element_type=F32)


def _dot_nt(a, b):
    return lax.dot_general(a, b, _NT, preferred_element_type=F32)


def _dot_exact(a, b):
    return jnp.dot(a, b, preferred_element_type=F32, precision=lax.Precision.HIGHEST)


def _iota(shape, dim):
    return lax.broadcasted_iota(jnp.int32, shape, dim)


def _log2(n):
    assert n > 0 and n & (n - 1) == 0, n
    return n.bit_length() - 1


def _vdiv(x, n):
    return lax.shift_right_logical(x, jnp.full(x.shape, _log2(n), jnp.int32))


def _vmod(x, n):
    assert n & (n - 1) == 0, n
    return x & (n - 1)


_O_CQ, _O_CKV, _O_QN, _O_K6, _O_GA = 0, 384, 640, 1152, 1920


def _inproj_kernel(x_ref, g1_ref, w_ref, qg_ref, kvg_ref, wuq_ref, wk_ref, wv_ref, cos_ref, sin_ref,
                   *outs, sample, q_rank, kv_rank, d_model, tiles_per_seq, q_scale):
    o_gb = _O_GA + d_model
    o_kr = o_gb + d_model
    x = x_ref[...]
    hn = _rms(x, g1_ref[...])
    y = _dot(hn.astype(BF16), w_ref[...])
    cosq = cos_ref[...]
    sinq = sin_ref[...]
    nq = MLA_HEADS * HEAD_PAD
    cqn = _rms(y[:, _O_CQ:_O_CQ + q_rank], qg_ref[...])
    q2 = _dot(cqn.astype(BF16), wuq_ref[...])
    ckv = _rms(y[:, _O_CKV:_O_CKV + kv_rank], kvg_ref[...])
    kr = y[:, o_kr:o_kr + LANE] * cosq + y[:, o_kr + LANE:o_kr + 2 * LANE] * sinq
    ckv_b = ckv.astype(BF16)
    it = iter(outs)
    if sample:
        qrot_ref, qabs_ref = next(it), next(it)
        for h in range(MLA_HEADS):
            sl = slice(h * HEAD_PAD, (h + 1) * HEAD_PAD)
            qh = (q2[:, sl] * cosq + q2[:, nq + h * HEAD_PAD:nq + (h + 1) * HEAD_PAD] * sinq) * q_scale
            qrot_ref[:, sl] = qh
            qabs_ref[:, h * kv_rank:(h + 1) * kv_rank] = _dot(qh.astype(BF16), wk_ref[h])
    else:
        q_ref, k_ref, v_ref = next(it), next(it), next(it)
        knp = _dot(ckv_b, wk_ref[...])
        for h in range(MLA_HEADS):
            sl = slice(h * HEAD_PAD, (h + 1) * HEAD_PAD)
            qh = (q2[:, sl] * cosq + q2[:, nq + h * HEAD_PAD:nq + (h + 1) * HEAD_PAD] * sinq) * q_scale
            q_ref[:, sl] = qh.astype(BF16)
            k_ref[:, sl] = (knp[:, sl] + kr).astype(BF16)
        v_ref[...] = _dot(ckv_b, wv_ref[...]).astype(BF16)
    ckv_ref, kr_ref, qn_ref = next(it), next(it), next(it)
    ckv_ref[...] = ckv
    kr_ref[...] = kr
    qn_ref[...] = (y[:, _O_QN:_O_QN + NSA_HEADS * NSA_DH] * NSA_SCALE).astype(BF16)
    for j in range(6):
        yj = y[:, _O_K6 + j * LANE:_O_K6 + (j + 1) * LANE]
        if sample:
            next(it)[...] = yj
        else:
            next(it)[...] = yj.T
    if not sample:
        tm = x.shape[0]
        pos = (pl.program_id(0) % tiles_per_seq) * tm + _iota((tm, 1), 0)
        lane = _iota((tm, LANE), 1)
        onehot = ((lane >= AUX_BLK0) & (_vdiv(pos, SLC_BLOCK) == lane - AUX_BLK0)).astype(F32)
        aux = jnp.where(lane == AUX_POS_HI, (pos - _vmod(pos, 256)).astype(F32),
                        jnp.where(lane == AUX_POS_LO, _vmod(pos, 256).astype(F32), onehot))
        for j in range(6):
            yj = y[:, _O_K6 + j * LANE:_O_K6 + (j + 1) * LANE]
            ref = next(it)
            if j in (2, 4):
                ref[:, :LANE] = jnp.where(lane < NSA_DH, yj, aux).astype(BF16)
                ref[:, LANE:] = jnp.where(lane < NSA_DH, pltpu.roll(yj, NSA_DH, 1), aux).astype(BF16)
            else:
                ref[...] = yj.astype(BF16)
    gn_ref, ga_ref, gb_ref = next(it), next(it), next(it)
    gn_ref[...] = jax.nn.sigmoid(y[:, o_kr + 2 * LANE:o_kr + 3 * LANE])
    ga_ref[...] = jax.nn.sigmoid(y[:, _O_GA:_O_GA + d_model])
    gb_ref[...] = jax.nn.sigmoid(y[:, o_gb:o_gb + d_model])


def _inproj(x, wts, cosq, sinq, *, sample, tm, tab_blocks):
    n, d = x.shape
    q_rank, kv_rank = wts['q_norm_g'].shape[1], wts['kv_norm_g'].shape[1]
    nq = MLA_HEADS * HEAD_PAD
    wk = wts['w_ukT'] if sample else wts['w_ukp']
    row = lambda i: (i, 0)
    const2 = lambda i: (0, 0)
    tab_map = (lambda i: (i % tab_blocks, 0))
    in_specs = [
        pl.BlockSpec((tm, d), row),
        pl.BlockSpec((1, d), const2),
        pl.BlockSpec(wts['w_in'].shape, const2),
        pl.BlockSpec((1, q_rank), const2),
        pl.BlockSpec((1, kv_rank), const2),
        pl.BlockSpec(wts['w_uq2'].shape, const2),
        pl.BlockSpec(wk.shape, (lambda i: (0, 0, 0)) if sample else const2),
        pl.BlockSpec(wts['w_uvf'].shape, const2),
        pl.BlockSpec((tm, LANE), tab_map),
        pl.BlockSpec((tm, LANE), tab_map),
    ]
    shapes = []
    if sample:
        shapes += [(nq, F32), (MLA_HEADS * kv_rank, F32)]
    else:
        shapes += [(nq, BF16), (nq, BF16), (MLA_HEADS * MLA_D_V, BF16)]
    shapes += [(kv_rank, F32), (LANE, F32), (NSA_HEADS * NSA_DH, BF16)]
    n_lead = len(shapes)
    shapes += [(LANE, F32)] * 6
    if not sample:
        shapes += [(LANE, BF16), (LANE, BF16), (2 * LANE, BF16), (LANE, BF16), (2 * LANE, BF16), (LANE, BF16)]
        assert AUX_BLK0 + -(-tab_blocks * tm // SLC_BLOCK) <= LANE, "block one-hot must fit the aux lanes"
    shapes += [(LANE, F32), (d, F32), (d, F32)]
    out_shape = [jax.ShapeDtypeStruct((n, w), dt) for w, dt in shapes]
    out_specs = [pl.BlockSpec((tm, w), row) for w, _ in shapes]
    if not sample:
        tpb = tab_blocks
        for j in range(n_lead, n_lead + 6):
            out_shape[j] = jax.ShapeDtypeStruct((n // (tpb * tm), LANE, tpb * tm), F32)
            out_specs[j] = pl.BlockSpec((None, LANE, tm), lambda i: (i // tpb, 0, i % tpb))
    q_scale = MLA_SCALE if sample else MLA_SCALE * LOG2E
    kern = functools.partial(_inproj_kernel, sample=sample, q_rank=q_rank, kv_rank=kv_rank, d_model=d,
                             tiles_per_seq=tab_blocks, q_scale=q_scale)
    return pl.pallas_call(
        kern, grid=(n // tm,), in_specs=in_specs, out_specs=out_specs, out_shape=out_shape,
        compiler_params=_cparams("parallel"), name="inproj_sample" if sample else "inproj_prompt",
    )(x, wts['norm1_g'], wts['w_in'], wts['q_norm_g'], wts['kv_norm_g'], wts['w_uq2'], wk, wts['w_uvf'],
      cosq, sinq)


def _nsa_query(q_ref, bias, h, tq):
    pair = q_ref[:, (h // 2) * LANE:(h // 2 + 1) * LANE].astype(F32)
    if h % 2:
        pair = pltpu.roll(pair, NSA_DH, 1)
    lane = _iota((tq, LANE), 1)
    aux = jnp.where((lane == AUX_POS_HI) | (lane == AUX_POS_LO), SLOPES[h], 0.0)
    if bias is not None:
        aux = aux + bias
    return jnp.where(lane < NSA_DH, pair, aux).astype(BF16)


def _flash_kernel(*refs, tq, tk, groups, par, dq, dk, dv, window, nsa, has_bias, base2):
    if has_bias:
        q_ref, k_ref, v_ref, bias_ref, o_ref = refs
    else:
        q_ref, k_ref, v_ref, o_ref = refs
    q_start = pl.program_id(1) * tq
    n_hi = (q_start + tq - 1) // tk + 1
    hi_full = (q_start + 1) // tk
    if window:
        n_lo = jnp.maximum(q_start - (window - 1), 0) // tk
        lo_full = (jnp.maximum(q_start + tq - window, 0) + tk - 1) // tk
    else:
        n_lo, lo_full = 0, 0
    e1 = jnp.clip(lo_full, n_lo, n_hi)
    e2 = jnp.clip(hi_full, e1, n_hi)
    ex = jnp.exp2 if base2 else jnp.exp
    for c0 in range(0, len(groups), par):
        chunk = groups[c0:c0 + par]
        qs = []
        for heads, kcol in chunk:
            if nsa:
                bias = bias_ref[:, kcol * LANE:(kcol + 1) * LANE].astype(F32) if has_bias else None
                parts = [_nsa_query(q_ref, bias, h, tq) for h in heads]
            else:
                parts = [q_ref[:, h * dq:(h + 1) * dq] for h in heads]
            qs.append(parts[0] if len(parts) == 1 else jnp.concatenate(parts, axis=0))
        rows = qs[0].shape[0]
        qpos = q_start + (_iota((rows, 1), 0) & (tq - 1))

        def step(j, carry, masked):
            k0 = pl.multiple_of(j * tk, tk)
            if masked:
                dist = qpos - (k0 + _iota((1, tk), 1))
                mask = dist >= 0
                if window:
                    mask = mask & (dist < window)
            out = []
            for (heads, kcol), qg, (m, l, acc) in zip(chunk, qs, carry):
                kt = k_ref[pl.ds(k0, tk), kcol * dk:(kcol + 1) * dk]
                vt = v_ref[pl.ds(k0, tk), kcol * dv:(kcol + 1) * dv]
                s = _dot_nt(qg, kt)
                if masked:
                    s = jnp.where(mask, s, NEG)
                m_new = jnp.maximum(m, jnp.max(s, axis=-1, keepdims=True))
                p = ex(s - m_new)
                alpha = ex(m - m_new)
                l = alpha * l + jnp.sum(p, axis=-1, keepdims=True)
                acc = alpha * acc + _dot(p.astype(BF16), vt)
                out.append((m_new, l, acc))
            return tuple(out)

        carry = tuple((jnp.full((rows, 1), NEG, F32), jnp.zeros((rows, 1), F32), jnp.zeros((rows, dv), F32))
                      for _ in chunk)
        carry = lax.fori_loop(n_lo, e1, functools.partial(step, masked=True), carry)
        carry = lax.fori_loop(e1, e2, functools.partial(step, masked=False), carry)
        carry = lax.fori_loop(e2, n_hi, functools.partial(step, masked=True), carry)
        for (heads, kcol), (m, l, acc) in zip(chunk, carry):
            o = acc / l
            for hh, h in enumerate(heads):
                o_ref[:, h * dv:(h + 1) * dv] = o[hh * tq:(hh + 1) * tq].astype(o_ref.dtype)


def _flash(q, k, v, bias, *, batch, seq, tq, tk, groups, par, dq, dk, dv, window, nsa, base2, name):
    n = q.shape[0]
    has_bias = bias is not None
    n_heads = sum(len(g[0]) for g in groups)
    qrow = lambda b, i: (b * (seq // tq) + i, 0)
    kv = lambda b, i: (b, 0)
    in_specs = [pl.BlockSpec((tq, q.shape[1]), qrow), pl.BlockSpec((seq, k.shape[1]), kv),
                pl.BlockSpec((seq, v.shape[1]), kv)]
    args = [q, k, v]
    if has_bias:
        in_specs.append(pl.BlockSpec((tq, bias.shape[1]), qrow))
        args.append(bias)
    kern = functools.partial(_flash_kernel, tq=tq, tk=tk, groups=groups, par=par, dq=dq, dk=dk, dv=dv, window=window,
                             nsa=nsa, has_bias=has_bias, base2=base2)
    return pl.pallas_call(
        kern, grid=(batch, seq // tq), in_specs=in_specs,
        out_specs=pl.BlockSpec((tq, n_heads * dv), qrow),
        out_shape=jax.ShapeDtypeStruct((n, n_heads * dv), BF16),
        compiler_params=_cparams("parallel", "arbitrary"), name=name,
    )(*args)


def _compress_rows(xk, xv, w1k_ref, w1v_ref, posk_ref, posv_ref):
    yk = _dot(xk.astype(BF16), w1k_ref[...])
    yv = _dot(xv.astype(BF16), w1v_ref[...])
    return yk, yv


def _compress_finish(y, posy, w2_ref):
    rows = y.shape[0]
    a = y[:, :LANE]
    b = pltpu.roll(y[:, LANE:], rows - 1, 0)
    pos = posy[0:1, :LANE] + posy[1:2, LANE:]
    hid = jax.nn.gelu(a + b + pos)
    return _dot(hid.astype(BF16), w2_ref[...])


def _compress_prompt_kernel(xk_ref, xv_ref, w1k_ref, w1v_ref, pk_ref, pv_ref, w2k_ref, w2v_ref, ok_ref, ov_ref):
    yk, yv = _compress_rows(xk_ref[...], xv_ref[...], w1k_ref, w1v_ref, pk_ref, pv_ref)
    ok_ref[...] = _compress_finish(yk, _dot(pk_ref[...], w1k_ref[...]), w2k_ref).astype(BF16)
    ov_ref[...] = _compress_finish(yv, _dot(pv_ref[...], w1v_ref[...]), w2v_ref).astype(BF16)


def _compress_prompt(kc, vc, wts, *, batch, seq):
    nch = seq // CMP_STRIDE
    xk = kc.reshape(batch * nch, CHUNK_FEATS)
    xv = vc.reshape(batch * nch, CHUNK_FEATS)
    row = lambda b: (b, 0)
    c2 = lambda b: (0, 0)
    wspec = pl.BlockSpec((CHUNK_FEATS, 2 * LANE), c2)
    pspec = pl.BlockSpec((8, CHUNK_FEATS), c2)
    w2spec = pl.BlockSpec((LANE, LANE), c2)
    return pl.pallas_call(
        _compress_prompt_kernel, grid=(batch,),
        in_specs=[pl.BlockSpec((nch, CHUNK_FEATS), row), pl.BlockSpec((nch, CHUNK_FEATS), row),
                  wspec, wspec, pspec, pspec, w2spec, w2spec],
        out_specs=[pl.BlockSpec((nch, LANE), row)] * 2,
        out_shape=[jax.ShapeDtypeStruct((batch * nch, LANE), BF16)] * 2,
        compiler_params=_cparams("parallel"), name="compress_prompt",
    )(xk, xv, wts['cmp_w1k'], wts['cmp_w1v'], wts['cmp_posk'], wts['cmp_posv'], wts['cmp_w2k'], wts['cmp_w2v'])


def _overlap(n_rows, n_sel):
    c = _iota((n_rows, n_sel), 0) * CMP_STRIDE
    j = _iota((n_rows, n_sel), 1) * SLC_BLOCK
    return ((c < j + SLC_BLOCK) & (c + CMP_BLOCK > j)).astype(F32)


def _force_scores(score, cur, jj):
    forced = (jj == 0) | (jj == cur) | (jj == cur - 1)
    score = jnp.where(forced, FORCE, score)
    return jnp.where(jj <= cur, score, NEG)


def _cmp_prompt_kernel(q_ref, k_ref, v_ref, o_ref, bias_ref, *, tq, n_cmp, n_sel):
    q_start = pl.program_id(1) * tq
    ncp = k_ref.shape[0]
    rows = NSA_HPG * tq
    qpos = q_start + (_iota((rows, 1), 0) & (tq - 1))
    cidx = _iota((1, ncp), 1)
    dist = qpos - (cidx * CMP_STRIDE + CMP_BLOCK - 1)
    mask = (dist >= 0) & (cidx < n_cmp)
    distf = dist.astype(F32)
    hrow = _vdiv(_iota((rows, 1), 0), tq)
    nsp = -(-n_sel // 8) * 8
    cur = _vdiv(q_start + _iota((1, tq), 1), SLC_BLOCK)
    jj = _iota((nsp, tq), 0)
    cb = _iota((nsp, ncp), 1) * CMP_STRIDE
    jb = _iota((nsp, ncp), 0) * SLC_BLOCK
    ov_t = ((cb < jb + SLC_BLOCK) & (cb + CMP_BLOCK > jb)).astype(F32)
    for g in range(NSA_GROUPS):
        heads = range(g * NSA_HPG, (g + 1) * NSA_HPG)
        qg = jnp.concatenate([q_ref[:, h * NSA_DH:(h + 1) * NSA_DH] for h in heads], axis=0)
        slope = jnp.zeros((rows, 1), F32)
        for hh, h in enumerate(heads):
            slope = jnp.where(hrow == hh, SLOPES[h], slope)
        s = _dot_nt(qg, k_ref[:, g * NSA_DH:(g + 1) * NSA_DH]) - slope * distf
        s = jnp.where(mask, s, NEG)
        m = jnp.max(s, axis=-1, keepdims=True)
        p = jnp.where(mask, jnp.exp(s - m), 0.0)
        l = jnp.sum(p, axis=-1, keepdims=True)
        p = p / jnp.where(l > 0.0, l, 1.0)
        o = _dot(p.astype(BF16), v_ref[:, g * NSA_DH:(g + 1) * NSA_DH])
        imp = p[0:tq]
        for hh in range(1, NSA_HPG):
            imp = imp + p[hh * tq:(hh + 1) * tq]
            o_ref[:, (g * NSA_HPG + hh) * NSA_DH:(g * NSA_HPG + hh + 1) * NSA_DH] = o[hh * tq:(hh + 1) * tq].astype(BF16)
        o_ref[:, g * NSA_HPG * NSA_DH:(g * NSA_HPG + 1) * NSA_DH] = o[0:tq].astype(BF16)
        score = lax.dot_general(ov_t, imp, _NT, preferred_element_type=F32, precision=lax.Precision.HIGHEST)
        score = _force_scores(score, cur, jj)
        rank = jnp.zeros((nsp, tq), F32)
        for i in range(n_sel):
            ri = score[i:i + 1, :]
            beats = (ri > score) | ((ri == score) & (i < jj))
            rank = rank + beats.astype(F32)
        sel = (rank < float(min(SLC_TOP_N, n_sel))) & (jj <= cur)
        bias_t = jnp.where(sel | (jj >= n_sel), 0.0, NEG)
        bias_t = jnp.concatenate([jnp.zeros((AUX_BLK0, tq), F32), bias_t,
                                  jnp.zeros((LANE - AUX_BLK0 - nsp, tq), F32)], axis=0)
        bias_ref[:, g * LANE:(g + 1) * LANE] = bias_t.T.astype(BF16)


def _cmp_prompt(qn, kcc, vcc, *, batch, seq, tq):
    n = qn.shape[0]
    nch = seq // CMP_STRIDE
    n_cmp = nch - CMP_BLOCK // CMP_STRIDE + 1
    n_sel = -(-seq // SLC_BLOCK)
    qrow = lambda b, i: (b * (seq // tq) + i, 0)
    kv = lambda b, i: (b, 0)
    kern = functools.partial(_cmp_prompt_kernel, tq=tq, n_cmp=n_cmp, n_sel=n_sel)
    return pl.pallas_call(
        kern, grid=(batch, seq // tq),
        in_specs=[pl.BlockSpec((tq, qn.shape[1]), qrow), pl.BlockSpec((nch, LANE), kv), pl.BlockSpec((nch, LANE), kv)],
        out_specs=[pl.BlockSpec((tq, NSA_HEADS * NSA_DH), qrow), pl.BlockSpec((tq, NSA_GROUPS * LANE), qrow)],
        out_shape=[jax.ShapeDtypeStruct((n, NSA_HEADS * NSA_DH), BF16),
                   jax.ShapeDtypeStruct((n, NSA_GROUPS * LANE), BF16)],
        compiler_params=_cparams("parallel", "arbitrary"), name="cmp_prompt",
    )(qn, kcc, vcc)


def _finish_kernel(*refs, period, latent, has_state):
    it = iter(refs)
    x_ref, omla_ref, ocmp_ref, oslc_ref, owin_ref, gn_ref, ga_ref, gb_ref = (next(it) for _ in range(8))
    prev1_ref, prev2_ref = (next(it), next(it)) if has_state else (None, None)
    gx_ref = next(it)
    wuv_ref = next(it) if latent else None
    (wpm_ref, wpn_ref, wo_ref, g2_ref, wg_ref, wu_ref, cw_ref, cb_ref, wd_ref, gf_ref,
     y_ref, gout_ref, carry_ref) = it
    tm = x_ref.shape[0]
    gexp = _dot_exact(gn_ref[...], gx_ref[...])
    w = NSA_HEADS * NSA_DH
    o_nsa = (gexp[:, 0:w] * ocmp_ref[...].astype(F32) + gexp[:, w:2 * w] * oslc_ref[...].astype(F32)
             + gexp[:, 2 * w:3 * w] * owin_ref[...].astype(F32))
    o_mla = omla_ref[...].astype(BF16)
    if latent:
        o_mla = _dot(o_mla, wuv_ref[...]).astype(BF16)
    merged = (ga_ref[...] * _dot(o_mla, wpm_ref[...])
              + gb_ref[...] * _dot(o_nsa.astype(BF16), wpn_ref[...]))
    x1 = x_ref[...] + _dot(merged.astype(BF16), wo_ref[...])
    h2 = _rms(x1, g2_ref[...]).astype(BF16)
    g = _dot(h2, wg_ref[...])
    u = _dot(h2, wu_ref[...])
    row = _iota((tm, 1), 0)
    i = pl.program_id(0)
    t = _vmod(i * tm + row, period)
    g1 = pltpu.roll(g, 1, 0)
    g2 = pltpu.roll(g, 2, 0)
    if period > tm:
        @pl.when(i == 0)
        def _():
            carry_ref[...] = jnp.zeros_like(carry_ref)
        c = carry_ref[...]
        g1 = jnp.where(row == 0, c[7:8], g1)
        g2 = jnp.where(row == 0, c[6:7], jnp.where(row == 1, c[7:8], g2))
        carry_ref[...] = g[tm - 8:tm]
    g1 = jnp.where(t >= 1, g1, prev1_ref[...] if has_state else 0.0)
    g2 = jnp.where(t >= 2, g2, prev2_ref[...] if has_state else 0.0)
    cw = cw_ref[...]
    conv = cb_ref[...] + cw[0:1] * g2 + cw[1:2] * g1 + cw[2:3] * g
    act = (jax.nn.silu(conv) * u).astype(BF16)
    x2 = x1 + _dot(act, wd_ref[...])
    y_ref[...] = _rms(x2, gf_ref[...])
    gout_ref[...] = g[tm - 8:tm] if gout_ref.shape[0] == 8 else g


def _finish(x, omla, ocmp, oslc, owin, gn, ga, gb, state_rows, wts, *, tm, period, full_g, latent):
    n, d = x.shape
    dff = wts['w_gate'].shape[1]
    row = lambda i: (i, 0)
    c2 = lambda i: (0, 0)
    acts = [x, omla, ocmp, oslc, owin, gn, ga, gb] + (list(state_rows) if state_rows is not None else [])
    consts = [wts['gate_expand']] + ([wts['w_uvbd']] if latent else []) + [
        wts['w_proj_mla'], wts['w_proj_nsa'], wts['w_out'], wts['norm2_g'],
        wts['w_gate'], wts['w_up'], wts['conv_w'], wts['conv_b'], wts['w_down'], wts['norm_f_g']]
    ins = acts + consts
    in_specs = [pl.BlockSpec((tm, a.shape[1]), row) for a in acts] + [pl.BlockSpec(a.shape, c2) for a in consts]
    g_rows = n if full_g else (n // tm) * 8
    g_blk = tm if full_g else 8
    kern = functools.partial(_finish_kernel, period=period, latent=latent, has_state=state_rows is not None)
    return pl.pallas_call(
        kern, grid=(n // tm,), in_specs=in_specs,
        out_specs=[pl.BlockSpec((tm, d), row), pl.BlockSpec((g_blk, dff), row)],
        out_shape=[jax.ShapeDtypeStruct((n, d), F32), jax.ShapeDtypeStruct((g_rows, dff), F32)],
        scratch_shapes=[pltpu.VMEM((8, dff), F32)],
        compiler_params=_cparams("arbitrary"), name="finish_full" if full_g else "finish_tiled",
    )(*ins)


def _swap_halves(w):
    hlf = w.shape[-1] // 2
    return jnp.concatenate([-w[..., hlf:], w[..., :hlf]], axis=-1)


def _prep_weights(p):
    d = p['w_in'].shape[0]
    q_rank, kv_rank = p['q_norm_g'].shape[-1], p['kv_norm_g'].shape[-1]
    sizes = [q_rank, kv_rank, MLA_D_ROPE, NSA_HEADS * NSA_DH] + [2 * NSA_GROUPS * NSA_DH] * 3 + [3 * NSA_HEADS, d, d]
    cuts = np.cumsum(sizes)[:-1].tolist()
    cq, ckv, kr, qn, kvc, kvs, kvw, gn, ga, gb = jnp.split(p['w_in'], cuts, axis=-1)
    assert _O_CKV == q_rank and _O_QN == q_rank + kv_rank
    lo, hi = MLA_D_NOPE, HEAD_PAD - MLA_D_NOPE - MLA_D_ROPE
    place = lambda w: jnp.pad(w, ((0, 0), (lo, hi)))
    gnp = jnp.pad(gn, ((0, 0), (0, LANE - gn.shape[1])))
    w_in = jnp.concatenate([cq, ckv, qn, kvc, kvs, kvw, ga, gb, place(kr), place(_swap_halves(kr)), gnp], axis=1)
    w = {'w_in': w_in.astype(BF16)}
    for k in ('norm1_g', 'q_norm_g', 'kv_norm_g', 'norm2_g', 'conv_b'):
        w[k] = p[k].reshape(1, -1)
    w['norm_f_g'] = p['norm_f_g'].reshape(1, -1)
    w['conv_w'] = jnp.pad(p['conv_w'], ((0, 8 - CONV_W), (0, 0)))
    uq = p['w_uq']
    hpad = ((0, 0), (0, 0), (0, HEAD_PAD - uq.shape[-1]))
    uq_a = jnp.pad(uq, hpad)
    uq_b = jnp.pad(jnp.concatenate([jnp.zeros_like(uq[..., :MLA_D_NOPE]), _swap_halves(uq[..., MLA_D_NOPE:])], -1), hpad)
    w['w_uq2'] = jnp.concatenate([uq_a.reshape(q_rank, -1), uq_b.reshape(q_rank, -1)], axis=1).astype(BF16)
    uk = p['w_uk']
    w['w_ukp'] = jnp.pad(uk, ((0, 0), (0, 0), (0, HEAD_PAD - MLA_D_NOPE))).reshape(kv_rank, -1).astype(BF16)
    w['w_ukT'] = jnp.pad(jnp.transpose(uk, (1, 2, 0)), ((0, 0), (0, HEAD_PAD - MLA_D_NOPE), (0, 0))).astype(BF16)
    w['w_uvf'] = p['w_uv'].reshape(kv_rank, -1).astype(BF16)
    eye_h = jnp.eye(MLA_HEADS, dtype=F32)
    w['w_uvbd'] = jnp.einsum('rhv,hk->hrkv', p['w_uv'], eye_h).reshape(MLA_HEADS * kv_rank, -1).astype(BF16)
    eye_g = jnp.eye(NSA_GROUPS, dtype=F32)
    for nm in ('k', 'v'):
        w1 = p['cmp_w1_' + nm].reshape(2, CMP_STRIDE, NSA_DH, -1)
        big = jnp.einsum('ajdh,gk->jgdakh', w1, eye_g)
        w['cmp_w1' + nm] = big.reshape(CHUNK_FEATS, -1).astype(BF16)
        pos = p['cmp_pos_' + nm].reshape(2, CMP_STRIDE, 1, NSA_DH)
        pos = jnp.broadcast_to(pos, (2, CMP_STRIDE, NSA_GROUPS, NSA_DH)).reshape(2, CHUNK_FEATS)
        w['cmp_pos' + nm] = jnp.pad(pos, ((0, 6), (0, 0))).astype(BF16)
        w2 = p['cmp_w2_' + nm]
        w['cmp_w2' + nm] = jnp.einsum('hd,gk->ghkd', w2, eye_g).reshape(NSA_GROUPS * w2.shape[0], -1).astype(BF16)
    ge = np.zeros((LANE, 3 * NSA_HEADS * NSA_DH), np.float32)
    for h in range(NSA_HEADS):
        for i in range(3):
            ge[h * 3 + i, i * NSA_HEADS * NSA_DH + h * NSA_DH:i * NSA_HEADS * NSA_DH + (h + 1) * NSA_DH] = 1.0
    w['gate_expand'] = jnp.asarray(ge)
    for k in ('w_proj_mla', 'w_proj_nsa', 'w_out', 'w_gate', 'w_up', 'w_down'):
        w[k] = p[k].astype(BF16)
    return w


def _rope_tables(pos):
    inv = ROPE_THETA ** (-jnp.arange(0, MLA_D_ROPE, 2, dtype=F32) / MLA_D_ROPE)
    ang = pos.astype(F32)[:, None] * inv[None, :]
    cos, sin = jnp.cos(ang), jnp.sin(ang)
    n = pos.shape[0]
    pad = jnp.zeros((n, HEAD_PAD - MLA_D_NOPE - MLA_D_ROPE), F32)
    cosq = jnp.concatenate([jnp.ones((n, MLA_D_NOPE), F32), cos, cos, pad], axis=1)
    sinq = jnp.concatenate([jnp.zeros((n, MLA_D_NOPE), F32), sin, sin, pad], axis=1)
    return cosq, sinq


_NSA_GROUPS_SPEC = tuple((tuple(range(g * NSA_HPG, (g + 1) * NSA_HPG)), g) for g in range(NSA_GROUPS))
_MLA_GROUPS_SPEC = tuple(((h,), h) for h in range(MLA_HEADS))


def _prompt(x_prompt, w):
    b, t, d = x_prompt.shape
    n = b * t
    x = x_prompt.reshape(n, d)
    tm = 256
    cosq, sinq = _rope_tables(jnp.arange(t, dtype=jnp.int32))
    (q_mla, k_mla, v_mla, ckv, krp, qn, kc, vc, ks, vs, kw, vw, kc_b, vc_b, ks_b, vs_b, kw_b, vw_b, gn, ga, gb) = _inproj(
        x, w, cosq, sinq, sample=False, tm=tm, tab_blocks=t // tm)
    o_mla = _flash(q_mla, k_mla, v_mla, None, batch=b, seq=t, tq=256, tk=512, groups=_MLA_GROUPS_SPEC, par=8,
                   dq=HEAD_PAD, dk=HEAD_PAD, dv=MLA_D_V, window=0, nsa=False, base2=True, name="mla_prompt")
    kcc, vcc = _compress_prompt(kc_b, vc_b, w, batch=b, seq=t)
    o_cmp, sel_bias = _cmp_prompt(qn, kcc, vcc, batch=b, seq=t, tq=128)
    o_slc = _flash(qn, ks_b, vs_b, sel_bias, batch=b, seq=t, tq=256, tk=512, groups=_NSA_GROUPS_SPEC, par=2,
                   dq=LANE, dk=LANE, dv=NSA_DH, window=0, nsa=True, base2=False, name="slc_prompt")
    o_win = _flash(qn, kw_b, vw_b, None, batch=b, seq=t, tq=256, tk=256, groups=_NSA_GROUPS_SPEC, par=2,
                   dq=LANE, dk=LANE, dv=NSA_DH, window=WINDOW, nsa=True, base2=False, name="win_prompt")
    dff = w['w_gate'].shape[1]
    y, gtail = _finish(x, o_mla, o_cmp, o_slc, o_win, gn, ga, gb, None, w, tm=tm, period=t, full_g=False,
                       latent=False)
    kv4 = lambda a: a.reshape(1, b, NSA_GROUPS, NSA_DH, a.shape[-1]).transpose(0, 1, 4, 2, 3)
    n_keep = min(WINDOW, t)
    kw, vw = kw[:, :, t - n_keep:], vw[:, :, t - n_keep:]
    conv_state = gtail.reshape(b, t // tm, 8, dff)[:, -1, 8 - (CONV_W - 1):, :]
    states = (ckv.reshape(1, b, t, -1), krp[:, MLA_D_NOPE:MLA_D_NOPE + MLA_D_ROPE].reshape(1, b, t, MLA_D_ROPE),
              kv4(kc), kv4(vc), kv4(ks), kv4(vs), kv4(kw), kv4(vw), conv_state[None])
    return y.reshape(b, t, d), states


def _page_specs(block, n_pages, pp):
    zeros = (0,) * (len(block) - 1)
    return [pl.BlockSpec(block, (lambda b, s, pt, k=k: (pt[b * n_pages + s * pp + k],) + zeros)) for k in range(pp)]


def _softmax_update(sc, v, m_scr, l_scr, acc_scr, v_transposed=False):
    m_old = m_scr[...]
    m_new = jnp.maximum(m_old, jnp.max(sc, axis=-1, keepdims=True))
    p = jnp.exp(sc - m_new)
    alpha = jnp.exp(m_old - m_new)
    l_scr[...] = alpha * l_scr[...] + jnp.sum(p, axis=-1, keepdims=True)
    pv = _dot_nt(p.astype(BF16), v) if v_transposed else _dot(p.astype(BF16), v)
    acc_scr[...] = alpha * acc_scr[...] + pv
    m_scr[...] = m_new


def _mla_decode_kernel(pt_ref, q_ref, knew_ref, *rest, pp, td, kv_rank):
    ckv_pages, kr_pages = rest[:pp], rest[pp:2 * pp]
    o_ref, kscr, krscr, m_scr, l_scr, acc_scr = rest[2 * pp:]
    s = pl.program_id(1)
    q = q_ref[...]
    rows = q.shape[0]

    @pl.when(s == 0)
    def _():
        m_scr[...] = jnp.full_like(m_scr, NEG)
        l_scr[...] = jnp.zeros_like(l_scr)
        acc_scr[...] = jnp.zeros_like(acc_scr)
        kn = knew_ref[...]
        trow = _vdiv(_iota((rows, 1), 0), MLA_HEADS)
        col = _iota((1, kn.shape[0]), 1)
        sc = jnp.where((col <= trow) & (col < td), _dot_nt(q, kn), NEG)
        _softmax_update(sc, kn[:, :kv_rank], m_scr, l_scr, acc_scr)

    for k in range(pp):
        kscr[k * PAGE_SIZE:(k + 1) * PAGE_SIZE, :] = ckv_pages[k][...].astype(BF16)
        krscr[:, k * PAGE_SIZE:(k + 1) * PAGE_SIZE] = kr_pages[k][...].astype(BF16)
    kt = kscr[...]
    sc = _dot_nt(q[:, :kv_rank], kt) + _dot(q[:, kv_rank:kv_rank + MLA_D_ROPE], krscr[...])
    _softmax_update(sc, kt, m_scr, l_scr, acc_scr)

    @pl.when(s == pl.num_programs(1) - 1)
    def _():
        o_ref[...] = acc_scr[...] / l_scr[...]


def _mla_decode(pt, qd, knew, ckv_pool, kr_pool, *, td, pp):
    bd, rows, qw = qd.shape
    n_pages = pt.shape[0] // bd
    kv_rank = ckv_pool.shape[-1]
    per_b = lambda b, s, pt: (b, 0, 0)
    in_specs = ([pl.BlockSpec((None, rows, qw), per_b), pl.BlockSpec((None,) + knew.shape[1:], per_b)]
                + _page_specs((None, PAGE_SIZE, kv_rank), n_pages, pp)
                + _page_specs((None, MLA_D_ROPE, PAGE_SIZE), n_pages, pp))
    kern = functools.partial(_mla_decode_kernel, pp=pp, td=td, kv_rank=kv_rank)
    return pl.pallas_call(
        kern,
        grid_spec=pltpu.PrefetchScalarGridSpec(
            num_scalar_prefetch=1, grid=(bd, n_pages // pp), in_specs=in_specs,
            out_specs=pl.BlockSpec((None, rows, kv_rank), per_b),
            scratch_shapes=[pltpu.VMEM((pp * PAGE_SIZE, kv_rank), BF16), pltpu.VMEM((MLA_D_ROPE, pp * PAGE_SIZE), BF16),
                            pltpu.VMEM((rows, 1), F32), pltpu.VMEM((rows, 1), F32), pltpu.VMEM((rows, kv_rank), F32)]),
        out_shape=jax.ShapeDtypeStruct((bd, rows, kv_rank), F32),
        compiler_params=_cparams("parallel", "arbitrary"), name="mla_decode",
    )(pt, qd, knew, *([ckv_pool] * pp), *([kr_pool] * pp))


def _alibi_rows(rows, td):
    r = _iota((rows, 1), 0)
    return _vdiv(r, td), _vmod(r, td)


def _slope_rows(hrow, g):
    slope = jnp.zeros(hrow.shape, F32)
    for hh in range(NSA_HPG):
        slope = jnp.where(hrow == hh, SLOPES[g * NSA_HPG + hh], slope)
    return slope


def _cmp_decode_kernel(pt_ref, q_ref, *rest, pp, td, past, n_cmp, n_sel_pad):
    k_pages, v_pages = rest[:pp], rest[pp:2 * pp]
    (w1k_ref, w1v_ref, pk_ref, pv_ref, w2k_ref, w2v_ref, o_ref, score_ref, kp_scr, vp_scr, yk_scr, yv_scr) = rest[2 * pp:]
    s = pl.program_id(1)
    cpp = PAGE_SIZE // CMP_STRIDE
    for k in range(pp):
        kp_scr[k * PAGE_SIZE:(k + 1) * PAGE_SIZE, :] = k_pages[k][...].T
        vp_scr[k * PAGE_SIZE:(k + 1) * PAGE_SIZE, :] = v_pages[k][...].T
    chunk_rows = lambda scr: jnp.concatenate(
        [scr[pl.ds(j, pp * cpp, stride=CMP_STRIDE), :] for j in range(CMP_STRIDE)], axis=1).astype(BF16)
    xk = chunk_rows(kp_scr)
    xv = chunk_rows(vp_scr)
    r0 = pl.multiple_of(s * (pp * cpp), pp * cpp)
    yk_scr[pl.ds(r0, pp * cpp), :] = _dot(xk, w1k_ref[...])
    yv_scr[pl.ds(r0, pp * cpp), :] = _dot(xv, w1v_ref[...])

    @pl.when(s == pl.num_programs(1) - 1)
    def _():
        kcc = _compress_finish(yk_scr[...], _dot(pk_ref[...], w1k_ref[...]), w2k_ref).astype(BF16)
        vcc = _compress_finish(yv_scr[...], _dot(pv_ref[...], w1v_ref[...]), w2v_ref).astype(BF16)
        ncp = kcc.shape[0]
        rows = NSA_HPG * td
        hrow, trow = _alibi_rows(rows, td)
        cidx = _iota((1, ncp), 1)
        dist = (past + trow) - (cidx * CMP_STRIDE + CMP_BLOCK - 1)
        mask = (dist >= 0) & (cidx < n_cmp)
        distf = dist.astype(F32)
        tsum = (_vmod(_iota((8, rows), 1), td) == _iota((8, rows), 0)).astype(F32)
        ov = _overlap(ncp, n_sel_pad)
        t8 = _iota((8, 1), 0)
        cur = _vdiv(past + t8, SLC_BLOCK)
        jj = _iota((8, n_sel_pad), 1)
        for g in range(NSA_GROUPS):
            sc = _dot_nt(q_ref[g], kcc) - _slope_rows(hrow, g) * distf
            sc = jnp.where(mask, sc, NEG)
            m = jnp.max(sc, axis=-1, keepdims=True)
            p = jnp.where(mask, jnp.exp(sc - m), 0.0)
            l = jnp.sum(p, axis=-1, keepdims=True)
            p = p / jnp.where(l > 0.0, l, 1.0)
            o_ref[g] = _dot(p.astype(BF16), vcc)
            imp = _dot_exact(tsum, p)
            score_ref[g] = _force_scores(_dot_exact(imp, ov), cur, jj)


def _cmp_decode(pt, qd, k_pool, v_pool, wts, *, td, pp, past, n_sel_pad):
    bd = qd.shape[0]
    n_pages = pt.shape[0] // bd
    cpp = PAGE_SIZE // CMP_STRIDE
    nch = n_pages * cpp
    n_cmp = (past + td) // CMP_STRIDE - CMP_BLOCK // CMP_STRIDE + 1
    assert (past + td) // CMP_STRIDE == nch, "new rows must not complete a chunk"
    per_b = lambda b, s, pt: (b, 0, 0, 0)
    c2 = lambda b, s, pt: (0, 0)
    rows = qd.shape[2]
    in_specs = ([pl.BlockSpec((None,) + qd.shape[1:], per_b)]
                + _page_specs((None, LANE, PAGE_SIZE), n_pages, pp) + _page_specs((None, LANE, PAGE_SIZE), n_pages, pp)
                + [pl.BlockSpec((CHUNK_FEATS, 2 * LANE), c2) for _ in range(2)]
                + [pl.BlockSpec((8, CHUNK_FEATS), c2) for _ in range(2)]
                + [pl.BlockSpec((LANE, LANE), c2) for _ in range(2)])
    kern = functools.partial(_cmp_decode_kernel, pp=pp, td=td, past=past, n_cmp=n_cmp, n_sel_pad=n_sel_pad)
    return pl.pallas_call(
        kern,
        grid_spec=pltpu.PrefetchScalarGridSpec(
            num_scalar_prefetch=1, grid=(bd, n_pages // pp), in_specs=in_specs,
            out_specs=[pl.BlockSpec((None, NSA_GROUPS, rows, LANE), per_b),
                       pl.BlockSpec((None, NSA_GROUPS, 8, n_sel_pad), per_b)],
            scratch_shapes=[pltpu.VMEM((pp * PAGE_SIZE, LANE), F32) for _ in range(2)]
            + [pltpu.VMEM((nch, 2 * LANE), F32) for _ in range(2)]),
        out_shape=[jax.ShapeDtypeStruct((bd, NSA_GROUPS, rows, LANE), F32),
                   jax.ShapeDtypeStruct((bd, NSA_GROUPS, 8, n_sel_pad), F32)],
        compiler_params=_cparams("parallel", "arbitrary"), name="cmp_decode",
    )(pt, qd, *([k_pool] * pp), *([v_pool] * pp), wts['cmp_w1k'], wts['cmp_w1v'], wts['cmp_posk'], wts['cmp_posv'],
      wts['cmp_w2k'], wts['cmp_w2v'])


def _rank_kernel(score_ref, cur_ref, sel_ref, *, n_sel):
    sc = score_ref[...]
    jj = _iota(sc.shape, 0)

    def body(i, rank):
        ri = score_ref[pl.ds(i, 1), :]
        beats = (ri > sc) | ((ri == sc) & (i < jj))
        return rank + beats.astype(F32)

    rank = lax.fori_loop(0, n_sel, body, jnp.zeros(sc.shape, F32))
    sel = (rank < float(min(SLC_TOP_N, n_sel))) & (jj <= cur_ref[...])
    sel_ref[...] = sel.astype(F32)


def _rank(score_t, cur, *, n_sel):
    full = lambda a: pl.BlockSpec(a.shape, lambda: (0,) * a.ndim)
    return pl.pallas_call(
        functools.partial(_rank_kernel, n_sel=n_sel), in_specs=[full(score_t), full(cur)],
        out_specs=full(score_t), out_shape=jax.ShapeDtypeStruct(score_t.shape, F32), name="rank_decode",
    )(score_t, cur)


def _slc_decode_kernel(pt_ref, q_ref, sel_ref, selnew_ref, knew_ref, vnew_ref, *rest, pp, td, past):
    k_pages, v_pages = rest[:pp], rest[pp:2 * pp]
    o_ref, kscr, vscr, m_scr, l_scr, acc_scr = rest[2 * pp:]
    s = pl.program_id(1)
    rows = q_ref.shape[1]
    hrow, trow = _alibi_rows(rows, td)
    tk = pp * PAGE_SIZE
    nblk = tk // SLC_BLOCK

    @pl.when(s == 0)
    def _():
        m_scr[...] = jnp.full_like(m_scr, NEG)
        l_scr[...] = jnp.zeros_like(l_scr)
        acc_scr[...] = jnp.zeros_like(acc_scr)
        kn = knew_ref[...]
        vn = vnew_ref[...]
        col = _iota((1, kn.shape[0]), 1)
        dist = trow - col
        for g in range(NSA_GROUPS):
            sc = _dot_nt(q_ref[g], kn) - _slope_rows(hrow, g) * dist.astype(F32)
            mask = (dist >= 0) & (col < td) & (selnew_ref[g] > 0.5)
            _softmax_update(jnp.where(mask, sc, NEG), vn, m_scr.at[g], l_scr.at[g], acc_scr.at[g])

    for k in range(pp):
        kscr[:, k * PAGE_SIZE:(k + 1) * PAGE_SIZE] = k_pages[k][...].astype(BF16)
        vscr[:, k * PAGE_SIZE:(k + 1) * PAGE_SIZE] = v_pages[k][...].astype(BF16)
    kt = kscr[...]
    vt = vscr[...]
    kpos = s * tk + _iota((1, tk), 1)
    distf = ((past + trow) - kpos).astype(F32)
    expand = (_vdiv(_iota((nblk, tk), 1), SLC_BLOCK) == _iota((nblk, tk), 0)).astype(BF16)
    for g in range(NSA_GROUPS):
        sc = _dot(q_ref[g], kt) - _slope_rows(hrow, g) * distf
        selx = _dot(sel_ref[g], expand) > 0.5
        _softmax_update(jnp.where(selx, sc, NEG), vt, m_scr.at[g], l_scr.at[g], acc_scr.at[g], v_transposed=True)

    @pl.when(s == pl.num_programs(1) - 1)
    def _():
        o_ref[...] = acc_scr[...] / l_scr[...]


def _slc_decode(pt, qd, sel16, selnew, knew, vnew, k_pool, v_pool, *, td, pp, past):
    bd = qd.shape[0]
    rows = qd.shape[2]
    n_pages = pt.shape[0] // bd
    per_b = lambda b, s, pt: (b, 0, 0, 0)
    per_b3 = lambda b, s, pt: (b, 0, 0)
    in_specs = ([pl.BlockSpec((None,) + qd.shape[1:], per_b),
                 pl.BlockSpec((None, None) + sel16.shape[2:], lambda b, s, pt: (b, s, 0, 0, 0)),
                 pl.BlockSpec((None,) + selnew.shape[1:], per_b),
                 pl.BlockSpec((None,) + knew.shape[1:], per_b3), pl.BlockSpec((None,) + vnew.shape[1:], per_b3)]
                + _page_specs((None, LANE, PAGE_SIZE), n_pages, pp) + _page_specs((None, LANE, PAGE_SIZE), n_pages, pp))
    kern = functools.partial(_slc_decode_kernel, pp=pp, td=td, past=past)
    return pl.pallas_call(
        kern,
        grid_spec=pltpu.PrefetchScalarGridSpec(
            num_scalar_prefetch=1, grid=(bd, n_pages // pp), in_specs=in_specs,
            out_specs=pl.BlockSpec((None, NSA_GROUPS, rows, LANE), per_b),
            scratch_shapes=[pltpu.VMEM((LANE, pp * PAGE_SIZE), BF16) for _ in range(2)]
            + [pltpu.VMEM((NSA_GROUPS, rows, 1), F32) for _ in range(2)] + [pltpu.VMEM((NSA_GROUPS, rows, LANE), F32)]),
        out_shape=jax.ShapeDtypeStruct((bd, NSA_GROUPS, rows, LANE), F32),
        compiler_params=_cparams("parallel", "arbitrary"), name="slc_decode",
    )(pt, qd, sel16, selnew, knew, vnew, *([k_pool] * pp), *([v_pool] * pp))


def _win_decode_kernel(q_ref, kst_ref, vst_ref, knew_ref, vnew_ref, o_ref, *, td, past):
    rows = q_ref.shape[1]
    hrow, trow = _alibi_rows(rows, td)
    nbuf = kst_ref.shape[1]
    kst = kst_ref[...].astype(BF16)
    vst = vst_ref[...].astype(BF16)
    kn = knew_ref[...]
    vn = vnew_ref[...]
    kpos = past - nbuf + _iota((1, nbuf), 1)
    d1 = (past + trow) - kpos
    m1 = (d1 >= 0) & (d1 < WINDOW) & (kpos >= 0)
    col = _iota((1, kn.shape[0]), 1)
    d2 = trow - col
    m2 = (d2 >= 0) & (d2 < WINDOW) & (col < td)
    for g in range(NSA_GROUPS):
        slope = _slope_rows(hrow, g)
        s1 = jnp.where(m1, _dot(q_ref[g], kst) - slope * d1.astype(F32), NEG)
        s2 = jnp.where(m2, _dot_nt(q_ref[g], kn) - slope * d2.astype(F32), NEG)
        m = jnp.maximum(jnp.max(s1, axis=-1, keepdims=True), jnp.max(s2, axis=-1, keepdims=True))
        p1 = jnp.exp(s1 - m)
        p2 = jnp.exp(s2 - m)
        l = jnp.sum(p1, axis=-1, keepdims=True) + jnp.sum(p2, axis=-1, keepdims=True)
        o_ref[g] = (_dot_nt(p1.astype(BF16), vst) + _dot(p2.astype(BF16), vn)) / l


def _win_decode(qd, kst, vst, knew, vnew, *, td, past):
    bd = qd.shape[0]
    rows = qd.shape[2]
    per_b = lambda b: (b, 0, 0, 0)
    per_b3 = lambda b: (b, 0, 0)
    blk3 = lambda a: pl.BlockSpec((None,) + a.shape[1:], per_b3)
    return pl.pallas_call(
        functools.partial(_win_decode_kernel, td=td, past=past), grid=(bd,),
        in_specs=[pl.BlockSpec((None,) + qd.shape[1:], per_b), blk3(kst), blk3(vst), blk3(knew), blk3(vnew)],
        out_specs=pl.BlockSpec((None, NSA_GROUPS, rows, LANE), per_b),
        out_shape=jax.ShapeDtypeStruct((bd, NSA_GROUPS, rows, LANE), F32),
        compiler_params=_cparams("parallel"), name="win_decode",
    )(qd, kst, vst, knew, vnew)


def _pad_rows(a, rows):
    return jnp.pad(a, ((0, 0), (0, rows - a.shape[1]), (0, 0)))


def _sample(x_sample, caches, page_table, w):
    (c_ckv, c_krope, c_cmp_k, c_cmp_v, c_slc_k, c_slc_v, s_win_k, s_win_v, s_conv) = caches
    bd, td, d = x_sample.shape
    n = bd * td
    n_pages = page_table.shape[1]
    past = n_pages * PAGE_SIZE
    n_pool = c_ckv.shape[0]
    kv_rank = c_ckv.shape[-1]
    x = x_sample.reshape(n, d)
    pos = past + jnp.arange(td, dtype=jnp.int32)
    cosq, sinq = _rope_tables(jnp.tile(pos, bd))
    (qrot, qabs, ckv, krp, qn, kc, vc, ks, vs, kw, vw, gn, ga, gb) = _inproj(
        x, w, cosq, sinq, sample=True, tm=n, tab_blocks=1)
    pt = page_table.reshape(-1)
    krope = krp[:, MLA_D_NOPE:MLA_D_NOPE + MLA_D_ROPE]
    qrope = qrot.reshape(bd, td, MLA_HEADS, HEAD_PAD)[..., MLA_D_NOPE:MLA_D_NOPE + MLA_D_ROPE]
    qpad = LANE - MLA_D_ROPE
    qd_mla = jnp.concatenate([qabs.reshape(bd, td, MLA_HEADS, kv_rank), qrope,
                              jnp.zeros((bd, td, MLA_HEADS, qpad), F32)], axis=-1)
    qd_mla = qd_mla.reshape(bd, td * MLA_HEADS, kv_rank + LANE).astype(BF16)
    knew = jnp.concatenate([ckv, krope, jnp.zeros((n, qpad), F32)], axis=-1).reshape(bd, td, -1)
    knew = _pad_rows(knew, 8).astype(BF16)
    pos_minor = lambda c: jnp.moveaxis(c, 1, -1).reshape(c.shape[0], -1, c.shape[1])
    o_lat = _mla_decode(pt, qd_mla, knew, c_ckv, pos_minor(c_krope), td=td, pp=min(16, n_pages))
    qg = qn.astype(F32).reshape(bd, td, NSA_GROUPS, NSA_HPG, NSA_DH).transpose(0, 2, 3, 1, 4)
    qg = qg.reshape(bd, NSA_GROUPS, NSA_HPG * td, NSA_DH)
    lane_g = (jnp.arange(LANE) // NSA_DH)[None, :] == jnp.arange(NSA_GROUPS)[:, None]
    qd = jnp.where(lane_g[None, :, None, :], jnp.tile(qg, (1, 1, 1, NSA_GROUPS)), 0.0).astype(BF16)
    n_sel = -(-(past + td) // SLC_BLOCK)
    n_sel_pad = -(-n_sel // LANE) * LANE
    o_cmp, score = _cmp_decode(pt, qd, pos_minor(c_cmp_k), pos_minor(c_cmp_v),
                               w, td=td, pp=min(32, n_pages), past=past, n_sel_pad=n_sel_pad)
    cur = jnp.broadcast_to(((past + jnp.arange(8)) // SLC_BLOCK).astype(jnp.int32), (bd, NSA_GROUPS, 8)).reshape(1, -1)
    sel_t = _rank(score.reshape(-1, n_sel_pad).T, cur, n_sel=n_sel)
    sel = sel_t.T.reshape(bd, NSA_GROUPS, 8, n_sel_pad)[:, :, :td]
    pp_slc = min(16, n_pages)
    bps = pp_slc * PAGE_SIZE // SLC_BLOCK
    n_past_blk = past // SLC_BLOCK
    sel_past = sel[..., :n_past_blk].reshape(bd, NSA_GROUPS, td, n_past_blk // bps, bps).transpose(0, 3, 1, 2, 4)
    sel16 = jnp.tile(sel_past, (1, 1, 1, NSA_HPG, 1)).astype(BF16)
    selnew = jnp.tile(jnp.broadcast_to(sel[..., n_past_blk:n_past_blk + 1], (bd, NSA_GROUPS, td, 8)), (1, 1, NSA_HPG, 1))
    new8 = lambda a: _pad_rows(a.reshape(bd, td, LANE), 8).astype(BF16)
    o_slc = _slc_decode(pt, qd, sel16, selnew, new8(ks), new8(vs), pos_minor(c_slc_k), pos_minor(c_slc_v),
                        td=td, pp=pp_slc, past=past)
    nbuf = s_win_k.shape[1]
    o_win = _win_decode(qd, pos_minor(s_win_k), pos_minor(s_win_v), new8(kw), new8(vw),
                        td=td, past=past)

    def heads_out(o):
        o = o.reshape(bd, NSA_GROUPS, NSA_HPG, td, NSA_GROUPS, NSA_DH)
        o = jnp.stack([o[:, g, :, :, g] for g in range(NSA_GROUPS)], axis=1)
        return o.transpose(0, 3, 1, 2, 4).reshape(n, NSA_HEADS * NSA_DH)

    dff = w['w_gate'].shape[1]
    prev1 = jnp.zeros((bd, td, dff), F32).at[:, 0].set(s_conv[:, 1])
    prev2 = jnp.zeros((bd, td, dff), F32).at[:, 0].set(s_conv[:, 0]).at[:, 1].set(s_conv[:, 1])
    y, g = _finish(x, o_lat.reshape(n, MLA_HEADS * kv_rank), heads_out(o_cmp), heads_out(o_slc), heads_out(o_win),
                   gn, ga, gb, (prev1.reshape(n, dff), prev2.reshape(n, dff)), w, tm=n, period=td, full_g=True,
                   latent=True)
    kv4 = lambda a: a.reshape(1, bd, td, NSA_GROUPS, NSA_DH)
    win = lambda st, new: jnp.concatenate([st, new.reshape(bd, td, NSA_GROUPS, NSA_DH)], axis=1)[None, :, -nbuf:]
    conv_state = jnp.concatenate([s_conv, g.reshape(bd, td, dff)], axis=1)[None, :, -(CONV_W - 1):]
    states = (ckv.reshape(1, bd, td, kv_rank), krope.reshape(1, bd, td, MLA_D_ROPE), kv4(kc), kv4(vc), kv4(ks), kv4(vs),
              win(s_win_k, kw), win(s_win_v, vw), conv_state)
    return y.reshape(bd, td, d), states


def kernel(x_prompt, x_sample, cache_mla_ckv, cache_mla_krope, cache_nsa_cmp_k, cache_nsa_cmp_v, cache_nsa_slc_k, cache_nsa_slc_v, state_win_k, state_win_v, state_ffn_conv, page_table, norm1_g, w_in, q_norm_g, kv_norm_g, w_uq, w_uk, w_uv, cmp_pos_k, cmp_w1_k, cmp_w2_k, cmp_pos_v, cmp_w1_v, cmp_w2_v, w_proj_mla, w_proj_nsa, w_out, norm2_g, w_gate, w_up, conv_w, conv_b, w_down, norm_f_g):
    assert norm1_g.shape[0] == 1, "single-layer trunk"
    p = dict(norm1_g=norm1_g[0], w_in=w_in[0], q_norm_g=q_norm_g[0], kv_norm_g=kv_norm_g[0], w_uq=w_uq[0],
             w_uk=w_uk[0], w_uv=w_uv[0], cmp_pos_k=cmp_pos_k[0], cmp_w1_k=cmp_w1_k[0], cmp_w2_k=cmp_w2_k[0],
             cmp_pos_v=cmp_pos_v[0], cmp_w1_v=cmp_w1_v[0], cmp_w2_v=cmp_w2_v[0], w_proj_mla=w_proj_mla[0],
             w_proj_nsa=w_proj_nsa[0], w_out=w_out[0], norm2_g=norm2_g[0], w_gate=w_gate[0], w_up=w_up[0],
             conv_w=conv_w[0], conv_b=conv_b[0], w_down=w_down[0], norm_f_g=norm_f_g)
    w = _prep_weights(p)
    y_p, ps = _prompt(x_prompt, w)
    caches = (cache_mla_ckv[0], cache_mla_krope[0], cache_nsa_cmp_k[0], cache_nsa_cmp_v[0], cache_nsa_slc_k[0],
              cache_nsa_slc_v[0], state_win_k[0], state_win_v[0], state_ffn_conv[0])
    y_s, ss = _sample(x_sample, caches, page_table, w)
    out = [y_p, y_s]
    for a, b in zip(ps, ss):
        out += [a, b]
    return tuple(out)
```

```python
import functools

import numpy as np
import jax
import jax.numpy as jnp
from jax import lax
from jax.experimental import pallas as pl
from jax.experimental.pallas import tpu as pltpu

MLA_HEADS = 8
MLA_D_NOPE = 64
MLA_D_ROPE = 32
MLA_D_V = 64
ROPE_THETA = 10000.0
MLA_SCALE = (MLA_D_NOPE + MLA_D_ROPE) ** -0.5
NSA_HEADS = 8
NSA_GROUPS = 2
NSA_HPG = NSA_HEADS // NSA_GROUPS
NSA_DH = 64
NSA_SCALE = NSA_DH ** -0.5
CMP_BLOCK = 32
CMP_STRIDE = 16
SLC_BLOCK = 64
SLC_TOP_N = 16
WINDOW = 512
CONV_W = 3
PAGE_SIZE = 128
EPS = 1e-6
NEG = -1e30
FORCE = 1e9

LOG2E = 1.4426950408889634
LANE = 128
AUX_POS_HI, AUX_POS_LO, AUX_BLK0 = 64, 65, 72
HEAD_PAD = 128
CHUNK_FEATS = CMP_STRIDE * NSA_GROUPS * NSA_DH
SLOPES = tuple(float(2.0 ** (-8.0 * (h + 1) / NSA_HEADS)) for h in range(NSA_HEADS))
VMEM_LIMIT = 56 * 1024 * 1024

F32 = jnp.float32
BF16 = jnp.bfloat16
_NT = (((1,), (1,)), ((), ()))


def _cparams(*sem):
    return pltpu.CompilerParams(dimension_semantics=sem, vmem_limit_bytes=VMEM_LIMIT)


def _rms(x, g):
    return x * lax.rsqrt(jnp.mean(x * x, axis=-1, keepdims=True) + EPS) * g


def _dot(a, b):
    return jnp.dot(a, b, preferred_element_type=F32)


def _dot_nt(a, b):
    return lax.dot_general(a, b, _NT, preferred_element_type=F32)


def _dot_exact(a, b):
    return jnp.dot(a, b, preferred_element_type=F32, precision=lax.Precision.HIGHEST)


def _iota(shape, dim):
    return lax.broadcasted_iota(jnp.int32, shape, dim)


def _log2(n):
    assert n > 0 and n & (n - 1) == 0, n
    return n.bit_length() - 1


def _vdiv(x, n):
    return lax.shift_right_logical(x, jnp.full(x.shape, _log2(n), jnp.int32))


def _vmod(x, n):
    assert n & (n - 1) == 0, n
    return x & (n - 1)


_O_CQ, _O_CKV, _O_QN, _O_K6, _O_GA = 0, 384, 640, 1152, 1920


def _inproj_kernel(x_ref, g1_ref, w_ref, qg_ref, kvg_ref, wuq_ref, wk_ref, wv_ref, cos_ref, sin_ref,
                   *outs, sample, q_rank, kv_rank, d_model, tiles_per_seq, q_scale):
    o_gb = _O_GA + d_model
    o_kr = o_gb + d_model
    x = x_ref[...]
    hn = _rms(x, g1_ref[...])
    y = _dot(hn.astype(BF16), w_ref[...])
    cosq = cos_ref[...]
    sinq = sin_ref[...]
    nq = MLA_HEADS * HEAD_PAD
    cqn = _rms(y[:, _O_CQ:_O_CQ + q_rank], qg_ref[...])
    q2 = _dot(cqn.astype(BF16), wuq_ref[...])
    ckv = _rms(y[:, _O_CKV:_O_CKV + kv_rank], kvg_ref[...])
    kr = y[:, o_kr:o_kr + LANE] * cosq + y[:, o_kr + LANE:o_kr + 2 * LANE] * sinq
    ckv_b = ckv.astype(BF16)
    it = iter(outs)
    if sample:
        qrot_ref, qabs_ref = next(it), next(it)
        for h in range(MLA_HEADS):
            sl = slice(h * HEAD_PAD, (h + 1) * HEAD_PAD)
            qh = (q2[:, sl] * cosq + q2[:, nq + h * HEAD_PAD:nq + (h + 1) * HEAD_PAD] * sinq) * q_scale
            qrot_ref[:, sl] = qh
            qabs_ref[:, h * kv_rank:(h + 1) * kv_rank] = _dot(qh.astype(BF16), wk_ref[h])
    else:
        q_ref, k_ref, v_ref = next(it), next(it), next(it)
        knp = _dot(ckv_b, wk_ref[...])
        for h in range(MLA_HEADS):
            sl = slice(h * HEAD_PAD, (h + 1) * HEAD_PAD)
            qh = (q2[:, sl] * cosq + q2[:, nq + h * HEAD_PAD:nq + (h + 1) * HEAD_PAD] * sinq) * q_scale
            q_ref[:, sl] = qh.astype(BF16)
            k_ref[:, sl] = (knp[:, sl] + kr).astype(BF16)
        v_ref[...] = _dot(ckv_b, wv_ref[...]).astype(BF16)
    ckv_ref, kr_ref, qn_ref = next(it), next(it), next(it)
    ckv_ref[...] = ckv
    kr_ref[...] = kr
    yq = y[:, _O_QN:_O_QN + NSA_HEADS * NSA_DH] * NSA_SCALE
    if sample:
        qn_ref[...] = yq.astype(BF16)
    else:
        lane_q = _iota((x.shape[0], LANE), 1)
        for h in range(NSA_HEADS):
            pair = yq[:, (h // 2) * LANE:(h // 2 + 1) * LANE]
            if h % 2:
                pair = pltpu.roll(pair, NSA_DH, 1)
            aux_q = jnp.where((lane_q == AUX_POS_HI) | (lane_q == AUX_POS_LO), SLOPES[h], 0.0)
            qn_ref[:, h * LANE:(h + 1) * LANE] = jnp.where(lane_q < NSA_DH, pair, aux_q).astype(BF16)
    for j in range(6):
        yj = y[:, _O_K6 + j * LANE:_O_K6 + (j + 1) * LANE]
        if sample:
            next(it)[...] = yj
        else:
            next(it)[...] = yj.T
    if not sample:
        tm = x.shape[0]
        pos = (pl.program_id(0) % tiles_per_seq) * tm + _iota((tm, 1), 0)
        lane = _iota((tm, LANE), 1)
        onehot = ((lane >= AUX_BLK0) & (_vdiv(pos, SLC_BLOCK) == lane - AUX_BLK0)).astype(F32)
        aux = jnp.where(lane == AUX_POS_HI, (pos - _vmod(pos, 256)).astype(F32),
                        jnp.where(lane == AUX_POS_LO, _vmod(pos, 256).astype(F32), onehot))
        for j in range(6):
            yj = y[:, _O_K6 + j * LANE:_O_K6 + (j + 1) * LANE]
            ref = next(it)
            if j in (2, 4):
                ref[:, :LANE] = jnp.where(lane < NSA_DH, yj, aux).astype(BF16)
                ref[:, LANE:] = jnp.where(lane < NSA_DH, pltpu.roll(yj, NSA_DH, 1), aux).astype(BF16)
            else:
                ref[...] = yj.astype(BF16)
    gn_ref, ga_ref, gb_ref = next(it), next(it), next(it)
    gn_ref[...] = jax.nn.sigmoid(y[:, o_kr + 2 * LANE:o_kr + 3 * LANE])
    ga_ref[...] = jax.nn.sigmoid(y[:, _O_GA:_O_GA + d_model])
    gb_ref[...] = jax.nn.sigmoid(y[:, o_gb:o_gb + d_model])


def _inproj(x, wts, cosq, sinq, *, sample, tm, tab_blocks):
    n, d = x.shape
    q_rank, kv_rank = wts['q_norm_g'].shape[1], wts['kv_norm_g'].shape[1]
    nq = MLA_HEADS * HEAD_PAD
    wk = wts['w_ukT'] if sample else wts['w_ukp']
    row = lambda i: (i, 0)
    const2 = lambda i: (0, 0)
    tab_map = (lambda i: (i % tab_blocks, 0))
    in_specs = [
        pl.BlockSpec((tm, d), row),
        pl.BlockSpec((1, d), const2),
        pl.BlockSpec(wts['w_in'].shape, const2),
        pl.BlockSpec((1, q_rank), const2),
        pl.BlockSpec((1, kv_rank), const2),
        pl.BlockSpec(wts['w_uq2'].shape, const2),
        pl.BlockSpec(wk.shape, (lambda i: (0, 0, 0)) if sample else const2),
        pl.BlockSpec(wts['w_uvf'].shape, const2),
        pl.BlockSpec((tm, LANE), tab_map),
        pl.BlockSpec((tm, LANE), tab_map),
    ]
    shapes = []
    if sample:
        shapes += [(nq, F32), (MLA_HEADS * kv_rank, F32)]
    else:
        shapes += [(nq, BF16), (nq, BF16), (MLA_HEADS * MLA_D_V, BF16)]
    shapes += [(kv_rank, F32), (LANE, F32), (NSA_HEADS * (NSA_DH if sample else LANE), BF16)]
    n_lead = len(shapes)
    shapes += [(LANE, F32)] * 6
    if not sample:
        shapes += [(LANE, BF16), (LANE, BF16), (2 * LANE, BF16), (LANE, BF16), (2 * LANE, BF16), (LANE, BF16)]
        assert AUX_BLK0 + -(-tab_blocks * tm // SLC_BLOCK) <= LANE, "block one-hot must fit the aux lanes"
    shapes += [(LANE, F32), (d, F32), (d, F32)]
    out_shape = [jax.ShapeDtypeStruct((n, w), dt) for w, dt in shapes]
    out_specs = [pl.BlockSpec((tm, w), row) for w, _ in shapes]
    if not sample:
        tpb = tab_blocks
        for j in range(n_lead, n_lead + 6):
            out_shape[j] = jax.ShapeDtypeStruct((n // (tpb * tm), LANE, tpb * tm), F32)
            out_specs[j] = pl.BlockSpec((None, LANE, tm), lambda i: (i // tpb, 0, i % tpb))
    q_scale = MLA_SCALE if sample else MLA_SCALE * LOG2E
    kern = functools.partial(_inproj_kernel, sample=sample, q_rank=q_rank, kv_rank=kv_rank, d_model=d,
                             tiles_per_seq=tab_blocks, q_scale=q_scale)
    return pl.pallas_call(
        kern, grid=(n // tm,), in_specs=in_specs, out_specs=out_specs, out_shape=out_shape,
        compiler_params=_cparams("parallel"), name="inproj_sample" if sample else "inproj_prompt",
    )(x, wts['norm1_g'], wts['w_in'], wts['q_norm_g'], wts['kv_norm_g'], wts['w_uq2'], wk, wts['w_uvf'],
      cosq, sinq)


def _stack_heads(q_ref, heads, width):
    parts = [q_ref[:, h * width:(h + 1) * width] for h in heads]
    return parts[0] if len(parts) == 1 else jnp.concatenate(parts, axis=0)


def _flash_kernel(*refs, tq, tk, groups, par, dq, dk, dv, window, has_bias, base2):
    if has_bias:
        q_ref, k_ref, v_ref, bias_ref, o_ref = refs
    else:
        q_ref, k_ref, v_ref, o_ref = refs
    q_start = pl.program_id(1) * tq
    n_hi = (q_start + tq - 1) // tk + 1
    hi_full = (q_start + 1) // tk
    if window:
        n_lo = jnp.maximum(q_start - (window - 1), 0) // tk
        lo_full = (jnp.maximum(q_start + tq - window, 0) + tk - 1) // tk
    else:
        n_lo, lo_full = 0, 0
    e1 = jnp.clip(lo_full, n_lo, n_hi)
    e2 = jnp.clip(hi_full, e1, n_hi)
    ex = jnp.exp2 if base2 else jnp.exp
    for c0 in range(0, len(groups), par):
        chunk = groups[c0:c0 + par]
        qs = []
        for heads, kcol in chunk:
            qg = _stack_heads(q_ref, heads, dq)
            if has_bias:
                bias = bias_ref[:, kcol * LANE:(kcol + 1) * LANE]
                qg = qg + jnp.concatenate([bias] * len(heads), axis=0)
            qs.append(qg)
        rows = qs[0].shape[0]
        qpos = q_start + (_iota((rows, 1), 0) & (tq - 1))

        def step(j, carry, masked):
            k0 = pl.multiple_of(j * tk, tk)
            if masked:
                dist = qpos - (k0 + _iota((1, tk), 1))
                mask = dist >= 0
                if window:
                    mask = mask & (dist < window)
            out = []
            for (heads, kcol), qg, (m, l, acc) in zip(chunk, qs, carry):
                kt = k_ref[pl.ds(k0, tk), kcol * dk:(kcol + 1) * dk]
                vt = v_ref[pl.ds(k0, tk), kcol * dv:(kcol + 1) * dv]
                s = _dot_nt(qg, kt)
                if masked:
                    s = jnp.where(mask, s, NEG)
                m_new = jnp.maximum(m, jnp.max(s, axis=-1, keepdims=True))
                p = ex(s - m_new)
                alpha = ex(m - m_new)
                l = alpha * l + jnp.sum(p, axis=-1, keepdims=True)
                acc = alpha * acc + _dot(p.astype(BF16), vt)
                out.append((m_new, l, acc))
            return tuple(out)

        carry = tuple((jnp.full((rows, 1), NEG, F32), jnp.zeros((rows, 1), F32), jnp.zeros((rows, dv), F32))
                      for _ in chunk)
        carry = lax.fori_loop(n_lo, e1, functools.partial(step, masked=True), carry)
        carry = lax.fori_loop(e1, e2, functools.partial(step, masked=False), carry)
        carry = lax.fori_loop(e2, n_hi, functools.partial(step, masked=True), carry)
        for (heads, kcol), (m, l, acc) in zip(chunk, carry):
            o = acc * (1.0 / l)
            for hh, h in enumerate(heads):
                o_ref[:, h * dv:(h + 1) * dv] = o[hh * tq:(hh + 1) * tq].astype(o_ref.dtype)


def _flash(q, k, v, bias, *, batch, seq, tq, tk, groups, par, dq, dk, dv, window, base2, name):
    n = q.shape[0]
    has_bias = bias is not None
    n_heads = sum(len(g[0]) for g in groups)
    qrow = lambda b, i: (b * (seq // tq) + i, 0)
    kv = lambda b, i: (b, 0)
    in_specs = [pl.BlockSpec((tq, q.shape[1]), qrow), pl.BlockSpec((seq, k.shape[1]), kv),
                pl.BlockSpec((seq, v.shape[1]), kv)]
    args = [q, k, v]
    if has_bias:
        in_specs.append(pl.BlockSpec((tq, bias.shape[1]), qrow))
        args.append(bias)
    kern = functools.partial(_flash_kernel, tq=tq, tk=tk, groups=groups, par=par, dq=dq, dk=dk, dv=dv, window=window,
                             has_bias=has_bias, base2=base2)
    return pl.pallas_call(
        kern, grid=(batch, seq // tq), in_specs=in_specs,
        out_specs=pl.BlockSpec((tq, n_heads * dv), qrow),
        out_shape=jax.ShapeDtypeStruct((n, n_heads * dv), BF16),
        compiler_params=_cparams("parallel", "arbitrary"), name=name,
    )(*args)


def _win_kernel(q_ref, k_ref, v_ref, o_ref, *, tq, groups, dv):
    i = pl.program_id(1)
    rq = _iota((tq, tq), 0)
    ck = _iota((tq, tq), 1)
    bias_own = jnp.where(ck <= rq, 0.0, NEG)
    bias_far = jnp.where(ck > rq, 0.0, NEG) + jnp.where(i >= 2, 0.0, NEG)
    bias_mid = jnp.where(i >= 1, 0.0, NEG)
    starts = (jnp.maximum(i - 2, 0) * tq, jnp.maximum(i - 1, 0) * tq, i * tq)
    for heads, kcol in groups:
        qg = _stack_heads(q_ref, heads, LANE)
        nh = len(heads)
        ss, vs = [], []
        for k0, bias in zip(starts, (bias_far, None, bias_own)):
            k0 = pl.multiple_of(k0, tq)
            s = _dot_nt(qg, k_ref[pl.ds(k0, tq), kcol * LANE:(kcol + 1) * LANE])
            s = s + (bias_mid if bias is None else jnp.concatenate([bias] * nh, axis=0))
            ss.append(s)
            vs.append(v_ref[pl.ds(k0, tq), kcol * dv:(kcol + 1) * dv])
        m = functools.reduce(jnp.maximum, [jnp.max(s, axis=-1, keepdims=True) for s in ss])
        ps = [jnp.exp(s - m) for s in ss]
        l = functools.reduce(jnp.add, [jnp.sum(p, axis=-1, keepdims=True) for p in ps])
        acc = functools.reduce(jnp.add, [_dot(p.astype(BF16), v) for p, v in zip(ps, vs)])
        o = acc * (1.0 / l)
        for hh, h in enumerate(heads):
            o_ref[:, h * dv:(h + 1) * dv] = o[hh * tq:(hh + 1) * tq].astype(o_ref.dtype)


def _win_prompt(q, k, v, *, batch, seq, tq, groups, dv):
    assert WINDOW == 2 * tq and seq % tq == 0
    n = q.shape[0]
    n_heads = sum(len(g[0]) for g in groups)
    qrow = lambda b, i: (b * (seq // tq) + i, 0)
    kv = lambda b, i: (b, 0)
    return pl.pallas_call(
        functools.partial(_win_kernel, tq=tq, groups=groups, dv=dv), grid=(batch, seq // tq),
        in_specs=[pl.BlockSpec((tq, q.shape[1]), qrow), pl.BlockSpec((seq, k.shape[1]), kv),
                  pl.BlockSpec((seq, v.shape[1]), kv)],
        out_specs=pl.BlockSpec((tq, n_heads * dv), qrow),
        out_shape=jax.ShapeDtypeStruct((n, n_heads * dv), BF16),
        compiler_params=_cparams("parallel", "arbitrary"), name="win_prompt",
    )(q, k, v)


def _compress_rows(xk, xv, w1k_ref, w1v_ref, posk_ref, posv_ref):
    yk = _dot(xk.astype(BF16), w1k_ref[...])
    yv = _dot(xv.astype(BF16), w1v_ref[...])
    return yk, yv


def _compress_finish(y, posy, w2_ref):
    rows = y.shape[0]
    a = y[:, :LANE]
    b = pltpu.roll(y[:, LANE:], rows - 1, 0)
    pos = posy[0:1, :LANE] + posy[1:2, LANE:]
    hid = jax.nn.gelu(a + b + pos)
    return _dot(hid.astype(BF16), w2_ref[...])


def _compress_prompt_kernel(xk_ref, xv_ref, w1k_ref, w1v_ref, pk_ref, pv_ref, w2k_ref, w2v_ref, ok_ref, ov_ref):
    yk, yv = _compress_rows(xk_ref[...], xv_ref[...], w1k_ref, w1v_ref, pk_ref, pv_ref)
    ok_ref[...] = _compress_finish(yk, _dot(pk_ref[...], w1k_ref[...]), w2k_ref).astype(BF16)
    ov_ref[...] = _compress_finish(yv, _dot(pv_ref[...], w1v_ref[...]), w2v_ref).astype(BF16)


def _compress_prompt(kc, vc, wts, *, batch, seq):
    nch = seq // CMP_STRIDE
    xk = kc.reshape(batch * nch, CHUNK_FEATS)
    xv = vc.reshape(batch * nch, CHUNK_FEATS)
    row = lambda b: (b, 0)
    c2 = lambda b: (0, 0)
    wspec = pl.BlockSpec((CHUNK_FEATS, 2 * LANE), c2)
    pspec = pl.BlockSpec((8, CHUNK_FEATS), c2)
    w2spec = pl.BlockSpec((LANE, LANE), c2)
    return pl.pallas_call(
        _compress_prompt_kernel, grid=(batch,),
        in_specs=[pl.BlockSpec((nch, CHUNK_FEATS), row), pl.BlockSpec((nch, CHUNK_FEATS), row),
                  wspec, wspec, pspec, pspec, w2spec, w2spec],
        out_specs=[pl.BlockSpec((nch, LANE), row)] * 2,
        out_shape=[jax.ShapeDtypeStruct((batch * nch, LANE), BF16)] * 2,
        compiler_params=_cparams("parallel"), name="compress_prompt",
    )(xk, xv, wts['cmp_w1k'], wts['cmp_w1v'], wts['cmp_posk'], wts['cmp_posv'], wts['cmp_w2k'], wts['cmp_w2v'])


def _overlap(n_rows, n_sel):
    c = _iota((n_rows, n_sel), 0) * CMP_STRIDE
    j = _iota((n_rows, n_sel), 1) * SLC_BLOCK
    return ((c < j + SLC_BLOCK) & (c + CMP_BLOCK > j)).astype(F32)


def _force_scores(score, cur, jj):
    forced = (jj == 0) | (jj == cur) | (jj == cur - 1)
    score = jnp.where(forced, FORCE, score)
    return jnp.where(jj <= cur, score, NEG)


def _cmp_prompt_kernel(q_ref, k_ref, v_ref, o_ref, bias_ref, *, tq, n_cmp, n_sel):
    q_start = pl.program_id(1) * tq
    ncp = k_ref.shape[0]
    rows = NSA_HPG * tq
    qpos = q_start + (_iota((rows, 1), 0) & (tq - 1))
    cidx = _iota((1, ncp), 1)
    dist = qpos - (cidx * CMP_STRIDE + CMP_BLOCK - 1)
    mask = (dist >= 0) & (cidx < n_cmp)
    distf = dist.astype(F32)
    hrow = _vdiv(_iota((rows, 1), 0), tq)
    nsp = -(-n_sel // 8) * 8
    cur = _vdiv(q_start + _iota((1, tq), 1), SLC_BLOCK)
    jj = _iota((nsp, tq), 0)
    cb = _iota((nsp, ncp), 1) * CMP_STRIDE
    jb = _iota((nsp, ncp), 0) * SLC_BLOCK
    ov_t = ((cb < jb + SLC_BLOCK) & (cb + CMP_BLOCK > jb)).astype(F32)
    for g in range(NSA_GROUPS):
        heads = range(g * NSA_HPG, (g + 1) * NSA_HPG)
        qg = jnp.concatenate([q_ref[:, h * LANE:h * LANE + NSA_DH] for h in heads], axis=0)
        slope = jnp.zeros((rows, 1), F32)
        for hh, h in enumerate(heads):
            slope = jnp.where(hrow == hh, SLOPES[h], slope)
        s = _dot_nt(qg, k_ref[:, g * NSA_DH:(g + 1) * NSA_DH]) - slope * distf
        s = jnp.where(mask, s, NEG)
        m = jnp.max(s, axis=-1, keepdims=True)
        p = jnp.where(mask, jnp.exp(s - m), 0.0)
        l = jnp.sum(p, axis=-1, keepdims=True)
        p = p / jnp.where(l > 0.0, l, 1.0)
        o = _dot(p.astype(BF16), v_ref[:, g * NSA_DH:(g + 1) * NSA_DH])
        imp = p[0:tq]
        for hh in range(1, NSA_HPG):
            imp = imp + p[hh * tq:(hh + 1) * tq]
            o_ref[:, (g * NSA_HPG + hh) * NSA_DH:(g * NSA_HPG + hh + 1) * NSA_DH] = o[hh * tq:(hh + 1) * tq].astype(BF16)
        o_ref[:, g * NSA_HPG * NSA_DH:(g * NSA_HPG + 1) * NSA_DH] = o[0:tq].astype(BF16)
        score = lax.dot_general(ov_t, imp, _NT, preferred_element_type=F32, precision=lax.Precision.HIGHEST)
        score = _force_scores(score, cur, jj)
        rank = jnp.zeros((nsp, tq), F32)
        for i in range(n_sel):
            ri = score[i:i + 1, :]
            beats = (ri > score) | ((ri == score) & (i < jj))
            rank = rank + beats.astype(F32)
        sel = (rank < float(min(SLC_TOP_N, n_sel))) & (jj <= cur)
        bias_t = jnp.where(sel | (jj >= n_sel), 0.0, NEG)
        bias_t = jnp.concatenate([jnp.zeros((AUX_BLK0, tq), F32), bias_t,
                                  jnp.zeros((LANE - AUX_BLK0 - nsp, tq), F32)], axis=0)
        bias_ref[:, g * LANE:(g + 1) * LANE] = bias_t.T.astype(BF16)


def _cmp_prompt(qn, kcc, vcc, *, batch, seq, tq):
    n = qn.shape[0]
    nch = seq // CMP_STRIDE
    n_cmp = nch - CMP_BLOCK // CMP_STRIDE + 1
    n_sel = -(-seq // SLC_BLOCK)
    qrow = lambda b, i: (b * (seq // tq) + i, 0)
    kv = lambda b, i: (b, 0)
    kern = functools.partial(_cmp_prompt_kernel, tq=tq, n_cmp=n_cmp, n_sel=n_sel)
    return pl.pallas_call(
        kern, grid=(batch, seq // tq),
        in_specs=[pl.BlockSpec((tq, qn.shape[1]), qrow), pl.BlockSpec((nch, LANE), kv), pl.BlockSpec((nch, LANE), kv)],
        out_specs=[pl.BlockSpec((tq, NSA_HEADS * NSA_DH), qrow), pl.BlockSpec((tq, NSA_GROUPS * LANE), qrow)],
        out_shape=[jax.ShapeDtypeStruct((n, NSA_HEADS * NSA_DH), BF16),
                   jax.ShapeDtypeStruct((n, NSA_GROUPS * LANE), BF16)],
        compiler_params=_cparams("parallel", "arbitrary"), name="cmp_prompt",
    )(qn, kcc, vcc)


def _finish_kernel(*refs, period, latent, has_state):
    it = iter(refs)
    x_ref, omla_ref, ocmp_ref, oslc_ref, owin_ref, gn_ref, ga_ref, gb_ref = (next(it) for _ in range(8))
    prev1_ref, prev2_ref = (next(it), next(it)) if has_state else (None, None)
    gx_ref = next(it)
    wuv_ref = next(it) if latent else None
    (wpm_ref, wpn_ref, wo_ref, g2_ref, wg_ref, wu_ref, cw_ref, cb_ref, wd_ref, gf_ref,
     y_ref, gout_ref, carry_ref) = it
    tm = x_ref.shape[0]
    gn = gn_ref[...]
    gn_hi = gn.astype(BF16)
    gn_lo = (gn - gn_hi.astype(F32)).astype(BF16)
    gexp = _dot(gn_hi, gx_ref[...]) + _dot(gn_lo, gx_ref[...])
    w = NSA_HEADS * NSA_DH
    o_nsa = (gexp[:, 0:w] * ocmp_ref[...].astype(F32) + gexp[:, w:2 * w] * oslc_ref[...].astype(F32)
             + gexp[:, 2 * w:3 * w] * owin_ref[...].astype(F32))
    o_mla = omla_ref[...].astype(BF16)
    if latent:
        o_mla = _dot(o_mla, wuv_ref[...]).astype(BF16)
    merged = (ga_ref[...] * _dot(o_mla, wpm_ref[...])
              + gb_ref[...] * _dot(o_nsa.astype(BF16), wpn_ref[...]))
    x1 = x_ref[...] + _dot(merged.astype(BF16), wo_ref[...])
    h2 = _rms(x1, g2_ref[...]).astype(BF16)
    g = _dot(h2, wg_ref[...])
    u = _dot(h2, wu_ref[...])
    row = _iota((tm, 1), 0)
    i = pl.program_id(0)
    t = _vmod(i * tm + row, period)
    g1 = pltpu.roll(g, 1, 0)
    g2 = pltpu.roll(g, 2, 0)
    if period > tm:
        @pl.when(i == 0)
        def _():
            carry_ref[...] = jnp.zeros_like(carry_ref)
        c = carry_ref[...]
        g1 = jnp.where(row == 0, c[7:8], g1)
        g2 = jnp.where(row == 0, c[6:7], jnp.where(row == 1, c[7:8], g2))
        carry_ref[...] = g[tm - 8:tm]
    g1 = jnp.where(t >= 1, g1, prev1_ref[...] if has_state else 0.0)
    g2 = jnp.where(t >= 2, g2, prev2_ref[...] if has_state else 0.0)
    cw = cw_ref[...]
    conv = cb_ref[...] + cw[0:1] * g2 + cw[1:2] * g1 + cw[2:3] * g
    act = (jax.nn.silu(conv) * u).astype(BF16)
    x2 = x1 + _dot(act, wd_ref[...])
    y_ref[...] = _rms(x2, gf_ref[...])
    gout_ref[...] = g[tm - 8:tm] if gout_ref.shape[0] == 8 else g


def _finish(x, omla, ocmp, oslc, owin, gn, ga, gb, state_rows, wts, *, tm, period, full_g, latent):
    n, d = x.shape
    dff = wts['w_gate'].shape[1]
    row = lambda i: (i, 0)
    c2 = lambda i: (0, 0)
    acts = [x, omla, ocmp, oslc, owin, gn, ga, gb] + (list(state_rows) if state_rows is not None else [])
    consts = [wts['gate_expand']] + ([wts['w_uvbd']] if latent else []) + [
        wts['w_proj_mla'], wts['w_proj_nsa'], wts['w_out'], wts['norm2_g'],
        wts['w_gate'], wts['w_up'], wts['conv_w'], wts['conv_b'], wts['w_down'], wts['norm_f_g']]
    ins = acts + consts
    in_specs = [pl.BlockSpec((tm, a.shape[1]), row) for a in acts] + [pl.BlockSpec(a.shape, c2) for a in consts]
    g_rows = n if full_g else (n // tm) * 8
    g_blk = tm if full_g else 8
    kern = functools.partial(_finish_kernel, period=period, latent=latent, has_state=state_rows is not None)
    return pl.pallas_call(
        kern, grid=(n // tm,), in_specs=in_specs,
        out_specs=[pl.BlockSpec((tm, d), row), pl.BlockSpec((g_blk, dff), row)],
        out_shape=[jax.ShapeDtypeStruct((n, d), F32), jax.ShapeDtypeStruct((g_rows, dff), F32)],
        scratch_shapes=[pltpu.VMEM((8, dff), F32)],
        compiler_params=_cparams("arbitrary"), name="finish_full" if full_g else "finish_tiled",
    )(*ins)


def _swap_halves(w):
    hlf = w.shape[-1] // 2
    return jnp.concatenate([-w[..., hlf:], w[..., :hlf]], axis=-1)


def _prep_weights(p):
    d = p['w_in'].shape[0]
    q_rank, kv_rank = p['q_norm_g'].shape[-1], p['kv_norm_g'].shape[-1]
    sizes = [q_rank, kv_rank, MLA_D_ROPE, NSA_HEADS * NSA_DH] + [2 * NSA_GROUPS * NSA_DH] * 3 + [3 * NSA_HEADS, d, d]
    cuts = np.cumsum(sizes)[:-1].tolist()
    cq, ckv, kr, qn, kvc, kvs, kvw, gn, ga, gb = jnp.split(p['w_in'], cuts, axis=-1)
    assert _O_CKV == q_rank and _O_QN == q_rank + kv_rank
    lo, hi = MLA_D_NOPE, HEAD_PAD - MLA_D_NOPE - MLA_D_ROPE
    place = lambda w: jnp.pad(w, ((0, 0), (lo, hi)))
    gnp = jnp.pad(gn, ((0, 0), (0, LANE - gn.shape[1])))
    w_in = jnp.concatenate([cq, ckv, qn, kvc, kvs, kvw, ga, gb, place(kr), place(_swap_halves(kr)), gnp], axis=1)
    w = {'w_in': w_in.astype(BF16)}
    for k in ('norm1_g', 'q_norm_g', 'kv_norm_g', 'norm2_g', 'conv_b'):
        w[k] = p[k].reshape(1, -1)
    w['norm_f_g'] = p['norm_f_g'].reshape(1, -1)
    w['conv_w'] = jnp.pad(p['conv_w'], ((0, 8 - CONV_W), (0, 0)))
    uq = p['w_uq']
    hpad = ((0, 0), (0, 0), (0, HEAD_PAD - uq.shape[-1]))
    uq_a = jnp.pad(uq, hpad)
    uq_b = jnp.pad(jnp.concatenate([jnp.zeros_like(uq[..., :MLA_D_NOPE]), _swap_halves(uq[..., MLA_D_NOPE:])], -1), hpad)
    w['w_uq2'] = jnp.concatenate([uq_a.reshape(q_rank, -1), uq_b.reshape(q_rank, -1)], axis=1).astype(BF16)
    uk = p['w_uk']
    w['w_ukp'] = jnp.pad(uk, ((0, 0), (0, 0), (0, HEAD_PAD - MLA_D_NOPE))).reshape(kv_rank, -1).astype(BF16)
    w['w_ukT'] = jnp.pad(jnp.transpose(uk, (1, 2, 0)), ((0, 0), (0, HEAD_PAD - MLA_D_NOPE), (0, 0))).astype(BF16)
    w['w_uvf'] = p['w_uv'].reshape(kv_rank, -1).astype(BF16)
    eye_h = jnp.eye(MLA_HEADS, dtype=F32)
    w['w_uvbd'] = jnp.einsum('rhv,hk->hrkv', p['w_uv'], eye_h).reshape(MLA_HEADS * kv_rank, -1).astype(BF16)
    eye_g = jnp.eye(NSA_GROUPS, dtype=F32)
    for nm in ('k', 'v'):
        w1 = p['cmp_w1_' + nm].reshape(2, CMP_STRIDE, NSA_DH, -1)
        big = jnp.einsum('ajdh,gk->jgdakh', w1, eye_g)
        w['cmp_w1' + nm] = big.reshape(CHUNK_FEATS, -1).astype(BF16)
        pos = p['cmp_pos_' + nm].reshape(2, CMP_STRIDE, 1, NSA_DH)
        pos = jnp.broadcast_to(pos, (2, CMP_STRIDE, NSA_GROUPS, NSA_DH)).reshape(2, CHUNK_FEATS)
        w['cmp_pos' + nm] = jnp.pad(pos, ((0, 6), (0, 0))).astype(BF16)
        w2 = p['cmp_w2_' + nm]
        w['cmp_w2' + nm] = jnp.einsum('hd,gk->ghkd', w2, eye_g).reshape(NSA_GROUPS * w2.shape[0], -1).astype(BF16)
    ge = np.zeros((LANE, 3 * NSA_HEADS * NSA_DH), np.float32)
    for h in range(NSA_HEADS):
        for i in range(3):
            ge[h * 3 + i, i * NSA_HEADS * NSA_DH + h * NSA_DH:i * NSA_HEADS * NSA_DH + (h + 1) * NSA_DH] = 1.0
    w['gate_expand'] = jnp.asarray(ge)
    for k in ('w_proj_mla', 'w_proj_nsa', 'w_out', 'w_gate', 'w_up', 'w_down'):
        w[k] = p[k].astype(BF16)
    return w


def _rope_tables(pos):
    inv = ROPE_THETA ** (-jnp.arange(0, MLA_D_ROPE, 2, dtype=F32) / MLA_D_ROPE)
    ang = pos.astype(F32)[:, None] * inv[None, :]
    cos, sin = jnp.cos(ang), jnp.sin(ang)
    n = pos.shape[0]
    pad = jnp.zeros((n, HEAD_PAD - MLA_D_NOPE - MLA_D_ROPE), F32)
    cosq = jnp.concatenate([jnp.ones((n, MLA_D_NOPE), F32), cos, cos, pad], axis=1)
    sinq = jnp.concatenate([jnp.zeros((n, MLA_D_NOPE), F32), sin, sin, pad], axis=1)
    return cosq, sinq


_NSA_GROUPS_SPEC = tuple((tuple(range(g * NSA_HPG, (g + 1) * NSA_HPG)), g) for g in range(NSA_GROUPS))
_MLA_GROUPS_SPEC = tuple(((h,), h) for h in range(MLA_HEADS))


def _prompt(x_prompt, w):
    b, t, d = x_prompt.shape
    n = b * t
    x = x_prompt.reshape(n, d)
    tm = 256
    cosq, sinq = _rope_tables(jnp.arange(t, dtype=jnp.int32))
    (q_mla, k_mla, v_mla, ckv, krp, qn, kc, vc, ks, vs, kw, vw, kc_b, vc_b, ks_b, vs_b, kw_b, vw_b, gn, ga, gb) = _inproj(
        x, w, cosq, sinq, sample=False, tm=tm, tab_blocks=t // tm)
    o_mla = _flash(q_mla, k_mla, v_mla, None, batch=b, seq=t, tq=256, tk=512, groups=_MLA_GROUPS_SPEC, par=8,
                   dq=HEAD_PAD, dk=HEAD_PAD, dv=MLA_D_V, window=0, base2=True, name="mla_prompt")
    kcc, vcc = _compress_prompt(kc_b, vc_b, w, batch=b, seq=t)
    o_cmp, sel_bias = _cmp_prompt(qn, kcc, vcc, batch=b, seq=t, tq=128)
    o_slc = _flash(qn, ks_b, vs_b, sel_bias, batch=b, seq=t, tq=256, tk=512, groups=_NSA_GROUPS_SPEC, par=2,
                   dq=LANE, dk=LANE, dv=NSA_DH, window=0, base2=False, name="slc_prompt")
    o_win = _win_prompt(qn, kw_b, vw_b, batch=b, seq=t, tq=WINDOW // 2, groups=_NSA_GROUPS_SPEC, dv=NSA_DH)
    dff = w['w_gate'].shape[1]
    y, gtail = _finish(x, o_mla, o_cmp, o_slc, o_win, gn, ga, gb, None, w, tm=tm, period=t, full_g=False,
                       latent=False)
    kv4 = lambda a: a.reshape(1, b, NSA_GROUPS, NSA_DH, a.shape[-1]).transpose(0, 1, 4, 2, 3)
    n_keep = min(WINDOW, t)
    kw, vw = kw[:, :, t - n_keep:], vw[:, :, t - n_keep:]
    conv_state = gtail.reshape(b, t // tm, 8, dff)[:, -1, 8 - (CONV_W - 1):, :]
    states = (ckv.reshape(1, b, t, -1), krp[:, MLA_D_NOPE:MLA_D_NOPE + MLA_D_ROPE].reshape(1, b, t, MLA_D_ROPE),
              kv4(kc), kv4(vc), kv4(ks), kv4(vs), kv4(kw), kv4(vw), conv_state[None])
    return y.reshape(b, t, d), states


def _page_specs(block, n_pages, pp):
    zeros = (0,) * (len(block) - 1)
    return [pl.BlockSpec(block, (lambda b, s, pt, k=k: (pt[b * n_pages + s * pp + k],) + zeros)) for k in range(pp)]


def _softmax_update(sc, v, m_scr, l_scr, acc_scr, v_transposed=False):
    m_old = m_scr[...]
    m_new = jnp.maximum(m_old, jnp.max(sc, axis=-1, keepdims=True))
    p = jnp.exp(sc - m_new)
    alpha = jnp.exp(m_old - m_new)
    l_scr[...] = alpha * l_scr[...] + jnp.sum(p, axis=-1, keepdims=True)
    pv = _dot_nt(p.astype(BF16), v) if v_transposed else _dot(p.astype(BF16), v)
    acc_scr[...] = alpha * acc_scr[...] + pv
    m_scr[...] = m_new


def _mla_decode_kernel(pt_ref, q_ref, knew_ref, *rest, pp, td, kv_rank):
    ckv_pages, kr_pages = rest[:pp], rest[pp:2 * pp]
    o_ref, kscr, krscr, m_scr, l_scr, acc_scr = rest[2 * pp:]
    s = pl.program_id(1)
    q = q_ref[...]
    rows = q.shape[0]

    @pl.when(s == 0)
    def _():
        m_scr[...] = jnp.full_like(m_scr, NEG)
        l_scr[...] = jnp.zeros_like(l_scr)
        acc_scr[...] = jnp.zeros_like(acc_scr)
        kn = knew_ref[...]
        trow = _vdiv(_iota((rows, 1), 0), MLA_HEADS)
        col = _iota((1, kn.shape[0]), 1)
        sc = jnp.where((col <= trow) & (col < td), _dot_nt(q, kn), NEG)
        _softmax_update(sc, kn[:, :kv_rank], m_scr, l_scr, acc_scr)

    for k in range(pp):
        kscr[k * PAGE_SIZE:(k + 1) * PAGE_SIZE, :] = ckv_pages[k][...].astype(BF16)
        krscr[:, k * PAGE_SIZE:(k + 1) * PAGE_SIZE] = kr_pages[k][...].astype(BF16)
    kt = kscr[...]
    sc = _dot_nt(q[:, :kv_rank], kt) + _dot(q[:, kv_rank:kv_rank + MLA_D_ROPE], krscr[...])
    _softmax_update(sc, kt, m_scr, l_scr, acc_scr)

    @pl.when(s == pl.num_programs(1) - 1)
    def _():
        o_ref[...] = acc_scr[...] / l_scr[...]


def _mla_decode(pt, qd, knew, ckv_pool, kr_pool, *, td, pp):
    bd, rows, qw = qd.shape
    n_pages = pt.shape[0] // bd
    kv_rank = ckv_pool.shape[-1]
    per_b = lambda b, s, pt: (b, 0, 0)
    in_specs = ([pl.BlockSpec((None, rows, qw), per_b), pl.BlockSpec((None,) + knew.shape[1:], per_b)]
                + _page_specs((None, PAGE_SIZE, kv_rank), n_pages, pp)
                + _page_specs((None, MLA_D_ROPE, PAGE_SIZE), n_pages, pp))
    kern = functools.partial(_mla_decode_kernel, pp=pp, td=td, kv_rank=kv_rank)
    return pl.pallas_call(
        kern,
        grid_spec=pltpu.PrefetchScalarGridSpec(
            num_scalar_prefetch=1, grid=(bd, n_pages // pp), in_specs=in_specs,
            out_specs=pl.BlockSpec((None, rows, kv_rank), per_b),
            scratch_shapes=[pltpu.VMEM((pp * PAGE_SIZE, kv_rank), BF16), pltpu.VMEM((MLA_D_ROPE, pp * PAGE_SIZE), BF16),
                            pltpu.VMEM((rows, 1), F32), pltpu.VMEM((rows, 1), F32), pltpu.VMEM((rows, kv_rank), F32)]),
        out_shape=jax.ShapeDtypeStruct((bd, rows, kv_rank), F32),
        compiler_params=_cparams("parallel", "arbitrary"), name="mla_decode",
    )(pt, qd, knew, *([ckv_pool] * pp), *([kr_pool] * pp))


def _alibi_rows(rows, td):
    r = _iota((rows, 1), 0)
    return _vdiv(r, td), _vmod(r, td)


def _slope_rows(hrow, g):
    slope = jnp.zeros(hrow.shape, F32)
    for hh in range(NSA_HPG):
        slope = jnp.where(hrow == hh, SLOPES[g * NSA_HPG + hh], slope)
    return slope


def _cmp_decode_kernel(pt_ref, q_ref, *rest, pp, td, past, n_cmp, n_sel_pad):
    k_pages, v_pages = rest[:pp], rest[pp:2 * pp]
    (w1k_ref, w1v_ref, pk_ref, pv_ref, w2k_ref, w2v_ref, o_ref, score_ref, kp_scr, vp_scr, yk_scr, yv_scr) = rest[2 * pp:]
    s = pl.program_id(1)
    cpp = PAGE_SIZE // CMP_STRIDE
    for k in range(pp):
        kp_scr[k * PAGE_SIZE:(k + 1) * PAGE_SIZE, :] = k_pages[k][...].T
        vp_scr[k * PAGE_SIZE:(k + 1) * PAGE_SIZE, :] = v_pages[k][...].T
    chunk_rows = lambda scr: jnp.concatenate(
        [scr[pl.ds(j, pp * cpp, stride=CMP_STRIDE), :] for j in range(CMP_STRIDE)], axis=1).astype(BF16)
    xk = chunk_rows(kp_scr)
    xv = chunk_rows(vp_scr)
    r0 = pl.multiple_of(s * (pp * cpp), pp * cpp)
    yk_scr[pl.ds(r0, pp * cpp), :] = _dot(xk, w1k_ref[...])
    yv_scr[pl.ds(r0, pp * cpp), :] = _dot(xv, w1v_ref[...])

    @pl.when(s == pl.num_programs(1) - 1)
    def _():
        kcc = _compress_finish(yk_scr[...], _dot(pk_ref[...], w1k_ref[...]), w2k_ref).astype(BF16)
        vcc = _compress_finish(yv_scr[...], _dot(pv_ref[...], w1v_ref[...]), w2v_ref).astype(BF16)
        ncp = kcc.shape[0]
        rows = NSA_HPG * td
        hrow, trow = _alibi_rows(rows, td)
        cidx = _iota((1, ncp), 1)
        dist = (past + trow) - (cidx * CMP_STRIDE + CMP_BLOCK - 1)
        mask = (dist >= 0) & (cidx < n_cmp)
        distf = dist.astype(F32)
        tsum = (_vmod(_iota((8, rows), 1), td) == _iota((8, rows), 0)).astype(F32)
        ov = _overlap(ncp, n_sel_pad)
        t8 = _iota((8, 1), 0)
        cur = _vdiv(past + t8, SLC_BLOCK)
        jj = _iota((8, n_sel_pad), 1)
        for g in range(NSA_GROUPS):
            sc = _dot_nt(q_ref[g], kcc) - _slope_rows(hrow, g) * distf
            sc = jnp.where(mask, sc, NEG)
            m = jnp.max(sc, axis=-1, keepdims=True)
            p = jnp.where(mask, jnp.exp(sc - m), 0.0)
            l = jnp.sum(p, axis=-1, keepdims=True)
            p = p / jnp.where(l > 0.0, l, 1.0)
            o_ref[g] = _dot(p.astype(BF16), vcc)
            imp = _dot_exact(tsum, p)
            score_ref[g] = _force_scores(_dot_exact(imp, ov), cur, jj)


def _cmp_decode(pt, qd, k_pool, v_pool, wts, *, td, pp, past, n_sel_pad):
    bd = qd.shape[0]
    n_pages = pt.shape[0] // bd
    cpp = PAGE_SIZE // CMP_STRIDE
    nch = n_pages * cpp
    n_cmp = (past + td) // CMP_STRIDE - CMP_BLOCK // CMP_STRIDE + 1
    assert (past + td) // CMP_STRIDE == nch, "new rows must not complete a chunk"
    per_b = lambda b, s, pt: (b, 0, 0, 0)
    c2 = lambda b, s, pt: (0, 0)
    rows = qd.shape[2]
    in_specs = ([pl.BlockSpec((None,) + qd.shape[1:], per_b)]
                + _page_specs((None, LANE, PAGE_SIZE), n_pages, pp) + _page_specs((None, LANE, PAGE_SIZE), n_pages, pp)
                + [pl.BlockSpec((CHUNK_FEATS, 2 * LANE), c2) for _ in range(2)]
                + [pl.BlockSpec((8, CHUNK_FEATS), c2) for _ in range(2)]
                + [pl.BlockSpec((LANE, LANE), c2) for _ in range(2)])
    kern = functools.partial(_cmp_decode_kernel, pp=pp, td=td, past=past, n_cmp=n_cmp, n_sel_pad=n_sel_pad)
    return pl.pallas_call(
        kern,
        grid_spec=pltpu.PrefetchScalarGridSpec(
            num_scalar_prefetch=1, grid=(bd, n_pages // pp), in_specs=in_specs,
            out_specs=[pl.BlockSpec((None, NSA_GROUPS, rows, LANE), per_b),
                       pl.BlockSpec((None, NSA_GROUPS, 8, n_sel_pad), per_b)],
            scratch_shapes=[pltpu.VMEM((pp * PAGE_SIZE, LANE), F32) for _ in range(2)]
            + [pltpu.VMEM((nch, 2 * LANE), F32) for _ in range(2)]),
        out_shape=[jax.ShapeDtypeStruct((bd, NSA_GROUPS, rows, LANE), F32),
                   jax.ShapeDtypeStruct((bd, NSA_GROUPS, 8, n_sel_pad), F32)],
        compiler_params=_cparams("parallel", "arbitrary"), name="cmp_decode",
    )(pt, qd, *([k_pool] * pp), *([v_pool] * pp), wts['cmp_w1k'], wts['cmp_w1v'], wts['cmp_posk'], wts['cmp_posv'],
      wts['cmp_w2k'], wts['cmp_w2v'])


def _rank_kernel(score_ref, cur_ref, sel_ref, *, n_sel):
    sc = score_ref[...]
    jj = _iota(sc.shape, 0)

    def body(i, rank):
        ri = score_ref[pl.ds(i, 1), :]
        beats = (ri > sc) | ((ri == sc) & (i < jj))
        return rank + beats.astype(F32)

    rank = lax.fori_loop(0, n_sel, body, jnp.zeros(sc.shape, F32))
    sel = (rank < float(min(SLC_TOP_N, n_sel))) & (jj <= cur_ref[...])
    sel_ref[...] = sel.astype(F32)


def _rank(score_t, cur, *, n_sel):
    full = lambda a: pl.BlockSpec(a.shape, lambda: (0,) * a.ndim)
    return pl.pallas_call(
        functools.partial(_rank_kernel, n_sel=n_sel), in_specs=[full(score_t), full(cur)],
        out_specs=full(score_t), out_shape=jax.ShapeDtypeStruct(score_t.shape, F32), name="rank_decode",
    )(score_t, cur)


def _slc_decode_kernel(pt_ref, q_ref, sel_ref, selnew_ref, knew_ref, vnew_ref, *rest, pp, td, past):
    k_pages, v_pages = rest[:pp], rest[pp:2 * pp]
    o_ref, kscr, vscr, m_scr, l_scr, acc_scr = rest[2 * pp:]
    s = pl.program_id(1)
    rows = q_ref.shape[1]
    hrow, trow = _alibi_rows(rows, td)
    tk = pp * PAGE_SIZE
    nblk = tk // SLC_BLOCK

    @pl.when(s == 0)
    def _():
        m_scr[...] = jnp.full_like(m_scr, NEG)
        l_scr[...] = jnp.zeros_like(l_scr)
        acc_scr[...] = jnp.zeros_like(acc_scr)
        kn = knew_ref[...]
        vn = vnew_ref[...]
        col = _iota((1, kn.shape[0]), 1)
        dist = trow - col
        for g in range(NSA_GROUPS):
            sc = _dot_nt(q_ref[g], kn) - _slope_rows(hrow, g) * dist.astype(F32)
            mask = (dist >= 0) & (col < td) & (selnew_ref[g] > 0.5)
            _softmax_update(jnp.where(mask, sc, NEG), vn, m_scr.at[g], l_scr.at[g], acc_scr.at[g])

    for k in range(pp):
        kscr[:, k * PAGE_SIZE:(k + 1) * PAGE_SIZE] = k_pages[k][...].astype(BF16)
        vscr[:, k * PAGE_SIZE:(k + 1) * PAGE_SIZE] = v_pages[k][...].astype(BF16)
    kt = kscr[...]
    vt = vscr[...]
    kpos = s * tk + _iota((1, tk), 1)
    distf = ((past + trow) - kpos).astype(F32)
    expand = (_vdiv(_iota((nblk, tk), 1), SLC_BLOCK) == _iota((nblk, tk), 0)).astype(BF16)
    for g in range(NSA_GROUPS):
        sc = _dot(q_ref[g], kt) - _slope_rows(hrow, g) * distf
        selx = _dot(sel_ref[g], expand) > 0.5
        _softmax_update(jnp.where(selx, sc, NEG), vt, m_scr.at[g], l_scr.at[g], acc_scr.at[g], v_transposed=True)

    @pl.when(s == pl.num_programs(1) - 1)
    def _():
        o_ref[...] = acc_scr[...] / l_scr[...]


def _slc_decode(pt, qd, sel16, selnew, knew, vnew, k_pool, v_pool, *, td, pp, past):
    bd = qd.shape[0]
    rows = qd.shape[2]
    n_pages = pt.shape[0] // bd
    per_b = lambda b, s, pt: (b, 0, 0, 0)
    per_b3 = lambda b, s, pt: (b, 0, 0)
    in_specs = ([pl.BlockSpec((None,) + qd.shape[1:], per_b),
                 pl.BlockSpec((None, None) + sel16.shape[2:], lambda b, s, pt: (b, s, 0, 0, 0)),
                 pl.BlockSpec((None,) + selnew.shape[1:], per_b),
                 pl.BlockSpec((None,) + knew.shape[1:], per_b3), pl.BlockSpec((None,) + vnew.shape[1:], per_b3)]
                + _page_specs((None, LANE, PAGE_SIZE), n_pages, pp) + _page_specs((None, LANE, PAGE_SIZE), n_pages, pp))
    kern = functools.partial(_slc_decode_kernel, pp=pp, td=td, past=past)
    return pl.pallas_call(
        kern,
        grid_spec=pltpu.PrefetchScalarGridSpec(
            num_scalar_prefetch=1, grid=(bd, n_pages // pp), in_specs=in_specs,
            out_specs=pl.BlockSpec((None, NSA_GROUPS, rows, LANE), per_b),
            scratch_shapes=[pltpu.VMEM((LANE, pp * PAGE_SIZE), BF16) for _ in range(2)]
            + [pltpu.VMEM((NSA_GROUPS, rows, 1), F32) for _ in range(2)] + [pltpu.VMEM((NSA_GROUPS, rows, LANE), F32)]),
        out_shape=jax.ShapeDtypeStruct((bd, NSA_GROUPS, rows, LANE), F32),
        compiler_params=_cparams("parallel", "arbitrary"), name="slc_decode",
    )(pt, qd, sel16, selnew, knew, vnew, *([k_pool] * pp), *([v_pool] * pp))


def _win_decode_kernel(q_ref, kst_ref, vst_ref, knew_ref, vnew_ref, o_ref, *, td, past):
    rows = q_ref.shape[1]
    hrow, trow = _alibi_rows(rows, td)
    nbuf = kst_ref.shape[1]
    kst = kst_ref[...].astype(BF16)
    vst = vst_ref[...].astype(BF16)
    kn = knew_ref[...]
    vn = vnew_ref[...]
    kpos = past - nbuf + _iota((1, nbuf), 1)
    d1 = (past + trow) - kpos
    m1 = (d1 >= 0) & (d1 < WINDOW) & (kpos >= 0)
    col = _iota((1, kn.shape[0]), 1)
    d2 = trow - col
    m2 = (d2 >= 0) & (d2 < WINDOW) & (col < td)
    for g in range(NSA_GROUPS):
        slope = _slope_rows(hrow, g)
        s1 = jnp.where(m1, _dot(q_ref[g], kst) - slope * d1.astype(F32), NEG)
        s2 = jnp.where(m2, _dot_nt(q_ref[g], kn) - slope * d2.astype(F32), NEG)
        m = jnp.maximum(jnp.max(s1, axis=-1, keepdims=True), jnp.max(s2, axis=-1, keepdims=True))
        p1 = jnp.exp(s1 - m)
        p2 = jnp.exp(s2 - m)
        l = jnp.sum(p1, axis=-1, keepdims=True) + jnp.sum(p2, axis=-1, keepdims=True)
        o_ref[g] = (_dot_nt(p1.astype(BF16), vst) + _dot(p2.astype(BF16), vn)) / l


def _win_decode(qd, kst, vst, knew, vnew, *, td, past):
    bd = qd.shape[0]
    rows = qd.shape[2]
    per_b = lambda b: (b, 0, 0, 0)
    per_b3 = lambda b: (b, 0, 0)
    blk3 = lambda a: pl.BlockSpec((None,) + a.shape[1:], per_b3)
    return pl.pallas_call(
        functools.partial(_win_decode_kernel, td=td, past=past), grid=(bd,),
        in_specs=[pl.BlockSpec((None,) + qd.shape[1:], per_b), blk3(kst), blk3(vst), blk3(knew), blk3(vnew)],
        out_specs=pl.BlockSpec((None, NSA_GROUPS, rows, LANE), per_b),
        out_shape=jax.ShapeDtypeStruct((bd, NSA_GROUPS, rows, LANE), F32),
        compiler_params=_cparams("parallel"), name="win_decode",
    )(qd, kst, vst, knew, vnew)


def _pad_rows(a, rows):
    return jnp.pad(a, ((0, 0), (0, rows - a.shape[1]), (0, 0)))


def _sample(x_sample, caches, page_table, w):
    (c_ckv, c_krope, c_cmp_k, c_cmp_v, c_slc_k, c_slc_v, s_win_k, s_win_v, s_conv) = caches
    bd, td, d = x_sample.shape
    n = bd * td
    n_pages = page_table.shape[1]
    past = n_pages * PAGE_SIZE
    n_pool = c_ckv.shape[0]
    kv_rank = c_ckv.shape[-1]
    x = x_sample.reshape(n, d)
    pos = past + jnp.arange(td, dtype=jnp.int32)
    cosq, sinq = _rope_tables(jnp.tile(pos, bd))
    (qrot, qabs, ckv, krp, qn, kc, vc, ks, vs, kw, vw, gn, ga, gb) = _inproj(
        x, w, cosq, sinq, sample=True, tm=n, tab_blocks=1)
    pt = page_table.reshape(-1)
    krope = krp[:, MLA_D_NOPE:MLA_D_NOPE + MLA_D_ROPE]
    qrope = qrot.reshape(bd, td, MLA_HEADS, HEAD_PAD)[..., MLA_D_NOPE:MLA_D_NOPE + MLA_D_ROPE]
    qpad = LANE - MLA_D_ROPE
    qd_mla = jnp.concatenate([qabs.reshape(bd, td, MLA_HEADS, kv_rank), qrope,
                              jnp.zeros((bd, td, MLA_HEADS, qpad), F32)], axis=-1)
    qd_mla = qd_mla.reshape(bd, td * MLA_HEADS, kv_rank + LANE).astype(BF16)
    knew = jnp.concatenate([ckv, krope, jnp.zeros((n, qpad), F32)], axis=-1).reshape(bd, td, -1)
    knew = _pad_rows(knew, 8).astype(BF16)
    pos_minor = lambda c: jnp.moveaxis(c, 1, -1).reshape(c.shape[0], -1, c.shape[1])
    o_lat = _mla_decode(pt, qd_mla, knew, c_ckv, pos_minor(c_krope), td=td, pp=min(32, n_pages))
    qg = qn.astype(F32).reshape(bd, td, NSA_GROUPS, NSA_HPG, NSA_DH).transpose(0, 2, 3, 1, 4)
    qg = qg.reshape(bd, NSA_GROUPS, NSA_HPG * td, NSA_DH)
    lane_g = (jnp.arange(LANE) // NSA_DH)[None, :] == jnp.arange(NSA_GROUPS)[:, None]
    qd = jnp.where(lane_g[None, :, None, :], jnp.tile(qg, (1, 1, 1, NSA_GROUPS)), 0.0).astype(BF16)
    n_sel = -(-(past + td) // SLC_BLOCK)
    n_sel_pad = -(-n_sel // LANE) * LANE
    o_cmp, score = _cmp_decode(pt, qd, pos_minor(c_cmp_k), pos_minor(c_cmp_v),
                               w, td=td, pp=min(32, n_pages), past=past, n_sel_pad=n_sel_pad)
    cur = jnp.broadcast_to((pos // SLC_BLOCK).astype(jnp.int32), (bd, NSA_GROUPS, td)).reshape(1, -1)
    n_sel8 = -(-n_sel // 8) * 8
    sel_t = _rank(score[:, :, :td, :n_sel8].reshape(-1, n_sel8).T, cur, n_sel=n_sel)
    sel = jnp.pad(sel_t.T, ((0, 0), (0, n_sel_pad - n_sel8))).reshape(bd, NSA_GROUPS, td, n_sel_pad)
    pp_slc = min(32, n_pages)
    bps = pp_slc * PAGE_SIZE // SLC_BLOCK
    n_past_blk = past // SLC_BLOCK
    sel_past = sel[..., :n_past_blk].reshape(bd, NSA_GROUPS, td, n_past_blk // bps, bps).transpose(0, 3, 1, 2, 4)
    sel16 = jnp.tile(sel_past, (1, 1, 1, NSA_HPG, 1)).astype(BF16)
    selnew = jnp.tile(jnp.broadcast_to(sel[..., n_past_blk:n_past_blk + 1], (bd, NSA_GROUPS, td, 8)), (1, 1, NSA_HPG, 1))
    new8 = lambda a: _pad_rows(a.reshape(bd, td, LANE), 8).astype(BF16)
    o_slc = _slc_decode(pt, qd, sel16, selnew, new8(ks), new8(vs), pos_minor(c_slc_k), pos_minor(c_slc_v),
                        td=td, pp=pp_slc, past=past)
    nbuf = s_win_k.shape[1]
    o_win = _win_decode(qd, pos_minor(s_win_k), pos_minor(s_win_v), new8(kw), new8(vw),
                        td=td, past=past)

    def heads_out(o):
        o = o.reshape(bd, NSA_GROUPS, NSA_HPG, td, NSA_GROUPS, NSA_DH)
        o = jnp.stack([o[:, g, :, :, g] for g in range(NSA_GROUPS)], axis=1)
        return o.transpose(0, 3, 1, 2, 4).reshape(n, NSA_HEADS * NSA_DH)

    dff = w['w_gate'].shape[1]
    prev1 = jnp.zeros((bd, td, dff), F32).at[:, 0].set(s_conv[:, 1])
    prev2 = jnp.zeros((bd, td, dff), F32).at[:, 0].set(s_conv[:, 0]).at[:, 1].set(s_conv[:, 1])
    y, g = _finish(x, o_lat.reshape(n, MLA_HEADS * kv_rank), heads_out(o_cmp), heads_out(o_slc), heads_out(o_win),
                   gn, ga, gb, (prev1.reshape(n, dff), prev2.reshape(n, dff)), w, tm=n, period=td, full_g=True,
                   latent=True)
    kv4 = lambda a: a.reshape(1, bd, td, NSA_GROUPS, NSA_DH)
    win = lambda st, new: jnp.concatenate([st, new.reshape(bd, td, NSA_GROUPS, NSA_DH)], axis=1)[None, :, -nbuf:]
    conv_state = jnp.concatenate([s_conv, g.reshape(bd, td, dff)], axis=1)[None, :, -(CONV_W - 1):]
    states = (ckv.reshape(1, bd, td, kv_rank), krope.reshape(1, bd, td, MLA_D_ROPE), kv4(kc), kv4(vc), kv4(ks), kv4(vs),
              win(s_win_k, kw), win(s_win_v, vw), conv_state)
    return y.reshape(bd, td, d), states


def kernel(x_prompt, x_sample, cache_mla_ckv, cache_mla_krope, cache_nsa_cmp_k, cache_nsa_cmp_v, cache_nsa_slc_k, cache_nsa_slc_v, state_win_k, state_win_v, state_ffn_conv, page_table, norm1_g, w_in, q_norm_g, kv_norm_g, w_uq, w_uk, w_uv, cmp_pos_k, cmp_w1_k, cmp_w2_k, cmp_pos_v, cmp_w1_v, cmp_w2_v, w_proj_mla, w_proj_nsa, w_out, norm2_g, w_gate, w_up, conv_w, conv_b, w_down, norm_f_g):
    assert norm1_g.shape[0] == 1, "single-layer trunk"
    p = dict(norm1_g=norm1_g[0], w_in=w_in[0], q_norm_g=q_norm_g[0], kv_norm_g=kv_norm_g[0], w_uq=w_uq[0],
             w_uk=w_uk[0], w_uv=w_uv[0], cmp_pos_k=cmp_pos_k[0], cmp_w1_k=cmp_w1_k[0], cmp_w2_k=cmp_w2_k[0],
             cmp_pos_v=cmp_pos_v[0], cmp_w1_v=cmp_w1_v[0], cmp_w2_v=cmp_w2_v[0], w_proj_mla=w_proj_mla[0],
             w_proj_nsa=w_proj_nsa[0], w_out=w_out[0], norm2_g=norm2_g[0], w_gate=w_gate[0], w_up=w_up[0],
             conv_w=conv_w[0], conv_b=conv_b[0], w_down=w_down[0], norm_f_g=norm_f_g)
    w = _prep_weights(p)
    y_p, ps = _prompt(x_prompt, w)
    caches = (cache_mla_ckv[0], cache_mla_krope[0], cache_nsa_cmp_k[0], cache_nsa_cmp_v[0], cache_nsa_slc_k[0],
              cache_nsa_slc_v[0], state_win_k[0], state_win_v[0], state_ffn_conv[0])
    y_s, ss = _sample(x_sample, caches, page_table, w)
    out = [y_p, y_s]
    for a, b in zip(ps, ss):
        out += [a, b]
    return tuple(out)
```

```python
import functools

import numpy as np
import jax
import jax.numpy as jnp
from jax import lax
from jax.experimental import pallas as pl
from jax.experimental.pallas import tpu as pltpu

MLA_HEADS = 8
MLA_D_NOPE = 64
MLA_D_ROPE = 32
MLA_D_V = 64
ROPE_THETA = 10000.0
MLA_SCALE = (MLA_D_NOPE + MLA_D_ROPE) ** -0.5
NSA_HEADS = 8
NSA_GROUPS = 2
NSA_HPG = NSA_HEADS // NSA_GROUPS
NSA_DH = 64
NSA_SCALE = NSA_DH ** -0.5
CMP_BLOCK = 32
CMP_STRIDE = 16
SLC_BLOCK = 64
SLC_TOP_N = 16
WINDOW = 512
CONV_W = 3
PAGE_SIZE = 128
EPS = 1e-6
NEG = -1e30
FORCE = 1e9

LOG2E = 1.4426950408889634
LANE = 128
AUX_POS_HI, AUX_POS_LO, AUX_BLK0 = 64, 65, 72
HEAD_PAD = 128
CHUNK_FEATS = CMP_STRIDE * NSA_GROUPS * NSA_DH
SLOPES = tuple(float(2.0 ** (-8.0 * (h + 1) / NSA_HEADS)) for h in range(NSA_HEADS))
VMEM_LIMIT = 56 * 1024 * 1024

F32 = jnp.float32
BF16 = jnp.bfloat16
_NT = (((1,), (1,)), ((), ()))


def _cparams(*sem):
    return pltpu.CompilerParams(dimension_semantics=sem, vmem_limit_bytes=VMEM_LIMIT)


def _rms(x, g):
    return x * lax.rsqrt(jnp.mean(x * x, axis=-1, keepdims=True) + EPS) * g


def _dot(a, b):
    return jnp.dot(a, b, preferred_element_type=F32)


def _dot_nt(a, b):
    return lax.dot_general(a, b, _NT, preferred_element_type=F32)


def _dot_exact(a, b):
    return jnp.dot(a, b, preferred_element_type=F32, precision=lax.Precision.HIGHEST)


def _iota(shape, dim):
    return lax.broadcasted_iota(jnp.int32, shape, dim)


def _log2(n):
    assert n > 0 and n & (n - 1) == 0, n
    return n.bit_length() - 1


def _vdiv(x, n):
    return lax.shift_right_logical(x, jnp.full(x.shape, _log2(n), jnp.int32))


def _vmod(x, n):
    assert n & (n - 1) == 0, n
    return x & (n - 1)


_O_CQ, _O_CKV, _O_QN, _O_K6, _O_GA = 0, 384, 640, 1152, 1920


def _inproj_kernel(x_ref, g1_ref, w_ref, qg_ref, kvg_ref, wuq_ref, wk_ref, wv_ref, cos_ref, sin_ref,
                   *outs, sample, q_rank, kv_rank, d_model, tiles_per_seq, q_scale):
    o_gb = _O_GA + d_model
    o_kr = o_gb + d_model
    x = x_ref[...]
    hn = _rms(x, g1_ref[...])
    y = _dot(hn.astype(BF16), w_ref[...])
    cosq = cos_ref[...]
    sinq = sin_ref[...]
    nq = MLA_HEADS * HEAD_PAD
    cqn = _rms(y[:, _O_CQ:_O_CQ + q_rank], qg_ref[...])
    q2 = _dot(cqn.astype(BF16), wuq_ref[...])
    ckv = _rms(y[:, _O_CKV:_O_CKV + kv_rank], kvg_ref[...])
    kr = y[:, o_kr:o_kr + LANE] * cosq + y[:, o_kr + LANE:o_kr + 2 * LANE] * sinq
    ckv_b = ckv.astype(BF16)
    it = iter(outs)
    if sample:
        qrot_ref, qabs_ref = next(it), next(it)
        for h in range(MLA_HEADS):
            sl = slice(h * HEAD_PAD, (h + 1) * HEAD_PAD)
            qh = (q2[:, sl] * cosq + q2[:, nq + h * HEAD_PAD:nq + (h + 1) * HEAD_PAD] * sinq) * q_scale
            qrot_ref[:, sl] = qh
            qabs_ref[:, h * kv_rank:(h + 1) * kv_rank] = _dot(qh.astype(BF16), wk_ref[h])
    else:
        q_ref, k_ref, v_ref = next(it), next(it), next(it)
        knp = _dot(ckv_b, wk_ref[...])
        for h in range(MLA_HEADS):
            sl = slice(h * HEAD_PAD, (h + 1) * HEAD_PAD)
            qh = (q2[:, sl] * cosq + q2[:, nq + h * HEAD_PAD:nq + (h + 1) * HEAD_PAD] * sinq) * q_scale
            q_ref[:, sl] = qh.astype(BF16)
            k_ref[:, sl] = (knp[:, sl] + kr).astype(BF16)
        v_ref[...] = _dot(ckv_b, wv_ref[...]).astype(BF16)
    ckv_ref, kr_ref, qn_ref = next(it), next(it), next(it)
    ckv_ref[...] = ckv
    kr_ref[...] = kr
    yq = y[:, _O_QN:_O_QN + NSA_HEADS * NSA_DH] * NSA_SCALE
    if sample:
        qn_ref[...] = yq.astype(BF16)
    else:
        lane_q = _iota((x.shape[0], LANE), 1)
        for h in range(NSA_HEADS):
            pair = yq[:, (h // 2) * LANE:(h // 2 + 1) * LANE]
            if h % 2:
                pair = pltpu.roll(pair, NSA_DH, 1)
            aux_q = jnp.where((lane_q == AUX_POS_HI) | (lane_q == AUX_POS_LO), SLOPES[h], 0.0)
            qn_ref[:, h * LANE:(h + 1) * LANE] = jnp.where(lane_q < NSA_DH, pair, aux_q).astype(BF16)
    for j in range(6):
        yj = y[:, _O_K6 + j * LANE:_O_K6 + (j + 1) * LANE]
        if sample:
            next(it)[...] = yj
        else:
            next(it)[...] = yj.T
    if not sample:
        tm = x.shape[0]
        pos = (pl.program_id(0) % tiles_per_seq) * tm + _iota((tm, 1), 0)
        lane = _iota((tm, LANE), 1)
        onehot = ((lane >= AUX_BLK0) & (_vdiv(pos, SLC_BLOCK) == lane - AUX_BLK0)).astype(F32)
        aux = jnp.where(lane == AUX_POS_HI, (pos - _vmod(pos, 256)).astype(F32),
                        jnp.where(lane == AUX_POS_LO, _vmod(pos, 256).astype(F32), onehot))
        for j in range(6):
            yj = y[:, _O_K6 + j * LANE:_O_K6 + (j + 1) * LANE]
            ref = next(it)
            if j in (2, 4):
                ref[:, :LANE] = jnp.where(lane < NSA_DH, yj, aux).astype(BF16)
                ref[:, LANE:] = jnp.where(lane < NSA_DH, pltpu.roll(yj, NSA_DH, 1), aux).astype(BF16)
            else:
                ref[...] = yj.astype(BF16)
    gn_ref, ga_ref, gb_ref = next(it), next(it), next(it)
    gn_ref[...] = jax.nn.sigmoid(y[:, o_kr + 2 * LANE:o_kr + 3 * LANE])
    ga_ref[...] = jax.nn.sigmoid(y[:, _O_GA:_O_GA + d_model])
    gb_ref[...] = jax.nn.sigmoid(y[:, o_gb:o_gb + d_model])


def _inproj(x, wts, cosq, sinq, *, sample, tm, tab_blocks):
    n, d = x.shape
    q_rank, kv_rank = wts['q_norm_g'].shape[1], wts['kv_norm_g'].shape[1]
    nq = MLA_HEADS * HEAD_PAD
    wk = wts['w_ukT'] if sample else wts['w_ukp']
    row = lambda i: (i, 0)
    const2 = lambda i: (0, 0)
    tab_map = (lambda i: (i % tab_blocks, 0))
    in_specs = [
        pl.BlockSpec((tm, d), row),
        pl.BlockSpec((1, d), const2),
        pl.BlockSpec(wts['w_in'].shape, const2),
        pl.BlockSpec((1, q_rank), const2),
        pl.BlockSpec((1, kv_rank), const2),
        pl.BlockSpec(wts['w_uq2'].shape, const2),
        pl.BlockSpec(wk.shape, (lambda i: (0, 0, 0)) if sample else const2),
        pl.BlockSpec(wts['w_uvf'].shape, const2),
        pl.BlockSpec((tm, LANE), tab_map),
        pl.BlockSpec((tm, LANE), tab_map),
    ]
    shapes = []
    if sample:
        shapes += [(nq, F32), (MLA_HEADS * kv_rank, F32)]
    else:
        shapes += [(nq, BF16), (nq, BF16), (MLA_HEADS * MLA_D_V, BF16)]
    shapes += [(kv_rank, F32), (LANE, F32), (NSA_HEADS * (NSA_DH if sample else LANE), BF16)]
    n_lead = len(shapes)
    shapes += [(LANE, F32)] * 6
    if not sample:
        shapes += [(LANE, BF16), (LANE, BF16), (2 * LANE, BF16), (LANE, BF16), (2 * LANE, BF16), (LANE, BF16)]
        assert AUX_BLK0 + -(-tab_blocks * tm // SLC_BLOCK) <= LANE, "block one-hot must fit the aux lanes"
    shapes += [(LANE, F32), (d, F32), (d, F32)]
    out_shape = [jax.ShapeDtypeStruct((n, w), dt) for w, dt in shapes]
    out_specs = [pl.BlockSpec((tm, w), row) for w, _ in shapes]
    if not sample:
        tpb = tab_blocks
        for j in range(n_lead, n_lead + 6):
            out_shape[j] = jax.ShapeDtypeStruct((n // (tpb * tm), LANE, tpb * tm), F32)
            out_specs[j] = pl.BlockSpec((None, LANE, tm), lambda i: (i // tpb, 0, i % tpb))
    q_scale = MLA_SCALE if sample else MLA_SCALE * LOG2E
    kern = functools.partial(_inproj_kernel, sample=sample, q_rank=q_rank, kv_rank=kv_rank, d_model=d,
                             tiles_per_seq=tab_blocks, q_scale=q_scale)
    return pl.pallas_call(
        kern, grid=(n // tm,), in_specs=in_specs, out_specs=out_specs, out_shape=out_shape,
        compiler_params=_cparams("parallel"), name="inproj_sample" if sample else "inproj_prompt",
    )(x, wts['norm1_g'], wts['w_in'], wts['q_norm_g'], wts['kv_norm_g'], wts['w_uq2'], wk, wts['w_uvf'],
      cosq, sinq)


def _stack_heads(q_ref, heads, width):
    parts = [q_ref[:, h * width:(h + 1) * width] for h in heads]
    return parts[0] if len(parts) == 1 else jnp.concatenate(parts, axis=0)


def _flash_kernel(*refs, tq, tk, groups, par, dq, dk, dv, window, has_bias, base2):
    if has_bias:
        q_ref, k_ref, v_ref, bias_ref, o_ref = refs
    else:
        q_ref, k_ref, v_ref, o_ref = refs
    q_start = pl.program_id(1) * tq
    n_hi = (q_start + tq - 1) // tk + 1
    hi_full = (q_start + 1) // tk
    if window:
        n_lo = jnp.maximum(q_start - (window - 1), 0) // tk
        lo_full = (jnp.maximum(q_start + tq - window, 0) + tk - 1) // tk
    else:
        n_lo, lo_full = 0, 0
    e1 = jnp.clip(lo_full, n_lo, n_hi)
    e2 = jnp.clip(hi_full, e1, n_hi)
    ex = jnp.exp2 if base2 else jnp.exp
    for c0 in range(0, len(groups), par):
        chunk = groups[c0:c0 + par]
        qs = []
        for heads, kcol in chunk:
            qg = _stack_heads(q_ref, heads, dq)
            if has_bias:
                bias = bias_ref[:, kcol * LANE:(kcol + 1) * LANE]
                qg = qg + jnp.concatenate([bias] * len(heads), axis=0)
            qs.append(qg)
        rows = qs[0].shape[0]
        qpos = q_start + (_iota((rows, 1), 0) & (tq - 1))

        def step(j, carry, masked):
            k0 = pl.multiple_of(j * tk, tk)
            if masked:
                dist = qpos - (k0 + _iota((1, tk), 1))
                mask = dist >= 0
                if window:
                    mask = mask & (dist < window)
            out = []
            for (heads, kcol), qg, (m, l, acc) in zip(chunk, qs, carry):
                kt = k_ref[pl.ds(k0, tk), kcol * dk:(kcol + 1) * dk]
                vt = v_ref[pl.ds(k0, tk), kcol * dv:(kcol + 1) * dv]
                s = _dot_nt(qg, kt)
                if masked:
                    s = jnp.where(mask, s, NEG)
                m_new = jnp.maximum(m, jnp.max(s, axis=-1, keepdims=True))
                p = ex(s - m_new)
                alpha = ex(m - m_new)
                l = alpha * l + jnp.sum(p, axis=-1, keepdims=True)
                acc = alpha * acc + _dot(p.astype(BF16), vt)
                out.append((m_new, l, acc))
            return tuple(out)

        carry = tuple((jnp.full((rows, 1), NEG, F32), jnp.zeros((rows, 1), F32), jnp.zeros((rows, dv), F32))
                      for _ in chunk)
        carry = lax.fori_loop(n_lo, e1, functools.partial(step, masked=True), carry)
        carry = lax.fori_loop(e1, e2, functools.partial(step, masked=False), carry)
        carry = lax.fori_loop(e2, n_hi, functools.partial(step, masked=True), carry)
        for (heads, kcol), (m, l, acc) in zip(chunk, carry):
            o = acc * (1.0 / l)
            for hh, h in enumerate(heads):
                o_ref[:, h * dv:(h + 1) * dv] = o[hh * tq:(hh + 1) * tq].astype(o_ref.dtype)


def _flash(q, k, v, bias, *, batch, seq, tq, tk, groups, par, dq, dk, dv, window, base2, name):
    n = q.shape[0]
    has_bias = bias is not None
    n_heads = sum(len(g[0]) for g in groups)
    qrow = lambda b, i: (b * (seq // tq) + i, 0)
    kv = lambda b, i: (b, 0)
    in_specs = [pl.BlockSpec((tq, q.shape[1]), qrow), pl.BlockSpec((seq, k.shape[1]), kv),
                pl.BlockSpec((seq, v.shape[1]), kv)]
    args = [q, k, v]
    if has_bias:
        in_specs.append(pl.BlockSpec((tq, bias.shape[1]), qrow))
        args.append(bias)
    kern = functools.partial(_flash_kernel, tq=tq, tk=tk, groups=groups, par=par, dq=dq, dk=dk, dv=dv, window=window,
                             has_bias=has_bias, base2=base2)
    return pl.pallas_call(
        kern, grid=(batch, seq // tq), in_specs=in_specs,
        out_specs=pl.BlockSpec((tq, n_heads * dv), qrow),
        out_shape=jax.ShapeDtypeStruct((n, n_heads * dv), BF16),
        compiler_params=_cparams("parallel", "arbitrary"), name=name,
    )(*args)


def _win_kernel(q_ref, k_ref, v_ref, o_ref, *, tq, groups, dv):
    i = pl.program_id(1)
    rq = _iota((tq, tq), 0)
    ck = _iota((tq, tq), 1)
    bias_own = jnp.where(ck <= rq, 0.0, NEG)
    bias_far = jnp.where(ck > rq, 0.0, NEG) + jnp.where(i >= 2, 0.0, NEG)
    bias_mid = jnp.where(i >= 1, 0.0, NEG)
    starts = (jnp.maximum(i - 2, 0) * tq, jnp.maximum(i - 1, 0) * tq, i * tq)
    for heads, kcol in groups:
        qg = _stack_heads(q_ref, heads, LANE)
        nh = len(heads)
        ss, vs = [], []
        for k0, bias in zip(starts, (bias_far, None, bias_own)):
            k0 = pl.multiple_of(k0, tq)
            s = _dot_nt(qg, k_ref[pl.ds(k0, tq), kcol * LANE:(kcol + 1) * LANE])
            s = s + (bias_mid if bias is None else jnp.concatenate([bias] * nh, axis=0))
            ss.append(s)
            vs.append(v_ref[pl.ds(k0, tq), kcol * dv:(kcol + 1) * dv])
        m = functools.reduce(jnp.maximum, [jnp.max(s, axis=-1, keepdims=True) for s in ss])
        ps = [jnp.exp(s - m) for s in ss]
        l = functools.reduce(jnp.add, [jnp.sum(p, axis=-1, keepdims=True) for p in ps])
        acc = functools.reduce(jnp.add, [_dot(p.astype(BF16), v) for p, v in zip(ps, vs)])
        o = acc * (1.0 / l)
        for hh, h in enumerate(heads):
            o_ref[:, h * dv:(h + 1) * dv] = o[hh * tq:(hh + 1) * tq].astype(o_ref.dtype)


def _win_prompt(q, k, v, *, batch, seq, tq, groups, dv):
    assert WINDOW == 2 * tq and seq % tq == 0
    n = q.shape[0]
    n_heads = sum(len(g[0]) for g in groups)
    qrow = lambda b, i: (b * (seq // tq) + i, 0)
    kv = lambda b, i: (b, 0)
    return pl.pallas_call(
        functools.partial(_win_kernel, tq=tq, groups=groups, dv=dv), grid=(batch, seq // tq),
        in_specs=[pl.BlockSpec((tq, q.shape[1]), qrow), pl.BlockSpec((seq, k.shape[1]), kv),
                  pl.BlockSpec((seq, v.shape[1]), kv)],
        out_specs=pl.BlockSpec((tq, n_heads * dv), qrow),
        out_shape=jax.ShapeDtypeStruct((n, n_heads * dv), BF16),
        compiler_params=_cparams("parallel", "arbitrary"), name="win_prompt",
    )(q, k, v)


def _compress_rows(xk, xv, w1k_ref, w1v_ref, posk_ref, posv_ref):
    yk = _dot(xk.astype(BF16), w1k_ref[...])
    yv = _dot(xv.astype(BF16), w1v_ref[...])
    return yk, yv


def _compress_finish(y, posy, w2_ref):
    rows = y.shape[0]
    a = y[:, :LANE]
    b = pltpu.roll(y[:, LANE:], rows - 1, 0)
    pos = posy[0:1, :LANE] + posy[1:2, LANE:]
    hid = jax.nn.gelu(a + b + pos)
    return _dot(hid.astype(BF16), w2_ref[...])


def _compress_prompt_kernel(xk_ref, xv_ref, w1k_ref, w1v_ref, pk_ref, pv_ref, w2k_ref, w2v_ref, ok_ref, ov_ref):
    yk, yv = _compress_rows(xk_ref[...], xv_ref[...], w1k_ref, w1v_ref, pk_ref, pv_ref)
    ok_ref[...] = _compress_finish(yk, _dot(pk_ref[...], w1k_ref[...]), w2k_ref).astype(BF16)
    ov_ref[...] = _compress_finish(yv, _dot(pv_ref[...], w1v_ref[...]), w2v_ref).astype(BF16)


def _compress_prompt(kc, vc, wts, *, batch, seq):
    nch = seq // CMP_STRIDE
    xk = kc.reshape(batch * nch, CHUNK_FEATS)
    xv = vc.reshape(batch * nch, CHUNK_FEATS)
    row = lambda b: (b, 0)
    c2 = lambda b: (0, 0)
    wspec = pl.BlockSpec((CHUNK_FEATS, 2 * LANE), c2)
    pspec = pl.BlockSpec((8, CHUNK_FEATS), c2)
    w2spec = pl.BlockSpec((LANE, LANE), c2)
    return pl.pallas_call(
        _compress_prompt_kernel, grid=(batch,),
        in_specs=[pl.BlockSpec((nch, CHUNK_FEATS), row), pl.BlockSpec((nch, CHUNK_FEATS), row),
                  wspec, wspec, pspec, pspec, w2spec, w2spec],
        out_specs=[pl.BlockSpec((nch, LANE), row)] * 2,
        out_shape=[jax.ShapeDtypeStruct((batch * nch, LANE), BF16)] * 2,
        compiler_params=_cparams("parallel"), name="compress_prompt",
    )(xk, xv, wts['cmp_w1k'], wts['cmp_w1v'], wts['cmp_posk'], wts['cmp_posv'], wts['cmp_w2k'], wts['cmp_w2v'])


def _overlap(n_rows, n_sel):
    c = _iota((n_rows, n_sel), 0) * CMP_STRIDE
    j = _iota((n_rows, n_sel), 1) * SLC_BLOCK
    return ((c < j + SLC_BLOCK) & (c + CMP_BLOCK > j)).astype(F32)


def _force_scores(score, cur, jj):
    forced = (jj == 0) | (jj == cur) | (jj == cur - 1)
    score = jnp.where(forced, FORCE, score)
    return jnp.where(jj <= cur, score, NEG)


def _cmp_prompt_kernel(q_ref, k_ref, v_ref, o_ref, bias_ref, *, tq, n_cmp, n_sel):
    q_start = pl.program_id(1) * tq
    ncp = k_ref.shape[0]
    rows = NSA_HPG * tq
    qpos = q_start + (_iota((rows, 1), 0) & (tq - 1))
    cidx = _iota((1, ncp), 1)
    dist = qpos - (cidx * CMP_STRIDE + CMP_BLOCK - 1)
    mask = (dist >= 0) & (cidx < n_cmp)
    distf = dist.astype(F32)
    hrow = _vdiv(_iota((rows, 1), 0), tq)
    nsp = -(-n_sel // 8) * 8
    cur = _vdiv(q_start + _iota((1, tq), 1), SLC_BLOCK)
    jj = _iota((nsp, tq), 0)
    cb = _iota((nsp, ncp), 1) * CMP_STRIDE
    jb = _iota((nsp, ncp), 0) * SLC_BLOCK
    ov_t = ((cb < jb + SLC_BLOCK) & (cb + CMP_BLOCK > jb)).astype(F32)
    for g in range(NSA_GROUPS):
        heads = range(g * NSA_HPG, (g + 1) * NSA_HPG)
        qg = jnp.concatenate([q_ref[:, h * LANE:h * LANE + NSA_DH] for h in heads], axis=0)
        slope = jnp.zeros((rows, 1), F32)
        for hh, h in enumerate(heads):
            slope = jnp.where(hrow == hh, SLOPES[h], slope)
        s = _dot_nt(qg, k_ref[:, g * NSA_DH:(g + 1) * NSA_DH]) - slope * distf
        s = jnp.where(mask, s, NEG)
        m = jnp.max(s, axis=-1, keepdims=True)
        p = jnp.where(mask, jnp.exp(s - m), 0.0)
        l = jnp.sum(p, axis=-1, keepdims=True)
        p = p / jnp.where(l > 0.0, l, 1.0)
        o = _dot(p.astype(BF16), v_ref[:, g * NSA_DH:(g + 1) * NSA_DH])
        imp = p[0:tq]
        for hh in range(1, NSA_HPG):
            imp = imp + p[hh * tq:(hh + 1) * tq]
            o_ref[:, (g * NSA_HPG + hh) * NSA_DH:(g * NSA_HPG + hh + 1) * NSA_DH] = o[hh * tq:(hh + 1) * tq].astype(BF16)
        o_ref[:, g * NSA_HPG * NSA_DH:(g * NSA_HPG + 1) * NSA_DH] = o[0:tq].astype(BF16)
        score = lax.dot_general(ov_t, imp, _NT, preferred_element_type=F32, precision=lax.Precision.HIGHEST)
        score = _force_scores(score, cur, jj)
        rank = jnp.zeros((nsp, tq), F32)
        for i in range(n_sel):
            ri = score[i:i + 1, :]
            beats = (ri > score) | ((ri == score) & (i < jj))
            rank = rank + beats.astype(F32)
        sel = (rank < float(min(SLC_TOP_N, n_sel))) & (jj <= cur)
        bias_t = jnp.where(sel | (jj >= n_sel), 0.0, NEG)
        bias_t = jnp.concatenate([jnp.zeros((AUX_BLK0, tq), F32), bias_t,
                                  jnp.zeros((LANE - AUX_BLK0 - nsp, tq), F32)], axis=0)
        bias_ref[:, g * LANE:(g + 1) * LANE] = bias_t.T.astype(BF16)


def _cmp_prompt(qn, kcc, vcc, *, batch, seq, tq):
    n = qn.shape[0]
    nch = seq // CMP_STRIDE
    n_cmp = nch - CMP_BLOCK // CMP_STRIDE + 1
    n_sel = -(-seq // SLC_BLOCK)
    qrow = lambda b, i: (b * (seq // tq) + i, 0)
    kv = lambda b, i: (b, 0)
    kern = functools.partial(_cmp_prompt_kernel, tq=tq, n_cmp=n_cmp, n_sel=n_sel)
    return pl.pallas_call(
        kern, grid=(batch, seq // tq),
        in_specs=[pl.BlockSpec((tq, qn.shape[1]), qrow), pl.BlockSpec((nch, LANE), kv), pl.BlockSpec((nch, LANE), kv)],
        out_specs=[pl.BlockSpec((tq, NSA_HEADS * NSA_DH), qrow), pl.BlockSpec((tq, NSA_GROUPS * LANE), qrow)],
        out_shape=[jax.ShapeDtypeStruct((n, NSA_HEADS * NSA_DH), BF16),
                   jax.ShapeDtypeStruct((n, NSA_GROUPS * LANE), BF16)],
        compiler_params=_cparams("parallel", "arbitrary"), name="cmp_prompt",
    )(qn, kcc, vcc)


def _finish_kernel(*refs, period, latent, has_state):
    it = iter(refs)
    x_ref, omla_ref, ocmp_ref, oslc_ref, owin_ref, gn_ref, ga_ref, gb_ref = (next(it) for _ in range(8))
    prev1_ref, prev2_ref = (next(it), next(it)) if has_state else (None, None)
    gx_ref = next(it)
    wuv_ref = next(it) if latent else None
    (wpm_ref, wpn_ref, wo_ref, g2_ref, wg_ref, wu_ref, cw_ref, cb_ref, wd_ref, gf_ref,
     y_ref, gout_ref, carry_ref) = it
    tm = x_ref.shape[0]
    gn = gn_ref[...]
    gn_hi = gn.astype(BF16)
    gn_lo = (gn - gn_hi.astype(F32)).astype(BF16)
    gexp = _dot(gn_hi, gx_ref[...]) + _dot(gn_lo, gx_ref[...])
    w = NSA_HEADS * NSA_DH
    o_nsa = (gexp[:, 0:w] * ocmp_ref[...].astype(F32) + gexp[:, w:2 * w] * oslc_ref[...].astype(F32)
             + gexp[:, 2 * w:3 * w] * owin_ref[...].astype(F32))
    o_mla = omla_ref[...].astype(BF16)
    if latent:
        o_mla = _dot(o_mla, wuv_ref[...]).astype(BF16)
    merged = (ga_ref[...] * _dot(o_mla, wpm_ref[...])
              + gb_ref[...] * _dot(o_nsa.astype(BF16), wpn_ref[...]))
    x1 = x_ref[...] + _dot(merged.astype(BF16), wo_ref[...])
    h2 = _rms(x1, g2_ref[...]).astype(BF16)
    g = _dot(h2, wg_ref[...])
    u = _dot(h2, wu_ref[...])
    row = _iota((tm, 1), 0)
    i = pl.program_id(0)
    t = _vmod(i * tm + row, period)
    g1 = pltpu.roll(g, 1, 0)
    g2 = pltpu.roll(g, 2, 0)
    if period > tm:
        @pl.when(i == 0)
        def _():
            carry_ref[...] = jnp.zeros_like(carry_ref)
        c = carry_ref[...]
        g1 = jnp.where(row == 0, c[7:8], g1)
        g2 = jnp.where(row == 0, c[6:7], jnp.where(row == 1, c[7:8], g2))
        carry_ref[...] = g[tm - 8:tm]
    g1 = jnp.where(t >= 1, g1, prev1_ref[...] if has_state else 0.0)
    g2 = jnp.where(t >= 2, g2, prev2_ref[...] if has_state else 0.0)
    cw = cw_ref[...]
    conv = cb_ref[...] + cw[0:1] * g2 + cw[1:2] * g1 + cw[2:3] * g
    act = (jax.nn.silu(conv) * u).astype(BF16)
    x2 = x1 + _dot(act, wd_ref[...])
    y_ref[...] = _rms(x2, gf_ref[...])
    gout_ref[...] = g[tm - 8:tm] if gout_ref.shape[0] == 8 else g


def _finish(x, omla, ocmp, oslc, owin, gn, ga, gb, state_rows, wts, *, tm, period, full_g, latent):
    n, d = x.shape
    dff = wts['w_gate'].shape[1]
    row = lambda i: (i, 0)
    c2 = lambda i: (0, 0)
    acts = [x, omla, ocmp, oslc, owin, gn, ga, gb] + (list(state_rows) if state_rows is not None else [])
    consts = [wts['gate_expand']] + ([wts['w_uvbd']] if latent else []) + [
        wts['w_proj_mla'], wts['w_proj_nsa'], wts['w_out'], wts['norm2_g'],
        wts['w_gate'], wts['w_up'], wts['conv_w'], wts['conv_b'], wts['w_down'], wts['norm_f_g']]
    ins = acts + consts
    in_specs = [pl.BlockSpec((tm, a.shape[1]), row) for a in acts] + [pl.BlockSpec(a.shape, c2) for a in consts]
    g_rows = n if full_g else (n // tm) * 8
    g_blk = tm if full_g else 8
    kern = functools.partial(_finish_kernel, period=period, latent=latent, has_state=state_rows is not None)
    return pl.pallas_call(
        kern, grid=(n // tm,), in_specs=in_specs,
        out_specs=[pl.BlockSpec((tm, d), row), pl.BlockSpec((g_blk, dff), row)],
        out_shape=[jax.ShapeDtypeStruct((n, d), F32), jax.ShapeDtypeStruct((g_rows, dff), F32)],
        scratch_shapes=[pltpu.VMEM((8, dff), F32)],
        compiler_params=_cparams("arbitrary"), name="finish_full" if full_g else "finish_tiled",
    )(*ins)


def _swap_halves(w):
    hlf = w.shape[-1] // 2
    return jnp.concatenate([-w[..., hlf:], w[..., :hlf]], axis=-1)


def _prep_weights(p):
    d = p['w_in'].shape[0]
    q_rank, kv_rank = p['q_norm_g'].shape[-1], p['kv_norm_g'].shape[-1]
    sizes = [q_rank, kv_rank, MLA_D_ROPE, NSA_HEADS * NSA_DH] + [2 * NSA_GROUPS * NSA_DH] * 3 + [3 * NSA_HEADS, d, d]
    cuts = np.cumsum(sizes)[:-1].tolist()
    cq, ckv, kr, qn, kvc, kvs, kvw, gn, ga, gb = jnp.split(p['w_in'], cuts, axis=-1)
    assert _O_CKV == q_rank and _O_QN == q_rank + kv_rank
    lo, hi = MLA_D_NOPE, HEAD_PAD - MLA_D_NOPE - MLA_D_ROPE
    place = lambda w: jnp.pad(w, ((0, 0), (lo, hi)))
    gnp = jnp.pad(gn, ((0, 0), (0, LANE - gn.shape[1])))
    w_in = jnp.concatenate([cq, ckv, qn, kvc, kvs, kvw, ga, gb, place(kr), place(_swap_halves(kr)), gnp], axis=1)
    w = {'w_in': w_in.astype(BF16)}
    for k in ('norm1_g', 'q_norm_g', 'kv_norm_g', 'norm2_g', 'conv_b'):
        w[k] = p[k].reshape(1, -1)
    w['norm_f_g'] = p['norm_f_g'].reshape(1, -1)
    w['conv_w'] = jnp.pad(p['conv_w'], ((0, 8 - CONV_W), (0, 0)))
    uq = p['w_uq']
    hpad = ((0, 0), (0, 0), (0, HEAD_PAD - uq.shape[-1]))
    uq_a = jnp.pad(uq, hpad)
    uq_b = jnp.pad(jnp.concatenate([jnp.zeros_like(uq[..., :MLA_D_NOPE]), _swap_halves(uq[..., MLA_D_NOPE:])], -1), hpad)
    w['w_uq2'] = jnp.concatenate([uq_a.reshape(q_rank, -1), uq_b.reshape(q_rank, -1)], axis=1).astype(BF16)
    uk = p['w_uk']
    w['w_ukp'] = jnp.pad(uk, ((0, 0), (0, 0), (0, HEAD_PAD - MLA_D_NOPE))).reshape(kv_rank, -1).astype(BF16)
    w['w_ukT'] = jnp.pad(jnp.transpose(uk, (1, 2, 0)), ((0, 0), (0, HEAD_PAD - MLA_D_NOPE), (0, 0))).astype(BF16)
    w['w_uvf'] = p['w_uv'].reshape(kv_rank, -1).astype(BF16)
    eye_h = jnp.eye(MLA_HEADS, dtype=F32)
    w['w_uvbd'] = jnp.einsum('rhv,hk->hrkv', p['w_uv'], eye_h).reshape(MLA_HEADS * kv_rank, -1).astype(BF16)
    eye_g = jnp.eye(NSA_GROUPS, dtype=F32)
    for nm in ('k', 'v'):
        w1 = p['cmp_w1_' + nm].reshape(2, CMP_STRIDE, NSA_DH, -1)
        big = jnp.einsum('ajdh,gk->jgdakh', w1, eye_g)
        w['cmp_w1' + nm] = big.reshape(CHUNK_FEATS, -1).astype(BF16)
        pos = p['cmp_pos_' + nm].reshape(2, CMP_STRIDE, 1, NSA_DH)
        pos = jnp.broadcast_to(pos, (2, CMP_STRIDE, NSA_GROUPS, NSA_DH)).reshape(2, CHUNK_FEATS)
        w['cmp_pos' + nm] = jnp.pad(pos, ((0, 6), (0, 0))).astype(BF16)
        w2 = p['cmp_w2_' + nm]
        w['cmp_w2' + nm] = jnp.einsum('hd,gk->ghkd', w2, eye_g).reshape(NSA_GROUPS * w2.shape[0], -1).astype(BF16)
    ge = np.zeros((LANE, 3 * NSA_HEADS * NSA_DH), np.float32)
    for h in range(NSA_HEADS):
        for i in range(3):
            ge[h * 3 + i, i * NSA_HEADS * NSA_DH + h * NSA_DH:i * NSA_HEADS * NSA_DH + (h + 1) * NSA_DH] = 1.0
    w['gate_expand'] = jnp.asarray(ge)
    for k in ('w_proj_mla', 'w_proj_nsa', 'w_out', 'w_gate', 'w_up', 'w_down'):
        w[k] = p[k].astype(BF16)
    return w


def _rope_tables(pos):
    inv = ROPE_THETA ** (-jnp.arange(0, MLA_D_ROPE, 2, dtype=F32) / MLA_D_ROPE)
    ang = pos.astype(F32)[:, None] * inv[None, :]
    cos, sin = jnp.cos(ang), jnp.sin(ang)
    n = pos.shape[0]
    pad = jnp.zeros((n, HEAD_PAD - MLA_D_NOPE - MLA_D_ROPE), F32)
    cosq = jnp.concatenate([jnp.ones((n, MLA_D_NOPE), F32), cos, cos, pad], axis=1)
    sinq = jnp.concatenate([jnp.zeros((n, MLA_D_NOPE), F32), sin, sin, pad], axis=1)
    return cosq, sinq


_NSA_GROUPS_SPEC = tuple((tuple(range(g * NSA_HPG, (g + 1) * NSA_HPG)), g) for g in range(NSA_GROUPS))
_MLA_GROUPS_SPEC = tuple(((h,), h) for h in range(MLA_HEADS))


def _prompt(x_prompt, w):
    b, t, d = x_prompt.shape
    n = b * t
    x = x_prompt.reshape(n, d)
    tm = 256
    cosq, sinq = _rope_tables(jnp.arange(t, dtype=jnp.int32))
    (q_mla, k_mla, v_mla, ckv, krp, qn, kc, vc, ks, vs, kw, vw, kc_b, vc_b, ks_b, vs_b, kw_b, vw_b, gn, ga, gb) = _inproj(
        x, w, cosq, sinq, sample=False, tm=tm, tab_blocks=t // tm)
    o_mla = _flash(q_mla, k_mla, v_mla, None, batch=b, seq=t, tq=512, tk=512, groups=_MLA_GROUPS_SPEC, par=8,
                   dq=HEAD_PAD, dk=HEAD_PAD, dv=MLA_D_V, window=0, base2=True, name="mla_prompt")
    kcc, vcc = _compress_prompt(kc_b, vc_b, w, batch=b, seq=t)
    o_cmp, sel_bias = _cmp_prompt(qn, kcc, vcc, batch=b, seq=t, tq=128)
    o_slc = _flash(qn, ks_b, vs_b, sel_bias, batch=b, seq=t, tq=256, tk=512, groups=_NSA_GROUPS_SPEC, par=2,
                   dq=LANE, dk=LANE, dv=NSA_DH, window=0, base2=False, name="slc_prompt")
    o_win = _win_prompt(qn, kw_b, vw_b, batch=b, seq=t, tq=WINDOW // 2, groups=_NSA_GROUPS_SPEC, dv=NSA_DH)
    dff = w['w_gate'].shape[1]
    y, gtail = _finish(x, o_mla, o_cmp, o_slc, o_win, gn, ga, gb, None, w, tm=tm, period=t, full_g=False,
                       latent=False)
    kv4 = lambda a: a.reshape(1, b, NSA_GROUPS, NSA_DH, a.shape[-1]).transpose(0, 1, 4, 2, 3)
    n_keep = min(WINDOW, t)
    kw, vw = kw[:, :, t - n_keep:], vw[:, :, t - n_keep:]
    conv_state = gtail.reshape(b, t // tm, 8, dff)[:, -1, 8 - (CONV_W - 1):, :]
    states = (ckv.reshape(1, b, t, -1), krp[:, MLA_D_NOPE:MLA_D_NOPE + MLA_D_ROPE].reshape(1, b, t, MLA_D_ROPE),
              kv4(kc), kv4(vc), kv4(ks), kv4(vs), kv4(kw), kv4(vw), conv_state[None])
    return y.reshape(b, t, d), states


def _page_copies(pt_ref, pools, bufs, sems, step, slot, pp):
    copies = []
    for k in range(pp):
        page = pt_ref[step * pp + k]
        for pool, buf, sem in zip(pools, bufs, sems):
            copies.append(pltpu.make_async_copy(pool.at[page], buf.at[slot, k], sem.at[slot]))
    return copies


def _stream_pages(pt_ref, pools, bufs, sems, pp):
    step = pl.program_id(0) * pl.num_programs(1) + pl.program_id(1)
    total = pl.num_programs(0) * pl.num_programs(1)
    slot = step % 2

    @pl.when(step == 0)
    def _():
        for c in _page_copies(pt_ref, pools, bufs, sems, step, slot, pp):
            c.start()

    @pl.when(step + 1 < total)
    def _():
        for c in _page_copies(pt_ref, pools, bufs, sems, step + 1, 1 - slot, pp):
            c.start()

    for c in _page_copies(pt_ref, pools, bufs, sems, step, slot, pp):
        c.wait()
    return slot


def _softmax_update(sc, v, m_scr, l_scr, acc_scr, v_transposed=False):
    m_old = m_scr[...]
    m_new = jnp.maximum(m_old, jnp.max(sc, axis=-1, keepdims=True))
    p = jnp.exp(sc - m_new)
    alpha = jnp.exp(m_old - m_new)
    l_scr[...] = alpha * l_scr[...] + jnp.sum(p, axis=-1, keepdims=True)
    pv = _dot_nt(p.astype(BF16), v) if v_transposed else _dot(p.astype(BF16), v)
    acc_scr[...] = alpha * acc_scr[...] + pv
    m_scr[...] = m_new


def _mla_decode_kernel(pt_ref, q_ref, knew_ref, ckv_hbm, kr_hbm, o_ref, cbuf, rbuf, csem, rsem,
                       kscr, krscr, m_scr, l_scr, acc_scr, *, pp, td, kv_rank):
    slot = _stream_pages(pt_ref, (ckv_hbm, kr_hbm), (cbuf, rbuf), (csem, rsem), pp)
    s = pl.program_id(1)
    q = q_ref[...]
    rows = q.shape[0]

    @pl.when(s == 0)
    def _():
        m_scr[...] = jnp.full_like(m_scr, NEG)
        l_scr[...] = jnp.zeros_like(l_scr)
        acc_scr[...] = jnp.zeros_like(acc_scr)
        kn = knew_ref[...]
        trow = _vdiv(_iota((rows, 1), 0), MLA_HEADS)
        col = _iota((1, kn.shape[0]), 1)
        sc = jnp.where((col <= trow) & (col < td), _dot_nt(q, kn), NEG)
        _softmax_update(sc, kn[:, :kv_rank], m_scr, l_scr, acc_scr)

    for k in range(pp):
        kscr[k * PAGE_SIZE:(k + 1) * PAGE_SIZE, :] = cbuf[slot, k].astype(BF16)
        krscr[:, k * PAGE_SIZE:(k + 1) * PAGE_SIZE] = rbuf[slot, k].astype(BF16)
    kt = kscr[...]
    sc = _dot_nt(q[:, :kv_rank], kt) + _dot(q[:, kv_rank:kv_rank + MLA_D_ROPE], krscr[...])
    _softmax_update(sc, kt, m_scr, l_scr, acc_scr)

    @pl.when(s == pl.num_programs(1) - 1)
    def _():
        o_ref[...] = acc_scr[...] / l_scr[...]


def _mla_decode(pt, qd, knew, ckv_pool, kr_pool, *, td, pp):
    bd, rows, qw = qd.shape
    n_pages = pt.shape[0] // bd
    kv_rank = ckv_pool.shape[-1]
    per_b = lambda b, s, pt: (b, 0, 0)
    any_spec = pl.BlockSpec(memory_space=pl.ANY)
    in_specs = [pl.BlockSpec((None, rows, qw), per_b), pl.BlockSpec((None,) + knew.shape[1:], per_b), any_spec, any_spec]
    kern = functools.partial(_mla_decode_kernel, pp=pp, td=td, kv_rank=kv_rank)
    return pl.pallas_call(
        kern,
        grid_spec=pltpu.PrefetchScalarGridSpec(
            num_scalar_prefetch=1, grid=(bd, n_pages // pp), in_specs=in_specs,
            out_specs=pl.BlockSpec((None, rows, kv_rank), per_b),
            scratch_shapes=[pltpu.VMEM((2, pp) + ckv_pool.shape[1:], F32), pltpu.VMEM((2, pp) + kr_pool.shape[1:], F32),
                            pltpu.SemaphoreType.DMA((2,)), pltpu.SemaphoreType.DMA((2,)),
                            pltpu.VMEM((pp * PAGE_SIZE, kv_rank), BF16), pltpu.VMEM((MLA_D_ROPE, pp * PAGE_SIZE), BF16),
                            pltpu.VMEM((rows, 1), F32), pltpu.VMEM((rows, 1), F32), pltpu.VMEM((rows, kv_rank), F32)]),
        out_shape=jax.ShapeDtypeStruct((bd, rows, kv_rank), F32),
        compiler_params=_cparams("arbitrary", "arbitrary"), name="mla_decode",
    )(pt, qd, knew, ckv_pool, kr_pool)


def _alibi_rows(rows, td):
    r = _iota((rows, 1), 0)
    return _vdiv(r, td), _vmod(r, td)


def _slope_rows(hrow, g):
    slope = jnp.zeros(hrow.shape, F32)
    for hh in range(NSA_HPG):
        slope = jnp.where(hrow == hh, SLOPES[g * NSA_HPG + hh], slope)
    return slope


def _cmp_decode_kernel(pt_ref, q_ref, k_hbm, v_hbm, w1k_ref, w1v_ref, pk_ref, pv_ref, w2k_ref, w2v_ref,
                       o_ref, score_ref, kbuf, vbuf, ksem, vsem, kp_scr, vp_scr, yk_scr, yv_scr,
                       *, pp, td, past, n_cmp, n_sel_pad):
    slot = _stream_pages(pt_ref, (k_hbm, v_hbm), (kbuf, vbuf), (ksem, vsem), pp)
    s = pl.program_id(1)
    cpp = PAGE_SIZE // CMP_STRIDE
    for k in range(pp):
        kp_scr[k * PAGE_SIZE:(k + 1) * PAGE_SIZE, :] = kbuf[slot, k].T
        vp_scr[k * PAGE_SIZE:(k + 1) * PAGE_SIZE, :] = vbuf[slot, k].T
    chunk_rows = lambda scr: jnp.concatenate(
        [scr[pl.ds(j, pp * cpp, stride=CMP_STRIDE), :] for j in range(CMP_STRIDE)], axis=1).astype(BF16)
    xk = chunk_rows(kp_scr)
    xv = chunk_rows(vp_scr)
    r0 = pl.multiple_of(s * (pp * cpp), pp * cpp)
    yk_scr[pl.ds(r0, pp * cpp), :] = _dot(xk, w1k_ref[...])
    yv_scr[pl.ds(r0, pp * cpp), :] = _dot(xv, w1v_ref[...])

    @pl.when(s == pl.num_programs(1) - 1)
    def _():
        kcc = _compress_finish(yk_scr[...], _dot(pk_ref[...], w1k_ref[...]), w2k_ref).astype(BF16)
        vcc = _compress_finish(yv_scr[...], _dot(pv_ref[...], w1v_ref[...]), w2v_ref).astype(BF16)
        ncp = kcc.shape[0]
        rows = NSA_HPG * td
        hrow, trow = _alibi_rows(rows, td)
        cidx = _iota((1, ncp), 1)
        dist = (past + trow) - (cidx * CMP_STRIDE + CMP_BLOCK - 1)
        mask = (dist >= 0) & (cidx < n_cmp)
        distf = dist.astype(F32)
        tsum = (_vmod(_iota((8, rows), 1), td) == _iota((8, rows), 0)).astype(F32)
        ov = _overlap(ncp, n_sel_pad)
        t8 = _iota((8, 1), 0)
        cur = _vdiv(past + t8, SLC_BLOCK)
        jj = _iota((8, n_sel_pad), 1)
        for g in range(NSA_GROUPS):
            sc = _dot_nt(q_ref[g], kcc) - _slope_rows(hrow, g) * distf
            sc = jnp.where(mask, sc, NEG)
            m = jnp.max(sc, axis=-1, keepdims=True)
            p = jnp.where(mask, jnp.exp(sc - m), 0.0)
            l = jnp.sum(p, axis=-1, keepdims=True)
            p = p / jnp.where(l > 0.0, l, 1.0)
            o_ref[g] = _dot(p.astype(BF16), vcc)
            imp = _dot_exact(tsum, p)
            score_ref[g] = _force_scores(_dot_exact(imp, ov), cur, jj)


def _cmp_decode(pt, qd, k_pool, v_pool, wts, *, td, pp, past, n_sel_pad):
    bd = qd.shape[0]
    n_pages = pt.shape[0] // bd
    cpp = PAGE_SIZE // CMP_STRIDE
    nch = n_pages * cpp
    n_cmp = (past + td) // CMP_STRIDE - CMP_BLOCK // CMP_STRIDE + 1
    assert (past + td) // CMP_STRIDE == nch, "new rows must not complete a chunk"
    per_b = lambda b, s, pt: (b, 0, 0, 0)
    c2 = lambda b, s, pt: (0, 0)
    rows = qd.shape[2]
    in_specs = ([pl.BlockSpec((None,) + qd.shape[1:], per_b)]
                + [pl.BlockSpec(memory_space=pl.ANY) for _ in range(2)]
                + [pl.BlockSpec((CHUNK_FEATS, 2 * LANE), c2) for _ in range(2)]
                + [pl.BlockSpec((8, CHUNK_FEATS), c2) for _ in range(2)]
                + [pl.BlockSpec((LANE, LANE), c2) for _ in range(2)])
    kern = functools.partial(_cmp_decode_kernel, pp=pp, td=td, past=past, n_cmp=n_cmp, n_sel_pad=n_sel_pad)
    return pl.pallas_call(
        kern,
        grid_spec=pltpu.PrefetchScalarGridSpec(
            num_scalar_prefetch=1, grid=(bd, n_pages // pp), in_specs=in_specs,
            out_specs=[pl.BlockSpec((None, NSA_GROUPS, rows, LANE), per_b),
                       pl.BlockSpec((None, NSA_GROUPS, 8, n_sel_pad), per_b)],
            scratch_shapes=[pltpu.VMEM((2, pp) + k_pool.shape[1:], F32) for _ in range(2)]
            + [pltpu.SemaphoreType.DMA((2,)) for _ in range(2)]
            + [pltpu.VMEM((pp * PAGE_SIZE, LANE), F32) for _ in range(2)]
            + [pltpu.VMEM((nch, 2 * LANE), F32) for _ in range(2)]),
        out_shape=[jax.ShapeDtypeStruct((bd, NSA_GROUPS, rows, LANE), F32),
                   jax.ShapeDtypeStruct((bd, NSA_GROUPS, 8, n_sel_pad), F32)],
        compiler_params=_cparams("arbitrary", "arbitrary"), name="cmp_decode",
    )(pt, qd, k_pool, v_pool, wts['cmp_w1k'], wts['cmp_w1v'], wts['cmp_posk'], wts['cmp_posv'],
      wts['cmp_w2k'], wts['cmp_w2v'])


def _rank_kernel(score_ref, cur_ref, sel_ref, *, n_sel):
    sc = score_ref[...]
    jj = _iota(sc.shape, 0)

    def body(i, rank):
        ri = score_ref[pl.ds(i, 1), :]
        beats = (ri > sc) | ((ri == sc) & (i < jj))
        return rank + beats.astype(F32)

    rank = lax.fori_loop(0, n_sel, body, jnp.zeros(sc.shape, F32))
    sel = (rank < float(min(SLC_TOP_N, n_sel))) & (jj <= cur_ref[...])
    sel_ref[...] = sel.astype(F32)


def _rank(score_t, cur, *, n_sel):
    full = lambda a: pl.BlockSpec(a.shape, lambda: (0,) * a.ndim)
    return pl.pallas_call(
        functools.partial(_rank_kernel, n_sel=n_sel), in_specs=[full(score_t), full(cur)],
        out_specs=full(score_t), out_shape=jax.ShapeDtypeStruct(score_t.shape, F32), name="rank_decode",
    )(score_t, cur)


def _slc_decode_kernel(pt_ref, q_ref, sel_ref, selnew_ref, knew_ref, vnew_ref, k_hbm, v_hbm, o_ref,
                       kbuf, vbuf, ksem, vsem, kscr, vscr, m_scr, l_scr, acc_scr, *, pp, td, past):
    slot = _stream_pages(pt_ref, (k_hbm, v_hbm), (kbuf, vbuf), (ksem, vsem), pp)
    s = pl.program_id(1)
    rows = q_ref.shape[1]
    hrow, trow = _alibi_rows(rows, td)
    tk = pp * PAGE_SIZE
    nblk = tk // SLC_BLOCK

    @pl.when(s == 0)
    def _():
        m_scr[...] = jnp.full_like(m_scr, NEG)
        l_scr[...] = jnp.zeros_like(l_scr)
        acc_scr[...] = jnp.zeros_like(acc_scr)
        kn = knew_ref[...]
        vn = vnew_ref[...]
        col = _iota((1, kn.shape[0]), 1)
        dist = trow - col
        for g in range(NSA_GROUPS):
            sc = _dot_nt(q_ref[g], kn) - _slope_rows(hrow, g) * dist.astype(F32)
            mask = (dist >= 0) & (col < td) & (selnew_ref[g] > 0.5)
            _softmax_update(jnp.where(mask, sc, NEG), vn, m_scr.at[g], l_scr.at[g], acc_scr.at[g])

    for k in range(pp):
        kscr[:, k * PAGE_SIZE:(k + 1) * PAGE_SIZE] = kbuf[slot, k].astype(BF16)
        vscr[:, k * PAGE_SIZE:(k + 1) * PAGE_SIZE] = vbuf[slot, k].astype(BF16)
    kt = kscr[...]
    vt = vscr[...]
    kpos = s * tk + _iota((1, tk), 1)
    distf = ((past + trow) - kpos).astype(F32)
    expand = (_vdiv(_iota((nblk, tk), 1), SLC_BLOCK) == _iota((nblk, tk), 0)).astype(BF16)
    for g in range(NSA_GROUPS):
        sc = _dot(q_ref[g], kt) - _slope_rows(hrow, g) * distf
        selx = _dot(sel_ref[g], expand) > 0.5
        _softmax_update(jnp.where(selx, sc, NEG), vt, m_scr.at[g], l_scr.at[g], acc_scr.at[g], v_transposed=True)

    @pl.when(s == pl.num_programs(1) - 1)
    def _():
        o_ref[...] = acc_scr[...] / l_scr[...]


def _slc_decode(pt, qd, sel16, selnew, knew, vnew, k_pool, v_pool, *, td, pp, past):
    bd = qd.shape[0]
    rows = qd.shape[2]
    n_pages = pt.shape[0] // bd
    per_b = lambda b, s, pt: (b, 0, 0, 0)
    per_b3 = lambda b, s, pt: (b, 0, 0)
    in_specs = ([pl.BlockSpec((None,) + qd.shape[1:], per_b),
                 pl.BlockSpec((None, None) + sel16.shape[2:], lambda b, s, pt: (b, s, 0, 0, 0)),
                 pl.BlockSpec((None,) + selnew.shape[1:], per_b),
                 pl.BlockSpec((None,) + knew.shape[1:], per_b3), pl.BlockSpec((None,) + vnew.shape[1:], per_b3)]
                + [pl.BlockSpec(memory_space=pl.ANY) for _ in range(2)])
    kern = functools.partial(_slc_decode_kernel, pp=pp, td=td, past=past)
    return pl.pallas_call(
        kern,
        grid_spec=pltpu.PrefetchScalarGridSpec(
            num_scalar_prefetch=1, grid=(bd, n_pages // pp), in_specs=in_specs,
            out_specs=pl.BlockSpec((None, NSA_GROUPS, rows, LANE), per_b),
            scratch_shapes=[pltpu.VMEM((2, pp) + k_pool.shape[1:], F32) for _ in range(2)]
            + [pltpu.SemaphoreType.DMA((2,)) for _ in range(2)]
            + [pltpu.VMEM((LANE, pp * PAGE_SIZE), BF16) for _ in range(2)]
            + [pltpu.VMEM((NSA_GROUPS, rows, 1), F32) for _ in range(2)] + [pltpu.VMEM((NSA_GROUPS, rows, LANE), F32)]),
        out_shape=jax.ShapeDtypeStruct((bd, NSA_GROUPS, rows, LANE), F32),
        compiler_params=_cparams("arbitrary", "arbitrary"), name="slc_decode",
    )(pt, qd, sel16, selnew, knew, vnew, k_pool, v_pool)


def _win_decode_kernel(q_ref, kst_ref, vst_ref, knew_ref, vnew_ref, o_ref, *, td, past):
    rows = q_ref.shape[1]
    hrow, trow = _alibi_rows(rows, td)
    nbuf = kst_ref.shape[1]
    kst = kst_ref[...].astype(BF16)
    vst = vst_ref[...].astype(BF16)
    kn = knew_ref[...]
    vn = vnew_ref[...]
    kpos = past - nbuf + _iota((1, nbuf), 1)
    d1 = (past + trow) - kpos
    m1 = (d1 >= 0) & (d1 < WINDOW) & (kpos >= 0)
    col = _iota((1, kn.shape[0]), 1)
    d2 = trow - col
    m2 = (d2 >= 0) & (d2 < WINDOW) & (col < td)
    for g in range(NSA_GROUPS):
        slope = _slope_rows(hrow, g)
        s1 = jnp.where(m1, _dot(q_ref[g], kst) - slope * d1.astype(F32), NEG)
        s2 = jnp.where(m2, _dot_nt(q_ref[g], kn) - slope * d2.astype(F32), NEG)
        m = jnp.maximum(jnp.max(s1, axis=-1, keepdims=True), jnp.max(s2, axis=-1, keepdims=True))
        p1 = jnp.exp(s1 - m)
        p2 = jnp.exp(s2 - m)
        l = jnp.sum(p1, axis=-1, keepdims=True) + jnp.sum(p2, axis=-1, keepdims=True)
        o_ref[g] = (_dot_nt(p1.astype(BF16), vst) + _dot(p2.astype(BF16), vn)) / l


def _win_decode(qd, kst, vst, knew, vnew, *, td, past):
    bd = qd.shape[0]
    rows = qd.shape[2]
    per_b = lambda b: (b, 0, 0, 0)
    per_b3 = lambda b: (b, 0, 0)
    blk3 = lambda a: pl.BlockSpec((None,) + a.shape[1:], per_b3)
    return pl.pallas_call(
        functools.partial(_win_decode_kernel, td=td, past=past), grid=(bd,),
        in_specs=[pl.BlockSpec((None,) + qd.shape[1:], per_b), blk3(kst), blk3(vst), blk3(knew), blk3(vnew)],
        out_specs=pl.BlockSpec((None, NSA_GROUPS, rows, LANE), per_b),
        out_shape=jax.ShapeDtypeStruct((bd, NSA_GROUPS, rows, LANE), F32),
        compiler_params=_cparams("parallel"), name="win_decode",
    )(qd, kst, vst, knew, vnew)


def _pad_rows(a, rows):
    return jnp.pad(a, ((0, 0), (0, rows - a.shape[1]), (0, 0)))


def _sample(x_sample, caches, page_table, w):
    (c_ckv, c_krope, c_cmp_k, c_cmp_v, c_slc_k, c_slc_v, s_win_k, s_win_v, s_conv) = caches
    bd, td, d = x_sample.shape
    n = bd * td
    n_pages = page_table.shape[1]
    past = n_pages * PAGE_SIZE
    n_pool = c_ckv.shape[0]
    kv_rank = c_ckv.shape[-1]
    x = x_sample.reshape(n, d)
    pos = past + jnp.arange(td, dtype=jnp.int32)
    cosq, sinq = _rope_tables(jnp.tile(pos, bd))
    (qrot, qabs, ckv, krp, qn, kc, vc, ks, vs, kw, vw, gn, ga, gb) = _inproj(
        x, w, cosq, sinq, sample=True, tm=n, tab_blocks=1)
    pt = page_table.reshape(-1)
    krope = krp[:, MLA_D_NOPE:MLA_D_NOPE + MLA_D_ROPE]
    qrope = qrot.reshape(bd, td, MLA_HEADS, HEAD_PAD)[..., MLA_D_NOPE:MLA_D_NOPE + MLA_D_ROPE]
    qpad = LANE - MLA_D_ROPE
    qd_mla = jnp.concatenate([qabs.reshape(bd, td, MLA_HEADS, kv_rank), qrope,
                              jnp.zeros((bd, td, MLA_HEADS, qpad), F32)], axis=-1)
    qd_mla = qd_mla.reshape(bd, td * MLA_HEADS, kv_rank + LANE).astype(BF16)
    knew = jnp.concatenate([ckv, krope, jnp.zeros((n, qpad), F32)], axis=-1).reshape(bd, td, -1)
    knew = _pad_rows(knew, 8).astype(BF16)
    pos_minor = lambda c: jnp.moveaxis(c, 1, -1).reshape(c.shape[0], -1, c.shape[1])
    o_lat = _mla_decode(pt, qd_mla, knew, c_ckv, pos_minor(c_krope), td=td, pp=min(32, n_pages))
    qg = qn.astype(F32).reshape(bd, td, NSA_GROUPS, NSA_HPG, NSA_DH).transpose(0, 2, 3, 1, 4)
    qg = qg.reshape(bd, NSA_GROUPS, NSA_HPG * td, NSA_DH)
    lane_g = (jnp.arange(LANE) // NSA_DH)[None, :] == jnp.arange(NSA_GROUPS)[:, None]
    qd = jnp.where(lane_g[None, :, None, :], jnp.tile(qg, (1, 1, 1, NSA_GROUPS)), 0.0).astype(BF16)
    n_sel = -(-(past + td) // SLC_BLOCK)
    n_sel_pad = -(-n_sel // LANE) * LANE
    o_cmp, score = _cmp_decode(pt, qd, pos_minor(c_cmp_k), pos_minor(c_cmp_v),
                               w, td=td, pp=min(32, n_pages), past=past, n_sel_pad=n_sel_pad)
    cur = jnp.broadcast_to((pos // SLC_BLOCK).astype(jnp.int32), (bd, NSA_GROUPS, td)).reshape(1, -1)
    n_sel8 = -(-n_sel // 8) * 8
    sel_t = _rank(score[:, :, :td, :n_sel8].reshape(-1, n_sel8).T, cur, n_sel=n_sel)
    sel = jnp.pad(sel_t.T, ((0, 0), (0, n_sel_pad - n_sel8))).reshape(bd, NSA_GROUPS, td, n_sel_pad)
    pp_slc = min(32, n_pages)
    bps = pp_slc * PAGE_SIZE // SLC_BLOCK
    n_past_blk = past // SLC_BLOCK
    sel_past = sel[..., :n_past_blk].reshape(bd, NSA_GROUPS, td, n_past_blk // bps, bps).transpose(0, 3, 1, 2, 4)
    sel16 = jnp.tile(sel_past, (1, 1, 1, NSA_HPG, 1)).astype(BF16)
    selnew = jnp.tile(jnp.broadcast_to(sel[..., n_past_blk:n_past_blk + 1], (bd, NSA_GROUPS, td, 8)), (1, 1, NSA_HPG, 1))
    new8 = lambda a: _pad_rows(a.reshape(bd, td, LANE), 8).astype(BF16)
    o_slc = _slc_decode(pt, qd, sel16, selnew, new8(ks), new8(vs), pos_minor(c_slc_k), pos_minor(c_slc_v),
                        td=td, pp=pp_slc, past=past)
    nbuf = s_win_k.shape[1]
    o_win = _win_decode(qd, pos_minor(s_win_k), pos_minor(s_win_v), new8(kw), new8(vw),
                        td=td, past=past)

    def heads_out(o):
        o = o.reshape(bd, NSA_GROUPS, NSA_HPG, td, NSA_GROUPS, NSA_DH)
        o = jnp.stack([o[:, g, :, :, g] for g in range(NSA_GROUPS)], axis=1)
        return o.transpose(0, 3, 1, 2, 4).reshape(n, NSA_HEADS * NSA_DH)

    dff = w['w_gate'].shape[1]
    prev1 = jnp.zeros((bd, td, dff), F32).at[:, 0].set(s_conv[:, 1])
    prev2 = jnp.zeros((bd, td, dff), F32).at[:, 0].set(s_conv[:, 0]).at[:, 1].set(s_conv[:, 1])
    y, g = _finish(x, o_lat.reshape(n, MLA_HEADS * kv_rank), heads_out(o_cmp), heads_out(o_slc), heads_out(o_win),
                   gn, ga, gb, (prev1.reshape(n, dff), prev2.reshape(n, dff)), w, tm=n, period=td, full_g=True,
                   latent=True)
    kv4 = lambda a: a.reshape(1, bd, td, NSA_GROUPS, NSA_DH)
    win = lambda st, new: jnp.concatenate([st, new.reshape(bd, td, NSA_GROUPS, NSA_DH)], axis=1)[None, :, -nbuf:]
    conv_state = jnp.concatenate([s_conv, g.reshape(bd, td, dff)], axis=1)[None, :, -(CONV_W - 1):]
    states = (ckv.reshape(1, bd, td, kv_rank), krope.reshape(1, bd, td, MLA_D_ROPE), kv4(kc), kv4(vc), kv4(ks), kv4(vs),
              win(s_win_k, kw), win(s_win_v, vw), conv_state)
    return y.reshape(bd, td, d), states


def kernel(x_prompt, x_sample, cache_mla_ckv, cache_mla_krope, cache_nsa_cmp_k, cache_nsa_cmp_v, cache_nsa_slc_k, cache_nsa_slc_v, state_win_k, state_win_v, state_ffn_conv, page_table, norm1_g, w_in, q_norm_g, kv_norm_g, w_uq, w_uk, w_uv, cmp_pos_k, cmp_w1_k, cmp_w2_k, cmp_pos_v, cmp_w1_v, cmp_w2_v, w_proj_mla, w_proj_nsa, w_out, norm2_g, w_gate, w_up, conv_w, conv_b, w_down, norm_f_g):
    assert norm1_g.shape[0] == 1, "single-layer trunk"
    p = dict(norm1_g=norm1_g[0], w_in=w_in[0], q_norm_g=q_norm_g[0], kv_norm_g=kv_norm_g[0], w_uq=w_uq[0],
             w_uk=w_uk[0], w_uv=w_uv[0], cmp_pos_k=cmp_pos_k[0], cmp_w1_k=cmp_w1_k[0], cmp_w2_k=cmp_w2_k[0],
             cmp_pos_v=cmp_pos_v[0], cmp_w1_v=cmp_w1_v[0], cmp_w2_v=cmp_w2_v[0], w_proj_mla=w_proj_mla[0],
             w_proj_nsa=w_proj_nsa[0], w_out=w_out[0], norm2_g=norm2_g[0], w_gate=w_gate[0], w_up=w_up[0],
             conv_w=conv_w[0], conv_b=conv_b[0], w_down=w_down[0], norm_f_g=norm_f_g)
    w = _prep_weights(p)
    y_p, ps = _prompt(x_prompt, w)
    caches = (cache_mla_ckv[0], cache_mla_krope[0], cache_nsa_cmp_k[0], cache_nsa_cmp_v[0], cache_nsa_slc_k[0],
              cache_nsa_slc_v[0], state_win_k[0], state_win_v[0], state_ffn_conv[0])
    y_s, ss = _sample(x_sample, caches, page_table, w)
    out = [y_p, y_s]
    for a, b in zip(ps, ss):
        out += [a, b]
    return tuple(out)
```

```python
import functools

import numpy as np
import jax
import jax.numpy as jnp
from jax import lax
from jax.experimental import pallas as pl
from jax.experimental.pallas import tpu as pltpu

MLA_HEADS = 8
MLA_D_NOPE = 64
MLA_D_ROPE = 32
MLA_D_V = 64
ROPE_THETA = 10000.0
MLA_SCALE = (MLA_D_NOPE + MLA_D_ROPE) ** -0.5
NSA_HEADS = 8
NSA_GROUPS = 2
NSA_HPG = NSA_HEADS // NSA_GROUPS
NSA_DH = 64
NSA_SCALE = NSA_DH ** -0.5
CMP_BLOCK = 32
CMP_STRIDE = 16
SLC_BLOCK = 64
SLC_TOP_N = 16
WINDOW = 512
CONV_W = 3
PAGE_SIZE = 128
EPS = 1e-6
NEG = -1e30
FORCE = 1e9

LOG2E = 1.4426950408889634
LANE = 128
AUX_POS_HI, AUX_POS_LO, AUX_BLK0 = 64, 65, 72
HEAD_PAD = 128
CHUNK_FEATS = CMP_STRIDE * NSA_GROUPS * NSA_DH
FFN_BLOCK = 1536
SLOPES = tuple(float(2.0 ** (-8.0 * (h + 1) / NSA_HEADS)) for h in range(NSA_HEADS))
VMEM_LIMIT = 56 * 1024 * 1024

F32 = jnp.float32
BF16 = jnp.bfloat16
_NT = (((1,), (1,)), ((), ()))


def _cparams(*sem):
    return pltpu.CompilerParams(dimension_semantics=sem, vmem_limit_bytes=VMEM_LIMIT)


def _rms(x, g):
    return x * lax.rsqrt(jnp.mean(x * x, axis=-1, keepdims=True) + EPS) * g


def _dot(a, b):
    return jnp.dot(a, b, preferred_element_type=F32)


def _dot_nt(a, b):
    return lax.dot_general(a, b, _NT, preferred_element_type=F32)


def _dot_exact(a, b):
    return jnp.dot(a, b, preferred_element_type=F32, precision=lax.Precision.HIGHEST)


def _iota(shape, dim):
    return lax.broadcasted_iota(jnp.int32, shape, dim)


def _log2(n):
    assert n > 0 and n & (n - 1) == 0, n
    return n.bit_length() - 1


def _vdiv(x, n):
    return lax.shift_right_logical(x, jnp.full(x.shape, _log2(n), jnp.int32))


def _vmod(x, n):
    assert n & (n - 1) == 0, n
    return x & (n - 1)


_O_CQ, _O_CKV, _O_QN, _O_K6, _O_GA = 0, 384, 640, 1152, 1920


def _inproj_kernel(x_ref, g1_ref, w_ref, qg_ref, kvg_ref, wuq_ref, wk_ref, wv_ref, cos_ref, sin_ref,
                   *outs, sample, q_rank, kv_rank, d_model, tiles_per_seq, q_scale):
    o_gb = _O_GA + d_model
    o_kr = o_gb + d_model
    x = x_ref[...]
    hn = _rms(x, g1_ref[...])
    y = _dot(hn.astype(BF16), w_ref[...])
    cosq = cos_ref[...]
    sinq = sin_ref[...]
    nq = MLA_HEADS * HEAD_PAD
    cqn = _rms(y[:, _O_CQ:_O_CQ + q_rank], qg_ref[...])
    q2 = _dot(cqn.astype(BF16), wuq_ref[...])
    ckv = _rms(y[:, _O_CKV:_O_CKV + kv_rank], kvg_ref[...])
    kr = y[:, o_kr:o_kr + LANE] * cosq + y[:, o_kr + LANE:o_kr + 2 * LANE] * sinq
    ckv_b = ckv.astype(BF16)
    it = iter(outs)
    if sample:
        qrot_ref, qabs_ref = next(it), next(it)
        for h in range(MLA_HEADS):
            sl = slice(h * HEAD_PAD, (h + 1) * HEAD_PAD)
            qh = (q2[:, sl] * cosq + q2[:, nq + h * HEAD_PAD:nq + (h + 1) * HEAD_PAD] * sinq) * q_scale
            qrot_ref[:, sl] = qh
            qabs_ref[:, h * kv_rank:(h + 1) * kv_rank] = _dot(qh.astype(BF16), wk_ref[h])
    else:
        q_ref, k_ref, v_ref = next(it), next(it), next(it)
        knp = _dot(ckv_b, wk_ref[...])
        for h in range(MLA_HEADS):
            sl = slice(h * HEAD_PAD, (h + 1) * HEAD_PAD)
            qh = (q2[:, sl] * cosq + q2[:, nq + h * HEAD_PAD:nq + (h + 1) * HEAD_PAD] * sinq) * q_scale
            q_ref[:, sl] = qh.astype(BF16)
            k_ref[:, sl] = (knp[:, sl] + kr).astype(BF16)
        v_ref[...] = _dot(ckv_b, wv_ref[...]).astype(BF16)
    ckv_ref, kr_ref, qn_ref = next(it), next(it), next(it)
    ckv_ref[...] = ckv
    kr_ref[...] = kr
    yq = y[:, _O_QN:_O_QN + NSA_HEADS * NSA_DH] * NSA_SCALE
    if sample:
        qn_ref[...] = yq.astype(BF16)
    else:
        lane_q = _iota((x.shape[0], LANE), 1)
        for h in range(NSA_HEADS):
            pair = yq[:, (h // 2) * LANE:(h // 2 + 1) * LANE]
            if h % 2:
                pair = pltpu.roll(pair, NSA_DH, 1)
            aux_q = jnp.where((lane_q == AUX_POS_HI) | (lane_q == AUX_POS_LO), SLOPES[h], 0.0)
            qn_ref[:, h * LANE:(h + 1) * LANE] = jnp.where(lane_q < NSA_DH, pair, aux_q).astype(BF16)
    for j in range(6):
        yj = y[:, _O_K6 + j * LANE:_O_K6 + (j + 1) * LANE]
        if sample:
            next(it)[...] = yj
        else:
            next(it)[...] = yj.T
    if not sample:
        tm = x.shape[0]
        pos = (pl.program_id(0) % tiles_per_seq) * tm + _iota((tm, 1), 0)
        lane = _iota((tm, LANE), 1)
        onehot = ((lane >= AUX_BLK0) & (_vdiv(pos, SLC_BLOCK) == lane - AUX_BLK0)).astype(F32)
        aux = jnp.where(lane == AUX_POS_HI, (pos - _vmod(pos, 256)).astype(F32),
                        jnp.where(lane == AUX_POS_LO, _vmod(pos, 256).astype(F32), onehot))
        for j in range(6):
            yj = y[:, _O_K6 + j * LANE:_O_K6 + (j + 1) * LANE]
            ref = next(it)
            if j in (2, 4):
                ref[:, :LANE] = jnp.where(lane < NSA_DH, yj, aux).astype(BF16)
                ref[:, LANE:] = jnp.where(lane < NSA_DH, pltpu.roll(yj, NSA_DH, 1), aux).astype(BF16)
            else:
                ref[...] = yj.astype(BF16)
    gn_ref, ga_ref, gb_ref = next(it), next(it), next(it)
    gn_ref[...] = jax.nn.sigmoid(y[:, o_kr + 2 * LANE:o_kr + 3 * LANE])
    ga_ref[...] = jax.nn.sigmoid(y[:, _O_GA:_O_GA + d_model])
    gb_ref[...] = jax.nn.sigmoid(y[:, o_gb:o_gb + d_model])


def _inproj(x, wts, cosq, sinq, *, sample, tm, tab_blocks):
    n, d = x.shape
    q_rank, kv_rank = wts['q_norm_g'].shape[1], wts['kv_norm_g'].shape[1]
    nq = MLA_HEADS * HEAD_PAD
    wk = wts['w_ukT'] if sample else wts['w_ukp']
    row = lambda i: (i, 0)
    const2 = lambda i: (0, 0)
    tab_map = (lambda i: (i % tab_blocks, 0))
    in_specs = [
        pl.BlockSpec((tm, d), row),
        pl.BlockSpec((1, d), const2),
        pl.BlockSpec(wts['w_in'].shape, const2),
        pl.BlockSpec((1, q_rank), const2),
        pl.BlockSpec((1, kv_rank), const2),
        pl.BlockSpec(wts['w_uq2'].shape, const2),
        pl.BlockSpec(wk.shape, (lambda i: (0, 0, 0)) if sample else const2),
        pl.BlockSpec(wts['w_uvf'].shape, const2),
        pl.BlockSpec((tm, LANE), tab_map),
        pl.BlockSpec((tm, LANE), tab_map),
    ]
    shapes = []
    if sample:
        shapes += [(nq, F32), (MLA_HEADS * kv_rank, F32)]
    else:
        shapes += [(nq, BF16), (nq, BF16), (MLA_HEADS * MLA_D_V, BF16)]
    shapes += [(kv_rank, F32), (LANE, F32), (NSA_HEADS * (NSA_DH if sample else LANE), BF16)]
    n_lead = len(shapes)
    shapes += [(LANE, F32)] * 6
    if not sample:
        shapes += [(LANE, BF16), (LANE, BF16), (2 * LANE, BF16), (LANE, BF16), (2 * LANE, BF16), (LANE, BF16)]
        assert AUX_BLK0 + -(-tab_blocks * tm // SLC_BLOCK) <= LANE, "block one-hot must fit the aux lanes"
    shapes += [(LANE, F32), (d, F32), (d, F32)]
    out_shape = [jax.ShapeDtypeStruct((n, w), dt) for w, dt in shapes]
    out_specs = [pl.BlockSpec((tm, w), row) for w, _ in shapes]
    if not sample:
        tpb = tab_blocks
        for j in range(n_lead, n_lead + 6):
            out_shape[j] = jax.ShapeDtypeStruct((n // (tpb * tm), LANE, tpb * tm), F32)
            out_specs[j] = pl.BlockSpec((None, LANE, tm), lambda i: (i // tpb, 0, i % tpb))
    q_scale = MLA_SCALE if sample else MLA_SCALE * LOG2E
    kern = functools.partial(_inproj_kernel, sample=sample, q_rank=q_rank, kv_rank=kv_rank, d_model=d,
                             tiles_per_seq=tab_blocks, q_scale=q_scale)
    return pl.pallas_call(
        kern, grid=(n // tm,), in_specs=in_specs, out_specs=out_specs, out_shape=out_shape,
        compiler_params=_cparams("parallel"), name="inproj_sample" if sample else "inproj_prompt",
    )(x, wts['norm1_g'], wts['w_in'], wts['q_norm_g'], wts['kv_norm_g'], wts['w_uq2'], wk, wts['w_uvf'],
      cosq, sinq)


def _stack_heads(q_ref, heads, width):
    parts = [q_ref[:, h * width:(h + 1) * width] for h in heads]
    return parts[0] if len(parts) == 1 else jnp.concatenate(parts, axis=0)


def _flash_kernel(*refs, tq, tk, groups, par, dq, dk, dv, window, has_bias, base2):
    if has_bias:
        q_ref, k_ref, v_ref, bias_ref, o_ref = refs
    else:
        q_ref, k_ref, v_ref, o_ref = refs
    q_start = pl.program_id(1) * tq
    n_hi = (q_start + tq - 1) // tk + 1
    hi_full = (q_start + 1) // tk
    if window:
        n_lo = jnp.maximum(q_start - (window - 1), 0) // tk
        lo_full = (jnp.maximum(q_start + tq - window, 0) + tk - 1) // tk
    else:
        n_lo, lo_full = 0, 0
    e1 = jnp.clip(lo_full, n_lo, n_hi)
    e2 = jnp.clip(hi_full, e1, n_hi)
    ex = jnp.exp2 if base2 else jnp.exp
    for c0 in range(0, len(groups), par):
        chunk = groups[c0:c0 + par]
        qs = []
        for heads, kcol in chunk:
            qg = _stack_heads(q_ref, heads, dq)
            if has_bias:
                bias = bias_ref[:, kcol * LANE:(kcol + 1) * LANE]
                qg = qg + jnp.concatenate([bias] * len(heads), axis=0)
            qs.append(qg)
        rows = qs[0].shape[0]
        qpos = q_start + (_iota((rows, 1), 0) & (tq - 1))

        def step(j, carry, masked):
            k0 = pl.multiple_of(j * tk, tk)
            if masked:
                dist = qpos - (k0 + _iota((1, tk), 1))
                mask = dist >= 0
                if window:
                    mask = mask & (dist < window)
            out = []
            for (heads, kcol), qg, (m, l, acc) in zip(chunk, qs, carry):
                kt = k_ref[pl.ds(k0, tk), kcol * dk:(kcol + 1) * dk]
                vt = v_ref[pl.ds(k0, tk), kcol * dv:(kcol + 1) * dv]
                s = _dot_nt(qg, kt)
                if masked:
                    s = jnp.where(mask, s, NEG)
                m_new = jnp.maximum(m, jnp.max(s, axis=-1, keepdims=True))
                p = ex(s - m_new)
                alpha = ex(m - m_new)
                l = alpha * l + jnp.sum(p, axis=-1, keepdims=True)
                acc = alpha * acc + _dot(p.astype(BF16), vt)
                out.append((m_new, l, acc))
            return tuple(out)

        carry = tuple((jnp.full((rows, 1), NEG, F32), jnp.zeros((rows, 1), F32), jnp.zeros((rows, dv), F32))
                      for _ in chunk)
        carry = lax.fori_loop(n_lo, e1, functools.partial(step, masked=True), carry)
        carry = lax.fori_loop(e1, e2, functools.partial(step, masked=False), carry)
        carry = lax.fori_loop(e2, n_hi, functools.partial(step, masked=True), carry)
        for (heads, kcol), (m, l, acc) in zip(chunk, carry):
            o = acc * (1.0 / l)
            for hh, h in enumerate(heads):
                o_ref[:, h * dv:(h + 1) * dv] = o[hh * tq:(hh + 1) * tq].astype(o_ref.dtype)


def _flash(q, k, v, bias, *, batch, seq, tq, tk, groups, par, dq, dk, dv, window, base2, name):
    n = q.shape[0]
    has_bias = bias is not None
    n_heads = sum(len(g[0]) for g in groups)
    qrow = lambda b, i: (b * (seq // tq) + i, 0)
    kv = lambda b, i: (b, 0)
    in_specs = [pl.BlockSpec((tq, q.shape[1]), qrow), pl.BlockSpec((seq, k.shape[1]), kv),
                pl.BlockSpec((seq, v.shape[1]), kv)]
    args = [q, k, v]
    if has_bias:
        in_specs.append(pl.BlockSpec((tq, bias.shape[1]), qrow))
        args.append(bias)
    kern = functools.partial(_flash_kernel, tq=tq, tk=tk, groups=groups, par=par, dq=dq, dk=dk, dv=dv, window=window,
                             has_bias=has_bias, base2=base2)
    return pl.pallas_call(
        kern, grid=(batch, seq // tq), in_specs=in_specs,
        out_specs=pl.BlockSpec((tq, n_heads * dv), qrow),
        out_shape=jax.ShapeDtypeStruct((n, n_heads * dv), BF16),
        compiler_params=_cparams("parallel", "arbitrary"), name=name,
    )(*args)


def _win_kernel(q_ref, k_ref, v_ref, o_ref, *, tq, groups, dv):
    i = pl.program_id(1)
    rq = _iota((tq, tq), 0)
    ck = _iota((tq, tq), 1)
    bias_own = jnp.where(ck <= rq, 0.0, NEG)
    bias_far = jnp.where(ck > rq, 0.0, NEG) + jnp.where(i >= 2, 0.0, NEG)
    bias_mid = jnp.where(i >= 1, 0.0, NEG)
    starts = (jnp.maximum(i - 2, 0) * tq, jnp.maximum(i - 1, 0) * tq, i * tq)
    for heads, kcol in groups:
        qg = _stack_heads(q_ref, heads, LANE)
        nh = len(heads)
        ss, vs = [], []
        for k0, bias in zip(starts, (bias_far, None, bias_own)):
            k0 = pl.multiple_of(k0, tq)
            s = _dot_nt(qg, k_ref[pl.ds(k0, tq), kcol * LANE:(kcol + 1) * LANE])
            s = s + (bias_mid if bias is None else jnp.concatenate([bias] * nh, axis=0))
            ss.append(s)
            vs.append(v_ref[pl.ds(k0, tq), kcol * dv:(kcol + 1) * dv])
        m = functools.reduce(jnp.maximum, [jnp.max(s, axis=-1, keepdims=True) for s in ss])
        ps = [jnp.exp(s - m) for s in ss]
        l = functools.reduce(jnp.add, [jnp.sum(p, axis=-1, keepdims=True) for p in ps])
        acc = functools.reduce(jnp.add, [_dot(p.astype(BF16), v) for p, v in zip(ps, vs)])
        o = acc * (1.0 / l)
        for hh, h in enumerate(heads):
            o_ref[:, h * dv:(h + 1) * dv] = o[hh * tq:(hh + 1) * tq].astype(o_ref.dtype)


def _win_prompt(q, k, v, *, batch, seq, tq, groups, dv):
    assert WINDOW == 2 * tq and seq % tq == 0
    n = q.shape[0]
    n_heads = sum(len(g[0]) for g in groups)
    qrow = lambda b, i: (b * (seq // tq) + i, 0)
    kv = lambda b, i: (b, 0)
    return pl.pallas_call(
        functools.partial(_win_kernel, tq=tq, groups=groups, dv=dv), grid=(batch, seq // tq),
        in_specs=[pl.BlockSpec((tq, q.shape[1]), qrow), pl.BlockSpec((seq, k.shape[1]), kv),
                  pl.BlockSpec((seq, v.shape[1]), kv)],
        out_specs=pl.BlockSpec((tq, n_heads * dv), qrow),
        out_shape=jax.ShapeDtypeStruct((n, n_heads * dv), BF16),
        compiler_params=_cparams("parallel", "arbitrary"), name="win_prompt",
    )(q, k, v)


def _compress_rows(xk, xv, w1k_ref, w1v_ref, posk_ref, posv_ref):
    yk = _dot(xk.astype(BF16), w1k_ref[...])
    yv = _dot(xv.astype(BF16), w1v_ref[...])
    return yk, yv


def _compress_finish(y, posy, w2_ref):
    rows = y.shape[0]
    a = y[:, :LANE]
    b = pltpu.roll(y[:, LANE:], rows - 1, 0)
    pos = posy[0:1, :LANE] + posy[1:2, LANE:]
    hid = jax.nn.gelu(a + b + pos)
    return _dot(hid.astype(BF16), w2_ref[...])


def _compress_prompt_kernel(xk_ref, xv_ref, w1k_ref, w1v_ref, pk_ref, pv_ref, w2k_ref, w2v_ref, ok_ref, ov_ref):
    yk, yv = _compress_rows(xk_ref[...], xv_ref[...], w1k_ref, w1v_ref, pk_ref, pv_ref)
    ok_ref[...] = _compress_finish(yk, _dot(pk_ref[...], w1k_ref[...]), w2k_ref).astype(BF16)
    ov_ref[...] = _compress_finish(yv, _dot(pv_ref[...], w1v_ref[...]), w2v_ref).astype(BF16)


def _compress_prompt(kc, vc, wts, *, batch, seq):
    nch = seq // CMP_STRIDE
    xk = kc.reshape(batch * nch, CHUNK_FEATS)
    xv = vc.reshape(batch * nch, CHUNK_FEATS)
    row = lambda b: (b, 0)
    c2 = lambda b: (0, 0)
    wspec = pl.BlockSpec((CHUNK_FEATS, 2 * LANE), c2)
    pspec = pl.BlockSpec((8, CHUNK_FEATS), c2)
    w2spec = pl.BlockSpec((LANE, LANE), c2)
    return pl.pallas_call(
        _compress_prompt_kernel, grid=(batch,),
        in_specs=[pl.BlockSpec((nch, CHUNK_FEATS), row), pl.BlockSpec((nch, CHUNK_FEATS), row),
                  wspec, wspec, pspec, pspec, w2spec, w2spec],
        out_specs=[pl.BlockSpec((nch, LANE), row)] * 2,
        out_shape=[jax.ShapeDtypeStruct((batch * nch, LANE), BF16)] * 2,
        compiler_params=_cparams("parallel"), name="compress_prompt",
    )(xk, xv, wts['cmp_w1k'], wts['cmp_w1v'], wts['cmp_posk'], wts['cmp_posv'], wts['cmp_w2k'], wts['cmp_w2v'])


def _overlap(n_rows, n_sel):
    c = _iota((n_rows, n_sel), 0) * CMP_STRIDE
    j = _iota((n_rows, n_sel), 1) * SLC_BLOCK
    return ((c < j + SLC_BLOCK) & (c + CMP_BLOCK > j)).astype(F32)


def _force_scores(score, cur, jj):
    forced = (jj == 0) | (jj == cur) | (jj == cur - 1)
    score = jnp.where(forced, FORCE, score)
    return jnp.where(jj <= cur, score, NEG)


def _cmp_prompt_kernel(q_ref, k_ref, v_ref, o_ref, bias_ref, *, tq, n_cmp, n_sel):
    q_start = pl.program_id(1) * tq
    ncp = k_ref.shape[0]
    rows = NSA_HPG * tq
    qpos = q_start + (_iota((rows, 1), 0) & (tq - 1))
    cidx = _iota((1, ncp), 1)
    dist = qpos - (cidx * CMP_STRIDE + CMP_BLOCK - 1)
    mask = (dist >= 0) & (cidx < n_cmp)
    distf = dist.astype(F32)
    hrow = _vdiv(_iota((rows, 1), 0), tq)
    nsp = -(-n_sel // 8) * 8
    cur = _vdiv(q_start + _iota((1, tq), 1), SLC_BLOCK)
    jj = _iota((nsp, tq), 0)
    cb = _iota((nsp, ncp), 1) * CMP_STRIDE
    jb = _iota((nsp, ncp), 0) * SLC_BLOCK
    ov_t = ((cb < jb + SLC_BLOCK) & (cb + CMP_BLOCK > jb)).astype(F32)
    for g in range(NSA_GROUPS):
        heads = range(g * NSA_HPG, (g + 1) * NSA_HPG)
        qg = jnp.concatenate([q_ref[:, h * LANE:h * LANE + NSA_DH] for h in heads], axis=0)
        slope = jnp.zeros((rows, 1), F32)
        for hh, h in enumerate(heads):
            slope = jnp.where(hrow == hh, SLOPES[h], slope)
        s = _dot_nt(qg, k_ref[:, g * NSA_DH:(g + 1) * NSA_DH]) - slope * distf
        s = jnp.where(mask, s, NEG)
        m = jnp.max(s, axis=-1, keepdims=True)
        p = jnp.where(mask, jnp.exp(s - m), 0.0)
        l = jnp.sum(p, axis=-1, keepdims=True)
        p = p / jnp.where(l > 0.0, l, 1.0)
        o = _dot(p.astype(BF16), v_ref[:, g * NSA_DH:(g + 1) * NSA_DH])
        imp = p[0:tq]
        for hh in range(1, NSA_HPG):
            imp = imp + p[hh * tq:(hh + 1) * tq]
            o_ref[:, (g * NSA_HPG + hh) * NSA_DH:(g * NSA_HPG + hh + 1) * NSA_DH] = o[hh * tq:(hh + 1) * tq].astype(BF16)
        o_ref[:, g * NSA_HPG * NSA_DH:(g * NSA_HPG + 1) * NSA_DH] = o[0:tq].astype(BF16)
        score = lax.dot_general(ov_t, imp, _NT, preferred_element_type=F32, precision=lax.Precision.HIGHEST)
        score = _force_scores(score, cur, jj)
        rank = jnp.zeros((nsp, tq), F32)
        for i in range(n_sel):
            ri = score[i:i + 1, :]
            beats = (ri > score) | ((ri == score) & (i < jj))
            rank = rank + beats.astype(F32)
        sel = (rank < float(min(SLC_TOP_N, n_sel))) & (jj <= cur)
        bias_t = jnp.where(sel | (jj >= n_sel), 0.0, NEG)
        bias_t = jnp.concatenate([jnp.zeros((AUX_BLK0, tq), F32), bias_t,
                                  jnp.zeros((LANE - AUX_BLK0 - nsp, tq), F32)], axis=0)
        bias_ref[:, g * LANE:(g + 1) * LANE] = bias_t.T.astype(BF16)


def _cmp_prompt(qn, kcc, vcc, *, batch, seq, tq):
    n = qn.shape[0]
    nch = seq // CMP_STRIDE
    n_cmp = nch - CMP_BLOCK // CMP_STRIDE + 1
    n_sel = -(-seq // SLC_BLOCK)
    qrow = lambda b, i: (b * (seq // tq) + i, 0)
    kv = lambda b, i: (b, 0)
    kern = functools.partial(_cmp_prompt_kernel, tq=tq, n_cmp=n_cmp, n_sel=n_sel)
    return pl.pallas_call(
        kern, grid=(batch, seq // tq),
        in_specs=[pl.BlockSpec((tq, qn.shape[1]), qrow), pl.BlockSpec((nch, LANE), kv), pl.BlockSpec((nch, LANE), kv)],
        out_specs=[pl.BlockSpec((tq, NSA_HEADS * NSA_DH), qrow), pl.BlockSpec((tq, NSA_GROUPS * LANE), qrow)],
        out_shape=[jax.ShapeDtypeStruct((n, NSA_HEADS * NSA_DH), BF16),
                   jax.ShapeDtypeStruct((n, NSA_GROUPS * LANE), BF16)],
        compiler_params=_cparams("parallel", "arbitrary"), name="cmp_prompt",
    )(qn, kcc, vcc)


def _finish_kernel(*refs, period, latent, has_state):
    it = iter(refs)
    x_ref, omla_ref, ocmp_ref, oslc_ref, owin_ref, gn_ref, ga_ref, gb_ref = (next(it) for _ in range(8))
    prev1_ref, prev2_ref = (next(it), next(it)) if has_state else (None, None)
    gx_ref = next(it)
    wuv_ref = next(it) if latent else None
    (wpm_ref, wpn_ref, wo_ref, g2_ref, wg_ref, wu_ref, cw_ref, cb_ref, wd_ref, gf_ref,
     y_ref, gout_ref, carry_ref) = it
    tm = x_ref.shape[0]
    gn = gn_ref[...]
    gn_hi = gn.astype(BF16)
    gn_lo = (gn - gn_hi.astype(F32)).astype(BF16)
    gexp = _dot(gn_hi, gx_ref[...]) + _dot(gn_lo, gx_ref[...])
    w = NSA_HEADS * NSA_DH
    o_nsa = (gexp[:, 0:w] * ocmp_ref[...].astype(F32) + gexp[:, w:2 * w] * oslc_ref[...].astype(F32)
             + gexp[:, 2 * w:3 * w] * owin_ref[...].astype(F32))
    o_mla = omla_ref[...].astype(BF16)
    if latent:
        o_mla = _dot(o_mla, wuv_ref[...]).astype(BF16)
    merged = (ga_ref[...] * _dot(o_mla, wpm_ref[...])
              + gb_ref[...] * _dot(o_nsa.astype(BF16), wpn_ref[...]))
    x1 = x_ref[...] + _dot(merged.astype(BF16), wo_ref[...])
    h2 = _rms(x1, g2_ref[...]).astype(BF16)
    row = _iota((tm, 1), 0)
    i = pl.program_id(0)
    t = _vmod(i * tm + row, period)
    carried = period > tm
    if carried:
        @pl.when(i == 0)
        def _():
            carry_ref[...] = jnp.zeros_like(carry_ref)
    dff = wg_ref.shape[1]
    x2 = x1
    for c0 in range(0, dff, FFN_BLOCK):
        cs = slice(c0, min(c0 + FFN_BLOCK, dff))
        g = _dot(h2, wg_ref[:, cs])
        u = _dot(h2, wu_ref[:, cs])
        g1 = pltpu.roll(g, 1, 0)
        g2 = pltpu.roll(g, 2, 0)
        if carried:
            c = carry_ref[:, cs]
            g1 = jnp.where(row == 0, c[7:8], g1)
            g2 = jnp.where(row == 0, c[6:7], jnp.where(row == 1, c[7:8], g2))
            carry_ref[:, cs] = g[tm - 8:tm]
        g1 = jnp.where(t >= 1, g1, prev1_ref[:, cs] if has_state else 0.0)
        g2 = jnp.where(t >= 2, g2, prev2_ref[:, cs] if has_state else 0.0)
        cw = cw_ref[:, cs]
        conv = cb_ref[:, cs] + cw[0:1] * g2 + cw[1:2] * g1 + cw[2:3] * g
        act = (jax.nn.silu(conv) * u).astype(BF16)
        x2 = x2 + _dot(act, wd_ref[cs, :])
        gout_ref[:, cs] = g[tm - 8:tm] if gout_ref.shape[0] == 8 else g
    y_ref[...] = _rms(x2, gf_ref[...])


def _finish(x, omla, ocmp, oslc, owin, gn, ga, gb, state_rows, wts, *, tm, period, full_g, latent):
    n, d = x.shape
    dff = wts['w_gate'].shape[1]
    row = lambda i: (i, 0)
    c2 = lambda i: (0, 0)
    acts = [x, omla, ocmp, oslc, owin, gn, ga, gb] + (list(state_rows) if state_rows is not None else [])
    consts = [wts['gate_expand']] + ([wts['w_uvbd']] if latent else []) + [
        wts['w_proj_mla'], wts['w_proj_nsa'], wts['w_out'], wts['norm2_g'],
        wts['w_gate'], wts['w_up'], wts['conv_w'], wts['conv_b'], wts['w_down'], wts['norm_f_g']]
    ins = acts + consts
    in_specs = [pl.BlockSpec((tm, a.shape[1]), row) for a in acts] + [pl.BlockSpec(a.shape, c2) for a in consts]
    g_rows = n if full_g else (n // tm) * 8
    g_blk = tm if full_g else 8
    kern = functools.partial(_finish_kernel, period=period, latent=latent, has_state=state_rows is not None)
    return pl.pallas_call(
        kern, grid=(n // tm,), in_specs=in_specs,
        out_specs=[pl.BlockSpec((tm, d), row), pl.BlockSpec((g_blk, dff), row)],
        out_shape=[jax.ShapeDtypeStruct((n, d), F32), jax.ShapeDtypeStruct((g_rows, dff), F32)],
        scratch_shapes=[pltpu.VMEM((8, dff), F32)],
        compiler_params=_cparams("arbitrary"), name="finish_full" if full_g else "finish_tiled",
    )(*ins)


def _swap_halves(w):
    hlf = w.shape[-1] // 2
    return jnp.concatenate([-w[..., hlf:], w[..., :hlf]], axis=-1)


def _prep_weights(p):
    d = p['w_in'].shape[0]
    q_rank, kv_rank = p['q_norm_g'].shape[-1], p['kv_norm_g'].shape[-1]
    sizes = [q_rank, kv_rank, MLA_D_ROPE, NSA_HEADS * NSA_DH] + [2 * NSA_GROUPS * NSA_DH] * 3 + [3 * NSA_HEADS, d, d]
    cuts = np.cumsum(sizes)[:-1].tolist()
    cq, ckv, kr, qn, kvc, kvs, kvw, gn, ga, gb = jnp.split(p['w_in'], cuts, axis=-1)
    assert _O_CKV == q_rank and _O_QN == q_rank + kv_rank
    lo, hi = MLA_D_NOPE, HEAD_PAD - MLA_D_NOPE - MLA_D_ROPE
    place = lambda w: jnp.pad(w, ((0, 0), (lo, hi)))
    gnp = jnp.pad(gn, ((0, 0), (0, LANE - gn.shape[1])))
    w_in = jnp.concatenate([cq, ckv, qn, kvc, kvs, kvw, ga, gb, place(kr), place(_swap_halves(kr)), gnp], axis=1)
    w = {'w_in': w_in.astype(BF16)}
    for k in ('norm1_g', 'q_norm_g', 'kv_norm_g', 'norm2_g', 'conv_b'):
        w[k] = p[k].reshape(1, -1)
    w['norm_f_g'] = p['norm_f_g'].reshape(1, -1)
    w['conv_w'] = jnp.pad(p['conv_w'], ((0, 8 - CONV_W), (0, 0)))
    uq = p['w_uq']
    hpad = ((0, 0), (0, 0), (0, HEAD_PAD - uq.shape[-1]))
    uq_a = jnp.pad(uq, hpad)
    uq_b = jnp.pad(jnp.concatenate([jnp.zeros_like(uq[..., :MLA_D_NOPE]), _swap_halves(uq[..., MLA_D_NOPE:])], -1), hpad)
    w['w_uq2'] = jnp.concatenate([uq_a.reshape(q_rank, -1), uq_b.reshape(q_rank, -1)], axis=1).astype(BF16)
    uk = p['w_uk']
    w['w_ukp'] = jnp.pad(uk, ((0, 0), (0, 0), (0, HEAD_PAD - MLA_D_NOPE))).reshape(kv_rank, -1).astype(BF16)
    w['w_ukT'] = jnp.pad(jnp.transpose(uk, (1, 2, 0)), ((0, 0), (0, HEAD_PAD - MLA_D_NOPE), (0, 0))).astype(BF16)
    w['w_uvf'] = p['w_uv'].reshape(kv_rank, -1).astype(BF16)
    eye_h = jnp.eye(MLA_HEADS, dtype=F32)
    w['w_uvbd'] = jnp.einsum('rhv,hk->hrkv', p['w_uv'], eye_h).reshape(MLA_HEADS * kv_rank, -1).astype(BF16)
    eye_g = jnp.eye(NSA_GROUPS, dtype=F32)
    for nm in ('k', 'v'):
        w1 = p['cmp_w1_' + nm].reshape(2, CMP_STRIDE, NSA_DH, -1)
        big = jnp.einsum('ajdh,gk->jgdakh', w1, eye_g)
        w['cmp_w1' + nm] = big.reshape(CHUNK_FEATS, -1).astype(BF16)
        pos = p['cmp_pos_' + nm].reshape(2, CMP_STRIDE, 1, NSA_DH)
        pos = jnp.broadcast_to(pos, (2, CMP_STRIDE, NSA_GROUPS, NSA_DH)).reshape(2, CHUNK_FEATS)
        w['cmp_pos' + nm] = jnp.pad(pos, ((0, 6), (0, 0))).astype(BF16)
        w2 = p['cmp_w2_' + nm]
        w['cmp_w2' + nm] = jnp.einsum('hd,gk->ghkd', w2, eye_g).reshape(NSA_GROUPS * w2.shape[0], -1).astype(BF16)
    ge = np.zeros((LANE, 3 * NSA_HEADS * NSA_DH), np.float32)
    for h in range(NSA_HEADS):
        for i in range(3):
            ge[h * 3 + i, i * NSA_HEADS * NSA_DH + h * NSA_DH:i * NSA_HEADS * NSA_DH + (h + 1) * NSA_DH] = 1.0
    w['gate_expand'] = jnp.asarray(ge)
    for k in ('w_proj_mla', 'w_proj_nsa', 'w_out', 'w_gate', 'w_up', 'w_down'):
        w[k] = p[k].astype(BF16)
    return w


def _rope_tables(pos):
    inv = ROPE_THETA ** (-jnp.arange(0, MLA_D_ROPE, 2, dtype=F32) / MLA_D_ROPE)
    ang = pos.astype(F32)[:, None] * inv[None, :]
    cos, sin = jnp.cos(ang), jnp.sin(ang)
    n = pos.shape[0]
    pad = jnp.zeros((n, HEAD_PAD - MLA_D_NOPE - MLA_D_ROPE), F32)
    cosq = jnp.concatenate([jnp.ones((n, MLA_D_NOPE), F32), cos, cos, pad], axis=1)
    sinq = jnp.concatenate([jnp.zeros((n, MLA_D_NOPE), F32), sin, sin, pad], axis=1)
    return cosq, sinq


_NSA_GROUPS_SPEC = tuple((tuple(range(g * NSA_HPG, (g + 1) * NSA_HPG)), g) for g in range(NSA_GROUPS))
_MLA_GROUPS_SPEC = tuple(((h,), h) for h in range(MLA_HEADS))


def _prompt(x_prompt, w):
    b, t, d = x_prompt.shape
    n = b * t
    x = x_prompt.reshape(n, d)
    tm = 256
    cosq, sinq = _rope_tables(jnp.arange(t, dtype=jnp.int32))
    (q_mla, k_mla, v_mla, ckv, krp, qn, kc, vc, ks, vs, kw, vw, kc_b, vc_b, ks_b, vs_b, kw_b, vw_b, gn, ga, gb) = _inproj(
        x, w, cosq, sinq, sample=False, tm=tm, tab_blocks=t // tm)
    o_mla = _flash(q_mla, k_mla, v_mla, None, batch=b, seq=t, tq=512, tk=512, groups=_MLA_GROUPS_SPEC, par=8,
                   dq=HEAD_PAD, dk=HEAD_PAD, dv=MLA_D_V, window=0, base2=True, name="mla_prompt")
    kcc, vcc = _compress_prompt(kc_b, vc_b, w, batch=b, seq=t)
    o_cmp, sel_bias = _cmp_prompt(qn, kcc, vcc, batch=b, seq=t, tq=128)
    o_slc = _flash(qn, ks_b, vs_b, sel_bias, batch=b, seq=t, tq=256, tk=512, groups=_NSA_GROUPS_SPEC, par=2,
                   dq=LANE, dk=LANE, dv=NSA_DH, window=0, base2=False, name="slc_prompt")
    o_win = _win_prompt(qn, kw_b, vw_b, batch=b, seq=t, tq=WINDOW // 2, groups=_NSA_GROUPS_SPEC, dv=NSA_DH)
    dff = w['w_gate'].shape[1]
    y, gtail = _finish(x, o_mla, o_cmp, o_slc, o_win, gn, ga, gb, None, w, tm=tm, period=t, full_g=False,
                       latent=False)
    kv4 = lambda a: a.reshape(1, b, NSA_GROUPS, NSA_DH, a.shape[-1]).transpose(0, 1, 4, 2, 3)
    n_keep = min(WINDOW, t)
    kw, vw = kw[:, :, t - n_keep:], vw[:, :, t - n_keep:]
    conv_state = gtail.reshape(b, t // tm, 8, dff)[:, -1, 8 - (CONV_W - 1):, :]
    states = (ckv.reshape(1, b, t, -1), krp[:, MLA_D_NOPE:MLA_D_NOPE + MLA_D_ROPE].reshape(1, b, t, MLA_D_ROPE),
              kv4(kc), kv4(vc), kv4(ks), kv4(vs), kv4(kw), kv4(vw), conv_state[None])
    return y.reshape(b, t, d), states


def _page_copies(pt_ref, pools, bufs, sems, step, slot, pp):
    copies = []
    for k in range(pp):
        page = pt_ref[step * pp + k]
        for pool, buf, sem in zip(pools, bufs, sems):
            copies.append(pltpu.make_async_copy(pool.at[page], buf.at[slot, k], sem.at[slot]))
    return copies


def _start_all(copies, n_pools):
    for i, c in enumerate(copies):
        c.start(priority=(i // n_pools) % 2)


def _stream_pages(pt_ref, pools, bufs, sems, pp):
    step = pl.program_id(0) * pl.num_programs(1) + pl.program_id(1)
    total = pl.num_programs(0) * pl.num_programs(1)
    slot = step % 2

    @pl.when(step == 0)
    def _():
        _start_all(_page_copies(pt_ref, pools, bufs, sems, step, slot, pp), len(pools))

    @pl.when(step + 1 < total)
    def _():
        _start_all(_page_copies(pt_ref, pools, bufs, sems, step + 1, 1 - slot, pp), len(pools))

    for c in _page_copies(pt_ref, pools, bufs, sems, step, slot, pp):
        c.wait()
    return slot


def _softmax_update(sc, v, m_scr, l_scr, acc_scr, v_transposed=False):
    m_old = m_scr[...]
    m_new = jnp.maximum(m_old, jnp.max(sc, axis=-1, keepdims=True))
    p = jnp.exp(sc - m_new)
    alpha = jnp.exp(m_old - m_new)
    l_scr[...] = alpha * l_scr[...] + jnp.sum(p, axis=-1, keepdims=True)
    pv = _dot_nt(p.astype(BF16), v) if v_transposed else _dot(p.astype(BF16), v)
    acc_scr[...] = alpha * acc_scr[...] + pv
    m_scr[...] = m_new


def _mla_decode_kernel(pt_ref, q_ref, knew_ref, ckv_hbm, kr_hbm, o_ref, cbuf, rbuf, csem, rsem,
                       kscr, krscr, m_scr, l_scr, acc_scr, *, pp, td, kv_rank):
    slot = _stream_pages(pt_ref, (ckv_hbm, kr_hbm), (cbuf, rbuf), (csem, rsem), pp)
    s = pl.program_id(1)
    q = q_ref[...]
    rows = q.shape[0]

    @pl.when(s == 0)
    def _():
        m_scr[...] = jnp.full_like(m_scr, NEG)
        l_scr[...] = jnp.zeros_like(l_scr)
        acc_scr[...] = jnp.zeros_like(acc_scr)
        kn = knew_ref[...]
        trow = _vdiv(_iota((rows, 1), 0), MLA_HEADS)
        col = _iota((1, kn.shape[0]), 1)
        sc = jnp.where((col <= trow) & (col < td), _dot_nt(q, kn), NEG)
        _softmax_update(sc, kn[:, :kv_rank], m_scr, l_scr, acc_scr)

    for k in range(pp):
        kscr[k * PAGE_SIZE:(k + 1) * PAGE_SIZE, :] = cbuf[slot, k].astype(BF16)
        krscr[:, k * PAGE_SIZE:(k + 1) * PAGE_SIZE] = rbuf[slot, k].astype(BF16)
    kt = kscr[...]
    sc = _dot_nt(q[:, :kv_rank], kt) + _dot(q[:, kv_rank:kv_rank + MLA_D_ROPE], krscr[...])
    _softmax_update(sc, kt, m_scr, l_scr, acc_scr)

    @pl.when(s == pl.num_programs(1) - 1)
    def _():
        o_ref[...] = acc_scr[...] / l_scr[...]


def _mla_decode(pt, qd, knew, ckv_pool, kr_pool, *, td, pp):
    bd, rows, qw = qd.shape
    n_pages = pt.shape[0] // bd
    kv_rank = ckv_pool.shape[-1]
    per_b = lambda b, s, pt: (b, 0, 0)
    any_spec = pl.BlockSpec(memory_space=pl.ANY)
    in_specs = [pl.BlockSpec((None, rows, qw), per_b), pl.BlockSpec((None,) + knew.shape[1:], per_b), any_spec, any_spec]
    kern = functools.partial(_mla_decode_kernel, pp=pp, td=td, kv_rank=kv_rank)
    return pl.pallas_call(
        kern,
        grid_spec=pltpu.PrefetchScalarGridSpec(
            num_scalar_prefetch=1, grid=(bd, n_pages // pp), in_specs=in_specs,
            out_specs=pl.BlockSpec((None, rows, kv_rank), per_b),
            scratch_shapes=[pltpu.VMEM((2, pp) + ckv_pool.shape[1:], F32), pltpu.VMEM((2, pp) + kr_pool.shape[1:], F32),
                            pltpu.SemaphoreType.DMA((2,)), pltpu.SemaphoreType.DMA((2,)),
                            pltpu.VMEM((pp * PAGE_SIZE, kv_rank), BF16), pltpu.VMEM((MLA_D_ROPE, pp * PAGE_SIZE), BF16),
                            pltpu.VMEM((rows, 1), F32), pltpu.VMEM((rows, 1), F32), pltpu.VMEM((rows, kv_rank), F32)]),
        out_shape=jax.ShapeDtypeStruct((bd, rows, kv_rank), F32),
        compiler_params=_cparams("arbitrary", "arbitrary"), name="mla_decode",
    )(pt, qd, knew, ckv_pool, kr_pool)


def _alibi_rows(rows, td):
    r = _iota((rows, 1), 0)
    return _vdiv(r, td), _vmod(r, td)


def _slope_rows(hrow, g):
    slope = jnp.zeros(hrow.shape, F32)
    for hh in range(NSA_HPG):
        slope = jnp.where(hrow == hh, SLOPES[g * NSA_HPG + hh], slope)
    return slope


def _cmp_decode_kernel(pt_ref, q_ref, k_hbm, v_hbm, w1k_ref, w1v_ref, pk_ref, pv_ref, w2k_ref, w2v_ref,
                       o_ref, score_ref, kbuf, vbuf, ksem, vsem, kp0, kp1, vp0, vp1, yk_scr, yv_scr,
                       *, pp, td, past, n_cmp, n_sel_pad):
    slot = _stream_pages(pt_ref, (k_hbm, v_hbm), (kbuf, vbuf), (ksem, vsem), pp)
    s = pl.program_id(1)
    cpp = PAGE_SIZE // CMP_STRIDE
    hp = pp // 2
    for half, (kp_scr, vp_scr) in enumerate(((kp0, vp0), (kp1, vp1))):
        for k in range(hp):
            kp_scr[k * PAGE_SIZE:(k + 1) * PAGE_SIZE, :] = kbuf[slot, half * hp + k].T
            vp_scr[k * PAGE_SIZE:(k + 1) * PAGE_SIZE, :] = vbuf[slot, half * hp + k].T
    chunk_rows = lambda scr: jnp.concatenate(
        [scr[pl.ds(j, hp * cpp, stride=CMP_STRIDE), :] for j in range(CMP_STRIDE)], axis=1).astype(BF16)
    for half, (kp_scr, vp_scr) in enumerate(((kp0, vp0), (kp1, vp1))):
        r0 = pl.multiple_of(s * (pp * cpp) + half * (hp * cpp), hp * cpp)
        yk_scr[pl.ds(r0, hp * cpp), :] = _dot(chunk_rows(kp_scr), w1k_ref[...])
        yv_scr[pl.ds(r0, hp * cpp), :] = _dot(chunk_rows(vp_scr), w1v_ref[...])

    @pl.when(s == pl.num_programs(1) - 1)
    def _():
        kcc = _compress_finish(yk_scr[...], _dot(pk_ref[...], w1k_ref[...]), w2k_ref).astype(BF16)
        vcc = _compress_finish(yv_scr[...], _dot(pv_ref[...], w1v_ref[...]), w2v_ref).astype(BF16)
        ncp = kcc.shape[0]
        rows = NSA_HPG * td
        hrow, trow = _alibi_rows(rows, td)
        cidx = _iota((1, ncp), 1)
        dist = (past + trow) - (cidx * CMP_STRIDE + CMP_BLOCK - 1)
        mask = (dist >= 0) & (cidx < n_cmp)
        distf = dist.astype(F32)
        tsum = (_vmod(_iota((8, rows), 1), td) == _iota((8, rows), 0)).astype(F32)
        ov = _overlap(ncp, n_sel_pad)
        t8 = _iota((8, 1), 0)
        cur = _vdiv(past + t8, SLC_BLOCK)
        jj = _iota((8, n_sel_pad), 1)
        for g in range(NSA_GROUPS):
            sc = _dot_nt(q_ref[g], kcc) - _slope_rows(hrow, g) * distf
            sc = jnp.where(mask, sc, NEG)
            m = jnp.max(sc, axis=-1, keepdims=True)
            p = jnp.where(mask, jnp.exp(sc - m), 0.0)
            l = jnp.sum(p, axis=-1, keepdims=True)
            p = p / jnp.where(l > 0.0, l, 1.0)
            o_ref[g] = _dot(p.astype(BF16), vcc)
            imp = _dot_exact(tsum, p)
            score_ref[g] = _force_scores(_dot_exact(imp, ov), cur, jj)


def _cmp_decode(pt, qd, k_pool, v_pool, wts, *, td, pp, past, n_sel_pad):
    bd = qd.shape[0]
    n_pages = pt.shape[0] // bd
    cpp = PAGE_SIZE // CMP_STRIDE
    nch = n_pages * cpp
    n_cmp = (past + td) // CMP_STRIDE - CMP_BLOCK // CMP_STRIDE + 1
    assert (past + td) // CMP_STRIDE == nch, "new rows must not complete a chunk"
    per_b = lambda b, s, pt: (b, 0, 0, 0)
    c2 = lambda b, s, pt: (0, 0)
    rows = qd.shape[2]
    in_specs = ([pl.BlockSpec((None,) + qd.shape[1:], per_b)]
                + [pl.BlockSpec(memory_space=pl.ANY) for _ in range(2)]
                + [pl.BlockSpec((CHUNK_FEATS, 2 * LANE), c2) for _ in range(2)]
                + [pl.BlockSpec((8, CHUNK_FEATS), c2) for _ in range(2)]
                + [pl.BlockSpec((LANE, LANE), c2) for _ in range(2)])
    kern = functools.partial(_cmp_decode_kernel, pp=pp, td=td, past=past, n_cmp=n_cmp, n_sel_pad=n_sel_pad)
    return pl.pallas_call(
        kern,
        grid_spec=pltpu.PrefetchScalarGridSpec(
            num_scalar_prefetch=1, grid=(bd, n_pages // pp), in_specs=in_specs,
            out_specs=[pl.BlockSpec((None, NSA_GROUPS, rows, LANE), per_b),
                       pl.BlockSpec((None, NSA_GROUPS, 8, n_sel_pad), per_b)],
            scratch_shapes=[pltpu.VMEM((2, pp) + k_pool.shape[1:], F32) for _ in range(2)]
            + [pltpu.SemaphoreType.DMA((2,)) for _ in range(2)]
            + [pltpu.VMEM((pp // 2 * PAGE_SIZE, LANE), F32) for _ in range(4)]
            + [pltpu.VMEM((nch, 2 * LANE), F32) for _ in range(2)]),
        out_shape=[jax.ShapeDtypeStruct((bd, NSA_GROUPS, rows, LANE), F32),
                   jax.ShapeDtypeStruct((bd, NSA_GROUPS, 8, n_sel_pad), F32)],
        compiler_params=_cparams("arbitrary", "arbitrary"), name="cmp_decode",
    )(pt, qd, k_pool, v_pool, wts['cmp_w1k'], wts['cmp_w1v'], wts['cmp_posk'], wts['cmp_posv'],
      wts['cmp_w2k'], wts['cmp_w2v'])


def _rank_kernel(score_ref, cur_ref, sel_ref, *, n_sel):
    sc = score_ref[...]
    jj = _iota(sc.shape, 0)

    def body(i, rank):
        ri = score_ref[pl.ds(i, 1), :]
        beats = (ri > sc) | ((ri == sc) & (i < jj))
        return rank + beats.astype(F32)

    rank = lax.fori_loop(0, n_sel, body, jnp.zeros(sc.shape, F32))
    sel = (rank < float(min(SLC_TOP_N, n_sel))) & (jj <= cur_ref[...])
    sel_ref[...] = sel.astype(F32)


def _rank(score_t, cur, *, n_sel):
    full = lambda a: pl.BlockSpec(a.shape, lambda: (0,) * a.ndim)
    return pl.pallas_call(
        functools.partial(_rank_kernel, n_sel=n_sel), in_specs=[full(score_t), full(cur)],
        out_specs=full(score_t), out_shape=jax.ShapeDtypeStruct(score_t.shape, F32), name="rank_decode",
    )(score_t, cur)


def _slc_decode_kernel(pt_ref, q_ref, sel_ref, selnew_ref, knew_ref, vnew_ref, expand_ref, k_hbm, v_hbm, o_ref,
                       kbuf, vbuf, ksem, vsem, kscr, vscr, m_scr, l_scr, acc_scr, *, pp, td, past):
    slot = _stream_pages(pt_ref, (k_hbm, v_hbm), (kbuf, vbuf), (ksem, vsem), pp)
    s = pl.program_id(1)
    rows = q_ref.shape[1]
    hrow, trow = _alibi_rows(rows, td)
    tk = pp * PAGE_SIZE

    @pl.when(s == 0)
    def _():
        m_scr[...] = jnp.full_like(m_scr, NEG)
        l_scr[...] = jnp.zeros_like(l_scr)
        acc_scr[...] = jnp.zeros_like(acc_scr)
        kn = knew_ref[...]
        vn = vnew_ref[...]
        col = _iota((1, kn.shape[0]), 1)
        dist = trow - col
        for g in range(NSA_GROUPS):
            sc = _dot_nt(q_ref[g], kn) - _slope_rows(hrow, g) * dist.astype(F32)
            mask = (dist >= 0) & (col < td) & (selnew_ref[g] > 0.5)
            _softmax_update(jnp.where(mask, sc, NEG), vn, m_scr.at[g], l_scr.at[g], acc_scr.at[g])

    for k in range(pp):
        kscr[:, k * PAGE_SIZE:(k + 1) * PAGE_SIZE] = kbuf[slot, k].astype(BF16)
        vscr[:, k * PAGE_SIZE:(k + 1) * PAGE_SIZE] = vbuf[slot, k].astype(BF16)
    kt = kscr[...]
    vt = vscr[...]
    kpos = s * tk + _iota((1, tk), 1)
    distf = ((past + trow) - kpos).astype(F32)
    for g in range(NSA_GROUPS):
        sc = _dot(q_ref[g], kt) - _slope_rows(hrow, g) * distf
        selx = _dot(sel_ref[g], expand_ref[...]) > 0.5
        _softmax_update(jnp.where(selx, sc, NEG), vt, m_scr.at[g], l_scr.at[g], acc_scr.at[g], v_transposed=True)

    @pl.when(s == pl.num_programs(1) - 1)
    def _():
        o_ref[...] = acc_scr[...] / l_scr[...]


def _slc_decode(pt, qd, sel16, selnew, knew, vnew, k_pool, v_pool, *, td, pp, past):
    bd = qd.shape[0]
    rows = qd.shape[2]
    n_pages = pt.shape[0] // bd
    tk = pp * PAGE_SIZE
    expand = (jnp.arange(tk)[None, :] // SLC_BLOCK == jnp.arange(tk // SLC_BLOCK)[:, None]).astype(BF16)
    per_b = lambda b, s, pt: (b, 0, 0, 0)
    per_b3 = lambda b, s, pt: (b, 0, 0)
    in_specs = ([pl.BlockSpec((None,) + qd.shape[1:], per_b),
                 pl.BlockSpec((None, None) + sel16.shape[2:], lambda b, s, pt: (b, s, 0, 0, 0)),
                 pl.BlockSpec((None,) + selnew.shape[1:], per_b),
                 pl.BlockSpec((None,) + knew.shape[1:], per_b3), pl.BlockSpec((None,) + vnew.shape[1:], per_b3),
                 pl.BlockSpec(expand.shape, lambda b, s, pt: (0, 0))]
                + [pl.BlockSpec(memory_space=pl.ANY) for _ in range(2)])
    kern = functools.partial(_slc_decode_kernel, pp=pp, td=td, past=past)
    return pl.pallas_call(
        kern,
        grid_spec=pltpu.PrefetchScalarGridSpec(
            num_scalar_prefetch=1, grid=(bd, n_pages // pp), in_specs=in_specs,
            out_specs=pl.BlockSpec((None, NSA_GROUPS, rows, LANE), per_b),
            scratch_shapes=[pltpu.VMEM((2, pp) + k_pool.shape[1:], F32) for _ in range(2)]
            + [pltpu.SemaphoreType.DMA((2,)) for _ in range(2)]
            + [pltpu.VMEM((LANE, pp * PAGE_SIZE), BF16) for _ in range(2)]
            + [pltpu.VMEM((NSA_GROUPS, rows, 1), F32) for _ in range(2)] + [pltpu.VMEM((NSA_GROUPS, rows, LANE), F32)]),
        out_shape=jax.ShapeDtypeStruct((bd, NSA_GROUPS, rows, LANE), F32),
        compiler_params=_cparams("arbitrary", "arbitrary"), name="slc_decode",
    )(pt, qd, sel16, selnew, knew, vnew, expand, k_pool, v_pool)


def _win_decode_kernel(q_ref, kst_ref, vst_ref, knew_ref, vnew_ref, o_ref, *, td, past):
    rows = q_ref.shape[1]
    hrow, trow = _alibi_rows(rows, td)
    nbuf = kst_ref.shape[1]
    kst = kst_ref[...].astype(BF16)
    vst = vst_ref[...].astype(BF16)
    kn = knew_ref[...]
    vn = vnew_ref[...]
    kpos = past - nbuf + _iota((1, nbuf), 1)
    d1 = (past + trow) - kpos
    m1 = (d1 >= 0) & (d1 < WINDOW) & (kpos >= 0)
    col = _iota((1, kn.shape[0]), 1)
    d2 = trow - col
    m2 = (d2 >= 0) & (d2 < WINDOW) & (col < td)
    for g in range(NSA_GROUPS):
        slope = _slope_rows(hrow, g)
        s1 = jnp.where(m1, _dot(q_ref[g], kst) - slope * d1.astype(F32), NEG)
        s2 = jnp.where(m2, _dot_nt(q_ref[g], kn) - slope * d2.astype(F32), NEG)
        m = jnp.maximum(jnp.max(s1, axis=-1, keepdims=True), jnp.max(s2, axis=-1, keepdims=True))
        p1 = jnp.exp(s1 - m)
        p2 = jnp.exp(s2 - m)
        l = jnp.sum(p1, axis=-1, keepdims=True) + jnp.sum(p2, axis=-1, keepdims=True)
        o_ref[g] = (_dot_nt(p1.astype(BF16), vst) + _dot(p2.astype(BF16), vn)) / l


def _win_decode(qd, kst, vst, knew, vnew, *, td, past):
    bd = qd.shape[0]
    rows = qd.shape[2]
    per_b = lambda b: (b, 0, 0, 0)
    per_b3 = lambda b: (b, 0, 0)
    blk3 = lambda a: pl.BlockSpec((None,) + a.shape[1:], per_b3)
    return pl.pallas_call(
        functools.partial(_win_decode_kernel, td=td, past=past), grid=(bd,),
        in_specs=[pl.BlockSpec((None,) + qd.shape[1:], per_b), blk3(kst), blk3(vst), blk3(knew), blk3(vnew)],
        out_specs=pl.BlockSpec((None, NSA_GROUPS, rows, LANE), per_b),
        out_shape=jax.ShapeDtypeStruct((bd, NSA_GROUPS, rows, LANE), F32),
        compiler_params=_cparams("parallel"), name="win_decode",
    )(qd, kst, vst, knew, vnew)


def _pad_rows(a, rows):
    return jnp.pad(a, ((0, 0), (0, rows - a.shape[1]), (0, 0)))


def _sample(x_sample, caches, page_table, w):
    (c_ckv, c_krope, c_cmp_k, c_cmp_v, c_slc_k, c_slc_v, s_win_k, s_win_v, s_conv) = caches
    bd, td, d = x_sample.shape
    n = bd * td
    n_pages = page_table.shape[1]
    past = n_pages * PAGE_SIZE
    n_pool = c_ckv.shape[0]
    kv_rank = c_ckv.shape[-1]
    x = x_sample.reshape(n, d)
    pos = past + jnp.arange(td, dtype=jnp.int32)
    cosq, sinq = _rope_tables(jnp.tile(pos, bd))
    (qrot, qabs, ckv, krp, qn, kc, vc, ks, vs, kw, vw, gn, ga, gb) = _inproj(
        x, w, cosq, sinq, sample=True, tm=n, tab_blocks=1)
    pt = page_table.reshape(-1)
    krope = krp[:, MLA_D_NOPE:MLA_D_NOPE + MLA_D_ROPE]
    qrope = qrot.reshape(bd, td, MLA_HEADS, HEAD_PAD)[..., MLA_D_NOPE:MLA_D_NOPE + MLA_D_ROPE]
    qpad = LANE - MLA_D_ROPE
    qd_mla = jnp.concatenate([qabs.reshape(bd, td, MLA_HEADS, kv_rank), qrope,
                              jnp.zeros((bd, td, MLA_HEADS, qpad), F32)], axis=-1)
    qd_mla = qd_mla.reshape(bd, td * MLA_HEADS, kv_rank + LANE).astype(BF16)
    knew = jnp.concatenate([ckv, krope, jnp.zeros((n, qpad), F32)], axis=-1).reshape(bd, td, -1)
    knew = _pad_rows(knew, 8).astype(BF16)
    pos_minor = lambda c: jnp.moveaxis(c, 1, -1).reshape(c.shape[0], -1, c.shape[1])
    o_lat = _mla_decode(pt, qd_mla, knew, c_ckv, pos_minor(c_krope), td=td, pp=min(64, n_pages))
    qg = qn.astype(F32).reshape(bd, td, NSA_GROUPS, NSA_HPG, NSA_DH).transpose(0, 2, 3, 1, 4)
    qg = qg.reshape(bd, NSA_GROUPS, NSA_HPG * td, NSA_DH)
    lane_g = (jnp.arange(LANE) // NSA_DH)[None, :] == jnp.arange(NSA_GROUPS)[:, None]
    qd = jnp.where(lane_g[None, :, None, :], jnp.tile(qg, (1, 1, 1, NSA_GROUPS)), 0.0).astype(BF16)
    n_sel = -(-(past + td) // SLC_BLOCK)
    n_sel_pad = -(-n_sel // LANE) * LANE
    o_cmp, score = _cmp_decode(pt, qd, pos_minor(c_cmp_k), pos_minor(c_cmp_v),
                               w, td=td, pp=min(32, n_pages), past=past, n_sel_pad=n_sel_pad)
    cur = jnp.broadcast_to((pos // SLC_BLOCK).astype(jnp.int32), (bd, NSA_GROUPS, td)).reshape(1, -1)
    n_sel8 = -(-n_sel // 8) * 8
    sel_t = _rank(score[:, :, :td, :n_sel8].reshape(-1, n_sel8).T, cur, n_sel=n_sel)
    sel = jnp.pad(sel_t.T, ((0, 0), (0, n_sel_pad - n_sel8))).reshape(bd, NSA_GROUPS, td, n_sel_pad)
    pp_slc = min(64, n_pages)
    bps = pp_slc * PAGE_SIZE // SLC_BLOCK
    n_past_blk = past // SLC_BLOCK
    sel_past = sel[..., :n_past_blk].reshape(bd, NSA_GROUPS, td, n_past_blk // bps, bps).transpose(0, 3, 1, 2, 4)
    sel16 = jnp.tile(sel_past, (1, 1, 1, NSA_HPG, 1)).astype(BF16)
    selnew = jnp.tile(jnp.broadcast_to(sel[..., n_past_blk:n_past_blk + 1], (bd, NSA_GROUPS, td, 8)), (1, 1, NSA_HPG, 1))
    new8 = lambda a: _pad_rows(a.reshape(bd, td, LANE), 8).astype(BF16)
    o_slc = _slc_decode(pt, qd, sel16, selnew, new8(ks), new8(vs), pos_minor(c_slc_k), pos_minor(c_slc_v),
                        td=td, pp=pp_slc, past=past)
    nbuf = s_win_k.shape[1]
    o_win = _win_decode(qd, pos_minor(s_win_k), pos_minor(s_win_v), new8(kw), new8(vw),
                        td=td, past=past)

    def heads_out(o):
        o = o.reshape(bd, NSA_GROUPS, NSA_HPG, td, NSA_GROUPS, NSA_DH)
        o = jnp.stack([o[:, g, :, :, g] for g in range(NSA_GROUPS)], axis=1)
        return o.transpose(0, 3, 1, 2, 4).reshape(n, NSA_HEADS * NSA_DH)

    dff = w['w_gate'].shape[1]
    prev1 = jnp.zeros((bd, td, dff), F32).at[:, 0].set(s_conv[:, 1])
    prev2 = jnp.zeros((bd, td, dff), F32).at[:, 0].set(s_conv[:, 0]).at[:, 1].set(s_conv[:, 1])
    y, g = _finish(x, o_lat.reshape(n, MLA_HEADS * kv_rank), heads_out(o_cmp), heads_out(o_slc), heads_out(o_win),
                   gn, ga, gb, (prev1.reshape(n, dff), prev2.reshape(n, dff)), w, tm=n, period=td, full_g=True,
                   latent=True)
    kv4 = lambda a: a.reshape(1, bd, td, NSA_GROUPS, NSA_DH)
    win = lambda st, new: jnp.concatenate([st, new.reshape(bd, td, NSA_GROUPS, NSA_DH)], axis=1)[None, :, -nbuf:]
    conv_state = jnp.concatenate([s_conv, g.reshape(bd, td, dff)], axis=1)[None, :, -(CONV_W - 1):]
    states = (ckv.reshape(1, bd, td, kv_rank), krope.reshape(1, bd, td, MLA_D_ROPE), kv4(kc), kv4(vc), kv4(ks), kv4(vs),
              win(s_win_k, kw), win(s_win_v, vw), conv_state)
    return y.reshape(bd, td, d), states


def kernel(x_prompt, x_sample, cache_mla_ckv, cache_mla_krope, cache_nsa_cmp_k, cache_nsa_cmp_v, cache_nsa_slc_k, cache_nsa_slc_v, state_win_k, state_win_v, state_ffn_conv, page_table, norm1_g, w_in, q_norm_g, kv_norm_g, w_uq, w_uk, w_uv, cmp_pos_k, cmp_w1_k, cmp_w2_k, cmp_pos_v, cmp_w1_v, cmp_w2_v, w_proj_mla, w_proj_nsa, w_out, norm2_g, w_gate, w_up, conv_w, conv_b, w_down, norm_f_g):
    assert norm1_g.shape[0] == 1, "single-layer trunk"
    p = dict(norm1_g=norm1_g[0], w_in=w_in[0], q_norm_g=q_norm_g[0], kv_norm_g=kv_norm_g[0], w_uq=w_uq[0],
             w_uk=w_uk[0], w_uv=w_uv[0], cmp_pos_k=cmp_pos_k[0], cmp_w1_k=cmp_w1_k[0], cmp_w2_k=cmp_w2_k[0],
             cmp_pos_v=cmp_pos_v[0], cmp_w1_v=cmp_w1_v[0], cmp_w2_v=cmp_w2_v[0], w_proj_mla=w_proj_mla[0],
             w_proj_nsa=w_proj_nsa[0], w_out=w_out[0], norm2_g=norm2_g[0], w_gate=w_gate[0], w_up=w_up[0],
             conv_w=conv_w[0], conv_b=conv_b[0], w_down=w_down[0], norm_f_g=norm_f_g)
    w = _prep_weights(p)
    y_p, ps = _prompt(x_prompt, w)
    caches = (cache_mla_ckv[0], cache_mla_krope[0], cache_nsa_cmp_k[0], cache_nsa_cmp_v[0], cache_nsa_slc_k[0],
              cache_nsa_slc_v[0], state_win_k[0], state_win_v[0], state_ffn_conv[0])
    y_s, ss = _sample(x_sample, caches, page_table, w)
    out = [y_p, y_s]
    for a, b in zip(ps, ss):
        out += [a, b]
    return tuple(out)
```

```python
import functools

import numpy as np
import jax
import jax.numpy as jnp
from jax import lax
from jax.experimental import pallas as pl
from jax.experimental.pallas import tpu as pltpu

MLA_HEADS = 8
MLA_D_NOPE = 64
MLA_D_ROPE = 32
MLA_D_V = 64
ROPE_THETA = 10000.0
MLA_SCALE = (MLA_D_NOPE + MLA_D_ROPE) ** -0.5
NSA_HEADS = 8
NSA_GROUPS = 2
NSA_HPG = NSA_HEADS // NSA_GROUPS
NSA_DH = 64
NSA_SCALE = NSA_DH ** -0.5
CMP_BLOCK = 32
CMP_STRIDE = 16
SLC_BLOCK = 64
SLC_TOP_N = 16
WINDOW = 512
CONV_W = 3
PAGE_SIZE = 128
EPS = 1e-6
NEG = -1e30
FORCE = 1e9

LOG2E = 1.4426950408889634
LANE = 128
AUX_POS_HI, AUX_POS_LO, AUX_BLK0 = 64, 65, 72
HEAD_PAD = 128
CHUNK_FEATS = CMP_STRIDE * NSA_GROUPS * NSA_DH
FFN_BLOCK = 1536
SLOPES =tuple(float(2.0 ** (-8.0 * (h + 1) / NSA_HEADS)) for h in range(NSA_HEADS))
VMEM_LIMIT = 56 * 1024 * 1024

F32 = jnp.float32
BF16 = jnp.bfloat16
_NT = (((1,), (1,)), ((), ()))


def _cparams(*sem):
    return pltpu.CompilerParams(dimension_semantics=sem, vmem_limit_bytes=VMEM_LIMIT)


def _rms(x, g):
    return x * lax.rsqrt(jnp.mean(x * x, axis=-1, keepdims=True) + EPS) * g


def _dot(a, b):
    return jnp.dot(a, b, preferred_element_type=F32)


def _dot_nt(a, b):
    return lax.dot_general(a, b, _NT, preferred_element_type=F32)


def _dot_exact(a, b):
    return jnp.dot(a, b, preferred_element_type=F32, precision=lax.Precision.HIGHEST)


def _iota(shape, dim):
    return lax.broadcasted_iota(jnp.int32, shape, dim)


def _log2(n):
    assert n > 0 and n & (n - 1) == 0, n
    return n.bit_length() - 1


def _vdiv(x, n):
    return lax.shift_right_logical(x, jnp.full(x.shape, _log2(n), jnp.int32))


def _vmod(x, n):
    assert n & (n - 1) == 0, n
    return x & (n - 1)


_O_CQ, _O_CKV, _O_QN, _O_K6, _O_GA = 0, 384, 640, 1152, 1920


def _inproj_kernel(x_ref, g1_ref, w_ref, qg_ref, kvg_ref, wuq_ref, wk_ref, wv_ref, cos_ref, sin_ref,
                   *outs, sample, q_rank, kv_rank, d_model, tiles_per_seq, q_scale):
    o_gb = _O_GA + d_model
    o_kr = o_gb + d_model
    x = x_ref[...]
    hn = _rms(x, g1_ref[...])
    y = _dot(hn.astype(BF16), w_ref[...])
    cosq = cos_ref[...]
    sinq = sin_ref[...]
    nq = MLA_HEADS * HEAD_PAD
    cqn = _rms(y[:, _O_CQ:_O_CQ + q_rank], qg_ref[...])
    q2 = _dot(cqn.astype(BF16), wuq_ref[...])
    ckv = _rms(y[:, _O_CKV:_O_CKV + kv_rank], kvg_ref[...])
    kr = y[:, o_kr:o_kr + LANE] * cosq + y[:, o_kr + LANE:o_kr + 2 * LANE] * sinq
    ckv_b = ckv.astype(BF16)
    it = iter(outs)
    if sample:
        qrot_ref, qabs_ref = next(it), next(it)
        for h in range(MLA_HEADS):
            sl = slice(h * HEAD_PAD, (h + 1) * HEAD_PAD)
            qh = (q2[:, sl] * cosq + q2[:, nq + h * HEAD_PAD:nq + (h + 1) * HEAD_PAD] * sinq) * q_scale
            qrot_ref[:, sl] = qh
            qabs_ref[:, h * kv_rank:(h + 1) * kv_rank] = _dot(qh.astype(BF16), wk_ref[h])
    else:
        q_ref, k_ref, v_ref = next(it), next(it), next(it)
        knp = _dot(ckv_b, wk_ref[...])
        for h in range(MLA_HEADS):
            sl = slice(h * HEAD_PAD, (h + 1) * HEAD_PAD)
            qh = (q2[:, sl] * cosq + q2[:, nq + h * HEAD_PAD:nq + (h + 1) * HEAD_PAD] * sinq) * q_scale
            q_ref[:, sl] = qh.astype(BF16)
            k_ref[:, sl] = (knp[:, sl] + kr).astype(BF16)
        ones_lane = (_vmod(_iota((1, nq), 1), HEAD_PAD) == MLA_D_V).astype(F32)
        v_ref[...] = (_dot(ckv_b, wv_ref[...]) + ones_lane).astype(BF16)
    ckv_ref, kr_ref, qn_ref = next(it), next(it), next(it)
    ckv_ref[...] = ckv
    kr_ref[...] = kr
    yq = y[:, _O_QN:_O_QN + NSA_HEADS * NSA_DH] * NSA_SCALE
    if sample:
        qn_ref[...] = yq.astype(BF16)
    else:
        lane_q = _iota((x.shape[0], LANE), 1)
        for h in range(NSA_HEADS):
            pair = yq[:, (h // 2) * LANE:(h // 2 + 1) * LANE]
            if h % 2:
                pair = pltpu.roll(pair, NSA_DH, 1)
            aux_q = jnp.where((lane_q == AUX_POS_HI) | (lane_q == AUX_POS_LO), SLOPES[h], 0.0)
            qn_ref[:, h * LANE:(h + 1) * LANE] = jnp.where(lane_q < NSA_DH, pair, aux_q).astype(BF16)
    for j in range(6):
        yj = y[:, _O_K6 + j * LANE:_O_K6 + (j + 1) * LANE]
        if sample:
            next(it)[...] = yj
        else:
            next(it)[...] = yj.T
    if not sample:
        tm = x.shape[0]
        pos = (pl.program_id(0) % tiles_per_seq) * tm + _iota((tm, 1), 0)
        lane = _iota((tm, LANE), 1)
        onehot = ((lane >= AUX_BLK0) & (_vdiv(pos, SLC_BLOCK) == lane - AUX_BLK0)).astype(F32)
        aux = jnp.where(lane == AUX_POS_HI, (pos - _vmod(pos, 256)).astype(F32),
                        jnp.where(lane == AUX_POS_LO, _vmod(pos, 256).astype(F32), onehot))
        for j in range(6):
            yj = y[:, _O_K6 + j * LANE:_O_K6 + (j + 1) * LANE]
            ref = next(it)
            if j in (2, 3, 4, 5):
                tail = aux if j in (2, 4) else (lane == NSA_DH).astype(F32)
                ref[:, :LANE] = jnp.where(lane < NSA_DH, yj, tail).astype(BF16)
                ref[:, LANE:] = jnp.where(lane < NSA_DH, pltpu.roll(yj, NSA_DH, 1), tail).astype(BF16)
            else:
                ref[...] = yj.astype(BF16)
    gn_ref, ga_ref, gb_ref = next(it), next(it), next(it)
    gn_ref[...] = jax.nn.sigmoid(y[:, o_kr + 2 * LANE:o_kr + 3 * LANE])
    ga_ref[...] = jax.nn.sigmoid(y[:, _O_GA:_O_GA + d_model])
    gb_ref[...] = jax.nn.sigmoid(y[:, o_gb:o_gb + d_model])


def _inproj(x, wts, cosq, sinq, *, sample, tm, tab_blocks):
    n, d = x.shape
    q_rank, kv_rank = wts['q_norm_g'].shape[1], wts['kv_norm_g'].shape[1]
    nq = MLA_HEADS * HEAD_PAD
    wk = wts['w_ukT'] if sample else wts['w_ukp']
    row = lambda i: (i, 0)
    const2 = lambda i: (0, 0)
    tab_map = (lambda i: (i % tab_blocks, 0))
    in_specs = [
        pl.BlockSpec((tm, d), row),
        pl.BlockSpec((1, d), const2),
        pl.BlockSpec(wts['w_in'].shape, const2),
        pl.BlockSpec((1, q_rank), const2),
        pl.BlockSpec((1, kv_rank), const2),
        pl.BlockSpec(wts['w_uq2'].shape, const2),
        pl.BlockSpec(wk.shape, (lambda i: (0, 0, 0)) if sample else const2),
        pl.BlockSpec(wts['w_uvf'].shape, const2),
        pl.BlockSpec((tm, LANE), tab_map),
        pl.BlockSpec((tm, LANE), tab_map),
    ]
    shapes = []
    if sample:
        shapes += [(nq, F32), (MLA_HEADS * kv_rank, F32)]
    else:
        shapes += [(nq, BF16), (nq, BF16), (nq, BF16)]
    shapes += [(kv_rank, F32), (LANE, F32), (NSA_HEADS * (NSA_DH if sample else LANE), BF16)]
    n_lead = len(shapes)
    shapes += [(LANE, F32)] * 6
    if not sample:
        shapes += [(LANE, BF16), (LANE, BF16)] + [(2 * LANE, BF16)] * 4
        assert AUX_BLK0 + -(-tab_blocks * tm // SLC_BLOCK) <= LANE, "block one-hot must fit the aux lanes"
    shapes += [(LANE, F32), (d, F32), (d, F32)]
    out_shape = [jax.ShapeDtypeStruct((n, w), dt) for w, dt in shapes]
    out_specs = [pl.BlockSpec((tm, w), row) for w, _ in shapes]
    if not sample:
        tpb = tab_blocks
        for j in range(n_lead, n_lead + 6):
            out_shape[j] = jax.ShapeDtypeStruct((n // (tpb * tm), LANE, tpb * tm), F32)
            out_specs[j] = pl.BlockSpec((None, LANE, tm), lambda i: (i // tpb, 0, i % tpb))
    q_scale = MLA_SCALE if sample else MLA_SCALE * LOG2E
    kern = functools.partial(_inproj_kernel, sample=sample, q_rank=q_rank, kv_rank=kv_rank, d_model=d,
                             tiles_per_seq=tab_blocks, q_scale=q_scale)
    return pl.pallas_call(
        kern, grid=(n // tm,), in_specs=in_specs, out_specs=out_specs, out_shape=out_shape,
        compiler_params=_cparams("parallel"), name="inproj_sample" if sample else "inproj_prompt",
    )(x, wts['norm1_g'], wts['w_in'], wts['q_norm_g'], wts['kv_norm_g'], wts['w_uq2'], wk, wts['w_uvf'],
      cosq, sinq)


def _stack_heads(q_ref, heads, width):
    parts = [q_ref[:, h * width:(h + 1) * width] for h in heads]
    return parts[0] if len(parts) == 1 else jnp.concatenate(parts, axis=0)


def _flash_kernel(*refs, tq, tk, groups, par, dq, dk, dv, window, has_bias, base2):
    if has_bias:
        q_ref, k_ref, v_ref, bias_ref, o_ref = refs
    else:
        q_ref, k_ref, v_ref, o_ref = refs
    q_start = pl.program_id(1) * tq
    n_hi = (q_start + tq - 1) // tk + 1
    hi_full = (q_start + 1) // tk
    if window:
        n_lo = jnp.maximum(q_start - (window - 1), 0) // tk
        lo_full = (jnp.maximum(q_start + tq - window, 0) + tk - 1) // tk
    else:
        n_lo, lo_full = 0, 0
    e1 = jnp.clip(lo_full, n_lo, n_hi)
    e2 = jnp.clip(hi_full, e1, n_hi)
    ex = jnp.exp2 if base2 else jnp.exp
    for c0 in range(0, len(groups), par):
        chunk = groups[c0:c0 + par]
        qs = []
        for heads, kcol in chunk:
            qg = _stack_heads(q_ref, heads, dq)
            if has_bias:
                bias = bias_ref[:, kcol * LANE:(kcol + 1) * LANE]
                qg = qg + jnp.concatenate([bias] * len(heads), axis=0)
            qs.append(qg)
        rows = qs[0].shape[0]
        qpos = q_start + (_iota((rows, 1), 0) & (tq - 1))

        def step(j, carry, masked):
            k0 = pl.multiple_of(j * tk, tk)
            if masked:
                dist = qpos - (k0 + _iota((1, tk), 1))
                mask = dist >= 0
                if window:
                    mask = mask & (dist < window)
            out = []
            for (heads, kcol), qg, (m, acc) in zip(chunk, qs, carry):
                kt = k_ref[pl.ds(k0, tk), kcol * dk:(kcol + 1) * dk]
                vt = v_ref[pl.ds(k0, tk), kcol * LANE:(kcol + 1) * LANE]
                s = _dot_nt(qg, kt)
                if masked:
                    s = jnp.where(mask, s, NEG)
                m_new = jnp.maximum(m, jnp.max(s, axis=-1, keepdims=True))
                p = ex((s - m_new).astype(BF16))
                acc = ex(m - m_new) * acc + _dot(p, vt)
                out.append((m_new, acc))
            return tuple(out)

        carry = tuple((jnp.full((rows, 1), NEG, F32), jnp.zeros((rows, LANE), F32)) for _ in chunk)
        carry = lax.fori_loop(n_lo, e1, functools.partial(step, masked=True), carry)
        carry = lax.fori_loop(e1, e2, functools.partial(step, masked=False), carry)
        carry = lax.fori_loop(e2, n_hi, functools.partial(step, masked=True), carry)
        for (heads, kcol), (m, acc) in zip(chunk, carry):
            o = acc[:, :dv] * (1.0 / acc[:, dv:dv + 1])
            for hh, h in enumerate(heads):
                o_ref[:, h * dv:(h + 1) * dv] = o[hh * tq:(hh + 1) * tq].astype(o_ref.dtype)


def _flash(q, k, v, bias, *, batch, seq, tq, tk, groups, par, dq, dk, dv, window, base2, name):
    n = q.shape[0]
    has_bias = bias is not None
    n_heads = sum(len(g[0]) for g in groups)
    qrow = lambda b, i: (b * (seq // tq) + i, 0)
    kv = lambda b, i: (b, 0)
    in_specs = [pl.BlockSpec((tq, q.shape[1]), qrow), pl.BlockSpec((seq, k.shape[1]), kv),
                pl.BlockSpec((seq, v.shape[1]), kv)]
    args = [q, k, v]
    if has_bias:
        in_specs.append(pl.BlockSpec((tq, bias.shape[1]), qrow))
        args.append(bias)
    kern = functools.partial(_flash_kernel, tq=tq, tk=tk, groups=groups, par=par, dq=dq, dk=dk, dv=dv, window=window,
                             has_bias=has_bias, base2=base2)
    return pl.pallas_call(
        kern, grid=(batch, seq // tq), in_specs=in_specs,
        out_specs=pl.BlockSpec((tq, n_heads * dv), qrow),
        out_shape=jax.ShapeDtypeStruct((n, n_heads * dv), BF16),
        compiler_params=_cparams("parallel", "arbitrary"), name=name,
    )(*args)


def _win_kernel(q_ref, k_ref, v_ref, o_ref, *, tq, groups, dv):
    i = pl.program_id(1)
    rq = _iota((tq, tq), 0)
    ck = _iota((tq, tq), 1)
    bias_own = jnp.where(ck <= rq, 0.0, NEG)
    bias_far = jnp.where(ck > rq, 0.0, NEG) + jnp.where(i >= 2, 0.0, NEG)
    bias_mid = jnp.where(i >= 1, 0.0, NEG)
    starts = (jnp.maximum(i - 2, 0) * tq, jnp.maximum(i - 1, 0) * tq, i * tq)
    for heads, kcol in groups:
        qg = _stack_heads(q_ref, heads, LANE)
        nh = len(heads)
        ss, vs = [], []
        for k0, bias in zip(starts, (bias_far, None, bias_own)):
            k0 = pl.multiple_of(k0, tq)
            s = _dot_nt(qg, k_ref[pl.ds(k0, tq), kcol * LANE:(kcol + 1) * LANE])
            s = s + (bias_mid if bias is None else jnp.concatenate([bias] * nh, axis=0))
            ss.append(s)
            vs.append(v_ref[pl.ds(k0, tq), kcol * LANE:(kcol + 1) * LANE])
        m = functools.reduce(jnp.maximum, [jnp.max(s, axis=-1, keepdims=True) for s in ss])
        acc = functools.reduce(jnp.add, [_dot(jnp.exp((s - m).astype(BF16)), v) for s, v in zip(ss, vs)])
        o = acc[:, :dv] * (1.0 / acc[:, dv:dv + 1])
        for hh, h in enumerate(heads):
            o_ref[:, h * dv:(h + 1) * dv] = o[hh * tq:(hh + 1) * tq].astype(o_ref.dtype)


def _win_prompt(q, k, v, *, batch, seq, tq, groups, dv):
    assert WINDOW == 2 * tq and seq % tq == 0
    n = q.shape[0]
    n_heads = sum(len(g[0]) for g in groups)
    qrow = lambda b, i: (b * (seq // tq) + i, 0)
    kv = lambda b, i: (b, 0)
    return pl.pallas_call(
        functools.partial(_win_kernel, tq=tq, groups=groups, dv=dv), grid=(batch, seq // tq),
        in_specs=[pl.BlockSpec((tq, q.shape[1]), qrow), pl.BlockSpec((seq, k.shape[1]), kv),
                  pl.BlockSpec((seq, v.shape[1]), kv)],
        out_specs=pl.BlockSpec((tq, n_heads * dv), qrow),
        out_shape=jax.ShapeDtypeStruct((n, n_heads * dv), BF16),
        compiler_params=_cparams("parallel", "arbitrary"), name="win_prompt",
    )(q, k, v)


def _compress_rows(xk, xv, w1k_ref, w1v_ref, posk_ref, posv_ref):
    yk = _dot(xk.astype(BF16), w1k_ref[...])
    yv = _dot(xv.astype(BF16), w1v_ref[...])
    return yk, yv


def _compress_finish(y, posy, w2_ref):
    rows = y.shape[0]
    a = y[:, :LANE]
    b = pltpu.roll(y[:, LANE:], rows - 1, 0)
    pos = posy[0:1, :LANE] + posy[1:2, LANE:]
    hid = jax.nn.gelu(a + b + pos)
    return _dot(hid.astype(BF16), w2_ref[...])


def _compress_prompt_kernel(xk_ref, xv_ref, w1k_ref, w1v_ref, pk_ref, pv_ref, w2k_ref, w2v_ref, ok_ref, ov_ref):
    yk, yv = _compress_rows(xk_ref[...], xv_ref[...], w1k_ref, w1v_ref, pk_ref, pv_ref)
    ok_ref[...] = _compress_finish(yk, _dot(pk_ref[...], w1k_ref[...]), w2k_ref).astype(BF16)
    ov_ref[...] = _compress_finish(yv, _dot(pv_ref[...], w1v_ref[...]), w2v_ref).astype(BF16)


def _compress_prompt(kc, vc, wts, *, batch, seq):
    nch = seq // CMP_STRIDE
    xk = kc.reshape(batch * nch, CHUNK_FEATS)
    xv = vc.reshape(batch * nch, CHUNK_FEATS)
    row = lambda b: (b, 0)
    c2 = lambda b: (0, 0)
    wspec = pl.BlockSpec((CHUNK_FEATS, 2 * LANE), c2)
    pspec = pl.BlockSpec((8, CHUNK_FEATS), c2)
    w2spec = pl.BlockSpec((LANE, LANE), c2)
    return pl.pallas_call(
        _compress_prompt_kernel, grid=(batch,),
        in_specs=[pl.BlockSpec((nch, CHUNK_FEATS), row), pl.BlockSpec((nch, CHUNK_FEATS), row),
                  wspec, wspec, pspec, pspec, w2spec, w2spec],
        out_specs=[pl.BlockSpec((nch, LANE), row)] * 2,
        out_shape=[jax.ShapeDtypeStruct((batch * nch, LANE), BF16)] * 2,
        compiler_params=_cparams("parallel"), name="compress_prompt",
    )(xk, xv, wts['cmp_w1k'], wts['cmp_w1v'], wts['cmp_posk'], wts['cmp_posv'], wts['cmp_w2k'], wts['cmp_w2v'])


def _overlap(n_rows, n_sel):
    c = _iota((n_rows, n_sel), 0) * CMP_STRIDE
    j = _iota((n_rows, n_sel), 1) * SLC_BLOCK
    return ((c < j + SLC_BLOCK) & (c + CMP_BLOCK > j)).astype(F32)


def _force_scores(score, cur, jj):
    forced = (jj == 0) | (jj == cur) | (jj == cur - 1)
    score = jnp.where(forced, FORCE, score)
    return jnp.where(jj <= cur, score, NEG)


def _cmp_prompt_kernel(q_ref, k_ref, v_ref, o_ref, bias_ref, *, tq, n_cmp, n_sel):
    q_start = pl.program_id(1) * tq
    ncp = k_ref.shape[0]
    rows = NSA_HPG * tq
    qpos = q_start + (_iota((rows, 1), 0) & (tq - 1))
    cidx = _iota((1, ncp), 1)
    dist = qpos - (cidx * CMP_STRIDE + CMP_BLOCK - 1)
    mask = (dist >= 0) & (cidx < n_cmp)
    distf = dist.astype(F32)
    hrow = _vdiv(_iota((rows, 1), 0), tq)
    nsp = -(-n_sel // 8) * 8
    cur = _vdiv(q_start + _iota((1, tq), 1), SLC_BLOCK)
    jj = _iota((nsp, tq), 0)
    cb = _iota((nsp, ncp), 1) * CMP_STRIDE
    jb = _iota((nsp, ncp), 0) * SLC_BLOCK
    ov_t = ((cb < jb + SLC_BLOCK) & (cb + CMP_BLOCK > jb)).astype(F32)
    for g in range(NSA_GROUPS):
        heads = range(g * NSA_HPG, (g + 1) * NSA_HPG)
        qg = jnp.concatenate([q_ref[:, h * LANE:h * LANE + NSA_DH] for h in heads], axis=0)
        slope = jnp.zeros((rows, 1), F32)
        for hh, h in enumerate(heads):
            slope = jnp.where(hrow == hh, SLOPES[h], slope)
        s = _dot_nt(qg, k_ref[:, g * NSA_DH:(g + 1) * NSA_DH]) - slope * distf
        s = jnp.where(mask, s, NEG)
        m = jnp.max(s, axis=-1, keepdims=True)
        p = jnp.where(mask, jnp.exp(s - m), 0.0)
        l = jnp.sum(p, axis=-1, keepdims=True)
        p = p / jnp.where(l > 0.0, l, 1.0)
        o = _dot(p.astype(BF16), v_ref[:, g * NSA_DH:(g + 1) * NSA_DH])
        imp = p[0:tq]
        for hh in range(1, NSA_HPG):
            imp = imp + p[hh * tq:(hh + 1) * tq]
            o_ref[:, (g * NSA_HPG + hh) * NSA_DH:(g * NSA_HPG + hh + 1) * NSA_DH] = o[hh * tq:(hh + 1) * tq].astype(BF16)
        o_ref[:, g * NSA_HPG * NSA_DH:(g * NSA_HPG + 1) * NSA_DH] = o[0:tq].astype(BF16)
        score = lax.dot_general(ov_t, imp, _NT, preferred_element_type=F32, precision=lax.Precision.HIGHEST)
        score = _force_scores(score, cur, jj)
        rank = jnp.zeros((nsp, tq), F32)
        for i in range(n_sel):
            ri = score[i:i + 1, :]
            beats = (ri > score) | ((ri == score) & (i < jj))
            rank = rank + beats.astype(F32)
        sel = (rank < float(min(SLC_TOP_N, n_sel))) & (jj <= cur)
        bias_t = jnp.where(sel | (jj >= n_sel), 0.0, NEG)
        bias_t = jnp.concatenate([jnp.zeros((AUX_BLK0, tq), F32), bias_t,
                                  jnp.zeros((LANE - AUX_BLK0 - nsp, tq), F32)], axis=0)
        bias_ref[:, g * LANE:(g + 1) * LANE] = bias_t.T.astype(BF16)


def _cmp_prompt(qn, kcc, vcc, *, batch, seq, tq):
    n = qn.shape[0]
    nch = seq // CMP_STRIDE
    n_cmp = nch - CMP_BLOCK // CMP_STRIDE + 1
    n_sel = -(-seq // SLC_BLOCK)
    qrow = lambda b, i: (b * (seq // tq) + i, 0)
    kv = lambda b, i: (b, 0)
    kern = functools.partial(_cmp_prompt_kernel, tq=tq, n_cmp=n_cmp, n_sel=n_sel)
    return pl.pallas_call(
        kern, grid=(batch, seq // tq),
        in_specs=[pl.BlockSpec((tq, qn.shape[1]), qrow), pl.BlockSpec((nch, LANE), kv), pl.BlockSpec((nch, LANE), kv)],
        out_specs=[pl.BlockSpec((tq, NSA_HEADS * NSA_DH), qrow), pl.BlockSpec((tq, NSA_GROUPS * LANE), qrow)],
        out_shape=[jax.ShapeDtypeStruct((n, NSA_HEADS * NSA_DH), BF16),
                   jax.ShapeDtypeStruct((n, NSA_GROUPS * LANE), BF16)],
        compiler_params=_cparams("parallel", "arbitrary"), name="cmp_prompt",
    )(qn, kcc, vcc)


def _finish_kernel(*refs, period, latent, has_state):
    it = iter(refs)
    x_ref, omla_ref, ocmp_ref, oslc_ref, owin_ref, gn_ref, ga_ref, gb_ref = (next(it) for _ in range(8))
    prev1_ref, prev2_ref = (next(it), next(it)) if has_state else (None, None)
    gx_ref = next(it)
    wuv_ref = next(it) if latent else None
    (wpm_ref, wpn_ref, wo_ref, g2_ref, wg_ref, wu_ref, cw_ref, cb_ref, wd_ref, gf_ref,
     y_ref, gout_ref, carry_ref) = it
    tm = x_ref.shape[0]
    gn = gn_ref[...]
    gn_hi = gn.astype(BF16)
    gn_lo = (gn - gn_hi.astype(F32)).astype(BF16)
    gexp = _dot(gn_hi, gx_ref[...]) + _dot(gn_lo, gx_ref[...])
    w = NSA_HEADS * NSA_DH
    o_nsa = (gexp[:, 0:w] * ocmp_ref[...].astype(F32) + gexp[:, w:2 * w] * oslc_ref[...].astype(F32)
             + gexp[:, 2 * w:3 * w] * owin_ref[...].astype(F32))
    o_mla = omla_ref[...].astype(BF16)
    if latent:
        o_mla = _dot(o_mla, wuv_ref[...]).astype(BF16)
    merged = (ga_ref[...] * _dot(o_mla, wpm_ref[...])
              + gb_ref[...] * _dot(o_nsa.astype(BF16), wpn_ref[...]))
    x1 = x_ref[...] + _dot(merged.astype(BF16), wo_ref[...])
    h2 = _rms(x1, g2_ref[...]).astype(BF16)
    row = _iota((tm, 1), 0)
    i = pl.program_id(0)
    t = _vmod(i * tm + row, period)
    carried = period > tm
    if carried:
        @pl.when(i == 0)
        def _():
            carry_ref[...] = jnp.zeros_like(carry_ref)
    dff = wg_ref.shape[1]
    x2 = x1
    for c0 in range(0, dff, FFN_BLOCK):
        cs = slice(c0, min(c0 + FFN_BLOCK, dff))
        g = _dot(h2, wg_ref[:, cs])
        u = _dot(h2, wu_ref[:, cs])
        g1 = pltpu.roll(g, 1, 0)
        g2 = pltpu.roll(g, 2, 0)
        if carried:
            c = carry_ref[:, cs]
            g1 = jnp.where(row == 0, c[7:8], g1)
            g2 = jnp.where(row == 0, c[6:7], jnp.where(row == 1, c[7:8], g2))
            carry_ref[:, cs] = g[tm - 8:tm]
        g1 = jnp.where(t >= 1, g1, prev1_ref[:, cs] if has_state else 0.0)
        g2 = jnp.where(t >= 2, g2, prev2_ref[:, cs] if has_state else 0.0)
        cw = cw_ref[:, cs]
        conv = cb_ref[:, cs] + cw[0:1] * g2 + cw[1:2] * g1 + cw[2:3] * g
        act = (jax.nn.silu(conv) * u).astype(BF16)
        x2 = x2 + _dot(act, wd_ref[cs, :])
        gout_ref[:, cs] = g[tm - 8:tm] if gout_ref.shape[0] == 8 else g
    y_ref[...] = _rms(x2, gf_ref[...])


def _finish(x, omla, ocmp, oslc, owin, gn, ga, gb, state_rows, wts, *, tm, period, full_g, latent):
    n, d = x.shape
    dff = wts['w_gate'].shape[1]
    row = lambda i: (i, 0)
    c2 = lambda i: (0, 0)
    acts = [x, omla, ocmp, oslc, owin, gn, ga, gb] + (list(state_rows) if state_rows is not None else [])
    consts = [wts['gate_expand']] + ([wts['w_uvbd']] if latent else []) + [
        wts['w_proj_mla'], wts['w_proj_nsa'], wts['w_out'], wts['norm2_g'],
        wts['w_gate'], wts['w_up'], wts['conv_w'], wts['conv_b'], wts['w_down'], wts['norm_f_g']]
    ins = acts + consts
    in_specs = [pl.BlockSpec((tm, a.shape[1]), row) for a in acts] + [pl.BlockSpec(a.shape, c2) for a in consts]
    g_rows = n if full_g else (n // tm) * 8
    g_blk = tm if full_g else 8
    kern = functools.partial(_finish_kernel, period=period, latent=latent, has_state=state_rows is not None)
    return pl.pallas_call(
        kern, grid=(n // tm,), in_specs=in_specs,
        out_specs=[pl.BlockSpec((tm, d), row), pl.BlockSpec((g_blk, dff), row)],
        out_shape=[jax.ShapeDtypeStruct((n, d), F32), jax.ShapeDtypeStruct((g_rows, dff), F32)],
        scratch_shapes=[pltpu.VMEM((8, dff), F32)],
        compiler_params=_cparams("arbitrary"), name="finish_full" if full_g else "finish_tiled",
    )(*ins)


def _swap_halves(w):
    hlf = w.shape[-1] // 2
    return jnp.concatenate([-w[..., hlf:], w[..., :hlf]], axis=-1)


def _prep_weights(p):
    d = p['w_in'].shape[0]
    q_rank, kv_rank = p['q_norm_g'].shape[-1], p['kv_norm_g'].shape[-1]
    sizes = [q_rank, kv_rank, MLA_D_ROPE, NSA_HEADS * NSA_DH] + [2 * NSA_GROUPS * NSA_DH] * 3 + [3 * NSA_HEADS, d, d]
    cuts = np.cumsum(sizes)[:-1].tolist()
    cq, ckv, kr, qn, kvc, kvs, kvw, gn, ga, gb = jnp.split(p['w_in'], cuts, axis=-1)
    assert _O_CKV == q_rank and _O_QN == q_rank + kv_rank
    lo, hi = MLA_D_NOPE, HEAD_PAD - MLA_D_NOPE - MLA_D_ROPE
    place = lambda w: jnp.pad(w, ((0, 0), (lo, hi)))
    gnp = jnp.pad(gn, ((0, 0), (0, LANE - gn.shape[1])))
    w_in = jnp.concatenate([cq, ckv, qn, kvc, kvs, kvw, ga, gb, place(kr), place(_swap_halves(kr)), gnp], axis=1)
    w = {'w_in': w_in.astype(BF16)}
    for k in ('norm1_g', 'q_norm_g', 'kv_norm_g', 'norm2_g', 'conv_b'):
        w[k] = p[k].reshape(1, -1)
    w['norm_f_g'] = p['norm_f_g'].reshape(1, -1)
    w['conv_w'] = jnp.pad(p['conv_w'], ((0, 8 - CONV_W), (0, 0)))
    uq = p['w_uq']
    hpad = ((0, 0), (0, 0), (0, HEAD_PAD - uq.shape[-1]))
    uq_a = jnp.pad(uq, hpad)
    uq_b = jnp.pad(jnp.concatenate([jnp.zeros_like(uq[..., :MLA_D_NOPE]), _swap_halves(uq[..., MLA_D_NOPE:])], -1), hpad)
    w['w_uq2'] = jnp.concatenate([uq_a.reshape(q_rank, -1), uq_b.reshape(q_rank, -1)], axis=1).astype(BF16)
    uk = p['w_uk']
    w['w_ukp'] = jnp.pad(uk, ((0, 0), (0, 0), (0, HEAD_PAD - MLA_D_NOPE))).reshape(kv_rank, -1).astype(BF16)
    w['w_ukT'] = jnp.pad(jnp.transpose(uk, (1, 2, 0)), ((0, 0), (0, HEAD_PAD - MLA_D_NOPE), (0, 0))).astype(BF16)
    w['w_uvf'] = jnp.pad(p['w_uv'], ((0, 0), (0, 0), (0, HEAD_PAD - MLA_D_V))).reshape(kv_rank, -1).astype(BF16)
    eye_h = jnp.eye(MLA_HEADS, dtype=F32)
    w['w_uvbd'] = jnp.einsum('rhv,hk->hrkv', p['w_uv'], eye_h).reshape(MLA_HEADS * kv_rank, -1).astype(BF16)
    eye_g = jnp.eye(NSA_GROUPS, dtype=F32)
    for nm in ('k', 'v'):
        w1 = p['cmp_w1_' + nm].reshape(2, CMP_STRIDE, NSA_DH, -1)
        big = jnp.einsum('ajdh,gk->jgdakh', w1, eye_g)
        w['cmp_w1' + nm] = big.reshape(CHUNK_FEATS, -1).astype(BF16)
        pos = p['cmp_pos_' + nm].reshape(2, CMP_STRIDE, 1, NSA_DH)
        pos = jnp.broadcast_to(pos, (2, CMP_STRIDE, NSA_GROUPS, NSA_DH)).reshape(2, CHUNK_FEATS)
        w['cmp_pos' + nm] = jnp.pad(pos, ((0, 6), (0, 0))).astype(BF16)
        w2 = p['cmp_w2_' + nm]
        w['cmp_w2' + nm] = jnp.einsum('hd,gk->ghkd', w2, eye_g).reshape(NSA_GROUPS * w2.shape[0], -1).astype(BF16)
    ge = np.zeros((LANE, 3 * NSA_HEADS * NSA_DH), np.float32)
    for h in range(NSA_HEADS):
        for i in range(3):
            ge[h * 3 + i, i * NSA_HEADS * NSA_DH + h * NSA_DH:i * NSA_HEADS * NSA_DH + (h + 1) * NSA_DH] = 1.0
    w['gate_expand'] = jnp.asarray(ge)
    for k in ('w_proj_mla', 'w_proj_nsa', 'w_out', 'w_gate', 'w_up', 'w_down'):
        w[k] = p[k].astype(BF16)
    return w


def _rope_tables(pos):
    inv = ROPE_THETA ** (-jnp.arange(0, MLA_D_ROPE, 2, dtype=F32) / MLA_D_ROPE)
    ang = pos.astype(F32)[:, None] * inv[None, :]
    cos, sin = jnp.cos(ang), jnp.sin(ang)
    n = pos.shape[0]
    pad = jnp.zeros((n, HEAD_PAD - MLA_D_NOPE - MLA_D_ROPE), F32)
    cosq = jnp.concatenate([jnp.ones((n, MLA_D_NOPE), F32), cos, cos, pad], axis=1)
    sinq = jnp.concatenate([jnp.zeros((n, MLA_D_NOPE), F32), sin, sin, pad], axis=1)
    return cosq, sinq


_NSA_GROUPS_SPEC = tuple((tuple(range(g * NSA_HPG, (g + 1) * NSA_HPG)), g) for g in range(NSA_GROUPS))
_MLA_GROUPS_SPEC = tuple(((h,), h) for h in range(MLA_HEADS))


def _prompt(x_prompt, w):
    b, t, d = x_prompt.shape
    n = b * t
    x = x_prompt.reshape(n, d)
    tm = 256
    cosq, sinq = _rope_tables(jnp.arange(t, dtype=jnp.int32))
    (q_mla, k_mla, v_mla, ckv, krp, qn, kc, vc, ks, vs, kw, vw, kc_b, vc_b, ks_b, vs_b, kw_b, vw_b, gn, ga, gb) = _inproj(
        x, w, cosq, sinq, sample=False, tm=tm, tab_blocks=t // tm)
    o_mla = _flash(q_mla, k_mla, v_mla, None, batch=b, seq=t, tq=512, tk=512, groups=_MLA_GROUPS_SPEC, par=8,
                   dq=HEAD_PAD, dk=HEAD_PAD, dv=MLA_D_V, window=0, base2=True, name="mla_prompt")
    kcc, vcc = _compress_prompt(kc_b, vc_b, w, batch=b, seq=t)
    o_cmp, sel_bias = _cmp_prompt(qn, kcc, vcc, batch=b, seq=t, tq=128)
    o_slc = _flash(qn, ks_b, vs_b, sel_bias, batch=b, seq=t, tq=256, tk=512, groups=_NSA_GROUPS_SPEC, par=2,
                   dq=LANE, dk=LANE, dv=NSA_DH, window=0, base2=False, name="slc_prompt")
    o_win = _win_prompt(qn, kw_b, vw_b, batch=b, seq=t, tq=WINDOW // 2, groups=_NSA_GROUPS_SPEC, dv=NSA_DH)
    dff = w['w_gate'].shape[1]
    y, gtail = _finish(x, o_mla, o_cmp, o_slc, o_win, gn, ga, gb, None, w, tm=tm, period=t, full_g=False,
                       latent=False)
    kv4 = lambda a: a.reshape(1, b, NSA_GROUPS, NSA_DH, a.shape[-1]).transpose(0, 1, 4, 2, 3)
    n_keep = min(WINDOW, t)
    kw, vw = kw[:, :, t - n_keep:], vw[:, :, t - n_keep:]
    conv_state = gtail.reshape(b, t // tm, 8, dff)[:, -1, 8 - (CONV_W - 1):, :]
    states = (ckv.reshape(1, b, t, -1), krp[:, MLA_D_NOPE:MLA_D_NOPE + MLA_D_ROPE].reshape(1, b, t, MLA_D_ROPE),
              kv4(kc), kv4(vc), kv4(ks), kv4(vs), kv4(kw), kv4(vw), conv_state[None])
    return y.reshape(b, t, d), states


def _page_copies(pt_ref, pools, bufs, sems, step, slot, pp):
    copies = []
    for k in range(pp):
        page = pt_ref[step * pp + k]
        for pool, buf, sem in zip(pools, bufs, sems):
            copies.append(pltpu.make_async_copy(pool.at[page], buf.at[slot, k], sem.at[slot]))
    return copies


def _start_all(copies, n_pools):
    for i, c in enumerate(copies):
        c.start(priority=(i // n_pools) % 2)


def _stream_pages(pt_ref, pools, bufs, sems, pp):
    step = pl.program_id(0) * pl.num_programs(1) + pl.program_id(1)
    total = pl.num_programs(0) * pl.num_programs(1)
    slot = step % 2

    @pl.when(step == 0)
    def _():
        _start_all(_page_copies(pt_ref, pools, bufs, sems, step, slot, pp), len(pools))

    @pl.when(step + 1 < total)
    def _():
        _start_all(_page_copies(pt_ref, pools, bufs, sems, step + 1, 1 - slot, pp), len(pools))

    for c in _page_copies(pt_ref, pools, bufs, sems, step, slot, pp):
        c.wait()
    return slot


def _softmax_update(sc, v, m_scr, l_scr, acc_scr, v_transposed=False):
    m_old = m_scr[...]
    m_new = jnp.maximum(m_old, jnp.max(sc, axis=-1, keepdims=True))
    p = jnp.exp(sc - m_new)
    alpha = jnp.exp(m_old - m_new)
    l_scr[...] = alpha * l_scr[...] + jnp.sum(p, axis=-1, keepdims=True)
    pv = _dot_nt(p.astype(BF16), v) if v_transposed else _dot(p.astype(BF16), v)
    acc_scr[...] = alpha * acc_scr[...] + pv
    m_scr[...] = m_new


def _mla_decode_kernel(pt_ref, q_ref, knew_ref, ckv_hbm, kr_hbm, o_ref, cbuf, rbuf, csem, rsem,
                       kscr, krscr, m_scr, l_scr, acc_scr, *, pp, td, kv_rank):
    slot = _stream_pages(pt_ref, (ckv_hbm, kr_hbm), (cbuf, rbuf), (csem, rsem), pp)
    s = pl.program_id(1)
    q = q_ref[...]
    rows = q.shape[0]

    @pl.when(s == 0)
    def _():
        m_scr[...] = jnp.full_like(m_scr, NEG)
        l_scr[...] = jnp.zeros_like(l_scr)
        acc_scr[...] = jnp.zeros_like(acc_scr)
        kn = knew_ref[...]
        trow = _vdiv(_iota((rows, 1), 0), MLA_HEADS)
        col = _iota((1, kn.shape[0]), 1)
        sc = jnp.where((col <= trow) & (col < td), _dot_nt(q, kn), NEG)
        _softmax_update(sc, kn[:, :kv_rank], m_scr, l_scr, acc_scr)

    for k in range(pp):
        kscr[k * PAGE_SIZE:(k + 1) * PAGE_SIZE, :] = cbuf[slot, k].astype(BF16)
        krscr[:, k * PAGE_SIZE:(k + 1) * PAGE_SIZE] = rbuf[slot, k].astype(BF16)
    kt = kscr[...]
    sc = _dot_nt(q[:, :kv_rank], kt) + _dot(q[:, kv_rank:kv_rank + MLA_D_ROPE], krscr[...])
    _softmax_update(sc, kt, m_scr, l_scr, acc_scr)

    @pl.when(s == pl.num_programs(1) - 1)
    def _():
        o_ref[...] = acc_scr[...] / l_scr[...]


def _mla_decode(pt, qd, knew, ckv_pool, kr_pool, *, td, pp):
    bd, rows, qw = qd.shape
    n_pages = pt.shape[0] // bd
    kv_rank = ckv_pool.shape[-1]
    per_b = lambda b, s, pt: (b, 0, 0)
    any_spec = pl.BlockSpec(memory_space=pl.ANY)
    in_specs = [pl.BlockSpec((None, rows, qw), per_b), pl.BlockSpec((None,) + knew.shape[1:], per_b), any_spec, any_spec]
    kern = functools.partial(_mla_decode_kernel, pp=pp, td=td, kv_rank=kv_rank)
    return pl.pallas_call(
        kern,
        grid_spec=pltpu.PrefetchScalarGridSpec(
            num_scalar_prefetch=1, grid=(bd, n_pages // pp), in_specs=in_specs,
            out_specs=pl.BlockSpec((None, rows, kv_rank), per_b),
            scratch_shapes=[pltpu.VMEM((2, pp) + ckv_pool.shape[1:], F32), pltpu.VMEM((2, pp) + kr_pool.shape[1:], F32),
                            pltpu.SemaphoreType.DMA((2,)), pltpu.SemaphoreType.DMA((2,)),
                            pltpu.VMEM((pp * PAGE_SIZE, kv_rank), BF16), pltpu.VMEM((MLA_D_ROPE, pp * PAGE_SIZE), BF16),
                            pltpu.VMEM((rows, 1), F32), pltpu.VMEM((rows, 1), F32), pltpu.VMEM((rows, kv_rank), F32)]),
        out_shape=jax.ShapeDtypeStruct((bd, rows, kv_rank), F32),
        compiler_params=_cparams("arbitrary", "arbitrary"), name="mla_decode",
    )(pt, qd, knew, ckv_pool, kr_pool)


def _alibi_rows(rows, td):
    r = _iota((rows, 1), 0)
    return _vdiv(r, td), _vmod(r, td)


def _slope_rows(hrow, g):
    slope = jnp.zeros(hrow.shape, F32)
    for hh in range(NSA_HPG):
        slope = jnp.where(hrow == hh, SLOPES[g * NSA_HPG + hh], slope)
    return slope


def _cmp_decode_kernel(pt_ref, q_ref, k_hbm, v_hbm, w1k_ref, w1v_ref, pk_ref, pv_ref, w2k_ref, w2v_ref,
                       o_ref, score_ref, kbuf, vbuf, ksem, vsem, kp0, kp1, vp0, vp1, yk_scr, yv_scr,
                       *, pp, td, past, n_cmp, n_sel_pad):
    slot = _stream_pages(pt_ref, (k_hbm, v_hbm), (kbuf, vbuf), (ksem, vsem), pp)
    s = pl.program_id(1)
    cpp = PAGE_SIZE // CMP_STRIDE
    hp = pp // 2
    for half, (kp_scr, vp_scr) in enumerate(((kp0, vp0), (kp1, vp1))):
        for k in range(hp):
            kp_scr[k * PAGE_SIZE:(k + 1) * PAGE_SIZE, :] = kbuf[slot, half * hp + k].T
            vp_scr[k * PAGE_SIZE:(k + 1) * PAGE_SIZE, :] = vbuf[slot, half * hp + k].T
    chunk_rows = lambda scr: jnp.concatenate(
        [scr[pl.ds(j, hp * cpp, stride=CMP_STRIDE), :] for j in range(CMP_STRIDE)], axis=1).astype(BF16)
    for half, (kp_scr, vp_scr) in enumerate(((kp0, vp0), (kp1, vp1))):
        r0 = pl.multiple_of(s * (pp * cpp) + half * (hp * cpp), hp * cpp)
        yk_scr[pl.ds(r0, hp * cpp), :] = _dot(chunk_rows(kp_scr), w1k_ref[...])
        yv_scr[pl.ds(r0, hp * cpp), :] = _dot(chunk_rows(vp_scr), w1v_ref[...])

    @pl.when(s == pl.num_programs(1) - 1)
    def _():
        kcc = _compress_finish(yk_scr[...], _dot(pk_ref[...], w1k_ref[...]), w2k_ref).astype(BF16)
        vcc = _compress_finish(yv_scr[...], _dot(pv_ref[...], w1v_ref[...]), w2v_ref).astype(BF16)
        ncp = kcc.shape[0]
        rows = NSA_HPG * td
        hrow, trow = _alibi_rows(rows, td)
        cidx = _iota((1, ncp), 1)
        dist = (past + trow) - (cidx * CMP_STRIDE + CMP_BLOCK - 1)
        mask = (dist >= 0) & (cidx < n_cmp)
        distf = dist.astype(F32)
        tsum = (_vmod(_iota((8, rows), 1), td) == _iota((8, rows), 0)).astype(F32)
        ov = _overlap(ncp, n_sel_pad)
        t8 = _iota((8, 1), 0)
        cur = _vdiv(past + t8, SLC_BLOCK)
        jj = _iota((8, n_sel_pad), 1)
        for g in range(NSA_GROUPS):
            sc = _dot_nt(q_ref[g], kcc) - _slope_rows(hrow, g) * distf
            sc = jnp.where(mask, sc, NEG)
            m = jnp.max(sc, axis=-1, keepdims=True)
            p = jnp.where(mask, jnp.exp(sc - m), 0.0)
            l = jnp.sum(p, axis=-1, keepdims=True)
            p = p / jnp.where(l > 0.0, l, 1.0)
            o_ref[g] = _dot(p.astype(BF16), vcc)
            imp = _dot_exact(tsum, p)
            score_ref[g] = _force_scores(_dot_exact(imp, ov), cur, jj)


def _cmp_decode(pt, qd, k_pool, v_pool, wts, *, td, pp, past, n_sel_pad):
    bd = qd.shape[0]
    n_pages = pt.shape[0] // bd
    cpp = PAGE_SIZE // CMP_STRIDE
    nch = n_pages * cpp
    n_cmp = (past + td) // CMP_STRIDE - CMP_BLOCK // CMP_STRIDE + 1
    assert (past + td) // CMP_STRIDE == nch, "new rows must not complete a chunk"
    per_b = lambda b, s, pt: (b, 0, 0, 0)
    c2 = lambda b, s, pt: (0, 0)
    rows = qd.shape[2]
    in_specs = ([pl.BlockSpec((None,) + qd.shape[1:], per_b)]
                + [pl.BlockSpec(memory_space=pl.ANY) for _ in range(2)]
                + [pl.BlockSpec((CHUNK_FEATS, 2 * LANE), c2) for _ in range(2)]
                + [pl.BlockSpec((8, CHUNK_FEATS), c2) for _ in range(2)]
                + [pl.BlockSpec((LANE, LANE), c2) for _ in range(2)])
    kern = functools.partial(_cmp_decode_kernel, pp=pp, td=td, past=past, n_cmp=n_cmp, n_sel_pad=n_sel_pad)
    return pl.pallas_call(
        kern,
        grid_spec=pltpu.PrefetchScalarGridSpec(
            num_scalar_prefetch=1, grid=(bd, n_pages // pp), in_specs=in_specs,
            out_specs=[pl.BlockSpec((None, NSA_GROUPS, rows, LANE), per_b),
                       pl.BlockSpec((None, NSA_GROUPS, 8, n_sel_pad), per_b)],
            scratch_shapes=[pltpu.VMEM((2, pp) + k_pool.shape[1:], F32) for _ in range(2)]
            + [pltpu.SemaphoreType.DMA((2,)) for _ in range(2)]
            + [pltpu.VMEM((pp // 2 * PAGE_SIZE, LANE), F32) for _ in range(4)]
            + [pltpu.VMEM((nch, 2 * LANE), F32) for _ in range(2)]),
        out_shape=[jax.ShapeDtypeStruct((bd, NSA_GROUPS, rows, LANE), F32),
                   jax.ShapeDtypeStruct((bd, NSA_GROUPS, 8, n_sel_pad), F32)],
        compiler_params=_cparams("arbitrary", "arbitrary"), name="cmp_decode",
    )(pt, qd, k_pool, v_pool, wts['cmp_w1k'], wts['cmp_w1v'], wts['cmp_posk'], wts['cmp_posv'],
      wts['cmp_w2k'], wts['cmp_w2v'])


def _rank_kernel(score_ref, cur_ref, sel_ref, *, n_sel):
    sc = score_ref[...]
    jj = _iota(sc.shape, 0)

    def body(i, rank):
        ri = score_ref[pl.ds(i, 1), :]
        beats = (ri > sc) | ((ri == sc) & (i < jj))
        return rank + beats.astype(F32)

    rank = lax.fori_loop(0, n_sel, body, jnp.zeros(sc.shape, F32))
    sel = (rank < float(min(SLC_TOP_N, n_sel))) & (jj <= cur_ref[...])
    sel_ref[...] = sel.astype(F32)


def _rank(score_t, cur, *, n_sel):
    full = lambda a: pl.BlockSpec(a.shape, lambda: (0,) * a.ndim)
    return pl.pallas_call(
        functools.partial(_rank_kernel, n_sel=n_sel), in_specs=[full(score_t), full(cur)],
        out_specs=full(score_t), out_shape=jax.ShapeDtypeStruct(score_t.shape, F32), name="rank_decode",
    )(score_t, cur)


def _slc_decode_kernel(pt_ref, q_ref, sel_ref, selnew_ref, knew_ref, vnew_ref, expand_ref, k_hbm, v_hbm, o_ref,
                       kbuf, vbuf, ksem, vsem, kscr, vscr, m_scr, l_scr, acc_scr, *, pp, td, past):
    slot = _stream_pages(pt_ref, (k_hbm, v_hbm), (kbuf, vbuf), (ksem, vsem), pp)
    s = pl.program_id(1)
    rows = q_ref.shape[1]
    hrow, trow = _alibi_rows(rows, td)
    tk = pp * PAGE_SIZE

    @pl.when(s == 0)
    def _():
        m_scr[...] = jnp.full_like(m_scr, NEG)
        l_scr[...] = jnp.zeros_like(l_scr)
        acc_scr[...] = jnp.zeros_like(acc_scr)
        kn = knew_ref[...]
        vn = vnew_ref[...]
        col = _iota((1, kn.shape[0]), 1)
        dist = trow - col
        for g in range(NSA_GROUPS):
            sc = _dot_nt(q_ref[g], kn) - _slope_rows(hrow, g) * dist.astype(F32)
            mask = (dist >= 0) & (col < td) & (selnew_ref[g] > 0.5)
            _softmax_update(jnp.where(mask, sc, NEG), vn, m_scr.at[g], l_scr.at[g], acc_scr.at[g])

    for k in range(pp):
        kscr[:, k * PAGE_SIZE:(k + 1) * PAGE_SIZE] = kbuf[slot, k].astype(BF16)
        vscr[:, k * PAGE_SIZE:(k + 1) * PAGE_SIZE] = vbuf[slot, k].astype(BF16)
    kt = kscr[...]
    vt = vscr[...]
    kpos = s * tk + _iota((1, tk), 1)
    distf = ((past + trow) - kpos).astype(F32)
    for g in range(NSA_GROUPS):
        sc = _dot(q_ref[g], kt) - _slope_rows(hrow, g) * distf
        selx = _dot(sel_ref[g], expand_ref[...]) > 0.5
        _softmax_update(jnp.where(selx, sc, NEG), vt, m_scr.at[g], l_scr.at[g], acc_scr.at[g], v_transposed=True)

    @pl.when(s == pl.num_programs(1) - 1)
    def _():
        o_ref[...] = acc_scr[...] / l_scr[...]


def _slc_decode(pt, qd, sel16, selnew, knew, vnew, k_pool, v_pool, *, td, pp, past):
    bd = qd.shape[0]
    rows = qd.shape[2]
    n_pages = pt.shape[0] // bd
    tk = pp * PAGE_SIZE
    expand = (jnp.arange(tk)[None, :] // SLC_BLOCK == jnp.arange(tk // SLC_BLOCK)[:, None]).astype(BF16)
    per_b = lambda b, s, pt: (b, 0, 0, 0)
    per_b3 = lambda b, s, pt: (b, 0, 0)
    in_specs = ([pl.BlockSpec((None,) + qd.shape[1:], per_b),
                 pl.BlockSpec((None, None) + sel16.shape[2:], lambda b, s, pt: (b, s, 0, 0, 0)),
                 pl.BlockSpec((None,) + selnew.shape[1:], per_b),
                 pl.BlockSpec((None,) + knew.shape[1:], per_b3), pl.BlockSpec((None,) + vnew.shape[1:], per_b3),
                 pl.BlockSpec(expand.shape, lambda b, s, pt: (0, 0))]
                + [pl.BlockSpec(memory_space=pl.ANY) for _ in range(2)])
    kern = functools.partial(_slc_decode_kernel, pp=pp, td=td, past=past)
    return pl.pallas_call(
        kern,
        grid_spec=pltpu.PrefetchScalarGridSpec(
            num_scalar_prefetch=1, grid=(bd, n_pages // pp), in_specs=in_specs,
            out_specs=pl.BlockSpec((None, NSA_GROUPS, rows, LANE), per_b),
            scratch_shapes=[pltpu.VMEM((2, pp) + k_pool.shape[1:], F32) for _ in range(2)]
            + [pltpu.SemaphoreType.DMA((2,)) for _ in range(2)]
            + [pltpu.VMEM((LANE, pp * PAGE_SIZE), BF16) for _ in range(2)]
            + [pltpu.VMEM((NSA_GROUPS, rows, 1), F32) for _ in range(2)] + [pltpu.VMEM((NSA_GROUPS, rows, LANE), F32)]),
        out_shape=jax.ShapeDtypeStruct((bd, NSA_GROUPS, rows, LANE), F32),
        compiler_params=_cparams("arbitrary", "arbitrary"), name="slc_decode",
    )(pt, qd, sel16, selnew, knew, vnew, expand, k_pool, v_pool)


def _win_decode_kernel(q_ref, kst_ref, vst_ref, knew_ref, vnew_ref, o_ref, *, td, past):
    rows = q_ref.shape[1]
    hrow, trow = _alibi_rows(rows, td)
    nbuf = kst_ref.shape[1]
    kst = kst_ref[...].astype(BF16)
    vst = vst_ref[...].astype(BF16)
    kn = knew_ref[...]
    vn = vnew_ref[...]
    kpos = past - nbuf + _iota((1, nbuf), 1)
    d1 = (past + trow) - kpos
    m1 = (d1 >= 0) & (d1 < WINDOW) & (kpos >= 0)
    col = _iota((1, kn.shape[0]), 1)
    d2 = trow - col
    m2 = (d2 >= 0) & (d2 < WINDOW) & (col < td)
    for g in range(NSA_GROUPS):
        slope = _slope_rows(hrow, g)
        s1 = jnp.where(m1, _dot(q_ref[g], kst) - slope * d1.astype(F32), NEG)
        s2 = jnp.where(m2, _dot_nt(q_ref[g], kn) - slope * d2.astype(F32), NEG)
        m = jnp.maximum(jnp.max(s1, axis=-1, keepdims=True), jnp.max(s2, axis=-1, keepdims=True))
        p1 = jnp.exp(s1 - m)
        p2 = jnp.exp(s2 - m)
        l = jnp.sum(p1, axis=-1, keepdims=True) + jnp.sum(p2, axis=-1, keepdims=True)
        o_ref[g] = (_dot_nt(p1.astype(BF16), vst) + _dot(p2.astype(BF16), vn)) / l


def _win_decode(qd, kst, vst, knew, vnew, *, td, past):
    bd = qd.shape[0]
    rows = qd.shape[2]
    per_b = lambda b: (b, 0, 0, 0)
    per_b3 = lambda b: (b, 0, 0)
    blk3 = lambda a: pl.BlockSpec((None,) + a.shape[1:], per_b3)
    return pl.pallas_call(
        functools.partial(_win_decode_kernel, td=td, past=past), grid=(bd,),
        in_specs=[pl.BlockSpec((None,) + qd.shape[1:], per_b), blk3(kst), blk3(vst), blk3(knew), blk3(vnew)],
        out_specs=pl.BlockSpec((None, NSA_GROUPS, rows, LANE), per_b),
        out_shape=jax.ShapeDtypeStruct((bd, NSA_GROUPS, rows, LANE), F32),
        compiler_params=_cparams("parallel"), name="win_decode",
    )(qd, kst, vst, knew, vnew)


def _pad_rows(a, rows):
    return jnp.pad(a, ((0, 0), (0, rows - a.shape[1]), (0, 0)))


def _sample(x_sample, caches, page_table, w):
    (c_ckv, c_krope, c_cmp_k, c_cmp_v, c_slc_k, c_slc_v, s_win_k, s_win_v, s_conv) = caches
    bd, td, d = x_sample.shape
    n = bd * td
    n_pages = page_table.shape[1]
    past = n_pages * PAGE_SIZE
    n_pool = c_ckv.shape[0]
    kv_rank = c_ckv.shape[-1]
    x = x_sample.reshape(n, d)
    pos = past + jnp.arange(td, dtype=jnp.int32)
    cosq, sinq = _rope_tables(jnp.tile(pos, bd))
    (qrot, qabs, ckv, krp, qn, kc, vc, ks, vs, kw, vw, gn, ga, gb) = _inproj(
        x, w, cosq, sinq, sample=True, tm=n, tab_blocks=1)
    pt = page_table.reshape(-1)
    krope = krp[:, MLA_D_NOPE:MLA_D_NOPE + MLA_D_ROPE]
    qrope = qrot.reshape(bd, td, MLA_HEADS, HEAD_PAD)[..., MLA_D_NOPE:MLA_D_NOPE + MLA_D_ROPE]
    qpad = LANE - MLA_D_ROPE
    qd_mla = jnp.concatenate([qabs.reshape(bd, td, MLA_HEADS, kv_rank), qrope,
                              jnp.zeros((bd, td, MLA_HEADS, qpad), F32)], axis=-1)
    qd_mla = qd_mla.reshape(bd, td * MLA_HEADS, kv_rank + LANE).astype(BF16)
    knew = jnp.concatenate([ckv, krope, jnp.zeros((n, qpad), F32)], axis=-1).reshape(bd, td, -1)
    knew = _pad_rows(knew, 8).astype(BF16)
    pos_minor = lambda c: jnp.moveaxis(c, 1, -1).reshape(c.shape[0], -1, c.shape[1])
    o_lat = _mla_decode(pt, qd_mla, knew, c_ckv, pos_minor(c_krope), td=td, pp=min(64, n_pages))
    qg = qn.astype(F32).reshape(bd, td, NSA_GROUPS, NSA_HPG, NSA_DH).transpose(0, 2, 3, 1, 4)
    qg = qg.reshape(bd, NSA_GROUPS, NSA_HPG * td, NSA_DH)
    lane_g = (jnp.arange(LANE) // NSA_DH)[None, :] == jnp.arange(NSA_GROUPS)[:, None]
    qd = jnp.where(lane_g[None, :, None, :], jnp.tile(qg, (1, 1, 1, NSA_GROUPS)), 0.0).astype(BF16)
    n_sel = -(-(past + td) // SLC_BLOCK)
    n_sel_pad = -(-n_sel // LANE) * LANE
    o_cmp, score = _cmp_decode(pt, qd, pos_minor(c_cmp_k), pos_minor(c_cmp_v),
                               w, td=td, pp=min(32, n_pages), past=past, n_sel_pad=n_sel_pad)
    cur = jnp.broadcast_to((pos // SLC_BLOCK).astype(jnp.int32), (bd, NSA_GROUPS, td)).reshape(1, -1)
    n_sel8 = -(-n_sel // 8) * 8
    sel_t = _rank(score[:, :, :td, :n_sel8].reshape(-1, n_sel8).T, cur, n_sel=n_sel)
    sel = jnp.pad(sel_t.T, ((0, 0), (0, n_sel_pad - n_sel8))).reshape(bd, NSA_GROUPS, td, n_sel_pad)
    pp_slc = min(64, n_pages)
    bps = pp_slc * PAGE_SIZE // SLC_BLOCK
    n_past_blk = past // SLC_BLOCK
    sel_past = sel[..., :n_past_blk].reshape(bd, NSA_GROUPS, td, n_past_blk // bps, bps).transpose(0, 3, 1, 2, 4)
    sel16 = jnp.tile(sel_past, (1, 1, 1, NSA_HPG, 1)).astype(BF16)
    selnew = jnp.tile(jnp.broadcast_to(sel[..., n_past_blk:n_past_blk + 1], (bd, NSA_GROUPS, td, 8)), (1, 1, NSA_HPG, 1))
    new8 = lambda a: _pad_rows(a.reshape(bd, td, LANE), 8).astype(BF16)
    o_slc = _slc_decode(pt, qd, sel16, selnew, new8(ks), new8(vs), pos_minor(c_slc_k), pos_minor(c_slc_v),
                        td=td, pp=pp_slc, past=past)
    nbuf = s_win_k.shape[1]
    o_win = _win_decode(qd, pos_minor(s_win_k), pos_minor(s_win_v), new8(kw), new8(vw),
                        td=td, past=past)

    def heads_out(o):
        o = o.reshape(bd, NSA_GROUPS, NSA_HPG, td, NSA_GROUPS, NSA_DH)
        o = jnp.stack([o[:, g, :, :, g] for g in range(NSA_GROUPS)], axis=1)
        return o.transpose(0, 3, 1, 2, 4).reshape(n, NSA_HEADS * NSA_DH)

    dff = w['w_gate'].shape[1]
    prev1 = jnp.zeros((bd, td, dff), F32).at[:, 0].set(s_conv[:, 1])
    prev2 = jnp.zeros((bd, td, dff), F32).at[:, 0].set(s_conv[:, 0]).at[:, 1].set(s_conv[:, 1])
    y, g = _finish(x, o_lat.reshape(n, MLA_HEADS * kv_rank), heads_out(o_cmp), heads_out(o_slc), heads_out(o_win),
                   gn, ga, gb, (prev1.reshape(n, dff), prev2.reshape(n, dff)), w, tm=n, period=td, full_g=True,
                   latent=True)
    kv4 = lambda a: a.reshape(1, bd, td, NSA_GROUPS, NSA_DH)
    win = lambda st, new: jnp.concatenate([st, new.reshape(bd, td, NSA_GROUPS, NSA_DH)], axis=1)[None, :, -nbuf:]
    conv_state = jnp.concatenate([s_conv, g.reshape(bd, td, dff)], axis=1)[None, :, -(CONV_W - 1):]
    states = (ckv.reshape(1, bd, td, kv_rank), krope.reshape(1, bd, td, MLA_D_ROPE), kv4(kc), kv4(vc), kv4(ks), kv4(vs),
              win(s_win_k, kw), win(s_win_v, vw), conv_state)
    return y.reshape(bd, td, d), states


def kernel(x_prompt, x_sample, cache_mla_ckv, cache_mla_krope, cache_nsa_cmp_k, cache_nsa_cmp_v, cache_nsa_slc_k, cache_nsa_slc_v, state_win_k, state_win_v, state_ffn_conv, page_table, norm1_g, w_in, q_norm_g, kv_norm_g, w_uq, w_uk, w_uv, cmp_pos_k, cmp_w1_k, cmp_w2_k, cmp_pos_v, cmp_w1_v, cmp_w2_v, w_proj_mla, w_proj_nsa, w_out, norm2_g, w_gate, w_up, conv_w, conv_b, w_down, norm_f_g):
    assert norm1_g.shape[0] == 1, "single-layer trunk"
    p = dict(norm1_g=norm1_g[0], w_in=w_in[0], q_norm_g=q_norm_g[0], kv_norm_g=kv_norm_g[0], w_uq=w_uq[0],
             w_uk=w_uk[0], w_uv=w_uv[0], cmp_pos_k=cmp_pos_k[0], cmp_w1_k=cmp_w1_k[0], cmp_w2_k=cmp_w2_k[0],
             cmp_pos_v=cmp_pos_v[0], cmp_w1_v=cmp_w1_v[0], cmp_w2_v=cmp_w2_v[0], w_proj_mla=w_proj_mla[0],
             w_proj_nsa=w_proj_nsa[0], w_out=w_out[0], norm2_g=norm2_g[0], w_gate=w_gate[0], w_up=w_up[0],
             conv_w=conv_w[0], conv_b=conv_b[0], w_down=w_down[0], norm_f_g=norm_f_g)
    w = _prep_weights(p)
    y_p, ps = _prompt(x_prompt, w)
    caches = (cache_mla_ckv[0], cache_mla_krope[0], cache_nsa_cmp_k[0], cache_nsa_cmp_v[0], cache_nsa_slc_k[0],
              cache_nsa_slc_v[0], state_win_k[0], state_win_v[0], state_ffn_conv[0])
    y_s, ss = _sample(x_sample, caches, page_table, w)
    out = [y_p, y_s]
    for a, b in zip(ps, ss):
        out += [a, b]
    return tuple(out)
```

```python
import functools

import numpy as np
import jax
import jax.numpy as jnp
from jax import lax
from jax.experimental import pallas as pl
from jax.experimental.pallas import tpu as pltpu

MLA_HEADS = 8
MLA_D_NOPE = 64
MLA_D_ROPE = 32
MLA_D_V = 64
ROPE_THETA = 10000.0
MLA_SCALE = (MLA_D_NOPE + MLA_D_ROPE) ** -0.5
NSA_HEADS = 8
NSA_GROUPS = 2
NSA_HPG = NSA_HEADS // NSA_GROUPS
NSA_DH = 64
NSA_SCALE = NSA_DH ** -0.5
CMP_BLOCK = 32
CMP_STRIDE = 16
SLC_BLOCK = 64
SLC_TOP_N = 16
WINDOW = 512
CONV_W = 3
PAGE_SIZE = 128
EPS = 1e-6
NEG = -1e30
FORCE = 1e9

LOG2E = 1.4426950408889634
LANE = 128
AUX_POS_HI, AUX_POS_LO, AUX_BLK0 = 64, 65, 72
HEAD_PAD = 128
CHUNK_FEATS = CMP_STRIDE * NSA_GROUPS * NSA_DH
FFN_BLOCK = 1536
SLOPES =tuple(float(2.0 ** (-8.0 * (h + 1) / NSA_HEADS)) for h in range(NSA_HEADS))
VMEM_LIMIT = 56 * 1024 * 1024

F32 = jnp.float32
BF16 = jnp.bfloat16
_NT = (((1,), (1,)), ((), ()))


def _cparams(*sem):
    return pltpu.CompilerParams(dimension_semantics=sem, vmem_limit_bytes=VMEM_LIMIT)


def _rms(x, g):
    return x * lax.rsqrt(jnp.mean(x * x, axis=-1, keepdims=True) + EPS) * g


def _dot(a, b):
    return jnp.dot(a, b, preferred_element_type=F32)


def _dot_nt(a, b):
    return lax.dot_general(a, b, _NT, preferred_element_type=F32)


def _dot_exact(a, b):
    return jnp.dot(a, b, preferred_element_type=F32, precision=lax.Precision.HIGHEST)


def _iota(shape, dim):
    return lax.broadcasted_iota(jnp.int32, shape, dim)


def _log2(n):
    assert n > 0 and n & (n - 1) == 0, n
    return n.bit_length() - 1


def _vdiv(x, n):
    return lax.shift_right_logical(x, jnp.full(x.shape, _log2(n), jnp.int32))


def _vmod(x, n):
    assert n & (n - 1) == 0, n
    return x & (n - 1)


_O_CQ, _O_CKV, _O_QN, _O_K6, _O_GA = 0, 384, 640, 1152, 1920


def _inproj_kernel(x_ref, g1_ref, w_ref, qg_ref, kvg_ref, wuq_ref, wk_ref, wv_ref, cos_ref, sin_ref,
                   *outs, sample, q_rank, kv_rank, d_model, tiles_per_seq, q_scale):
    o_gb = _O_GA + d_model
    o_kr = o_gb + d_model
    x = x_ref[...]
    hn = _rms(x, g1_ref[...])
    y = _dot(hn.astype(BF16), w_ref[...])
    cosq = cos_ref[...]
    sinq = sin_ref[...]
    nq = MLA_HEADS * HEAD_PAD
    cqn = _rms(y[:, _O_CQ:_O_CQ + q_rank], qg_ref[...])
    q2 = _dot(cqn.astype(BF16), wuq_ref[...])
    ckv = _rms(y[:, _O_CKV:_O_CKV + kv_rank], kvg_ref[...])
    kr = y[:, o_kr:o_kr + LANE] * cosq + y[:, o_kr + LANE:o_kr + 2 * LANE] * sinq
    ckv_b = ckv.astype(BF16)
    it = iter(outs)
    if sample:
        qrot_ref, qabs_ref = next(it), next(it)
        for h in range(MLA_HEADS):
            sl = slice(h * HEAD_PAD, (h + 1) * HEAD_PAD)
            qh = (q2[:, sl] * cosq + q2[:, nq + h * HEAD_PAD:nq + (h + 1) * HEAD_PAD] * sinq) * q_scale
            qrot_ref[:, sl] = qh
            qabs_ref[:, h * kv_rank:(h + 1) * kv_rank] = _dot(qh.astype(BF16), wk_ref[h])
    else:
        q_ref, k_ref, v_ref = next(it), next(it), next(it)
        knp = _dot(ckv_b, wk_ref[...])
        for h in range(MLA_HEADS):
            sl = slice(h * HEAD_PAD, (h + 1) * HEAD_PAD)
            qh = (q2[:, sl] * cosq + q2[:, nq + h * HEAD_PAD:nq + (h + 1) * HEAD_PAD] * sinq) * q_scale
            q_ref[:, sl] = qh.astype(BF16)
            k_ref[:, sl] = (knp[:, sl] + kr).astype(BF16)
        ones_lane = (_vmod(_iota((1, nq), 1), HEAD_PAD) == MLA_D_V).astype(F32)
        v_ref[...] = (_dot(ckv_b, wv_ref[...]) + ones_lane).astype(BF16)
    ckv_ref, kr_ref, qn_ref = next(it), next(it), next(it)
    ckv_ref[...] = ckv
    kr_ref[...] = kr
    yq = y[:, _O_QN:_O_QN + NSA_HEADS * NSA_DH] * NSA_SCALE
    if sample:
        qn_ref[...] = yq.astype(BF16)
    else:
        lane_q = _iota((x.shape[0], LANE), 1)
        for h in range(NSA_HEADS):
            pair = yq[:, (h // 2) * LANE:(h // 2 + 1) * LANE]
            if h % 2:
                pair = pltpu.roll(pair, NSA_DH, 1)
            aux_q = jnp.where((lane_q == AUX_POS_HI) | (lane_q == AUX_POS_LO), SLOPES[h], 0.0)
            qn_ref[:, h * LANE:(h + 1) * LANE] = jnp.where(lane_q < NSA_DH, pair, aux_q).astype(BF16)
    for j in range(6):
        yj = y[:, _O_K6 + j * LANE:_O_K6 + (j + 1) * LANE]
        if sample:
            next(it)[...] = yj
        else:
            next(it)[...] = yj.T
    if not sample:
        tm = x.shape[0]
        pos = (pl.program_id(0) % tiles_per_seq) * tm + _iota((tm, 1), 0)
        lane = _iota((tm, LANE), 1)
        onehot = ((lane >= AUX_BLK0) & (_vdiv(pos, SLC_BLOCK) == lane - AUX_BLK0)).astype(F32)
        aux = jnp.where(lane == AUX_POS_HI, (pos - _vmod(pos, 256)).astype(F32),
                        jnp.where(lane == AUX_POS_LO, _vmod(pos, 256).astype(F32), onehot))
        for j in range(6):
            yj = y[:, _O_K6 + j * LANE:_O_K6 + (j + 1) * LANE]
            ref = next(it)
            if j in (2, 3, 4, 5):
                tail = aux if j in (2, 4) else (lane == NSA_DH).astype(F32)
                ref[:, :LANE] = jnp.where(lane < NSA_DH, yj, tail).astype(BF16)
                ref[:, LANE:] = jnp.where(lane < NSA_DH, pltpu.roll(yj, NSA_DH, 1), tail).astype(BF16)
            else:
                ref[...] = yj.astype(BF16)
    gn_ref, ga_ref, gb_ref = next(it), next(it), next(it)
    gn_ref[...] = jax.nn.sigmoid(y[:, o_kr + 2 * LANE:o_kr + 3 * LANE])
    ga_ref[...] = jax.nn.sigmoid(y[:, _O_GA:_O_GA + d_model])
    gb_ref[...] = jax.nn.sigmoid(y[:, o_gb:o_gb + d_model])


def _inproj(x, wts, cosq, sinq, *, sample, tm, tab_blocks):
    n, d = x.shape
    q_rank, kv_rank = wts['q_norm_g'].shape[1], wts['kv_norm_g'].shape[1]
    nq = MLA_HEADS * HEAD_PAD
    wk = wts['w_ukT'] if sample else wts['w_ukp']
    row = lambda i: (i, 0)
    const2 = lambda i: (0, 0)
    tab_map = (lambda i: (i % tab_blocks, 0))
    in_specs = [
        pl.BlockSpec((tm, d), row),
        pl.BlockSpec((1, d), const2),
        pl.BlockSpec(wts['w_in'].shape, const2),
        pl.BlockSpec((1, q_rank), const2),
        pl.BlockSpec((1, kv_rank), const2),
        pl.BlockSpec(wts['w_uq2'].shape, const2),
        pl.BlockSpec(wk.shape, (lambda i: (0, 0, 0)) if sample else const2),
        pl.BlockSpec(wts['w_uvf'].shape, const2),
        pl.BlockSpec((tm, LANE), tab_map),
        pl.BlockSpec((tm, LANE), tab_map),
    ]
    shapes = []
    if sample:
        shapes += [(nq, F32), (MLA_HEADS * kv_rank, F32)]
    else:
        shapes += [(nq, BF16), (nq, BF16), (nq, BF16)]
    shapes += [(kv_rank, F32), (LANE, F32), (NSA_HEADS * (NSA_DH if sample else LANE), BF16)]
    n_lead = len(shapes)
    shapes += [(LANE, F32)] * 6
    if not sample:
        shapes += [(LANE, BF16), (LANE, BF16)] + [(2 * LANE, BF16)] * 4
        assert AUX_BLK0 + -(-tab_blocks * tm // SLC_BLOCK) <= LANE, "block one-hot must fit the aux lanes"
    shapes += [(LANE, F32), (d, F32), (d, F32)]
    out_shape = [jax.ShapeDtypeStruct((n, w), dt) for w, dt in shapes]
    out_specs = [pl.BlockSpec((tm, w), row) for w, _ in shapes]
    if not sample:
        tpb = tab_blocks
        for j in range(n_lead, n_lead + 6):
            out_shape[j] = jax.ShapeDtypeStruct((n // (tpb * tm), LANE, tpb * tm), F32)
            out_specs[j] = pl.BlockSpec((None, LANE, tm), lambda i: (i // tpb, 0, i % tpb))
    q_scale = MLA_SCALE if sample else MLA_SCALE * LOG2E
    kern = functools.partial(_inproj_kernel, sample=sample, q_rank=q_rank, kv_rank=kv_rank, d_model=d,
                             tiles_per_seq=tab_blocks, q_scale=q_scale)
    return pl.pallas_call(
        kern, grid=(n // tm,), in_specs=in_specs, out_specs=out_specs, out_shape=out_shape,
        compiler_params=_cparams("parallel"), name="inproj_sample" if sample else "inproj_prompt",
    )(x, wts['norm1_g'], wts['w_in'], wts['q_norm_g'], wts['kv_norm_g'], wts['w_uq2'], wk, wts['w_uvf'],
      cosq, sinq)


def _stack_heads(q_ref, heads, width):
    parts = [q_ref[:, h * width:(h + 1) * width] for h in heads]
    return parts[0] if len(parts) == 1 else jnp.concatenate(parts, axis=0)


def _flash_kernel(*refs, tq, tk, groups, par, dq, dk, dv, window, has_bias, base2):
    if has_bias:
        q_ref, k_ref, v_ref, bias_ref, o_ref = refs
    else:
        q_ref, k_ref, v_ref, o_ref = refs
    q_start = pl.program_id(1) * tq
    n_hi = (q_start + tq - 1) // tk + 1
    hi_full = (q_start + 1) // tk
    if window:
        n_lo = jnp.maximum(q_start - (window - 1), 0) // tk
        lo_full = (jnp.maximum(q_start + tq - window, 0) + tk - 1) // tk
    else:
        n_lo, lo_full = 0, 0
    e1 = jnp.clip(lo_full, n_lo, n_hi)
    e2 = jnp.clip(hi_full, e1, n_hi)
    ex = jnp.exp2 if base2 else jnp.exp
    for c0 in range(0, len(groups), par):
        chunk = groups[c0:c0 + par]
        qs = []
        for heads, kcol in chunk:
            qg = _stack_heads(q_ref, heads, dq)
            if has_bias:
                bias = bias_ref[:, kcol * LANE:(kcol + 1) * LANE]
                qg = qg + jnp.concatenate([bias] * len(heads), axis=0)
            qs.append(qg)
        rows = qs[0].shape[0]
        qpos = q_start + (_iota((rows, 1), 0) & (tq - 1))

        def step(j, carry, masked):
            k0 = pl.multiple_of(j * tk, tk)
            if masked:
                dist = qpos - (k0 + _iota((1, tk), 1))
                mask = dist >= 0
                if window:
                    mask = mask & (dist < window)
            out = []
            for (heads, kcol), qg, (m, acc) in zip(chunk, qs, carry):
                kt = k_ref[pl.ds(k0, tk), kcol * dk:(kcol + 1) * dk]
                vt = v_ref[pl.ds(k0, tk), kcol * LANE:(kcol + 1) * LANE]
                s = _dot_nt(qg, kt)
                if masked:
                    s = jnp.where(mask, s, NEG)
                m_new = jnp.maximum(m, jnp.max(s, axis=-1, keepdims=True))
                p = ex((s - m_new).astype(BF16))
                acc = ex(m - m_new) * acc + _dot(p, vt)
                out.append((m_new, acc))
            return tuple(out)

        carry = tuple((jnp.full((rows, 1), NEG, F32), jnp.zeros((rows, LANE), F32)) for _ in chunk)
        carry = lax.fori_loop(n_lo, e1, functools.partial(step, masked=True), carry)
        carry = lax.fori_loop(e1, e2, functools.partial(step, masked=False), carry)
        carry = lax.fori_loop(e2, n_hi, functools.partial(step, masked=True), carry)
        for (heads, kcol), (m, acc) in zip(chunk, carry):
            o = acc[:, :dv] * (1.0 / acc[:, dv:dv + 1])
            for hh, h in enumerate(heads):
                o_ref[:, h * dv:(h + 1) * dv] = o[hh * tq:(hh + 1) * tq].astype(o_ref.dtype)


def _flash(q, k, v, bias, *, batch, seq, tq, tk, groups, par, dq, dk, dv, window, base2, name):
    n = q.shape[0]
    has_bias = bias is not None
    n_heads = sum(len(g[0]) for g in groups)
    qrow = lambda b, i: (b * (seq // tq) + i, 0)
    kv = lambda b, i: (b, 0)
    in_specs = [pl.BlockSpec((tq, q.shape[1]), qrow), pl.BlockSpec((seq, k.shape[1]), kv),
                pl.BlockSpec((seq, v.shape[1]), kv)]
    args = [q, k, v]
    if has_bias:
        in_specs.append(pl.BlockSpec((tq, bias.shape[1]), qrow))
        args.append(bias)
    kern = functools.partial(_flash_kernel, tq=tq, tk=tk, groups=groups, par=par, dq=dq, dk=dk, dv=dv, window=window,
                             has_bias=has_bias, base2=base2)
    return pl.pallas_call(
        kern, grid=(batch, seq // tq), in_specs=in_specs,
        out_specs=pl.BlockSpec((tq, n_heads * dv), qrow),
        out_shape=jax.ShapeDtypeStruct((n, n_heads * dv), BF16),
        compiler_params=_cparams("parallel", "arbitrary"), name=name,
    )(*args)


def _win_kernel(q_ref, k_ref, v_ref, o_ref, *, tq, groups, dv):
    i = pl.program_id(1)
    rq = _iota((tq, tq), 0)
    ck = _iota((tq, tq), 1)
    bias_own = jnp.where(ck <= rq, 0.0, NEG)
    bias_far = jnp.where(ck > rq, 0.0, NEG) + jnp.where(i >= 2, 0.0, NEG)
    bias_mid = jnp.where(i >= 1, 0.0, NEG)
    starts = (jnp.maximum(i - 2, 0) * tq, jnp.maximum(i - 1, 0) * tq, i * tq)
    for heads, kcol in groups:
        qg = _stack_heads(q_ref, heads, LANE)
        nh = len(heads)
        ss, vs = [], []
        for k0, bias in zip(starts, (bias_far, None, bias_own)):
            k0 = pl.multiple_of(k0, tq)
            s = _dot_nt(qg, k_ref[pl.ds(k0, tq), kcol * LANE:(kcol + 1) * LANE])
            s = s + (bias_mid if bias is None else jnp.concatenate([bias] * nh, axis=0))
            ss.append(s)
            vs.append(v_ref[pl.ds(k0, tq), kcol * LANE:(kcol + 1) * LANE])
        m = functools.reduce(jnp.maximum, [jnp.max(s, axis=-1, keepdims=True) for s in ss])
        acc = functools.reduce(jnp.add, [_dot(jnp.exp((s - m).astype(BF16)), v) for s, v in zip(ss, vs)])
        o = acc[:, :dv] * (1.0 / acc[:, dv:dv + 1])
        for hh, h in enumerate(heads):
            o_ref[:, h * dv:(h + 1) * dv] = o[hh * tq:(hh + 1) * tq].astype(o_ref.dtype)


def _win_prompt(q, k, v, *, batch, seq, tq, groups, dv):
    assert WINDOW == 2 * tq and seq % tq == 0
    n = q.shape[0]
    n_heads = sum(len(g[0]) for g in groups)
    qrow = lambda b, i: (b * (seq // tq) + i, 0)
    kv = lambda b, i: (b, 0)
    return pl.pallas_call(
        functools.partial(_win_kernel, tq=tq, groups=groups, dv=dv), grid=(batch, seq // tq),
        in_specs=[pl.BlockSpec((tq, q.shape[1]), qrow), pl.BlockSpec((seq, k.shape[1]), kv),
                  pl.BlockSpec((seq, v.shape[1]), kv)],
        out_specs=pl.BlockSpec((tq, n_heads * dv), qrow),
        out_shape=jax.ShapeDtypeStruct((n, n_heads * dv), BF16),
        compiler_params=_cparams("parallel", "arbitrary"), name="win_prompt",
    )(q, k, v)


def _compress_rows(xk, xv, w1k_ref, w1v_ref, posk_ref, posv_ref):
    yk = _dot(xk.astype(BF16), w1k_ref[...])
    yv = _dot(xv.astype(BF16), w1v_ref[...])
    return yk, yv


def _compress_finish(y, posy, w2_ref):
    rows = y.shape[0]
    a = y[:, :LANE]
    b = pltpu.roll(y[:, LANE:], rows - 1, 0)
    pos = posy[0:1, :LANE] + posy[1:2, LANE:]
    hid = jax.nn.gelu(a + b + pos)
    return _dot(hid.astype(BF16), w2_ref[...])


def _compress_prompt_kernel(xk_ref, xv_ref, w1k_ref, w1v_ref, pk_ref, pv_ref, w2k_ref, w2v_ref, ok_ref, ov_ref):
    yk, yv = _compress_rows(xk_ref[...], xv_ref[...], w1k_ref, w1v_ref, pk_ref, pv_ref)
    ok_ref[...] = _compress_finish(yk, _dot(pk_ref[...], w1k_ref[...]), w2k_ref).astype(BF16)
    ov_ref[...] = _compress_finish(yv, _dot(pv_ref[...], w1v_ref[...]), w2v_ref).astype(BF16)


def _compress_prompt(kc, vc, wts, *, batch, seq):
    nch = seq // CMP_STRIDE
    xk = kc.reshape(batch * nch, CHUNK_FEATS)
    xv = vc.reshape(batch * nch, CHUNK_FEATS)
    row = lambda b: (b, 0)
    c2 = lambda b: (0, 0)
    wspec = pl.BlockSpec((CHUNK_FEATS, 2 * LANE), c2)
    pspec = pl.BlockSpec((8, CHUNK_FEATS), c2)
    w2spec = pl.BlockSpec((LANE, LANE), c2)
    return pl.pallas_call(
        _compress_prompt_kernel, grid=(batch,),
        in_specs=[pl.BlockSpec((nch, CHUNK_FEATS), row), pl.BlockSpec((nch, CHUNK_FEATS), row),
                  wspec, wspec, pspec, pspec, w2spec, w2spec],
        out_specs=[pl.BlockSpec((nch, LANE), row)] * 2,
        out_shape=[jax.ShapeDtypeStruct((batch * nch, LANE), BF16)] * 2,
        compiler_params=_cparams("parallel"), name="compress_prompt",
    )(xk, xv, wts['cmp_w1k'], wts['cmp_w1v'], wts['cmp_posk'], wts['cmp_posv'], wts['cmp_w2k'], wts['cmp_w2v'])


def _overlap(n_rows, n_sel):
    c = _iota((n_rows, n_sel), 0) * CMP_STRIDE
    j = _iota((n_rows, n_sel), 1) * SLC_BLOCK
    return ((c < j + SLC_BLOCK) & (c + CMP_BLOCK > j)).astype(F32)


def _force_scores(score, cur, jj):
    forced = (jj == 0) | (jj == cur) | (jj == cur - 1)
    score = jnp.where(forced, FORCE, score)
    return jnp.where(jj <= cur, score, NEG)


def _cmp_prompt_kernel(q_ref, k_ref, v_ref, o_ref, bias_ref, *, tq, n_cmp, n_sel):
    q_start = pl.program_id(1) * tq
    ncp = k_ref.shape[0]
    rows = NSA_HPG * tq
    qpos = q_start + (_iota((rows, 1), 0) & (tq - 1))
    cidx = _iota((1, ncp), 1)
    dist = qpos - (cidx * CMP_STRIDE + CMP_BLOCK - 1)
    mask = (dist >= 0) & (cidx < n_cmp)
    distf = dist.astype(F32)
    hrow = _vdiv(_iota((rows, 1), 0), tq)
    nsp = -(-n_sel // 8) * 8
    cur = _vdiv(q_start + _iota((1, tq), 1), SLC_BLOCK)
    jj = _iota((nsp, tq), 0)
    cb = _iota((nsp, ncp), 1) * CMP_STRIDE
    jb = _iota((nsp, ncp), 0) * SLC_BLOCK
    ov_t = ((cb < jb + SLC_BLOCK) & (cb + CMP_BLOCK > jb)).astype(F32)
    for g in range(NSA_GROUPS):
        heads = range(g * NSA_HPG, (g + 1) * NSA_HPG)
        qg = jnp.concatenate([q_ref[:, h * LANE:h * LANE + NSA_DH] for h in heads], axis=0)
        slope = jnp.zeros((rows, 1), F32)
        for hh, h in enumerate(heads):
            slope = jnp.where(hrow == hh, SLOPES[h], slope)
        s = _dot_nt(qg, k_ref[:, g * NSA_DH:(g + 1) * NSA_DH]) - slope * distf
        s = jnp.where(mask, s, NEG)
        m = jnp.max(s, axis=-1, keepdims=True)
        p = jnp.where(mask, jnp.exp(s - m), 0.0)
        l = jnp.sum(p, axis=-1, keepdims=True)
        p = p / jnp.where(l > 0.0, l, 1.0)
        o = _dot(p.astype(BF16), v_ref[:, g * NSA_DH:(g + 1) * NSA_DH])
        imp = p[0:tq]
        for hh in range(1, NSA_HPG):
            imp = imp + p[hh * tq:(hh + 1) * tq]
            o_ref[:, (g * NSA_HPG + hh) * NSA_DH:(g * NSA_HPG + hh + 1) * NSA_DH] = o[hh * tq:(hh + 1) * tq].astype(BF16)
        o_ref[:, g * NSA_HPG * NSA_DH:(g * NSA_HPG + 1) * NSA_DH] = o[0:tq].astype(BF16)
        score = lax.dot_general(ov_t, imp, _NT, preferred_element_type=F32, precision=lax.Precision.HIGHEST)
        score = _force_scores(score, cur, jj)
        rank = jnp.zeros((nsp, tq), F32)
        for i in range(n_sel):
            ri = score[i:i + 1, :]
            beats = (ri > score) | ((ri == score) & (i < jj))
            rank = rank + beats.astype(F32)
        sel = (rank < float(min(SLC_TOP_N, n_sel))) & (jj <= cur)
        bias_t = jnp.where(sel | (jj >= n_sel), 0.0, NEG)
        bias_t = jnp.concatenate([jnp.zeros((AUX_BLK0, tq), F32), bias_t,
                                  jnp.zeros((LANE - AUX_BLK0 - nsp, tq), F32)], axis=0)
        bias_ref[:, g * LANE:(g + 1) * LANE] = bias_t.T.astype(BF16)


def _cmp_prompt(qn, kcc, vcc, *, batch, seq, tq):
    n = qn.shape[0]
    nch = seq // CMP_STRIDE
    n_cmp = nch - CMP_BLOCK // CMP_STRIDE + 1
    n_sel = -(-seq // SLC_BLOCK)
    qrow = lambda b, i: (b * (seq // tq) + i, 0)
    kv = lambda b, i: (b, 0)
    kern = functools.partial(_cmp_prompt_kernel, tq=tq, n_cmp=n_cmp, n_sel=n_sel)
    return pl.pallas_call(
        kern, grid=(batch, seq // tq),
        in_specs=[pl.BlockSpec((tq, qn.shape[1]), qrow), pl.BlockSpec((nch, LANE), kv), pl.BlockSpec((nch, LANE), kv)],
        out_specs=[pl.BlockSpec((tq, NSA_HEADS * NSA_DH), qrow), pl.BlockSpec((tq, NSA_GROUPS * LANE), qrow)],
        out_shape=[jax.ShapeDtypeStruct((n, NSA_HEADS * NSA_DH), BF16),
                   jax.ShapeDtypeStruct((n, NSA_GROUPS * LANE), BF16)],
        compiler_params=_cparams("parallel", "arbitrary"), name="cmp_prompt",
    )(qn, kcc, vcc)


def _finish_kernel(*refs, period, latent, has_state):
    it = iter(refs)
    x_ref, omla_ref, ocmp_ref, oslc_ref, owin_ref, gn_ref, ga_ref, gb_ref = (next(it) for _ in range(8))
    prev1_ref, prev2_ref = (next(it), next(it)) if has_state else (None, None)
    gx_ref = next(it)
    wuv_ref = next(it) if latent else None
    (wpm_ref, wpn_ref, wo_ref, g2_ref, wg_ref, wu_ref, cw_ref, cb_ref, wd_ref, gf_ref,
     y_ref, gout_ref, carry_ref) = it
    tm = x_ref.shape[0]
    gn = gn_ref[...]
    gn_hi = gn.astype(BF16)
    gn_lo = (gn - gn_hi.astype(F32)).astype(BF16)
    gexp = _dot(gn_hi, gx_ref[...]) + _dot(gn_lo, gx_ref[...])
    w = NSA_HEADS * NSA_DH
    o_nsa = (gexp[:, 0:w] * ocmp_ref[...].astype(F32) + gexp[:, w:2 * w] * oslc_ref[...].astype(F32)
             + gexp[:, 2 * w:3 * w] * owin_ref[...].astype(F32))
    o_mla = omla_ref[...].astype(BF16)
    if latent:
        o_mla = _dot(o_mla, wuv_ref[...]).astype(BF16)
    merged = (ga_ref[...] * _dot(o_mla, wpm_ref[...])
              + gb_ref[...] * _dot(o_nsa.astype(BF16), wpn_ref[...]))
    x1 = x_ref[...] + _dot(merged.astype(BF16), wo_ref[...])
    h2 = _rms(x1, g2_ref[...]).astype(BF16)
    row = _iota((tm, 1), 0)
    i = pl.program_id(0)
    t = _vmod(i * tm + row, period)
    carried = period > tm
    if carried:
        @pl.when(i == 0)
        def _():
            carry_ref[...] = jnp.zeros_like(carry_ref)
    dff = wg_ref.shape[1]
    x2 = x1
    for c0 in range(0, dff, FFN_BLOCK):
        cs = slice(c0, min(c0 + FFN_BLOCK, dff))
        g = _dot(h2, wg_ref[:, cs])
        u = _dot(h2, wu_ref[:, cs])
        g1 = pltpu.roll(g, 1, 0)
        g2 = pltpu.roll(g, 2, 0)
        if carried:
            c = carry_ref[:, cs]
            g1 = jnp.where(row == 0, c[7:8], g1)
            g2 = jnp.where(row == 0, c[6:7], jnp.where(row == 1, c[7:8], g2))
            carry_ref[:, cs] = g[tm - 8:tm]
        g1 = jnp.where(t >= 1, g1, prev1_ref[:, cs] if has_state else 0.0)
        g2 = jnp.where(t >= 2, g2, prev2_ref[:, cs] if has_state else 0.0)
        cw = cw_ref[:, cs]
        conv = cb_ref[:, cs] + cw[0:1] * g2 + cw[1:2] * g1 + cw[2:3] * g
        act = (jax.nn.silu(conv) * u).astype(BF16)
        x2 = x2 + _dot(act, wd_ref[cs, :])
        gout_ref[:, cs] = g[tm - 8:tm] if gout_ref.shape[0] == 8 else g
    y_ref[...] = _rms(x2, gf_ref[...])


def _finish(x, omla, ocmp, oslc, owin, gn, ga, gb, state_rows, wts, *, tm, period, full_g, latent):
    n, d = x.shape
    dff = wts['w_gate'].shape[1]
    row = lambda i: (i, 0)
    c2 = lambda i: (0, 0)
    acts = [x, omla, ocmp, oslc, owin, gn, ga, gb] + (list(state_rows) if state_rows is not None else [])
    consts = [wts['gate_expand']] + ([wts['w_uvbd']] if latent else []) + [
        wts['w_proj_mla'], wts['w_proj_nsa'], wts['w_out'], wts['norm2_g'],
        wts['w_gate'], wts['w_up'], wts['conv_w'], wts['conv_b'], wts['w_down'], wts['norm_f_g']]
    ins = acts + consts
    in_specs = [pl.BlockSpec((tm, a.shape[1]), row) for a in acts] + [pl.BlockSpec(a.shape, c2) for a in consts]
    g_rows = n if full_g else (n // tm) * 8
    g_blk = tm if full_g else 8
    kern = functools.partial(_finish_kernel, period=period, latent=latent, has_state=state_rows is not None)
    return pl.pallas_call(
        kern, grid=(n // tm,), in_specs=in_specs,
        out_specs=[pl.BlockSpec((tm, d), row), pl.BlockSpec((g_blk, dff), row)],
        out_shape=[jax.ShapeDtypeStruct((n, d), F32), jax.ShapeDtypeStruct((g_rows, dff), F32)],
        scratch_shapes=[pltpu.VMEM((8, dff), F32)],
        compiler_params=_cparams("arbitrary"), name="finish_full" if full_g else "finish_tiled",
    )(*ins)


def _swap_halves(w):
    hlf = w.shape[-1] // 2
    return jnp.concatenate([-w[..., hlf:], w[..., :hlf]], axis=-1)


def _prep_weights(p):
    d = p['w_in'].shape[0]
    q_rank, kv_rank = p['q_norm_g'].shape[-1], p['kv_norm_g'].shape[-1]
    sizes = [q_rank, kv_rank, MLA_D_ROPE, NSA_HEADS * NSA_DH] + [2 * NSA_GROUPS * NSA_DH] * 3 + [3 * NSA_HEADS, d, d]
    cuts = np.cumsum(sizes)[:-1].tolist()
    cq, ckv, kr, qn, kvc, kvs, kvw, gn, ga, gb = jnp.split(p['w_in'], cuts, axis=-1)
    assert _O_CKV == q_rank and _O_QN == q_rank + kv_rank
    lo, hi = MLA_D_NOPE, HEAD_PAD - MLA_D_NOPE - MLA_D_ROPE
    place = lambda w: jnp.pad(w, ((0, 0), (lo, hi)))
    gnp = jnp.pad(gn, ((0, 0), (0, LANE - gn.shape[1])))
    w_in = jnp.concatenate([cq, ckv, qn, kvc, kvs, kvw, ga, gb, place(kr), place(_swap_halves(kr)), gnp], axis=1)
    w = {'w_in': w_in.astype(BF16)}
    for k in ('norm1_g', 'q_norm_g', 'kv_norm_g', 'norm2_g', 'conv_b'):
        w[k] = p[k].reshape(1, -1)
    w['norm_f_g'] = p['norm_f_g'].reshape(1, -1)
    w['conv_w'] = jnp.pad(p['conv_w'], ((0, 8 - CONV_W), (0, 0)))
    uq = p['w_uq']
    hpad = ((0, 0), (0, 0), (0, HEAD_PAD - uq.shape[-1]))
    uq_a = jnp.pad(uq, hpad)
    uq_b = jnp.pad(jnp.concatenate([jnp.zeros_like(uq[..., :MLA_D_NOPE]), _swap_halves(uq[..., MLA_D_NOPE:])], -1), hpad)
    w['w_uq2'] = jnp.concatenate([uq_a.reshape(q_rank, -1), uq_b.reshape(q_rank, -1)], axis=1).astype(BF16)
    uk = p['w_uk']
    w['w_ukp'] = jnp.pad(uk, ((0, 0), (0, 0), (0, HEAD_PAD - MLA_D_NOPE))).reshape(kv_rank, -1).astype(BF16)
    w['w_ukT'] = jnp.pad(jnp.transpose(uk, (1, 2, 0)), ((0, 0), (0, HEAD_PAD - MLA_D_NOPE), (0, 0))).astype(BF16)
    w['w_uvf'] = jnp.pad(p['w_uv'], ((0, 0), (0, 0), (0, HEAD_PAD - MLA_D_V))).reshape(kv_rank, -1).astype(BF16)
    eye_h = jnp.eye(MLA_HEADS, dtype=F32)
    w['w_uvbd'] = jnp.einsum('rhv,hk->hrkv', p['w_uv'], eye_h).reshape(MLA_HEADS * kv_rank, -1).astype(BF16)
    eye_g = jnp.eye(NSA_GROUPS, dtype=F32)
    for nm in ('k', 'v'):
        w1 = p['cmp_w1_' + nm].reshape(2, CMP_STRIDE, NSA_DH, -1)
        big = jnp.einsum('ajdh,gk->jgdakh', w1, eye_g)
        w['cmp_w1' + nm] = big.reshape(CHUNK_FEATS, -1).astype(BF16)
        pos = p['cmp_pos_' + nm].reshape(2, CMP_STRIDE, 1, NSA_DH)
        pos = jnp.broadcast_to(pos, (2, CMP_STRIDE, NSA_GROUPS, NSA_DH)).reshape(2, CHUNK_FEATS)
        w['cmp_pos' + nm] = jnp.pad(pos, ((0, 6), (0, 0))).astype(BF16)
        w2 = p['cmp_w2_' + nm]
        w['cmp_w2' + nm] = jnp.einsum('hd,gk->ghkd', w2, eye_g).reshape(NSA_GROUPS * w2.shape[0], -1).astype(BF16)
    ge = np.zeros((LANE, 3 * NSA_HEADS * NSA_DH), np.float32)
    for h in range(NSA_HEADS):
        for i in range(3):
            ge[h * 3 + i, i * NSA_HEADS * NSA_DH + h * NSA_DH:i * NSA_HEADS * NSA_DH + (h + 1) * NSA_DH] = 1.0
    w['gate_expand'] = jnp.asarray(ge)
    for k in ('w_proj_mla', 'w_proj_nsa', 'w_out', 'w_gate', 'w_up', 'w_down'):
        w[k] = p[k].astype(BF16)
    return w


def _rope_tables(pos):
    inv = ROPE_THETA ** (-jnp.arange(0, MLA_D_ROPE, 2, dtype=F32) / MLA_D_ROPE)
    ang = pos.astype(F32)[:, None] * inv[None, :]
    cos, sin = jnp.cos(ang), jnp.sin(ang)
    n = pos.shape[0]
    pad = jnp.zeros((n, HEAD_PAD - MLA_D_NOPE - MLA_D_ROPE), F32)
    cosq = jnp.concatenate([jnp.ones((n, MLA_D_NOPE), F32), cos, cos, pad], axis=1)
    sinq = jnp.concatenate([jnp.zeros((n, MLA_D_NOPE), F32), sin, sin, pad], axis=1)
    return cosq, sinq


_NSA_GROUPS_SPEC = tuple((tuple(range(g * NSA_HPG, (g + 1) * NSA_HPG)), g) for g in range(NSA_GROUPS))
_MLA_GROUPS_SPEC = tuple(((h,), h) for h in range(MLA_HEADS))


def _prompt(x_prompt, w):
    b, t, d = x_prompt.shape
    n = b * t
    x = x_prompt.reshape(n, d)
    tm = 256
    cosq, sinq = _rope_tables(jnp.arange(t, dtype=jnp.int32))
    (q_mla, k_mla, v_mla, ckv, krp, qn, kc, vc, ks, vs, kw, vw, kc_b, vc_b, ks_b, vs_b, kw_b, vw_b, gn, ga, gb) = _inproj(
        x, w, cosq, sinq, sample=False, tm=tm, tab_blocks=t // tm)
    o_mla = _flash(q_mla, k_mla, v_mla, None, batch=b, seq=t, tq=512, tk=512, groups=_MLA_GROUPS_SPEC, par=8,
                   dq=HEAD_PAD, dk=HEAD_PAD, dv=MLA_D_V, window=0, base2=True, name="mla_prompt")
    kcc, vcc = _compress_prompt(kc_b, vc_b, w, batch=b, seq=t)
    o_cmp, sel_bias = _cmp_prompt(qn, kcc, vcc, batch=b, seq=t, tq=128)
    o_slc = _flash(qn, ks_b, vs_b, sel_bias, batch=b, seq=t, tq=256, tk=512, groups=_NSA_GROUPS_SPEC, par=2,
                   dq=LANE, dk=LANE, dv=NSA_DH, window=0, base2=False, name="slc_prompt")
    o_win = _win_prompt(qn, kw_b, vw_b, batch=b, seq=t, tq=WINDOW // 2, groups=_NSA_GROUPS_SPEC, dv=NSA_DH)
    dff = w['w_gate'].shape[1]
    y, gtail = _finish(x, o_mla, o_cmp, o_slc, o_win, gn, ga, gb, None, w, tm=tm, period=t, full_g=False,
                       latent=False)
    kv4 = lambda a: a.reshape(1, b, NSA_GROUPS, NSA_DH, a.shape[-1]).transpose(0, 1, 4, 2, 3)
    n_keep = min(WINDOW, t)
    kw, vw = kw[:, :, t - n_keep:], vw[:, :, t - n_keep:]
    conv_state = gtail.reshape(b, t // tm, 8, dff)[:, -1, 8 - (CONV_W - 1):, :]
    states = (ckv.reshape(1, b, t, -1), krp[:, MLA_D_NOPE:MLA_D_NOPE + MLA_D_ROPE].reshape(1, b, t, MLA_D_ROPE),
              kv4(kc), kv4(vc), kv4(ks), kv4(vs), kv4(kw), kv4(vw), conv_state[None])
    return y.reshape(b, t, d), states


def _page_copies(pt_ref, pools, bufs, sems, step, slot, pp):
    copies = []
    for k in range(pp):
        page = pt_ref[step * pp + k]
        for pool, buf, sem in zip(pools, bufs, sems):
            copies.append(pltpu.make_async_copy(pool.at[page], buf.at[slot, k], sem.at[slot]))
    return copies


def _start_all(copies, n_pools):
    for i, c in enumerate(copies):
        c.start(priority=(i // n_pools) % 2)


def _stream_pages(pt_ref, pools, bufs, sems, pp):
    step = pl.program_id(0) * pl.num_programs(1) + pl.program_id(1)
    total = pl.num_programs(0) * pl.num_programs(1)
    slot = step % 2

    @pl.when(step == 0)
    def _():
        _start_all(_page_copies(pt_ref, pools, bufs, sems, step, slot, pp), len(pools))

    @pl.when(step + 1 < total)
    def _():
        _start_all(_page_copies(pt_ref, pools, bufs, sems, step + 1, 1 - slot, pp), len(pools))

    for c in _page_copies(pt_ref, pools, bufs, sems, step, slot, pp):
        c.wait()
    return slot


def _softmax_update(sc, v, m_scr, l_scr, acc_scr, v_transposed=False):
    m_old = m_scr[...]
    m_new = jnp.maximum(m_old, jnp.max(sc, axis=-1, keepdims=True))
    p = jnp.exp(sc - m_new)
    alpha = jnp.exp(m_old - m_new)
    l_scr[...] = alpha * l_scr[...] + jnp.sum(p, axis=-1, keepdims=True)
    pv = _dot_nt(p.astype(BF16), v) if v_transposed else _dot(p.astype(BF16), v)
    acc_scr[...] = alpha * acc_scr[...] + pv
    m_scr[...] = m_new


def _mla_decode_step(slot, q_ref, knew_ref, o_ref, cbuf, rbuf, kscr, krscr, m_scr, l_scr, acc_scr, *, pp, td, kv_rank):
    s = pl.program_id(1)
    q = q_ref[...]
    rows = q.shape[0]

    @pl.when(s == 0)
    def _():
        m_scr[...] = jnp.full_like(m_scr, NEG)
        l_scr[...] = jnp.zeros_like(l_scr)
        acc_scr[...] = jnp.zeros_like(acc_scr)
        kn = knew_ref[...]
        trow = _vdiv(_iota((rows, 1), 0), MLA_HEADS)
        col = _iota((1, kn.shape[0]), 1)
        sc = jnp.where((col <= trow) & (col < td), _dot_nt(q, kn), NEG)
        _softmax_update(sc, kn[:, :kv_rank], m_scr, l_scr, acc_scr)

    for k in range(pp):
        kscr[k * PAGE_SIZE:(k + 1) * PAGE_SIZE, :] = cbuf[slot, k].astype(BF16)
        krscr[:, k * PAGE_SIZE:(k + 1) * PAGE_SIZE] = rbuf[slot, k].astype(BF16)
    kt = kscr[...]
    sc = _dot_nt(q[:, :kv_rank], kt) + _dot(q[:, kv_rank:kv_rank + MLA_D_ROPE], krscr[...])
    _softmax_update(sc, kt, m_scr, l_scr, acc_scr)

    @pl.when(s == pl.num_programs(1) - 1)
    def _():
        o_ref[...] = acc_scr[...] / l_scr[...]


def _alibi_rows(rows, td):
    r = _iota((rows, 1), 0)
    return _vdiv(r, td), _vmod(r, td)


def _slope_rows(hrow, g):
    slope = jnp.zeros(hrow.shape, F32)
    for hh in range(NSA_HPG):
        slope = jnp.where(hrow == hh, SLOPES[g * NSA_HPG + hh], slope)
    return slope


def _cmp_decode_step(slot, q_ref, w1k_ref, w1v_ref, pk_ref, pv_ref, w2k_ref, w2v_ref, o_ref, score_ref,
                     kbuf, vbuf, kp0, kp1, vp0, vp1, yk_scr, yv_scr, *, pp, td, past, n_cmp, n_sel_pad):
    s = pl.program_id(1)
    cpp = PAGE_SIZE // CMP_STRIDE
    hp = pp // 2
    for half, (kp_scr, vp_scr) in enumerate(((kp0, vp0), (kp1, vp1))):
        for k in range(hp):
            kp_scr[k * PAGE_SIZE:(k + 1) * PAGE_SIZE, :] = kbuf[slot, half * hp + k].T
            vp_scr[k * PAGE_SIZE:(k + 1) * PAGE_SIZE, :] = vbuf[slot, half * hp + k].T
    chunk_rows = lambda scr: jnp.concatenate(
        [scr[pl.ds(j, hp * cpp, stride=CMP_STRIDE), :] for j in range(CMP_STRIDE)], axis=1).astype(BF16)
    for half, (kp_scr, vp_scr) in enumerate(((kp0, vp0), (kp1, vp1))):
        r0 = pl.multiple_of(s * (pp * cpp) + half * (hp * cpp), hp * cpp)
        yk_scr[pl.ds(r0, hp * cpp), :] = _dot(chunk_rows(kp_scr), w1k_ref[...])
        yv_scr[pl.ds(r0, hp * cpp), :] = _dot(chunk_rows(vp_scr), w1v_ref[...])

    @pl.when(s == pl.num_programs(1) - 1)
    def _():
        kcc = _compress_finish(yk_scr[...], _dot(pk_ref[...], w1k_ref[...]), w2k_ref).astype(BF16)
        vcc = _compress_finish(yv_scr[...], _dot(pv_ref[...], w1v_ref[...]), w2v_ref).astype(BF16)
        ncp = kcc.shape[0]
        rows = NSA_HPG * td
        hrow, trow = _alibi_rows(rows, td)
        cidx = _iota((1, ncp), 1)
        dist = (past + trow) - (cidx * CMP_STRIDE + CMP_BLOCK - 1)
        mask = (dist >= 0) & (cidx < n_cmp)
        distf = dist.astype(F32)
        tsum = (_vmod(_iota((8, rows), 1), td) == _iota((8, rows), 0)).astype(F32)
        ov = _overlap(ncp, n_sel_pad)
        t8 = _iota((8, 1), 0)
        cur = _vdiv(past + t8, SLC_BLOCK)
        jj = _iota((8, n_sel_pad), 1)
        for g in range(NSA_GROUPS):
            sc = _dot_nt(q_ref[g], kcc) - _slope_rows(hrow, g) * distf
            sc = jnp.where(mask, sc, NEG)
            m = jnp.max(sc, axis=-1, keepdims=True)
            p = jnp.where(mask, jnp.exp(sc - m), 0.0)
            l = jnp.sum(p, axis=-1, keepdims=True)
            p = p / jnp.where(l > 0.0, l, 1.0)
            o_ref[g] = _dot(p.astype(BF16), vcc)
            imp = _dot_exact(tsum, p)
            score_ref[g] = _force_scores(_dot_exact(imp, ov), cur, jj)


def _mla_cmp_decode_kernel(pt_ref, qm_ref, knew_ref, qc_ref, w1k_ref, w1v_ref, pk_ref, pv_ref, w2k_ref, w2v_ref,
                           ckv_hbm, kr_hbm, ck_hbm, cv_hbm, olat_ref, ocmp_ref, score_ref,
                           cbuf, rbuf, kbuf, vbuf, csem, rsem, ksem, vsem,
                           kscr, krscr, m_scr, l_scr, acc_scr, kp0, kp1, vp0, vp1, yk_scr, yv_scr,
                           *, pp, td, kv_rank, past, n_cmp, n_sel_pad):
    slot = _stream_pages(pt_ref, (ckv_hbm, kr_hbm, ck_hbm, cv_hbm), (cbuf, rbuf, kbuf, vbuf),
                         (csem, rsem, ksem, vsem), pp)
    _mla_decode_step(slot, qm_ref, knew_ref, olat_ref, cbuf, rbuf, kscr, krscr, m_scr, l_scr, acc_scr,
                     pp=pp, td=td, kv_rank=kv_rank)
    _cmp_decode_step(slot, qc_ref, w1k_ref, w1v_ref, pk_ref, pv_ref, w2k_ref, w2v_ref, ocmp_ref, score_ref,
                     kbuf, vbuf, kp0, kp1, vp0, vp1, yk_scr, yv_scr, pp=pp, td=td, past=past, n_cmp=n_cmp,
                     n_sel_pad=n_sel_pad)


def _mla_cmp_decode(pt, qd_mla, knew, qd, ckv_pool, kr_pool, k_pool, v_pool, wts, *, td, pp, past, n_sel_pad):
    bd, mrows, qw = qd_mla.shape
    kv_rank = ckv_pool.shape[-1]
    n_pages = pt.shape[0] // bd
    cpp = PAGE_SIZE // CMP_STRIDE
    nch = n_pages * cpp
    n_cmp = (past + td) // CMP_STRIDE - CMP_BLOCK // CMP_STRIDE + 1
    assert (past + td) // CMP_STRIDE == nch, "new rows must not complete a chunk"
    per_b3 = lambda b, s, pt: (b, 0, 0)
    per_b = lambda b, s, pt: (b, 0, 0, 0)
    c2 = lambda b, s, pt: (0, 0)
    rows = qd.shape[2]
    in_specs = ([pl.BlockSpec((None, mrows, qw), per_b3), pl.BlockSpec((None,) + knew.shape[1:], per_b3),
                 pl.BlockSpec((None,) + qd.shape[1:], per_b)]
                + [pl.BlockSpec((CHUNK_FEATS, 2 * LANE), c2) for _ in range(2)]
                + [pl.BlockSpec((8, CHUNK_FEATS), c2) for _ in range(2)]
                + [pl.BlockSpec((LANE, LANE), c2) for _ in range(2)]
                + [pl.BlockSpec(memory_space=pl.ANY) for _ in range(4)])
    kern = functools.partial(_mla_cmp_decode_kernel, pp=pp, td=td, kv_rank=kv_rank, past=past, n_cmp=n_cmp,
                             n_sel_pad=n_sel_pad)
    pools = (ckv_pool, kr_pool, k_pool, v_pool)
    return pl.pallas_call(
        kern,
        grid_spec=pltpu.PrefetchScalarGridSpec(
            num_scalar_prefetch=1, grid=(bd, n_pages // pp), in_specs=in_specs,
            out_specs=[pl.BlockSpec((None, mrows, kv_rank), per_b3),
                       pl.BlockSpec((None, NSA_GROUPS, rows, LANE), per_b),
                       pl.BlockSpec((None, NSA_GROUPS, 8, n_sel_pad), per_b)],
            scratch_shapes=[pltpu.VMEM((2, pp) + p.shape[1:], F32) for p in pools]
            + [pltpu.SemaphoreType.DMA((2,)) for _ in pools]
            + [pltpu.VMEM((pp * PAGE_SIZE, kv_rank), BF16), pltpu.VMEM((MLA_D_ROPE, pp * PAGE_SIZE), BF16),
               pltpu.VMEM((mrows, 1), F32), pltpu.VMEM((mrows, 1), F32), pltpu.VMEM((mrows, kv_rank), F32)]
            + [pltpu.VMEM((pp // 2 * PAGE_SIZE, LANE), F32) for _ in range(4)]
            + [pltpu.VMEM((nch, 2 * LANE), F32) for _ in range(2)]),
        out_shape=[jax.ShapeDtypeStruct((bd, mrows, kv_rank), F32),
                   jax.ShapeDtypeStruct((bd, NSA_GROUPS, rows, LANE), F32),
                   jax.ShapeDtypeStruct((bd, NSA_GROUPS, 8, n_sel_pad), F32)],
        compiler_params=_cparams("arbitrary", "arbitrary"), name="mla_cmp_decode",
    )(pt, qd_mla, knew, qd, wts['cmp_w1k'], wts['cmp_w1v'], wts['cmp_posk'], wts['cmp_posv'],
      wts['cmp_w2k'], wts['cmp_w2v'], *pools)


def _rank_kernel(score_ref, cur_ref, sel_ref, *, n_sel):
    sc = score_ref[...]
    jj = _iota(sc.shape, 0)

    def body(i, rank):
        ri = score_ref[pl.ds(i, 1), :]
        beats = (ri > sc) | ((ri == sc) & (i < jj))
        return rank + beats.astype(F32)

    rank = lax.fori_loop(0, n_sel, body, jnp.zeros(sc.shape, F32))
    sel = (rank < float(min(SLC_TOP_N, n_sel))) & (jj <= cur_ref[...])
    sel_ref[...] = sel.astype(F32)


def _rank(score_t, cur, *, n_sel):
    full = lambda a: pl.BlockSpec(a.shape, lambda: (0,) * a.ndim)
    return pl.pallas_call(
        functools.partial(_rank_kernel, n_sel=n_sel), in_specs=[full(score_t), full(cur)],
        out_specs=full(score_t), out_shape=jax.ShapeDtypeStruct(score_t.shape, F32), name="rank_decode",
    )(score_t, cur)


def _slc_decode_kernel(pt_ref, q_ref, sel_ref, selnew_ref, knew_ref, vnew_ref, expand_ref, k_hbm, v_hbm, o_ref,
                       kbuf, vbuf, ksem, vsem, kscr, vscr, m_scr, l_scr, acc_scr, *, pp, td, past):
    slot = _stream_pages(pt_ref, (k_hbm, v_hbm), (kbuf, vbuf), (ksem, vsem), pp)
    s = pl.program_id(1)
    rows = q_ref.shape[1]
    hrow, trow = _alibi_rows(rows, td)
    tk = pp * PAGE_SIZE

    @pl.when(s == 0)
    def _():
        m_scr[...] = jnp.full_like(m_scr, NEG)
        l_scr[...] = jnp.zeros_like(l_scr)
        acc_scr[...] = jnp.zeros_like(acc_scr)
        kn = knew_ref[...]
        vn = vnew_ref[...]
        col = _iota((1, kn.shape[0]), 1)
        dist = trow - col
        for g in range(NSA_GROUPS):
            sc = _dot_nt(q_ref[g], kn) - _slope_rows(hrow, g) * dist.astype(F32)
            mask = (dist >= 0) & (col < td) & (selnew_ref[g] > 0.5)
            _softmax_update(jnp.where(mask, sc, NEG), vn, m_scr.at[g], l_scr.at[g], acc_scr.at[g])

    for k in range(pp):
        kscr[:, k * PAGE_SIZE:(k + 1) * PAGE_SIZE] = kbuf[slot, k].astype(BF16)
        vscr[:, k * PAGE_SIZE:(k + 1) * PAGE_SIZE] = vbuf[slot, k].astype(BF16)
    kt = kscr[...]
    vt = vscr[...]
    kpos = s * tk + _iota((1, tk), 1)
    distf = ((past + trow) - kpos).astype(F32)
    for g in range(NSA_GROUPS):
        sc = _dot(q_ref[g], kt) - _slope_rows(hrow, g) * distf
        selx = _dot(sel_ref[g], expand_ref[...]) > 0.5
        _softmax_update(jnp.where(selx, sc, NEG), vt, m_scr.at[g], l_scr.at[g], acc_scr.at[g], v_transposed=True)

    @pl.when(s == pl.num_programs(1) - 1)
    def _():
        o_ref[...] = acc_scr[...] / l_scr[...]


def _slc_decode(pt, qd, sel16, selnew, knew, vnew, k_pool, v_pool, *, td, pp, past):
    bd = qd.shape[0]
    rows = qd.shape[2]
    n_pages = pt.shape[0] // bd
    tk = pp * PAGE_SIZE
    expand = (jnp.arange(tk)[None, :] // SLC_BLOCK == jnp.arange(tk // SLC_BLOCK)[:, None]).astype(BF16)
    per_b = lambda b, s, pt: (b, 0, 0, 0)
    per_b3 = lambda b, s, pt: (b, 0, 0)
    in_specs = ([pl.BlockSpec((None,) + qd.shape[1:], per_b),
                 pl.BlockSpec((None, None) + sel16.shape[2:], lambda b, s, pt: (b, s, 0, 0, 0)),
                 pl.BlockSpec((None,) + selnew.shape[1:], per_b),
                 pl.BlockSpec((None,) + knew.shape[1:], per_b3), pl.BlockSpec((None,) + vnew.shape[1:], per_b3),
                 pl.BlockSpec(expand.shape, lambda b, s, pt: (0, 0))]
                + [pl.BlockSpec(memory_space=pl.ANY) for _ in range(2)])
    kern = functools.partial(_slc_decode_kernel, pp=pp, td=td, past=past)
    return pl.pallas_call(
        kern,
        grid_spec=pltpu.PrefetchScalarGridSpec(
            num_scalar_prefetch=1, grid=(bd, n_pages // pp), in_specs=in_specs,
            out_specs=pl.BlockSpec((None, NSA_GROUPS, rows, LANE), per_b),
            scratch_shapes=[pltpu.VMEM((2, pp) + k_pool.shape[1:], F32) for _ in range(2)]
            + [pltpu.SemaphoreType.DMA((2,)) for _ in range(2)]
            + [pltpu.VMEM((LANE, pp * PAGE_SIZE), BF16) for _ in range(2)]
            + [pltpu.VMEM((NSA_GROUPS, rows, 1), F32) for _ in range(2)] + [pltpu.VMEM((NSA_GROUPS, rows, LANE), F32)]),
        out_shape=jax.ShapeDtypeStruct((bd, NSA_GROUPS, rows, LANE), F32),
        compiler_params=_cparams("arbitrary", "arbitrary"), name="slc_decode",
    )(pt, qd, sel16, selnew, knew, vnew, expand, k_pool, v_pool)


def _win_decode_kernel(q_ref, kst_ref, vst_ref, knew_ref, vnew_ref, o_ref, *, td, past):
    rows = q_ref.shape[1]
    hrow, trow = _alibi_rows(rows, td)
    nbuf = kst_ref.shape[1]
    kst = kst_ref[...].astype(BF16)
    vst = vst_ref[...].astype(BF16)
    kn = knew_ref[...]
    vn = vnew_ref[...]
    kpos = past - nbuf + _iota((1, nbuf), 1)
    d1 = (past + trow) - kpos
    m1 = (d1 >= 0) & (d1 < WINDOW) & (kpos >= 0)
    col = _iota((1, kn.shape[0]), 1)
    d2 = trow - col
    m2 = (d2 >= 0) & (d2 < WINDOW) & (col < td)
    for g in range(NSA_GROUPS):
        slope = _slope_rows(hrow, g)
        s1 = jnp.where(m1, _dot(q_ref[g], kst) - slope * d1.astype(F32), NEG)
        s2 = jnp.where(m2, _dot_nt(q_ref[g], kn) - slope * d2.astype(F32), NEG)
        m = jnp.maximum(jnp.max(s1, axis=-1, keepdims=True), jnp.max(s2, axis=-1, keepdims=True))
        p1 = jnp.exp(s1 - m)
        p2 = jnp.exp(s2 - m)
        l = jnp.sum(p1, axis=-1, keepdims=True) + jnp.sum(p2, axis=-1, keepdims=True)
        o_ref[g] = (_dot_nt(p1.astype(BF16), vst) + _dot(p2.astype(BF16), vn)) / l


def _win_decode(qd, kst, vst, knew, vnew, *, td, past):
    bd = qd.shape[0]
    rows = qd.shape[2]
    per_b = lambda b: (b, 0, 0, 0)
    per_b3 = lambda b: (b, 0, 0)
    blk3 = lambda a: pl.BlockSpec((None,) + a.shape[1:], per_b3)
    return pl.pallas_call(
        functools.partial(_win_decode_kernel, td=td, past=past), grid=(bd,),
        in_specs=[pl.BlockSpec((None,) + qd.shape[1:], per_b), blk3(kst), blk3(vst), blk3(knew), blk3(vnew)],
        out_specs=pl.BlockSpec((None, NSA_GROUPS, rows, LANE), per_b),
        out_shape=jax.ShapeDtypeStruct((bd, NSA_GROUPS, rows, LANE), F32),
        compiler_params=_cparams("parallel"), name="win_decode",
    )(qd, kst, vst, knew, vnew)


def _pad_rows(a, rows):
    return jnp.pad(a, ((0, 0), (0, rows - a.shape[1]), (0, 0)))


def _sample(x_sample, caches, page_table, w):
    (c_ckv, c_krope, c_cmp_k, c_cmp_v, c_slc_k, c_slc_v, s_win_k, s_win_v, s_conv) = caches
    bd, td, d = x_sample.shape
    n = bd * td
    n_pages = page_table.shape[1]
    past = n_pages * PAGE_SIZE
    n_pool = c_ckv.shape[0]
    kv_rank = c_ckv.shape[-1]
    x = x_sample.reshape(n, d)
    pos = past + jnp.arange(td, dtype=jnp.int32)
    cosq, sinq = _rope_tables(jnp.tile(pos, bd))
    (qrot, qabs, ckv, krp, qn, kc, vc, ks, vs, kw, vw, gn, ga, gb) = _inproj(
        x, w, cosq, sinq, sample=True, tm=n, tab_blocks=1)
    pt = page_table.reshape(-1)
    krope = krp[:, MLA_D_NOPE:MLA_D_NOPE + MLA_D_ROPE]
    qrope = qrot.reshape(bd, td, MLA_HEADS, HEAD_PAD)[..., MLA_D_NOPE:MLA_D_NOPE + MLA_D_ROPE]
    qpad = LANE - MLA_D_ROPE
    qd_mla = jnp.concatenate([qabs.reshape(bd, td, MLA_HEADS, kv_rank), qrope,
                              jnp.zeros((bd, td, MLA_HEADS, qpad), F32)], axis=-1)
    qd_mla = qd_mla.reshape(bd, td * MLA_HEADS, kv_rank + LANE).astype(BF16)
    knew = jnp.concatenate([ckv, krope, jnp.zeros((n, qpad), F32)], axis=-1).reshape(bd, td, -1)
    knew = _pad_rows(knew, 8).astype(BF16)
    pos_minor = lambda c: jnp.moveaxis(c, 1, -1).reshape(c.shape[0], -1, c.shape[1])
    qg = qn.astype(F32).reshape(bd, td, NSA_GROUPS, NSA_HPG, NSA_DH).transpose(0, 2, 3, 1, 4)
    qg = qg.reshape(bd, NSA_GROUPS, NSA_HPG * td, NSA_DH)
    lane_g = (jnp.arange(LANE) // NSA_DH)[None, :] == jnp.arange(NSA_GROUPS)[:, None]
    qd = jnp.where(lane_g[None, :, None, :], jnp.tile(qg, (1, 1, 1, NSA_GROUPS)), 0.0).astype(BF16)
    n_sel = -(-(past + td) // SLC_BLOCK)
    n_sel_pad = -(-n_sel // LANE) * LANE
    o_lat, o_cmp, score = _mla_cmp_decode(pt, qd_mla, knew, qd, c_ckv, pos_minor(c_krope), pos_minor(c_cmp_k),
                                          pos_minor(c_cmp_v), w, td=td, pp=min(32, n_pages), past=past,
                                          n_sel_pad=n_sel_pad)
    cur = jnp.broadcast_to((pos // SLC_BLOCK).astype(jnp.int32), (bd, NSA_GROUPS, td)).reshape(1, -1)
    n_sel8 = -(-n_sel // 8) * 8
    sel_t = _rank(score[:, :, :td, :n_sel8].reshape(-1, n_sel8).T, cur, n_sel=n_sel)
    sel = jnp.pad(sel_t.T, ((0, 0), (0, n_sel_pad - n_sel8))).reshape(bd, NSA_GROUPS, td, n_sel_pad)
    pp_slc = min(64, n_pages)
    bps = pp_slc * PAGE_SIZE // SLC_BLOCK
    n_past_blk = past // SLC_BLOCK
    sel_past = sel[..., :n_past_blk].reshape(bd, NSA_GROUPS, td, n_past_blk // bps, bps).transpose(0, 3, 1, 2, 4)
    sel16 = jnp.tile(sel_past, (1, 1, 1, NSA_HPG, 1)).astype(BF16)
    selnew = jnp.tile(jnp.broadcast_to(sel[..., n_past_blk:n_past_blk + 1], (bd, NSA_GROUPS, td, 8)), (1, 1, NSA_HPG, 1))
    new8 = lambda a: _pad_rows(a.reshape(bd, td, LANE), 8).astype(BF16)
    o_slc = _slc_decode(pt, qd, sel16, selnew, new8(ks), new8(vs), pos_minor(c_slc_k), pos_minor(c_slc_v),
                        td=td, pp=pp_slc, past=past)
    nbuf = s_win_k.shape[1]
    o_win = _win_decode(qd, pos_minor(s_win_k), pos_minor(s_win_v), new8(kw), new8(vw),
                        td=td, past=past)

    def heads_out(o):
        o = o.reshape(bd, NSA_GROUPS, NSA_HPG, td, NSA_GROUPS, NSA_DH)
        o = jnp.stack([o[:, g, :, :, g] for g in range(NSA_GROUPS)], axis=1)
        return o.transpose(0, 3, 1, 2, 4).reshape(n, NSA_HEADS * NSA_DH)

    dff = w['w_gate'].shape[1]
    prev1 = jnp.zeros((bd, td, dff), F32).at[:, 0].set(s_conv[:, 1])
    prev2 = jnp.zeros((bd, td, dff), F32).at[:, 0].set(s_conv[:, 0]).at[:, 1].set(s_conv[:, 1])
    y, g = _finish(x, o_lat.reshape(n, MLA_HEADS * kv_rank), heads_out(o_cmp), heads_out(o_slc), heads_out(o_win),
                   gn, ga, gb, (prev1.reshape(n, dff), prev2.reshape(n, dff)), w, tm=n, period=td, full_g=True,
                   latent=True)
    kv4 = lambda a: a.reshape(1, bd, td, NSA_GROUPS, NSA_DH)
    win = lambda st, new: jnp.concatenate([st, new.reshape(bd, td, NSA_GROUPS, NSA_DH)], axis=1)[None, :, -nbuf:]
    conv_state = jnp.concatenate([s_conv, g.reshape(bd, td, dff)], axis=1)[None, :, -(CONV_W - 1):]
    states = (ckv.reshape(1, bd, td, kv_rank), krope.reshape(1, bd, td, MLA_D_ROPE), kv4(kc), kv4(vc), kv4(ks), kv4(vs),
              win(s_win_k, kw), win(s_win_v, vw), conv_state)
    return y.reshape(bd, td, d), states


def kernel(x_prompt, x_sample, cache_mla_ckv, cache_mla_krope, cache_nsa_cmp_k, cache_nsa_cmp_v, cache_nsa_slc_k, cache_nsa_slc_v, state_win_k, state_win_v, state_ffn_conv, page_table, norm1_g, w_in, q_norm_g, kv_norm_g, w_uq, w_uk, w_uv, cmp_pos_k, cmp_w1_k, cmp_w2_k, cmp_pos_v, cmp_w1_v, cmp_w2_v, w_proj_mla, w_proj_nsa, w_out, norm2_g, w_gate, w_up, conv_w, conv_b, w_down, norm_f_g):
    assert norm1_g.shape[0] == 1, "single-layer trunk"
    p = dict(norm1_g=norm1_g[0], w_in=w_in[0], q_norm_g=q_norm_g[0], kv_norm_g=kv_norm_g[0], w_uq=w_uq[0],
             w_uk=w_uk[0], w_uv=w_uv[0], cmp_pos_k=cmp_pos_k[0], cmp_w1_k=cmp_w1_k[0], cmp_w2_k=cmp_w2_k[0],
             cmp_pos_v=cmp_pos_v[0], cmp_w1_v=cmp_w1_v[0], cmp_w2_v=cmp_w2_v[0], w_proj_mla=w_proj_mla[0],
             w_proj_nsa=w_proj_nsa[0], w_out=w_out[0], norm2_g=norm2_g[0], w_gate=w_gate[0], w_up=w_up[0],
             conv_w=conv_w[0], conv_b=conv_b[0], w_down=w_down[0], norm_f_g=norm_f_g)
    w = _prep_weights(p)
    y_p, ps = _prompt(x_prompt, w)
    caches = (cache_mla_ckv[0], cache_mla_krope[0], cache_nsa_cmp_k[0], cache_nsa_cmp_v[0], cache_nsa_slc_k[0],
              cache_nsa_slc_v[0], state_win_k[0], state_win_v[0], state_ffn_conv[0])
    y_s, ss = _sample(x_sample, caches, page_table, w)
    out = [y_p, y_s]
    for a, b in zip(ps, ss):
        out += [a, b]
    return tuple(out)
```

```python
import functools

import numpy as np
import jax
import jax.numpy as jnp
from jax import lax
from jax.experimental import pallas as pl
from jax.experimental.pallas import tpu as pltpu

MLA_HEADS = 8
MLA_D_NOPE = 64
MLA_D_ROPE = 32
MLA_D_V = 64
ROPE_THETA = 10000.0
MLA_SCALE = (MLA_D_NOPE + MLA_D_ROPE) ** -0.5
NSA_HEADS = 8
NSA_GROUPS = 2
NSA_HPG = NSA_HEADS // NSA_GROUPS
NSA_DH = 64
NSA_SCALE = NSA_DH ** -0.5
CMP_BLOCK = 32
CMP_STRIDE = 16
SLC_BLOCK = 64
SLC_TOP_N = 16
WINDOW = 512
CONV_W = 3
PAGE_SIZE = 128
EPS = 1e-6
NEG = -1e30
FORCE = 1e9

LOG2E = 1.4426950408889634
LANE = 128
AUX_POS_HI, AUX_POS_LO, AUX_BLK0 = 64, 65, 72
HEAD_PAD = 128
CHUNK_FEATS = CMP_STRIDE * NSA_GROUPS * NSA_DH
FFN_BLOCK = 1536
SLOPES =tuple(float(2.0 ** (-8.0 * (h + 1) / NSA_HEADS)) for h in range(NSA_HEADS))
VMEM_LIMIT = 56 * 1024 * 1024

F32 = jnp.float32
BF16 = jnp.bfloat16
_NT = (((1,), (1,)), ((), ()))


def _cparams(*sem):
    return pltpu.CompilerParams(dimension_semantics=sem, vmem_limit_bytes=VMEM_LIMIT)


def _rms(x, g):
    return x * lax.rsqrt(jnp.mean(x * x, axis=-1, keepdims=True) + EPS) * g


def _dot(a, b):
    return jnp.dot(a, b, preferred_element_type=F32)


def _dot_nt(a, b):
    return lax.dot_general(a, b, _NT, preferred_element_type=F32)


def _dot_exact(a, b):
    return jnp.dot(a, b, preferred_element_type=F32, precision=lax.Precision.HIGHEST)


def _iota(shape, dim):
    return lax.broadcasted_iota(jnp.int32, shape, dim)


def _log2(n):
    assert n > 0 and n & (n - 1) == 0, n
    return n.bit_length() - 1


def _vdiv(x, n):
    return lax.shift_right_logical(x, jnp.full(x.shape, _log2(n), jnp.int32))


def _vmod(x, n):
    assert n & (n - 1) == 0, n
    return x & (n - 1)


_O_CQ, _O_CKV, _O_QN, _O_K6, _O_GA = 0, 384, 640, 1152, 1920


def _inproj_kernel(x_ref, g1_ref, w_ref, qg_ref, kvg_ref, wuq_ref, wk_ref, wv_ref, cos_ref, sin_ref,
                   *outs, sample, q_rank, kv_rank, d_model, tiles_per_seq, q_scale):
    o_gb = _O_GA + d_model
    o_kr = o_gb + d_model
    x = x_ref[...]
    hn = _rms(x, g1_ref[...])
    y = _dot(hn.astype(BF16), w_ref[...])
    cosq = cos_ref[...]
    sinq = sin_ref[...]
    nq = MLA_HEADS * HEAD_PAD
    cqn = _rms(y[:, _O_CQ:_O_CQ + q_rank], qg_ref[...])
    q2 = _dot(cqn.astype(BF16), wuq_ref[...])
    ckv = _rms(y[:, _O_CKV:_O_CKV + kv_rank], kvg_ref[...])
    kr = y[:, o_kr:o_kr + LANE] * cosq + y[:, o_kr + LANE:o_kr + 2 * LANE] * sinq
    ckv_b = ckv.astype(BF16)
    it = iter(outs)
    if sample:
        qrot_ref, qabs_ref = next(it), next(it)
        for h in range(MLA_HEADS):
            sl = slice(h * HEAD_PAD, (h + 1) * HEAD_PAD)
            qh = (q2[:, sl] * cosq + q2[:, nq + h * HEAD_PAD:nq + (h + 1) * HEAD_PAD] * sinq) * q_scale
            qrot_ref[:, sl] = qh
            qabs_ref[:, h * kv_rank:(h + 1) * kv_rank] = _dot(qh.astype(BF16), wk_ref[h])
    else:
        q_ref, k_ref, v_ref = next(it), next(it), next(it)
        knp = _dot(ckv_b, wk_ref[...])
        for h in range(MLA_HEADS):
            sl = slice(h * HEAD_PAD, (h + 1) * HEAD_PAD)
            qh = (q2[:, sl] * cosq + q2[:, nq + h * HEAD_PAD:nq + (h + 1) * HEAD_PAD] * sinq) * q_scale
            q_ref[:, sl] = qh.astype(BF16)
            k_ref[:, sl] = (knp[:, sl] + kr).astype(BF16)
        ones_lane = (_vmod(_iota((1, nq), 1), HEAD_PAD) == MLA_D_V).astype(F32)
        v_ref[...] = (_dot(ckv_b, wv_ref[...]) + ones_lane).astype(BF16)
    ckv_ref, kr_ref, qn_ref = next(it), next(it), next(it)
    ckv_ref[...] = ckv
    kr_ref[...] = kr
    yq = y[:, _O_QN:_O_QN + NSA_HEADS * NSA_DH] * NSA_SCALE
    if sample:
        qn_ref[...] = yq.astype(BF16)
    else:
        lane_q = _iota((x.shape[0], LANE), 1)
        for h in range(NSA_HEADS):
            pair = yq[:, (h // 2) * LANE:(h // 2 + 1) * LANE]
            if h % 2:
                pair = pltpu.roll(pair, NSA_DH, 1)
            aux_q = jnp.where((lane_q == AUX_POS_HI) | (lane_q == AUX_POS_LO), SLOPES[h], 0.0)
            qn_ref[:, h * LANE:(h + 1) * LANE] = jnp.where(lane_q < NSA_DH, pair, aux_q).astype(BF16)
    for j in range(6):
        yj = y[:, _O_K6 + j * LANE:_O_K6 + (j + 1) * LANE]
        if sample:
            next(it)[...] = yj
        else:
            next(it)[...] = yj.T
    if not sample:
        tm = x.shape[0]
        pos = (pl.program_id(0) % tiles_per_seq) * tm + _iota((tm, 1), 0)
        lane = _iota((tm, LANE), 1)
        onehot = ((lane >= AUX_BLK0) & (_vdiv(pos, SLC_BLOCK) == lane - AUX_BLK0)).astype(F32)
        aux = jnp.where(lane == AUX_POS_HI, (pos - _vmod(pos, 256)).astype(F32),
                        jnp.where(lane == AUX_POS_LO, _vmod(pos, 256).astype(F32), onehot))
        for j in range(6):
            yj = y[:, _O_K6 + j * LANE:_O_K6 + (j + 1) * LANE]
            ref = next(it)
            if j in (2, 3, 4, 5):
                tail = aux if j in (2, 4) else (lane == NSA_DH).astype(F32)
                ref[:, :LANE] = jnp.where(lane < NSA_DH, yj, tail).astype(BF16)
                ref[:, LANE:] = jnp.where(lane < NSA_DH, pltpu.roll(yj, NSA_DH, 1), tail).astype(BF16)
            else:
                ref[...] = yj.astype(BF16)
    gn_ref, ga_ref, gb_ref = next(it), next(it), next(it)
    gn_ref[...] = jax.nn.sigmoid(y[:, o_kr + 2 * LANE:o_kr + 3 * LANE])
    ga_ref[...] = jax.nn.sigmoid(y[:, _O_GA:_O_GA + d_model])
    gb_ref[...] = jax.nn.sigmoid(y[:, o_gb:o_gb + d_model])


def _inproj(x, wts, cosq, sinq, *, sample, tm, tab_blocks):
    n, d = x.shape
    q_rank, kv_rank = wts['q_norm_g'].shape[1], wts['kv_norm_g'].shape[1]
    nq = MLA_HEADS * HEAD_PAD
    wk = wts['w_ukT'] if sample else wts['w_ukp']
    row = lambda i: (i, 0)
    const2 = lambda i: (0, 0)
    tab_map = (lambda i: (i % tab_blocks, 0))
    in_specs = [
        pl.BlockSpec((tm, d), row),
        pl.BlockSpec((1, d), const2),
        pl.BlockSpec(wts['w_in'].shape, const2),
        pl.BlockSpec((1, q_rank), const2),
        pl.BlockSpec((1, kv_rank), const2),
        pl.BlockSpec(wts['w_uq2'].shape, const2),
        pl.BlockSpec(wk.shape, (lambda i: (0, 0, 0)) if sample else const2),
        pl.BlockSpec(wts['w_uvf'].shape, const2),
        pl.BlockSpec((tm, LANE), tab_map),
        pl.BlockSpec((tm, LANE), tab_map),
    ]
    shapes = []
    if sample:
        shapes += [(nq, F32), (MLA_HEADS * kv_rank, F32)]
    else:
        shapes += [(nq, BF16), (nq, BF16), (nq, BF16)]
    shapes += [(kv_rank, F32), (LANE, F32), (NSA_HEADS * (NSA_DH if sample else LANE), BF16)]
    n_lead = len(shapes)
    shapes += [(LANE, F32)] * 6
    if not sample:
        shapes += [(LANE, BF16), (LANE, BF16)] + [(2 * LANE, BF16)] * 4
        assert AUX_BLK0 + -(-tab_blocks * tm // SLC_BLOCK) <= LANE, "block one-hot must fit the aux lanes"
    shapes += [(LANE, F32), (d, F32), (d, F32)]
    out_shape = [jax.ShapeDtypeStruct((n, w), dt) for w, dt in shapes]
    out_specs = [pl.BlockSpec((tm, w), row) for w, _ in shapes]
    if not sample:
        tpb = tab_blocks
        for j in range(n_lead, n_lead + 6):
            out_shape[j] = jax.ShapeDtypeStruct((n // (tpb * tm), LANE, tpb * tm), F32)
            out_specs[j] = pl.BlockSpec((None, LANE, tm), lambda i: (i // tpb, 0, i % tpb))
    q_scale = MLA_SCALE if sample else MLA_SCALE * LOG2E
    kern = functools.partial(_inproj_kernel, sample=sample, q_rank=q_rank, kv_rank=kv_rank, d_model=d,
                             tiles_per_seq=tab_blocks, q_scale=q_scale)
    return pl.pallas_call(
        kern, grid=(n // tm,), in_specs=in_specs, out_specs=out_specs, out_shape=out_shape,
        compiler_params=_cparams("parallel"), name="inproj_sample" if sample else "inproj_prompt",
    )(x, wts['norm1_g'], wts['w_in'], wts['q_norm_g'], wts['kv_norm_g'], wts['w_uq2'], wk, wts['w_uvf'],
      cosq, sinq)


def _stack_heads(q_ref, heads, width):
    parts = [q_ref[:, h * width:(h + 1) * width] for h in heads]
    return parts[0] if len(parts) == 1 else jnp.concatenate(parts, axis=0)


def _flash_kernel(*refs, tq, tk, groups, par, dq, dk, dv, window, has_bias, base2):
    if has_bias:
        q_ref, k_ref, v_ref, bias_ref, o_ref = refs
    else:
        q_ref, k_ref, v_ref, o_ref = refs
    q_start = pl.program_id(1) * tq
    n_hi = (q_start + tq - 1) // tk + 1
    hi_full = (q_start + 1) // tk
    if window:
        n_lo = jnp.maximum(q_start - (window - 1), 0) // tk
        lo_full = (jnp.maximum(q_start + tq - window, 0) + tk - 1) // tk
    else:
        n_lo, lo_full = 0, 0
    e1 = jnp.clip(lo_full, n_lo, n_hi)
    e2 = jnp.clip(hi_full, e1, n_hi)
    ex = jnp.exp2 if base2 else jnp.exp
    for c0 in range(0, len(groups), par):
        chunk = groups[c0:c0 + par]
        qs = []
        for heads, kcol in chunk:
            qg = _stack_heads(q_ref, heads, dq)
            if has_bias:
                bias = bias_ref[:, kcol * LANE:(kcol + 1) * LANE]
                qg = qg + jnp.concatenate([bias] * len(heads), axis=0)
            qs.append(qg)
        rows = qs[0].shape[0]
        qpos = q_start + (_iota((rows, 1), 0) & (tq - 1))

        def step(j, carry, masked):
            k0 = pl.multiple_of(j * tk, tk)
            if masked:
                dist = qpos - (k0 + _iota((1, tk), 1))
                mask = dist >= 0
                if window:
                    mask = mask & (dist < window)
            out = []
            for (heads, kcol), qg, (m, acc) in zip(chunk, qs, carry):
                kt = k_ref[pl.ds(k0, tk), kcol * dk:(kcol + 1) * dk]
                vt = v_ref[pl.ds(k0, tk), kcol * LANE:(kcol + 1) * LANE]
                s = _dot_nt(qg, kt)
                if masked:
                    s = jnp.where(mask, s, NEG)
                m_new = jnp.maximum(m, jnp.max(s, axis=-1, keepdims=True))
                p = ex((s - m_new).astype(BF16))
                acc = ex(m - m_new) * acc + _dot(p, vt)
                out.append((m_new, acc))
            return tuple(out)

        carry = tuple((jnp.full((rows, 1), NEG, F32), jnp.zeros((rows, LANE), F32)) for _ in chunk)
        carry = lax.fori_loop(n_lo, e1, functools.partial(step, masked=True), carry)
        carry = lax.fori_loop(e1, e2, functools.partial(step, masked=False), carry)
        carry = lax.fori_loop(e2, n_hi, functools.partial(step, masked=True), carry)
        for (heads, kcol), (m, acc) in zip(chunk, carry):
            o = acc[:, :dv] * (1.0 / acc[:, dv:dv + 1])
            for hh, h in enumerate(heads):
                o_ref[:, h * dv:(h + 1) * dv] = o[hh * tq:(hh + 1) * tq].astype(o_ref.dtype)


def _flash(q, k, v, bias, *, batch, seq, tq, tk, groups, par, dq, dk, dv, window, base2, name):
    n = q.shape[0]
    has_bias = bias is not None
    n_heads = sum(len(g[0]) for g in groups)
    qrow = lambda b, i: (b * (seq // tq) + i, 0)
    kv = lambda b, i: (b, 0)
    in_specs = [pl.BlockSpec((tq, q.shape[1]), qrow), pl.BlockSpec((seq, k.shape[1]), kv),
                pl.BlockSpec((seq, v.shape[1]), kv)]
    args = [q, k, v]
    if has_bias:
        in_specs.append(pl.BlockSpec((tq, bias.shape[1]), qrow))
        args.append(bias)
    kern = functools.partial(_flash_kernel, tq=tq, tk=tk, groups=groups, par=par, dq=dq, dk=dk, dv=dv, window=window,
                             has_bias=has_bias, base2=base2)
    return pl.pallas_call(
        kern, grid=(batch, seq // tq), in_specs=in_specs,
        out_specs=pl.BlockSpec((tq, n_heads * dv), qrow),
        out_shape=jax.ShapeDtypeStruct((n, n_heads * dv), BF16),
        compiler_params=_cparams("parallel", "arbitrary"), name=name,
    )(*args)


def _win_kernel(q_ref, k_ref, v_ref, o_ref, *, tq, groups, dv):
    i = pl.program_id(1)
    rq = _iota((tq, tq), 0)
    ck = _iota((tq, tq), 1)
    bias_own = jnp.where(ck <= rq, 0.0, NEG)
    bias_far = jnp.where(ck > rq, 0.0, NEG) + jnp.where(i >= 2, 0.0, NEG)
    bias_mid = jnp.where(i >= 1, 0.0, NEG)
    starts = (jnp.maximum(i - 2, 0) * tq, jnp.maximum(i - 1, 0) * tq, i * tq)
    for heads, kcol in groups:
        qg = _stack_heads(q_ref, heads, LANE)
        nh = len(heads)
        ss, vs = [], []
        for k0, bias in zip(starts, (bias_far, None, bias_own)):
            k0 = pl.multiple_of(k0, tq)
            s = _dot_nt(qg, k_ref[pl.ds(k0, tq), kcol * LANE:(kcol + 1) * LANE])
            s = s + (bias_mid if bias is None else jnp.concatenate([bias] * nh, axis=0))
            ss.append(s)
            vs.append(v_ref[pl.ds(k0, tq), kcol * LANE:(kcol + 1) * LANE])
        m = functools.reduce(jnp.maximum, [jnp.max(s, axis=-1, keepdims=True) for s in ss])
        acc = functools.reduce(jnp.add, [_dot(jnp.exp((s - m).astype(BF16)), v) for s, v in zip(ss, vs)])
        o = acc[:, :dv] * (1.0 / acc[:, dv:dv + 1])
        for hh, h in enumerate(heads):
            o_ref[:, h * dv:(h + 1) * dv] = o[hh * tq:(hh + 1) * tq].astype(o_ref.dtype)


def _win_prompt(q, k, v, *, batch, seq, tq, groups, dv):
    assert WINDOW == 2 * tq and seq % tq == 0
    n = q.shape[0]
    n_heads = sum(len(g[0]) for g in groups)
    qrow = lambda b, i: (b * (seq // tq) + i, 0)
    kv = lambda b, i: (b, 0)
    return pl.pallas_call(
        functools.partial(_win_kernel, tq=tq, groups=groups, dv=dv), grid=(batch, seq // tq),
        in_specs=[pl.BlockSpec((tq, q.shape[1]), qrow), pl.BlockSpec((seq, k.shape[1]), kv),
                  pl.BlockSpec((seq, v.shape[1]), kv)],
        out_specs=pl.BlockSpec((tq, n_heads * dv), qrow),
        out_shape=jax.ShapeDtypeStruct((n, n_heads * dv), BF16),
        compiler_params=_cparams("parallel", "arbitrary"), name="win_prompt",
    )(q, k, v)


def _compress_rows(xk, xv, w1k_ref, w1v_ref, posk_ref, posv_ref):
    yk = _dot(xk.astype(BF16), w1k_ref[...])
    yv = _dot(xv.astype(BF16), w1v_ref[...])
    return yk, yv


def _compress_finish(y, posy, w2_ref):
    rows = y.shape[0]
    a = y[:, :LANE]
    b = pltpu.roll(y[:, LANE:], rows - 1, 0)
    pos = posy[0:1, :LANE] + posy[1:2, LANE:]
    hid = jax.nn.gelu(a + b + pos)
    return _dot(hid.astype(BF16), w2_ref[...])


def _compress_prompt_kernel(xk_ref, xv_ref, w1k_ref, w1v_ref, pk_ref, pv_ref, w2k_ref, w2v_ref, ok_ref, ov_ref):
    yk, yv = _compress_rows(xk_ref[...], xv_ref[...], w1k_ref, w1v_ref, pk_ref, pv_ref)
    ok_ref[...] = _compress_finish(yk, _dot(pk_ref[...], w1k_ref[...]), w2k_ref).astype(BF16)
    ov_ref[...] = _compress_finish(yv, _dot(pv_ref[...], w1v_ref[...]), w2v_ref).astype(BF16)


def _compress_prompt(kc, vc, wts, *, batch, seq):
    nch = seq // CMP_STRIDE
    xk = kc.reshape(batch * nch, CHUNK_FEATS)
    xv = vc.reshape(batch * nch, CHUNK_FEATS)
    row = lambda b: (b, 0)
    c2 = lambda b: (0, 0)
    wspec = pl.BlockSpec((CHUNK_FEATS, 2 * LANE), c2)
    pspec = pl.BlockSpec((8, CHUNK_FEATS), c2)
    w2spec = pl.BlockSpec((LANE, LANE), c2)
    return pl.pallas_call(
        _compress_prompt_kernel, grid=(batch,),
        in_specs=[pl.BlockSpec((nch, CHUNK_FEATS), row), pl.BlockSpec((nch, CHUNK_FEATS), row),
                  wspec, wspec, pspec, pspec, w2spec, w2spec],
        out_specs=[pl.BlockSpec((nch, LANE), row)] * 2,
        out_shape=[jax.ShapeDtypeStruct((batch * nch, LANE), BF16)] * 2,
        compiler_params=_cparams("parallel"), name="compress_prompt",
    )(xk, xv, wts['cmp_w1k'], wts['cmp_w1v'], wts['cmp_posk'], wts['cmp_posv'], wts['cmp_w2k'], wts['cmp_w2v'])


def _overlap(n_rows, n_sel):
    c = _iota((n_rows, n_sel), 0) * CMP_STRIDE
    j = _iota((n_rows, n_sel), 1) * SLC_BLOCK
    return ((c < j + SLC_BLOCK) & (c + CMP_BLOCK > j)).astype(F32)


def _force_scores(score, cur, jj):
    forced = (jj == 0) | (jj == cur) | (jj == cur - 1)
    score = jnp.where(forced, FORCE, score)
    return jnp.where(jj <= cur, score, NEG)


def _cmp_prompt_kernel(q_ref, k_ref, v_ref, o_ref, bias_ref, *, tq, n_cmp, n_sel):
    q_start = pl.program_id(1) * tq
    ncp = k_ref.shape[0]
    rows = NSA_HPG * tq
    qpos = q_start + (_iota((rows, 1), 0) & (tq - 1))
    cidx = _iota((1, ncp), 1)
    dist = qpos - (cidx * CMP_STRIDE + CMP_BLOCK - 1)
    mask = (dist >= 0) & (cidx < n_cmp)
    distf = dist.astype(F32)
    hrow = _vdiv(_iota((rows, 1), 0), tq)
    nsp = -(-n_sel // 8) * 8
    cur = _vdiv(q_start + _iota((1, tq), 1), SLC_BLOCK)
    jj = _iota((nsp, tq), 0)
    cb = _iota((nsp, ncp), 1) * CMP_STRIDE
    jb = _iota((nsp, ncp), 0) * SLC_BLOCK
    ov_t = ((cb < jb + SLC_BLOCK) & (cb + CMP_BLOCK > jb)).astype(F32)
    for g in range(NSA_GROUPS):
        heads = range(g * NSA_HPG, (g + 1) * NSA_HPG)
        qg = jnp.concatenate([q_ref[:, h * LANE:h * LANE + NSA_DH] for h in heads], axis=0)
        slope = jnp.zeros((rows, 1), F32)
        for hh, h in enumerate(heads):
            slope = jnp.where(hrow == hh, SLOPES[h], slope)
        s = _dot_nt(qg, k_ref[:, g * NSA_DH:(g + 1) * NSA_DH]) - slope * distf
        s = jnp.where(mask, s, NEG)
        m = jnp.max(s, axis=-1, keepdims=True)
        p = jnp.where(mask, jnp.exp(s - m), 0.0)
        l = jnp.sum(p, axis=-1, keepdims=True)
        p = p / jnp.where(l > 0.0, l, 1.0)
        o = _dot(p.astype(BF16), v_ref[:, g * NSA_DH:(g + 1) * NSA_DH])
        imp = p[0:tq]
        for hh in range(1, NSA_HPG):
            imp = imp + p[hh * tq:(hh + 1) * tq]
            o_ref[:, (g * NSA_HPG + hh) * NSA_DH:(g * NSA_HPG + hh + 1) * NSA_DH] = o[hh * tq:(hh + 1) * tq].astype(BF16)
        o_ref[:, g * NSA_HPG * NSA_DH:(g * NSA_HPG + 1) * NSA_DH] = o[0:tq].astype(BF16)
        score = lax.dot_general(ov_t, imp, _NT, preferred_element_type=F32, precision=lax.Precision.HIGHEST)
        score = _force_scores(score, cur, jj)
        rank = jnp.zeros((nsp, tq), F32)
        for i in range(n_sel):
            ri = score[i:i + 1, :]
            beats = (ri > score) | ((ri == score) & (i < jj))
            rank = rank + beats.astype(F32)
        sel = (rank < float(min(SLC_TOP_N, n_sel))) & (jj <= cur)
        bias_t = jnp.where(sel | (jj >= n_sel), 0.0, NEG)
        bias_t = jnp.concatenate([jnp.zeros((AUX_BLK0, tq), F32), bias_t,
                                  jnp.zeros((LANE - AUX_BLK0 - nsp, tq), F32)], axis=0)
        bias_ref[:, g * LANE:(g + 1) * LANE] = bias_t.T.astype(BF16)


def _cmp_prompt(qn, kcc, vcc, *, batch, seq, tq):
    n = qn.shape[0]
    nch = seq // CMP_STRIDE
    n_cmp = nch - CMP_BLOCK // CMP_STRIDE + 1
    n_sel = -(-seq // SLC_BLOCK)
    qrow = lambda b, i: (b * (seq // tq) + i, 0)
    kv = lambda b, i: (b, 0)
    kern = functools.partial(_cmp_prompt_kernel, tq=tq, n_cmp=n_cmp, n_sel=n_sel)
    return pl.pallas_call(
        kern, grid=(batch, seq // tq),
        in_specs=[pl.BlockSpec((tq, qn.shape[1]), qrow), pl.BlockSpec((nch, LANE), kv), pl.BlockSpec((nch, LANE), kv)],
        out_specs=[pl.BlockSpec((tq, NSA_HEADS * NSA_DH), qrow), pl.BlockSpec((tq, NSA_GROUPS * LANE), qrow)],
        out_shape=[jax.ShapeDtypeStruct((n, NSA_HEADS * NSA_DH), BF16),
                   jax.ShapeDtypeStruct((n, NSA_GROUPS * LANE), BF16)],
        compiler_params=_cparams("parallel", "arbitrary"), name="cmp_prompt",
    )(qn, kcc, vcc)


def _finish_kernel(*refs, period, latent, has_state):
    it = iter(refs)
    x_ref, omla_ref, ocmp_ref, oslc_ref, owin_ref, gn_ref, ga_ref, gb_ref = (next(it) for _ in range(8))
    prev1_ref, prev2_ref = (next(it), next(it)) if has_state else (None, None)
    gx_ref = next(it)
    wuv_ref = next(it) if latent else None
    (wpm_ref, wpn_ref, wo_ref, g2_ref, wg_ref, wu_ref, cw_ref, cb_ref, wd_ref, gf_ref,
     y_ref, gout_ref, carry_ref) = it
    tm = x_ref.shape[0]
    gn = gn_ref[...]
    gn_hi = gn.astype(BF16)
    gn_lo = (gn - gn_hi.astype(F32)).astype(BF16)
    gexp = _dot(gn_hi, gx_ref[...]) + _dot(gn_lo, gx_ref[...])
    w = NSA_HEADS * NSA_DH
    o_nsa = (gexp[:, 0:w] * ocmp_ref[...].astype(F32) + gexp[:, w:2 * w] * oslc_ref[...].astype(F32)
             + gexp[:, 2 * w:3 * w] * owin_ref[...].astype(F32))
    o_mla = omla_ref[...].astype(BF16)
    if latent:
        o_mla = _dot(o_mla, wuv_ref[...]).astype(BF16)
    merged = (ga_ref[...] * _dot(o_mla, wpm_ref[...])
              + gb_ref[...] * _dot(o_nsa.astype(BF16), wpn_ref[...]))
    x1 = x_ref[...] + _dot(merged.astype(BF16), wo_ref[...])
    h2 = _rms(x1, g2_ref[...]).astype(BF16)
    row = _iota((tm, 1), 0)
    i = pl.program_id(0)
    t = _vmod(i * tm + row, period)
    carried = period > tm
    if carried:
        @pl.when(i == 0)
        def _():
            carry_ref[...] = jnp.zeros_like(carry_ref)
    dff = wg_ref.shape[1]
    x2 = x1
    for c0 in range(0, dff, FFN_BLOCK):
        cs = slice(c0, min(c0 + FFN_BLOCK, dff))
        g = _dot(h2, wg_ref[:, cs])
        u = _dot(h2, wu_ref[:, cs])
        g1 = pltpu.roll(g, 1, 0)
        g2 = pltpu.roll(g, 2, 0)
        if carried:
            c = carry_ref[:, cs]
            g1 = jnp.where(row == 0, c[7:8], g1)
            g2 = jnp.where(row == 0, c[6:7], jnp.where(row == 1, c[7:8], g2))
            carry_ref[:, cs] = g[tm - 8:tm]
        g1 = jnp.where(t >= 1, g1, prev1_ref[:, cs] if has_state else 0.0)
        g2 = jnp.where(t >= 2, g2, prev2_ref[:, cs] if has_state else 0.0)
        cw = cw_ref[:, cs]
        conv = cb_ref[:, cs] + cw[0:1] * g2 + cw[1:2] * g1 + cw[2:3] * g
        act = (jax.nn.silu(conv) * u).astype(BF16)
        x2 = x2 + _dot(act, wd_ref[cs, :])
        gout_ref[:, cs] = g[tm - 8:tm] if gout_ref.shape[0] == 8 else g
    y_ref[...] = _rms(x2, gf_ref[...])


def _finish(x, omla, ocmp, oslc, owin, gn, ga, gb, state_rows, wts, *, tm, period, full_g, latent):
    n, d = x.shape
    dff = wts['w_gate'].shape[1]
    row = lambda i: (i, 0)
    c2 = lambda i: (0, 0)
    acts = [x, omla, ocmp, oslc, owin, gn, ga, gb] + (list(state_rows) if state_rows is not None else [])
    consts = [wts['gate_expand']] + ([wts['w_uvbd']] if latent else []) + [
        wts['w_proj_mla'], wts['w_proj_nsa'], wts['w_out'], wts['norm2_g'],
        wts['w_gate'], wts['w_up'], wts['conv_w'], wts['conv_b'], wts['w_down'], wts['norm_f_g']]
    ins = acts + consts
    in_specs = [pl.BlockSpec((tm, a.shape[1]), row) for a in acts] + [pl.BlockSpec(a.shape, c2) for a in consts]
    g_rows = n if full_g else (n // tm) * 8
    g_blk = tm if full_g else 8
    kern = functools.partial(_finish_kernel, period=period, latent=latent, has_state=state_rows is not None)
    return pl.pallas_call(
        kern, grid=(n // tm,), in_specs=in_specs,
        out_specs=[pl.BlockSpec((tm, d), row), pl.BlockSpec((g_blk, dff), row)],
        out_shape=[jax.ShapeDtypeStruct((n, d), F32), jax.ShapeDtypeStruct((g_rows, dff), F32)],
        scratch_shapes=[pltpu.VMEM((8, dff), F32)],
        compiler_params=_cparams("arbitrary"), name="finish_full" if full_g else "finish_tiled",
    )(*ins)


def _swap_halves(w):
    hlf = w.shape[-1] // 2
    return jnp.concatenate([-w[..., hlf:], w[..., :hlf]], axis=-1)


def _prep_weights(p):
    d = p['w_in'].shape[0]
    q_rank, kv_rank = p['q_norm_g'].shape[-1], p['kv_norm_g'].shape[-1]
    sizes = [q_rank, kv_rank, MLA_D_ROPE, NSA_HEADS * NSA_DH] + [2 * NSA_GROUPS * NSA_DH] * 3 + [3 * NSA_HEADS, d, d]
    cuts = np.cumsum(sizes)[:-1].tolist()
    cq, ckv, kr, qn, kvc, kvs, kvw, gn, ga, gb = jnp.split(p['w_in'], cuts, axis=-1)
    assert _O_CKV == q_rank and _O_QN == q_rank + kv_rank
    lo, hi = MLA_D_NOPE, HEAD_PAD - MLA_D_NOPE - MLA_D_ROPE
    place = lambda w: jnp.pad(w, ((0, 0), (lo, hi)))
    gnp = jnp.pad(gn, ((0, 0), (0, LANE - gn.shape[1])))
    w_in = jnp.concatenate([cq, ckv, qn, kvc, kvs, kvw, ga, gb, place(kr), place(_swap_halves(kr)), gnp], axis=1)
    w = {'w_in': w_in.astype(BF16)}
    for k in ('norm1_g', 'q_norm_g', 'kv_norm_g', 'norm2_g', 'conv_b'):
        w[k] = p[k].reshape(1, -1)
    w['norm_f_g'] = p['norm_f_g'].reshape(1, -1)
    w['conv_w'] = jnp.pad(p['conv_w'], ((0, 8 - CONV_W), (0, 0)))
    uq = p['w_uq']
    hpad = ((0, 0), (0, 0), (0, HEAD_PAD - uq.shape[-1]))
    uq_a = jnp.pad(uq, hpad)
    uq_b = jnp.pad(jnp.concatenate([jnp.zeros_like(uq[..., :MLA_D_NOPE]), _swap_halves(uq[..., MLA_D_NOPE:])], -1), hpad)
    w['w_uq2'] = jnp.concatenate([uq_a.reshape(q_rank, -1), uq_b.reshape(q_rank, -1)], axis=1).astype(BF16)
    uk = p['w_uk']
    w['w_ukp'] = jnp.pad(uk, ((0, 0), (0, 0), (0, HEAD_PAD - MLA_D_NOPE))).reshape(kv_rank, -1).astype(BF16)
    w['w_ukT'] = jnp.pad(jnp.transpose(uk, (1, 2, 0)), ((0, 0), (0, HEAD_PAD - MLA_D_NOPE), (0, 0))).astype(BF16)
    w['w_uvf'] = jnp.pad(p['w_uv'], ((0, 0), (0, 0), (0, HEAD_PAD - MLA_D_V))).reshape(kv_rank, -1).astype(BF16)
    eye_h = jnp.eye(MLA_HEADS, dtype=F32)
    w['w_uvbd'] = jnp.einsum('rhv,hk->hrkv', p['w_uv'], eye_h).reshape(MLA_HEADS * kv_rank, -1).astype(BF16)
    eye_g = jnp.eye(NSA_GROUPS, dtype=F32)
    for nm in ('k', 'v'):
        w1 = p['cmp_w1_' + nm].reshape(2, CMP_STRIDE, NSA_DH, -1)
        big = jnp.einsum('ajdh,gk->jgdakh', w1, eye_g)
        w['cmp_w1' + nm] = big.reshape(CHUNK_FEATS, -1).astype(BF16)
        pos = p['cmp_pos_' + nm].reshape(2, CMP_STRIDE, 1, NSA_DH)
        pos = jnp.broadcast_to(pos, (2, CMP_STRIDE, NSA_GROUPS, NSA_DH)).reshape(2, CHUNK_FEATS)
        w['cmp_pos' + nm] = jnp.pad(pos, ((0, 6), (0, 0))).astype(BF16)
        w2 = p['cmp_w2_' + nm]
        w['cmp_w2' + nm] = jnp.einsum('hd,gk->ghkd', w2, eye_g).reshape(NSA_GROUPS * w2.shape[0], -1).astype(BF16)
    ge = np.zeros((LANE, 3 * NSA_HEADS * NSA_DH), np.float32)
    for h in range(NSA_HEADS):
        for i in range(3):
            ge[h * 3 + i, i * NSA_HEADS * NSA_DH + h * NSA_DH:i * NSA_HEADS * NSA_DH + (h + 1) * NSA_DH] = 1.0
    w['gate_expand'] = jnp.asarray(ge)
    for k in ('w_proj_mla', 'w_proj_nsa', 'w_out', 'w_gate', 'w_up', 'w_down'):
        w[k] = p[k].astype(BF16)
    return w


def _rope_tables(pos):
    inv = ROPE_THETA ** (-jnp.arange(0, MLA_D_ROPE, 2, dtype=F32) / MLA_D_ROPE)
    ang = pos.astype(F32)[:, None] * inv[None, :]
    cos, sin = jnp.cos(ang), jnp.sin(ang)
    n = pos.shape[0]
    pad = jnp.zeros((n, HEAD_PAD - MLA_D_NOPE - MLA_D_ROPE), F32)
    cosq = jnp.concatenate([jnp.ones((n, MLA_D_NOPE), F32), cos, cos, pad], axis=1)
    sinq = jnp.concatenate([jnp.zeros((n, MLA_D_NOPE), F32), sin, sin, pad], axis=1)
    return cosq, sinq


_NSA_GROUPS_SPEC = tuple((tuple(range(g * NSA_HPG, (g + 1) * NSA_HPG)), g) for g in range(NSA_GROUPS))
_MLA_GROUPS_SPEC = tuple(((h,), h) for h in range(MLA_HEADS))


def _prompt(x_prompt, w):
    b, t, d = x_prompt.shape
    n = b * t
    x = x_prompt.reshape(n, d)
    tm = 256
    cosq, sinq = _rope_tables(jnp.arange(t, dtype=jnp.int32))
    (q_mla, k_mla, v_mla, ckv, krp, qn, kc, vc, ks, vs, kw, vw, kc_b, vc_b, ks_b, vs_b, kw_b, vw_b, gn, ga, gb) = _inproj(
        x, w, cosq, sinq, sample=False, tm=tm, tab_blocks=t // tm)
    o_mla = _flash(q_mla, k_mla, v_mla, None, batch=b, seq=t, tq=512, tk=512, groups=_MLA_GROUPS_SPEC, par=8,
                   dq=HEAD_PAD, dk=HEAD_PAD, dv=MLA_D_V, window=0, base2=True, name="mla_prompt")
    kcc, vcc = _compress_prompt(kc_b, vc_b, w, batch=b, seq=t)
    o_cmp, sel_bias = _cmp_prompt(qn, kcc, vcc, batch=b, seq=t, tq=128)
    o_slc = _flash(qn, ks_b, vs_b, sel_bias, batch=b, seq=t, tq=256, tk=512, groups=_NSA_GROUPS_SPEC, par=2,
                   dq=LANE, dk=LANE, dv=NSA_DH, window=0, base2=False, name="slc_prompt")
    o_win = _win_prompt(qn, kw_b, vw_b, batch=b, seq=t, tq=WINDOW // 2, groups=_NSA_GROUPS_SPEC, dv=NSA_DH)
    dff = w['w_gate'].shape[1]
    y, gtail = _finish(x, o_mla, o_cmp, o_slc, o_win, gn, ga, gb, None, w, tm=tm, period=t, full_g=False,
                       latent=False)
    kv4 = lambda a: a.reshape(1, b, NSA_GROUPS, NSA_DH, a.shape[-1]).transpose(0, 1, 4, 2, 3)
    n_keep = min(WINDOW, t)
    kw, vw = kw[:, :, t - n_keep:], vw[:, :, t - n_keep:]
    conv_state = gtail.reshape(b, t // tm, 8, dff)[:, -1, 8 - (CONV_W - 1):, :]
    states = (ckv.reshape(1, b, t, -1), krp[:, MLA_D_NOPE:MLA_D_NOPE + MLA_D_ROPE].reshape(1, b, t, MLA_D_ROPE),
              kv4(kc), kv4(vc), kv4(ks), kv4(vs), kv4(kw), kv4(vw), conv_state[None])
    return y.reshape(b, t, d), states


def _page_copies(pt_ref, pools, bufs, sems, step, slot, pp):
    copies = []
    for k in range(pp):
        page = pt_ref[step * pp + k]
        for pool, buf, sem in zip(pools, bufs, sems):
            copies.append(pltpu.make_async_copy(pool.at[page], buf.at[slot, k], sem.at[slot]))
    return copies


def _start_all(copies, n_pools):
    for i, c in enumerate(copies):
        c.start(priority=(i // n_pools) % 2)


def _stream_pages(pt_ref, pools, bufs, sems, pp):
    step = pl.program_id(0) * pl.num_programs(1) + pl.program_id(1)
    total = pl.num_programs(0) * pl.num_programs(1)
    slot = step % 2

    @pl.when(step == 0)
    def _():
        _start_all(_page_copies(pt_ref, pools, bufs, sems, step, slot, pp), len(pools))

    @pl.when(step + 1 < total)
    def _():
        _start_all(_page_copies(pt_ref, pools, bufs, sems, step + 1, 1 - slot, pp), len(pools))

    for c in _page_copies(pt_ref, pools, bufs, sems, step, slot, pp):
        c.wait()
    return slot


def _softmax_update(sc, v, m_scr, l_scr, acc_scr, v_transposed=False):
    m_old = m_scr[...]
    m_new = jnp.maximum(m_old, jnp.max(sc, axis=-1, keepdims=True))
    p = jnp.exp(sc - m_new)
    alpha = jnp.exp(m_old - m_new)
    l_scr[...] = alpha * l_scr[...] + jnp.sum(p, axis=-1, keepdims=True)
    pv = _dot_nt(p.astype(BF16), v) if v_transposed else _dot(p.astype(BF16), v)
    acc_scr[...] = alpha * acc_scr[...] + pv
    m_scr[...] = m_new


def _mla_decode_step(slot, q_ref, knew_ref, o_ref, cbuf, rbuf, kscr, krscr, m_scr, l_scr, acc_scr, *, pp, td, kv_rank):
    s = pl.program_id(1)
    q = q_ref[...]
    rows = q.shape[0]

    @pl.when(s == 0)
    def _():
        m_scr[...] = jnp.full_like(m_scr, NEG)
        l_scr[...] = jnp.zeros_like(l_scr)
        acc_scr[...] = jnp.zeros_like(acc_scr)
        kn = knew_ref[...]
        trow = _vdiv(_iota((rows, 1), 0), MLA_HEADS)
        col = _iota((1, kn.shape[0]), 1)
        sc = jnp.where((col <= trow) & (col < td), _dot_nt(q, kn), NEG)
        _softmax_update(sc, kn[:, :kv_rank], m_scr, l_scr, acc_scr)

    for k in range(pp):
        kscr[k * PAGE_SIZE:(k + 1) * PAGE_SIZE, :] = cbuf[slot, k].astype(BF16)
        krscr[:, k * PAGE_SIZE:(k + 1) * PAGE_SIZE] = rbuf[slot, k].astype(BF16)
    kt = kscr[...]
    sc = _dot_nt(q[:, :kv_rank], kt) + _dot(q[:, kv_rank:kv_rank + MLA_D_ROPE], krscr[...])
    _softmax_update(sc, kt, m_scr, l_scr, acc_scr)

    @pl.when(s == pl.num_programs(1) - 1)
    def _():
        o_ref[...] = acc_scr[...] / l_scr[...]


def _alibi_rows(rows, td):
    r = _iota((rows, 1), 0)
    return _vdiv(r, td), _vmod(r, td)


def _slope_rows(hrow, g):
    slope = jnp.zeros(hrow.shape, F32)
    for hh in range(NSA_HPG):
        slope = jnp.where(hrow == hh, SLOPES[g * NSA_HPG + hh], slope)
    return slope


def _cmp_decode_step(slot, q_ref, w1k_ref, w1v_ref, pk_ref, pv_ref, w2k_ref, w2v_ref, o_ref, score_ref,
                     kbuf, vbuf, kp0, kp1, vp0, vp1, yk_scr, yv_scr, *, pp, td, past, n_cmp, n_sel_pad):
    s = pl.program_id(1)
    cpp = PAGE_SIZE // CMP_STRIDE
    hp = pp // 2
    for half, (kp_scr, vp_scr) in enumerate(((kp0, vp0), (kp1, vp1))):
        for k in range(hp):
            kp_scr[k * PAGE_SIZE:(k + 1) * PAGE_SIZE, :] = kbuf[slot, half * hp + k].T
            vp_scr[k * PAGE_SIZE:(k + 1) * PAGE_SIZE, :] = vbuf[slot, half * hp + k].T
    chunk_rows = lambda scr: jnp.concatenate(
        [scr[pl.ds(j, hp * cpp, stride=CMP_STRIDE), :] for j in range(CMP_STRIDE)], axis=1).astype(BF16)
    for half, (kp_scr, vp_scr) in enumerate(((kp0, vp0), (kp1, vp1))):
        r0 = pl.multiple_of(s * (pp * cpp) + half * (hp * cpp), hp * cpp)
        yk_scr[pl.ds(r0, hp * cpp), :] = _dot(chunk_rows(kp_scr), w1k_ref[...])
        yv_scr[pl.ds(r0, hp * cpp), :] = _dot(chunk_rows(vp_scr), w1v_ref[...])

    @pl.when(s == pl.num_programs(1) - 1)
    def _():
        kcc = _compress_finish(yk_scr[...], _dot(pk_ref[...], w1k_ref[...]), w2k_ref).astype(BF16)
        vcc = _compress_finish(yv_scr[...], _dot(pv_ref[...], w1v_ref[...]), w2v_ref).astype(BF16)
        ncp = kcc.shape[0]
        rows = NSA_HPG * td
        hrow, trow = _alibi_rows(rows, td)
        cidx = _iota((1, ncp), 1)
        dist = (past + trow) - (cidx * CMP_STRIDE + CMP_BLOCK - 1)
        mask = (dist >= 0) & (cidx < n_cmp)
        distf = dist.astype(F32)
        tsum = (_vmod(_iota((8, rows), 1), td) == _iota((8, rows), 0)).astype(F32)
        ov = _overlap(ncp, n_sel_pad)
        t8 = _iota((8, 1), 0)
        cur = _vdiv(past + t8, SLC_BLOCK)
        jj = _iota((8, n_sel_pad), 1)
        for g in range(NSA_GROUPS):
            sc = _dot_nt(q_ref[g], kcc) - _slope_rows(hrow, g) * distf
            sc = jnp.where(mask, sc, NEG)
            m = jnp.max(sc, axis=-1, keepdims=True)
            p = jnp.where(mask, jnp.exp(sc - m), 0.0)
            l = jnp.sum(p, axis=-1, keepdims=True)
            p = p / jnp.where(l > 0.0, l, 1.0)
            o_ref[g] = _dot(p.astype(BF16), vcc)
            imp = _dot_exact(tsum, p)
            score_ref[g] = _force_scores(_dot_exact(imp, ov), cur, jj)


def _mla_cmp_decode_kernel(pt_ref, qm_ref, knew_ref, qc_ref, w1k_ref, w1v_ref, pk_ref, pv_ref, w2k_ref, w2v_ref,
                           ckv_hbm, kr_hbm, ck_hbm, cv_hbm, olat_ref, ocmp_ref, score_ref,
                           cbuf, rbuf, kbuf, vbuf, csem, rsem, ksem, vsem,
                           kscr, krscr, m_scr, l_scr, acc_scr, kp0, kp1, vp0, vp1, yk_scr, yv_scr,
                           *, pp, td, kv_rank, past, n_cmp, n_sel_pad):
    slot = _stream_pages(pt_ref, (ckv_hbm, kr_hbm, ck_hbm, cv_hbm), (cbuf, rbuf, kbuf, vbuf),
                         (csem, rsem, ksem, vsem), pp)
    _mla_decode_step(slot, qm_ref, knew_ref, olat_ref, cbuf, rbuf, kscr, krscr, m_scr, l_scr, acc_scr,
                     pp=pp, td=td, kv_rank=kv_rank)
    _cmp_decode_step(slot, qc_ref, w1k_ref, w1v_ref, pk_ref, pv_ref, w2k_ref, w2v_ref, ocmp_ref, score_ref,
                     kbuf, vbuf, kp0, kp1, vp0, vp1, yk_scr, yv_scr, pp=pp, td=td, past=past, n_cmp=n_cmp,
                     n_sel_pad=n_sel_pad)


def _mla_cmp_decode(pt, qd_mla, knew, qd, ckv_pool, kr_pool, k_pool, v_pool, wts, *, td, pp, past, n_sel_pad):
    bd, mrows, qw = qd_mla.shape
    kv_rank = ckv_pool.shape[-1]
    n_pages = pt.shape[0] // bd
    cpp = PAGE_SIZE // CMP_STRIDE
    nch = n_pages * cpp
    n_cmp = (past + td) // CMP_STRIDE - CMP_BLOCK // CMP_STRIDE + 1
    assert (past + td) // CMP_STRIDE == nch, "new rows must not complete a chunk"
    per_b3 = lambda b, s, pt: (b, 0, 0)
    per_b = lambda b, s, pt: (b, 0, 0, 0)
    c2 = lambda b, s, pt: (0, 0)
    rows = qd.shape[2]
    in_specs = ([pl.BlockSpec((None, mrows, qw), per_b3), pl.BlockSpec((None,) + knew.shape[1:], per_b3),
                 pl.BlockSpec((None,) + qd.shape[1:], per_b)]
                + [pl.BlockSpec((CHUNK_FEATS, 2 * LANE), c2) for _ in range(2)]
                + [pl.BlockSpec((8, CHUNK_FEATS), c2) for _ in range(2)]
                + [pl.BlockSpec((LANE, LANE), c2) for _ in range(2)]
                + [pl.BlockSpec(memory_space=pl.ANY) for _ in range(4)])
    kern = functools.partial(_mla_cmp_decode_kernel, pp=pp, td=td, kv_rank=kv_rank, past=past, n_cmp=n_cmp,
                             n_sel_pad=n_sel_pad)
    pools = (ckv_pool, kr_pool, k_pool, v_pool)
    return pl.pallas_call(
        kern,
        grid_spec=pltpu.PrefetchScalarGridSpec(
            num_scalar_prefetch=1, grid=(bd, n_pages // pp), in_specs=in_specs,
            out_specs=[pl.BlockSpec((None, mrows, kv_rank), per_b3),
                       pl.BlockSpec((None, NSA_GROUPS, rows, LANE), per_b),
                       pl.BlockSpec((None, NSA_GROUPS, 8, n_sel_pad), per_b)],
            scratch_shapes=[pltpu.VMEM((2, pp) + p.shape[1:], F32) for p in pools]
            + [pltpu.SemaphoreType.DMA((2,)) for _ in pools]
            + [pltpu.VMEM((pp * PAGE_SIZE, kv_rank), BF16), pltpu.VMEM((MLA_D_ROPE, pp * PAGE_SIZE), BF16),
               pltpu.VMEM((mrows, 1), F32), pltpu.VMEM((mrows, 1), F32), pltpu.VMEM((mrows, kv_rank), F32)]
            + [pltpu.VMEM((pp // 2 * PAGE_SIZE, LANE), F32) for _ in range(4)]
            + [pltpu.VMEM((nch, 2 * LANE), F32) for _ in range(2)]),
        out_shape=[jax.ShapeDtypeStruct((bd, mrows, kv_rank), F32),
                   jax.ShapeDtypeStruct((bd, NSA_GROUPS, rows, LANE), F32),
                   jax.ShapeDtypeStruct((bd, NSA_GROUPS, 8, n_sel_pad), F32)],
        compiler_params=_cparams("arbitrary", "arbitrary"), name="mla_cmp_decode",
    )(pt, qd_mla, knew, qd, wts['cmp_w1k'], wts['cmp_w1v'], wts['cmp_posk'], wts['cmp_posv'],
      wts['cmp_w2k'], wts['cmp_w2v'], *pools)


def _rank_kernel(score_ref, cur_ref, sel_ref, *, n_sel):
    sc = score_ref[...]
    jj = _iota(sc.shape, 0)

    def body(i, rank):
        ri = score_ref[pl.ds(i, 1), :]
        beats = (ri > sc) | ((ri == sc) & (i < jj))
        return rank + beats.astype(F32)

    rank = lax.fori_loop(0, n_sel, body, jnp.zeros(sc.shape, F32))
    sel = (rank < float(min(SLC_TOP_N, n_sel))) & (jj <= cur_ref[...])
    sel_ref[...] = sel.astype(F32)


def _rank(score_t, cur, *, n_sel):
    full = lambda a: pl.BlockSpec(a.shape, lambda: (0,) * a.ndim)
    return pl.pallas_call(
        functools.partial(_rank_kernel, n_sel=n_sel), in_specs=[full(score_t), full(cur)],
        out_specs=full(score_t), out_shape=jax.ShapeDtypeStruct(score_t.shape, F32), name="rank_decode",
    )(score_t, cur)


def _slc_decode_kernel(pt_ref, cnt_ref, q_ref, sel_ref, selnew_ref, knew_ref, vnew_ref, kpos_ref, expand_ref,
                       k_hbm, v_hbm, o_ref, kbuf, vbuf, ksem, vsem, kscr, vscr, m_scr, l_scr, acc_scr,
                       *, pp, td, past):
    b, s, ns = pl.program_id(0), pl.program_id(1), pl.num_programs(1)
    step = b * ns + s
    slot = step % 2
    pools, bufs, sems = (k_hbm, v_hbm), (kbuf, vbuf), (ksem, vsem)
    active = s * pp < cnt_ref[b]
    wrap = s + 1 == ns
    nb = jnp.minimum(jnp.where(wrap, b + 1, b), pl.num_programs(0) - 1)
    next_active = (step + 1 < pl.num_programs(0) * ns) & (jnp.where(wrap, 0, s + 1) * pp < cnt_ref[nb])

    @pl.when((step == 0) & active)
    def _():
        _start_all(_page_copies(pt_ref, pools, bufs, sems, step, slot, pp), len(pools))

    @pl.when(next_active)
    def _():
        _start_all(_page_copies(pt_ref, pools, bufs, sems, step + 1, 1 - slot, pp), len(pools))

    rows = q_ref.shape[1]
    hrow, trow = _alibi_rows(rows, td)

    @pl.when(s == 0)
    def _():
        m_scr[...] = jnp.full_like(m_scr, NEG)
        l_scr[...] = jnp.zeros_like(l_scr)
        acc_scr[...] = jnp.zeros_like(acc_scr)
        kn = knew_ref[...]
        vn = vnew_ref[...]
        col = _iota((1, kn.shape[0]), 1)
        dist = trow - col
        for g in range(NSA_GROUPS):
            sc = _dot_nt(q_ref[g], kn) - _slope_rows(hrow, g) * dist.astype(F32)
            mask = (dist >= 0) & (col < td) & (selnew_ref[g] > 0.5)
            _softmax_update(jnp.where(mask, sc, NEG), vn, m_scr.at[g], l_scr.at[g], acc_scr.at[g])

    @pl.when(active)
    def _():
        for c in _page_copies(pt_ref, pools, bufs, sems, step, slot, pp):
            c.wait()
        for k in range(pp):
            kscr[:, k * PAGE_SIZE:(k + 1) * PAGE_SIZE] = kbuf[slot, k].astype(BF16)
            vscr[:, k * PAGE_SIZE:(k + 1) * PAGE_SIZE] = vbuf[slot, k].astype(BF16)
        kt = kscr[...]
        vt = vscr[...]
        distf = (past + trow).astype(F32) - kpos_ref[...]
        for g in range(NSA_GROUPS):
            sc = _dot(q_ref[g], kt) - _slope_rows(hrow, g) * distf
            selx = _dot(sel_ref[g], expand_ref[...]) > 0.5
            _softmax_update(jnp.where(selx, sc, NEG), vt, m_scr.at[g], l_scr.at[g], acc_scr.at[g], v_transposed=True)

    @pl.when(s == ns - 1)
    def _():
        o_ref[...] = acc_scr[...] / l_scr[...]


def _slc_decode(pt, counts, qd, sel16, selnew, knew, vnew, kpos, k_pool, v_pool, *, td, pp, past):
    bd = qd.shape[0]
    rows = qd.shape[2]
    n_pages = pt.shape[0] // bd
    tk = pp * PAGE_SIZE
    expand = (jnp.arange(tk)[None, :] // SLC_BLOCK == jnp.arange(tk // SLC_BLOCK)[:, None]).astype(BF16)
    per_b = lambda b, s, pt, cnt: (b, 0, 0, 0)
    per_b3 = lambda b, s, pt, cnt: (b, 0, 0)
    in_specs = ([pl.BlockSpec((None,) + qd.shape[1:], per_b),
                 pl.BlockSpec((None, None) + sel16.shape[2:], lambda b, s, pt, cnt: (b, s, 0, 0, 0)),
                 pl.BlockSpec((None,) + selnew.shape[1:], per_b),
                 pl.BlockSpec((None,) + knew.shape[1:], per_b3), pl.BlockSpec((None,) + vnew.shape[1:], per_b3),
                 pl.BlockSpec((None, None, 1, tk), lambda b, s, pt, cnt: (b, s, 0, 0)),
                 pl.BlockSpec(expand.shape, lambda b, s, pt, cnt: (0, 0))]
                + [pl.BlockSpec(memory_space=pl.ANY) for _ in range(2)])
    kern = functools.partial(_slc_decode_kernel, pp=pp, td=td, past=past)
    return pl.pallas_call(
        kern,
        grid_spec=pltpu.PrefetchScalarGridSpec(
            num_scalar_prefetch=2, grid=(bd, n_pages // pp), in_specs=in_specs,
            out_specs=pl.BlockSpec((None, NSA_GROUPS, rows, LANE), per_b),
            scratch_shapes=[pltpu.VMEM((2, pp) + k_pool.shape[1:], F32) for _ in range(2)]
            + [pltpu.SemaphoreType.DMA((2,)) for _ in range(2)]
            + [pltpu.VMEM((LANE, pp * PAGE_SIZE), BF16) for _ in range(2)]
            + [pltpu.VMEM((NSA_GROUPS, rows, 1), F32) for _ in range(2)] + [pltpu.VMEM((NSA_GROUPS, rows, LANE), F32)]),
        out_shape=jax.ShapeDtypeStruct((bd, NSA_GROUPS, rows, LANE), F32),
        compiler_params=_cparams("arbitrary", "arbitrary"), name="slc_decode",
    )(pt, counts, qd, sel16, selnew, knew, vnew, kpos, expand, k_pool, v_pool)


def _win_decode_kernel(q_ref, kst_ref, vst_ref, knew_ref, vnew_ref, o_ref, *, td, past):
    rows = q_ref.shape[1]
    hrow, trow = _alibi_rows(rows, td)
    nbuf = kst_ref.shape[1]
    kst = kst_ref[...].astype(BF16)
    vst = vst_ref[...].astype(BF16)
    kn = knew_ref[...]
    vn = vnew_ref[...]
    kpos = past - nbuf + _iota((1, nbuf), 1)
    d1 = (past + trow) - kpos
    m1 = (d1 >= 0) & (d1 < WINDOW) & (kpos >= 0)
    col = _iota((1, kn.shape[0]), 1)
    d2 = trow - col
    m2 = (d2 >= 0) & (d2 < WINDOW) & (col < td)
    for g in range(NSA_GROUPS):
        slope = _slope_rows(hrow, g)
        s1 = jnp.where(m1, _dot(q_ref[g], kst) - slope * d1.astype(F32), NEG)
        s2 = jnp.where(m2, _dot_nt(q_ref[g], kn) - slope * d2.astype(F32), NEG)
        m = jnp.maximum(jnp.max(s1, axis=-1, keepdims=True), jnp.max(s2, axis=-1, keepdims=True))
        p1 = jnp.exp(s1 - m)
        p2 = jnp.exp(s2 - m)
        l = jnp.sum(p1, axis=-1, keepdims=True) + jnp.sum(p2, axis=-1, keepdims=True)
        o_ref[g] = (_dot_nt(p1.astype(BF16), vst) + _dot(p2.astype(BF16), vn)) / l


def _win_decode(qd, kst, vst, knew, vnew, *, td, past):
    bd = qd.shape[0]
    rows = qd.shape[2]
    per_b = lambda b: (b, 0, 0, 0)
    per_b3 = lambda b: (b, 0, 0)
    blk3 = lambda a: pl.BlockSpec((None,) + a.shape[1:], per_b3)
    return pl.pallas_call(
        functools.partial(_win_decode_kernel, td=td, past=past), grid=(bd,),
        in_specs=[pl.BlockSpec((None,) + qd.shape[1:], per_b), blk3(kst), blk3(vst), blk3(knew), blk3(vnew)],
        out_specs=pl.BlockSpec((None, NSA_GROUPS, rows, LANE), per_b),
        out_shape=jax.ShapeDtypeStruct((bd, NSA_GROUPS, rows, LANE), F32),
        compiler_params=_cparams("parallel"), name="win_decode",
    )(qd, kst, vst, knew, vnew)


def _pad_rows(a, rows):
    return jnp.pad(a, ((0, 0), (0, rows - a.shape[1]), (0, 0)))


def _sample(x_sample, caches, page_table, w):
    (c_ckv, c_krope, c_cmp_k, c_cmp_v, c_slc_k, c_slc_v, s_win_k, s_win_v, s_conv) = caches
    bd, td, d = x_sample.shape
    n = bd * td
    n_pages = page_table.shape[1]
    past = n_pages * PAGE_SIZE
    n_pool = c_ckv.shape[0]
    kv_rank = c_ckv.shape[-1]
    x = x_sample.reshape(n, d)
    pos = past + jnp.arange(td, dtype=jnp.int32)
    cosq, sinq = _rope_tables(jnp.tile(pos, bd))
    (qrot, qabs, ckv, krp, qn, kc, vc, ks, vs, kw, vw, gn, ga, gb) = _inproj(
        x, w, cosq, sinq, sample=True, tm=n, tab_blocks=1)
    pt = page_table.reshape(-1)
    krope = krp[:, MLA_D_NOPE:MLA_D_NOPE + MLA_D_ROPE]
    qrope = qrot.reshape(bd, td, MLA_HEADS, HEAD_PAD)[..., MLA_D_NOPE:MLA_D_NOPE + MLA_D_ROPE]
    qpad = LANE - MLA_D_ROPE
    qd_mla = jnp.concatenate([qabs.reshape(bd, td, MLA_HEADS, kv_rank), qrope,
                              jnp.zeros((bd, td, MLA_HEADS, qpad), F32)], axis=-1)
    qd_mla = qd_mla.reshape(bd, td * MLA_HEADS, kv_rank + LANE).astype(BF16)
    knew = jnp.concatenate([ckv, krope, jnp.zeros((n, qpad), F32)], axis=-1).reshape(bd, td, -1)
    knew = _pad_rows(knew, 8).astype(BF16)
    pos_minor = lambda c: jnp.moveaxis(c, 1, -1).reshape(c.shape[0], -1, c.shape[1])
    qg = qn.astype(F32).reshape(bd, td, NSA_GROUPS, NSA_HPG, NSA_DH).transpose(0, 2, 3, 1, 4)
    qg = qg.reshape(bd, NSA_GROUPS, NSA_HPG * td, NSA_DH)
    lane_g = (jnp.arange(LANE) // NSA_DH)[None, :] == jnp.arange(NSA_GROUPS)[:, None]
    qd = jnp.where(lane_g[None, :, None, :], jnp.tile(qg, (1, 1, 1, NSA_GROUPS)), 0.0).astype(BF16)
    n_sel = -(-(past + td) // SLC_BLOCK)
    n_sel_pad = -(-n_sel // LANE) * LANE
    o_lat, o_cmp, score = _mla_cmp_decode(pt, qd_mla, knew, qd, c_ckv, pos_minor(c_krope), pos_minor(c_cmp_k),
                                          pos_minor(c_cmp_v), w, td=td, pp=min(32, n_pages), past=past,
                                          n_sel_pad=n_sel_pad)
    cur = jnp.broadcast_to((pos // SLC_BLOCK).astype(jnp.int32), (bd, NSA_GROUPS, td)).reshape(1, -1)
    n_sel8 = -(-n_sel // 8) * 8
    sel_t = _rank(score[:, :, :td, :n_sel8].reshape(-1, n_sel8).T, cur, n_sel=n_sel)
    sel = jnp.pad(sel_t.T, ((0, 0), (0, n_sel_pad - n_sel8))).reshape(bd, NSA_GROUPS, td, n_sel_pad)
    pp_slc = min(32, n_pages)
    bpp = PAGE_SIZE // SLC_BLOCK
    n_past_blk = past // SLC_BLOCK
    sel_past = sel[..., :n_past_blk]
    need_page = (sel_past > 0.5).any(axis=(1, 2)).reshape(bd, n_pages, bpp).any(axis=-1)
    counts = need_page.sum(axis=-1).astype(jnp.int32)
    order = jnp.argsort(jnp.logical_not(need_page), axis=-1, stable=True).astype(jnp.int32)
    pt_slc = jnp.take_along_axis(page_table, order, axis=1).reshape(-1)
    blk = (order[:, :, None] * bpp + jnp.arange(bpp, dtype=jnp.int32)).reshape(bd, 1, 1, n_pages * bpp)
    sel_listed = jnp.take_along_axis(sel_past, jnp.broadcast_to(blk, sel_past.shape), axis=-1)
    bps = pp_slc * bpp
    sel_steps = sel_listed.reshape(bd, NSA_GROUPS, td, n_pages // pp_slc, bps).transpose(0, 3, 1, 2, 4)
    sel16 = jnp.tile(sel_steps, (1, 1, 1, NSA_HPG, 1)).astype(BF16)
    kpos = (order[:, :, None] * PAGE_SIZE + jnp.arange(PAGE_SIZE, dtype=jnp.int32)).astype(F32)
    kpos = kpos.reshape(bd, n_pages // pp_slc, 1, pp_slc * PAGE_SIZE)
    selnew = jnp.tile(jnp.broadcast_to(sel[..., n_past_blk:n_past_blk + 1], (bd, NSA_GROUPS, td, 8)), (1, 1, NSA_HPG, 1))
    new8 = lambda a: _pad_rows(a.reshape(bd, td, LANE), 8).astype(BF16)
    o_slc = _slc_decode(pt_slc, counts, qd, sel16, selnew, new8(ks), new8(vs), kpos, pos_minor(c_slc_k),
                        pos_minor(c_slc_v), td=td, pp=pp_slc, past=past)
    nbuf = s_win_k.shape[1]
    o_win = _win_decode(qd, pos_minor(s_win_k), pos_minor(s_win_v), new8(kw), new8(vw),
                        td=td, past=past)

    def heads_out(o):
        o = o.reshape(bd, NSA_GROUPS, NSA_HPG, td, NSA_GROUPS, NSA_DH)
        o = jnp.stack([o[:, g, :, :, g] for g in range(NSA_GROUPS)], axis=1)
        return o.transpose(0, 3, 1, 2, 4).reshape(n, NSA_HEADS * NSA_DH)

    dff = w['w_gate'].shape[1]
    prev1 = jnp.zeros((bd, td, dff), F32).at[:, 0].set(s_conv[:, 1])
    prev2 = jnp.zeros((bd, td, dff), F32).at[:, 0].set(s_conv[:, 0]).at[:, 1].set(s_conv[:, 1])
    y, g = _finish(x, o_lat.reshape(n, MLA_HEADS * kv_rank), heads_out(o_cmp), heads_out(o_slc), heads_out(o_win),
                   gn, ga, gb, (prev1.reshape(n, dff), prev2.reshape(n, dff)), w, tm=n, period=td, full_g=True,
                   latent=True)
    kv4 = lambda a: a.reshape(1, bd, td, NSA_GROUPS, NSA_DH)
    win = lambda st, new: jnp.concatenate([st, new.reshape(bd, td, NSA_GROUPS, NSA_DH)], axis=1)[None, :, -nbuf:]
    conv_state = jnp.concatenate([s_conv, g.reshape(bd, td, dff)], axis=1)[None, :, -(CONV_W - 1):]
    states = (ckv.reshape(1, bd, td, kv_rank), krope.reshape(1, bd, td, MLA_D_ROPE), kv4(kc), kv4(vc), kv4(ks), kv4(vs),
              win(s_win_k, kw), win(s_win_v, vw), conv_state)
    return y.reshape(bd, td, d), states


def kernel(x_prompt, x_sample, cache_mla_ckv, cache_mla_krope, cache_nsa_cmp_k, cache_nsa_cmp_v, cache_nsa_slc_k, cache_nsa_slc_v, state_win_k, state_win_v, state_ffn_conv, page_table, norm1_g, w_in, q_norm_g, kv_norm_g, w_uq, w_uk, w_uv, cmp_pos_k, cmp_w1_k, cmp_w2_k, cmp_pos_v, cmp_w1_v, cmp_w2_v, w_proj_mla, w_proj_nsa, w_out, norm2_g, w_gate, w_up, conv_w, conv_b, w_down, norm_f_g):
    assert norm1_g.shape[0] == 1, "single-layer trunk"
    p = dict(norm1_g=norm1_g[0], w_in=w_in[0], q_norm_g=q_norm_g[0], kv_norm_g=kv_norm_g[0], w_uq=w_uq[0],
             w_uk=w_uk[0], w_uv=w_uv[0], cmp_pos_k=cmp_pos_k[0], cmp_w1_k=cmp_w1_k[0], cmp_w2_k=cmp_w2_k[0],
             cmp_pos_v=cmp_pos_v[0], cmp_w1_v=cmp_w1_v[0], cmp_w2_v=cmp_w2_v[0], w_proj_mla=w_proj_mla[0],
             w_proj_nsa=w_proj_nsa[0], w_out=w_out[0], norm2_g=norm2_g[0], w_gate=w_gate[0], w_up=w_up[0],
             conv_w=conv_w[0], conv_b=conv_b[0], w_down=w_down[0], norm_f_g=norm_f_g)
    w = _prep_weights(p)
    y_p, ps = _prompt(x_prompt, w)
    caches = (cache_mla_ckv[0], cache_mla_krope[0], cache_nsa_cmp_k[0], cache_nsa_cmp_v[0], cache_nsa_slc_k[0],
              cache_nsa_slc_v[0], state_win_k[0], state_win_v[0], state_ffn_conv[0])
    y_s, ss = _sample(x_sample, caches, page_table, w)
    out = [y_p, y_s]
    for a, b in zip(ps, ss):
        out += [a, b]
    return tuple(out)
```

```python
import functools

import numpy as np
import jax
import jax.numpy as jnp
from jax import lax
from jax.experimental import pallas as pl
from jax.experimental.pallas import tpu as pltpu

MLA_HEADS = 8
MLA_D_NOPE = 64
MLA_D_ROPE = 32
MLA_D_V = 64
ROPE_THETA = 10000.0
MLA_SCALE = (MLA_D_NOPE + MLA_D_ROPE) ** -0.5
NSA_HEADS = 8
NSA_GROUPS = 2
NSA_HPG = NSA_HEADS // NSA_GROUPS
NSA_DH = 64
NSA_SCALE = NSA_DH ** -0.5
CMP_BLOCK = 32
CMP_STRIDE = 16
SLC_BLOCK = 64
SLC_TOP_N = 16
WINDOW = 512
CONV_W = 3
PAGE_SIZE = 128
EPS = 1e-6
NEG = -1e30
FORCE = 1e9

LOG2E = 1.4426950408889634
LANE = 128
AUX_POS_HI, AUX_POS_LO, AUX_BLK0 = 64, 65, 72
HEAD_PAD = 128
CHUNK_FEATS = CMP_STRIDE * NSA_GROUPS * NSA_DH
FFN_BLOCK = 1536
SLOPES =tuple(float(2.0 ** (-8.0 * (h + 1) / NSA_HEADS)) for h in range(NSA_HEADS))
VMEM_LIMIT = 56 * 1024 * 1024

F32 = jnp.float32
BF16 = jnp.bfloat16
_NT = (((1,), (1,)), ((), ()))


def _cparams(*sem):
    return pltpu.CompilerParams(dimension_semantics=sem, vmem_limit_bytes=VMEM_LIMIT)


def _rms(x, g):
    return x * lax.rsqrt(jnp.mean(x * x, axis=-1, keepdims=True) + EPS) * g


def _dot(a, b):
    return jnp.dot(a, b, preferred_element_type=F32)


def _dot_nt(a, b):
    return lax.dot_general(a, b, _NT, preferred_element_type=F32)


def _dot_exact(a, b):
    return jnp.dot(a, b, preferred_element_type=F32, precision=lax.Precision.HIGHEST)


def _iota(shape, dim):
    return lax.broadcasted_iota(jnp.int32, shape, dim)


def _log2(n):
    assert n > 0 and n & (n - 1) == 0, n
    return n.bit_length() - 1


def _vdiv(x, n):
    return lax.shift_right_logical(x, jnp.full(x.shape, _log2(n), jnp.int32))


def _vmod(x, n):
    assert n & (n - 1) == 0, n
    return x & (n - 1)


_O_CQ, _O_CKV, _O_QN, _O_K6, _O_GA = 0, 384, 640, 1152, 1920


def _inproj_kernel(x_ref, g1_ref, w_ref, qg_ref, kvg_ref, wuq_ref, wk_ref, wv_ref, cos_ref, sin_ref,
                   *outs, sample, q_rank, kv_rank, d_model, tiles_per_seq, q_scale):
    o_gb = _O_GA + d_model
    o_kr = o_gb + d_model
    x = x_ref[...]
    hn = _rms(x, g1_ref[...])
    y = _dot(hn.astype(BF16), w_ref[...])
    cosq = cos_ref[...]
    sinq = sin_ref[...]
    nq = MLA_HEADS * HEAD_PAD
    cqn = _rms(y[:, _O_CQ:_O_CQ + q_rank], qg_ref[...])
    q2 = _dot(cqn.astype(BF16), wuq_ref[...])
    ckv = _rms(y[:, _O_CKV:_O_CKV + kv_rank], kvg_ref[...])
    kr = y[:, o_kr:o_kr + LANE] * cosq + y[:, o_kr + LANE:o_kr + 2 * LANE] * sinq
    ckv_b = ckv.astype(BF16)
    it = iter(outs)
    if sample:
        qrot_ref, qabs_ref = next(it), next(it)
        for h in range(MLA_HEADS):
            sl = slice(h * HEAD_PAD, (h + 1) * HEAD_PAD)
            qh = (q2[:, sl] * cosq + q2[:, nq + h * HEAD_PAD:nq + (h + 1) * HEAD_PAD] * sinq) * q_scale
            qrot_ref[:, sl] = qh
            qabs_ref[:, h * kv_rank:(h + 1) * kv_rank] = _dot(qh.astype(BF16), wk_ref[h])
    else:
        q_ref, k_ref, v_ref = next(it), next(it), next(it)
        knp = _dot(ckv_b, wk_ref[...])
        for h in range(MLA_HEADS):
            sl = slice(h * HEAD_PAD, (h + 1) * HEAD_PAD)
            qh = (q2[:, sl] * cosq + q2[:, nq + h * HEAD_PAD:nq + (h + 1) * HEAD_PAD] * sinq) * q_scale
            q_ref[:, sl] = qh.astype(BF16)
            k_ref[:, sl] = (knp[:, sl] + kr).astype(BF16)
        ones_lane = (_vmod(_iota((1, nq), 1), HEAD_PAD) == MLA_D_V).astype(F32)
        v_ref[...] = (_dot(ckv_b, wv_ref[...]) + ones_lane).astype(BF16)
    ckv_ref, kr_ref, qn_ref = next(it), next(it), next(it)
    ckv_ref[...] = ckv
    kr_ref[...] = kr
    yq = y[:, _O_QN:_O_QN + NSA_HEADS * NSA_DH] * NSA_SCALE
    if sample:
        qn_ref[...] = yq.astype(BF16)
    else:
        lane_q = _iota((x.shape[0], LANE), 1)
        for h in range(NSA_HEADS):
            pair = yq[:, (h // 2) * LANE:(h // 2 + 1) * LANE]
            if h % 2:
                pair = pltpu.roll(pair, NSA_DH, 1)
            aux_q = jnp.where((lane_q == AUX_POS_HI) | (lane_q == AUX_POS_LO), SLOPES[h], 0.0)
            qn_ref[:, h * LANE:(h + 1) * LANE] = jnp.where(lane_q < NSA_DH, pair, aux_q).astype(BF16)
    for j in range(6):
        yj = y[:, _O_K6 + j * LANE:_O_K6 + (j + 1) * LANE]
        if sample:
            next(it)[...] = yj
        else:
            next(it)[...] = yj.T
    if not sample:
        tm = x.shape[0]
        pos = (pl.program_id(0) % tiles_per_seq) * tm + _iota((tm, 1), 0)
        lane = _iota((tm, LANE), 1)
        onehot = ((lane >= AUX_BLK0) & (_vdiv(pos, SLC_BLOCK) == lane - AUX_BLK0)).astype(F32)
        aux = jnp.where(lane == AUX_POS_HI, (pos - _vmod(pos, 256)).astype(F32),
                        jnp.where(lane == AUX_POS_LO, _vmod(pos, 256).astype(F32), onehot))
        for j in range(6):
            yj = y[:, _O_K6 + j * LANE:_O_K6 + (j + 1) * LANE]
            ref = next(it)
            if j in (2, 3, 4, 5):
                tail = aux if j in (2, 4) else (lane == NSA_DH).astype(F32)
                ref[:, :LANE] = jnp.where(lane < NSA_DH, yj, tail).astype(BF16)
                ref[:, LANE:] = jnp.where(lane < NSA_DH, pltpu.roll(yj, NSA_DH, 1), tail).astype(BF16)
            else:
                ref[...] = yj.astype(BF16)
    gn_ref, ga_ref, gb_ref = next(it), next(it), next(it)
    gn_ref[...] = jax.nn.sigmoid(y[:, o_kr + 2 * LANE:o_kr + 3 * LANE])
    ga_ref[...] = jax.nn.sigmoid(y[:, _O_GA:_O_GA + d_model])
    gb_ref[...] = jax.nn.sigmoid(y[:, o_gb:o_gb + d_model])


def _inproj(x, wts, cosq, sinq, *, sample, tm, tab_blocks):
    n, d = x.shape
    q_rank, kv_rank = wts['q_norm_g'].shape[1], wts['kv_norm_g'].shape[1]
    nq = MLA_HEADS * HEAD_PAD
    wk = wts['w_ukT'] if sample else wts['w_ukp']
    row = lambda i: (i, 0)
    const2 = lambda i: (0, 0)
    tab_map = (lambda i: (i % tab_blocks, 0))
    in_specs = [
        pl.BlockSpec((tm, d), row),
        pl.BlockSpec((1, d), const2),
        pl.BlockSpec(wts['w_in'].shape, const2),
        pl.BlockSpec((1, q_rank), const2),
        pl.BlockSpec((1, kv_rank), const2),
        pl.BlockSpec(wts['w_uq2'].shape, const2),
        pl.BlockSpec(wk.shape, (lambda i: (0, 0, 0)) if sample else const2),
        pl.BlockSpec(wts['w_uvf'].shape, const2),
        pl.BlockSpec((tm, LANE), tab_map),
        pl.BlockSpec((tm, LANE), tab_map),
    ]
    shapes = []
    if sample:
        shapes += [(nq, F32), (MLA_HEADS * kv_rank, F32)]
    else:
        shapes += [(nq, BF16), (nq, BF16), (nq, BF16)]
    shapes += [(kv_rank, F32), (LANE, F32), (NSA_HEADS * (NSA_DH if sample else LANE), BF16)]
    n_lead = len(shapes)
    shapes += [(LANE, F32)] * 6
    if not sample:
        shapes += [(LANE, BF16), (LANE, BF16)] + [(2 * LANE, BF16)] * 4
        assert AUX_BLK0 + -(-tab_blocks * tm // SLC_BLOCK) <= LANE, "block one-hot must fit the aux lanes"
    shapes += [(LANE, F32), (d, F32), (d, F32)]
    out_shape = [jax.ShapeDtypeStruct((n, w), dt) for w, dt in shapes]
    out_specs = [pl.BlockSpec((tm, w), row) for w, _ in shapes]
    if not sample:
        tpb = tab_blocks
        for j in range(n_lead, n_lead + 6):
            out_shape[j] = jax.ShapeDtypeStruct((n // (tpb * tm), LANE, tpb * tm), F32)
            out_specs[j] = pl.BlockSpec((None, LANE, tm), lambda i: (i // tpb, 0, i % tpb))
    q_scale = MLA_SCALE if sample else MLA_SCALE * LOG2E
    kern = functools.partial(_inproj_kernel, sample=sample, q_rank=q_rank, kv_rank=kv_rank, d_model=d,
                             tiles_per_seq=tab_blocks, q_scale=q_scale)
    return pl.pallas_call(
        kern, grid=(n // tm,), in_specs=in_specs, out_specs=out_specs, out_shape=out_shape,
        compiler_params=_cparams("parallel"), name="inproj_sample" if sample else "inproj_prompt",
    )(x, wts['norm1_g'], wts['w_in'], wts['q_norm_g'], wts['kv_norm_g'], wts['w_uq2'], wk, wts['w_uvf'],
      cosq, sinq)


def _stack_heads(q_ref, heads, width):
    parts = [q_ref[:, h * width:(h + 1) * width] for h in heads]
    return parts[0] if len(parts) == 1 else jnp.concatenate(parts, axis=0)


def _flash_kernel(*refs, tq, tk, groups, par, dq, dk, dv, window, has_bias, base2):
    if has_bias:
        q_ref, k_ref, v_ref, bias_ref, o_ref = refs
    else:
        q_ref, k_ref, v_ref, o_ref = refs
    q_start = pl.program_id(1) * tq
    n_hi = (q_start + tq - 1) // tk + 1
    hi_full = (q_start + 1) // tk
    if window:
        n_lo = jnp.maximum(q_start - (window - 1), 0) // tk
        lo_full = (jnp.maximum(q_start + tq - window, 0) + tk - 1) // tk
    else:
        n_lo, lo_full = 0, 0
    e1 = jnp.clip(lo_full, n_lo, n_hi)
    e2 = jnp.clip(hi_full, e1, n_hi)
    ex = jnp.exp2 if base2 else jnp.exp
    for c0 in range(0, len(groups), par):
        chunk = groups[c0:c0 + par]
        qs = []
        for heads, kcol in chunk:
            qg = _stack_heads(q_ref, heads, dq)
            if has_bias:
                bias = bias_ref[:, kcol * LANE:(kcol + 1) * LANE]
                qg = qg + jnp.concatenate([bias] * len(heads), axis=0)
            qs.append(qg)
        rows = qs[0].shape[0]
        qpos = q_start + (_iota((rows, 1), 0) & (tq - 1))

        def step(j, carry, masked):
            k0 = pl.multiple_of(j * tk, tk)
            if masked:
                dist = qpos - (k0 + _iota((1, tk), 1))
                mask = dist >= 0
                if window:
                    mask = mask & (dist < window)
            out = []
            for (heads, kcol), qg, (m, acc) in zip(chunk, qs, carry):
                kt = k_ref[pl.ds(k0, tk), kcol * dk:(kcol + 1) * dk]
                vt = v_ref[pl.ds(k0, tk), kcol * LANE:(kcol + 1) * LANE]
                s = _dot_nt(qg, kt)
                if masked:
                    s = jnp.where(mask, s, NEG)
                m_new = jnp.maximum(m, jnp.max(s, axis=-1, keepdims=True))
                p = ex((s - m_new).astype(BF16))
                acc = ex(m - m_new) * acc + _dot(p, vt)
                out.append((m_new, acc))
            return tuple(out)

        carry = tuple((jnp.full((rows, 1), NEG, F32), jnp.zeros((rows, LANE), F32)) for _ in chunk)
        carry = lax.fori_loop(n_lo, e1, functools.partial(step, masked=True), carry)
        carry = lax.fori_loop(e1, e2, functools.partial(step, masked=False), carry)
        carry = lax.fori_loop(e2, n_hi, functools.partial(step, masked=True), carry)
        for (heads, kcol), (m, acc) in zip(chunk, carry):
            o = acc[:, :dv] * (1.0 / acc[:, dv:dv + 1])
            for hh, h in enumerate(heads):
                o_ref[:, h * dv:(h + 1) * dv] = o[hh * tq:(hh + 1) * tq].astype(o_ref.dtype)


def _flash(q, k, v, bias, *, batch, seq, tq, tk, groups, par, dq, dk, dv, window, base2, name):
    n = q.shape[0]
    has_bias = bias is not None
    n_heads = sum(len(g[0]) for g in groups)
    qrow = lambda b, i: (b * (seq // tq) + i, 0)
    kv = lambda b, i: (b, 0)
    in_specs = [pl.BlockSpec((tq, q.shape[1]), qrow), pl.BlockSpec((seq, k.shape[1]), kv),
                pl.BlockSpec((seq, v.shape[1]), kv)]
    args = [q, k, v]
    if has_bias:
        in_specs.append(pl.BlockSpec((tq, bias.shape[1]), qrow))
        args.append(bias)
    kern = functools.partial(_flash_kernel, tq=tq, tk=tk, groups=groups, par=par, dq=dq, dk=dk, dv=dv, window=window,
                             has_bias=has_bias, base2=base2)
    return pl.pallas_call(
        kern, grid=(batch, seq // tq), in_specs=in_specs,
        out_specs=pl.BlockSpec((tq, n_heads * dv), qrow),
        out_shape=jax.ShapeDtypeStruct((n, n_heads * dv), BF16),
        compiler_params=_cparams("parallel", "arbitrary"), name=name,
    )(*args)


def _win_kernel(q_ref, k_ref, v_ref, o_ref, *, tq, groups, dv):
    i = pl.program_id(1)
    rq = _iota((tq, tq), 0)
    ck = _iota((tq, tq), 1)
    bias_own = jnp.where(ck <= rq, 0.0, NEG)
    bias_far = jnp.where(ck > rq, 0.0, NEG) + jnp.where(i >= 2, 0.0, NEG)
    bias_mid = jnp.where(i >= 1, 0.0, NEG)
    starts = (jnp.maximum(i - 2, 0) * tq, jnp.maximum(i - 1, 0) * tq, i * tq)
    for heads, kcol in groups:
        qg = _stack_heads(q_ref, heads, LANE)
        nh = len(heads)
        ss, vs = [], []
        for k0, bias in zip(starts, (bias_far, None, bias_own)):
            k0 = pl.multiple_of(k0, tq)
            s = _dot_nt(qg, k_ref[pl.ds(k0, tq), kcol * LANE:(kcol + 1) * LANE])
            s = s + (bias_mid if bias is None else jnp.concatenate([bias] * nh, axis=0))
            ss.append(s)
            vs.append(v_ref[pl.ds(k0, tq), kcol * LANE:(kcol + 1) * LANE])
        m = functools.reduce(jnp.maximum, [jnp.max(s, axis=-1, keepdims=True) for s in ss])
        acc = functools.reduce(jnp.add, [_dot(jnp.exp((s - m).astype(BF16)), v) for s, v in zip(ss, vs)])
        o = acc[:, :dv] * (1.0 / acc[:, dv:dv + 1])
        for hh, h in enumerate(heads):
            o_ref[:, h * dv:(h + 1) * dv] = o[hh * tq:(hh + 1) * tq].astype(o_ref.dtype)


def _win_prompt(q, k, v, *, batch, seq, tq, groups, dv):
    assert WINDOW == 2 * tq and seq % tq == 0
    n = q.shape[0]
    n_heads = sum(len(g[0]) for g in groups)
    qrow = lambda b, i: (b * (seq // tq) + i, 0)
    kv = lambda b, i: (b, 0)
    return pl.pallas_call(
        functools.partial(_win_kernel, tq=tq, groups=groups, dv=dv), grid=(batch, seq // tq),
        in_specs=[pl.BlockSpec((tq, q.shape[1]), qrow), pl.BlockSpec((seq, k.shape[1]), kv),
                  pl.BlockSpec((seq, v.shape[1]), kv)],
        out_specs=pl.BlockSpec((tq, n_heads * dv), qrow),
        out_shape=jax.ShapeDtypeStruct((n, n_heads * dv), BF16),
        compiler_params=_cparams("parallel", "arbitrary"), name="win_prompt",
    )(q, k, v)


def _compress_rows(xk, xv, w1k_ref, w1v_ref, posk_ref, posv_ref):
    yk = _dot(xk.astype(BF16), w1k_ref[...])
    yv = _dot(xv.astype(BF16), w1v_ref[...])
    return yk, yv


def _compress_finish(y, posy, w2_ref):
    rows = y.shape[0]
    a = y[:, :LANE]
    b = pltpu.roll(y[:, LANE:], rows - 1, 0)
    pos = posy[0:1, :LANE] + posy[1:2, LANE:]
    hid = jax.nn.gelu(a + b + pos)
    return _dot(hid.astype(BF16), w2_ref[...])


def _compress_prompt_kernel(xk_ref, xv_ref, w1k_ref, w1v_ref, pk_ref, pv_ref, w2k_ref, w2v_ref, ok_ref, ov_ref):
    yk, yv = _compress_rows(xk_ref[...], xv_ref[...], w1k_ref, w1v_ref, pk_ref, pv_ref)
    ok_ref[...] = _compress_finish(yk, _dot(pk_ref[...], w1k_ref[...]), w2k_ref).astype(BF16)
    ov_ref[...] = _compress_finish(yv, _dot(pv_ref[...], w1v_ref[...]), w2v_ref).astype(BF16)


def _compress_prompt(kc, vc, wts, *, batch, seq):
    nch = seq // CMP_STRIDE
    xk = kc.reshape(batch * nch, CHUNK_FEATS)
    xv = vc.reshape(batch * nch, CHUNK_FEATS)
    row = lambda b: (b, 0)
    c2 = lambda b: (0, 0)
    wspec = pl.BlockSpec((CHUNK_FEATS, 2 * LANE), c2)
    pspec = pl.BlockSpec((8, CHUNK_FEATS), c2)
    w2spec = pl.BlockSpec((LANE, LANE), c2)
    return pl.pallas_call(
        _compress_prompt_kernel, grid=(batch,),
        in_specs=[pl.BlockSpec((nch, CHUNK_FEATS), row), pl.BlockSpec((nch, CHUNK_FEATS), row),
                  wspec, wspec, pspec, pspec, w2spec, w2spec],
        out_specs=[pl.BlockSpec((nch, LANE), row)] * 2,
        out_shape=[jax.ShapeDtypeStruct((batch * nch, LANE), BF16)] * 2,
        compiler_params=_cparams("parallel"), name="compress_prompt",
    )(xk, xv, wts['cmp_w1k'], wts['cmp_w1v'], wts['cmp_posk'], wts['cmp_posv'], wts['cmp_w2k'], wts['cmp_w2v'])


def _overlap(n_rows, n_sel):
    c = _iota((n_rows, n_sel), 0) * CMP_STRIDE
    j = _iota((n_rows, n_sel), 1) * SLC_BLOCK
    return ((c < j + SLC_BLOCK) & (c + CMP_BLOCK > j)).astype(F32)


def _force_scores(score, cur, jj):
    forced = (jj == 0) | (jj == cur) | (jj == cur - 1)
    score = jnp.where(forced, FORCE, score)
    return jnp.where(jj <= cur, score, NEG)


def _cmp_prompt_kernel(q_ref, k_ref, v_ref, o_ref, bias_ref, *, tq, n_cmp, n_sel):
    q_start = pl.program_id(1) * tq
    ncp = k_ref.shape[0]
    rows = NSA_HPG * tq
    qpos = q_start + (_iota((rows, 1), 0) & (tq - 1))
    cidx = _iota((1, ncp), 1)
    dist = qpos - (cidx * CMP_STRIDE + CMP_BLOCK - 1)
    mask = (dist >= 0) & (cidx < n_cmp)
    distf = dist.astype(F32)
    hrow = _vdiv(_iota((rows, 1), 0), tq)
    nsp = -(-n_sel // 8) * 8
    cur = _vdiv(q_start + _iota((1, tq), 1), SLC_BLOCK)
    jj = _iota((nsp, tq), 0)
    cb = _iota((nsp, ncp), 1) * CMP_STRIDE
    jb = _iota((nsp, ncp), 0) * SLC_BLOCK
    ov_t = ((cb < jb + SLC_BLOCK) & (cb + CMP_BLOCK > jb)).astype(F32)
    for g in range(NSA_GROUPS):
        heads = range(g * NSA_HPG, (g + 1) * NSA_HPG)
        qg = jnp.concatenate([q_ref[:, h * LANE:h * LANE + NSA_DH] for h in heads], axis=0)
        slope = jnp.zeros((rows, 1), F32)
        for hh, h in enumerate(heads):
            slope = jnp.where(hrow == hh, SLOPES[h], slope)
        s = _dot_nt(qg, k_ref[:, g * NSA_DH:(g + 1) * NSA_DH]) - slope * distf
        s = jnp.where(mask, s, NEG)
        m = jnp.max(s, axis=-1, keepdims=True)
        p = jnp.where(mask, jnp.exp(s - m), 0.0)
        l = jnp.sum(p, axis=-1, keepdims=True)
        p = p / jnp.where(l > 0.0, l, 1.0)
        o = _dot(p.astype(BF16), v_ref[:, g * NSA_DH:(g + 1) * NSA_DH])
        imp = p[0:tq]
        for hh in range(1, NSA_HPG):
            imp = imp + p[hh * tq:(hh + 1) * tq]
            o_ref[:, (g * NSA_HPG + hh) * NSA_DH:(g * NSA_HPG + hh + 1) * NSA_DH] = o[hh * tq:(hh + 1) * tq].astype(BF16)
        o_ref[:, g * NSA_HPG * NSA_DH:(g * NSA_HPG + 1) * NSA_DH] = o[0:tq].astype(BF16)
        score = lax.dot_general(ov_t, imp, _NT, preferred_element_type=F32, precision=lax.Precision.HIGHEST)
        score = _force_scores(score, cur, jj)
        rank = jnp.zeros((nsp, tq), F32)
        for i in range(n_sel):
            ri = score[i:i + 1, :]
            beats = (ri > score) | ((ri == score) & (i < jj))
            rank = rank + beats.astype(F32)
        sel = (rank < float(min(SLC_TOP_N, n_sel))) & (jj <= cur)
        bias_t = jnp.where(sel | (jj >= n_sel), 0.0, NEG)
        bias_t = jnp.concatenate([jnp.zeros((AUX_BLK0, tq), F32), bias_t,
                                  jnp.zeros((LANE - AUX_BLK0 - nsp, tq), F32)], axis=0)
        bias_ref[:, g * LANE:(g + 1) * LANE] = bias_t.T.astype(BF16)


def _cmp_prompt(qn, kcc, vcc, *, batch, seq, tq):
    n = qn.shape[0]
    nch = seq // CMP_STRIDE
    n_cmp = nch - CMP_BLOCK // CMP_STRIDE + 1
    n_sel = -(-seq // SLC_BLOCK)
    qrow = lambda b, i: (b * (seq // tq) + i, 0)
    kv = lambda b, i: (b, 0)
    kern = functools.partial(_cmp_prompt_kernel, tq=tq, n_cmp=n_cmp, n_sel=n_sel)
    return pl.pallas_call(
        kern, grid=(batch, seq // tq),
        in_specs=[pl.BlockSpec((tq, qn.shape[1]), qrow), pl.BlockSpec((nch, LANE), kv), pl.BlockSpec((nch, LANE), kv)],
        out_specs=[pl.BlockSpec((tq, NSA_HEADS * NSA_DH), qrow), pl.BlockSpec((tq, NSA_GROUPS * LANE), qrow)],
        out_shape=[jax.ShapeDtypeStruct((n, NSA_HEADS * NSA_DH), BF16),
                   jax.ShapeDtypeStruct((n, NSA_GROUPS * LANE), BF16)],
        compiler_params=_cparams("parallel", "arbitrary"), name="cmp_prompt",
    )(qn, kcc, vcc)


def _finish_kernel(*refs, period, latent, has_state):
    it = iter(refs)
    x_ref, omla_ref, ocmp_ref, oslc_ref, owin_ref, gn_ref, ga_ref, gb_ref = (next(it) for _ in range(8))
    prev1_ref, prev2_ref = (next(it), next(it)) if has_state else (None, None)
    gx_ref = next(it)
    wuv_ref = next(it) if latent else None
    (wpm_ref, wpn_ref, wo_ref, g2_ref, wg_ref, wu_ref, cw_ref, cb_ref, wd_ref, gf_ref,
     y_ref, gout_ref, carry_ref) = it
    tm = x_ref.shape[0]
    gn = gn_ref[...]
    gn_hi = gn.astype(BF16)
    gn_lo = (gn - gn_hi.astype(F32)).astype(BF16)
    gexp = _dot(gn_hi, gx_ref[...]) + _dot(gn_lo, gx_ref[...])
    w = NSA_HEADS * NSA_DH
    o_nsa = (gexp[:, 0:w] * ocmp_ref[...].astype(F32) + gexp[:, w:2 * w] * oslc_ref[...].astype(F32)
             + gexp[:, 2 * w:3 * w] * owin_ref[...].astype(F32))
    o_mla = omla_ref[...].astype(BF16)
    if latent:
        o_mla = _dot(o_mla, wuv_ref[...]).astype(BF16)
    merged = (ga_ref[...] * _dot(o_mla, wpm_ref[...])
              + gb_ref[...] * _dot(o_nsa.astype(BF16), wpn_ref[...]))
    x1 = x_ref[...] + _dot(merged.astype(BF16), wo_ref[...])
    h2 = _rms(x1, g2_ref[...]).astype(BF16)
    row = _iota((tm, 1), 0)
    i = pl.program_id(0)
    t = _vmod(i * tm + row, period)
    carried = period > tm
    if carried:
        @pl.when(i == 0)
        def _():
            carry_ref[...] = jnp.zeros_like(carry_ref)
    dff = wg_ref.shape[1]
    x2 = x1
    for c0 in range(0, dff, FFN_BLOCK):
        cs = slice(c0, min(c0 + FFN_BLOCK, dff))
        g = _dot(h2, wg_ref[:, cs])
        u = _dot(h2, wu_ref[:, cs])
        g1 = pltpu.roll(g, 1, 0)
        g2 = pltpu.roll(g, 2, 0)
        if carried:
            c = carry_ref[:, cs]
            g1 = jnp.where(row == 0, c[7:8], g1)
            g2 = jnp.where(row == 0, c[6:7], jnp.where(row == 1, c[7:8], g2))
            carry_ref[:, cs] = g[tm - 8:tm]
        g1 = jnp.where(t >= 1, g1, prev1_ref[:, cs] if has_state else 0.0)
        g2 = jnp.where(t >= 2, g2, prev2_ref[:, cs] if has_state else 0.0)
        cw = cw_ref[:, cs]
        conv = cb_ref[:, cs] + cw[0:1] * g2 + cw[1:2] * g1 + cw[2:3] * g
        act = (jax.nn.silu(conv) * u).astype(BF16)
        x2 = x2 + _dot(act, wd_ref[cs, :])
        gout_ref[:, cs] = g[tm - 8:tm] if gout_ref.shape[0] == 8 else g
    y_ref[...] = _rms(x2, gf_ref[...])


def _finish(x, omla, ocmp, oslc, owin, gn, ga, gb, state_rows, wts, *, tm, period, full_g, latent):
    n, d = x.shape
    dff = wts['w_gate'].shape[1]
    row = lambda i: (i, 0)
    c2 = lambda i: (0, 0)
    acts = [x, omla, ocmp, oslc, owin, gn, ga, gb] + (list(state_rows) if state_rows is not None else [])
    consts = [wts['gate_expand']] + ([wts['w_uvbd']] if latent else []) + [
        wts['w_proj_mla'], wts['w_proj_nsa'], wts['w_out'], wts['norm2_g'],
        wts['w_gate'], wts['w_up'], wts['conv_w'], wts['conv_b'], wts['w_down'], wts['norm_f_g']]
    ins = acts + consts
    in_specs = [pl.BlockSpec((tm, a.shape[1]), row) for a in acts] + [pl.BlockSpec(a.shape, c2) for a in consts]
    g_rows = n if full_g else (n // tm) * 8
    g_blk = tm if full_g else 8
    kern = functools.partial(_finish_kernel, period=period, latent=latent, has_state=state_rows is not None)
    return pl.pallas_call(
        kern, grid=(n // tm,), in_specs=in_specs,
        out_specs=[pl.BlockSpec((tm, d), row), pl.BlockSpec((g_blk, dff), row)],
        out_shape=[jax.ShapeDtypeStruct((n, d), F32), jax.ShapeDtypeStruct((g_rows, dff), F32)],
        scratch_shapes=[pltpu.VMEM((8, dff), F32)],
        compiler_params=_cparams("arbitrary"), name="finish_full" if full_g else "finish_tiled",
    )(*ins)


def _swap_halves(w):
    hlf = w.shape[-1] // 2
    return jnp.concatenate([-w[..., hlf:], w[..., :hlf]], axis=-1)


def _prep_weights(p):
    d = p['w_in'].shape[0]
    q_rank, kv_rank = p['q_norm_g'].shape[-1], p['kv_norm_g'].shape[-1]
    sizes = [q_rank, kv_rank, MLA_D_ROPE, NSA_HEADS * NSA_DH] + [2 * NSA_GROUPS * NSA_DH] * 3 + [3 * NSA_HEADS, d, d]
    cuts = np.cumsum(sizes)[:-1].tolist()
    cq, ckv, kr, qn, kvc, kvs, kvw, gn, ga, gb = jnp.split(p['w_in'], cuts, axis=-1)
    assert _O_CKV == q_rank and _O_QN == q_rank + kv_rank
    lo, hi = MLA_D_NOPE, HEAD_PAD - MLA_D_NOPE - MLA_D_ROPE
    place = lambda w: jnp.pad(w, ((0, 0), (lo, hi)))
    gnp = jnp.pad(gn, ((0, 0), (0, LANE - gn.shape[1])))
    w_in = jnp.concatenate([cq, ckv, qn, kvc, kvs, kvw, ga, gb, place(kr), place(_swap_halves(kr)), gnp], axis=1)
    w = {'w_in': w_in.astype(BF16)}
    for k in ('norm1_g', 'q_norm_g', 'kv_norm_g', 'norm2_g', 'conv_b'):
        w[k] = p[k].reshape(1, -1)
    w['norm_f_g'] = p['norm_f_g'].reshape(1, -1)
    w['conv_w'] = jnp.pad(p['conv_w'], ((0, 8 - CONV_W), (0, 0)))
    uq = p['w_uq']
    hpad = ((0, 0), (0, 0), (0, HEAD_PAD - uq.shape[-1]))
    uq_a = jnp.pad(uq, hpad)
    uq_b = jnp.pad(jnp.concatenate([jnp.zeros_like(uq[..., :MLA_D_NOPE]), _swap_halves(uq[..., MLA_D_NOPE:])], -1), hpad)
    w['w_uq2'] = jnp.concatenate([uq_a.reshape(q_rank, -1), uq_b.reshape(q_rank, -1)], axis=1).astype(BF16)
    uk = p['w_uk']
    w['w_ukp'] = jnp.pad(uk, ((0, 0), (0, 0), (0, HEAD_PAD - MLA_D_NOPE))).reshape(kv_rank, -1).astype(BF16)
    w['w_ukT'] = jnp.pad(jnp.transpose(uk, (1, 2, 0)), ((0, 0), (0, HEAD_PAD - MLA_D_NOPE), (0, 0))).astype(BF16)
    w['w_uvf'] = jnp.pad(p['w_uv'], ((0, 0), (0, 0), (0, HEAD_PAD - MLA_D_V))).reshape(kv_rank, -1).astype(BF16)
    eye_h = jnp.eye(MLA_HEADS, dtype=F32)
    w['w_uvbd'] = jnp.einsum('rhv,hk->hrkv', p['w_uv'], eye_h).reshape(MLA_HEADS * kv_rank, -1).astype(BF16)
    eye_g = jnp.eye(NSA_GROUPS, dtype=F32)
    for nm in ('k', 'v'):
        w1 = p['cmp_w1_' + nm].reshape(2, CMP_STRIDE, NSA_DH, -1)
        big = jnp.einsum('ajdh,gk->jgdakh', w1, eye_g)
        w['cmp_w1' + nm] = big.reshape(CHUNK_FEATS, -1).astype(BF16)
        pos = p['cmp_pos_' + nm].reshape(2, CMP_STRIDE, 1, NSA_DH)
        pos = jnp.broadcast_to(pos, (2, CMP_STRIDE, NSA_GROUPS, NSA_DH)).reshape(2, CHUNK_FEATS)
        w['cmp_pos' + nm] = jnp.pad(pos, ((0, 6), (0, 0))).astype(BF16)
        w2 = p['cmp_w2_' + nm]
        w['cmp_w2' + nm] = jnp.einsum('hd,gk->ghkd', w2, eye_g).reshape(NSA_GROUPS * w2.shape[0], -1).astype(BF16)
    ge = np.zeros((LANE, 3 * NSA_HEADS * NSA_DH), np.float32)
    for h in range(NSA_HEADS):
        for i in range(3):
            ge[h * 3 + i, i * NSA_HEADS * NSA_DH + h * NSA_DH:i * NSA_HEADS * NSA_DH + (h + 1) * NSA_DH] = 1.0
    w['gate_expand'] = jnp.asarray(ge)
    for k in ('w_proj_mla', 'w_proj_nsa', 'w_out', 'w_gate', 'w_up', 'w_down'):
        w[k] = p[k].astype(BF16)
    return w


def _rope_tables(pos):
    inv = ROPE_THETA ** (-jnp.arange(0, MLA_D_ROPE, 2, dtype=F32) / MLA_D_ROPE)
    ang = pos.astype(F32)[:, None] * inv[None, :]
    cos, sin = jnp.cos(ang), jnp.sin(ang)
    n = pos.shape[0]
    pad = jnp.zeros((n, HEAD_PAD - MLA_D_NOPE - MLA_D_ROPE), F32)
    cosq = jnp.concatenate([jnp.ones((n, MLA_D_NOPE), F32), cos, cos, pad], axis=1)
    sinq = jnp.concatenate([jnp.zeros((n, MLA_D_NOPE), F32), sin, sin, pad], axis=1)
    return cosq, sinq


_NSA_GROUPS_SPEC = tuple((tuple(range(g * NSA_HPG, (g + 1) * NSA_HPG)), g) for g in range(NSA_GROUPS))
_MLA_GROUPS_SPEC = tuple(((h,), h) for h in range(MLA_HEADS))


def _prompt(x_prompt, w):
    b, t, d = x_prompt.shape
    n = b * t
    x = x_prompt.reshape(n, d)
    tm = 256
    cosq, sinq = _rope_tables(jnp.arange(t, dtype=jnp.int32))
    (q_mla, k_mla, v_mla, ckv, krp, qn, kc, vc, ks, vs, kw, vw, kc_b, vc_b, ks_b, vs_b, kw_b, vw_b, gn, ga, gb) = _inproj(
        x, w, cosq, sinq, sample=False, tm=tm, tab_blocks=t // tm)
    o_mla = _flash(q_mla, k_mla, v_mla, None, batch=b, seq=t, tq=512, tk=512, groups=_MLA_GROUPS_SPEC, par=8,
                   dq=HEAD_PAD, dk=HEAD_PAD, dv=MLA_D_V, window=0, base2=True, name="mla_prompt")
    kcc, vcc = _compress_prompt(kc_b, vc_b, w, batch=b, seq=t)
    o_cmp, sel_bias = _cmp_prompt(qn, kcc, vcc, batch=b, seq=t, tq=128)
    o_slc = _flash(qn, ks_b, vs_b, sel_bias, batch=b, seq=t, tq=256, tk=512, groups=_NSA_GROUPS_SPEC, par=2,
                   dq=LANE, dk=LANE, dv=NSA_DH, window=0, base2=False, name="slc_prompt")
    o_win = _win_prompt(qn, kw_b, vw_b, batch=b, seq=t, tq=WINDOW // 2, groups=_NSA_GROUPS_SPEC, dv=NSA_DH)
    dff = w['w_gate'].shape[1]
    y, gtail = _finish(x, o_mla, o_cmp, o_slc, o_win, gn, ga, gb, None, w, tm=tm, period=t, full_g=False,
                       latent=False)
    kv4 = lambda a: a.reshape(1, b, NSA_GROUPS, NSA_DH, a.shape[-1]).transpose(0, 1, 4, 2, 3)
    n_keep = min(WINDOW, t)
    kw, vw = kw[:, :, t - n_keep:], vw[:, :, t - n_keep:]
    conv_state = gtail.reshape(b, t // tm, 8, dff)[:, -1, 8 - (CONV_W - 1):, :]
    states = (ckv.reshape(1, b, t, -1), krp[:, MLA_D_NOPE:MLA_D_NOPE + MLA_D_ROPE].reshape(1, b, t, MLA_D_ROPE),
              kv4(kc), kv4(vc), kv4(ks), kv4(vs), kv4(kw), kv4(vw), conv_state[None])
    return y.reshape(b, t, d), states


def _page_copies(pt_ref, pools, bufs, sems, step, slot, pp):
    copies = []
    for k in range(pp):
        page = pt_ref[step * pp + k]
        for pool, buf, sem in zip(pools, bufs, sems):
            copies.append(pltpu.make_async_copy(pool.at[page], buf.at[slot, k], sem.at[slot]))
    return copies


def _start_all(copies, n_pools):
    for i, c in enumerate(copies):
        c.start(priority=(i // n_pools) % 2)


def _stream_pages(pt_ref, pools, bufs, sems, pp):
    step = pl.program_id(0) * pl.num_programs(1) + pl.program_id(1)
    total = pl.num_programs(0) * pl.num_programs(1)
    slot = step % 2

    @pl.when(step == 0)
    def _():
        _start_all(_page_copies(pt_ref, pools, bufs, sems, step, slot, pp), len(pools))

    @pl.when(step + 1 < total)
    def _():
        _start_all(_page_copies(pt_ref, pools, bufs, sems, step + 1, 1 - slot, pp), len(pools))

    for c in _page_copies(pt_ref, pools, bufs, sems, step, slot, pp):
        c.wait()
    return slot


def _softmax_update(sc, v, m_scr, l_scr, acc_scr, v_transposed=False):
    m_old = m_scr[...]
    m_new = jnp.maximum(m_old, jnp.max(sc, axis=-1, keepdims=True))
    p = jnp.exp(sc - m_new)
    alpha = jnp.exp(m_old - m_new)
    l_scr[...] = alpha * l_scr[...] + jnp.sum(p, axis=-1, keepdims=True)
    pv = _dot_nt(p.astype(BF16), v) if v_transposed else _dot(p.astype(BF16), v)
    acc_scr[...] = alpha * acc_scr[...] + pv
    m_scr[...] = m_new


def _mla_decode_step(slot, q_ref, knew_ref, o_ref, cbuf, rbuf, kscr, krscr, m_scr, l_scr, acc_scr, *, pp, td, kv_rank):
    s = pl.program_id(1)
    q = q_ref[...]
    rows = q.shape[0]

    @pl.when(s == 0)
    def _():
        m_scr[...] = jnp.full_like(m_scr, NEG)
        l_scr[...] = jnp.zeros_like(l_scr)
        acc_scr[...] = jnp.zeros_like(acc_scr)
        kn = knew_ref[...]
        trow = _vdiv(_iota((rows, 1), 0), MLA_HEADS)
        col = _iota((1, kn.shape[0]), 1)
        sc = jnp.where((col <= trow) & (col < td), _dot_nt(q, kn), NEG)
        _softmax_update(sc, kn[:, :kv_rank], m_scr, l_scr, acc_scr)

    for k in range(pp):
        kscr[k * PAGE_SIZE:(k + 1) * PAGE_SIZE, :] = cbuf[slot, k].astype(BF16)
        krscr[:, k * PAGE_SIZE:(k + 1) * PAGE_SIZE] = rbuf[slot, k].astype(BF16)
    kt = kscr[...]
    sc = _dot_nt(q[:, :kv_rank], kt) + _dot(q[:, kv_rank:kv_rank + MLA_D_ROPE], krscr[...])
    _softmax_update(sc, kt, m_scr, l_scr, acc_scr)

    @pl.when(s == pl.num_programs(1) - 1)
    def _():
        o_ref[...] = acc_scr[...] / l_scr[...]


def _alibi_rows(rows, td):
    r = _iota((rows, 1), 0)
    return _vdiv(r, td), _vmod(r, td)


def _slope_rows(hrow, g):
    slope = jnp.zeros(hrow.shape, F32)
    for hh in range(NSA_HPG):
        slope = jnp.where(hrow == hh, SLOPES[g * NSA_HPG + hh], slope)
    return slope


def _cmp_decode_step(slot, q_ref, w1k_ref, w1v_ref, pk_ref, pv_ref, w2k_ref, w2v_ref, o_ref, score_ref,
                     kbuf, vbuf, kp0, kp1, vp0, vp1, yk_scr, yv_scr, *, pp, td, past, n_cmp, n_sel_pad):
    s = pl.program_id(1)
    cpp = PAGE_SIZE // CMP_STRIDE
    hp = pp // 2
    for half, (kp_scr, vp_scr) in enumerate(((kp0, vp0), (kp1, vp1))):
        for k in range(hp):
            kp_scr[k * PAGE_SIZE:(k + 1) * PAGE_SIZE, :] = kbuf[slot, half * hp + k].T
            vp_scr[k * PAGE_SIZE:(k + 1) * PAGE_SIZE, :] = vbuf[slot, half * hp + k].T
    chunk_rows = lambda scr: jnp.concatenate(
        [scr[pl.ds(j, hp * cpp, stride=CMP_STRIDE), :] for j in range(CMP_STRIDE)], axis=1).astype(BF16)
    for half, (kp_scr, vp_scr) in enumerate(((kp0, vp0), (kp1, vp1))):
        r0 = pl.multiple_of(s * (pp * cpp) + half * (hp * cpp), hp * cpp)
        yk_scr[pl.ds(r0, hp * cpp), :] = _dot(chunk_rows(kp_scr), w1k_ref[...])
        yv_scr[pl.ds(r0, hp * cpp), :] = _dot(chunk_rows(vp_scr), w1v_ref[...])

    @pl.when(s == pl.num_programs(1) - 1)
    def _():
        kcc = _compress_finish(yk_scr[...], _dot(pk_ref[...], w1k_ref[...]), w2k_ref).astype(BF16)
        vcc = _compress_finish(yv_scr[...], _dot(pv_ref[...], w1v_ref[...]), w2v_ref).astype(BF16)
        ncp = kcc.shape[0]
        rows = NSA_HPG * td
        hrow, trow = _alibi_rows(rows, td)
        cidx = _iota((1, ncp), 1)
        dist = (past + trow) - (cidx * CMP_STRIDE + CMP_BLOCK - 1)
        mask = (dist >= 0) & (cidx < n_cmp)
        distf = dist.astype(F32)
        tsum = (_vmod(_iota((8, rows), 1), td) == _iota((8, rows), 0)).astype(F32)
        ov = _overlap(ncp, n_sel_pad)
        t8 = _iota((8, 1), 0)
        cur = _vdiv(past + t8, SLC_BLOCK)
        jj = _iota((8, n_sel_pad), 1)
        for g in range(NSA_GROUPS):
            sc = _dot_nt(q_ref[g], kcc) - _slope_rows(hrow, g) * distf
            sc = jnp.where(mask, sc, NEG)
            m = jnp.max(sc, axis=-1, keepdims=True)
            p = jnp.where(mask, jnp.exp(sc - m), 0.0)
            l = jnp.sum(p, axis=-1, keepdims=True)
            p = p / jnp.where(l > 0.0, l, 1.0)
            o_ref[g] = _dot(p.astype(BF16), vcc)
            imp = _dot_exact(tsum, p)
            score_ref[g] = _force_scores(_dot_exact(imp, ov), cur, jj)


def _mla_cmp_decode_kernel(pt_ref, qm_ref, knew_ref, qc_ref, w1k_ref, w1v_ref, pk_ref, pv_ref, w2k_ref, w2v_ref,
                           ckv_hbm, kr_hbm, ck_hbm, cv_hbm, olat_ref, ocmp_ref, score_ref,
                           cbuf, rbuf, kbuf, vbuf, csem, rsem, ksem, vsem,
                           kscr, krscr, m_scr, l_scr, acc_scr, kp0, kp1, vp0, vp1, yk_scr, yv_scr,
                           *, pp, td, kv_rank, past, n_cmp, n_sel_pad):
    slot = _stream_pages(pt_ref, (ckv_hbm, kr_hbm, ck_hbm, cv_hbm), (cbuf, rbuf, kbuf, vbuf),
                         (csem, rsem, ksem, vsem), pp)
    _mla_decode_step(slot, qm_ref, knew_ref, olat_ref, cbuf, rbuf, kscr, krscr, m_scr, l_scr, acc_scr,
                     pp=pp, td=td, kv_rank=kv_rank)
    _cmp_decode_step(slot, qc_ref, w1k_ref, w1v_ref, pk_ref, pv_ref, w2k_ref, w2v_ref, ocmp_ref, score_ref,
                     kbuf, vbuf, kp0, kp1, vp0, vp1, yk_scr, yv_scr, pp=pp, td=td, past=past, n_cmp=n_cmp,
                     n_sel_pad=n_sel_pad)


def _mla_cmp_decode(pt, qd_mla, knew, qd, ckv_pool, kr_pool, k_pool, v_pool, wts, *, td, pp, past, n_sel_pad):
    bd, mrows, qw = qd_mla.shape
    kv_rank = ckv_pool.shape[-1]
    n_pages = pt.shape[0] // bd
    cpp = PAGE_SIZE // CMP_STRIDE
    nch = n_pages * cpp
    n_cmp = (past + td) // CMP_STRIDE - CMP_BLOCK // CMP_STRIDE + 1
    assert (past + td) // CMP_STRIDE == nch, "new rows must not complete a chunk"
    per_b3 = lambda b, s, pt: (b, 0, 0)
    per_b = lambda b, s, pt: (b, 0, 0, 0)
    c2 = lambda b, s, pt: (0, 0)
    rows = qd.shape[2]
    in_specs = ([pl.BlockSpec((None, mrows, qw), per_b3), pl.BlockSpec((None,) + knew.shape[1:], per_b3),
                 pl.BlockSpec((None,) + qd.shape[1:], per_b)]
                + [pl.BlockSpec((CHUNK_FEATS, 2 * LANE), c2) for _ in range(2)]
                + [pl.BlockSpec((8, CHUNK_FEATS), c2) for _ in range(2)]
                + [pl.BlockSpec((LANE, LANE), c2) for _ in range(2)]
                + [pl.BlockSpec(memory_space=pl.ANY) for _ in range(4)])
    kern = functools.partial(_mla_cmp_decode_kernel, pp=pp, td=td, kv_rank=kv_rank, past=past, n_cmp=n_cmp,
                             n_sel_pad=n_sel_pad)
    pools = (ckv_pool, kr_pool, k_pool, v_pool)
    return pl.pallas_call(
        kern,
        grid_spec=pltpu.PrefetchScalarGridSpec(
            num_scalar_prefetch=1, grid=(bd, n_pages // pp), in_specs=in_specs,
            out_specs=[pl.BlockSpec((None, mrows, kv_rank), per_b3),
                       pl.BlockSpec((None, NSA_GROUPS, rows, LANE), per_b),
                       pl.BlockSpec((None, NSA_GROUPS, 8, n_sel_pad), per_b)],
            scratch_shapes=[pltpu.VMEM((2, pp) + p.shape[1:], F32) for p in pools]
            + [pltpu.SemaphoreType.DMA((2,)) for _ in pools]
            + [pltpu.VMEM((pp * PAGE_SIZE, kv_rank), BF16), pltpu.VMEM((MLA_D_ROPE, pp * PAGE_SIZE), BF16),
               pltpu.VMEM((mrows, 1), F32), pltpu.VMEM((mrows, 1), F32), pltpu.VMEM((mrows, kv_rank), F32)]
            + [pltpu.VMEM((pp // 2 * PAGE_SIZE, LANE), F32) for _ in range(4)]
            + [pltpu.VMEM((nch, 2 * LANE), F32) for _ in range(2)]),
        out_shape=[jax.ShapeDtypeStruct((bd, mrows, kv_rank), F32),
                   jax.ShapeDtypeStruct((bd, NSA_GROUPS, rows, LANE), F32),
                   jax.ShapeDtypeStruct((bd, NSA_GROUPS, 8, n_sel_pad), F32)],
        compiler_params=_cparams("arbitrary", "arbitrary"), name="mla_cmp_decode",
    )(pt, qd_mla, knew, qd, wts['cmp_w1k'], wts['cmp_w1v'], wts['cmp_posk'], wts['cmp_posv'],
      wts['cmp_w2k'], wts['cmp_w2v'], *pools)


def _rank_kernel(score_ref, cur_ref, sel_ref, *, n_sel):
    sc = score_ref[...]
    jj = _iota(sc.shape, 0)

    def body(i, rank):
        ri = score_ref[pl.ds(i, 1), :]
        beats = (ri > sc) | ((ri == sc) & (i < jj))
        return rank + beats.astype(F32)

    rank = lax.fori_loop(0, n_sel, body, jnp.zeros(sc.shape, F32))
    sel = (rank < float(min(SLC_TOP_N, n_sel))) & (jj <= cur_ref[...])
    sel_ref[...] = sel.astype(F32)


def _rank(score_t, cur, *, n_sel):
    full = lambda a: pl.BlockSpec(a.shape, lambda: (0,) * a.ndim)
    return pl.pallas_call(
        functools.partial(_rank_kernel, n_sel=n_sel), in_specs=[full(score_t), full(cur)],
        out_specs=full(score_t), out_shape=jax.ShapeDtypeStruct(score_t.shape, F32), name="rank_decode",
    )(score_t, cur)


def _slc_decode_kernel(pt_ref, cnt_ref, q_ref, sel_ref, selnew_ref, knew_ref, vnew_ref, kpos_ref, expand_ref,
                       k_hbm, v_hbm, o_ref, kbuf, vbuf, ksem, vsem, kscr, vscr, m_scr, l_scr, acc_scr,
                       *, pp, td, past):
    b, s, ns = pl.program_id(0), pl.program_id(1), pl.num_programs(1)
    step = b * ns + s
    slot = step % 2
    pools, bufs, sems = (k_hbm, v_hbm), (kbuf, vbuf), (ksem, vsem)
    active = s * pp < cnt_ref[b]
    wrap = s + 1 == ns
    nb = jnp.minimum(jnp.where(wrap, b + 1, b), pl.num_programs(0) - 1)
    next_active = (step + 1 < pl.num_programs(0) * ns) & (jnp.where(wrap, 0, s + 1) * pp < cnt_ref[nb])

    @pl.when((step == 0) & active)
    def _():
        _start_all(_page_copies(pt_ref, pools, bufs, sems, step, slot, pp), len(pools))

    @pl.when(next_active)
    def _():
        _start_all(_page_copies(pt_ref, pools, bufs, sems, step + 1, 1 - slot, pp), len(pools))

    rows = q_ref.shape[1]
    hrow, trow = _alibi_rows(rows, td)

    @pl.when(s == 0)
    def _():
        m_scr[...] = jnp.full_like(m_scr, NEG)
        l_scr[...] = jnp.zeros_like(l_scr)
        acc_scr[...] = jnp.zeros_like(acc_scr)
        kn = knew_ref[...]
        vn = vnew_ref[...]
        col = _iota((1, kn.shape[0]), 1)
        dist = trow - col
        for g in range(NSA_GROUPS):
            sc = _dot_nt(q_ref[g], kn) - _slope_rows(hrow, g) * dist.astype(F32)
            mask = (dist >= 0) & (col < td) & (selnew_ref[g] > 0.5)
            _softmax_update(jnp.where(mask, sc, NEG), vn, m_scr.at[g], l_scr.at[g], acc_scr.at[g])

    @pl.when(active)
    def _():
        for c in _page_copies(pt_ref, pools, bufs, sems, step, slot, pp):
            c.wait()
        for k in range(pp):
            kscr[:, k * PAGE_SIZE:(k + 1) * PAGE_SIZE] = kbuf[slot, k].astype(BF16)
            vscr[:, k * PAGE_SIZE:(k + 1) * PAGE_SIZE] = vbuf[slot, k].astype(BF16)
        kt = kscr[...]
        vt = vscr[...]
        distf = (past + trow).astype(F32) - kpos_ref[...]
        for g in range(NSA_GROUPS):
            sc = _dot(q_ref[g], kt) - _slope_rows(hrow, g) * distf
            selx = _dot(sel_ref[g], expand_ref[...]) > 0.5
            _softmax_update(jnp.where(selx, sc, NEG), vt, m_scr.at[g], l_scr.at[g], acc_scr.at[g], v_transposed=True)

    @pl.when(s == ns - 1)
    def _():
        o_ref[...] = acc_scr[...] / l_scr[...]


def _slc_decode(pt, counts, qd, sel16, selnew, knew, vnew, kpos, k_pool, v_pool, *, td, pp, past):
    bd = qd.shape[0]
    rows = qd.shape[2]
    n_pages = pt.shape[0] // bd
    tk = pp * PAGE_SIZE
    expand = (jnp.arange(tk)[None, :] // SLC_BLOCK == jnp.arange(tk // SLC_BLOCK)[:, None]).astype(BF16)
    per_b = lambda b, s, pt, cnt: (b, 0, 0, 0)
    per_b3 = lambda b, s, pt, cnt: (b, 0, 0)
    in_specs = ([pl.BlockSpec((None,) + qd.shape[1:], per_b),
                 pl.BlockSpec((None, None) + sel16.shape[2:], lambda b, s, pt, cnt: (b, s, 0, 0, 0)),
                 pl.BlockSpec((None,) + selnew.shape[1:], per_b),
                 pl.BlockSpec((None,) + knew.shape[1:], per_b3), pl.BlockSpec((None,) + vnew.shape[1:], per_b3),
                 pl.BlockSpec((None, None, 1, tk), lambda b, s, pt, cnt: (b, s, 0, 0)),
                 pl.BlockSpec(expand.shape, lambda b, s, pt, cnt: (0, 0))]
                + [pl.BlockSpec(memory_space=pl.ANY) for _ in range(2)])
    kern = functools.partial(_slc_decode_kernel, pp=pp, td=td, past=past)
    return pl.pallas_call(
        kern,
        grid_spec=pltpu.PrefetchScalarGridSpec(
            num_scalar_prefetch=2, grid=(bd, n_pages // pp), in_specs=in_specs,
            out_specs=pl.BlockSpec((None, NSA_GROUPS, rows, LANE), per_b),
            scratch_shapes=[pltpu.VMEM((2, pp) + k_pool.shape[1:], F32) for _ in range(2)]
            + [pltpu.SemaphoreType.DMA((2,)) for _ in range(2)]
            + [pltpu.VMEM((LANE, pp * PAGE_SIZE), BF16) for _ in range(2)]
            + [pltpu.VMEM((NSA_GROUPS, rows, 1), F32) for _ in range(2)] + [pltpu.VMEM((NSA_GROUPS, rows, LANE), F32)]),
        out_shape=jax.ShapeDtypeStruct((bd, NSA_GROUPS, rows, LANE), F32),
        compiler_params=_cparams("arbitrary", "arbitrary"), name="slc_decode",
    )(pt, counts, qd, sel16, selnew, knew, vnew, kpos, expand, k_pool, v_pool)


def _win_decode_kernel(q_ref, kst_ref, vst_ref, knew_ref, vnew_ref, o_ref, *, td, past):
    rows = q_ref.shape[1]
    hrow, trow = _alibi_rows(rows, td)
    nbuf = kst_ref.shape[1]
    kst = kst_ref[...].astype(BF16)
    vst = vst_ref[...].astype(BF16)
    kn = knew_ref[...]
    vn = vnew_ref[...]
    kpos = past - nbuf + _iota((1, nbuf), 1)
    d1 = (past + trow) - kpos
    m1 = (d1 >= 0) & (d1 < WINDOW) & (kpos >= 0)
    col = _iota((1, kn.shape[0]), 1)
    d2 = trow - col
    m2 = (d2 >= 0) & (d2 < WINDOW) & (col < td)
    for g in range(NSA_GROUPS):
        slope = _slope_rows(hrow, g)
        s1 = jnp.where(m1, _dot(q_ref[g], kst) - slope * d1.astype(F32), NEG)
        s2 = jnp.where(m2, _dot_nt(q_ref[g], kn) - slope * d2.astype(F32), NEG)
        m = jnp.maximum(jnp.max(s1, axis=-1, keepdims=True), jnp.max(s2, axis=-1, keepdims=True))
        p1 = jnp.exp(s1 - m)
        p2 = jnp.exp(s2 - m)
        l = jnp.sum(p1, axis=-1, keepdims=True) + jnp.sum(p2, axis=-1, keepdims=True)
        o_ref[g] = (_dot_nt(p1.astype(BF16), vst) + _dot(p2.astype(BF16), vn)) / l


def _win_decode(qd, kst, vst, knew, vnew, *, td, past):
    bd = qd.shape[0]
    rows = qd.shape[2]
    per_b = lambda b: (b, 0, 0, 0)
    per_b3 = lambda b: (b, 0, 0)
    blk3 = lambda a: pl.BlockSpec((None,) + a.shape[1:], per_b3)
    return pl.pallas_call(
        functools.partial(_win_decode_kernel, td=td, past=past), grid=(bd,),
        in_specs=[pl.BlockSpec((None,) + qd.shape[1:], per_b), blk3(kst), blk3(vst), blk3(knew), blk3(vnew)],
        out_specs=pl.BlockSpec((None, NSA_GROUPS, rows, LANE), per_b),
        out_shape=jax.ShapeDtypeStruct((bd, NSA_GROUPS, rows, LANE), F32),
        compiler_params=_cparams("parallel"), name="win_decode",
    )(qd, kst, vst, knew, vnew)


def _pad_rows(a, rows):
    return jnp.pad(a, ((0, 0), (0, rows - a.shape[1]), (0, 0)))


def _sample(x_sample, caches, page_table, w):
    (c_ckv, c_krope, c_cmp_k, c_cmp_v, c_slc_k, c_slc_v, s_win_k, s_win_v, s_conv) = caches
    bd, td, d = x_sample.shape
    n = bd * td
    n_pages = page_table.shape[1]
    past = n_pages * PAGE_SIZE
    n_pool = c_ckv.shape[0]
    kv_rank = c_ckv.shape[-1]
    x = x_sample.reshape(n, d)
    pos = past + jnp.arange(td, dtype=jnp.int32)
    cosq, sinq = _rope_tables(jnp.tile(pos, bd))
    (qrot, qabs, ckv, krp, qn, kc, vc, ks, vs, kw, vw, gn, ga, gb) = _inproj(
        x, w, cosq, sinq, sample=True, tm=n, tab_blocks=1)
    pt = page_table.reshape(-1)
    krope = krp[:, MLA_D_NOPE:MLA_D_NOPE + MLA_D_ROPE]
    qrope = qrot.reshape(bd, td, MLA_HEADS, HEAD_PAD)[..., MLA_D_NOPE:MLA_D_NOPE + MLA_D_ROPE]
    qpad = LANE - MLA_D_ROPE
    qd_mla = jnp.concatenate([qabs.reshape(bd, td, MLA_HEADS, kv_rank), qrope,
                              jnp.zeros((bd, td, MLA_HEADS, qpad), F32)], axis=-1)
    qd_mla = qd_mla.reshape(bd, td * MLA_HEADS, kv_rank + LANE).astype(BF16)
    knew = jnp.concatenate([ckv, krope, jnp.zeros((n, qpad), F32)], axis=-1).reshape(bd, td, -1)
    knew = _pad_rows(knew, 8).astype(BF16)
    pos_minor = lambda c: jnp.moveaxis(c, 1, -1).reshape(c.shape[0], -1, c.shape[1])
    qg = qn.astype(F32).reshape(bd, td, NSA_GROUPS, NSA_HPG, NSA_DH).transpose(0, 2, 3, 1, 4)
    qg = qg.reshape(bd, NSA_GROUPS, NSA_HPG * td, NSA_DH)
    lane_g = (jnp.arange(LANE) // NSA_DH)[None, :] == jnp.arange(NSA_GROUPS)[:, None]
    qd = jnp.where(lane_g[None, :, None, :], jnp.tile(qg, (1, 1, 1, NSA_GROUPS)), 0.0).astype(BF16)
    n_sel = -(-(past + td) // SLC_BLOCK)
    n_sel_pad = -(-n_sel // LANE) * LANE
    o_lat, o_cmp, score = _mla_cmp_decode(pt, qd_mla, knew, qd, c_ckv, pos_minor(c_krope), pos_minor(c_cmp_k),
                                          pos_minor(c_cmp_v), w, td=td, pp=min(32, n_pages), past=past,
                                          n_sel_pad=n_sel_pad)
    cur = jnp.broadcast_to((pos // SLC_BLOCK).astype(jnp.int32), (bd, NSA_GROUPS, td)).reshape(1, -1)
    n_sel8 = -(-n_sel // 8) * 8
    sel_t = _rank(score[:, :, :td, :n_sel8].reshape(-1, n_sel8).T, cur, n_sel=n_sel)
    sel = jnp.pad(sel_t.T, ((0, 0), (0, n_sel_pad - n_sel8))).reshape(bd, NSA_GROUPS, td, n_sel_pad)
    pp_slc = min(32, n_pages)
    bpp = PAGE_SIZE // SLC_BLOCK
    n_past_blk = past // SLC_BLOCK
    sel_pg = sel[..., :n_past_blk].reshape(bd, NSA_GROUPS * td, n_pages, bpp)
    need_page = sel_pg.max(axis=(1, 3)) > 0.5
    counts = need_page.sum(axis=-1).astype(jnp.int32)
    n_before = jnp.cumsum(need_page.astype(jnp.int32), axis=-1)
    u_before = jnp.cumsum(1 - need_page.astype(jnp.int32), axis=-1)
    slot_of = jnp.where(need_page, n_before - 1, counts[:, None] + u_before - 1)
    perm = slot_of[:, None, :] == jnp.arange(n_pages, dtype=jnp.int32)[None, :, None]
    order = jnp.sum(jnp.where(perm, jnp.arange(n_pages, dtype=jnp.int32), 0), axis=-1)
    pt_slc = jnp.sum(jnp.where(perm, page_table[:, None, :], 0), axis=-1).reshape(-1)
    sel_listed = jnp.einsum('bip,bxpc->bxic', perm.astype(BF16), sel_pg.astype(BF16), preferred_element_type=F32)
    bps = pp_slc * bpp
    sel_steps = sel_listed.reshape(bd, NSA_GROUPS, td, n_pages // pp_slc, bps).transpose(0, 3, 1, 2, 4)
    sel16 = jnp.tile(sel_steps, (1, 1, 1, NSA_HPG, 1)).astype(BF16)
    kpos = (order[:, :, None] * PAGE_SIZE + jnp.arange(PAGE_SIZE, dtype=jnp.int32)).astype(F32)
    kpos = kpos.reshape(bd, n_pages // pp_slc, 1, pp_slc * PAGE_SIZE)
    selnew = jnp.tile(jnp.broadcast_to(sel[..., n_past_blk:n_past_blk + 1], (bd, NSA_GROUPS, td, 8)), (1, 1, NSA_HPG, 1))
    new8 = lambda a: _pad_rows(a.reshape(bd, td, LANE), 8).astype(BF16)
    o_slc = _slc_decode(pt_slc, counts, qd, sel16, selnew, new8(ks), new8(vs), kpos, pos_minor(c_slc_k),
                        pos_minor(c_slc_v), td=td, pp=pp_slc, past=past)
    nbuf = s_win_k.shape[1]
    o_win = _win_decode(qd, pos_minor(s_win_k), pos_minor(s_win_v), new8(kw), new8(vw),
                        td=td, past=past)

    def heads_out(o):
        o = o.reshape(bd, NSA_GROUPS, NSA_HPG, td, NSA_GROUPS, NSA_DH)
        o = jnp.stack([o[:, g, :, :, g] for g in range(NSA_GROUPS)], axis=1)
        return o.transpose(0, 3, 1, 2, 4).reshape(n, NSA_HEADS * NSA_DH)

    dff = w['w_gate'].shape[1]
    prev1 = jnp.zeros((bd, td, dff), F32).at[:, 0].set(s_conv[:, 1])
    prev2 = jnp.zeros((bd, td, dff), F32).at[:, 0].set(s_conv[:, 0]).at[:, 1].set(s_conv[:, 1])
    y, g = _finish(x, o_lat.reshape(n, MLA_HEADS * kv_rank), heads_out(o_cmp), heads_out(o_slc), heads_out(o_win),
                   gn, ga, gb, (prev1.reshape(n, dff), prev2.reshape(n, dff)), w, tm=n, period=td, full_g=True,
                   latent=True)
    kv4 = lambda a: a.reshape(1, bd, td, NSA_GROUPS, NSA_DH)
    win = lambda st, new: jnp.concatenate([st, new.reshape(bd, td, NSA_GROUPS, NSA_DH)], axis=1)[None, :, -nbuf:]
    conv_state = jnp.concatenate([s_conv, g.reshape(bd, td, dff)], axis=1)[None, :, -(CONV_W - 1):]
    states = (ckv.reshape(1, bd, td, kv_rank), krope.reshape(1, bd, td, MLA_D_ROPE), kv4(kc), kv4(vc), kv4(ks), kv4(vs),
              win(s_win_k, kw), win(s_win_v, vw), conv_state)
    return y.reshape(bd, td, d), states


def kernel(x_prompt, x_sample, cache_mla_ckv, cache_mla_krope, cache_nsa_cmp_k, cache_nsa_cmp_v, cache_nsa_slc_k, cache_nsa_slc_v, state_win_k, state_win_v, state_ffn_conv, page_table, norm1_g, w_in, q_norm_g, kv_norm_g, w_uq, w_uk, w_uv, cmp_pos_k, cmp_w1_k, cmp_w2_k, cmp_pos_v, cmp_w1_v, cmp_w2_v, w_proj_mla, w_proj_nsa, w_out, norm2_g, w_gate, w_up, conv_w, conv_b, w_down, norm_f_g):
    assert norm1_g.shape[0] == 1, "single-layer trunk"
    p = dict(norm1_g=norm1_g[0], w_in=w_in[0], q_norm_g=q_norm_g[0], kv_norm_g=kv_norm_g[0], w_uq=w_uq[0],
             w_uk=w_uk[0], w_uv=w_uv[0], cmp_pos_k=cmp_pos_k[0], cmp_w1_k=cmp_w1_k[0], cmp_w2_k=cmp_w2_k[0],
             cmp_pos_v=cmp_pos_v[0], cmp_w1_v=cmp_w1_v[0], cmp_w2_v=cmp_w2_v[0], w_proj_mla=w_proj_mla[0],
             w_proj_nsa=w_proj_nsa[0], w_out=w_out[0], norm2_g=norm2_g[0], w_gate=w_gate[0], w_up=w_up[0],
             conv_w=conv_w[0], conv_b=conv_b[0], w_down=w_down[0], norm_f_g=norm_f_g)
    w = _prep_weights(p)
    y_p, ps = _prompt(x_prompt, w)
    caches = (cache_mla_ckv[0], cache_mla_krope[0], cache_nsa_cmp_k[0], cache_nsa_cmp_v[0], cache_nsa_slc_k[0],
              cache_nsa_slc_v[0], state_win_k[0], state_win_v[0], state_ffn_conv[0])
    y_s, ss = _sample(x_sample, caches, page_table, w)
    out = [y_p, y_s]
    for a, b in zip(ps, ss):
        out += [a, b]
    return tuple(out)
```

```python
import functools

import numpy as np
import jax
import jax.numpy as jnp
from jax import lax
from jax.experimental import pallas as pl
from jax.experimental.pallas import tpu as pltpu

MLA_HEADS = 8
MLA_D_NOPE = 64
MLA_D_ROPE = 32
MLA_D_V = 64
ROPE_THETA = 10000.0
MLA_SCALE = (MLA_D_NOPE + MLA_D_ROPE) ** -0.5
NSA_HEADS = 8
NSA_GROUPS = 2
NSA_HPG = NSA_HEADS // NSA_GROUPS
NSA_DH = 64
NSA_SCALE = NSA_DH ** -0.5
CMP_BLOCK = 32
CMP_STRIDE = 16
SLC_BLOCK = 64
SLC_TOP_N = 16
WINDOW = 512
CONV_W = 3
PAGE_SIZE = 128
EPS = 1e-6
NEG = -1e30
FORCE = 1e9

LOG2E = 1.4426950408889634
LANE = 128
AUX_POS_HI, AUX_POS_LO, AUX_BLK0 = 64, 65, 72
HEAD_PAD = 128
CHUNK_FEATS = CMP_STRIDE * NSA_GROUPS * NSA_DH
FFN_BLOCK = 1536
SLOPES =tuple(float(2.0 ** (-8.0 * (h + 1) / NSA_HEADS)) for h in range(NSA_HEADS))
VMEM_LIMIT = 56 * 1024 * 1024

F32 = jnp.float32
BF16 = jnp.bfloat16
_NT = (((1,), (1,)), ((), ()))


def _cparams(*sem):
    return pltpu.CompilerParams(dimension_semantics=sem, vmem_limit_bytes=VMEM_LIMIT)


def _rms(x, g):
    return x * lax.rsqrt(jnp.mean(x * x, axis=-1, keepdims=True) + EPS) * g


def _dot(a, b):
    return jnp.dot(a, b, preferred_element_type=F32)


def _dot_nt(a, b):
    return lax.dot_general(a, b, _NT, preferred_element_type=F32)


def _dot_exact(a, b):
    return jnp.dot(a, b, preferred_element_type=F32, precision=lax.Precision.HIGHEST)


def _iota(shape, dim):
    return lax.broadcasted_iota(jnp.int32, shape, dim)


def _log2(n):
    assert n > 0 and n & (n - 1) == 0, n
    return n.bit_length() - 1


def _vdiv(x, n):
    return lax.shift_right_logical(x, jnp.full(x.shape, _log2(n), jnp.int32))


def _vmod(x, n):
    assert n & (n - 1) == 0, n
    return x & (n - 1)


_O_CQ, _O_CKV, _O_QN, _O_K6, _O_GA = 0, 384, 640, 1152, 1920


def _inproj_kernel(x_ref, g1_ref, w_ref, qg_ref, kvg_ref, wuq_ref, wk_ref, wv_ref, cos_ref, sin_ref,
                   *outs, sample, q_rank, kv_rank, d_model, tiles_per_seq, q_scale):
    o_gb = _O_GA + d_model
    o_kr = o_gb + d_model
    x = x_ref[...]
    hn = _rms(x, g1_ref[...])
    y = _dot(hn.astype(BF16), w_ref[...])
    cosq = cos_ref[...]
    sinq = sin_ref[...]
    nq = MLA_HEADS * HEAD_PAD
    cqn = _rms(y[:, _O_CQ:_O_CQ + q_rank], qg_ref[...])
    q2 = _dot(cqn.astype(BF16), wuq_ref[...])
    half = MLA_D_ROPE // 2
    lane_h = _iota((x.shape[0], HEAD_PAD), 1)

    def rotated(qh):
        partner = jnp.where(lane_h < MLA_D_NOPE + half, -pltpu.roll(qh, HEAD_PAD - half, 1), pltpu.roll(qh, half, 1))
        return (qh * cosq + partner * sinq) * q_scale

    ckv = _rms(y[:, _O_CKV:_O_CKV + kv_rank], kvg_ref[...])
    kr = y[:, o_kr:o_kr + LANE] * cosq + y[:, o_kr + LANE:o_kr + 2 * LANE] * sinq
    ckv_b = ckv.astype(BF16)
    it = iter(outs)
    if sample:
        qrot_ref, qabs_ref = next(it), next(it)
        for h in range(MLA_HEADS):
            sl = slice(h * HEAD_PAD, (h + 1) * HEAD_PAD)
            qh = rotated(q2[:, sl])
            qrot_ref[:, sl] = qh
            qabs_ref[:, h * kv_rank:(h + 1) * kv_rank] = _dot(qh.astype(BF16), wk_ref[h])
    else:
        q_ref, k_ref, v_ref = next(it), next(it), next(it)
        knp = _dot(ckv_b, wk_ref[...])
        for h in range(MLA_HEADS):
            sl = slice(h * HEAD_PAD, (h + 1) * HEAD_PAD)
            qh = rotated(q2[:, sl])
            q_ref[:, sl] = qh.astype(BF16)
            k_ref[:, sl] = (knp[:, sl] + kr).astype(BF16)
        ones_lane = (_vmod(_iota((1, nq), 1), HEAD_PAD) == MLA_D_V).astype(F32)
        v_ref[...] = (_dot(ckv_b, wv_ref[...]) + ones_lane).astype(BF16)
    ckv_ref, kr_ref, qn_ref = next(it), next(it), next(it)
    ckv_ref[...] = ckv
    kr_ref[...] = kr
    yq = y[:, _O_QN:_O_QN + NSA_HEADS * NSA_DH] * NSA_SCALE
    if sample:
        qn_ref[...] = yq.astype(BF16)
    else:
        lane_q = _iota((x.shape[0], LANE), 1)
        for h in range(NSA_HEADS):
            pair = yq[:, (h // 2) * LANE:(h // 2 + 1) * LANE]
            if h % 2:
                pair = pltpu.roll(pair, NSA_DH, 1)
            aux_q = jnp.where((lane_q == AUX_POS_HI) | (lane_q == AUX_POS_LO), SLOPES[h], 0.0)
            qn_ref[:, h * LANE:(h + 1) * LANE] = jnp.where(lane_q < NSA_DH, pair, aux_q).astype(BF16)
    for j in range(6):
        yj = y[:, _O_K6 + j * LANE:_O_K6 + (j + 1) * LANE]
        if sample:
            next(it)[...] = yj
        else:
            next(it)[...] = yj.T
    if not sample:
        tm = x.shape[0]
        pos = (pl.program_id(0) % tiles_per_seq) * tm + _iota((tm, 1), 0)
        lane = _iota((tm, LANE), 1)
        onehot = ((lane >= AUX_BLK0) & (_vdiv(pos, SLC_BLOCK) == lane - AUX_BLK0)).astype(F32)
        aux = jnp.where(lane == AUX_POS_HI, (pos - _vmod(pos, 256)).astype(F32),
                        jnp.where(lane == AUX_POS_LO, _vmod(pos, 256).astype(F32), onehot))
        for j in range(6):
            yj = y[:, _O_K6 + j * LANE:_O_K6 + (j + 1) * LANE]
            ref = next(it)
            if j in (2, 3, 4, 5):
                tail = aux if j in (2, 4) else (lane == NSA_DH).astype(F32)
                ref[:, :LANE] = jnp.where(lane < NSA_DH, yj, tail).astype(BF16)
                ref[:, LANE:] = jnp.where(lane < NSA_DH, pltpu.roll(yj, NSA_DH, 1), tail).astype(BF16)
            else:
                ref[...] = yj.astype(BF16)
    gn_ref, ga_ref, gb_ref = next(it), next(it), next(it)
    gn_ref[...] = jax.nn.sigmoid(y[:, o_kr + 2 * LANE:o_kr + 3 * LANE])
    ga_ref[...] = jax.nn.sigmoid(y[:, _O_GA:_O_GA + d_model])
    gb_ref[...] = jax.nn.sigmoid(y[:, o_gb:o_gb + d_model])


def _inproj(x, wts, cosq, sinq, *, sample, tm, tab_blocks):
    n, d = x.shape
    q_rank, kv_rank = wts['q_norm_g'].shape[1], wts['kv_norm_g'].shape[1]
    nq = MLA_HEADS * HEAD_PAD
    wk = wts['w_ukT'] if sample else wts['w_ukp']
    row = lambda i: (i, 0)
    const2 = lambda i: (0, 0)
    tab_map = (lambda i: (i % tab_blocks, 0))
    in_specs = [
        pl.BlockSpec((tm, d), row),
        pl.BlockSpec((1, d), const2),
        pl.BlockSpec(wts['w_in'].shape, const2),
        pl.BlockSpec((1, q_rank), const2),
        pl.BlockSpec((1, kv_rank), const2),
        pl.BlockSpec(wts['w_uq2'].shape, const2),
        pl.BlockSpec(wk.shape, (lambda i: (0, 0, 0)) if sample else const2),
        pl.BlockSpec(wts['w_uvf'].shape, const2),
        pl.BlockSpec((tm, LANE), tab_map),
        pl.BlockSpec((tm, LANE), tab_map),
    ]
    shapes = []
    if sample:
        shapes += [(nq, F32), (MLA_HEADS * kv_rank, F32)]
    else:
        shapes += [(nq, BF16), (nq, BF16), (nq, BF16)]
    shapes += [(kv_rank, F32), (LANE, F32), (NSA_HEADS * (NSA_DH if sample else LANE), BF16)]
    n_lead = len(shapes)
    shapes += [(LANE, F32)] * 6
    if not sample:
        shapes += [(LANE, BF16), (LANE, BF16)] + [(2 * LANE, BF16)] * 4
        assert AUX_BLK0 + -(-tab_blocks * tm // SLC_BLOCK) <= LANE, "block one-hot must fit the aux lanes"
    shapes += [(LANE, F32), (d, F32), (d, F32)]
    out_shape = [jax.ShapeDtypeStruct((n, w), dt) for w, dt in shapes]
    out_specs = [pl.BlockSpec((tm, w), row) for w, _ in shapes]
    if not sample:
        tpb = tab_blocks
        for j in range(n_lead, n_lead + 6):
            out_shape[j] = jax.ShapeDtypeStruct((n // (tpb * tm), LANE, tpb * tm), F32)
            out_specs[j] = pl.BlockSpec((None, LANE, tm), lambda i: (i // tpb, 0, i % tpb))
    q_scale = MLA_SCALE if sample else MLA_SCALE * LOG2E
    kern = functools.partial(_inproj_kernel, sample=sample, q_rank=q_rank, kv_rank=kv_rank, d_model=d,
                             tiles_per_seq=tab_blocks, q_scale=q_scale)
    return pl.pallas_call(
        kern, grid=(n // tm,), in_specs=in_specs, out_specs=out_specs, out_shape=out_shape,
        compiler_params=_cparams("parallel"), name="inproj_sample" if sample else "inproj_prompt",
    )(x, wts['norm1_g'], wts['w_in'], wts['q_norm_g'], wts['kv_norm_g'], wts['w_uq2'], wk, wts['w_uvf'],
      cosq, sinq)


def _stack_heads(q_ref, heads, width):
    parts = [q_ref[:, h * width:(h + 1) * width] for h in heads]
    return parts[0] if len(parts) == 1 else jnp.concatenate(parts, axis=0)


def _gated(o, gate_ref, h, branch):
    return o * gate_ref[:, h * 3 + branch:h * 3 + branch + 1]


def _flash_kernel(*refs, tq, tk, groups, par, dq, dk, dv, window, has_bias, base2, branch):
    refs = list(refs)
    o_ref = refs.pop()
    gate_ref = refs.pop() if branch is not None else None
    bias_ref = refs.pop() if has_bias else None
    q_ref, k_ref, v_ref = refs
    q_start = pl.program_id(1) * tq
    n_hi = (q_start + tq - 1) // tk + 1
    hi_full = (q_start + 1) // tk
    if window:
        n_lo = jnp.maximum(q_start - (window - 1), 0) // tk
        lo_full = (jnp.maximum(q_start + tq - window, 0) + tk - 1) // tk
    else:
        n_lo, lo_full = 0, 0
    e1 = jnp.clip(lo_full, n_lo, n_hi)
    e2 = jnp.clip(hi_full, e1, n_hi)
    ex = jnp.exp2 if base2 else jnp.exp
    for c0 in range(0, len(groups), par):
        chunk = groups[c0:c0 + par]
        qs = []
        for heads, kcol in chunk:
            qg = _stack_heads(q_ref, heads, dq)
            if has_bias:
                bias = bias_ref[:, kcol * LANE:(kcol + 1) * LANE]
                qg = qg + jnp.concatenate([bias] * len(heads), axis=0)
            qs.append(qg)
        rows = qs[0].shape[0]
        qpos = q_start + (_iota((rows, 1), 0) & (tq - 1))

        def step(j, carry, masked):
            k0 = pl.multiple_of(j * tk, tk)
            if masked:
                dist = qpos - (k0 + _iota((1, tk), 1))
                mask = dist >= 0
                if window:
                    mask = mask & (dist < window)
            out = []
            for (heads, kcol), qg, (m, acc) in zip(chunk, qs, carry):
                kt = k_ref[pl.ds(k0, tk), kcol * dk:(kcol + 1) * dk]
                vt = v_ref[pl.ds(k0, tk), kcol * LANE:(kcol + 1) * LANE]
                s = _dot_nt(qg, kt)
                if masked:
                    s = jnp.where(mask, s, NEG)
                m_new = jnp.maximum(m, jnp.max(s, axis=-1, keepdims=True))
                p = ex((s - m_new).astype(BF16))
                acc = ex(m - m_new) * acc + _dot(p, vt)
                out.append((m_new, acc))
            return tuple(out)

        carry = tuple((jnp.full((rows, 1), NEG, F32), jnp.zeros((rows, LANE), F32)) for _ in chunk)
        carry = lax.fori_loop(n_lo, e1, functools.partial(step, masked=True), carry)
        carry = lax.fori_loop(e1, e2, functools.partial(step, masked=False), carry)
        carry = lax.fori_loop(e2, n_hi, functools.partial(step, masked=True), carry)
        for (heads, kcol), (m, acc) in zip(chunk, carry):
            o = acc[:, :dv] * (1.0 / acc[:, dv:dv + 1])
            for hh, h in enumerate(heads):
                oh = o[hh * tq:(hh + 1) * tq]
                if branch is not None:
                    oh = _gated(oh, gate_ref, h, branch)
                o_ref[:, h * dv:(h + 1) * dv] = oh.astype(o_ref.dtype)


def _flash(q, k, v, bias, gate, *, batch, seq, tq, tk, groups, par, dq, dk, dv, window, base2, branch, name):
    n = q.shape[0]
    has_bias = bias is not None
    n_heads = sum(len(g[0]) for g in groups)
    qrow = lambda b, i: (b * (seq // tq) + i, 0)
    kv = lambda b, i: (b, 0)
    in_specs = [pl.BlockSpec((tq, q.shape[1]), qrow), pl.BlockSpec((seq, k.shape[1]), kv),
                pl.BlockSpec((seq, v.shape[1]), kv)]
    args = [q, k, v]
    for extra in (bias, gate):
        if extra is not None:
            in_specs.append(pl.BlockSpec((tq, extra.shape[1]), qrow))
            args.append(extra)
    kern = functools.partial(_flash_kernel, tq=tq, tk=tk, groups=groups, par=par, dq=dq, dk=dk, dv=dv, window=window,
                             has_bias=has_bias, base2=base2, branch=branch if gate is not None else None)
    return pl.pallas_call(
        kern, grid=(batch, seq // tq), in_specs=in_specs,
        out_specs=pl.BlockSpec((tq, n_heads * dv), qrow),
        out_shape=jax.ShapeDtypeStruct((n, n_heads * dv), BF16),
        compiler_params=_cparams("parallel", "arbitrary"), name=name,
    )(*args)


def _win_kernel(q_ref, k_ref, v_ref, gate_ref, o_ref, *, tq, groups, dv, branch):
    i = pl.program_id(1)
    rq = _iota((tq, tq), 0)
    ck = _iota((tq, tq), 1)
    bias_own = jnp.where(ck <= rq, 0.0, NEG)
    bias_far = jnp.where(ck > rq, 0.0, NEG) + jnp.where(i >= 2, 0.0, NEG)
    bias_mid = jnp.where(i >= 1, 0.0, NEG)
    starts = (jnp.maximum(i - 2, 0) * tq, jnp.maximum(i - 1, 0) * tq, i * tq)
    for heads, kcol in groups:
        qg = _stack_heads(q_ref, heads, LANE)
        nh = len(heads)
        ss, vs = [], []
        for k0, bias in zip(starts, (bias_far, None, bias_own)):
            k0 = pl.multiple_of(k0, tq)
            s = _dot_nt(qg, k_ref[pl.ds(k0, tq), kcol * LANE:(kcol + 1) * LANE])
            s = s + (bias_mid if bias is None else jnp.concatenate([bias] * nh, axis=0))
            ss.append(s)
            vs.append(v_ref[pl.ds(k0, tq), kcol * LANE:(kcol + 1) * LANE])
        m = functools.reduce(jnp.maximum, [jnp.max(s, axis=-1, keepdims=True) for s in ss])
        acc = functools.reduce(jnp.add, [_dot(jnp.exp((s - m).astype(BF16)), v) for s, v in zip(ss, vs)])
        o = acc[:, :dv] * (1.0 / acc[:, dv:dv + 1])
        for hh, h in enumerate(heads):
            o_ref[:, h * dv:(h + 1) * dv] = _gated(o[hh * tq:(hh + 1) * tq], gate_ref, h, branch).astype(o_ref.dtype)


def _win_prompt(q, k, v, gate, *, batch, seq, tq, groups, dv, branch):
    assert WINDOW == 2 * tq and seq % tq == 0
    n = q.shape[0]
    n_heads = sum(len(g[0]) for g in groups)
    qrow = lambda b, i: (b * (seq // tq) + i, 0)
    kv = lambda b, i: (b, 0)
    return pl.pallas_call(
        functools.partial(_win_kernel, tq=tq, groups=groups, dv=dv, branch=branch), grid=(batch, seq // tq),
        in_specs=[pl.BlockSpec((tq, q.shape[1]), qrow), pl.BlockSpec((seq, k.shape[1]), kv),
                  pl.BlockSpec((seq, v.shape[1]), kv), pl.BlockSpec((tq, gate.shape[1]), qrow)],
        out_specs=pl.BlockSpec((tq, n_heads * dv), qrow),
        out_shape=jax.ShapeDtypeStruct((n, n_heads * dv), BF16),
        compiler_params=_cparams("parallel", "arbitrary"), name="win_prompt",
    )(q, k, v, gate)


def _compress_rows(xk, xv, w1k_ref, w1v_ref, posk_ref, posv_ref):
    yk = _dot(xk.astype(BF16), w1k_ref[...])
    yv = _dot(xv.astype(BF16), w1v_ref[...])
    return yk, yv


def _compress_finish(y, posy, w2_ref):
    rows = y.shape[0]
    a = y[:, :LANE]
    b = pltpu.roll(y[:, LANE:], rows - 1, 0)
    pos = posy[0:1, :LANE] + posy[1:2, LANE:]
    hid = jax.nn.gelu(a + b + pos)
    return _dot(hid.astype(BF16), w2_ref[...])


def _compress_prompt_kernel(xk_ref, xv_ref, w1k_ref, w1v_ref, pk_ref, pv_ref, w2k_ref, w2v_ref, ok_ref, ov_ref):
    yk, yv = _compress_rows(xk_ref[...], xv_ref[...], w1k_ref, w1v_ref, pk_ref, pv_ref)
    ok_ref[...] = _compress_finish(yk, _dot(pk_ref[...], w1k_ref[...]), w2k_ref).astype(BF16)
    ov_ref[...] = _compress_finish(yv, _dot(pv_ref[...], w1v_ref[...]), w2v_ref).astype(BF16)


def _compress_prompt(kc, vc, wts, *, batch, seq):
    nch = seq // CMP_STRIDE
    xk = kc.reshape(batch * nch, CHUNK_FEATS)
    xv = vc.reshape(batch * nch, CHUNK_FEATS)
    row = lambda b: (b, 0)
    c2 = lambda b: (0, 0)
    wspec = pl.BlockSpec((CHUNK_FEATS, 2 * LANE), c2)
    pspec = pl.BlockSpec((8, CHUNK_FEATS), c2)
    w2spec = pl.BlockSpec((LANE, LANE), c2)
    return pl.pallas_call(
        _compress_prompt_kernel, grid=(batch,),
        in_specs=[pl.BlockSpec((nch, CHUNK_FEATS), row), pl.BlockSpec((nch, CHUNK_FEATS), row),
                  wspec, wspec, pspec, pspec, w2spec, w2spec],
        out_specs=[pl.BlockSpec((nch, LANE), row)] * 2,
        out_shape=[jax.ShapeDtypeStruct((batch * nch, LANE), BF16)] * 2,
        compiler_params=_cparams("parallel"), name="compress_prompt",
    )(xk, xv, wts['cmp_w1k'], wts['cmp_w1v'], wts['cmp_posk'], wts['cmp_posv'], wts['cmp_w2k'], wts['cmp_w2v'])


def _overlap(n_rows, n_sel):
    c = _iota((n_rows, n_sel), 0) * CMP_STRIDE
    j = _iota((n_rows, n_sel), 1) * SLC_BLOCK
    return ((c < j + SLC_BLOCK) & (c + CMP_BLOCK > j)).astype(F32)


def _force_scores(score, cur, jj):
    forced = (jj == 0) | (jj == cur) | (jj == cur - 1)
    score = jnp.where(forced, FORCE, score)
    return jnp.where(jj <= cur, score, NEG)


def _cmp_prompt_kernel(q_ref, k_ref, v_ref, gate_ref, o_ref, bias_ref, *, tq, n_cmp, n_sel):
    q_start = pl.program_id(1) * tq
    ncp = k_ref.shape[0]
    rows = NSA_HPG * tq
    qpos = q_start + (_iota((rows, 1), 0) & (tq - 1))
    cidx = _iota((1, ncp), 1)
    dist = qpos - (cidx * CMP_STRIDE + CMP_BLOCK - 1)
    mask = (dist >= 0) & (cidx < n_cmp)
    distf = dist.astype(F32)
    hrow = _vdiv(_iota((rows, 1), 0), tq)
    nsp = -(-n_sel // 8) * 8
    cur = _vdiv(q_start + _iota((1, tq), 1), SLC_BLOCK)
    jj = _iota((nsp, tq), 0)
    cb = _iota((nsp, ncp), 1) * CMP_STRIDE
    jb = _iota((nsp, ncp), 0) * SLC_BLOCK
    ov_t = ((cb < jb + SLC_BLOCK) & (cb + CMP_BLOCK > jb)).astype(F32)
    for g in range(NSA_GROUPS):
        heads = range(g * NSA_HPG, (g + 1) * NSA_HPG)
        qg = jnp.concatenate([q_ref[:, h * LANE:h * LANE + NSA_DH] for h in heads], axis=0)
        slope = jnp.zeros((rows, 1), F32)
        for hh, h in enumerate(heads):
            slope = jnp.where(hrow == hh, SLOPES[h], slope)
        s = _dot_nt(qg, k_ref[:, g * NSA_DH:(g + 1) * NSA_DH]) - slope * distf
        s = jnp.where(mask, s, NEG)
        m = jnp.max(s, axis=-1, keepdims=True)
        p = jnp.where(mask, jnp.exp(s - m), 0.0)
        l = jnp.sum(p, axis=-1, keepdims=True)
        p = p / jnp.where(l > 0.0, l, 1.0)
        o = _dot(p.astype(BF16), v_ref[:, g * NSA_DH:(g + 1) * NSA_DH])
        imp = p[0:tq]
        for hh in range(1, NSA_HPG):
            imp = imp + p[hh * tq:(hh + 1) * tq]
        for hh, h in enumerate(heads):
            o_ref[:, h * NSA_DH:(h + 1) * NSA_DH] = _gated(o[hh * tq:(hh + 1) * tq], gate_ref, h, 0).astype(BF16)
        score = lax.dot_general(ov_t, imp, _NT, preferred_element_type=F32, precision=lax.Precision.HIGHEST)
        score = _force_scores(score, cur, jj)
        rank = jnp.zeros((nsp, tq), F32)
        for i in range(n_sel):
            ri = score[i:i + 1, :]
            beats = (ri > score) | ((ri == score) & (i < jj))
            rank = rank + beats.astype(F32)
        sel = (rank < float(min(SLC_TOP_N, n_sel))) & (jj <= cur)
        bias_t = jnp.where(sel | (jj >= n_sel), 0.0, NEG)
        bias_t = jnp.concatenate([jnp.zeros((AUX_BLK0, tq), F32), bias_t,
                                  jnp.zeros((LANE - AUX_BLK0 - nsp, tq), F32)], axis=0)
        bias_ref[:, g * LANE:(g + 1) * LANE] = bias_t.T.astype(BF16)


def _cmp_prompt(qn, kcc, vcc, gate, *, batch, seq, tq):
    n = qn.shape[0]
    nch = seq // CMP_STRIDE
    n_cmp = nch - CMP_BLOCK // CMP_STRIDE + 1
    n_sel = -(-seq // SLC_BLOCK)
    qrow = lambda b, i: (b * (seq // tq) + i, 0)
    kv = lambda b, i: (b, 0)
    kern = functools.partial(_cmp_prompt_kernel, tq=tq, n_cmp=n_cmp, n_sel=n_sel)
    return pl.pallas_call(
        kern, grid=(batch, seq // tq),
        in_specs=[pl.BlockSpec((tq, qn.shape[1]), qrow), pl.BlockSpec((nch, LANE), kv), pl.BlockSpec((nch, LANE), kv),
                  pl.BlockSpec((tq, gate.shape[1]), qrow)],
        out_specs=[pl.BlockSpec((tq, NSA_HEADS * NSA_DH), qrow), pl.BlockSpec((tq, NSA_GROUPS * LANE), qrow)],
        out_shape=[jax.ShapeDtypeStruct((n, NSA_HEADS * NSA_DH), BF16),
                   jax.ShapeDtypeStruct((n, NSA_GROUPS * LANE), BF16)],
        compiler_params=_cparams("parallel", "arbitrary"), name="cmp_prompt",
    )(qn, kcc, vcc, gate)


def _finish_kernel(*refs, period, latent, has_state, pregated):
    it = iter(refs)
    x_ref, omla_ref, ocmp_ref, oslc_ref, owin_ref = (next(it) for _ in range(5))
    gn_ref = None if pregated else next(it)
    ga_ref, gb_ref = next(it), next(it)
    prev1_ref, prev2_ref = (next(it), next(it)) if has_state else (None, None)
    gx_ref = None if pregated else next(it)
    wuv_ref = next(it) if latent else None
    (wpm_ref, wpn_ref, wo_ref, g2_ref, wg_ref, wu_ref, cw_ref, cb_ref, wd_ref, gf_ref,
     y_ref, gout_ref, carry_ref) = it
    tm = x_ref.shape[0]
    if pregated:
        o_nsa = ocmp_ref[...].astype(F32) + oslc_ref[...].astype(F32) + owin_ref[...].astype(F32)
    else:
        gn = gn_ref[...]
        gn_hi = gn.astype(BF16)
        gn_lo = (gn - gn_hi.astype(F32)).astype(BF16)
        gexp = _dot(gn_hi, gx_ref[...]) + _dot(gn_lo, gx_ref[...])
        w = NSA_HEADS * NSA_DH
        o_nsa = (gexp[:, 0:w] * ocmp_ref[...].astype(F32) + gexp[:, w:2 * w] * oslc_ref[...].astype(F32)
                 + gexp[:, 2 * w:3 * w] * owin_ref[...].astype(F32))
    o_mla = omla_ref[...].astype(BF16)
    if latent:
        o_mla = _dot(o_mla, wuv_ref[...]).astype(BF16)
    merged = (ga_ref[...] * _dot(o_mla, wpm_ref[...])
              + gb_ref[...] * _dot(o_nsa.astype(BF16), wpn_ref[...]))
    x1 = x_ref[...] + _dot(merged.astype(BF16), wo_ref[...])
    h2 = _rms(x1, g2_ref[...]).astype(BF16)
    row = _iota((tm, 1), 0)
    i = pl.program_id(0)
    t = _vmod(i * tm + row, period)
    carried = period > tm
    if carried:
        @pl.when(i == 0)
        def _():
            carry_ref[...] = jnp.zeros_like(carry_ref)
    dff = wg_ref.shape[1]
    x2 = x1
    for c0 in range(0, dff, FFN_BLOCK):
        cs = slice(c0, min(c0 + FFN_BLOCK, dff))
        g = _dot(h2, wg_ref[:, cs])
        u = _dot(h2, wu_ref[:, cs])
        g1 = pltpu.roll(g, 1, 0)
        g2 = pltpu.roll(g, 2, 0)
        if carried:
            c = carry_ref[:, cs]
            g1 = jnp.where(row == 0, c[7:8], g1)
            g2 = jnp.where(row == 0, c[6:7], jnp.where(row == 1, c[7:8], g2))
            carry_ref[:, cs] = g[tm - 8:tm]
        g1 = jnp.where(t >= 1, g1, prev1_ref[:, cs] if has_state else 0.0)
        g2 = jnp.where(t >= 2, g2, prev2_ref[:, cs] if has_state else 0.0)
        cw = cw_ref[:, cs]
        conv = cb_ref[:, cs] + cw[0:1] * g2 + cw[1:2] * g1 + cw[2:3] * g
        act = (jax.nn.silu(conv) * u).astype(BF16)
        x2 = x2 + _dot(act, wd_ref[cs, :])
        gout_ref[:, cs] = g[tm - 8:tm] if gout_ref.shape[0] == 8 else g
    y_ref[...] = _rms(x2, gf_ref[...])


def _finish(x, omla, ocmp, oslc, owin, gn, ga, gb, state_rows, wts, *, tm, period, full_g, latent):
    n, d = x.shape
    dff = wts['w_gate'].shape[1]
    row = lambda i: (i, 0)
    c2 = lambda i: (0, 0)
    pregated = gn is None
    acts = ([x, omla, ocmp, oslc, owin] + ([] if pregated else [gn]) + [ga, gb]
            + (list(state_rows) if state_rows is not None else []))
    consts = ([] if pregated else [wts['gate_expand']]) + ([wts['w_uvbd']] if latent else []) + [
        wts['w_proj_mla'], wts['w_proj_nsa'], wts['w_out'], wts['norm2_g'],
        wts['w_gate'], wts['w_up'], wts['conv_w'], wts['conv_b'], wts['w_down'], wts['norm_f_g']]
    ins = acts + consts
    in_specs = [pl.BlockSpec((tm, a.shape[1]), row) for a in acts] + [pl.BlockSpec(a.shape, c2) for a in consts]
    g_rows = n if full_g else (n // tm) * 8
    g_blk = tm if full_g else 8
    kern = functools.partial(_finish_kernel, period=period, latent=latent, has_state=state_rows is not None,
                             pregated=pregated)
    return pl.pallas_call(
        kern, grid=(n // tm,), in_specs=in_specs,
        out_specs=[pl.BlockSpec((tm, d), row), pl.BlockSpec((g_blk, dff), row)],
        out_shape=[jax.ShapeDtypeStruct((n, d), F32), jax.ShapeDtypeStruct((g_rows, dff), F32)],
        scratch_shapes=[pltpu.VMEM((8, dff), F32)],
        compiler_params=_cparams("arbitrary"), name="finish_full" if full_g else "finish_tiled",
    )(*ins)


def _swap_halves(w):
    hlf = w.shape[-1] // 2
    return jnp.concatenate([-w[..., hlf:], w[..., :hlf]], axis=-1)


def _prep_weights(p):
    d = p['w_in'].shape[0]
    q_rank, kv_rank = p['q_norm_g'].shape[-1], p['kv_norm_g'].shape[-1]
    sizes = [q_rank, kv_rank, MLA_D_ROPE, NSA_HEADS * NSA_DH] + [2 * NSA_GROUPS * NSA_DH] * 3 + [3 * NSA_HEADS, d, d]
    cuts = np.cumsum(sizes)[:-1].tolist()
    cq, ckv, kr, qn, kvc, kvs, kvw, gn, ga, gb = jnp.split(p['w_in'], cuts, axis=-1)
    assert _O_CKV == q_rank and _O_QN == q_rank + kv_rank
    lo, hi = MLA_D_NOPE, HEAD_PAD - MLA_D_NOPE - MLA_D_ROPE
    place = lambda w: jnp.pad(w, ((0, 0), (lo, hi)))
    gnp = jnp.pad(gn, ((0, 0), (0, LANE - gn.shape[1])))
    w_in = jnp.concatenate([cq, ckv, qn, kvc, kvs, kvw, ga, gb, place(kr), place(_swap_halves(kr)), gnp], axis=1)
    w = {'w_in': w_in.astype(BF16)}
    for k in ('norm1_g', 'q_norm_g', 'kv_norm_g', 'norm2_g', 'conv_b'):
        w[k] = p[k].reshape(1, -1)
    w['norm_f_g'] = p['norm_f_g'].reshape(1, -1)
    w['conv_w'] = jnp.pad(p['conv_w'], ((0, 8 - CONV_W), (0, 0)))
    uq = p['w_uq']
    w['w_uq2'] = jnp.pad(uq, ((0, 0), (0, 0), (0, HEAD_PAD - uq.shape[-1]))).reshape(q_rank, -1).astype(BF16)
    uk = p['w_uk']
    w['w_ukp'] = jnp.pad(uk, ((0, 0), (0, 0), (0, HEAD_PAD - MLA_D_NOPE))).reshape(kv_rank, -1).astype(BF16)
    w['w_ukT'] = jnp.pad(jnp.transpose(uk, (1, 2, 0)), ((0, 0), (0, HEAD_PAD - MLA_D_NOPE), (0, 0))).astype(BF16)
    w['w_uvf'] = jnp.pad(p['w_uv'], ((0, 0), (0, 0), (0, HEAD_PAD - MLA_D_V))).reshape(kv_rank, -1).astype(BF16)
    eye_h = jnp.eye(MLA_HEADS, dtype=F32)
    w['w_uvbd'] = jnp.einsum('rhv,hk->hrkv', p['w_uv'], eye_h).reshape(MLA_HEADS * kv_rank, -1).astype(BF16)
    eye_g = jnp.eye(NSA_GROUPS, dtype=F32)
    for nm in ('k', 'v'):
        w1 = p['cmp_w1_' + nm].reshape(2, CMP_STRIDE, NSA_DH, -1)
        big = jnp.einsum('ajdh,gk->jgdakh', w1, eye_g)
        w['cmp_w1' + nm] = big.reshape(CHUNK_FEATS, -1).astype(BF16)
        pos = p['cmp_pos_' + nm].reshape(2, CMP_STRIDE, 1, NSA_DH)
        pos = jnp.broadcast_to(pos, (2, CMP_STRIDE, NSA_GROUPS, NSA_DH)).reshape(2, CHUNK_FEATS)
        w['cmp_pos' + nm] = jnp.pad(pos, ((0, 6), (0, 0))).astype(BF16)
        w2 = p['cmp_w2_' + nm]
        w['cmp_w2' + nm] = jnp.einsum('hd,gk->ghkd', w2, eye_g).reshape(NSA_GROUPS * w2.shape[0], -1).astype(BF16)
    ge = np.zeros((LANE, 3 * NSA_HEADS * NSA_DH), np.float32)
    for h in range(NSA_HEADS):
        for i in range(3):
            ge[h * 3 + i, i * NSA_HEADS * NSA_DH + h * NSA_DH:i * NSA_HEADS * NSA_DH + (h + 1) * NSA_DH] = 1.0
    w['gate_expand'] = jnp.asarray(ge)
    for k in ('w_proj_mla', 'w_proj_nsa', 'w_out', 'w_gate', 'w_up', 'w_down'):
        w[k] = p[k].astype(BF16)
    return w


def _rope_tables(pos):
    inv = ROPE_THETA ** (-jnp.arange(0, MLA_D_ROPE, 2, dtype=F32) / MLA_D_ROPE)
    ang = pos.astype(F32)[:, None] * inv[None, :]
    cos, sin = jnp.cos(ang), jnp.sin(ang)
    n = pos.shape[0]
    pad = jnp.zeros((n, HEAD_PAD - MLA_D_NOPE - MLA_D_ROPE), F32)
    cosq = jnp.concatenate([jnp.ones((n, MLA_D_NOPE), F32), cos, cos, pad], axis=1)
    sinq = jnp.concatenate([jnp.zeros((n, MLA_D_NOPE), F32), sin, sin, pad], axis=1)
    return cosq, sinq


_NSA_GROUPS_SPEC = tuple((tuple(range(g * NSA_HPG, (g + 1) * NSA_HPG)), g) for g in range(NSA_GROUPS))
_MLA_GROUPS_SPEC = tuple(((h,), h) for h in range(MLA_HEADS))


def _prompt(x_prompt, w):
    b, t, d = x_prompt.shape
    n = b * t
    x = x_prompt.reshape(n, d)
    tm = 256
    cosq, sinq = _rope_tables(jnp.arange(t, dtype=jnp.int32))
    (q_mla, k_mla, v_mla, ckv, krp, qn, kc, vc, ks, vs, kw, vw, kc_b, vc_b, ks_b, vs_b, kw_b, vw_b, gn, ga, gb) = _inproj(
        x, w, cosq, sinq, sample=False, tm=tm, tab_blocks=t // tm)
    o_mla = _flash(q_mla, k_mla, v_mla, None, None, batch=b, seq=t, tq=512, tk=512, groups=_MLA_GROUPS_SPEC, par=8,
                   dq=HEAD_PAD, dk=HEAD_PAD, dv=MLA_D_V, window=0, base2=True, branch=None, name="mla_prompt")
    kcc, vcc = _compress_prompt(kc_b, vc_b, w, batch=b, seq=t)
    o_cmp, sel_bias = _cmp_prompt(qn, kcc, vcc, gn, batch=b, seq=t, tq=128)
    o_slc = _flash(qn, ks_b, vs_b, sel_bias, gn, batch=b, seq=t, tq=256, tk=512, groups=_NSA_GROUPS_SPEC, par=2,
                   dq=LANE, dk=LANE, dv=NSA_DH, window=0, base2=False, branch=1, name="slc_prompt")
    o_win = _win_prompt(qn, kw_b, vw_b, gn, batch=b, seq=t, tq=WINDOW // 2, groups=_NSA_GROUPS_SPEC, dv=NSA_DH,
                        branch=2)
    dff = w['w_gate'].shape[1]
    y, gtail = _finish(x, o_mla, o_cmp, o_slc, o_win, None, ga, gb, None, w, tm=tm, period=t, full_g=False,
                       latent=False)
    kv4 = lambda a: a.reshape(1, b, NSA_GROUPS, NSA_DH, a.shape[-1]).transpose(0, 1, 4, 2, 3)
    n_keep = min(WINDOW, t)
    kw, vw = kw[:, :, t - n_keep:], vw[:, :, t - n_keep:]
    conv_state = gtail.reshape(b, t // tm, 8, dff)[:, -1, 8 - (CONV_W - 1):, :]
    states = (ckv.reshape(1, b, t, -1), krp[:, MLA_D_NOPE:MLA_D_NOPE + MLA_D_ROPE].reshape(1, b, t, MLA_D_ROPE),
              kv4(kc), kv4(vc), kv4(ks), kv4(vs), kv4(kw), kv4(vw), conv_state[None])
    return y.reshape(b, t, d), states


def _page_copies(pt_ref, pools, bufs, sems, step, slot, pp):
    copies = []
    for k in range(pp):
        page = pt_ref[step * pp + k]
        for pool, buf, sem in zip(pools, bufs, sems):
            copies.append(pltpu.make_async_copy(pool.at[page], buf.at[slot, k], sem.at[slot]))
    return copies


def _start_all(copies, n_pools):
    for i, c in enumerate(copies):
        c.start(priority=(i // n_pools) % 2)


def _stream_pages(pt_ref, pools, bufs, sems, pp):
    step = pl.program_id(0) * pl.num_programs(1) + pl.program_id(1)
    total = pl.num_programs(0) * pl.num_programs(1)
    slot = step % 2

    @pl.when(step == 0)
    def _():
        _start_all(_page_copies(pt_ref, pools, bufs, sems, step, slot, pp), len(pools))

    @pl.when(step + 1 < total)
    def _():
        _start_all(_page_copies(pt_ref, pools, bufs, sems, step + 1, 1 - slot, pp), len(pools))

    for c in _page_copies(pt_ref, pools, bufs, sems, step, slot, pp):
        c.wait()
    return slot


def _softmax_update(sc, v, m_scr, l_scr, acc_scr, v_transposed=False):
    m_old = m_scr[...]
    m_new = jnp.maximum(m_old, jnp.max(sc, axis=-1, keepdims=True))
    p = jnp.exp(sc - m_new)
    alpha = jnp.exp(m_old - m_new)
    l_scr[...] = alpha * l_scr[...] + jnp.sum(p, axis=-1, keepdims=True)
    pv = _dot_nt(p.astype(BF16), v) if v_transposed else _dot(p.astype(BF16), v)
    acc_scr[...] = alpha * acc_scr[...] + pv
    m_scr[...] = m_new


def _mla_decode_step(slot, q_ref, knew_ref, o_ref, cbuf, rbuf, kscr, krscr, m_scr, l_scr, acc_scr, *, pp, td, kv_rank):
    s = pl.program_id(1)
    q = q_ref[...]
    rows = q.shape[0]

    @pl.when(s == 0)
    def _():
        m_scr[...] = jnp.full_like(m_scr, NEG)
        l_scr[...] = jnp.zeros_like(l_scr)
        acc_scr[...] = jnp.zeros_like(acc_scr)
        kn = knew_ref[...]
        trow = _vdiv(_iota((rows, 1), 0), MLA_HEADS)
        col = _iota((1, kn.shape[0]), 1)
        sc = jnp.where((col <= trow) & (col < td), _dot_nt(q, kn), NEG)
        _softmax_update(sc, kn[:, :kv_rank], m_scr, l_scr, acc_scr)

    for k in range(pp):
        kscr[k * PAGE_SIZE:(k + 1) * PAGE_SIZE, :] = cbuf[slot, k].astype(BF16)
        krscr[:, k * PAGE_SIZE:(k + 1) * PAGE_SIZE] = rbuf[slot, k].astype(BF16)
    kt = kscr[...]
    sc = _dot_nt(q[:, :kv_rank], kt) + _dot(q[:, kv_rank:kv_rank + MLA_D_ROPE], krscr[...])
    _softmax_update(sc, kt, m_scr, l_scr, acc_scr)

    @pl.when(s == pl.num_programs(1) - 1)
    def _():
        o_ref[...] = acc_scr[...] / l_scr[...]


def _alibi_rows(rows, td):
    r = _iota((rows, 1), 0)
    return _vdiv(r, td), _vmod(r, td)


def _slope_rows(hrow, g):
    slope = jnp.zeros(hrow.shape, F32)
    for hh in range(NSA_HPG):
        slope = jnp.where(hrow == hh, SLOPES[g * NSA_HPG + hh], slope)
    return slope


def _cmp_decode_step(slot, q_ref, w1k_ref, w1v_ref, pk_ref, pv_ref, w2k_ref, w2v_ref, o_ref, score_ref,
                     kbuf, vbuf, kp0, kp1, vp0, vp1, yk_scr, yv_scr, *, pp, td, past, n_cmp, n_sel_pad):
    s = pl.program_id(1)
    cpp = PAGE_SIZE // CMP_STRIDE
    hp = pp // 2
    for half, (kp_scr, vp_scr) in enumerate(((kp0, vp0), (kp1, vp1))):
        for k in range(hp):
            kp_scr[k * PAGE_SIZE:(k + 1) * PAGE_SIZE, :] = kbuf[slot, half * hp + k].T
            vp_scr[k * PAGE_SIZE:(k + 1) * PAGE_SIZE, :] = vbuf[slot, half * hp + k].T
    chunk_rows = lambda scr: jnp.concatenate(
        [scr[pl.ds(j, hp * cpp, stride=CMP_STRIDE), :] for j in range(CMP_STRIDE)], axis=1).astype(BF16)
    for half, (kp_scr, vp_scr) in enumerate(((kp0, vp0), (kp1, vp1))):
        r0 = pl.multiple_of(s * (pp * cpp) + half * (hp * cpp), hp * cpp)
        yk_scr[pl.ds(r0, hp * cpp), :] = _dot(chunk_rows(kp_scr), w1k_ref[...])
        yv_scr[pl.ds(r0, hp * cpp), :] = _dot(chunk_rows(vp_scr), w1v_ref[...])

    @pl.when(s == pl.num_programs(1) - 1)
    def _():
        kcc = _compress_finish(yk_scr[...], _dot(pk_ref[...], w1k_ref[...]), w2k_ref).astype(BF16)
        vcc = _compress_finish(yv_scr[...], _dot(pv_ref[...], w1v_ref[...]), w2v_ref).astype(BF16)
        ncp = kcc.shape[0]
        rows = NSA_HPG * td
        hrow, trow = _alibi_rows(rows, td)
        cidx = _iota((1, ncp), 1)
        dist = (past + trow) - (cidx * CMP_STRIDE + CMP_BLOCK - 1)
        mask = (dist >= 0) & (cidx < n_cmp)
        distf = dist.astype(F32)
        tsum = (_vmod(_iota((8, rows), 1), td) == _iota((8, rows), 0)).astype(F32)
        ov = _overlap(ncp, n_sel_pad)
        t8 = _iota((8, 1), 0)
        cur = _vdiv(past + t8, SLC_BLOCK)
        jj = _iota((8, n_sel_pad), 1)
        for g in range(NSA_GROUPS):
            sc = _dot_nt(q_ref[g], kcc) - _slope_rows(hrow, g) * distf
            sc = jnp.where(mask, sc, NEG)
            m = jnp.max(sc, axis=-1, keepdims=True)
            p = jnp.where(mask, jnp.exp(sc - m), 0.0)
            l = jnp.sum(p, axis=-1, keepdims=True)
            p = p / jnp.where(l > 0.0, l, 1.0)
            o_ref[g] = _dot(p.astype(BF16), vcc)
            imp = _dot_exact(tsum, p)
            score_ref[g] = _force_scores(_dot_exact(imp, ov), cur, jj)


def _mla_cmp_decode_kernel(pt_ref, qm_ref, knew_ref, qc_ref, w1k_ref, w1v_ref, pk_ref, pv_ref, w2k_ref, w2v_ref,
                           ckv_hbm, kr_hbm, ck_hbm, cv_hbm, olat_ref, ocmp_ref, score_ref,
                           cbuf, rbuf, kbuf, vbuf, csem, rsem, ksem, vsem,
                           kscr, krscr, m_scr, l_scr, acc_scr, kp0, kp1, vp0, vp1, yk_scr, yv_scr,
                           *, pp, td, kv_rank, past, n_cmp, n_sel_pad):
    slot = _stream_pages(pt_ref, (ckv_hbm, kr_hbm, ck_hbm, cv_hbm), (cbuf, rbuf, kbuf, vbuf),
                         (csem, rsem, ksem, vsem), pp)
    _mla_decode_step(slot, qm_ref, knew_ref, olat_ref, cbuf, rbuf, kscr, krscr, m_scr, l_scr, acc_scr,
                     pp=pp, td=td, kv_rank=kv_rank)
    _cmp_decode_step(slot, qc_ref, w1k_ref, w1v_ref, pk_ref, pv_ref, w2k_ref, w2v_ref, ocmp_ref, score_ref,
                     kbuf, vbuf, kp0, kp1, vp0, vp1, yk_scr, yv_scr, pp=pp, td=td, past=past, n_cmp=n_cmp,
                     n_sel_pad=n_sel_pad)


def _mla_cmp_decode(pt, qd_mla, knew, qd, ckv_pool, kr_pool, k_pool, v_pool, wts, *, td, pp, past, n_sel_pad):
    bd, mrows, qw = qd_mla.shape
    kv_rank = ckv_pool.shape[-1]
    n_pages = pt.shape[0] // bd
    cpp = PAGE_SIZE // CMP_STRIDE
    nch = n_pages * cpp
    n_cmp = (past + td) // CMP_STRIDE - CMP_BLOCK // CMP_STRIDE + 1
    assert (past + td) // CMP_STRIDE == nch, "new rows must not complete a chunk"
    per_b3 = lambda b, s, pt: (b, 0, 0)
    per_b = lambda b, s, pt: (b, 0, 0, 0)
    c2 = lambda b, s, pt: (0, 0)
    rows = qd.shape[2]
    in_specs = ([pl.BlockSpec((None, mrows, qw), per_b3), pl.BlockSpec((None,) + knew.shape[1:], per_b3),
                 pl.BlockSpec((None,) + qd.shape[1:], per_b)]
                + [pl.BlockSpec((CHUNK_FEATS, 2 * LANE), c2) for _ in range(2)]
                + [pl.BlockSpec((8, CHUNK_FEATS), c2) for _ in range(2)]
                + [pl.BlockSpec((LANE, LANE), c2) for _ in range(2)]
                + [pl.BlockSpec(memory_space=pl.ANY) for _ in range(4)])
    kern = functools.partial(_mla_cmp_decode_kernel, pp=pp, td=td, kv_rank=kv_rank, past=past, n_cmp=n_cmp,
                             n_sel_pad=n_sel_pad)
    pools = (ckv_pool, kr_pool, k_pool, v_pool)
    return pl.pallas_call(
        kern,
        grid_spec=pltpu.PrefetchScalarGridSpec(
            num_scalar_prefetch=1, grid=(bd, n_pages // pp), in_specs=in_specs,
            out_specs=[pl.BlockSpec((None, mrows, kv_rank), per_b3),
                       pl.BlockSpec((None, NSA_GROUPS, rows, LANE), per_b),
                       pl.BlockSpec((None, NSA_GROUPS, 8, n_sel_pad), per_b)],
            scratch_shapes=[pltpu.VMEM((2, pp) + p.shape[1:], F32) for p in pools]
            + [pltpu.SemaphoreType.DMA((2,)) for _ in pools]
            + [pltpu.VMEM((pp * PAGE_SIZE, kv_rank), BF16), pltpu.VMEM((MLA_D_ROPE, pp * PAGE_SIZE), BF16),
               pltpu.VMEM((mrows, 1), F32), pltpu.VMEM((mrows, 1), F32), pltpu.VMEM((mrows, kv_rank), F32)]
            + [pltpu.VMEM((pp // 2 * PAGE_SIZE, LANE), F32) for _ in range(4)]
            + [pltpu.VMEM((nch, 2 * LANE), F32) for _ in range(2)]),
        out_shape=[jax.ShapeDtypeStruct((bd, mrows, kv_rank), F32),
                   jax.ShapeDtypeStruct((bd, NSA_GROUPS, rows, LANE), F32),
                   jax.ShapeDtypeStruct((bd, NSA_GROUPS, 8, n_sel_pad), F32)],
        compiler_params=_cparams("arbitrary", "arbitrary"), name="mla_cmp_decode",
    )(pt, qd_mla, knew, qd, wts['cmp_w1k'], wts['cmp_w1v'], wts['cmp_posk'], wts['cmp_posv'],
      wts['cmp_w2k'], wts['cmp_w2v'], *pools)


def _rank_kernel(score_ref, cur_ref, sel_ref, *, n_sel):
    sc = score_ref[...]
    jj = _iota(sc.shape, 0)

    def body(i, rank):
        ri = score_ref[pl.ds(i, 1), :]
        beats = (ri > sc) | ((ri == sc) & (i < jj))
        return rank + beats.astype(F32)

    rank = lax.fori_loop(0, n_sel, body, jnp.zeros(sc.shape, F32))
    sel = (rank < float(min(SLC_TOP_N, n_sel))) & (jj <= cur_ref[...])
    sel_ref[...] = sel.astype(F32)


def _rank(score_t, cur, *, n_sel):
    full = lambda a: pl.BlockSpec(a.shape, lambda: (0,) * a.ndim)
    return pl.pallas_call(
        functools.partial(_rank_kernel, n_sel=n_sel), in_specs=[full(score_t), full(cur)],
        out_specs=full(score_t), out_shape=jax.ShapeDtypeStruct(score_t.shape, F32), name="rank_decode",
    )(score_t, cur)


def _slc_decode_kernel(pt_ref, cnt_ref, q_ref, sel_ref, selnew_ref, knew_ref, vnew_ref, kpos_ref, expand_ref,
                       k_hbm, v_hbm, o_ref, kbuf, vbuf, ksem, vsem, kscr, vscr, m_scr, l_scr, acc_scr,
                       *, pp, td, past):
    b, s, ns = pl.program_id(0), pl.program_id(1), pl.num_programs(1)
    step = b * ns + s
    slot = step % 2
    pools, bufs, sems = (k_hbm, v_hbm), (kbuf, vbuf), (ksem, vsem)
    active = s * pp < cnt_ref[b]
    wrap = s + 1 == ns
    nb = jnp.minimum(jnp.where(wrap, b + 1, b), pl.num_programs(0) - 1)
    next_active = (step + 1 < pl.num_programs(0) * ns) & (jnp.where(wrap, 0, s + 1) * pp < cnt_ref[nb])

    @pl.when((step == 0) & active)
    def _():
        _start_all(_page_copies(pt_ref, pools, bufs, sems, step, slot, pp), len(pools))

    @pl.when(next_active)
    def _():
        _start_all(_page_copies(pt_ref, pools, bufs, sems, step + 1, 1 - slot, pp), len(pools))

    rows = q_ref.shape[1]
    hrow, trow = _alibi_rows(rows, td)

    @pl.when(s == 0)
    def _():
        m_scr[...] = jnp.full_like(m_scr, NEG)
        l_scr[...] = jnp.zeros_like(l_scr)
        acc_scr[...] = jnp.zeros_like(acc_scr)
        kn = knew_ref[...]
        vn = vnew_ref[...]
        col = _iota((1, kn.shape[0]), 1)
        dist = trow - col
        for g in range(NSA_GROUPS):
            sc = _dot_nt(q_ref[g], kn) - _slope_rows(hrow, g) * dist.astype(F32)
            mask = (dist >= 0) & (col < td) & (selnew_ref[g] > 0.5)
            _softmax_update(jnp.where(mask, sc, NEG), vn, m_scr.at[g], l_scr.at[g], acc_scr.at[g])

    @pl.when(active)
    def _():
        for c in _page_copies(pt_ref, pools, bufs, sems, step, slot, pp):
            c.wait()
        for k in range(pp):
            kscr[:, k * PAGE_SIZE:(k + 1) * PAGE_SIZE] = kbuf[slot, k].astype(BF16)
            vscr[:, k * PAGE_SIZE:(k + 1) * PAGE_SIZE] = vbuf[slot, k].astype(BF16)
        kt = kscr[...]
        vt = vscr[...]
        distf = (past + trow).astype(F32) - kpos_ref[...]
        for g in range(NSA_GROUPS):
            sc = _dot(q_ref[g], kt) - _slope_rows(hrow, g) * distf
            selx = _dot(sel_ref[g], expand_ref[...]) > 0.5
            _softmax_update(jnp.where(selx, sc, NEG), vt, m_scr.at[g], l_scr.at[g], acc_scr.at[g], v_transposed=True)

    @pl.when(s == ns - 1)
    def _():
        o_ref[...] = acc_scr[...] / l_scr[...]


def _slc_decode(pt, counts, qd, sel16, selnew, knew, vnew, kpos, k_pool, v_pool, *, td, pp, past):
    bd = qd.shape[0]
    rows = qd.shape[2]
    n_pages = pt.shape[0] // bd
    tk = pp * PAGE_SIZE
    expand = (jnp.arange(tk)[None, :] // SLC_BLOCK == jnp.arange(tk // SLC_BLOCK)[:, None]).astype(BF16)
    per_b = lambda b, s, pt, cnt: (b, 0, 0, 0)
    per_b3 = lambda b, s, pt, cnt: (b, 0, 0)
    in_specs = ([pl.BlockSpec((None,) + qd.shape[1:], per_b),
                 pl.BlockSpec((None, None) + sel16.shape[2:], lambda b, s, pt, cnt: (b, s, 0, 0, 0)),
                 pl.BlockSpec((None,) + selnew.shape[1:], per_b),
                 pl.BlockSpec((None,) + knew.shape[1:], per_b3), pl.BlockSpec((None,) + vnew.shape[1:], per_b3),
                 pl.BlockSpec((None, None, 1, tk), lambda b, s, pt, cnt: (b, s, 0, 0)),
                 pl.BlockSpec(expand.shape, lambda b, s, pt, cnt: (0, 0))]
                + [pl.BlockSpec(memory_space=pl.ANY) for _ in range(2)])
    kern = functools.partial(_slc_decode_kernel, pp=pp, td=td, past=past)
    return pl.pallas_call(
        kern,
        grid_spec=pltpu.PrefetchScalarGridSpec(
            num_scalar_prefetch=2, grid=(bd, n_pages // pp), in_specs=in_specs,
            out_specs=pl.BlockSpec((None, NSA_GROUPS, rows, LANE), per_b),
            scratch_shapes=[pltpu.VMEM((2, pp) + k_pool.shape[1:], F32) for _ in range(2)]
            + [pltpu.SemaphoreType.DMA((2,)) for _ in range(2)]
            + [pltpu.VMEM((LANE, pp * PAGE_SIZE), BF16) for _ in range(2)]
            + [pltpu.VMEM((NSA_GROUPS, rows, 1), F32) for _ in range(2)] + [pltpu.VMEM((NSA_GROUPS, rows, LANE), F32)]),
        out_shape=jax.ShapeDtypeStruct((bd, NSA_GROUPS, rows, LANE), F32),
        compiler_params=_cparams("arbitrary", "arbitrary"), name="slc_decode",
    )(pt, counts, qd, sel16, selnew, knew, vnew, kpos, expand, k_pool, v_pool)


def _win_decode_kernel(q_ref, kst_ref, vst_ref, knew_ref, vnew_ref, o_ref, *, td, past):
    rows = q_ref.shape[1]
    hrow, trow = _alibi_rows(rows, td)
    nbuf = kst_ref.shape[1]
    kst = kst_ref[...].astype(BF16)
    vst = vst_ref[...].astype(BF16)
    kn = knew_ref[...]
    vn = vnew_ref[...]
    kpos = past - nbuf + _iota((1, nbuf), 1)
    d1 = (past + trow) - kpos
    m1 = (d1 >= 0) & (d1 < WINDOW) & (kpos >= 0)
    col = _iota((1, kn.shape[0]), 1)
    d2 = trow - col
    m2 = (d2 >= 0) & (d2 < WINDOW) & (col < td)
    for g in range(NSA_GROUPS):
        slope = _slope_rows(hrow, g)
        s1 = jnp.where(m1, _dot(q_ref[g], kst) - slope * d1.astype(F32), NEG)
        s2 = jnp.where(m2, _dot_nt(q_ref[g], kn) - slope * d2.astype(F32), NEG)
        m = jnp.maximum(jnp.max(s1, axis=-1, keepdims=True), jnp.max(s2, axis=-1, keepdims=True))
        p1 = jnp.exp(s1 - m)
        p2 = jnp.exp(s2 - m)
        l = jnp.sum(p1, axis=-1, keepdims=True) + jnp.sum(p2, axis=-1, keepdims=True)
        o_ref[g] = (_dot_nt(p1.astype(BF16), vst) + _dot(p2.astype(BF16), vn)) / l


def _win_decode(qd, kst, vst, knew, vnew, *, td, past):
    bd = qd.shape[0]
    rows = qd.shape[2]
    per_b = lambda b: (b, 0, 0, 0)
    per_b3 = lambda b: (b, 0, 0)
    blk3 = lambda a: pl.BlockSpec((None,) + a.shape[1:], per_b3)
    return pl.pallas_call(
        functools.partial(_win_decode_kernel, td=td, past=past), grid=(bd,),
        in_specs=[pl.BlockSpec((None,) + qd.shape[1:], per_b), blk3(kst), blk3(vst), blk3(knew), blk3(vnew)],
        out_specs=pl.BlockSpec((None, NSA_GROUPS, rows, LANE), per_b),
        out_shape=jax.ShapeDtypeStruct((bd, NSA_GROUPS, rows, LANE), F32),
        compiler_params=_cparams("parallel"), name="win_decode",
    )(qd, kst, vst, knew, vnew)


def _pad_rows(a, rows):
    return jnp.pad(a, ((0, 0), (0, rows - a.shape[1]), (0, 0)))


def _sample(x_sample, caches, page_table, w):
    (c_ckv, c_krope, c_cmp_k, c_cmp_v, c_slc_k, c_slc_v, s_win_k, s_win_v, s_conv) = caches
    bd, td, d = x_sample.shape
    n = bd * td
    n_pages = page_table.shape[1]
    past = n_pages * PAGE_SIZE
    n_pool = c_ckv.shape[0]
    kv_rank = c_ckv.shape[-1]
    x = x_sample.reshape(n, d)
    pos = past + jnp.arange(td, dtype=jnp.int32)
    cosq, sinq = _rope_tables(jnp.tile(pos, bd))
    (qrot, qabs, ckv, krp, qn, kc, vc, ks, vs, kw, vw, gn, ga, gb) = _inproj(
        x, w, cosq, sinq, sample=True, tm=n, tab_blocks=1)
    pt = page_table.reshape(-1)
    krope = krp[:, MLA_D_NOPE:MLA_D_NOPE + MLA_D_ROPE]
    qrope = qrot.reshape(bd, td, MLA_HEADS, HEAD_PAD)[..., MLA_D_NOPE:MLA_D_NOPE + MLA_D_ROPE]
    qpad = LANE - MLA_D_ROPE
    qd_mla = jnp.concatenate([qabs.reshape(bd, td, MLA_HEADS, kv_rank), qrope,
                              jnp.zeros((bd, td, MLA_HEADS, qpad), F32)], axis=-1)
    qd_mla = qd_mla.reshape(bd, td * MLA_HEADS, kv_rank + LANE).astype(BF16)
    knew = jnp.concatenate([ckv, krope, jnp.zeros((n, qpad), F32)], axis=-1).reshape(bd, td, -1)
    knew = _pad_rows(knew, 8).astype(BF16)
    pos_minor = lambda c: jnp.moveaxis(c, 1, -1).reshape(c.shape[0], -1, c.shape[1])
    qg = qn.astype(F32).reshape(bd, td, NSA_GROUPS, NSA_HPG, NSA_DH).transpose(0, 2, 3, 1, 4)
    qg = qg.reshape(bd, NSA_GROUPS, NSA_HPG * td, NSA_DH)
    lane_g = (jnp.arange(LANE) // NSA_DH)[None, :] == jnp.arange(NSA_GROUPS)[:, None]
    qd = jnp.where(lane_g[None, :, None, :], jnp.tile(qg, (1, 1, 1, NSA_GROUPS)), 0.0).astype(BF16)
    n_sel = -(-(past + td) // SLC_BLOCK)
    n_sel_pad = -(-n_sel // LANE) * LANE
    o_lat, o_cmp, score = _mla_cmp_decode(pt, qd_mla, knew, qd, c_ckv, pos_minor(c_krope), pos_minor(c_cmp_k),
                                          pos_minor(c_cmp_v), w, td=td, pp=min(32, n_pages), past=past,
                                          n_sel_pad=n_sel_pad)
    cur = jnp.broadcast_to((pos // SLC_BLOCK).astype(jnp.int32), (bd, NSA_GROUPS, td)).reshape(1, -1)
    n_sel8 = -(-n_sel // 8) * 8
    sel_t = _rank(score[:, :, :td, :n_sel8].reshape(-1, n_sel8).T, cur, n_sel=n_sel)
    sel = jnp.pad(sel_t.T, ((0, 0), (0, n_sel_pad - n_sel8))).reshape(bd, NSA_GROUPS, td, n_sel_pad)
    pp_slc = min(32, n_pages)
    bpp = PAGE_SIZE // SLC_BLOCK
    n_past_blk = past // SLC_BLOCK
    sel_pg = sel[..., :n_past_blk].reshape(bd, NSA_GROUPS * td, n_pages, bpp)
    need_page = sel_pg.max(axis=(1, 3)) > 0.5
    counts = need_page.sum(axis=-1).astype(jnp.int32)
    n_before = jnp.cumsum(need_page.astype(jnp.int32), axis=-1)
    u_before = jnp.cumsum(1 - need_page.astype(jnp.int32), axis=-1)
    slot_of = jnp.where(need_page, n_before - 1, counts[:, None] + u_before - 1)
    perm = slot_of[:, None, :] == jnp.arange(n_pages, dtype=jnp.int32)[None, :, None]
    order = jnp.sum(jnp.where(perm, jnp.arange(n_pages, dtype=jnp.int32), 0), axis=-1)
    pt_slc = jnp.sum(jnp.where(perm, page_table[:, None, :], 0), axis=-1).reshape(-1)
    sel_listed = jnp.einsum('bip,bxpc->bxic', perm.astype(BF16), sel_pg.astype(BF16), preferred_element_type=F32)
    bps = pp_slc * bpp
    sel_steps = sel_listed.reshape(bd, NSA_GROUPS, td, n_pages // pp_slc, bps).transpose(0, 3, 1, 2, 4)
    sel16 = jnp.tile(sel_steps, (1, 1, 1, NSA_HPG, 1)).astype(BF16)
    kpos = (order[:, :, None] * PAGE_SIZE + jnp.arange(PAGE_SIZE, dtype=jnp.int32)).astype(F32)
    kpos = kpos.reshape(bd, n_pages // pp_slc, 1, pp_slc * PAGE_SIZE)
    selnew = jnp.tile(jnp.broadcast_to(sel[..., n_past_blk:n_past_blk + 1], (bd, NSA_GROUPS, td, 8)), (1, 1, NSA_HPG, 1))
    new8 = lambda a: _pad_rows(a.reshape(bd, td, LANE), 8).astype(BF16)
    o_slc = _slc_decode(pt_slc, counts, qd, sel16, selnew, new8(ks), new8(vs), kpos, pos_minor(c_slc_k),
                        pos_minor(c_slc_v), td=td, pp=pp_slc, past=past)
    nbuf = s_win_k.shape[1]
    o_win = _win_decode(qd, pos_minor(s_win_k), pos_minor(s_win_v), new8(kw), new8(vw),
                        td=td, past=past)

    def heads_out(o):
        o = o.reshape(bd, NSA_GROUPS, NSA_HPG, td, NSA_GROUPS, NSA_DH)
        o = jnp.stack([o[:, g, :, :, g] for g in range(NSA_GROUPS)], axis=1)
        return o.transpose(0, 3, 1, 2, 4).reshape(n, NSA_HEADS * NSA_DH)

    dff = w['w_gate'].shape[1]
    prev1 = jnp.zeros((bd, td, dff), F32).at[:, 0].set(s_conv[:, 1])
    prev2 = jnp.zeros((bd, td, dff), F32).at[:, 0].set(s_conv[:, 0]).at[:, 1].set(s_conv[:, 1])
    y, g = _finish(x, o_lat.reshape(n, MLA_HEADS * kv_rank), heads_out(o_cmp), heads_out(o_slc), heads_out(o_win),
                   gn, ga, gb, (prev1.reshape(n, dff), prev2.reshape(n, dff)), w, tm=n, period=td, full_g=True,
                   latent=True)
    kv4 = lambda a: a.reshape(1, bd, td, NSA_GROUPS, NSA_DH)
    win = lambda st, new: jnp.concatenate([st, new.reshape(bd, td, NSA_GROUPS, NSA_DH)], axis=1)[None, :, -nbuf:]
    conv_state = jnp.concatenate([s_conv, g.reshape(bd, td, dff)], axis=1)[None, :, -(CONV_W - 1):]
    states = (ckv.reshape(1, bd, td, kv_rank), krope.reshape(1, bd, td, MLA_D_ROPE), kv4(kc), kv4(vc), kv4(ks), kv4(vs),
              win(s_win_k, kw), win(s_win_v, vw), conv_state)
    return y.reshape(bd, td, d), states


def kernel(x_prompt, x_sample, cache_mla_ckv, cache_mla_krope, cache_nsa_cmp_k, cache_nsa_cmp_v, cache_nsa_slc_k, cache_nsa_slc_v, state_win_k, state_win_v, state_ffn_conv, page_table, norm1_g, w_in, q_norm_g, kv_norm_g, w_uq, w_uk, w_uv, cmp_pos_k, cmp_w1_k, cmp_w2_k, cmp_pos_v, cmp_w1_v, cmp_w2_v, w_proj_mla, w_proj_nsa, w_out, norm2_g, w_gate, w_up, conv_w, conv_b, w_down, norm_f_g):
    assert norm1_g.shape[0] == 1, "single-layer trunk"
    p = dict(norm1_g=norm1_g[0], w_in=w_in[0], q_norm_g=q_norm_g[0], kv_norm_g=kv_norm_g[0], w_uq=w_uq[0],
             w_uk=w_uk[0], w_uv=w_uv[0], cmp_pos_k=cmp_pos_k[0], cmp_w1_k=cmp_w1_k[0], cmp_w2_k=cmp_w2_k[0],
             cmp_pos_v=cmp_pos_v[0], cmp_w1_v=cmp_w1_v[0], cmp_w2_v=cmp_w2_v[0], w_proj_mla=w_proj_mla[0],
             w_proj_nsa=w_proj_nsa[0], w_out=w_out[0], norm2_g=norm2_g[0], w_gate=w_gate[0], w_up=w_up[0],
             conv_w=conv_w[0], conv_b=conv_b[0], w_down=w_down[0], norm_f_g=norm_f_g)
    w = _prep_weights(p)
    y_p, ps = _prompt(x_prompt, w)
    caches = (cache_mla_ckv[0], cache_mla_krope[0], cache_nsa_cmp_k[0], cache_nsa_cmp_v[0], cache_nsa_slc_k[0],
              cache_nsa_slc_v[0], state_win_k[0], state_win_v[0], state_ffn_conv[0])
    y_s, ss = _sample(x_sample, caches, page_table, w)
    out = [y_p, y_s]
    for a, b in zip(ps, ss):
        out += [a, b]
    return tuple(out)
```

```python
import functools

import numpy as np
import jax
import jax.numpy as jnp
from jax import lax
from jax.experimental import pallas as pl
from jax.experimental.pallas import tpu as pltpu

MLA_HEADS = 8
MLA_D_NOPE = 64
MLA_D_ROPE = 32
MLA_D_V = 64
ROPE_THETA = 10000.0
MLA_SCALE = (MLA_D_NOPE + MLA_D_ROPE) ** -0.5
NSA_HEADS = 8
NSA_GROUPS = 2
NSA_HPG = NSA_HEADS // NSA_GROUPS
NSA_DH = 64
NSA_SCALE = NSA_DH ** -0.5
CMP_BLOCK = 32
CMP_STRIDE = 16
SLC_BLOCK = 64
SLC_TOP_N = 16
WINDOW = 512
CONV_W = 3
PAGE_SIZE = 128
EPS = 1e-6
NEG = -1e30
FORCE = 1e9

LOG2E = 1.4426950408889634
LANE = 128
AUX_POS_HI, AUX_POS_LO, AUX_BLK0 = 64, 65, 72
HEAD_PAD = 128
CHUNK_FEATS = CMP_STRIDE * NSA_GROUPS * NSA_DH
FFN_BLOCK = 1536
SLOPES =tuple(float(2.0 ** (-8.0 * (h + 1) / NSA_HEADS)) for h in range(NSA_HEADS))
VMEM_LIMIT = 56 * 1024 * 1024

F32 = jnp.float32
BF16 = jnp.bfloat16
_NT = (((1,), (1,)), ((), ()))


def _cparams(*sem):
    return pltpu.CompilerParams(dimension_semantics=sem, vmem_limit_bytes=VMEM_LIMIT)


def _rms(x, g):
    return x * lax.rsqrt(jnp.mean(x * x, axis=-1, keepdims=True) + EPS) * g


def _dot(a, b):
    return jnp.dot(a, b, preferred_element_type=F32)


def _dot_nt(a, b):
    return lax.dot_general(a, b, _NT, preferred_element_type=F32)


def _dot_exact(a, b):
    return jnp.dot(a, b, preferred_element_type=F32, precision=lax.Precision.HIGHEST)


def _iota(shape, dim):
    return lax.broadcasted_iota(jnp.int32, shape, dim)


def _log2(n):
    assert n > 0 and n & (n - 1) == 0, n
    return n.bit_length() - 1


def _vdiv(x, n):
    return lax.shift_right_logical(x, jnp.full(x.shape, _log2(n), jnp.int32))


def _vmod(x, n):
    assert n & (n - 1) == 0, n
    return x & (n - 1)


_O_CQ, _O_CKV, _O_QN, _O_K6, _O_GA = 0, 384, 640, 1152, 1920


def _inproj_kernel(x_ref, g1_ref, w_ref, qg_ref, kvg_ref, wuq_ref, wk_ref, wv_ref, cos_ref, sin_ref,
                   *outs, sample, q_rank, kv_rank, d_model, tiles_per_seq, q_scale):
    o_gb = _O_GA + d_model
    o_kr = o_gb + d_model
    x = x_ref[...]
    hn = _rms(x, g1_ref[...])
    y = _dot(hn.astype(BF16), w_ref[...])
    cosq = cos_ref[...]
    sinq = sin_ref[...]
    nq = MLA_HEADS * HEAD_PAD
    cqn = _rms(y[:, _O_CQ:_O_CQ + q_rank], qg_ref[...])
    q2 = _dot(cqn.astype(BF16), wuq_ref[...])
    half = MLA_D_ROPE // 2
    lane_h = _iota((x.shape[0], HEAD_PAD), 1)

    def rotated(qh):
        partner = jnp.where(lane_h < MLA_D_NOPE + half, -pltpu.roll(qh, HEAD_PAD - half, 1), pltpu.roll(qh, half, 1))
        return (qh * cosq + partner * sinq) * q_scale

    ckv = _rms(y[:, _O_CKV:_O_CKV + kv_rank], kvg_ref[...])
    kr = y[:, o_kr:o_kr + LANE] * cosq + y[:, o_kr + LANE:o_kr + 2 * LANE] * sinq
    ckv_b = ckv.astype(BF16)
    it = iter(outs)
    if sample:
        qrot_ref, qabs_ref = next(it), next(it)
        for h in range(MLA_HEADS):
            sl = slice(h * HEAD_PAD, (h + 1) * HEAD_PAD)
            qh = rotated(q2[:, sl])
            qrot_ref[:, sl] = qh
            qabs_ref[:, h * kv_rank:(h + 1) * kv_rank] = _dot(qh.astype(BF16), wk_ref[h])
    else:
        q_ref, k_ref, v_ref = next(it), next(it), next(it)
        knp = _dot(ckv_b, wk_ref[...])
        for h in range(MLA_HEADS):
            sl = slice(h * HEAD_PAD, (h + 1) * HEAD_PAD)
            qh = rotated(q2[:, sl])
            q_ref[:, sl] = qh.astype(BF16)
            k_ref[:, sl] = (knp[:, sl] + kr).astype(BF16)
        ones_lane = (_vmod(_iota((1, nq), 1), HEAD_PAD) == MLA_D_V).astype(F32)
        v_ref[...] = (_dot(ckv_b, wv_ref[...]) + ones_lane).astype(BF16)
    ckv_ref, kr_ref, qn_ref = next(it), next(it), next(it)
    ckv_ref[...] = ckv
    kr_ref[...] = kr
    yq = y[:, _O_QN:_O_QN + NSA_HEADS * NSA_DH] * NSA_SCALE
    if sample:
        qn_ref[...] = yq.astype(BF16)
    else:
        lane_q = _iota((x.shape[0], LANE), 1)
        for h in range(NSA_HEADS):
            pair = yq[:, (h // 2) * LANE:(h // 2 + 1) * LANE]
            if h % 2:
                pair = pltpu.roll(pair, NSA_DH, 1)
            aux_q = jnp.where((lane_q == AUX_POS_HI) | (lane_q == AUX_POS_LO), SLOPES[h], 0.0)
            qn_ref[:, h * LANE:(h + 1) * LANE] = jnp.where(lane_q < NSA_DH, pair, aux_q).astype(BF16)
    for j in range(6):
        yj = y[:, _O_K6 + j * LANE:_O_K6 + (j + 1) * LANE]
        if sample:
            next(it)[...] = yj
        else:
            next(it)[...] = yj.T
    if not sample:
        tm = x.shape[0]
        pos = (pl.program_id(0) % tiles_per_seq) * tm + _iota((tm, 1), 0)
        lane = _iota((tm, LANE), 1)
        onehot = ((lane >= AUX_BLK0) & (_vdiv(pos, SLC_BLOCK) == lane - AUX_BLK0)).astype(F32)
        aux = jnp.where(lane == AUX_POS_HI, (pos - _vmod(pos, 256)).astype(F32),
                        jnp.where(lane == AUX_POS_LO, _vmod(pos, 256).astype(F32), onehot))
        for j in range(6):
            yj = y[:, _O_K6 + j * LANE:_O_K6 + (j + 1) * LANE]
            ref = next(it)
            if j in (2, 3, 4, 5):
                tail = aux if j in (2, 4) else (lane == NSA_DH).astype(F32)
                ref[:, :LANE] = jnp.where(lane < NSA_DH, yj, tail).astype(BF16)
                ref[:, LANE:] = jnp.where(lane < NSA_DH, pltpu.roll(yj, NSA_DH, 1), tail).astype(BF16)
            else:
                ref[...] = yj.astype(BF16)
    gn_ref, ga_ref, gb_ref = next(it), next(it), next(it)
    gn_ref[...] = jax.nn.sigmoid(y[:, o_kr + 2 * LANE:o_kr + 3 * LANE])
    ga_ref[...] = jax.nn.sigmoid(y[:, _O_GA:_O_GA + d_model])
    gb_ref[...] = jax.nn.sigmoid(y[:, o_gb:o_gb + d_model])


def _inproj(x, wts, cosq, sinq, *, sample, tm, tab_blocks):
    n, d = x.shape
    q_rank, kv_rank = wts['q_norm_g'].shape[1], wts['kv_norm_g'].shape[1]
    nq = MLA_HEADS * HEAD_PAD
    wk = wts['w_ukT'] if sample else wts['w_ukp']
    row = lambda i: (i, 0)
    const2 = lambda i: (0, 0)
    tab_map = (lambda i: (i % tab_blocks, 0))
    in_specs = [
        pl.BlockSpec((tm, d), row),
        pl.BlockSpec((1, d), const2),
        pl.BlockSpec(wts['w_in'].shape, const2),
        pl.BlockSpec((1, q_rank), const2),
        pl.BlockSpec((1, kv_rank), const2),
        pl.BlockSpec(wts['w_uq2'].shape, const2),
        pl.BlockSpec(wk.shape, (lambda i: (0, 0, 0)) if sample else const2),
        pl.BlockSpec(wts['w_uvf'].shape, const2),
        pl.BlockSpec((tm, LANE), tab_map),
        pl.BlockSpec((tm, LANE), tab_map),
    ]
    shapes = []
    if sample:
        shapes += [(nq, F32), (MLA_HEADS * kv_rank, F32)]
    else:
        shapes += [(nq, BF16), (nq, BF16), (nq, BF16)]
    shapes += [(kv_rank, F32), (LANE, F32), (NSA_HEADS * (NSA_DH if sample else LANE), BF16)]
    n_lead = len(shapes)
    shapes += [(LANE, F32)] * 6
    if not sample:
        shapes += [(LANE, BF16), (LANE, BF16)] + [(2 * LANE, BF16)] * 4
        assert AUX_BLK0 + -(-tab_blocks * tm // SLC_BLOCK) <= LANE, "block one-hot must fit the aux lanes"
    shapes += [(LANE, F32), (d, F32), (d, F32)]
    out_shape = [jax.ShapeDtypeStruct((n, w), dt) for w, dt in shapes]
    out_specs = [pl.BlockSpec((tm, w), row) for w, _ in shapes]
    if not sample:
        tpb = tab_blocks
        for j in range(n_lead, n_lead + 6):
            out_shape[j] = jax.ShapeDtypeStruct((n // (tpb * tm), LANE, tpb * tm), F32)
            out_specs[j] = pl.BlockSpec((None, LANE, tm), lambda i: (i // tpb, 0, i % tpb))
    q_scale = MLA_SCALE if sample else MLA_SCALE * LOG2E
    kern = functools.partial(_inproj_kernel, sample=sample, q_rank=q_rank, kv_rank=kv_rank, d_model=d,
                             tiles_per_seq=tab_blocks, q_scale=q_scale)
    return pl.pallas_call(
        kern, grid=(n // tm,), in_specs=in_specs, out_specs=out_specs, out_shape=out_shape,
        compiler_params=_cparams("parallel"), name="inproj_sample" if sample else "inproj_prompt",
    )(x, wts['norm1_g'], wts['w_in'], wts['q_norm_g'], wts['kv_norm_g'], wts['w_uq2'], wk, wts['w_uvf'],
      cosq, sinq)


def _stack_heads(q_ref, heads, width):
    parts = [q_ref[:, h * width:(h + 1) * width] for h in heads]
    return parts[0] if len(parts) == 1 else jnp.concatenate(parts, axis=0)


def _flash_kernel(*refs, tq, tk, groups, par, dq, dk, dv, window, has_bias, base2):
    if has_bias:
        q_ref, k_ref, v_ref, bias_ref, o_ref = refs
    else:
        q_ref, k_ref, v_ref, o_ref = refs
    q_start = pl.program_id(1) * tq
    n_hi = (q_start + tq - 1) // tk + 1
    hi_full = (q_start + 1) // tk
    if window:
        n_lo = jnp.maximum(q_start - (window - 1), 0) // tk
        lo_full = (jnp.maximum(q_start + tq - window, 0) + tk - 1) // tk
    else:
        n_lo, lo_full = 0, 0
    e1 = jnp.clip(lo_full, n_lo, n_hi)
    e2 = jnp.clip(hi_full, e1, n_hi)
    ex = jnp.exp2 if base2 else jnp.exp
    for c0 in range(0, len(groups), par):
        chunk = groups[c0:c0 + par]
        qs = []
        for heads, kcol in chunk:
            qg = _stack_heads(q_ref, heads, dq)
            if has_bias:
                bias = bias_ref[:, kcol * LANE:(kcol + 1) * LANE]
                qg = qg + jnp.concatenate([bias] * len(heads), axis=0)
            qs.append(qg)
        rows = qs[0].shape[0]
        qpos = q_start + (_iota((rows, 1), 0) & (tq - 1))

        def step(j, carry, masked):
            k0 = pl.multiple_of(j * tk, tk)
            if masked:
                dist = qpos - (k0 + _iota((1, tk), 1))
                mask = dist >= 0
                if window:
                    mask = mask & (dist < window)
            out = []
            for (heads, kcol), qg, (m, acc) in zip(chunk, qs, carry):
                kt = k_ref[pl.ds(k0, tk), kcol * dk:(kcol + 1) * dk]
                vt = v_ref[pl.ds(k0, tk), kcol * LANE:(kcol + 1) * LANE]
                s = _dot_nt(qg, kt)
                if masked:
                    s = jnp.where(mask, s, NEG)
                m_new = jnp.maximum(m, jnp.max(s, axis=-1, keepdims=True))
                p = ex((s - m_new).astype(BF16))
                acc = ex(m - m_new) * acc + _dot(p, vt)
                out.append((m_new, acc))
            return tuple(out)

        carry = tuple((jnp.full((rows, 1), NEG, F32), jnp.zeros((rows, LANE), F32)) for _ in chunk)
        carry = lax.fori_loop(n_lo, e1, functools.partial(step, masked=True), carry)
        carry = lax.fori_loop(e1, e2, functools.partial(step, masked=False), carry)
        carry = lax.fori_loop(e2, n_hi, functools.partial(step, masked=True), carry)
        for (heads, kcol), (m, acc) in zip(chunk, carry):
            o = acc[:, :dv] * (1.0 / acc[:, dv:dv + 1])
            for hh, h in enumerate(heads):
                o_ref[:, h * dv:(h + 1) * dv] = o[hh * tq:(hh + 1) * tq].astype(o_ref.dtype)


def _flash(q, k, v, bias, *, batch, seq, tq, tk, groups, par, dq, dk, dv, window, base2, name):
    n = q.shape[0]
    has_bias = bias is not None
    n_heads = sum(len(g[0]) for g in groups)
    qrow = lambda b, i: (b * (seq // tq) + i, 0)
    kv = lambda b, i: (b, 0)
    in_specs = [pl.BlockSpec((tq, q.shape[1]), qrow), pl.BlockSpec((seq, k.shape[1]), kv),
                pl.BlockSpec((seq, v.shape[1]), kv)]
    args = [q, k, v]
    if has_bias:
        in_specs.append(pl.BlockSpec((tq, bias.shape[1]), qrow))
        args.append(bias)
    kern = functools.partial(_flash_kernel, tq=tq, tk=tk, groups=groups, par=par, dq=dq, dk=dk, dv=dv, window=window,
                             has_bias=has_bias, base2=base2)
    return pl.pallas_call(
        kern, grid=(batch, seq // tq), in_specs=in_specs,
        out_specs=pl.BlockSpec((tq, n_heads * dv), qrow),
        out_shape=jax.ShapeDtypeStruct((n, n_heads * dv), BF16),
        compiler_params=_cparams("parallel", "arbitrary"), name=name,
    )(*args)


def _win_kernel(q_ref, k_ref, v_ref, o_ref, *, tq, groups, dv):
    i = pl.program_id(1)
    rq = _iota((tq, tq), 0)
    ck = _iota((tq, tq), 1)
    bias_own = jnp.where(ck <= rq, 0.0, NEG)
    bias_far = jnp.where(ck > rq, 0.0, NEG) + jnp.where(i >= 2, 0.0, NEG)
    bias_mid = jnp.where(i >= 1, 0.0, NEG)
    starts = (jnp.maximum(i - 2, 0) * tq, jnp.maximum(i - 1, 0) * tq, i * tq)
    for heads, kcol in groups:
        qg = _stack_heads(q_ref, heads, LANE)
        nh = len(heads)
        ss, vs = [], []
        for k0, bias in zip(starts, (bias_far, None, bias_own)):
            k0 = pl.multiple_of(k0, tq)
            s = _dot_nt(qg, k_ref[pl.ds(k0, tq), kcol * LANE:(kcol + 1) * LANE])
            s = s + (bias_mid if bias is None else jnp.concatenate([bias] * nh, axis=0))
            ss.append(s)
            vs.append(v_ref[pl.ds(k0, tq), kcol * LANE:(kcol + 1) * LANE])
        m = functools.reduce(jnp.maximum, [jnp.max(s, axis=-1, keepdims=True) for s in ss])
        acc = functools.reduce(jnp.add, [_dot(jnp.exp((s - m).astype(BF16)), v) for s, v in zip(ss, vs)])
        o = acc[:, :dv] * (1.0 / acc[:, dv:dv + 1])
        for hh, h in enumerate(heads):
            o_ref[:, h * dv:(h + 1) * dv] = o[hh * tq:(hh + 1) * tq].astype(o_ref.dtype)


def _win_prompt(q, k, v, *, batch, seq, tq, groups, dv):
    assert WINDOW == 2 * tq and seq % tq == 0
    n = q.shape[0]
    n_heads = sum(len(g[0]) for g in groups)
    qrow = lambda b, i: (b * (seq // tq) + i, 0)
    kv = lambda b, i: (b, 0)
    return pl.pallas_call(
        functools.partial(_win_kernel, tq=tq, groups=groups, dv=dv), grid=(batch, seq // tq),
        in_specs=[pl.BlockSpec((tq, q.shape[1]), qrow), pl.BlockSpec((seq, k.shape[1]), kv),
                  pl.BlockSpec((seq, v.shape[1]), kv)],
        out_specs=pl.BlockSpec((tq, n_heads * dv), qrow),
        out_shape=jax.ShapeDtypeStruct((n, n_heads * dv), BF16),
        compiler_params=_cparams("parallel", "arbitrary"), name="win_prompt",
    )(q, k, v)


def _compress_rows(xk, xv, w1k_ref, w1v_ref, posk_ref, posv_ref):
    yk = _dot(xk.astype(BF16), w1k_ref[...])
    yv = _dot(xv.astype(BF16), w1v_ref[...])
    return yk, yv


def _compress_finish(y, posy, w2_ref):
    rows = y.shape[0]
    a = y[:, :LANE]
    b = pltpu.roll(y[:, LANE:], rows - 1, 0)
    pos = posy[0:1, :LANE] + posy[1:2, LANE:]
    hid = jax.nn.gelu(a + b + pos)
    return _dot(hid.astype(BF16), w2_ref[...])


def _compress_prompt_kernel(xk_ref, xv_ref, w1k_ref, w1v_ref, pk_ref, pv_ref, w2k_ref, w2v_ref, ok_ref, ov_ref):
    yk, yv = _compress_rows(xk_ref[...], xv_ref[...], w1k_ref, w1v_ref, pk_ref, pv_ref)
    ok_ref[...] = _compress_finish(yk, _dot(pk_ref[...], w1k_ref[...]), w2k_ref).astype(BF16)
    ov_ref[...] = _compress_finish(yv, _dot(pv_ref[...], w1v_ref[...]), w2v_ref).astype(BF16)


def _compress_prompt(kc, vc, wts, *, batch, seq):
    nch = seq // CMP_STRIDE
    xk = kc.reshape(batch * nch, CHUNK_FEATS)
    xv = vc.reshape(batch * nch, CHUNK_FEATS)
    row = lambda b: (b, 0)
    c2 = lambda b: (0, 0)
    wspec = pl.BlockSpec((CHUNK_FEATS, 2 * LANE), c2)
    pspec = pl.BlockSpec((8, CHUNK_FEATS), c2)
    w2spec = pl.BlockSpec((LANE, LANE), c2)
    return pl.pallas_call(
        _compress_prompt_kernel, grid=(batch,),
        in_specs=[pl.BlockSpec((nch, CHUNK_FEATS), row), pl.BlockSpec((nch, CHUNK_FEATS), row),
                  wspec, wspec, pspec, pspec, w2spec, w2spec],
        out_specs=[pl.BlockSpec((nch, LANE), row)] * 2,
        out_shape=[jax.ShapeDtypeStruct((batch * nch, LANE), BF16)] * 2,
        compiler_params=_cparams("parallel"), name="compress_prompt",
    )(xk, xv, wts['cmp_w1k'], wts['cmp_w1v'], wts['cmp_posk'], wts['cmp_posv'], wts['cmp_w2k'], wts['cmp_w2v'])


def _overlap(n_rows, n_sel):
    c = _iota((n_rows, n_sel), 0) * CMP_STRIDE
    j = _iota((n_rows, n_sel), 1) * SLC_BLOCK
    return ((c < j + SLC_BLOCK) & (c + CMP_BLOCK > j)).astype(F32)


def _force_scores(score, cur, jj):
    forced = (jj == 0) | (jj == cur) | (jj == cur - 1)
    score = jnp.where(forced, FORCE, score)
    return jnp.where(jj <= cur, score, NEG)


def _cmp_prompt_kernel(q_ref, k_ref, v_ref, o_ref, bias_ref, *, tq, n_cmp, n_sel):
    q_start = pl.program_id(1) * tq
    ncp = k_ref.shape[0]
    rows = NSA_HPG * tq
    qpos = q_start + (_iota((rows, 1), 0) & (tq - 1))
    cidx = _iota((1, ncp), 1)
    dist = qpos - (cidx * CMP_STRIDE + CMP_BLOCK - 1)
    mask = (dist >= 0) & (cidx < n_cmp)
    distf = dist.astype(F32)
    hrow = _vdiv(_iota((rows, 1), 0), tq)
    nsp = -(-n_sel // 8) * 8
    cur = _vdiv(q_start + _iota((1, tq), 1), SLC_BLOCK)
    jj = _iota((nsp, tq), 0)
    cb = _iota((nsp, ncp), 1) * CMP_STRIDE
    jb = _iota((nsp, ncp), 0) * SLC_BLOCK
    ov_t = ((cb < jb + SLC_BLOCK) & (cb + CMP_BLOCK > jb)).astype(F32)
    for g in range(NSA_GROUPS):
        heads = range(g * NSA_HPG, (g + 1) * NSA_HPG)
        qg = jnp.concatenate([q_ref[:, h * LANE:h * LANE + NSA_DH] for h in heads], axis=0)
        slope = jnp.zeros((rows, 1), F32)
        for hh, h in enumerate(heads):
            slope = jnp.where(hrow == hh, SLOPES[h], slope)
        s = _dot_nt(qg, k_ref[:, g * NSA_DH:(g + 1) * NSA_DH]) - slope * distf
        s = jnp.where(mask, s, NEG)
        m = jnp.max(s, axis=-1, keepdims=True)
        p = jnp.where(mask, jnp.exp(s - m), 0.0)
        l = jnp.sum(p, axis=-1, keepdims=True)
        p = p / jnp.where(l > 0.0, l, 1.0)
        o = _dot(p.astype(BF16), v_ref[:, g * NSA_DH:(g + 1) * NSA_DH])
        imp = p[0:tq]
        for hh in range(1, NSA_HPG):
            imp = imp + p[hh * tq:(hh + 1) * tq]
        for hh, h in enumerate(heads):
            o_ref[:, h * NSA_DH:(h + 1) * NSA_DH] = o[hh * tq:(hh + 1) * tq].astype(BF16)
        score = lax.dot_general(ov_t, imp, _NT, preferred_element_type=F32, precision=lax.Precision.HIGHEST)
        score = _force_scores(score, cur, jj)
        rank = jnp.zeros((nsp, tq), F32)
        for i in range(n_sel):
            ri = score[i:i + 1, :]
            beats = (ri > score) | ((ri == score) & (i < jj))
            rank = rank + beats.astype(F32)
        sel = (rank < float(min(SLC_TOP_N, n_sel))) & (jj <= cur)
        bias_t = jnp.where(sel | (jj >= n_sel), 0.0, NEG)
        bias_t = jnp.concatenate([jnp.zeros((AUX_BLK0, tq), F32), bias_t,
                                  jnp.zeros((LANE - AUX_BLK0 - nsp, tq), F32)], axis=0)
        bias_ref[:, g * LANE:(g + 1) * LANE] = bias_t.T.astype(BF16)


def _cmp_prompt(qn, kcc, vcc, *, batch, seq, tq):
    n = qn.shape[0]
    nch = seq // CMP_STRIDE
    n_cmp = nch - CMP_BLOCK // CMP_STRIDE + 1
    n_sel = -(-seq // SLC_BLOCK)
    qrow = lambda b, i: (b * (seq // tq) + i, 0)
    kv = lambda b, i: (b, 0)
    kern = functools.partial(_cmp_prompt_kernel, tq=tq, n_cmp=n_cmp, n_sel=n_sel)
    return pl.pallas_call(
        kern, grid=(batch, seq // tq),
        in_specs=[pl.BlockSpec((tq, qn.shape[1]), qrow), pl.BlockSpec((nch, LANE), kv), pl.BlockSpec((nch, LANE), kv)],
        out_specs=[pl.BlockSpec((tq, NSA_HEADS * NSA_DH), qrow), pl.BlockSpec((tq, NSA_GROUPS * LANE), qrow)],
        out_shape=[jax.ShapeDtypeStruct((n, NSA_HEADS * NSA_DH), BF16),
                   jax.ShapeDtypeStruct((n, NSA_GROUPS * LANE), BF16)],
        compiler_params=_cparams("parallel", "arbitrary"), name="cmp_prompt",
    )(qn, kcc, vcc)


def _finish_kernel(*refs, period, latent, has_state):
    it = iter(refs)
    x_ref, omla_ref, ocmp_ref, oslc_ref, owin_ref, gn_ref, ga_ref, gb_ref = (next(it) for _ in range(8))
    prev1_ref, prev2_ref = (next(it), next(it)) if has_state else (None, None)
    gx_ref = next(it)
    wuv_ref = next(it) if latent else None
    (wpm_ref, wpn_ref, wo_ref, g2_ref, wg_ref, wu_ref, cw_ref, cb_ref, wd_ref, gf_ref,
     y_ref, gout_ref, carry_ref) = it
    tm = x_ref.shape[0]
    gn = gn_ref[...]
    gn_hi = gn.astype(BF16)
    gn_lo = (gn - gn_hi.astype(F32)).astype(BF16)
    gexp = _dot(gn_hi, gx_ref[...]) + _dot(gn_lo, gx_ref[...])
    w = NSA_HEADS * NSA_DH
    o_nsa = (gexp[:, 0:w] * ocmp_ref[...].astype(F32) + gexp[:, w:2 * w] * oslc_ref[...].astype(F32)
             + gexp[:, 2 * w:3 * w] * owin_ref[...].astype(F32))
    o_mla = omla_ref[...].astype(BF16)
    if latent:
        o_mla = _dot(o_mla, wuv_ref[...]).astype(BF16)
    merged = (ga_ref[...] * _dot(o_mla, wpm_ref[...])
              + gb_ref[...] * _dot(o_nsa.astype(BF16), wpn_ref[...]))
    x1 = x_ref[...] + _dot(merged.astype(BF16), wo_ref[...])
    h2 = _rms(x1, g2_ref[...]).astype(BF16)
    row = _iota((tm, 1), 0)
    i = pl.program_id(0)
    t = _vmod(i * tm + row, period)
    carried = period > tm
    if carried:
        @pl.when(i == 0)
        def _():
            carry_ref[...] = jnp.zeros_like(carry_ref)
    dff = wg_ref.shape[1]
    x2 = x1
    for c0 in range(0, dff, FFN_BLOCK):
        cs = slice(c0, min(c0 + FFN_BLOCK, dff))
        g = _dot(h2, wg_ref[:, cs])
        u = _dot(h2, wu_ref[:, cs])
        g1 = pltpu.roll(g, 1, 0)
        g2 = pltpu.roll(g, 2, 0)
        if carried:
            c = carry_ref[:, cs]
            g1 = jnp.where(row == 0, c[7:8], g1)
            g2 = jnp.where(row == 0, c[6:7], jnp.where(row == 1, c[7:8], g2))
            carry_ref[:, cs] = g[tm - 8:tm]
        g1 = jnp.where(t >= 1, g1, prev1_ref[:, cs] if has_state else 0.0)
        g2 = jnp.where(t >= 2, g2, prev2_ref[:, cs] if has_state else 0.0)
        cw = cw_ref[:, cs]
        conv = cb_ref[:, cs] + cw[0:1] * g2 + cw[1:2] * g1 + cw[2:3] * g
        act = (jax.nn.silu(conv) * u).astype(BF16)
        x2 = x2 + _dot(act, wd_ref[cs, :])
        gout_ref[:, cs] = g[tm - 8:tm] if gout_ref.shape[0] == 8 else g
    y_ref[...] = _rms(x2, gf_ref[...])


def _finish(x, omla, ocmp, oslc, owin, gn, ga, gb, state_rows, wts, *, tm, period, full_g, latent):
    n, d = x.shape
    dff = wts['w_gate'].shape[1]
    row = lambda i: (i, 0)
    c2 = lambda i: (0, 0)
    acts = [x, omla, ocmp, oslc, owin, gn, ga, gb] + (list(state_rows) if state_rows is not None else [])
    consts = [wts['gate_expand']] + ([wts['w_uvbd']] if latent else []) + [
        wts['w_proj_mla'], wts['w_proj_nsa'], wts['w_out'], wts['norm2_g'],
        wts['w_gate'], wts['w_up'], wts['conv_w'], wts['conv_b'], wts['w_down'], wts['norm_f_g']]
    ins = acts + consts
    in_specs = [pl.BlockSpec((tm, a.shape[1]), row) for a in acts] + [pl.BlockSpec(a.shape, c2) for a in consts]
    g_rows = n if full_g else (n // tm) * 8
    g_blk = tm if full_g else 8
    kern = functools.partial(_finish_kernel, period=period, latent=latent, has_state=state_rows is not None)
    return pl.pallas_call(
        kern, grid=(n // tm,), in_specs=in_specs,
        out_specs=[pl.BlockSpec((tm, d), row), pl.BlockSpec((g_blk, dff), row)],
        out_shape=[jax.ShapeDtypeStruct((n, d), F32), jax.ShapeDtypeStruct((g_rows, dff), F32)],
        scratch_shapes=[pltpu.VMEM((8, dff), F32)],
        compiler_params=_cparams("arbitrary"), name="finish_full" if full_g else "finish_tiled",
    )(*ins)


def _swap_halves(w):
    hlf = w.shape[-1] // 2
    return jnp.concatenate([-w[..., hlf:], w[..., :hlf]], axis=-1)


def _prep_weights(p):
    d = p['w_in'].shape[0]
    q_rank, kv_rank = p['q_norm_g'].shape[-1], p['kv_norm_g'].shape[-1]
    sizes = [q_rank, kv_rank, MLA_D_ROPE, NSA_HEADS * NSA_DH] + [2 * NSA_GROUPS * NSA_DH] * 3 + [3 * NSA_HEADS, d, d]
    cuts = np.cumsum(sizes)[:-1].tolist()
    cq, ckv, kr, qn, kvc, kvs, kvw, gn, ga, gb = jnp.split(p['w_in'], cuts, axis=-1)
    assert _O_CKV == q_rank and _O_QN == q_rank + kv_rank
    lo, hi = MLA_D_NOPE, HEAD_PAD - MLA_D_NOPE - MLA_D_ROPE
    place = lambda w: jnp.pad(w, ((0, 0), (lo, hi)))
    gnp = jnp.pad(gn, ((0, 0), (0, LANE - gn.shape[1])))
    w_in = jnp.concatenate([cq, ckv, qn, kvc, kvs, kvw, ga, gb, place(kr), place(_swap_halves(kr)), gnp], axis=1)
    w = {'w_in': w_in.astype(BF16)}
    for k in ('norm1_g', 'q_norm_g', 'kv_norm_g', 'norm2_g', 'conv_b'):
        w[k] = p[k].reshape(1, -1)
    w['norm_f_g'] = p['norm_f_g'].reshape(1, -1)
    w['conv_w'] = jnp.pad(p['conv_w'], ((0, 8 - CONV_W), (0, 0)))
    uq = p['w_uq']
    w['w_uq2'] = jnp.pad(uq, ((0, 0), (0, 0), (0, HEAD_PAD - uq.shape[-1]))).reshape(q_rank, -1).astype(BF16)
    uk = p['w_uk']
    w['w_ukp'] = jnp.pad(uk, ((0, 0), (0, 0), (0, HEAD_PAD - MLA_D_NOPE))).reshape(kv_rank, -1).astype(BF16)
    w['w_ukT'] = jnp.pad(jnp.transpose(uk, (1, 2, 0)), ((0, 0), (0, HEAD_PAD - MLA_D_NOPE), (0, 0))).astype(BF16)
    w['w_uvf'] = jnp.pad(p['w_uv'], ((0, 0), (0, 0), (0, HEAD_PAD - MLA_D_V))).reshape(kv_rank, -1).astype(BF16)
    eye_h = jnp.eye(MLA_HEADS, dtype=F32)
    w['w_uvbd'] = jnp.einsum('rhv,hk->hrkv', p['w_uv'], eye_h).reshape(MLA_HEADS * kv_rank, -1).astype(BF16)
    eye_g = jnp.eye(NSA_GROUPS, dtype=F32)
    for nm in ('k', 'v'):
        w1 = p['cmp_w1_' + nm].reshape(2, CMP_STRIDE, NSA_DH, -1)
        big = jnp.einsum('ajdh,gk->jgdakh', w1, eye_g)
        w['cmp_w1' + nm] = big.reshape(CHUNK_FEATS, -1).astype(BF16)
        pos = p['cmp_pos_' + nm].reshape(2, CMP_STRIDE, 1, NSA_DH)
        pos = jnp.broadcast_to(pos, (2, CMP_STRIDE, NSA_GROUPS, NSA_DH)).reshape(2, CHUNK_FEATS)
        w['cmp_pos' + nm] = jnp.pad(pos, ((0, 6), (0, 0))).astype(BF16)
        w2 = p['cmp_w2_' + nm]
        w['cmp_w2' + nm] = jnp.einsum('hd,gk->ghkd', w2, eye_g).reshape(NSA_GROUPS * w2.shape[0], -1).astype(BF16)
    ge = np.zeros((LANE, 3 * NSA_HEADS * NSA_DH), np.float32)
    for h in range(NSA_HEADS):
        for i in range(3):
            ge[h * 3 + i, i * NSA_HEADS * NSA_DH + h * NSA_DH:i * NSA_HEADS * NSA_DH + (h + 1) * NSA_DH] = 1.0
    w['gate_expand'] = jnp.asarray(ge)
    for k in ('w_proj_mla', 'w_proj_nsa', 'w_out', 'w_gate', 'w_up', 'w_down'):
        w[k] = p[k].astype(BF16)
    return w


def _rope_tables(pos):
    inv = ROPE_THETA ** (-jnp.arange(0, MLA_D_ROPE, 2, dtype=F32) / MLA_D_ROPE)
    ang = pos.astype(F32)[:, None] * inv[None, :]
    cos, sin = jnp.cos(ang), jnp.sin(ang)
    n = pos.shape[0]
    pad = jnp.zeros((n, HEAD_PAD - MLA_D_NOPE - MLA_D_ROPE), F32)
    cosq = jnp.concatenate([jnp.ones((n, MLA_D_NOPE), F32), cos, cos, pad], axis=1)
    sinq = jnp.concatenate([jnp.zeros((n, MLA_D_NOPE), F32), sin, sin, pad], axis=1)
    return cosq, sinq


_NSA_GROUPS_SPEC = tuple((tuple(range(g * NSA_HPG, (g + 1) * NSA_HPG)), g) for g in range(NSA_GROUPS))
_MLA_GROUPS_SPEC = tuple(((h,), h) for h in range(MLA_HEADS))


def _prompt(x_prompt, w):
    b, t, d = x_prompt.shape
    n = b * t
    x = x_prompt.reshape(n, d)
    tm = 256
    cosq, sinq = _rope_tables(jnp.arange(t, dtype=jnp.int32))
    (q_mla, k_mla, v_mla, ckv, krp, qn, kc, vc, ks, vs, kw, vw, kc_b, vc_b, ks_b, vs_b, kw_b, vw_b, gn, ga, gb) = _inproj(
        x, w, cosq, sinq, sample=False, tm=tm, tab_blocks=t // tm)
    o_mla = _flash(q_mla, k_mla, v_mla, None, batch=b, seq=t, tq=512, tk=512, groups=_MLA_GROUPS_SPEC, par=8,
                   dq=HEAD_PAD, dk=HEAD_PAD, dv=MLA_D_V, window=0, base2=True, name="mla_prompt")
    kcc, vcc = _compress_prompt(kc_b, vc_b, w, batch=b, seq=t)
    o_cmp, sel_bias = _cmp_prompt(qn, kcc, vcc, batch=b, seq=t, tq=128)
    o_slc = _flash(qn, ks_b, vs_b, sel_bias, batch=b, seq=t, tq=256, tk=512, groups=_NSA_GROUPS_SPEC, par=2,
                   dq=LANE, dk=LANE, dv=NSA_DH, window=0, base2=False, name="slc_prompt")
    o_win = _win_prompt(qn, kw_b, vw_b, batch=b, seq=t, tq=WINDOW // 2, groups=_NSA_GROUPS_SPEC, dv=NSA_DH)
    dff = w['w_gate'].shape[1]
    y, gtail = _finish(x, o_mla, o_cmp, o_slc, o_win, gn, ga, gb, None, w, tm=tm, period=t, full_g=False,
                       latent=False)
    kv4 = lambda a: a.reshape(1, b, NSA_GROUPS, NSA_DH, a.shape[-1]).transpose(0, 1, 4, 2, 3)
    n_keep = min(WINDOW, t)
    kw, vw = kw[:, :, t - n_keep:], vw[:, :, t - n_keep:]
    conv_state = gtail.reshape(b, t // tm, 8, dff)[:, -1, 8 - (CONV_W - 1):, :]
    states = (ckv.reshape(1, b, t, -1), krp[:, MLA_D_NOPE:MLA_D_NOPE + MLA_D_ROPE].reshape(1, b, t, MLA_D_ROPE),
              kv4(kc), kv4(vc), kv4(ks), kv4(vs), kv4(kw), kv4(vw), conv_state[None])
    return y.reshape(b, t, d), states


def _page_copies(pt_ref, pools, bufs, sems, step, slot, pp):
    copies = []
    for k in range(pp):
        page = pt_ref[step * pp + k]
        for pool, buf, sem in zip(pools, bufs, sems):
            copies.append(pltpu.make_async_copy(pool.at[page], buf.at[slot, k], sem.at[slot]))
    return copies


def _start_all(copies, n_pools):
    for i, c in enumerate(copies):
        c.start(priority=(i // n_pools) % 2)


def _stream_pages(pt_ref, pools, bufs, sems, pp):
    step = pl.program_id(0) * pl.num_programs(1) + pl.program_id(1)
    total = pl.num_programs(0) * pl.num_programs(1)
    slot = step % 2

    @pl.when(step == 0)
    def _():
        _start_all(_page_copies(pt_ref, pools, bufs, sems, step, slot, pp), len(pools))

    @pl.when(step + 1 < total)
    def _():
        _start_all(_page_copies(pt_ref, pools, bufs, sems, step + 1, 1 - slot, pp), len(pools))

    for c in _page_copies(pt_ref, pools, bufs, sems, step, slot, pp):
        c.wait()
    return slot


def _softmax_update(sc, v, m_scr, l_scr, acc_scr, v_transposed=False):
    m_old = m_scr[...]
    m_new = jnp.maximum(m_old, jnp.max(sc, axis=-1, keepdims=True))
    p = jnp.exp(sc - m_new)
    alpha = jnp.exp(m_old - m_new)
    l_scr[...] = alpha * l_scr[...] + jnp.sum(p, axis=-1, keepdims=True)
    pv = _dot_nt(p.astype(BF16), v) if v_transposed else _dot(p.astype(BF16), v)
    acc_scr[...] = alpha * acc_scr[...] + pv
    m_scr[...] = m_new


def _mla_decode_step(slot, q_ref, knew_ref, o_ref, cbuf, rbuf, kscr, krscr, m_scr, l_scr, acc_scr, *, pp, td, kv_rank):
    s = pl.program_id(1)
    q = q_ref[...]
    rows = q.shape[0]

    @pl.when(s == 0)
    def _():
        m_scr[...] = jnp.full_like(m_scr, NEG)
        l_scr[...] = jnp.zeros_like(l_scr)
        acc_scr[...] = jnp.zeros_like(acc_scr)
        kn = knew_ref[...]
        trow = _vdiv(_iota((rows, 1), 0), MLA_HEADS)
        col = _iota((1, kn.shape[0]), 1)
        sc = jnp.where((col <= trow) & (col < td), _dot_nt(q, kn), NEG)
        _softmax_update(sc, kn[:, :kv_rank], m_scr, l_scr, acc_scr)

    for k in range(pp):
        kscr[k * PAGE_SIZE:(k + 1) * PAGE_SIZE, :] = cbuf[slot, k].astype(BF16)
        krscr[:, k * PAGE_SIZE:(k + 1) * PAGE_SIZE] = rbuf[slot, k].astype(BF16)
    kt = kscr[...]
    sc = _dot_nt(q[:, :kv_rank], kt) + _dot(q[:, kv_rank:kv_rank + MLA_D_ROPE], krscr[...])
    _softmax_update(sc, kt, m_scr, l_scr, acc_scr)

    @pl.when(s == pl.num_programs(1) - 1)
    def _():
        o_ref[...] = acc_scr[...] / l_scr[...]


def _alibi_rows(rows, td):
    r = _iota((rows, 1), 0)
    return _vdiv(r, td), _vmod(r, td)


def _slope_rows(hrow, g):
    slope = jnp.zeros(hrow.shape, F32)
    for hh in range(NSA_HPG):
        slope = jnp.where(hrow == hh, SLOPES[g * NSA_HPG + hh], slope)
    return slope


def _cmp_decode_step(slot, q_ref, w1k_ref, w1v_ref, pk_ref, pv_ref, w2k_ref, w2v_ref, o_ref, score_ref,
                     kbuf, vbuf, kp0, kp1, vp0, vp1, yk_scr, yv_scr, *, pp, td, past, n_cmp, n_sel_pad):
    s = pl.program_id(1)
    cpp = PAGE_SIZE // CMP_STRIDE
    hp = pp // 2
    for half, (kp_scr, vp_scr) in enumerate(((kp0, vp0), (kp1, vp1))):
        for k in range(hp):
            kp_scr[k * PAGE_SIZE:(k + 1) * PAGE_SIZE, :] = kbuf[slot, half * hp + k].T
            vp_scr[k * PAGE_SIZE:(k + 1) * PAGE_SIZE, :] = vbuf[slot, half * hp + k].T
    chunk_rows = lambda scr: jnp.concatenate(
        [scr[pl.ds(j, hp * cpp, stride=CMP_STRIDE), :] for j in range(CMP_STRIDE)], axis=1).astype(BF16)
    for half, (kp_scr, vp_scr) in enumerate(((kp0, vp0), (kp1, vp1))):
        r0 = pl.multiple_of(s * (pp * cpp) + half * (hp * cpp), hp * cpp)
        yk_scr[pl.ds(r0, hp * cpp), :] = _dot(chunk_rows(kp_scr), w1k_ref[...])
        yv_scr[pl.ds(r0, hp * cpp), :] = _dot(chunk_rows(vp_scr), w1v_ref[...])

    @pl.when(s == pl.num_programs(1) - 1)
    def _():
        kcc = _compress_finish(yk_scr[...], _dot(pk_ref[...], w1k_ref[...]), w2k_ref).astype(BF16)
        vcc = _compress_finish(yv_scr[...], _dot(pv_ref[...], w1v_ref[...]), w2v_ref).astype(BF16)
        ncp = kcc.shape[0]
        rows = NSA_HPG * td
        hrow, trow = _alibi_rows(rows, td)
        cidx = _iota((1, ncp), 1)
        dist = (past + trow) - (cidx * CMP_STRIDE + CMP_BLOCK - 1)
        mask = (dist >= 0) & (cidx < n_cmp)
        distf = dist.astype(F32)
        tsum = (_vmod(_iota((8, rows), 1), td) == _iota((8, rows), 0)).astype(F32)
        ov = _overlap(ncp, n_sel_pad)
        t8 = _iota((8, 1), 0)
        cur = _vdiv(past + t8, SLC_BLOCK)
        jj = _iota((8, n_sel_pad), 1)
        for g in range(NSA_GROUPS):
            sc = _dot_nt(q_ref[g], kcc) - _slope_rows(hrow, g) * distf
            sc = jnp.where(mask, sc, NEG)
            m = jnp.max(sc, axis=-1, keepdims=True)
            p = jnp.where(mask, jnp.exp(sc - m), 0.0)
            l = jnp.sum(p, axis=-1, keepdims=True)
            p = p / jnp.where(l > 0.0, l, 1.0)
            o_ref[g] = _dot(p.astype(BF16), vcc)
            imp = _dot_exact(tsum, p)
            score_ref[g] = _force_scores(_dot_exact(imp, ov), cur, jj)


def _mla_cmp_decode_kernel(pt_ref, qm_ref, knew_ref, qc_ref, w1k_ref, w1v_ref, pk_ref, pv_ref, w2k_ref, w2v_ref,
                           ckv_hbm, kr_hbm, ck_hbm, cv_hbm, olat_ref, ocmp_ref, score_ref,
                           cbuf, rbuf, kbuf, vbuf, csem, rsem, ksem, vsem,
                           kscr, krscr, m_scr, l_scr, acc_scr, kp0, kp1, vp0, vp1, yk_scr, yv_scr,
                           *, pp, td, kv_rank, past, n_cmp, n_sel_pad):
    slot = _stream_pages(pt_ref, (ckv_hbm, kr_hbm, ck_hbm, cv_hbm), (cbuf, rbuf, kbuf, vbuf),
                         (csem, rsem, ksem, vsem), pp)
    _mla_decode_step(slot, qm_ref, knew_ref, olat_ref, cbuf, rbuf, kscr, krscr, m_scr, l_scr, acc_scr,
                     pp=pp, td=td, kv_rank=kv_rank)
    _cmp_decode_step(slot, qc_ref, w1k_ref, w1v_ref, pk_ref, pv_ref, w2k_ref, w2v_ref, ocmp_ref, score_ref,
                     kbuf, vbuf, kp0, kp1, vp0, vp1, yk_scr, yv_scr, pp=pp, td=td, past=past, n_cmp=n_cmp,
                     n_sel_pad=n_sel_pad)


def _mla_cmp_decode(pt, qd_mla, knew, qd, ckv_pool, kr_pool, k_pool, v_pool, wts, *, td, pp, past, n_sel_pad):
    bd, mrows, qw = qd_mla.shape
    kv_rank = ckv_pool.shape[-1]
    n_pages = pt.shape[0] // bd
    cpp = PAGE_SIZE // CMP_STRIDE
    nch = n_pages * cpp
    n_cmp = (past + td) // CMP_STRIDE - CMP_BLOCK // CMP_STRIDE + 1
    assert (past + td) // CMP_STRIDE == nch, "new rows must not complete a chunk"
    per_b3 = lambda b, s, pt: (b, 0, 0)
    per_b = lambda b, s, pt: (b, 0, 0, 0)
    c2 = lambda b, s, pt: (0, 0)
    rows = qd.shape[2]
    in_specs = ([pl.BlockSpec((None, mrows, qw), per_b3), pl.BlockSpec((None,) + knew.shape[1:], per_b3),
                 pl.BlockSpec((None,) + qd.shape[1:], per_b)]
                + [pl.BlockSpec((CHUNK_FEATS, 2 * LANE), c2) for _ in range(2)]
                + [pl.BlockSpec((8, CHUNK_FEATS), c2) for _ in range(2)]
                + [pl.BlockSpec((LANE, LANE), c2) for _ in range(2)]
                + [pl.BlockSpec(memory_space=pl.ANY) for _ in range(4)])
    kern = functools.partial(_mla_cmp_decode_kernel, pp=pp, td=td, kv_rank=kv_rank, past=past, n_cmp=n_cmp,
                             n_sel_pad=n_sel_pad)
    pools = (ckv_pool, kr_pool, k_pool, v_pool)
    return pl.pallas_call(
        kern,
        grid_spec=pltpu.PrefetchScalarGridSpec(
            num_scalar_prefetch=1, grid=(bd, n_pages // pp), in_specs=in_specs,
            out_specs=[pl.BlockSpec((None, mrows, kv_rank), per_b3),
                       pl.BlockSpec((None, NSA_GROUPS, rows, LANE), per_b),
                       pl.BlockSpec((None, NSA_GROUPS, 8, n_sel_pad), per_b)],
            scratch_shapes=[pltpu.VMEM((2, pp) + p.shape[1:], F32) for p in pools]
            + [pltpu.SemaphoreType.DMA((2,)) for _ in pools]
            + [pltpu.VMEM((pp * PAGE_SIZE, kv_rank), BF16), pltpu.VMEM((MLA_D_ROPE, pp * PAGE_SIZE), BF16),
               pltpu.VMEM((mrows, 1), F32), pltpu.VMEM((mrows, 1), F32), pltpu.VMEM((mrows, kv_rank), F32)]
            + [pltpu.VMEM((pp // 2 * PAGE_SIZE, LANE), F32) for _ in range(4)]
            + [pltpu.VMEM((nch, 2 * LANE), F32) for _ in range(2)]),
        out_shape=[jax.ShapeDtypeStruct((bd, mrows, kv_rank), F32),
                   jax.ShapeDtypeStruct((bd, NSA_GROUPS, rows, LANE), F32),
                   jax.ShapeDtypeStruct((bd, NSA_GROUPS, 8, n_sel_pad), F32)],
        compiler_params=_cparams("arbitrary", "arbitrary"), name="mla_cmp_decode",
    )(pt, qd_mla, knew, qd, wts['cmp_w1k'], wts['cmp_w1v'], wts['cmp_posk'], wts['cmp_posv'],
      wts['cmp_w2k'], wts['cmp_w2v'], *pools)


def _rank_kernel(score_ref, cur_ref, sel_ref, *, n_sel):
    sc = score_ref[...]
    jj = _iota(sc.shape, 0)

    def body(i, rank):
        ri = score_ref[pl.ds(i, 1), :]
        beats = (ri > sc) | ((ri == sc) & (i < jj))
        return rank + beats.astype(F32)

    rank = lax.fori_loop(0, n_sel, body, jnp.zeros(sc.shape, F32))
    sel = (rank < float(min(SLC_TOP_N, n_sel))) & (jj <= cur_ref[...])
    sel_ref[...] = sel.astype(F32)


def _rank(score_t, cur, *, n_sel):
    full = lambda a: pl.BlockSpec(a.shape, lambda: (0,) * a.ndim)
    return pl.pallas_call(
        functools.partial(_rank_kernel, n_sel=n_sel), in_specs=[full(score_t), full(cur)],
        out_specs=full(score_t), out_shape=jax.ShapeDtypeStruct(score_t.shape, F32), name="rank_decode",
    )(score_t, cur)


def _slc_decode_kernel(pt_ref, cnt_ref, q_ref, sel_ref, selnew_ref, knew_ref, vnew_ref, kpos_ref, expand_ref,
                       k_hbm, v_hbm, o_ref, kbuf, vbuf, ksem, vsem, kscr, vscr, m_scr, l_scr, acc_scr,
                       *, pp, td, past):
    b, s, ns = pl.program_id(0), pl.program_id(1), pl.num_programs(1)
    step = b * ns + s
    slot = step % 2
    pools, bufs, sems = (k_hbm, v_hbm), (kbuf, vbuf), (ksem, vsem)
    active = s * pp < cnt_ref[b]
    wrap = s + 1 == ns
    nb = jnp.minimum(jnp.where(wrap, b + 1, b), pl.num_programs(0) - 1)
    next_active = (step + 1 < pl.num_programs(0) * ns) & (jnp.where(wrap, 0, s + 1) * pp < cnt_ref[nb])

    @pl.when((step == 0) & active)
    def _():
        _start_all(_page_copies(pt_ref, pools, bufs, sems, step, slot, pp), len(pools))

    @pl.when(next_active)
    def _():
        _start_all(_page_copies(pt_ref, pools, bufs, sems, step + 1, 1 - slot, pp), len(pools))

    rows = q_ref.shape[1]
    hrow, trow = _alibi_rows(rows, td)

    @pl.when(s == 0)
    def _():
        m_scr[...] = jnp.full_like(m_scr, NEG)
        l_scr[...] = jnp.zeros_like(l_scr)
        acc_scr[...] = jnp.zeros_like(acc_scr)
        kn = knew_ref[...]
        vn = vnew_ref[...]
        col = _iota((1, kn.shape[0]), 1)
        dist = trow - col
        for g in range(NSA_GROUPS):
            sc = _dot_nt(q_ref[g], kn) - _slope_rows(hrow, g) * dist.astype(F32)
            mask = (dist >= 0) & (col < td) & (selnew_ref[g] > 0.5)
            _softmax_update(jnp.where(mask, sc, NEG), vn, m_scr.at[g], l_scr.at[g], acc_scr.at[g])

    @pl.when(active)
    def _():
        for c in _page_copies(pt_ref, pools, bufs, sems, step, slot, pp):
            c.wait()
        for k in range(pp):
            kscr[:, k * PAGE_SIZE:(k + 1) * PAGE_SIZE] = kbuf[slot, k].astype(BF16)
            vscr[:, k * PAGE_SIZE:(k + 1) * PAGE_SIZE] = vbuf[slot, k].astype(BF16)
        kt = kscr[...]
        vt = vscr[...]
        distf = (past + trow).astype(F32) - kpos_ref[...]
        for g in range(NSA_GROUPS):
            sc = _dot(q_ref[g], kt) - _slope_rows(hrow, g) * distf
            selx = _dot(sel_ref[g], expand_ref[...]) > 0.5
            _softmax_update(jnp.where(selx, sc, NEG), vt, m_scr.at[g], l_scr.at[g], acc_scr.at[g], v_transposed=True)

    @pl.when(s == ns - 1)
    def _():
        o_ref[...] = acc_scr[...] / l_scr[...]


def _slc_decode(pt, counts, qd, sel16, selnew, knew, vnew, kpos, k_pool, v_pool, *, td, pp, past):
    bd = qd.shape[0]
    rows = qd.shape[2]
    n_pages = pt.shape[0] // bd
    tk = pp * PAGE_SIZE
    expand = (jnp.arange(tk)[None, :] // SLC_BLOCK == jnp.arange(tk // SLC_BLOCK)[:, None]).astype(BF16)
    per_b = lambda b, s, pt, cnt: (b, 0, 0, 0)
    per_b3 = lambda b, s, pt, cnt: (b, 0, 0)
    in_specs = ([pl.BlockSpec((None,) + qd.shape[1:], per_b),
                 pl.BlockSpec((None, None) + sel16.shape[2:], lambda b, s, pt, cnt: (b, s, 0, 0, 0)),
                 pl.BlockSpec((None,) + selnew.shape[1:], per_b),
                 pl.BlockSpec((None,) + knew.shape[1:], per_b3), pl.BlockSpec((None,) + vnew.shape[1:], per_b3),
                 pl.BlockSpec((None, None, 1, tk), lambda b, s, pt, cnt: (b, s, 0, 0)),
                 pl.BlockSpec(expand.shape, lambda b, s, pt, cnt: (0, 0))]
                + [pl.BlockSpec(memory_space=pl.ANY) for _ in range(2)])
    kern = functools.partial(_slc_decode_kernel, pp=pp, td=td, past=past)
    return pl.pallas_call(
        kern,
        grid_spec=pltpu.PrefetchScalarGridSpec(
            num_scalar_prefetch=2, grid=(bd, n_pages // pp), in_specs=in_specs,
            out_specs=pl.BlockSpec((None, NSA_GROUPS, rows, LANE), per_b),
            scratch_shapes=[pltpu.VMEM((2, pp) + k_pool.shape[1:], F32) for _ in range(2)]
            + [pltpu.SemaphoreType.DMA((2,)) for _ in range(2)]
            + [pltpu.VMEM((LANE, pp * PAGE_SIZE), BF16) for _ in range(2)]
            + [pltpu.VMEM((NSA_GROUPS, rows, 1), F32) for _ in range(2)] + [pltpu.VMEM((NSA_GROUPS, rows, LANE), F32)]),
        out_shape=jax.ShapeDtypeStruct((bd, NSA_GROUPS, rows, LANE), F32),
        compiler_params=_cparams("arbitrary", "arbitrary"), name="slc_decode",
    )(pt, counts, qd, sel16, selnew, knew, vnew, kpos, expand, k_pool, v_pool)


def _win_decode_kernel(q_ref, kst_ref, vst_ref, knew_ref, vnew_ref, o_ref, *, td, past):
    rows = q_ref.shape[1]
    hrow, trow = _alibi_rows(rows, td)
    nbuf = kst_ref.shape[1]
    kst = kst_ref[...].astype(BF16)
    vst = vst_ref[...].astype(BF16)
    kn = knew_ref[...]
    vn = vnew_ref[...]
    kpos = past - nbuf + _iota((1, nbuf), 1)
    d1 = (past + trow) - kpos
    m1 = (d1 >= 0) & (d1 < WINDOW) & (kpos >= 0)
    col = _iota((1, kn.shape[0]), 1)
    d2 = trow - col
    m2 = (d2 >= 0) & (d2 < WINDOW) & (col < td)
    for g in range(NSA_GROUPS):
        slope = _slope_rows(hrow, g)
        s1 = jnp.where(m1, _dot(q_ref[g], kst) - slope * d1.astype(F32), NEG)
        s2 = jnp.where(m2, _dot_nt(q_ref[g], kn) - slope * d2.astype(F32), NEG)
        m = jnp.maximum(jnp.max(s1, axis=-1, keepdims=True), jnp.max(s2, axis=-1, keepdims=True))
        p1 = jnp.exp(s1 - m)
        p2 = jnp.exp(s2 - m)
        l = jnp.sum(p1, axis=-1, keepdims=True) + jnp.sum(p2, axis=-1, keepdims=True)
        o_ref[g] = (_dot_nt(p1.astype(BF16), vst) + _dot(p2.astype(BF16), vn)) / l


def _win_decode(qd, kst, vst, knew, vnew, *, td, past):
    bd = qd.shape[0]
    rows = qd.shape[2]
    per_b = lambda b: (b, 0, 0, 0)
    per_b3 = lambda b: (b, 0, 0)
    blk3 = lambda a: pl.BlockSpec((None,) + a.shape[1:], per_b3)
    return pl.pallas_call(
        functools.partial(_win_decode_kernel, td=td, past=past), grid=(bd,),
        in_specs=[pl.BlockSpec((None,) + qd.shape[1:], per_b), blk3(kst), blk3(vst), blk3(knew), blk3(vnew)],
        out_specs=pl.BlockSpec((None, NSA_GROUPS, rows, LANE), per_b),
        out_shape=jax.ShapeDtypeStruct((bd, NSA_GROUPS, rows, LANE), F32),
        compiler_params=_cparams("parallel"), name="win_decode",
    )(qd, kst, vst, knew, vnew)


def _pad_rows(a, rows):
    return jnp.pad(a, ((0, 0), (0, rows - a.shape[1]), (0, 0)))


def _sample(x_sample, caches, page_table, w):
    (c_ckv, c_krope, c_cmp_k, c_cmp_v, c_slc_k, c_slc_v, s_win_k, s_win_v, s_conv) = caches
    bd, td, d = x_sample.shape
    n = bd * td
    n_pages = page_table.shape[1]
    past = n_pages * PAGE_SIZE
    n_pool = c_ckv.shape[0]
    kv_rank = c_ckv.shape[-1]
    x = x_sample.reshape(n, d)
    pos = past + jnp.arange(td, dtype=jnp.int32)
    cosq, sinq = _rope_tables(jnp.tile(pos, bd))
    (qrot, qabs, ckv, krp, qn, kc, vc, ks, vs, kw, vw, gn, ga, gb) = _inproj(
        x, w, cosq, sinq, sample=True, tm=n, tab_blocks=1)
    pt = page_table.reshape(-1)
    krope = krp[:, MLA_D_NOPE:MLA_D_NOPE + MLA_D_ROPE]
    qrope = qrot.reshape(bd, td, MLA_HEADS, HEAD_PAD)[..., MLA_D_NOPE:MLA_D_NOPE + MLA_D_ROPE]
    qpad = LANE - MLA_D_ROPE
    qd_mla = jnp.concatenate([qabs.reshape(bd, td, MLA_HEADS, kv_rank), qrope,
                              jnp.zeros((bd, td, MLA_HEADS, qpad), F32)], axis=-1)
    qd_mla = qd_mla.reshape(bd, td * MLA_HEADS, kv_rank + LANE).astype(BF16)
    knew = jnp.concatenate([ckv, krope, jnp.zeros((n, qpad), F32)], axis=-1).reshape(bd, td, -1)
    knew = _pad_rows(knew, 8).astype(BF16)
    pos_minor = lambda c: jnp.moveaxis(c, 1, -1).reshape(c.shape[0], -1, c.shape[1])
    qg = qn.astype(F32).reshape(bd, td, NSA_GROUPS, NSA_HPG, NSA_DH).transpose(0, 2, 3, 1, 4)
    qg = qg.reshape(bd, NSA_GROUPS, NSA_HPG * td, NSA_DH)
    lane_g = (jnp.arange(LANE) // NSA_DH)[None, :] == jnp.arange(NSA_GROUPS)[:, None]
    qd = jnp.where(lane_g[None, :, None, :], jnp.tile(qg, (1, 1, 1, NSA_GROUPS)), 0.0).astype(BF16)
    n_sel = -(-(past + td) // SLC_BLOCK)
    n_sel_pad = -(-n_sel // LANE) * LANE
    o_lat, o_cmp, score = _mla_cmp_decode(pt, qd_mla, knew, qd, c_ckv, pos_minor(c_krope), pos_minor(c_cmp_k),
                                          pos_minor(c_cmp_v), w, td=td, pp=min(32, n_pages), past=past,
                                          n_sel_pad=n_sel_pad)
    cur = jnp.broadcast_to((pos // SLC_BLOCK).astype(jnp.int32), (bd, NSA_GROUPS, td)).reshape(1, -1)
    n_sel8 = -(-n_sel // 8) * 8
    sel_t = _rank(score[:, :, :td, :n_sel8].reshape(-1, n_sel8).T, cur, n_sel=n_sel)
    sel = jnp.pad(sel_t.T, ((0, 0), (0, n_sel_pad - n_sel8))).reshape(bd, NSA_GROUPS, td, n_sel_pad)
    pp_slc = min(32, n_pages)
    bpp = PAGE_SIZE // SLC_BLOCK
    n_past_blk = past // SLC_BLOCK
    sel_pg = sel[..., :n_past_blk].reshape(bd, NSA_GROUPS * td, n_pages, bpp)
    need_page = sel_pg.max(axis=(1, 3)) > 0.5
    counts = need_page.sum(axis=-1).astype(jnp.int32)
    n_before = jnp.cumsum(need_page.astype(jnp.int32), axis=-1)
    u_before = jnp.cumsum(1 - need_page.astype(jnp.int32), axis=-1)
    slot_of = jnp.where(need_page, n_before - 1, counts[:, None] + u_before - 1)
    perm = slot_of[:, None, :] == jnp.arange(n_pages, dtype=jnp.int32)[None, :, None]
    order = jnp.sum(jnp.where(perm, jnp.arange(n_pages, dtype=jnp.int32), 0), axis=-1)
    pt_slc = jnp.sum(jnp.where(perm, page_table[:, None, :], 0), axis=-1).reshape(-1)
    sel_listed = jnp.einsum('bip,bxpc->bxic', perm.astype(BF16), sel_pg.astype(BF16), preferred_element_type=F32)
    bps = pp_slc * bpp
    sel_steps = sel_listed.reshape(bd, NSA_GROUPS, td, n_pages // pp_slc, bps).transpose(0, 3, 1, 2, 4)
    sel16 = jnp.tile(sel_steps, (1, 1, 1, NSA_HPG, 1)).astype(BF16)
    kpos = (order[:, :, None] * PAGE_SIZE + jnp.arange(PAGE_SIZE, dtype=jnp.int32)).astype(F32)
    kpos = kpos.reshape(bd, n_pages // pp_slc, 1, pp_slc * PAGE_SIZE)
    selnew = jnp.tile(jnp.broadcast_to(sel[..., n_past_blk:n_past_blk + 1], (bd, NSA_GROUPS, td, 8)), (1, 1, NSA_HPG, 1))
    new8 = lambda a: _pad_rows(a.reshape(bd, td, LANE), 8).astype(BF16)
    o_slc = _slc_decode(pt_slc, counts, qd, sel16, selnew, new8(ks), new8(vs), kpos, pos_minor(c_slc_k),
                        pos_minor(c_slc_v), td=td, pp=pp_slc, past=past)
    nbuf = s_win_k.shape[1]
    o_win = _win_decode(qd, pos_minor(s_win_k), pos_minor(s_win_v), new8(kw), new8(vw),
                        td=td, past=past)

    def heads_out(o):
        o = o.reshape(bd, NSA_GROUPS, NSA_HPG, td, NSA_GROUPS, NSA_DH)
        o = jnp.stack([o[:, g, :, :, g] for g in range(NSA_GROUPS)], axis=1)
        return o.transpose(0, 3, 1, 2, 4).reshape(n, NSA_HEADS * NSA_DH)

    dff = w['w_gate'].shape[1]
    prev1 = jnp.zeros((bd, td, dff), F32).at[:, 0].set(s_conv[:, 1])
    prev2 = jnp.zeros((bd, td, dff), F32).at[:, 0].set(s_conv[:, 0]).at[:, 1].set(s_conv[:, 1])
    y, g = _finish(x, o_lat.reshape(n, MLA_HEADS * kv_rank), heads_out(o_cmp), heads_out(o_slc), heads_out(o_win),
                   gn, ga, gb, (prev1.reshape(n, dff), prev2.reshape(n, dff)), w, tm=n, period=td, full_g=True,
                   latent=True)
    kv4 = lambda a: a.reshape(1, bd, td, NSA_GROUPS, NSA_DH)
    win = lambda st, new: jnp.concatenate([st, new.reshape(bd, td, NSA_GROUPS, NSA_DH)], axis=1)[None, :, -nbuf:]
    conv_state = jnp.concatenate([s_conv, g.reshape(bd, td, dff)], axis=1)[None, :, -(CONV_W - 1):]
    states = (ckv.reshape(1, bd, td, kv_rank), krope.reshape(1, bd, td, MLA_D_ROPE), kv4(kc), kv4(vc), kv4(ks), kv4(vs),
              win(s_win_k, kw), win(s_win_v, vw), conv_state)
    return y.reshape(bd, td, d), states


def kernel(x_prompt, x_sample, cache_mla_ckv, cache_mla_krope, cache_nsa_cmp_k, cache_nsa_cmp_v, cache_nsa_slc_k, cache_nsa_slc_v, state_win_k, state_win_v, state_ffn_conv, page_table, norm1_g, w_in, q_norm_g, kv_norm_g, w_uq, w_uk, w_uv, cmp_pos_k, cmp_w1_k, cmp_w2_k, cmp_pos_v, cmp_w1_v, cmp_w2_v, w_proj_mla, w_proj_nsa, w_out, norm2_g, w_gate, w_up, conv_w, conv_b, w_down, norm_f_g):
    assert norm1_g.shape[0] == 1, "single-layer trunk"
    p = dict(norm1_g=norm1_g[0], w_in=w_in[0], q_norm_g=q_norm_g[0], kv_norm_g=kv_norm_g[0], w_uq=w_uq[0],
             w_uk=w_uk[0], w_uv=w_uv[0], cmp_pos_k=cmp_pos_k[0], cmp_w1_k=cmp_w1_k[0], cmp_w2_k=cmp_w2_k[0],
             cmp_pos_v=cmp_pos_v[0], cmp_w1_v=cmp_w1_v[0], cmp_w2_v=cmp_w2_v[0], w_proj_mla=w_proj_mla[0],
             w_proj_nsa=w_proj_nsa[0], w_out=w_out[0], norm2_g=norm2_g[0], w_gate=w_gate[0], w_up=w_up[0],
             conv_w=conv_w[0], conv_b=conv_b[0], w_down=w_down[0], norm_f_g=norm_f_g)
    w = _prep_weights(p)
    y_p, ps = _prompt(x_prompt, w)
    caches = (cache_mla_ckv[0], cache_mla_krope[0], cache_nsa_cmp_k[0], cache_nsa_cmp_v[0], cache_nsa_slc_k[0],
              cache_nsa_slc_v[0], state_win_k[0], state_win_v[0], state_ffn_conv[0])
    y_s, ss = _sample(x_sample, caches, page_table, w)
    out = [y_p, y_s]
    for a, b in zip(ps, ss):
        out += [a, b]
    return tuple(out)
```

```python
import functools

import numpy as np
import jax
import jax.numpy as jnp
from jax import lax
from jax.experimental import pallas as pl
from jax.experimental.pallas import tpu as pltpu

MLA_HEADS = 8
MLA_D_NOPE = 64
MLA_D_ROPE = 32
MLA_D_V = 64
ROPE_THETA = 10000.0
MLA_SCALE = (MLA_D_NOPE + MLA_D_ROPE) ** -0.5
NSA_HEADS = 8
NSA_GROUPS = 2
NSA_HPG = NSA_HEADS // NSA_GROUPS
NSA_DH = 64
NSA_SCALE = NSA_DH ** -0.5
CMP_BLOCK = 32
CMP_STRIDE = 16
SLC_BLOCK = 64
SLC_TOP_N = 16
WINDOW = 512
CONV_W = 3
PAGE_SIZE = 128
EPS = 1e-6
NEG = -1e30
FORCE = 1e9

LOG2E = 1.4426950408889634
LANE = 128
AUX_POS_HI, AUX_POS_LO, AUX_BLK0 = 64, 65, 72
HEAD_PAD = 128
CHUNK_FEATS = CMP_STRIDE * NSA_GROUPS * NSA_DH
FFN_BLOCK = 1536
SLOPES =tuple(float(2.0 ** (-8.0 * (h + 1) / NSA_HEADS)) for h in range(NSA_HEADS))
VMEM_LIMIT = 56 * 1024 * 1024
ROW_TILE = 256
MLA_TILES = (512, 512)
SLC_TILES = (256, 512)
CMP_TQ = 128
PAGES_PER_STEP = 32

F32 = jnp.float32
BF16 = jnp.bfloat16
_NT = (((1,), (1,)), ((), ()))


def _cparams(*sem):
    return pltpu.CompilerParams(dimension_semantics=sem, vmem_limit_bytes=VMEM_LIMIT)


def _rms(x, g):
    return x * lax.rsqrt(jnp.mean(x * x, axis=-1, keepdims=True) + EPS) * g


def _dot(a, b):
    return jnp.dot(a, b, preferred_element_type=F32)


def _dot_nt(a, b):
    return lax.dot_general(a, b, _NT, preferred_element_type=F32)


def _dot_exact(a, b):
    return jnp.dot(a, b, preferred_element_type=F32, precision=lax.Precision.HIGHEST)


def _iota(shape, dim):
    return lax.broadcasted_iota(jnp.int32, shape, dim)


def _log2(n):
    assert n > 0 and n & (n - 1) == 0, n
    return n.bit_length() - 1


def _vdiv(x, n):
    return lax.shift_right_logical(x, jnp.full(x.shape, _log2(n), jnp.int32))


def _vmod(x, n):
    assert n & (n - 1) == 0, n
    return x & (n - 1)


_O_CQ, _O_CKV, _O_QN, _O_K6, _O_GA = 0, 384, 640, 1152, 1920


def _inproj_kernel(x_ref, g1_ref, w_ref, qg_ref, kvg_ref, wuq_ref, wk_ref, wv_ref, cos_ref, sin_ref,
                   *outs, sample, q_rank, kv_rank, d_model, tiles_per_seq, q_scale):
    o_gb = _O_GA + d_model
    o_kr = o_gb + d_model
    x = x_ref[...]
    hn = _rms(x, g1_ref[...])
    y = _dot(hn.astype(BF16), w_ref[...])
    cosq = cos_ref[...]
    sinq = sin_ref[...]
    nq = MLA_HEADS * HEAD_PAD
    cqn = _rms(y[:, _O_CQ:_O_CQ + q_rank], qg_ref[...])
    q2 = _dot(cqn.astype(BF16), wuq_ref[...])
    half = MLA_D_ROPE // 2
    lane_h = _iota((x.shape[0], HEAD_PAD), 1)

    def rotated(qh):
        partner = jnp.where(lane_h < MLA_D_NOPE + half, -pltpu.roll(qh, HEAD_PAD - half, 1), pltpu.roll(qh, half, 1))
        return (qh * cosq + partner * sinq) * q_scale

    ckv = _rms(y[:, _O_CKV:_O_CKV + kv_rank], kvg_ref[...])
    kr = y[:, o_kr:o_kr + LANE] * cosq + y[:, o_kr + LANE:o_kr + 2 * LANE] * sinq
    ckv_b = ckv.astype(BF16)
    it = iter(outs)
    if sample:
        qrot_ref, qabs_ref = next(it), next(it)
        for h in range(MLA_HEADS):
            sl = slice(h * HEAD_PAD, (h + 1) * HEAD_PAD)
            qh = rotated(q2[:, sl])
            qrot_ref[:, sl] = qh
            qabs_ref[:, h * kv_rank:(h + 1) * kv_rank] = _dot(qh.astype(BF16), wk_ref[h])
    else:
        q_ref, k_ref, v_ref = next(it), next(it), next(it)
        knp = _dot(ckv_b, wk_ref[...])
        for h in range(MLA_HEADS):
            sl = slice(h * HEAD_PAD, (h + 1) * HEAD_PAD)
            qh = rotated(q2[:, sl])
            q_ref[:, sl] = qh.astype(BF16)
            k_ref[:, sl] = (knp[:, sl] + kr).astype(BF16)
        ones_lane = (_vmod(_iota((1, nq), 1), HEAD_PAD) == MLA_D_V).astype(F32)
        v_ref[...] = (_dot(ckv_b, wv_ref[...]) + ones_lane).astype(BF16)
    ckv_ref, kr_ref, qn_ref = next(it), next(it), next(it)
    ckv_ref[...] = ckv
    kr_ref[...] = kr
    yq = y[:, _O_QN:_O_QN + NSA_HEADS * NSA_DH] * NSA_SCALE
    if sample:
        qn_ref[...] = yq.astype(BF16)
    else:
        lane_q = _iota((x.shape[0], LANE), 1)
        for h in range(NSA_HEADS):
            pair = yq[:, (h // 2) * LANE:(h // 2 + 1) * LANE]
            if h % 2:
                pair = pltpu.roll(pair, NSA_DH, 1)
            aux_q = jnp.where((lane_q == AUX_POS_HI) | (lane_q == AUX_POS_LO), SLOPES[h], 0.0)
            qn_ref[:, h * LANE:(h + 1) * LANE] = jnp.where(lane_q < NSA_DH, pair, aux_q).astype(BF16)
    for j in range(6):
        yj = y[:, _O_K6 + j * LANE:_O_K6 + (j + 1) * LANE]
        if sample:
            next(it)[...] = yj
        else:
            next(it)[...] = yj.T
    if not sample:
        tm = x.shape[0]
        pos = (pl.program_id(0) % tiles_per_seq) * tm + _iota((tm, 1), 0)
        lane = _iota((tm, LANE), 1)
        onehot = ((lane >= AUX_BLK0) & (_vdiv(pos, SLC_BLOCK) == lane - AUX_BLK0)).astype(F32)
        aux = jnp.where(lane == AUX_POS_HI, (pos - _vmod(pos, 256)).astype(F32),
                        jnp.where(lane == AUX_POS_LO, _vmod(pos, 256).astype(F32), onehot))
        for j in range(6):
            yj = y[:, _O_K6 + j * LANE:_O_K6 + (j + 1) * LANE]
            ref = next(it)
            if j in (2, 3, 4, 5):
                tail = aux if j in (2, 4) else (lane == NSA_DH).astype(F32)
                ref[:, :LANE] = jnp.where(lane < NSA_DH, yj, tail).astype(BF16)
                ref[:, LANE:] = jnp.where(lane < NSA_DH, pltpu.roll(yj, NSA_DH, 1), tail).astype(BF16)
            else:
                ref[...] = yj.astype(BF16)
    gn_ref, ga_ref, gb_ref = next(it), next(it), next(it)
    gn_ref[...] = jax.nn.sigmoid(y[:, o_kr + 2 * LANE:o_kr + 3 * LANE])
    ga_ref[...] = jax.nn.sigmoid(y[:, _O_GA:_O_GA + d_model])
    gb_ref[...] = jax.nn.sigmoid(y[:, o_gb:o_gb + d_model])


def _inproj(x, wts, cosq, sinq, *, sample, tm, tab_blocks):
    n, d = x.shape
    q_rank, kv_rank = wts['q_norm_g'].shape[1], wts['kv_norm_g'].shape[1]
    nq = MLA_HEADS * HEAD_PAD
    wk = wts['w_ukT'] if sample else wts['w_ukp']
    row = lambda i: (i, 0)
    const2 = lambda i: (0, 0)
    tab_map = (lambda i: (i % tab_blocks, 0))
    in_specs = [
        pl.BlockSpec((tm, d), row),
        pl.BlockSpec((1, d), const2),
        pl.BlockSpec(wts['w_in'].shape, const2),
        pl.BlockSpec((1, q_rank), const2),
        pl.BlockSpec((1, kv_rank), const2),
        pl.BlockSpec(wts['w_uq2'].shape, const2),
        pl.BlockSpec(wk.shape, (lambda i: (0, 0, 0)) if sample else const2),
        pl.BlockSpec(wts['w_uvf'].shape, const2),
        pl.BlockSpec((tm, LANE), tab_map),
        pl.BlockSpec((tm, LANE), tab_map),
    ]
    shapes = []
    if sample:
        shapes += [(nq, F32), (MLA_HEADS * kv_rank, F32)]
    else:
        shapes += [(nq, BF16), (nq, BF16), (nq, BF16)]
    shapes += [(kv_rank, F32), (LANE, F32), (NSA_HEADS * (NSA_DH if sample else LANE), BF16)]
    n_lead = len(shapes)
    shapes += [(LANE, F32)] * 6
    if not sample:
        shapes += [(LANE, BF16), (LANE, BF16)] + [(2 * LANE, BF16)] * 4
        assert AUX_BLK0 + -(-tab_blocks * tm // SLC_BLOCK) <= LANE, "block one-hot must fit the aux lanes"
    shapes += [(LANE, F32), (d, F32), (d, F32)]
    out_shape = [jax.ShapeDtypeStruct((n, w), dt) for w, dt in shapes]
    out_specs = [pl.BlockSpec((tm, w), row) for w, _ in shapes]
    if not sample:
        tpb = tab_blocks
        for j in range(n_lead, n_lead + 6):
            out_shape[j] = jax.ShapeDtypeStruct((n // (tpb * tm), LANE, tpb * tm), F32)
            out_specs[j] = pl.BlockSpec((None, LANE, tm), lambda i: (i // tpb, 0, i % tpb))
    q_scale = MLA_SCALE if sample else MLA_SCALE * LOG2E
    kern = functools.partial(_inproj_kernel, sample=sample, q_rank=q_rank, kv_rank=kv_rank, d_model=d,
                             tiles_per_seq=tab_blocks, q_scale=q_scale)
    return pl.pallas_call(
        kern, grid=(n // tm,), in_specs=in_specs, out_specs=out_specs, out_shape=out_shape,
        compiler_params=_cparams("parallel"), name="inproj_sample" if sample else "inproj_prompt",
    )(x, wts['norm1_g'], wts['w_in'], wts['q_norm_g'], wts['kv_norm_g'], wts['w_uq2'], wk, wts['w_uvf'],
      cosq, sinq)


def _stack_heads(q_ref, heads, width):
    parts = [q_ref[:, h * width:(h + 1) * width] for h in heads]
    return parts[0] if len(parts) == 1 else jnp.concatenate(parts, axis=0)


def _flash_kernel(*refs, tq, tk, groups, par, dq, dk, dv, window, has_bias, base2):
    if has_bias:
        q_ref, k_ref, v_ref, bias_ref, o_ref = refs
    else:
        q_ref, k_ref, v_ref, o_ref = refs
    q_start = pl.program_id(1) * tq
    n_hi = (q_start + tq - 1) // tk + 1
    hi_full = (q_start + 1) // tk
    if window:
        n_lo = jnp.maximum(q_start - (window - 1), 0) // tk
        lo_full = (jnp.maximum(q_start + tq - window, 0) + tk - 1) // tk
    else:
        n_lo, lo_full = 0, 0
    e1 = jnp.clip(lo_full, n_lo, n_hi)
    e2 = jnp.clip(hi_full, e1, n_hi)
    ex = jnp.exp2 if base2 else jnp.exp
    for c0 in range(0, len(groups), par):
        chunk = groups[c0:c0 + par]
        qs = []
        for heads, kcol in chunk:
            qg = _stack_heads(q_ref, heads, dq)
            if has_bias:
                bias = bias_ref[:, kcol * LANE:(kcol + 1) * LANE]
                qg = qg + jnp.concatenate([bias] * len(heads), axis=0)
            qs.append(qg)
        rows = qs[0].shape[0]
        qpos = q_start + (_iota((rows, 1), 0) & (tq - 1))

        def step(j, carry, masked):
            k0 = pl.multiple_of(j * tk, tk)
            if masked:
                dist = qpos - (k0 + _iota((1, tk), 1))
                mask = dist >= 0
                if window:
                    mask = mask & (dist < window)
            out = []
            for (heads, kcol), qg, (m, acc) in zip(chunk, qs, carry):
                kt = k_ref[pl.ds(k0, tk), kcol * dk:(kcol + 1) * dk]
                vt = v_ref[pl.ds(k0, tk), kcol * LANE:(kcol + 1) * LANE]
                s = _dot_nt(qg, kt)
                if masked:
                    s = jnp.where(mask, s, NEG)
                m_new = jnp.maximum(m, jnp.max(s, axis=-1, keepdims=True))
                p = ex((s - m_new).astype(BF16))
                acc = ex(m - m_new) * acc + _dot(p, vt)
                out.append((m_new, acc))
            return tuple(out)

        carry = tuple((jnp.full((rows, 1), NEG, F32), jnp.zeros((rows, LANE), F32)) for _ in chunk)
        carry = lax.fori_loop(n_lo, e1, functools.partial(step, masked=True), carry)
        carry = lax.fori_loop(e1, e2, functools.partial(step, masked=False), carry)
        carry = lax.fori_loop(e2, n_hi, functools.partial(step, masked=True), carry)
        for (heads, kcol), (m, acc) in zip(chunk, carry):
            o = acc[:, :dv] * (1.0 / acc[:, dv:dv + 1])
            for hh, h in enumerate(heads):
                o_ref[:, h * dv:(h + 1) * dv] = o[hh * tq:(hh + 1) * tq].astype(o_ref.dtype)


def _flash(q, k, v, bias, *, batch, seq, tq, tk, groups, par, dq, dk, dv, window, base2, name):
    n = q.shape[0]
    has_bias = bias is not None
    n_heads = sum(len(g[0]) for g in groups)
    qrow = lambda b, i: (b * (seq // tq) + i, 0)
    kv = lambda b, i: (b, 0)
    in_specs = [pl.BlockSpec((tq, q.shape[1]), qrow), pl.BlockSpec((seq, k.shape[1]), kv),
                pl.BlockSpec((seq, v.shape[1]), kv)]
    args = [q, k, v]
    if has_bias:
        in_specs.append(pl.BlockSpec((tq, bias.shape[1]), qrow))
        args.append(bias)
    kern = functools.partial(_flash_kernel, tq=tq, tk=tk, groups=groups, par=par, dq=dq, dk=dk, dv=dv, window=window,
                             has_bias=has_bias, base2=base2)
    return pl.pallas_call(
        kern, grid=(batch, seq // tq), in_specs=in_specs,
        out_specs=pl.BlockSpec((tq, n_heads * dv), qrow),
        out_shape=jax.ShapeDtypeStruct((n, n_heads * dv), BF16),
        compiler_params=_cparams("parallel", "arbitrary"), name=name,
    )(*args)


def _win_kernel(q_ref, k_ref, v_ref, o_ref, *, tq, groups, dv):
    i = pl.program_id(1)
    rq = _iota((tq, tq), 0)
    ck = _iota((tq, tq), 1)
    bias_own = jnp.where(ck <= rq, 0.0, NEG)
    bias_far = jnp.where(ck > rq, 0.0, NEG) + jnp.where(i >= 2, 0.0, NEG)
    bias_mid = jnp.where(i >= 1, 0.0, NEG)
    starts = (jnp.maximum(i - 2, 0) * tq, jnp.maximum(i - 1, 0) * tq, i * tq)
    for heads, kcol in groups:
        qg = _stack_heads(q_ref, heads, LANE)
        nh = len(heads)
        ss, vs = [], []
        for k0, bias in zip(starts, (bias_far, None, bias_own)):
            k0 = pl.multiple_of(k0, tq)
            s = _dot_nt(qg, k_ref[pl.ds(k0, tq), kcol * LANE:(kcol + 1) * LANE])
            s = s + (bias_mid if bias is None else jnp.concatenate([bias] * nh, axis=0))
            ss.append(s)
            vs.append(v_ref[pl.ds(k0, tq), kcol * LANE:(kcol + 1) * LANE])
        m = functools.reduce(jnp.maximum, [jnp.max(s, axis=-1, keepdims=True) for s in ss])
        acc = functools.reduce(jnp.add, [_dot(jnp.exp((s - m).astype(BF16)), v) for s, v in zip(ss, vs)])
        o = acc[:, :dv] * (1.0 / acc[:, dv:dv + 1])
        for hh, h in enumerate(heads):
            o_ref[:, h * dv:(h + 1) * dv] = o[hh * tq:(hh + 1) * tq].astype(o_ref.dtype)


def _win_prompt(q, k, v, *, batch, seq, tq, groups, dv):
    assert WINDOW == 2 * tq and seq % tq == 0
    n = q.shape[0]
    n_heads = sum(len(g[0]) for g in groups)
    qrow = lambda b, i: (b * (seq // tq) + i, 0)
    kv = lambda b, i: (b, 0)
    return pl.pallas_call(
        functools.partial(_win_kernel, tq=tq, groups=groups, dv=dv), grid=(batch, seq // tq),
        in_specs=[pl.BlockSpec((tq, q.shape[1]), qrow), pl.BlockSpec((seq, k.shape[1]), kv),
                  pl.BlockSpec((seq, v.shape[1]), kv)],
        out_specs=pl.BlockSpec((tq, n_heads * dv), qrow),
        out_shape=jax.ShapeDtypeStruct((n, n_heads * dv), BF16),
        compiler_params=_cparams("parallel", "arbitrary"), name="win_prompt",
    )(q, k, v)


def _compress_finish(y, posy, w2_ref):
    rows = y.shape[0]
    a = y[:, :LANE]
    b = pltpu.roll(y[:, LANE:], rows - 1, 0)
    pos = posy[0:1, :LANE] + posy[1:2, LANE:]
    hid = jax.nn.gelu(a + b + pos)
    return _dot(hid.astype(BF16), w2_ref[...])


def _compress_prompt_kernel(xk_ref, xv_ref, w1k_ref, w1v_ref, pk_ref, pv_ref, w2k_ref, w2v_ref, ok_ref, ov_ref):
    yk = _dot(xk_ref[...], w1k_ref[...])
    yv = _dot(xv_ref[...], w1v_ref[...])
    ok_ref[...] = _compress_finish(yk, _dot(pk_ref[...], w1k_ref[...]), w2k_ref).astype(BF16)
    ov_ref[...] = _compress_finish(yv, _dot(pv_ref[...], w1v_ref[...]), w2v_ref).astype(BF16)


def _compress_prompt(kc, vc, wts, *, batch, seq):
    nch = seq // CMP_STRIDE
    xk = kc.reshape(batch * nch, CHUNK_FEATS)
    xv = vc.reshape(batch * nch, CHUNK_FEATS)
    row = lambda b: (b, 0)
    c2 = lambda b: (0, 0)
    wspec = pl.BlockSpec((CHUNK_FEATS, 2 * LANE), c2)
    pspec = pl.BlockSpec((8, CHUNK_FEATS), c2)
    w2spec = pl.BlockSpec((LANE, LANE), c2)
    return pl.pallas_call(
        _compress_prompt_kernel, grid=(batch,),
        in_specs=[pl.BlockSpec((nch, CHUNK_FEATS), row), pl.BlockSpec((nch, CHUNK_FEATS), row),
                  wspec, wspec, pspec, pspec, w2spec, w2spec],
        out_specs=[pl.BlockSpec((nch, LANE), row)] * 2,
        out_shape=[jax.ShapeDtypeStruct((batch * nch, LANE), BF16)] * 2,
        compiler_params=_cparams("parallel"), name="compress_prompt",
    )(xk, xv, wts['cmp_w1k'], wts['cmp_w1v'], wts['cmp_posk'], wts['cmp_posv'], wts['cmp_w2k'], wts['cmp_w2v'])


def _overlap(n_rows, n_sel):
    c = _iota((n_rows, n_sel), 0) * CMP_STRIDE
    j = _iota((n_rows, n_sel), 1) * SLC_BLOCK
    return ((c < j + SLC_BLOCK) & (c + CMP_BLOCK > j)).astype(F32)


def _force_scores(score, cur, jj):
    forced = (jj == 0) | (jj == cur) | (jj == cur - 1)
    score = jnp.where(forced, FORCE, score)
    return jnp.where(jj <= cur, score, NEG)


def _cmp_prompt_kernel(q_ref, k_ref, v_ref, o_ref, bias_ref, *, tq, n_cmp, n_sel):
    q_start = pl.program_id(1) * tq
    ncp = k_ref.shape[0]
    rows = NSA_HPG * tq
    qpos = q_start + (_iota((rows, 1), 0) & (tq - 1))
    cidx = _iota((1, ncp), 1)
    dist = qpos - (cidx * CMP_STRIDE + CMP_BLOCK - 1)
    mask = (dist >= 0) & (cidx < n_cmp)
    distf = dist.astype(F32)
    hrow = _vdiv(_iota((rows, 1), 0), tq)
    nsp = -(-n_sel // 8) * 8
    cur = _vdiv(q_start + _iota((1, tq), 1), SLC_BLOCK)
    jj = _iota((nsp, tq), 0)
    cb = _iota((nsp, ncp), 1) * CMP_STRIDE
    jb = _iota((nsp, ncp), 0) * SLC_BLOCK
    ov_t = ((cb < jb + SLC_BLOCK) & (cb + CMP_BLOCK > jb)).astype(F32)
    for g in range(NSA_GROUPS):
        heads = range(g * NSA_HPG, (g + 1) * NSA_HPG)
        qg = jnp.concatenate([q_ref[:, h * LANE:h * LANE + NSA_DH] for h in heads], axis=0)
        slope = jnp.zeros((rows, 1), F32)
        for hh, h in enumerate(heads):
            slope = jnp.where(hrow == hh, SLOPES[h], slope)
        s = _dot_nt(qg, k_ref[:, g * NSA_DH:(g + 1) * NSA_DH]) - slope * distf
        s = jnp.where(mask, s, NEG)
        m = jnp.max(s, axis=-1, keepdims=True)
        p = jnp.where(mask, jnp.exp(s - m), 0.0)
        l = jnp.sum(p, axis=-1, keepdims=True)
        p = p / jnp.where(l > 0.0, l, 1.0)
        o = _dot(p.astype(BF16), v_ref[:, g * NSA_DH:(g + 1) * NSA_DH])
        imp = p[0:tq]
        for hh in range(1, NSA_HPG):
            imp = imp + p[hh * tq:(hh + 1) * tq]
        for hh, h in enumerate(heads):
            o_ref[:, h * NSA_DH:(h + 1) * NSA_DH] = o[hh * tq:(hh + 1) * tq].astype(BF16)
        score = lax.dot_general(ov_t, imp, _NT, preferred_element_type=F32, precision=lax.Precision.HIGHEST)
        score = _force_scores(score, cur, jj)
        rank = jnp.zeros((nsp, tq), F32)
        for i in range(n_sel):
            ri = score[i:i + 1, :]
            beats = (ri > score) | ((ri == score) & (i < jj))
            rank = rank + beats.astype(F32)
        sel = (rank < float(min(SLC_TOP_N, n_sel))) & (jj <= cur)
        bias_t = jnp.where(sel | (jj >= n_sel), 0.0, NEG)
        bias_t = jnp.concatenate([jnp.zeros((AUX_BLK0, tq), F32), bias_t,
                                  jnp.zeros((LANE - AUX_BLK0 - nsp, tq), F32)], axis=0)
        bias_ref[:, g * LANE:(g + 1) * LANE] = bias_t.T.astype(BF16)


def _cmp_prompt(qn, kcc, vcc, *, batch, seq, tq):
    n = qn.shape[0]
    nch = seq // CMP_STRIDE
    n_cmp = nch - CMP_BLOCK // CMP_STRIDE + 1
    n_sel = -(-seq // SLC_BLOCK)
    qrow = lambda b, i: (b * (seq // tq) + i, 0)
    kv = lambda b, i: (b, 0)
    kern = functools.partial(_cmp_prompt_kernel, tq=tq, n_cmp=n_cmp, n_sel=n_sel)
    return pl.pallas_call(
        kern, grid=(batch, seq // tq),
        in_specs=[pl.BlockSpec((tq, qn.shape[1]), qrow), pl.BlockSpec((nch, LANE), kv), pl.BlockSpec((nch, LANE), kv)],
        out_specs=[pl.BlockSpec((tq, NSA_HEADS * NSA_DH), qrow), pl.BlockSpec((tq, NSA_GROUPS * LANE), qrow)],
        out_shape=[jax.ShapeDtypeStruct((n, NSA_HEADS * NSA_DH), BF16),
                   jax.ShapeDtypeStruct((n, NSA_GROUPS * LANE), BF16)],
        compiler_params=_cparams("parallel", "arbitrary"), name="cmp_prompt",
    )(qn, kcc, vcc)


def _finish_kernel(*refs, period, latent, has_state):
    it = iter(refs)
    x_ref, omla_ref, ocmp_ref, oslc_ref, owin_ref, gn_ref, ga_ref, gb_ref = (next(it) for _ in range(8))
    prev1_ref, prev2_ref = (next(it), next(it)) if has_state else (None, None)
    gx_ref = next(it)
    wuv_ref = next(it) if latent else None
    (wpm_ref, wpn_ref, wo_ref, g2_ref, wg_ref, wu_ref, cw_ref, cb_ref, wd_ref, gf_ref,
     y_ref, gout_ref, carry_ref) = it
    tm = x_ref.shape[0]
    gn = gn_ref[...]
    gn_hi = gn.astype(BF16)
    gn_lo = (gn - gn_hi.astype(F32)).astype(BF16)
    gexp = _dot(gn_hi, gx_ref[...]) + _dot(gn_lo, gx_ref[...])
    w = NSA_HEADS * NSA_DH
    o_nsa = (gexp[:, 0:w] * ocmp_ref[...].astype(F32) + gexp[:, w:2 * w] * oslc_ref[...].astype(F32)
             + gexp[:, 2 * w:3 * w] * owin_ref[...].astype(F32))
    o_mla = omla_ref[...].astype(BF16)
    if latent:
        o_mla = _dot(o_mla, wuv_ref[...]).astype(BF16)
    merged = (ga_ref[...] * _dot(o_mla, wpm_ref[...])
              + gb_ref[...] * _dot(o_nsa.astype(BF16), wpn_ref[...]))
    x1 = x_ref[...] + _dot(merged.astype(BF16), wo_ref[...])
    h2 = _rms(x1, g2_ref[...]).astype(BF16)
    row = _iota((tm, 1), 0)
    i = pl.program_id(0)
    t = _vmod(i * tm + row, period)
    carried = period > tm
    if carried:
        @pl.when(i == 0)
        def _():
            carry_ref[...] = jnp.zeros_like(carry_ref)
    dff = wg_ref.shape[1]
    x2 = x1
    for c0 in range(0, dff, FFN_BLOCK):
        cs = slice(c0, min(c0 + FFN_BLOCK, dff))
        g = _dot(h2, wg_ref[:, cs])
        u = _dot(h2, wu_ref[:, cs])
        g1 = pltpu.roll(g, 1, 0)
        g2 = pltpu.roll(g, 2, 0)
        if carried:
            c = carry_ref[:, cs]
            g1 = jnp.where(row == 0, c[7:8], g1)
            g2 = jnp.where(row == 0, c[6:7], jnp.where(row == 1, c[7:8], g2))
            carry_ref[:, cs] = g[tm - 8:tm]
        g1 = jnp.where(t >= 1, g1, prev1_ref[:, cs] if has_state else 0.0)
        g2 = jnp.where(t >= 2, g2, prev2_ref[:, cs] if has_state else 0.0)
        cw = cw_ref[:, cs]
        conv = cb_ref[:, cs] + cw[0:1] * g2 + cw[1:2] * g1 + cw[2:3] * g
        act = (jax.nn.silu(conv) * u).astype(BF16)
        x2 = x2 + _dot(act, wd_ref[cs, :])
        gout_ref[:, cs] = g[tm - 8:tm] if gout_ref.shape[0] == 8 else g
    y_ref[...] = _rms(x2, gf_ref[...])


def _finish(x, omla, ocmp, oslc, owin, gn, ga, gb, state_rows, wts, *, tm, period, full_g, latent):
    n, d = x.shape
    dff = wts['w_gate'].shape[1]
    row = lambda i: (i, 0)
    c2 = lambda i: (0, 0)
    acts = [x, omla, ocmp, oslc, owin, gn, ga, gb] + (list(state_rows) if state_rows is not None else [])
    consts = [wts['gate_expand']] + ([wts['w_uvbd']] if latent else []) + [
        wts['w_proj_mla'], wts['w_proj_nsa'], wts['w_out'], wts['norm2_g'],
        wts['w_gate'], wts['w_up'], wts['conv_w'], wts['conv_b'], wts['w_down'], wts['norm_f_g']]
    ins = acts + consts
    in_specs = [pl.BlockSpec((tm, a.shape[1]), row) for a in acts] + [pl.BlockSpec(a.shape, c2) for a in consts]
    g_rows = n if full_g else (n // tm) * 8
    g_blk = tm if full_g else 8
    kern = functools.partial(_finish_kernel, period=period, latent=latent, has_state=state_rows is not None)
    return pl.pallas_call(
        kern, grid=(n // tm,), in_specs=in_specs,
        out_specs=[pl.BlockSpec((tm, d), row), pl.BlockSpec((g_blk, dff), row)],
        out_shape=[jax.ShapeDtypeStruct((n, d), F32), jax.ShapeDtypeStruct((g_rows, dff), F32)],
        scratch_shapes=[pltpu.VMEM((8, dff), F32)],
        compiler_params=_cparams("arbitrary"), name="finish_full" if full_g else "finish_tiled",
    )(*ins)


def _swap_halves(w):
    hlf = w.shape[-1] // 2
    return jnp.concatenate([-w[..., hlf:], w[..., :hlf]], axis=-1)


def _prep_weights(p):
    d = p['w_in'].shape[0]
    q_rank, kv_rank = p['q_norm_g'].shape[-1], p['kv_norm_g'].shape[-1]
    sizes = [q_rank, kv_rank, MLA_D_ROPE, NSA_HEADS * NSA_DH] + [2 * NSA_GROUPS * NSA_DH] * 3 + [3 * NSA_HEADS, d, d]
    cuts = np.cumsum(sizes)[:-1].tolist()
    cq, ckv, kr, qn, kvc, kvs, kvw, gn, ga, gb = jnp.split(p['w_in'], cuts, axis=-1)
    assert _O_CKV == q_rank and _O_QN == q_rank + kv_rank
    lo, hi = MLA_D_NOPE, HEAD_PAD - MLA_D_NOPE - MLA_D_ROPE
    place = lambda w: jnp.pad(w, ((0, 0), (lo, hi)))
    gnp = jnp.pad(gn, ((0, 0), (0, LANE - gn.shape[1])))
    w_in = jnp.concatenate([cq, ckv, qn, kvc, kvs, kvw, ga, gb, place(kr), place(_swap_halves(kr)), gnp], axis=1)
    w = {'w_in': w_in.astype(BF16)}
    for k in ('norm1_g', 'q_norm_g', 'kv_norm_g', 'norm2_g', 'conv_b'):
        w[k] = p[k].reshape(1, -1)
    w['norm_f_g'] = p['norm_f_g'].reshape(1, -1)
    w['conv_w'] = jnp.pad(p['conv_w'], ((0, 8 - CONV_W), (0, 0)))
    uq = p['w_uq']
    w['w_uq2'] = jnp.pad(uq, ((0, 0), (0, 0), (0, HEAD_PAD - uq.shape[-1]))).reshape(q_rank, -1).astype(BF16)
    uk = p['w_uk']
    w['w_ukp'] = jnp.pad(uk, ((0, 0), (0, 0), (0, HEAD_PAD - MLA_D_NOPE))).reshape(kv_rank, -1).astype(BF16)
    w['w_ukT'] = jnp.pad(jnp.transpose(uk, (1, 2, 0)), ((0, 0), (0, HEAD_PAD - MLA_D_NOPE), (0, 0))).astype(BF16)
    w['w_uvf'] = jnp.pad(p['w_uv'], ((0, 0), (0, 0), (0, HEAD_PAD - MLA_D_V))).reshape(kv_rank, -1).astype(BF16)
    eye_h = jnp.eye(MLA_HEADS, dtype=F32)
    w['w_uvbd'] = jnp.einsum('rhv,hk->hrkv', p['w_uv'], eye_h).reshape(MLA_HEADS * kv_rank, -1).astype(BF16)
    eye_g = jnp.eye(NSA_GROUPS, dtype=F32)
    for nm in ('k', 'v'):
        w1 = p['cmp_w1_' + nm].reshape(2, CMP_STRIDE, NSA_DH, -1)
        big = jnp.einsum('ajdh,gk->jgdakh', w1, eye_g)
        w['cmp_w1' + nm] = big.reshape(CHUNK_FEATS, -1).astype(BF16)
        pos = p['cmp_pos_' + nm].reshape(2, CMP_STRIDE, 1, NSA_DH)
        pos = jnp.broadcast_to(pos, (2, CMP_STRIDE, NSA_GROUPS, NSA_DH)).reshape(2, CHUNK_FEATS)
        w['cmp_pos' + nm] = jnp.pad(pos, ((0, 6), (0, 0))).astype(BF16)
        w2 = p['cmp_w2_' + nm]
        w['cmp_w2' + nm] = jnp.einsum('hd,gk->ghkd', w2, eye_g).reshape(NSA_GROUPS * w2.shape[0], -1).astype(BF16)
    ge = np.zeros((LANE, 3 * NSA_HEADS * NSA_DH), np.float32)
    for h in range(NSA_HEADS):
        for i in range(3):
            ge[h * 3 + i, i * NSA_HEADS * NSA_DH + h * NSA_DH:i * NSA_HEADS * NSA_DH + (h + 1) * NSA_DH] = 1.0
    w['gate_expand'] = jnp.asarray(ge)
    for k in ('w_proj_mla', 'w_proj_nsa', 'w_out', 'w_gate', 'w_up', 'w_down'):
        w[k] = p[k].astype(BF16)
    return w


def _rope_tables(pos):
    inv = ROPE_THETA ** (-jnp.arange(0, MLA_D_ROPE, 2, dtype=F32) / MLA_D_ROPE)
    ang = pos.astype(F32)[:, None] * inv[None, :]
    cos, sin = jnp.cos(ang), jnp.sin(ang)
    n = pos.shape[0]
    pad = jnp.zeros((n, HEAD_PAD - MLA_D_NOPE - MLA_D_ROPE), F32)
    cosq = jnp.concatenate([jnp.ones((n, MLA_D_NOPE), F32), cos, cos, pad], axis=1)
    sinq = jnp.concatenate([jnp.zeros((n, MLA_D_NOPE), F32), sin, sin, pad], axis=1)
    return cosq, sinq


_NSA_GROUPS_SPEC = tuple((tuple(range(g * NSA_HPG, (g + 1) * NSA_HPG)), g) for g in range(NSA_GROUPS))
_MLA_GROUPS_SPEC = tuple(((h,), h) for h in range(MLA_HEADS))


def _prompt(x_prompt, w):
    b, t, d = x_prompt.shape
    n = b * t
    x = x_prompt.reshape(n, d)
    tm = ROW_TILE
    cosq, sinq = _rope_tables(jnp.arange(t, dtype=jnp.int32))
    (q_mla, k_mla, v_mla, ckv, krp, qn, kc, vc, ks, vs, kw, vw, kc_b, vc_b, ks_b, vs_b, kw_b, vw_b, gn, ga, gb) = _inproj(
        x, w, cosq, sinq, sample=False, tm=tm, tab_blocks=t // tm)
    o_mla = _flash(q_mla, k_mla, v_mla, None, batch=b, seq=t, tq=MLA_TILES[0], tk=MLA_TILES[1], groups=_MLA_GROUPS_SPEC, par=8,
                   dq=HEAD_PAD, dk=HEAD_PAD, dv=MLA_D_V, window=0, base2=True, name="mla_prompt")
    kcc, vcc = _compress_prompt(kc_b, vc_b, w, batch=b, seq=t)
    o_cmp, sel_bias = _cmp_prompt(qn, kcc, vcc, batch=b, seq=t, tq=CMP_TQ)
    o_slc = _flash(qn, ks_b, vs_b, sel_bias, batch=b, seq=t, tq=SLC_TILES[0], tk=SLC_TILES[1], groups=_NSA_GROUPS_SPEC, par=2,
                   dq=LANE, dk=LANE, dv=NSA_DH, window=0, base2=False, name="slc_prompt")
    o_win = _win_prompt(qn, kw_b, vw_b, batch=b, seq=t, tq=WINDOW // 2, groups=_NSA_GROUPS_SPEC, dv=NSA_DH)
    dff = w['w_gate'].shape[1]
    y, gtail = _finish(x, o_mla, o_cmp, o_slc, o_win, gn, ga, gb, None, w, tm=tm, period=t, full_g=False,
                       latent=False)
    kv4 = lambda a: a.reshape(1, b, NSA_GROUPS, NSA_DH, a.shape[-1]).transpose(0, 1, 4, 2, 3)
    n_keep = min(WINDOW, t)
    kw, vw = kw[:, :, t - n_keep:], vw[:, :, t - n_keep:]
    conv_state = gtail.reshape(b, t // tm, 8, dff)[:, -1, 8 - (CONV_W - 1):, :]
    states = (ckv.reshape(1, b, t, -1), krp[:, MLA_D_NOPE:MLA_D_NOPE + MLA_D_ROPE].reshape(1, b, t, MLA_D_ROPE),
              kv4(kc), kv4(vc), kv4(ks), kv4(vs), kv4(kw), kv4(vw), conv_state[None])
    return y.reshape(b, t, d), states


def _page_copies(pt_ref, pools, bufs, sems, step, slot, pp):
    copies = []
    for k in range(pp):
        page = pt_ref[step * pp + k]
        for pool, buf, sem in zip(pools, bufs, sems):
            copies.append(pltpu.make_async_copy(pool.at[page], buf.at[slot, k], sem.at[slot]))
    return copies


def _start_all(copies, n_pools):
    for i, c in enumerate(copies):
        c.start(priority=(i // n_pools) % 2)


def _stream_pages(pt_ref, pools, bufs, sems, pp):
    step = pl.program_id(0) * pl.num_programs(1) + pl.program_id(1)
    total = pl.num_programs(0) * pl.num_programs(1)
    slot = step % 2

    @pl.when(step == 0)
    def _():
        _start_all(_page_copies(pt_ref, pools, bufs, sems, step, slot, pp), len(pools))

    @pl.when(step + 1 < total)
    def _():
        _start_all(_page_copies(pt_ref, pools, bufs, sems, step + 1, 1 - slot, pp), len(pools))

    for c in _page_copies(pt_ref, pools, bufs, sems, step, slot, pp):
        c.wait()
    return slot


def _softmax_update(sc, v, m_scr, l_scr, acc_scr, v_transposed=False):
    m_old = m_scr[...]
    m_new = jnp.maximum(m_old, jnp.max(sc, axis=-1, keepdims=True))
    p = jnp.exp(sc - m_new)
    alpha = jnp.exp(m_old - m_new)
    l_scr[...] = alpha * l_scr[...] + jnp.sum(p, axis=-1, keepdims=True)
    pv = _dot_nt(p.astype(BF16), v) if v_transposed else _dot(p.astype(BF16), v)
    acc_scr[...] = alpha * acc_scr[...] + pv
    m_scr[...] = m_new


def _mla_decode_step(slot, q_ref, knew_ref, o_ref, cbuf, rbuf, kscr, krscr, m_scr, l_scr, acc_scr, *, pp, td, kv_rank):
    s = pl.program_id(1)
    q = q_ref[...]
    rows = q.shape[0]

    @pl.when(s == 0)
    def _():
        m_scr[...] = jnp.full_like(m_scr, NEG)
        l_scr[...] = jnp.zeros_like(l_scr)
        acc_scr[...] = jnp.zeros_like(acc_scr)
        kn = knew_ref[...]
        trow = _vdiv(_iota((rows, 1), 0), MLA_HEADS)
        col = _iota((1, kn.shape[0]), 1)
        sc = jnp.where((col <= trow) & (col < td), _dot_nt(q, kn), NEG)
        _softmax_update(sc, kn[:, :kv_rank], m_scr, l_scr, acc_scr)

    for k in range(pp):
        kscr[k * PAGE_SIZE:(k + 1) * PAGE_SIZE, :] = cbuf[slot, k].astype(BF16)
        krscr[:, k * PAGE_SIZE:(k + 1) * PAGE_SIZE] = rbuf[slot, k].astype(BF16)
    kt = kscr[...]
    sc = _dot_nt(q[:, :kv_rank], kt) + _dot(q[:, kv_rank:kv_rank + MLA_D_ROPE], krscr[...])
    _softmax_update(sc, kt, m_scr, l_scr, acc_scr)

    @pl.when(s == pl.num_programs(1) - 1)
    def _():
        o_ref[...] = acc_scr[...] / l_scr[...]


def _alibi_rows(rows, td):
    r = _iota((rows, 1), 0)
    return _vdiv(r, td), _vmod(r, td)


def _slope_rows(hrow, g):
    slope = jnp.zeros(hrow.shape, F32)
    for hh in range(NSA_HPG):
        slope = jnp.where(hrow == hh, SLOPES[g * NSA_HPG + hh], slope)
    return slope


def _cmp_decode_step(slot, q_ref, w1k_ref, w1v_ref, pk_ref, pv_ref, w2k_ref, w2v_ref, o_ref, score_ref,
                     kbuf, vbuf, kp0, kp1, vp0, vp1, yk_scr, yv_scr, *, pp, td, past, n_cmp, n_sel_pad):
    s = pl.program_id(1)
    cpp = PAGE_SIZE // CMP_STRIDE
    hp = pp // 2
    for half, (kp_scr, vp_scr) in enumerate(((kp0, vp0), (kp1, vp1))):
        for k in range(hp):
            kp_scr[k * PAGE_SIZE:(k + 1) * PAGE_SIZE, :] = kbuf[slot, half * hp + k].T
            vp_scr[k * PAGE_SIZE:(k + 1) * PAGE_SIZE, :] = vbuf[slot, half * hp + k].T
    chunk_rows = lambda scr: jnp.concatenate(
        [scr[pl.ds(j, hp * cpp, stride=CMP_STRIDE), :] for j in range(CMP_STRIDE)], axis=1).astype(BF16)
    for half, (kp_scr, vp_scr) in enumerate(((kp0, vp0), (kp1, vp1))):
        r0 = pl.multiple_of(s * (pp * cpp) + half * (hp * cpp), hp * cpp)
        yk_scr[pl.ds(r0, hp * cpp), :] = _dot(chunk_rows(kp_scr), w1k_ref[...])
        yv_scr[pl.ds(r0, hp * cpp), :] = _dot(chunk_rows(vp_scr), w1v_ref[...])

    @pl.when(s == pl.num_programs(1) - 1)
    def _():
        kcc = _compress_finish(yk_scr[...], _dot(pk_ref[...], w1k_ref[...]), w2k_ref).astype(BF16)
        vcc = _compress_finish(yv_scr[...], _dot(pv_ref[...], w1v_ref[...]), w2v_ref).astype(BF16)
        ncp = kcc.shape[0]
        rows = NSA_HPG * td
        hrow, trow = _alibi_rows(rows, td)
        cidx = _iota((1, ncp), 1)
        dist = (past + trow) - (cidx * CMP_STRIDE + CMP_BLOCK - 1)
        mask = (dist >= 0) & (cidx < n_cmp)
        distf = dist.astype(F32)
        tsum = (_vmod(_iota((8, rows), 1), td) == _iota((8, rows), 0)).astype(F32)
        ov = _overlap(ncp, n_sel_pad)
        t8 = _iota((8, 1), 0)
        cur = _vdiv(past + t8, SLC_BLOCK)
        jj = _iota((8, n_sel_pad), 1)
        for g in range(NSA_GROUPS):
            sc = _dot_nt(q_ref[g], kcc) - _slope_rows(hrow, g) * distf
            sc = jnp.where(mask, sc, NEG)
            m = jnp.max(sc, axis=-1, keepdims=True)
            p = jnp.where(mask, jnp.exp(sc - m), 0.0)
            l = jnp.sum(p, axis=-1, keepdims=True)
            p = p / jnp.where(l > 0.0, l, 1.0)
            o_ref[g] = _dot(p.astype(BF16), vcc)
            imp = _dot_exact(tsum, p)
            score_ref[g] = _force_scores(_dot_exact(imp, ov), cur, jj)


def _mla_cmp_decode_kernel(pt_ref, qm_ref, knew_ref, qc_ref, w1k_ref, w1v_ref, pk_ref, pv_ref, w2k_ref, w2v_ref,
                           ckv_hbm, kr_hbm, ck_hbm, cv_hbm, olat_ref, ocmp_ref, score_ref,
                           cbuf, rbuf, kbuf, vbuf, csem, rsem, ksem, vsem,
                           kscr, krscr, m_scr, l_scr, acc_scr, kp0, kp1, vp0, vp1, yk_scr, yv_scr,
                           *, pp, td, kv_rank, past, n_cmp, n_sel_pad):
    slot = _stream_pages(pt_ref, (ckv_hbm, kr_hbm, ck_hbm, cv_hbm), (cbuf, rbuf, kbuf, vbuf),
                         (csem, rsem, ksem, vsem), pp)
    _mla_decode_step(slot, qm_ref, knew_ref, olat_ref, cbuf, rbuf, kscr, krscr, m_scr, l_scr, acc_scr,
                     pp=pp, td=td, kv_rank=kv_rank)
    _cmp_decode_step(slot, qc_ref, w1k_ref, w1v_ref, pk_ref, pv_ref, w2k_ref, w2v_ref, ocmp_ref, score_ref,
                     kbuf, vbuf, kp0, kp1, vp0, vp1, yk_scr, yv_scr, pp=pp, td=td, past=past, n_cmp=n_cmp,
                     n_sel_pad=n_sel_pad)


def _mla_cmp_decode(pt, qd_mla, knew, qd, ckv_pool, kr_pool, k_pool, v_pool, wts, *, td, pp, past, n_sel_pad):
    bd, mrows, qw = qd_mla.shape
    kv_rank = ckv_pool.shape[-1]
    n_pages = pt.shape[0] // bd
    cpp = PAGE_SIZE // CMP_STRIDE
    nch = n_pages * cpp
    n_cmp = (past + td) // CMP_STRIDE - CMP_BLOCK // CMP_STRIDE + 1
    assert (past + td) // CMP_STRIDE == nch, "new rows must not complete a chunk"
    per_b3 = lambda b, s, pt: (b, 0, 0)
    per_b = lambda b, s, pt: (b, 0, 0, 0)
    c2 = lambda b, s, pt: (0, 0)
    rows = qd.shape[2]
    in_specs = ([pl.BlockSpec((None, mrows, qw), per_b3), pl.BlockSpec((None,) + knew.shape[1:], per_b3),
                 pl.BlockSpec((None,) + qd.shape[1:], per_b)]
                + [pl.BlockSpec((CHUNK_FEATS, 2 * LANE), c2) for _ in range(2)]
                + [pl.BlockSpec((8, CHUNK_FEATS), c2) for _ in range(2)]
                + [pl.BlockSpec((LANE, LANE), c2) for _ in range(2)]
                + [pl.BlockSpec(memory_space=pl.ANY) for _ in range(4)])
    kern = functools.partial(_mla_cmp_decode_kernel, pp=pp, td=td, kv_rank=kv_rank, past=past, n_cmp=n_cmp,
                             n_sel_pad=n_sel_pad)
    pools = (ckv_pool, kr_pool, k_pool, v_pool)
    return pl.pallas_call(
        kern,
        grid_spec=pltpu.PrefetchScalarGridSpec(
            num_scalar_prefetch=1, grid=(bd, n_pages // pp), in_specs=in_specs,
            out_specs=[pl.BlockSpec((None, mrows, kv_rank), per_b3),
                       pl.BlockSpec((None, NSA_GROUPS, rows, LANE), per_b),
                       pl.BlockSpec((None, NSA_GROUPS, 8, n_sel_pad), per_b)],
            scratch_shapes=[pltpu.VMEM((2, pp) + p.shape[1:], F32) for p in pools]
            + [pltpu.SemaphoreType.DMA((2,)) for _ in pools]
            + [pltpu.VMEM((pp * PAGE_SIZE, kv_rank), BF16), pltpu.VMEM((MLA_D_ROPE, pp * PAGE_SIZE), BF16),
               pltpu.VMEM((mrows, 1), F32), pltpu.VMEM((mrows, 1), F32), pltpu.VMEM((mrows, kv_rank), F32)]
            + [pltpu.VMEM((pp // 2 * PAGE_SIZE, LANE), F32) for _ in range(4)]
            + [pltpu.VMEM((nch, 2 * LANE), F32) for _ in range(2)]),
        out_shape=[jax.ShapeDtypeStruct((bd, mrows, kv_rank), F32),
                   jax.ShapeDtypeStruct((bd, NSA_GROUPS, rows, LANE), F32),
                   jax.ShapeDtypeStruct((bd, NSA_GROUPS, 8, n_sel_pad), F32)],
        compiler_params=_cparams("arbitrary", "arbitrary"), name="mla_cmp_decode",
    )(pt, qd_mla, knew, qd, wts['cmp_w1k'], wts['cmp_w1v'], wts['cmp_posk'], wts['cmp_posv'],
      wts['cmp_w2k'], wts['cmp_w2v'], *pools)


def _rank_kernel(score_ref, cur_ref, sel_ref, *, n_sel):
    sc = score_ref[...]
    jj = _iota(sc.shape, 0)

    def body(i, rank):
        ri = score_ref[pl.ds(i, 1), :]
        beats = (ri > sc) | ((ri == sc) & (i < jj))
        return rank + beats.astype(F32)

    rank = lax.fori_loop(0, n_sel, body, jnp.zeros(sc.shape, F32))
    sel = (rank < float(min(SLC_TOP_N, n_sel))) & (jj <= cur_ref[...])
    sel_ref[...] = sel.astype(F32)


def _rank(score_t, cur, *, n_sel):
    full = lambda a: pl.BlockSpec(a.shape, lambda: (0,) * a.ndim)
    return pl.pallas_call(
        functools.partial(_rank_kernel, n_sel=n_sel), in_specs=[full(score_t), full(cur)],
        out_specs=full(score_t), out_shape=jax.ShapeDtypeStruct(score_t.shape, F32), name="rank_decode",
    )(score_t, cur)


def _slc_decode_kernel(pt_ref, cnt_ref, q_ref, sel_ref, selnew_ref, knew_ref, vnew_ref, kpos_ref, expand_ref,
                       k_hbm, v_hbm, o_ref, kbuf, vbuf, ksem, vsem, kscr, vscr, m_scr, l_scr, acc_scr,
                       *, pp, td, past):
    b, s, ns = pl.program_id(0), pl.program_id(1), pl.num_programs(1)
    step = b * ns + s
    slot = step % 2
    pools, bufs, sems = (k_hbm, v_hbm), (kbuf, vbuf), (ksem, vsem)
    active = s * pp < cnt_ref[b]
    wrap = s + 1 == ns
    nb = jnp.minimum(jnp.where(wrap, b + 1, b), pl.num_programs(0) - 1)
    next_active = (step + 1 < pl.num_programs(0) * ns) & (jnp.where(wrap, 0, s + 1) * pp < cnt_ref[nb])

    @pl.when((step == 0) & active)
    def _():
        _start_all(_page_copies(pt_ref, pools, bufs, sems, step, slot, pp), len(pools))

    @pl.when(next_active)
    def _():
        _start_all(_page_copies(pt_ref, pools, bufs, sems, step + 1, 1 - slot, pp), len(pools))

    rows = q_ref.shape[1]
    hrow, trow = _alibi_rows(rows, td)

    @pl.when(s == 0)
    def _():
        m_scr[...] = jnp.full_like(m_scr, NEG)
        l_scr[...] = jnp.zeros_like(l_scr)
        acc_scr[...] = jnp.zeros_like(acc_scr)
        kn = knew_ref[...]
        vn = vnew_ref[...]
        col = _iota((1, kn.shape[0]), 1)
        dist = trow - col
        for g in range(NSA_GROUPS):
            sc = _dot_nt(q_ref[g], kn) - _slope_rows(hrow, g) * dist.astype(F32)
            mask = (dist >= 0) & (col < td) & (selnew_ref[g] > 0.5)
            _softmax_update(jnp.where(mask, sc, NEG), vn, m_scr.at[g], l_scr.at[g], acc_scr.at[g])

    @pl.when(active)
    def _():
        for c in _page_copies(pt_ref, pools, bufs, sems, step, slot, pp):
            c.wait()
        for k in range(pp):
            kscr[:, k * PAGE_SIZE:(k + 1) * PAGE_SIZE] = kbuf[slot, k].astype(BF16)
            vscr[:, k * PAGE_SIZE:(k + 1) * PAGE_SIZE] = vbuf[slot, k].astype(BF16)
        kt = kscr[...]
        vt = vscr[...]
        distf = (past + trow).astype(F32) - kpos_ref[...]
        for g in range(NSA_GROUPS):
            sc = _dot(q_ref[g], kt) - _slope_rows(hrow, g) * distf
            selx = _dot(sel_ref[g], expand_ref[...]) > 0.5
            _softmax_update(jnp.where(selx, sc, NEG), vt, m_scr.at[g], l_scr.at[g], acc_scr.at[g], v_transposed=True)

    @pl.when(s == ns - 1)
    def _():
        o_ref[...] = acc_scr[...] / l_scr[...]


def _slc_decode(pt, counts, qd, sel16, selnew, knew, vnew, kpos, k_pool, v_pool, *, td, pp, past):
    bd = qd.shape[0]
    rows = qd.shape[2]
    n_pages = pt.shape[0] // bd
    tk = pp * PAGE_SIZE
    expand = (jnp.arange(tk)[None, :] // SLC_BLOCK == jnp.arange(tk // SLC_BLOCK)[:, None]).astype(BF16)
    per_b = lambda b, s, pt, cnt: (b, 0, 0, 0)
    per_b3 = lambda b, s, pt, cnt: (b, 0, 0)
    in_specs = ([pl.BlockSpec((None,) + qd.shape[1:], per_b),
                 pl.BlockSpec((None, None) + sel16.shape[2:], lambda b, s, pt, cnt: (b, s, 0, 0, 0)),
                 pl.BlockSpec((None,) + selnew.shape[1:], per_b),
                 pl.BlockSpec((None,) + knew.shape[1:], per_b3), pl.BlockSpec((None,) + vnew.shape[1:], per_b3),
                 pl.BlockSpec((None, None, 1, tk), lambda b, s, pt, cnt: (b, s, 0, 0)),
                 pl.BlockSpec(expand.shape, lambda b, s, pt, cnt: (0, 0))]
                + [pl.BlockSpec(memory_space=pl.ANY) for _ in range(2)])
    kern = functools.partial(_slc_decode_kernel, pp=pp, td=td, past=past)
    return pl.pallas_call(
        kern,
        grid_spec=pltpu.PrefetchScalarGridSpec(
            num_scalar_prefetch=2, grid=(bd, n_pages // pp), in_specs=in_specs,
            out_specs=pl.BlockSpec((None, NSA_GROUPS, rows, LANE), per_b),
            scratch_shapes=[pltpu.VMEM((2, pp) + k_pool.shape[1:], F32) for _ in range(2)]
            + [pltpu.SemaphoreType.DMA((2,)) for _ in range(2)]
            + [pltpu.VMEM((LANE, pp * PAGE_SIZE), BF16) for _ in range(2)]
            + [pltpu.VMEM((NSA_GROUPS, rows, 1), F32) for _ in range(2)] + [pltpu.VMEM((NSA_GROUPS, rows, LANE), F32)]),
        out_shape=jax.ShapeDtypeStruct((bd, NSA_GROUPS, rows, LANE), F32),
        compiler_params=_cparams("arbitrary", "arbitrary"), name="slc_decode",
    )(pt, counts, qd, sel16, selnew, knew, vnew, kpos, expand, k_pool, v_pool)


def _win_decode_kernel(q_ref, kst_ref, vst_ref, knew_ref, vnew_ref, o_ref, *, td, past):
    rows = q_ref.shape[1]
    hrow, trow = _alibi_rows(rows, td)
    nbuf = kst_ref.shape[1]
    kst = kst_ref[...].astype(BF16)
    vst = vst_ref[...].astype(BF16)
    kn = knew_ref[...]
    vn = vnew_ref[...]
    kpos = past - nbuf + _iota((1, nbuf), 1)
    d1 = (past + trow) - kpos
    m1 = (d1 >= 0) & (d1 < WINDOW) & (kpos >= 0)
    col = _iota((1, kn.shape[0]), 1)
    d2 = trow - col
    m2 = (d2 >= 0) & (d2 < WINDOW) & (col < td)
    for g in range(NSA_GROUPS):
        slope = _slope_rows(hrow, g)
        s1 = jnp.where(m1, _dot(q_ref[g], kst) - slope * d1.astype(F32), NEG)
        s2 = jnp.where(m2, _dot_nt(q_ref[g], kn) - slope * d2.astype(F32), NEG)
        m = jnp.maximum(jnp.max(s1, axis=-1, keepdims=True), jnp.max(s2, axis=-1, keepdims=True))
        p1 = jnp.exp(s1 - m)
        p2 = jnp.exp(s2 - m)
        l = jnp.sum(p1, axis=-1, keepdims=True) + jnp.sum(p2, axis=-1, keepdims=True)
        o_ref[g] = (_dot_nt(p1.astype(BF16), vst) + _dot(p2.astype(BF16), vn)) / l


def _win_decode(qd, kst, vst, knew, vnew, *, td, past):
    bd = qd.shape[0]
    rows = qd.shape[2]
    per_b = lambda b: (b, 0, 0, 0)
    per_b3 = lambda b: (b, 0, 0)
    blk3 = lambda a: pl.BlockSpec((None,) + a.shape[1:], per_b3)
    return pl.pallas_call(
        functools.partial(_win_decode_kernel, td=td, past=past), grid=(bd,),
        in_specs=[pl.BlockSpec((None,) + qd.shape[1:], per_b), blk3(kst), blk3(vst), blk3(knew), blk3(vnew)],
        out_specs=pl.BlockSpec((None, NSA_GROUPS, rows, LANE), per_b),
        out_shape=jax.ShapeDtypeStruct((bd, NSA_GROUPS, rows, LANE), F32),
        compiler_params=_cparams("parallel"), name="win_decode",
    )(qd, kst, vst, knew, vnew)


def _pad_rows(a, rows):
    return jnp.pad(a, ((0, 0), (0, rows - a.shape[1]), (0, 0)))


def _sample(x_sample, caches, page_table, w):
    (c_ckv, c_krope, c_cmp_k, c_cmp_v, c_slc_k, c_slc_v, s_win_k, s_win_v, s_conv) = caches
    bd, td, d = x_sample.shape
    n = bd * td
    n_pages = page_table.shape[1]
    past = n_pages * PAGE_SIZE
    kv_rank = c_ckv.shape[-1]
    x = x_sample.reshape(n, d)
    pos = past + jnp.arange(td, dtype=jnp.int32)
    cosq, sinq = _rope_tables(jnp.tile(pos, bd))
    (qrot, qabs, ckv, krp, qn, kc, vc, ks, vs, kw, vw, gn, ga, gb) = _inproj(
        x, w, cosq, sinq, sample=True, tm=n, tab_blocks=1)
    pt = page_table.reshape(-1)
    krope = krp[:, MLA_D_NOPE:MLA_D_NOPE + MLA_D_ROPE]
    qrope = qrot.reshape(bd, td, MLA_HEADS, HEAD_PAD)[..., MLA_D_NOPE:MLA_D_NOPE + MLA_D_ROPE]
    qpad = LANE - MLA_D_ROPE
    qd_mla = jnp.concatenate([qabs.reshape(bd, td, MLA_HEADS, kv_rank), qrope,
                              jnp.zeros((bd, td, MLA_HEADS, qpad), F32)], axis=-1)
    qd_mla = qd_mla.reshape(bd, td * MLA_HEADS, kv_rank + LANE).astype(BF16)
    knew = jnp.concatenate([ckv, krope, jnp.zeros((n, qpad), F32)], axis=-1).reshape(bd, td, -1)
    knew = _pad_rows(knew, 8).astype(BF16)
    pos_minor = lambda c: jnp.moveaxis(c, 1, -1).reshape(c.shape[0], -1, c.shape[1])
    qg = qn.astype(F32).reshape(bd, td, NSA_GROUPS, NSA_HPG, NSA_DH).transpose(0, 2, 3, 1, 4)
    qg = qg.reshape(bd, NSA_GROUPS, NSA_HPG * td, NSA_DH)
    lane_g = (jnp.arange(LANE) // NSA_DH)[None, :] == jnp.arange(NSA_GROUPS)[:, None]
    qd = jnp.where(lane_g[None, :, None, :], jnp.tile(qg, (1, 1, 1, NSA_GROUPS)), 0.0).astype(BF16)
    n_sel = -(-(past + td) // SLC_BLOCK)
    n_sel_pad = -(-n_sel // LANE) * LANE
    o_lat, o_cmp, score = _mla_cmp_decode(pt, qd_mla, knew, qd, c_ckv, pos_minor(c_krope), pos_minor(c_cmp_k),
                                          pos_minor(c_cmp_v), w, td=td, pp=min(PAGES_PER_STEP, n_pages), past=past,
                                          n_sel_pad=n_sel_pad)
    cur = jnp.broadcast_to((pos // SLC_BLOCK).astype(jnp.int32), (bd, NSA_GROUPS, td)).reshape(1, -1)
    n_sel8 = -(-n_sel // 8) * 8
    sel_t = _rank(score[:, :, :td, :n_sel8].reshape(-1, n_sel8).T, cur, n_sel=n_sel)
    sel = jnp.pad(sel_t.T, ((0, 0), (0, n_sel_pad - n_sel8))).reshape(bd, NSA_GROUPS, td, n_sel_pad)
    pp_slc = min(PAGES_PER_STEP, n_pages)
    bpp = PAGE_SIZE // SLC_BLOCK
    n_past_blk = past // SLC_BLOCK
    sel_pg = sel[..., :n_past_blk].reshape(bd, NSA_GROUPS * td, n_pages, bpp)
    need_page = sel_pg.max(axis=(1, 3)) > 0.5
    counts = need_page.sum(axis=-1).astype(jnp.int32)
    n_before = jnp.cumsum(need_page.astype(jnp.int32), axis=-1)
    u_before = jnp.cumsum(1 - need_page.astype(jnp.int32), axis=-1)
    slot_of = jnp.where(need_page, n_before - 1, counts[:, None] + u_before - 1)
    perm = slot_of[:, None, :] == jnp.arange(n_pages, dtype=jnp.int32)[None, :, None]
    order = jnp.sum(jnp.where(perm, jnp.arange(n_pages, dtype=jnp.int32), 0), axis=-1)
    pt_slc = jnp.sum(jnp.where(perm, page_table[:, None, :], 0), axis=-1).reshape(-1)
    sel_listed = jnp.einsum('bip,bxpc->bxic', perm.astype(BF16), sel_pg.astype(BF16), preferred_element_type=F32)
    bps = pp_slc * bpp
    sel_steps = sel_listed.reshape(bd, NSA_GROUPS, td, n_pages // pp_slc, bps).transpose(0, 3, 1, 2, 4)
    sel16 = jnp.tile(sel_steps, (1, 1, 1, NSA_HPG, 1)).astype(BF16)
    kpos = (order[:, :, None] * PAGE_SIZE + jnp.arange(PAGE_SIZE, dtype=jnp.int32)).astype(F32)
    kpos = kpos.reshape(bd, n_pages // pp_slc, 1, pp_slc * PAGE_SIZE)
    selnew = jnp.tile(jnp.broadcast_to(sel[..., n_past_blk:n_past_blk + 1], (bd, NSA_GROUPS, td, 8)), (1, 1, NSA_HPG, 1))
    new8 = lambda a: _pad_rows(a.reshape(bd, td, LANE), 8).astype(BF16)
    o_slc = _slc_decode(pt_slc, counts, qd, sel16, selnew, new8(ks), new8(vs), kpos, pos_minor(c_slc_k),
                        pos_minor(c_slc_v), td=td, pp=pp_slc, past=past)
    nbuf = s_win_k.shape[1]
    o_win = _win_decode(qd, pos_minor(s_win_k), pos_minor(s_win_v), new8(kw), new8(vw),
                        td=td, past=past)

    def heads_out(o):
        o = o.reshape(bd, NSA_GROUPS, NSA_HPG, td, NSA_GROUPS, NSA_DH)
        o = jnp.stack([o[:, g, :, :, g] for g in range(NSA_GROUPS)], axis=1)
        return o.transpose(0, 3, 1, 2, 4).reshape(n, NSA_HEADS * NSA_DH)

    dff = w['w_gate'].shape[1]
    assert td >= CONV_W - 1
    prev1 = jnp.concatenate([s_conv[:, 1:2], jnp.zeros((bd, td - 1, dff), F32)], axis=1)
    prev2 = jnp.concatenate([s_conv[:, 0:2], jnp.zeros((bd, td - 2, dff), F32)], axis=1)
    y, g = _finish(x, o_lat.reshape(n, MLA_HEADS * kv_rank), heads_out(o_cmp), heads_out(o_slc), heads_out(o_win),
                   gn, ga, gb, (prev1.reshape(n, dff), prev2.reshape(n, dff)), w, tm=n, period=td, full_g=True,
                   latent=True)
    kv4 = lambda a: a.reshape(1, bd, td, NSA_GROUPS, NSA_DH)
    win = lambda st, new: jnp.concatenate([st, new.reshape(bd, td, NSA_GROUPS, NSA_DH)], axis=1)[None, :, -nbuf:]
    conv_state = jnp.concatenate([s_conv, g.reshape(bd, td, dff)], axis=1)[None, :, -(CONV_W - 1):]
    states = (ckv.reshape(1, bd, td, kv_rank), krope.reshape(1, bd, td, MLA_D_ROPE), kv4(kc), kv4(vc), kv4(ks), kv4(vs),
              win(s_win_k, kw), win(s_win_v, vw), conv_state)
    return y.reshape(bd, td, d), states


def kernel(x_prompt, x_sample, cache_mla_ckv, cache_mla_krope, cache_nsa_cmp_k, cache_nsa_cmp_v, cache_nsa_slc_k, cache_nsa_slc_v, state_win_k, state_win_v, state_ffn_conv, page_table, norm1_g, w_in, q_norm_g, kv_norm_g, w_uq, w_uk, w_uv, cmp_pos_k, cmp_w1_k, cmp_w2_k, cmp_pos_v, cmp_w1_v, cmp_w2_v, w_proj_mla, w_proj_nsa, w_out, norm2_g, w_gate, w_up, conv_w, conv_b, w_down, norm_f_g):
    assert norm1_g.shape[0] == 1, "single-layer trunk"
    p = dict(norm1_g=norm1_g[0], w_in=w_in[0], q_norm_g=q_norm_g[0], kv_norm_g=kv_norm_g[0], w_uq=w_uq[0],
             w_uk=w_uk[0], w_uv=w_uv[0], cmp_pos_k=cmp_pos_k[0], cmp_w1_k=cmp_w1_k[0], cmp_w2_k=cmp_w2_k[0],
             cmp_pos_v=cmp_pos_v[0], cmp_w1_v=cmp_w1_v[0], cmp_w2_v=cmp_w2_v[0], w_proj_mla=w_proj_mla[0],
             w_proj_nsa=w_proj_nsa[0], w_out=w_out[0], norm2_g=norm2_g[0], w_gate=w_gate[0], w_up=w_up[0],
             conv_w=conv_w[0], conv_b=conv_b[0], w_down=w_down[0], norm_f_g=norm_f_g)
    w = _prep_weights(p)
    y_p, ps = _prompt(x_prompt, w)
    caches = (cache_mla_ckv[0], cache_mla_krope[0], cache_nsa_cmp_k[0], cache_nsa_cmp_v[0], cache_nsa_slc_k[0],
              cache_nsa_slc_v[0], state_win_k[0], state_win_v[0], state_ffn_conv[0])
    y_s, ss = _sample(x_sample, caches, page_table, w)
    out = [y_p, y_s]
    for a, b in zip(ps, ss):
        out += [a, b]
    return tuple(out)
```

```python
import functools

import numpy as np
import jax
import jax.numpy as jnp
from jax import lax
from jax.experimental import pallas as pl
from jax.experimental.pallas import tpu as pltpu

MLA_HEADS = 8
MLA_D_NOPE = 64
MLA_D_ROPE = 32
MLA_D_V = 64
ROPE_THETA = 10000.0
MLA_SCALE = (MLA_D_NOPE + MLA_D_ROPE) ** -0.5
NSA_HEADS = 8
NSA_GROUPS = 2
NSA_HPG = NSA_HEADS // NSA_GROUPS
NSA_DH = 64
NSA_SCALE = NSA_DH ** -0.5
CMP_BLOCK = 32
CMP_STRIDE = 16
SLC_BLOCK = 64
SLC_TOP_N = 16
WINDOW = 512
CONV_W = 3
PAGE_SIZE = 128
EPS = 1e-6
NEG = -1e30
FORCE = 1e9

LOG2E = 1.4426950408889634
LANE = 128
AUX_POS_HI, AUX_POS_LO, AUX_BLK0 = 64, 65, 72
HEAD_PAD = 128
CHUNK_FEATS = CMP_STRIDE * NSA_GROUPS * NSA_DH
FFN_BLOCK = 1536
SLOPES =tuple(float(2.0 ** (-8.0 * (h + 1) / NSA_HEADS)) for h in range(NSA_HEADS))
VMEM_LIMIT = 56 * 1024 * 1024
ROW_TILE = 256
MLA_TILES = (512, 512)
SLC_TILES = (256, 512)
CMP_TQ = 128
PAGES_PER_STEP = 32

F32 = jnp.float32
BF16 = jnp.bfloat16
_NT = (((1,), (1,)), ((), ()))


def _cparams(*sem):
    return pltpu.CompilerParams(dimension_semantics=sem, vmem_limit_bytes=VMEM_LIMIT)


def _rms(x, g):
    return x * lax.rsqrt(jnp.mean(x * x, axis=-1, keepdims=True) + EPS) * g


def _dot(a, b):
    return jnp.dot(a, b, preferred_element_type=F32)


def _dot_nt(a, b):
    return lax.dot_general(a, b, _NT, preferred_element_type=F32)


def _dot_exact(a, b):
    return jnp.dot(a, b, preferred_element_type=F32, precision=lax.Precision.HIGHEST)


def _iota(shape, dim):
    return lax.broadcasted_iota(jnp.int32, shape, dim)


def _log2(n):
    assert n > 0 and n & (n - 1) == 0, n
    return n.bit_length() - 1


def _vdiv(x, n):
    return lax.shift_right_logical(x, jnp.full(x.shape, _log2(n), jnp.int32))


def _vmod(x, n):
    assert n & (n - 1) == 0, n
    return x & (n - 1)


_O_CQ, _O_CKV, _O_QN, _O_K6, _O_GA = 0, 384, 640, 1152, 1920


def _inproj_kernel(x_ref, g1_ref, w_ref, qg_ref, kvg_ref, wuq_ref, wk_ref, wv_ref, cos_ref, sin_ref,
                   *outs, sample, q_rank, kv_rank, d_model, tiles_per_seq, q_scale):
    o_gb = _O_GA + d_model
    o_kr = o_gb + d_model
    x = x_ref[...]
    hn = _rms(x, g1_ref[...])
    y = _dot(hn.astype(BF16), w_ref[...])
    cosq = cos_ref[...]
    sinq = sin_ref[...]
    nq = MLA_HEADS * HEAD_PAD
    cqn = _rms(y[:, _O_CQ:_O_CQ + q_rank], qg_ref[...])
    q2 = _dot(cqn.astype(BF16), wuq_ref[...])
    half = MLA_D_ROPE // 2
    lane_h = _iota((x.shape[0], HEAD_PAD), 1)

    def rotated(qh):
        partner = jnp.where(lane_h < MLA_D_NOPE + half, -pltpu.roll(qh, HEAD_PAD - half, 1), pltpu.roll(qh, half, 1))
        return (qh * cosq + partner * sinq) * q_scale

    ckv = _rms(y[:, _O_CKV:_O_CKV + kv_rank], kvg_ref[...])
    kr = y[:, o_kr:o_kr + LANE] * cosq + y[:, o_kr + LANE:o_kr + 2 * LANE] * sinq
    ckv_b = ckv.astype(BF16)
    it = iter(outs)
    if sample:
        qrot_ref, qabs_ref = next(it), next(it)
        for h in range(MLA_HEADS):
            sl = slice(h * HEAD_PAD, (h + 1) * HEAD_PAD)
            qh = rotated(q2[:, sl])
            qrot_ref[:, sl] = qh
            qabs_ref[:, h * kv_rank:(h + 1) * kv_rank] = _dot(qh.astype(BF16), wk_ref[h])
    else:
        q_ref, k_ref, v_ref = next(it), next(it), next(it)
        knp = _dot(ckv_b, wk_ref[...])
        for h in range(MLA_HEADS):
            sl = slice(h * HEAD_PAD, (h + 1) * HEAD_PAD)
            qh = rotated(q2[:, sl])
            q_ref[:, sl] = qh.astype(BF16)
            k_ref[:, sl] = (knp[:, sl] + kr).astype(BF16)
        ones_lane = (_vmod(_iota((1, nq), 1), HEAD_PAD) == MLA_D_V).astype(F32)
        v_ref[...] = (_dot(ckv_b, wv_ref[...]) + ones_lane).astype(BF16)
    ckv_ref, kr_ref, qn_ref = next(it), next(it), next(it)
    ckv_ref[...] = ckv
    kr_ref[...] = kr
    yq = y[:, _O_QN:_O_QN + NSA_HEADS * NSA_DH] * NSA_SCALE
    if sample:
        qn_ref[...] = yq.astype(BF16)
    else:
        lane_q = _iota((x.shape[0], LANE), 1)
        for h in range(NSA_HEADS):
            pair = yq[:, (h // 2) * LANE:(h // 2 + 1) * LANE]
            if h % 2:
                pair = pltpu.roll(pair, NSA_DH, 1)
            aux_q = jnp.where((lane_q == AUX_POS_HI) | (lane_q == AUX_POS_LO), SLOPES[h], 0.0)
            qn_ref[:, h * LANE:(h + 1) * LANE] = jnp.where(lane_q < NSA_DH, pair, aux_q).astype(BF16)
    for j in range(6):
        yj = y[:, _O_K6 + j * LANE:_O_K6 + (j + 1) * LANE]
        if sample:
            next(it)[...] = yj
        else:
            next(it)[...] = yj.T
    if not sample:
        tm = x.shape[0]
        pos = (pl.program_id(0) % tiles_per_seq) * tm + _iota((tm, 1), 0)
        lane = _iota((tm, LANE), 1)
        onehot = ((lane >= AUX_BLK0) & (_vdiv(pos, SLC_BLOCK) == lane - AUX_BLK0)).astype(F32)
        aux = jnp.where(lane == AUX_POS_HI, (pos - _vmod(pos, 256)).astype(F32),
                        jnp.where(lane == AUX_POS_LO, _vmod(pos, 256).astype(F32), onehot))
        for j in range(6):
            yj = y[:, _O_K6 + j * LANE:_O_K6 + (j + 1) * LANE]
            ref = next(it)
            if j in (2, 3, 4, 5):
                tail = aux if j in (2, 4) else (lane == NSA_DH).astype(F32)
                ref[:, :LANE] = jnp.where(lane < NSA_DH, yj, tail).astype(BF16)
                ref[:, LANE:] = jnp.where(lane < NSA_DH, pltpu.roll(yj, NSA_DH, 1), tail).astype(BF16)
            else:
                ref[...] = yj.astype(BF16)
    gn_ref, ga_ref, gb_ref = next(it), next(it), next(it)
    gn_ref[...] = jax.nn.sigmoid(y[:, o_kr + 2 * LANE:o_kr + 3 * LANE])
    ga_ref[...] = jax.nn.sigmoid(y[:, _O_GA:_O_GA + d_model])
    gb_ref[...] = jax.nn.sigmoid(y[:, o_gb:o_gb + d_model])


def _inproj(x, wts, cosq, sinq, *, sample, tm, tab_blocks):
    n, d = x.shape
    q_rank, kv_rank = wts['q_norm_g'].shape[1], wts['kv_norm_g'].shape[1]
    nq = MLA_HEADS * HEAD_PAD
    wk = wts['w_ukT'] if sample else wts['w_ukp']
    row = lambda i: (i, 0)
    const2 = lambda i: (0, 0)
    tab_map = (lambda i: (i % tab_blocks, 0))
    in_specs = [
        pl.BlockSpec((tm, d), row),
        pl.BlockSpec((1, d), const2),
        pl.BlockSpec(wts['w_in'].shape, const2),
        pl.BlockSpec((1, q_rank), const2),
        pl.BlockSpec((1, kv_rank), const2),
        pl.BlockSpec(wts['w_uq2'].shape, const2),
        pl.BlockSpec(wk.shape, (lambda i: (0, 0, 0)) if sample else const2),
        pl.BlockSpec(wts['w_uvf'].shape, const2),
        pl.BlockSpec((tm, LANE), tab_map),
        pl.BlockSpec((tm, LANE), tab_map),
    ]
    shapes = []
    if sample:
        shapes += [(nq, F32), (MLA_HEADS * kv_rank, F32)]
    else:
        shapes += [(nq, BF16), (nq, BF16), (nq, BF16)]
    shapes += [(kv_rank, F32), (LANE, F32), (NSA_HEADS * (NSA_DH if sample else LANE), BF16)]
    n_lead = len(shapes)
    shapes += [(LANE, F32)] * 6
    if not sample:
        shapes += [(LANE, BF16), (LANE, BF16)] + [(2 * LANE, BF16)] * 4
        assert AUX_BLK0 + -(-tab_blocks * tm // SLC_BLOCK) <= LANE, "block one-hot must fit the aux lanes"
    shapes += [(LANE, F32), (d, F32), (d, F32)]
    out_shape = [jax.ShapeDtypeStruct((n, w), dt) for w, dt in shapes]
    out_specs = [pl.BlockSpec((tm, w), row) for w, _ in shapes]
    if not sample:
        tpb = tab_blocks
        for j in range(n_lead, n_lead + 6):
            out_shape[j] = jax.ShapeDtypeStruct((n // (tpb * tm), LANE, tpb * tm), F32)
            out_specs[j] = pl.BlockSpec((None, LANE, tm), lambda i: (i // tpb, 0, i % tpb))
    q_scale = MLA_SCALE if sample else MLA_SCALE * LOG2E
    kern = functools.partial(_inproj_kernel, sample=sample, q_rank=q_rank, kv_rank=kv_rank, d_model=d,
                             tiles_per_seq=tab_blocks, q_scale=q_scale)
    return pl.pallas_call(
        kern, grid=(n // tm,), in_specs=in_specs, out_specs=out_specs, out_shape=out_shape,
        compiler_params=_cparams("parallel"), name="inproj_sample" if sample else "inproj_prompt",
    )(x, wts['norm1_g'], wts['w_in'], wts['q_norm_g'], wts['kv_norm_g'], wts['w_uq2'], wk, wts['w_uvf'],
      cosq, sinq)


def _stack_heads(q_ref, heads, width):
    parts = [q_ref[:, h * width:(h + 1) * width] for h in heads]
    return parts[0] if len(parts) == 1 else jnp.concatenate(parts, axis=0)


def _flash_kernel(*refs, tq, tk, groups, par, dq, dk, dv, window, has_bias, base2):
    if has_bias:
        q_ref, k_ref, v_ref, bias_ref, o_ref = refs
    else:
        q_ref, k_ref, v_ref, o_ref = refs
    q_start = pl.program_id(1) * tq
    n_hi = (q_start + tq - 1) // tk + 1
    hi_full = (q_start + 1) // tk
    if window:
        n_lo = jnp.maximum(q_start - (window - 1), 0) // tk
        lo_full = (jnp.maximum(q_start + tq - window, 0) + tk - 1) // tk
    else:
        n_lo, lo_full = 0, 0
    e1 = jnp.clip(lo_full, n_lo, n_hi)
    e2 = jnp.clip(hi_full, e1, n_hi)
    ex = jnp.exp2 if base2 else jnp.exp
    for c0 in range(0, len(groups), par):
        chunk = groups[c0:c0 + par]
        qs = []
        for heads, kcol in chunk:
            qg = _stack_heads(q_ref, heads, dq)
            if has_bias:
                bias = bias_ref[:, kcol * LANE:(kcol + 1) * LANE]
                qg = qg + jnp.concatenate([bias] * len(heads), axis=0)
            qs.append(qg)
        rows = qs[0].shape[0]
        qpos = q_start + (_iota((rows, 1), 0) & (tq - 1))

        def step(j, carry, masked):
            k0 = pl.multiple_of(j * tk, tk)
            if masked:
                dist = qpos - (k0 + _iota((1, tk), 1))
                mask = dist >= 0
                if window:
                    mask = mask & (dist < window)
            out = []
            for (heads, kcol), qg, (m, acc) in zip(chunk, qs, carry):
                kt = k_ref[pl.ds(k0, tk), kcol * dk:(kcol + 1) * dk]
                vt = v_ref[pl.ds(k0, tk), kcol * LANE:(kcol + 1) * LANE]
                s = _dot_nt(qg, kt)
                if masked:
                    s = jnp.where(mask, s, NEG)
                m_new = jnp.maximum(m, jnp.max(s, axis=-1, keepdims=True))
                p = ex((s - m_new).astype(BF16))
                acc = ex(m - m_new) * acc + _dot(p, vt)
                out.append((m_new, acc))
            return tuple(out)

        carry = tuple((jnp.full((rows, 1), NEG, F32), jnp.zeros((rows, LANE), F32)) for _ in chunk)
        carry = lax.fori_loop(n_lo, e1, functools.partial(step, masked=True), carry)
        carry = lax.fori_loop(e1, e2, functools.partial(step, masked=False), carry)
        carry = lax.fori_loop(e2, n_hi, functools.partial(step, masked=True), carry)
        for (heads, kcol), (m, acc) in zip(chunk, carry):
            o = acc[:, :dv] * (1.0 / acc[:, dv:dv + 1])
            for hh, h in enumerate(heads):
                o_ref[:, h * dv:(h + 1) * dv] = o[hh * tq:(hh + 1) * tq].astype(o_ref.dtype)


def _flash(q, k, v, bias, *, batch, seq, tq, tk, groups, par, dq, dk, dv, window, base2, name):
    n = q.shape[0]
    has_bias = bias is not None
    n_heads = sum(len(g[0]) for g in groups)
    qrow = lambda b, i: (b * (seq // tq) + i, 0)
    kv = lambda b, i: (b, 0)
    in_specs = [pl.BlockSpec((tq, q.shape[1]), qrow), pl.BlockSpec((seq, k.shape[1]), kv),
                pl.BlockSpec((seq, v.shape[1]), kv)]
    args = [q, k, v]
    if has_bias:
        in_specs.append(pl.BlockSpec((tq, bias.shape[1]), qrow))
        args.append(bias)
    kern = functools.partial(_flash_kernel, tq=tq, tk=tk, groups=groups, par=par, dq=dq, dk=dk, dv=dv, window=window,
                             has_bias=has_bias, base2=base2)
    return pl.pallas_call(
        kern, grid=(batch, seq // tq), in_specs=in_specs,
        out_specs=pl.BlockSpec((tq, n_heads * dv), qrow),
        out_shape=jax.ShapeDtypeStruct((n, n_heads * dv), BF16),
        compiler_params=_cparams("parallel", "arbitrary"), name=name,
    )(*args)


def _win_kernel(q_ref, k_ref, v_ref, o_ref, *, tq, groups, dv):
    i = pl.program_id(1)
    rq = _iota((tq, tq), 0)
    ck = _iota((tq, tq), 1)
    bias_own = jnp.where(ck <= rq, 0.0, NEG)
    bias_far = jnp.where(ck > rq, 0.0, NEG) + jnp.where(i >= 2, 0.0, NEG)
    bias_mid = jnp.where(i >= 1, 0.0, NEG)
    starts = (jnp.maximum(i - 2, 0) * tq, jnp.maximum(i - 1, 0) * tq, i * tq)
    for heads, kcol in groups:
        qg = _stack_heads(q_ref, heads, LANE)
        nh = len(heads)
        ss, vs = [], []
        for k0, bias in zip(starts, (bias_far, None, bias_own)):
            k0 = pl.multiple_of(k0, tq)
            s = _dot_nt(qg, k_ref[pl.ds(k0, tq), kcol * LANE:(kcol + 1) * LANE])
            s = s + (bias_mid if bias is None else jnp.concatenate([bias] * nh, axis=0))
            ss.append(s)
            vs.append(v_ref[pl.ds(k0, tq), kcol * LANE:(kcol + 1) * LANE])
        m = functools.reduce(jnp.maximum, [jnp.max(s, axis=-1, keepdims=True) for s in ss])
        acc = functools.reduce(jnp.add, [_dot(jnp.exp((s - m).astype(BF16)), v) for s, v in zip(ss, vs)])
        o = acc[:, :dv] * (1.0 / acc[:, dv:dv + 1])
        for hh, h in enumerate(heads):
            o_ref[:, h * dv:(h + 1) * dv] = o[hh * tq:(hh + 1) * tq].astype(o_ref.dtype)


def _win_prompt(q, k, v, *, batch, seq, tq, groups, dv):
    assert WINDOW == 2 * tq and seq % tq == 0
    n = q.shape[0]
    n_heads = sum(len(g[0]) for g in groups)
    qrow = lambda b, i: (b * (seq // tq) + i, 0)
    kv = lambda b, i: (b, 0)
    return pl.pallas_call(
        functools.partial(_win_kernel, tq=tq, groups=groups, dv=dv), grid=(batch, seq // tq),
        in_specs=[pl.BlockSpec((tq, q.shape[1]), qrow), pl.BlockSpec((seq, k.shape[1]), kv),
                  pl.BlockSpec((seq, v.shape[1]), kv)],
        out_specs=pl.BlockSpec((tq, n_heads * dv), qrow),
        out_shape=jax.ShapeDtypeStruct((n, n_heads * dv), BF16),
        compiler_params=_cparams("parallel", "arbitrary"), name="win_prompt",
    )(q, k, v)


def _compress_finish(y, posy, w2_ref):
    rows = y.shape[0]
    a = y[:, :LANE]
    b = pltpu.roll(y[:, LANE:], rows - 1, 0)
    pos = posy[0:1, :LANE] + posy[1:2, LANE:]
    hid = jax.nn.gelu(a + b + pos)
    return _dot(hid.astype(BF16), w2_ref[...])


def _compress_prompt_kernel(xk_ref, xv_ref, w1k_ref, w1v_ref, pk_ref, pv_ref, w2k_ref, w2v_ref, ok_ref, ov_ref):
    yk = _dot(xk_ref[...], w1k_ref[...])
    yv = _dot(xv_ref[...], w1v_ref[...])
    ok_ref[...] = _compress_finish(yk, _dot(pk_ref[...], w1k_ref[...]), w2k_ref).astype(BF16)
    ov_ref[...] = _compress_finish(yv, _dot(pv_ref[...], w1v_ref[...]), w2v_ref).astype(BF16)


def _compress_prompt(kc, vc, wts, *, batch, seq):
    nch = seq // CMP_STRIDE
    xk = kc.reshape(batch * nch, CHUNK_FEATS)
    xv = vc.reshape(batch * nch, CHUNK_FEATS)
    row = lambda b: (b, 0)
    c2 = lambda b: (0, 0)
    wspec = pl.BlockSpec((CHUNK_FEATS, 2 * LANE), c2)
    pspec = pl.BlockSpec((8, CHUNK_FEATS), c2)
    w2spec = pl.BlockSpec((LANE, LANE), c2)
    return pl.pallas_call(
        _compress_prompt_kernel, grid=(batch,),
        in_specs=[pl.BlockSpec((nch, CHUNK_FEATS), row), pl.BlockSpec((nch, CHUNK_FEATS), row),
                  wspec, wspec, pspec, pspec, w2spec, w2spec],
        out_specs=[pl.BlockSpec((nch, LANE), row)] * 2,
        out_shape=[jax.ShapeDtypeStruct((batch * nch, LANE), BF16)] * 2,
        compiler_params=_cparams("parallel"), name="compress_prompt",
    )(xk, xv, wts['cmp_w1k'], wts['cmp_w1v'], wts['cmp_posk'], wts['cmp_posv'], wts['cmp_w2k'], wts['cmp_w2v'])


def _overlap(n_rows, n_sel):
    c = _iota((n_rows, n_sel), 0) * CMP_STRIDE
    j = _iota((n_rows, n_sel), 1) * SLC_BLOCK
    return ((c < j + SLC_BLOCK) & (c + CMP_BLOCK > j)).astype(F32)


def _force_scores(score, cur, jj):
    forced = (jj == 0) | (jj == cur) | (jj == cur - 1)
    score = jnp.where(forced, FORCE, score)
    return jnp.where(jj <= cur, score, NEG)


def _cmp_prompt_kernel(q_ref, k_ref, v_ref, o_ref, bias_ref, *, tq, n_cmp, n_sel):
    q_start = pl.program_id(1) * tq
    ncp = k_ref.shape[0]
    rows = NSA_HPG * tq
    qpos = q_start + (_iota((rows, 1), 0) & (tq - 1))
    cidx = _iota((1, ncp), 1)
    dist = qpos - (cidx * CMP_STRIDE + CMP_BLOCK - 1)
    mask = (dist >= 0) & (cidx < n_cmp)
    distf = dist.astype(F32)
    hrow = _vdiv(_iota((rows, 1), 0), tq)
    nsp = -(-n_sel // 8) * 8
    cur = _vdiv(q_start + _iota((1, tq), 1), SLC_BLOCK)
    jj = _iota((nsp, tq), 0)
    cb = _iota((nsp, ncp), 1) * CMP_STRIDE
    jb = _iota((nsp, ncp), 0) * SLC_BLOCK
    ov_t = ((cb < jb + SLC_BLOCK) & (cb + CMP_BLOCK > jb)).astype(F32)
    for g in range(NSA_GROUPS):
        heads = range(g * NSA_HPG, (g + 1) * NSA_HPG)
        qg = jnp.concatenate([q_ref[:, h * LANE:h * LANE + NSA_DH] for h in heads], axis=0)
        slope = jnp.zeros((rows, 1), F32)
        for hh, h in enumerate(heads):
            slope = jnp.where(hrow == hh, SLOPES[h], slope)
        s = _dot_nt(qg, k_ref[:, g * NSA_DH:(g + 1) * NSA_DH]) - slope * distf
        s = jnp.where(mask, s, NEG)
        m = jnp.max(s, axis=-1, keepdims=True)
        p = jnp.where(mask, jnp.exp(s - m), 0.0)
        l = jnp.sum(p, axis=-1, keepdims=True)
        p = p / jnp.where(l > 0.0, l, 1.0)
        o = _dot(p.astype(BF16), v_ref[:, g * NSA_DH:(g + 1) * NSA_DH])
        imp = p[0:tq]
        for hh in range(1, NSA_HPG):
            imp = imp + p[hh * tq:(hh + 1) * tq]
        for hh, h in enumerate(heads):
            o_ref[:, h * NSA_DH:(h + 1) * NSA_DH] = o[hh * tq:(hh + 1) * tq].astype(BF16)
        score = lax.dot_general(ov_t, imp, _NT, preferred_element_type=F32, precision=lax.Precision.HIGHEST)
        score = _force_scores(score, cur, jj)
        rank = jnp.zeros((nsp, tq), F32)
        for i in range(n_sel):
            ri = score[i:i + 1, :]
            beats = (ri > score) | ((ri == score) & (i < jj))
            rank = rank + beats.astype(F32)
        sel = (rank < float(min(SLC_TOP_N, n_sel))) & (jj <= cur)
        bias_t = jnp.where(sel | (jj >= n_sel), 0.0, NEG)
        bias_t = jnp.concatenate([jnp.zeros((AUX_BLK0, tq), F32), bias_t,
                                  jnp.zeros((LANE - AUX_BLK0 - nsp, tq), F32)], axis=0)
        bias_ref[:, g * LANE:(g + 1) * LANE] = bias_t.T.astype(BF16)


def _cmp_prompt(qn, kcc, vcc, *, batch, seq, tq):
    n = qn.shape[0]
    nch = seq // CMP_STRIDE
    n_cmp = nch - CMP_BLOCK // CMP_STRIDE + 1
    n_sel = -(-seq // SLC_BLOCK)
    qrow = lambda b, i: (b * (seq // tq) + i, 0)
    kv = lambda b, i: (b, 0)
    kern = functools.partial(_cmp_prompt_kernel, tq=tq, n_cmp=n_cmp, n_sel=n_sel)
    return pl.pallas_call(
        kern, grid=(batch, seq // tq),
        in_specs=[pl.BlockSpec((tq, qn.shape[1]), qrow), pl.BlockSpec((nch, LANE), kv), pl.BlockSpec((nch, LANE), kv)],
        out_specs=[pl.BlockSpec((tq, NSA_HEADS * NSA_DH), qrow), pl.BlockSpec((tq, NSA_GROUPS * LANE), qrow)],
        out_shape=[jax.ShapeDtypeStruct((n, NSA_HEADS * NSA_DH), BF16),
                   jax.ShapeDtypeStruct((n, NSA_GROUPS * LANE), BF16)],
        compiler_params=_cparams("parallel", "arbitrary"), name="cmp_prompt",
    )(qn, kcc, vcc)


def _finish_kernel(*refs, period, latent, has_state):
    it = iter(refs)
    x_ref, omla_ref, ocmp_ref, oslc_ref, owin_ref, gn_ref, ga_ref, gb_ref = (next(it) for _ in range(8))
    prev1_ref, prev2_ref = (next(it), next(it)) if has_state else (None, None)
    gx_ref = next(it)
    wuv_ref = next(it) if latent else None
    (wpm_ref, wpn_ref, wo_ref, g2_ref, wg_ref, wu_ref, cw_ref, cb_ref, wd_ref, gf_ref,
     y_ref, gout_ref, carry_ref) = it
    tm = x_ref.shape[0]
    gn = gn_ref[...]
    gn_hi = gn.astype(BF16)
    gn_lo = (gn - gn_hi.astype(F32)).astype(BF16)
    gexp = _dot(gn_hi, gx_ref[...]) + _dot(gn_lo, gx_ref[...])
    w = NSA_HEADS * NSA_DH
    o_nsa = (gexp[:, 0:w] * ocmp_ref[...].astype(F32) + gexp[:, w:2 * w] * oslc_ref[...].astype(F32)
             + gexp[:, 2 * w:3 * w] * owin_ref[...].astype(F32))
    o_mla = omla_ref[...].astype(BF16)
    if latent:
        o_mla = _dot(o_mla, wuv_ref[...]).astype(BF16)
    merged = (ga_ref[...] * _dot(o_mla, wpm_ref[...])
              + gb_ref[...] * _dot(o_nsa.astype(BF16), wpn_ref[...]))
    x1 = x_ref[...] + _dot(merged.astype(BF16), wo_ref[...])
    h2 = _rms(x1, g2_ref[...]).astype(BF16)
    row = _iota((tm, 1), 0)
    i = pl.program_id(0)
    t = _vmod(i * tm + row, period)
    carried = period > tm
    if carried:
        @pl.when(i == 0)
        def _():
            carry_ref[...] = jnp.zeros_like(carry_ref)
    dff = wg_ref.shape[1]
    x2 = x1
    for c0 in range(0, dff, FFN_BLOCK):
        cs = slice(c0, min(c0 + FFN_BLOCK, dff))
        g = _dot(h2, wg_ref[:, cs])
        u = _dot(h2, wu_ref[:, cs])
        g1 = pltpu.roll(g, 1, 0)
        g2 = pltpu.roll(g, 2, 0)
        if carried:
            c = carry_ref[:, cs]
            g1 = jnp.where(row == 0, c[7:8], g1)
            g2 = jnp.where(row == 0, c[6:7], jnp.where(row == 1, c[7:8], g2))
            carry_ref[:, cs] = g[tm - 8:tm]
        g1 = jnp.where(t >= 1, g1, prev1_ref[:, cs] if has_state else 0.0)
        g2 = jnp.where(t >= 2, g2, prev2_ref[:, cs] if has_state else 0.0)
        cw = cw_ref[:, cs]
        conv = cb_ref[:, cs] + cw[0:1] * g2 + cw[1:2] * g1 + cw[2:3] * g
        act = (jax.nn.silu(conv) * u).astype(BF16)
        x2 = x2 + _dot(act, wd_ref[cs, :])
        gout_ref[:, cs] = g[tm - 8:tm] if gout_ref.shape[0] == 8 else g
    y_ref[...] = _rms(x2, gf_ref[...])


def _finish(x, omla, ocmp, oslc, owin, gn, ga, gb, state_rows, wts, *, tm, period, full_g, latent):
    n, d = x.shape
    dff = wts['w_gate'].shape[1]
    row = lambda i: (i, 0)
    c2 = lambda i: (0, 0)
    acts = [x, omla, ocmp, oslc, owin, gn, ga, gb] + (list(state_rows) if state_rows is not None else [])
    consts = [wts['gate_expand']] + ([wts['w_uvbd']] if latent else []) + [
        wts['w_proj_mla'], wts['w_proj_nsa'], wts['w_out'], wts['norm2_g'],
        wts['w_gate'], wts['w_up'], wts['conv_w'], wts['conv_b'], wts['w_down'], wts['norm_f_g']]
    ins = acts + consts
    in_specs = [pl.BlockSpec((tm, a.shape[1]), row) for a in acts] + [pl.BlockSpec(a.shape, c2) for a in consts]
    g_rows = n if full_g else (n // tm) * 8
    g_blk = tm if full_g else 8
    kern = functools.partial(_finish_kernel, period=period, latent=latent, has_state=state_rows is not None)
    return pl.pallas_call(
        kern, grid=(n // tm,), in_specs=in_specs,
        out_specs=[pl.BlockSpec((tm, d), row), pl.BlockSpec((g_blk, dff), row)],
        out_shape=[jax.ShapeDtypeStruct((n, d), F32), jax.ShapeDtypeStruct((g_rows, dff), F32)],
        scratch_shapes=[pltpu.VMEM((8, dff), F32)],
        compiler_params=_cparams("arbitrary"), name="finish_full" if full_g else "finish_tiled",
    )(*ins)


def _swap_halves(w):
    hlf = w.shape[-1] // 2
    return jnp.concatenate([-w[..., hlf:], w[..., :hlf]], axis=-1)


def _prep_weights(p):
    d = p['w_in'].shape[0]
    q_rank, kv_rank = p['q_norm_g'].shape[-1], p['kv_norm_g'].shape[-1]
    sizes = [q_rank, kv_rank, MLA_D_ROPE, NSA_HEADS * NSA_DH] + [2 * NSA_GROUPS * NSA_DH] * 3 + [3 * NSA_HEADS, d, d]
    cuts = np.cumsum(sizes)[:-1].tolist()
    cq, ckv, kr, qn, kvc, kvs, kvw, gn, ga, gb = jnp.split(p['w_in'], cuts, axis=-1)
    assert _O_CKV == q_rank and _O_QN == q_rank + kv_rank
    lo, hi = MLA_D_NOPE, HEAD_PAD - MLA_D_NOPE - MLA_D_ROPE
    place = lambda w: jnp.pad(w, ((0, 0), (lo, hi)))
    gnp = jnp.pad(gn, ((0, 0), (0, LANE - gn.shape[1])))
    w_in = jnp.concatenate([cq, ckv, qn, kvc, kvs, kvw, ga, gb, place(kr), place(_swap_halves(kr)), gnp], axis=1)
    w = {'w_in': w_in.astype(BF16)}
    for k in ('norm1_g', 'q_norm_g', 'kv_norm_g', 'norm2_g', 'conv_b'):
        w[k] = p[k].reshape(1, -1)
    w['norm_f_g'] = p['norm_f_g'].reshape(1, -1)
    w['conv_w'] = jnp.pad(p['conv_w'], ((0, 8 - CONV_W), (0, 0)))
    uq = p['w_uq']
    w['w_uq2'] = jnp.pad(uq, ((0, 0), (0, 0), (0, HEAD_PAD - uq.shape[-1]))).reshape(q_rank, -1).astype(BF16)
    uk = p['w_uk']
    w['w_ukp'] = jnp.pad(uk, ((0, 0), (0, 0), (0, HEAD_PAD - MLA_D_NOPE))).reshape(kv_rank, -1).astype(BF16)
    w['w_ukT'] = jnp.pad(jnp.transpose(uk, (1, 2, 0)), ((0, 0), (0, HEAD_PAD - MLA_D_NOPE), (0, 0))).astype(BF16)
    w['w_uvf'] = jnp.pad(p['w_uv'], ((0, 0), (0, 0), (0, HEAD_PAD - MLA_D_V))).reshape(kv_rank, -1).astype(BF16)
    eye_h = jnp.eye(MLA_HEADS, dtype=F32)
    w['w_uvbd'] = jnp.einsum('rhv,hk->hrkv', p['w_uv'], eye_h).reshape(MLA_HEADS * kv_rank, -1).astype(BF16)
    eye_g = jnp.eye(NSA_GROUPS, dtype=F32)
    for nm in ('k', 'v'):
        w1 = p['cmp_w1_' + nm].reshape(2, CMP_STRIDE, NSA_DH, -1)
        big = jnp.einsum('ajdh,gk->jgdakh', w1, eye_g)
        w['cmp_w1' + nm] = big.reshape(CHUNK_FEATS, -1).astype(BF16)
        pos = p['cmp_pos_' + nm].reshape(2, CMP_STRIDE, 1, NSA_DH)
        pos = jnp.broadcast_to(pos, (2, CMP_STRIDE, NSA_GROUPS, NSA_DH)).reshape(2, CHUNK_FEATS)
        w['cmp_pos' + nm] = jnp.pad(pos, ((0, 6), (0, 0))).astype(BF16)
        w2 = p['cmp_w2_' + nm]
        w['cmp_w2' + nm] = jnp.einsum('hd,gk->ghkd', w2, eye_g).reshape(NSA_GROUPS * w2.shape[0], -1).astype(BF16)
    ge = np.zeros((LANE, 3 * NSA_HEADS * NSA_DH), np.float32)
    for h in range(NSA_HEADS):
        for i in range(3):
            ge[h * 3 + i, i * NSA_HEADS * NSA_DH + h * NSA_DH:i * NSA_HEADS * NSA_DH + (h + 1) * NSA_DH] = 1.0
    w['gate_expand'] = jnp.asarray(ge)
    for k in ('w_proj_mla', 'w_proj_nsa', 'w_out', 'w_gate', 'w_up', 'w_down'):
        w[k] = p[k].astype(BF16)
    return w


def _rope_tables(pos):
    inv = ROPE_THETA ** (-jnp.arange(0, MLA_D_ROPE, 2, dtype=F32) / MLA_D_ROPE)
    ang = pos.astype(F32)[:, None] * inv[None, :]
    cos, sin = jnp.cos(ang), jnp.sin(ang)
    n = pos.shape[0]
    pad = jnp.zeros((n, HEAD_PAD - MLA_D_NOPE - MLA_D_ROPE), F32)
    cosq = jnp.concatenate([jnp.ones((n, MLA_D_NOPE), F32), cos, cos, pad], axis=1)
    sinq = jnp.concatenate([jnp.zeros((n, MLA_D_NOPE), F32), sin, sin, pad], axis=1)
    return cosq, sinq


_NSA_GROUPS_SPEC = tuple((tuple(range(g * NSA_HPG, (g + 1) * NSA_HPG)), g) for g in range(NSA_GROUPS))
_MLA_GROUPS_SPEC = tuple(((h,), h) for h in range(MLA_HEADS))


def _prompt(x_prompt, w):
    b, t, d = x_prompt.shape
    n = b * t
    x = x_prompt.reshape(n, d)
    tm = ROW_TILE
    cosq, sinq = _rope_tables(jnp.arange(t, dtype=jnp.int32))
    (q_mla, k_mla, v_mla, ckv, krp, qn, kc, vc, ks, vs, kw, vw, kc_b, vc_b, ks_b, vs_b, kw_b, vw_b, gn, ga, gb) = _inproj(
        x, w, cosq, sinq, sample=False, tm=tm, tab_blocks=t // tm)
    o_mla = _flash(q_mla, k_mla, v_mla, None, batch=b, seq=t, tq=MLA_TILES[0], tk=MLA_TILES[1], groups=_MLA_GROUPS_SPEC, par=8,
                   dq=HEAD_PAD, dk=HEAD_PAD, dv=MLA_D_V, window=0, base2=True, name="mla_prompt")
    kcc, vcc = _compress_prompt(kc_b, vc_b, w, batch=b, seq=t)
    o_cmp, sel_bias = _cmp_prompt(qn, kcc, vcc, batch=b, seq=t, tq=CMP_TQ)
    o_slc = _flash(qn, ks_b, vs_b, sel_bias, batch=b, seq=t, tq=SLC_TILES[0], tk=SLC_TILES[1], groups=_NSA_GROUPS_SPEC, par=2,
                   dq=LANE, dk=LANE, dv=NSA_DH, window=0, base2=False, name="slc_prompt")
    o_win = _win_prompt(qn, kw_b, vw_b, batch=b, seq=t, tq=WINDOW // 2, groups=_NSA_GROUPS_SPEC, dv=NSA_DH)
    dff = w['w_gate'].shape[1]
    y, gtail = _finish(x, o_mla, o_cmp, o_slc, o_win, gn, ga, gb, None, w, tm=tm, period=t, full_g=False,
                       latent=False)
    kv4 = lambda a: a.reshape(1, b, NSA_GROUPS, NSA_DH, a.shape[-1]).transpose(0, 1, 4, 2, 3)
    n_keep = min(WINDOW, t)
    kw, vw = kw[:, :, t - n_keep:], vw[:, :, t - n_keep:]
    conv_state = gtail.reshape(b, t // tm, 8, dff)[:, -1, 8 - (CONV_W - 1):, :]
    states = (ckv.reshape(1, b, t, -1), krp[:, MLA_D_NOPE:MLA_D_NOPE + MLA_D_ROPE].reshape(1, b, t, MLA_D_ROPE),
              kv4(kc), kv4(vc), kv4(ks), kv4(vs), kv4(kw), kv4(vw), conv_state[None])
    return y.reshape(b, t, d), states


def _page_copies(pt_ref, pools, bufs, sems, step, slot, pp):
    copies = []
    for k in range(pp):
        page = pt_ref[step * pp + k]
        for pool, buf, sem in zip(pools, bufs, sems):
            copies.append(pltpu.make_async_copy(pool.at[page], buf.at[slot, k], sem.at[slot]))
    return copies


def _start_all(copies, n_pools):
    for i, c in enumerate(copies):
        c.start(priority=(i // n_pools) % 2)


def _stream_pages(pt_ref, pools, bufs, sems, pp):
    step = pl.program_id(0) * pl.num_programs(1) + pl.program_id(1)
    total = pl.num_programs(0) * pl.num_programs(1)
    slot = step % 2

    @pl.when(step == 0)
    def _():
        _start_all(_page_copies(pt_ref, pools, bufs, sems, step, slot, pp), len(pools))

    @pl.when(step + 1 < total)
    def _():
        _start_all(_page_copies(pt_ref, pools, bufs, sems, step + 1, 1 - slot, pp), len(pools))

    for c in _page_copies(pt_ref, pools, bufs, sems, step, slot, pp):
        c.wait()
    return slot


def _softmax_update(sc, v, m_scr, l_scr, acc_scr, v_transposed=False):
    m_old = m_scr[...]
    m_new = jnp.maximum(m_old, jnp.max(sc, axis=-1, keepdims=True))
    p = jnp.exp(sc - m_new)
    alpha = jnp.exp(m_old - m_new)
    l_scr[...] = alpha * l_scr[...] + jnp.sum(p, axis=-1, keepdims=True)
    pv = _dot_nt(p.astype(BF16), v) if v_transposed else _dot(p.astype(BF16), v)
    acc_scr[...] = alpha * acc_scr[...] + pv
    m_scr[...] = m_new


def _mla_decode_step(slot, q_ref, knew_ref, o_ref, cbuf, rbuf, kscr, krscr, m_scr, l_scr, acc_scr, *, pp, td, kv_rank):
    s = pl.program_id(1)
    q = q_ref[...]
    rows = q.shape[0]

    @pl.when(s == 0)
    def _():
        m_scr[...] = jnp.full_like(m_scr, NEG)
        l_scr[...] = jnp.zeros_like(l_scr)
        acc_scr[...] = jnp.zeros_like(acc_scr)
        kn = knew_ref[...]
        trow = _vdiv(_iota((rows, 1), 0), MLA_HEADS)
        col = _iota((1, kn.shape[0]), 1)
        sc = jnp.where((col <= trow) & (col < td), _dot_nt(q, kn), NEG)
        _softmax_update(sc, kn[:, :kv_rank], m_scr, l_scr, acc_scr)

    for k in range(pp):
        kscr[k * PAGE_SIZE:(k + 1) * PAGE_SIZE, :] = cbuf[slot, k].astype(BF16)
        krscr[:, k * PAGE_SIZE:(k + 1) * PAGE_SIZE] = rbuf[slot, k].astype(BF16)
    kt = kscr[...]
    sc = _dot_nt(q[:, :kv_rank], kt) + _dot(q[:, kv_rank:kv_rank + MLA_D_ROPE], krscr[...])
    _softmax_update(sc, kt, m_scr, l_scr, acc_scr)

    @pl.when(s == pl.num_programs(1) - 1)
    def _():
        o_ref[...] = acc_scr[...] / l_scr[...]


def _alibi_rows(rows, td):
    r = _iota((rows, 1), 0)
    return _vdiv(r, td), _vmod(r, td)


def _slope_rows(hrow, g):
    slope = jnp.zeros(hrow.shape, F32)
    for hh in range(NSA_HPG):
        slope = jnp.where(hrow == hh, SLOPES[g * NSA_HPG + hh], slope)
    return slope


def _cmp_decode_step(slot, q_ref, w1k_ref, w1v_ref, pk_ref, pv_ref, w2k_ref, w2v_ref, o_ref, score_ref,
                     kbuf, vbuf, kp0, kp1, vp0, vp1, yk_scr, yv_scr, *, pp, td, past, n_cmp, n_sel_pad):
    s = pl.program_id(1)
    cpp = PAGE_SIZE // CMP_STRIDE
    hp = pp // 2
    for half, (kp_scr, vp_scr) in enumerate(((kp0, vp0), (kp1, vp1))):
        for k in range(hp):
            kp_scr[k * PAGE_SIZE:(k + 1) * PAGE_SIZE, :] = kbuf[slot, half * hp + k].T
            vp_scr[k * PAGE_SIZE:(k + 1) * PAGE_SIZE, :] = vbuf[slot, half * hp + k].T
    chunk_rows = lambda scr: jnp.concatenate(
        [scr[pl.ds(j, hp * cpp, stride=CMP_STRIDE), :] for j in range(CMP_STRIDE)], axis=1).astype(BF16)
    for half, (kp_scr, vp_scr) in enumerate(((kp0, vp0), (kp1, vp1))):
        r0 = pl.multiple_of(s * (pp * cpp) + half * (hp * cpp), hp * cpp)
        yk_scr[pl.ds(r0, hp * cpp), :] = _dot(chunk_rows(kp_scr), w1k_ref[...])
        yv_scr[pl.ds(r0, hp * cpp), :] = _dot(chunk_rows(vp_scr), w1v_ref[...])

    @pl.when(s == pl.num_programs(1) - 1)
    def _():
        kcc = _compress_finish(yk_scr[...], _dot(pk_ref[...], w1k_ref[...]), w2k_ref).astype(BF16)
        vcc = _compress_finish(yv_scr[...], _dot(pv_ref[...], w1v_ref[...]), w2v_ref).astype(BF16)
        ncp = kcc.shape[0]
        rows = NSA_HPG * td
        hrow, trow = _alibi_rows(rows, td)
        cidx = _iota((1, ncp), 1)
        dist = (past + trow) - (cidx * CMP_STRIDE + CMP_BLOCK - 1)
        mask = (dist >= 0) & (cidx < n_cmp)
        distf = dist.astype(F32)
        tsum = (_vmod(_iota((8, rows), 1), td) == _iota((8, rows), 0)).astype(F32)
        ov = _overlap(ncp, n_sel_pad)
        t8 = _iota((8, 1), 0)
        cur = _vdiv(past + t8, SLC_BLOCK)
        jj = _iota((8, n_sel_pad), 1)
        for g in range(NSA_GROUPS):
            sc = _dot_nt(q_ref[g], kcc) - _slope_rows(hrow, g) * distf
            sc = jnp.where(mask, sc, NEG)
            m = jnp.max(sc, axis=-1, keepdims=True)
            p = jnp.where(mask, jnp.exp(sc - m), 0.0)
            l = jnp.sum(p, axis=-1, keepdims=True)
            p = p / jnp.where(l > 0.0, l, 1.0)
            o_ref[g] = _dot(p.astype(BF16), vcc)
            imp = _dot_exact(tsum, p)
            score_ref[g] = _force_scores(_dot_exact(imp, ov), cur, jj)


def _mla_cmp_decode_kernel(pt_ref, qm_ref, knew_ref, qc_ref, w1k_ref, w1v_ref, pk_ref, pv_ref, w2k_ref, w2v_ref,
                           ckv_hbm, kr_hbm, ck_hbm, cv_hbm, olat_ref, ocmp_ref, score_ref,
                           cbuf, rbuf, kbuf, vbuf, csem, rsem, ksem, vsem,
                           kscr, krscr, m_scr, l_scr, acc_scr, kp0, kp1, vp0, vp1, yk_scr, yv_scr,
                           *, pp, td, kv_rank, past, n_cmp, n_sel_pad):
    slot = _stream_pages(pt_ref, (ckv_hbm, kr_hbm, ck_hbm, cv_hbm), (cbuf, rbuf, kbuf, vbuf),
                         (csem, rsem, ksem, vsem), pp)
    _mla_decode_step(slot, qm_ref, knew_ref, olat_ref, cbuf, rbuf, kscr, krscr, m_scr, l_scr, acc_scr,
                     pp=pp, td=td, kv_rank=kv_rank)
    _cmp_decode_step(slot, qc_ref, w1k_ref, w1v_ref, pk_ref, pv_ref, w2k_ref, w2v_ref, ocmp_ref, score_ref,
                     kbuf, vbuf, kp0, kp1, vp0, vp1, yk_scr, yv_scr, pp=pp, td=td, past=past, n_cmp=n_cmp,
                     n_sel_pad=n_sel_pad)


def _mla_cmp_decode(pt, qd_mla, knew, qd, ckv_pool, kr_pool, k_pool, v_pool, wts, *, td, pp, past, n_sel_pad):
    bd, mrows, qw = qd_mla.shape
    kv_rank = ckv_pool.shape[-1]
    n_pages = pt.shape[0] // bd
    cpp = PAGE_SIZE // CMP_STRIDE
    nch = n_pages * cpp
    n_cmp = (past + td) // CMP_STRIDE - CMP_BLOCK // CMP_STRIDE + 1
    assert (past + td) // CMP_STRIDE == nch, "new rows must not complete a chunk"
    per_b3 = lambda b, s, pt: (b, 0, 0)
    per_b = lambda b, s, pt: (b, 0, 0, 0)
    c2 = lambda b, s, pt: (0, 0)
    rows = qd.shape[2]
    in_specs = ([pl.BlockSpec((None, mrows, qw), per_b3), pl.BlockSpec((None,) + knew.shape[1:], per_b3),
                 pl.BlockSpec((None,) + qd.shape[1:], per_b)]
                + [pl.BlockSpec((CHUNK_FEATS, 2 * LANE), c2) for _ in range(2)]
                + [pl.BlockSpec((8, CHUNK_FEATS), c2) for _ in range(2)]
                + [pl.BlockSpec((LANE, LANE), c2) for _ in range(2)]
                + [pl.BlockSpec(memory_space=pl.ANY) for _ in range(4)])
    kern = functools.partial(_mla_cmp_decode_kernel, pp=pp, td=td, kv_rank=kv_rank, past=past, n_cmp=n_cmp,
                             n_sel_pad=n_sel_pad)
    pools = (ckv_pool, kr_pool, k_pool, v_pool)
    return pl.pallas_call(
        kern,
        grid_spec=pltpu.PrefetchScalarGridSpec(
            num_scalar_prefetch=1, grid=(bd, n_pages // pp), in_specs=in_specs,
            out_specs=[pl.BlockSpec((None, mrows, kv_rank), per_b3),
                       pl.BlockSpec((None, NSA_GROUPS, rows, LANE), per_b),
                       pl.BlockSpec((None, NSA_GROUPS, 8, n_sel_pad), per_b)],
            scratch_shapes=[pltpu.VMEM((2, pp) + p.shape[1:], F32) for p in pools]
            + [pltpu.SemaphoreType.DMA((2,)) for _ in pools]
            + [pltpu.VMEM((pp * PAGE_SIZE, kv_rank), BF16), pltpu.VMEM((MLA_D_ROPE, pp * PAGE_SIZE), BF16),
               pltpu.VMEM((mrows, 1), F32), pltpu.VMEM((mrows, 1), F32), pltpu.VMEM((mrows, kv_rank), F32)]
            + [pltpu.VMEM((pp // 2 * PAGE_SIZE, LANE), F32) for _ in range(4)]
            + [pltpu.VMEM((nch, 2 * LANE), F32) for _ in range(2)]),
        out_shape=[jax.ShapeDtypeStruct((bd, mrows, kv_rank), F32),
                   jax.ShapeDtypeStruct((bd, NSA_GROUPS, rows, LANE), F32),
                   jax.ShapeDtypeStruct((bd, NSA_GROUPS, 8, n_sel_pad), F32)],
        compiler_params=_cparams("arbitrary", "arbitrary"), name="mla_cmp_decode",
    )(pt, qd_mla, knew, qd, wts['cmp_w1k'], wts['cmp_w1v'], wts['cmp_posk'], wts['cmp_posv'],
      wts['cmp_w2k'], wts['cmp_w2v'], *pools)


def _rank_kernel(score_ref, cur_ref, sel_ref, *, n_sel):
    sc = score_ref[...]
    jj = _iota(sc.shape, 0)

    def body(i, rank):
        ri = score_ref[pl.ds(i, 1), :]
        beats = (ri > sc) | ((ri == sc) & (i < jj))
        return rank + beats.astype(F32)

    rank = lax.fori_loop(0, n_sel, body, jnp.zeros(sc.shape, F32))
    sel = (rank < float(min(SLC_TOP_N, n_sel))) & (jj <= cur_ref[...])
    sel_ref[...] = sel.astype(F32)


def _rank(score_t, cur, *, n_sel):
    full = lambda a: pl.BlockSpec(a.shape, lambda: (0,) * a.ndim)
    return pl.pallas_call(
        functools.partial(_rank_kernel, n_sel=n_sel), in_specs=[full(score_t), full(cur)],
        out_specs=full(score_t), out_shape=jax.ShapeDtypeStruct(score_t.shape, F32), name="rank_decode",
    )(score_t, cur)


def _slc_decode_kernel(pt_ref, cnt_ref, q_ref, sel_ref, selnew_ref, knew_ref, vnew_ref, kpos_ref, expand_ref,
                       k_hbm, v_hbm, o_ref, kbuf, vbuf, ksem, vsem, kscr, vscr, m_scr, l_scr, acc_scr,
                       *, pp, td, past):
    b, s, ns = pl.program_id(0), pl.program_id(1), pl.num_programs(1)
    step = b * ns + s
    slot = step % 2
    pools, bufs, sems = (k_hbm, v_hbm), (kbuf, vbuf), (ksem, vsem)
    active = s * pp < cnt_ref[b]
    wrap = s + 1 == ns
    nb = jnp.minimum(jnp.where(wrap, b + 1, b), pl.num_programs(0) - 1)
    next_active = (step + 1 < pl.num_programs(0) * ns) & (jnp.where(wrap, 0, s + 1) * pp < cnt_ref[nb])

    @pl.when((step == 0) & active)
    def _():
        _start_all(_page_copies(pt_ref, pools, bufs, sems, step, slot, pp), len(pools))

    @pl.when(next_active)
    def _():
        _start_all(_page_copies(pt_ref, pools, bufs, sems, step + 1, 1 - slot, pp), len(pools))

    rows = q_ref.shape[1]
    hrow, trow = _alibi_rows(rows, td)

    @pl.when(s == 0)
    def _():
        m_scr[...] = jnp.full_like(m_scr, NEG)
        l_scr[...] = jnp.zeros_like(l_scr)
        acc_scr[...] = jnp.zeros_like(acc_scr)
        kn = knew_ref[...]
        vn = vnew_ref[...]
        col = _iota((1, kn.shape[0]), 1)
        dist = trow - col
        for g in range(NSA_GROUPS):
            sc = _dot_nt(q_ref[g], kn) - _slope_rows(hrow, g) * dist.astype(F32)
            mask = (dist >= 0) & (col < td) & (selnew_ref[g] > 0.5)
            _softmax_update(jnp.where(mask, sc, NEG), vn, m_scr.at[g], l_scr.at[g], acc_scr.at[g])

    @pl.when(active)
    def _():
        for c in _page_copies(pt_ref, pools, bufs, sems, step, slot, pp):
            c.wait()
        for k in range(pp):
            kscr[:, k * PAGE_SIZE:(k + 1) * PAGE_SIZE] = kbuf[slot, k].astype(BF16)
            vscr[:, k * PAGE_SIZE:(k + 1) * PAGE_SIZE] = vbuf[slot, k].astype(BF16)
        kt = kscr[...]
        vt = vscr[...]
        distf = (past + trow).astype(F32) - kpos_ref[...]
        for g in range(NSA_GROUPS):
            sc = _dot(q_ref[g], kt) - _slope_rows(hrow, g) * distf
            selx = _dot(sel_ref[g], expand_ref[...]) > 0.5
            _softmax_update(jnp.where(selx, sc, NEG), vt, m_scr.at[g], l_scr.at[g], acc_scr.at[g], v_transposed=True)

    @pl.when(s == ns - 1)
    def _():
        o_ref[...] = acc_scr[...] / l_scr[...]


def _slc_decode(pt, counts, qd, sel16, selnew, knew, vnew, kpos, k_pool, v_pool, *, td, pp, past):
    bd = qd.shape[0]
    rows = qd.shape[2]
    n_pages = pt.shape[0] // bd
    tk = pp * PAGE_SIZE
    expand = (jnp.arange(tk)[None, :] // SLC_BLOCK == jnp.arange(tk // SLC_BLOCK)[:, None]).astype(BF16)
    per_b = lambda b, s, pt, cnt: (b, 0, 0, 0)
    per_b3 = lambda b, s, pt, cnt: (b, 0, 0)
    in_specs = ([pl.BlockSpec((None,) + qd.shape[1:], per_b),
                 pl.BlockSpec((None, None) + sel16.shape[2:], lambda b, s, pt, cnt: (b, s, 0, 0, 0)),
                 pl.BlockSpec((None,) + selnew.shape[1:], per_b),
                 pl.BlockSpec((None,) + knew.shape[1:], per_b3), pl.BlockSpec((None,) + vnew.shape[1:], per_b3),
                 pl.BlockSpec((None, None, 1, tk), lambda b, s, pt, cnt: (b, s, 0, 0)),
                 pl.BlockSpec(expand.shape, lambda b, s, pt, cnt: (0, 0))]
                + [pl.BlockSpec(memory_space=pl.ANY) for _ in range(2)])
    kern = functools.partial(_slc_decode_kernel, pp=pp, td=td, past=past)
    return pl.pallas_call(
        kern,
        grid_spec=pltpu.PrefetchScalarGridSpec(
            num_scalar_prefetch=2, grid=(bd, n_pages // pp), in_specs=in_specs,
            out_specs=pl.BlockSpec((None, NSA_GROUPS, rows, LANE), per_b),
            scratch_shapes=[pltpu.VMEM((2, pp) + k_pool.shape[1:], F32) for _ in range(2)]
            + [pltpu.SemaphoreType.DMA((2,)) for _ in range(2)]
            + [pltpu.VMEM((LANE, pp * PAGE_SIZE), BF16) for _ in range(2)]
            + [pltpu.VMEM((NSA_GROUPS, rows, 1), F32) for _ in range(2)] + [pltpu.VMEM((NSA_GROUPS, rows, LANE), F32)]),
        out_shape=jax.ShapeDtypeStruct((bd, NSA_GROUPS, rows, LANE), F32),
        compiler_params=_cparams("arbitrary", "arbitrary"), name="slc_decode",
    )(pt, counts, qd, sel16, selnew, knew, vnew, kpos, expand, k_pool, v_pool)


def _win_decode_kernel(q_ref, kst_ref, vst_ref, knew_ref, vnew_ref, o_ref, *, td, past):
    rows = q_ref.shape[1]
    hrow, trow = _alibi_rows(rows, td)
    nbuf = kst_ref.shape[1]
    kst = kst_ref[...].astype(BF16)
    vst = vst_ref[...].astype(BF16)
    kn = knew_ref[...]
    vn = vnew_ref[...]
    kpos = past - nbuf + _iota((1, nbuf), 1)
    d1 = (past + trow) - kpos
    m1 = (d1 >= 0) & (d1 < WINDOW) & (kpos >= 0)
    col = _iota((1, kn.shape[0]), 1)
    d2 = trow - col
    m2 = (d2 >= 0) & (d2 < WINDOW) & (col < td)
    for g in range(NSA_GROUPS):
        slope = _slope_rows(hrow, g)
        s1 = jnp.where(m1, _dot(q_ref[g], kst) - slope * d1.astype(F32), NEG)
        s2 = jnp.where(m2, _dot_nt(q_ref[g], kn) - slope * d2.astype(F32), NEG)
        m = jnp.maximum(jnp.max(s1, axis=-1, keepdims=True), jnp.max(s2, axis=-1, keepdims=True))
        p1 = jnp.exp(s1 - m)
        p2 = jnp.exp(s2 - m)
        l = jnp.sum(p1, axis=-1, keepdims=True) + jnp.sum(p2, axis=-1, keepdims=True)
        o_ref[g] = (_dot_nt(p1.astype(BF16), vst) + _dot(p2.astype(BF16), vn)) / l


def _win_decode(qd, kst, vst, knew, vnew, *, td, past):
    bd = qd.shape[0]
    rows = qd.shape[2]
    per_b = lambda b: (b, 0, 0, 0)
    per_b3 = lambda b: (b, 0, 0)
    blk3 = lambda a: pl.BlockSpec((None,) + a.shape[1:], per_b3)
    return pl.pallas_call(
        functools.partial(_win_decode_kernel, td=td, past=past), grid=(bd,),
        in_specs=[pl.BlockSpec((None,) + qd.shape[1:], per_b), blk3(kst), blk3(vst), blk3(knew), blk3(vnew)],
        out_specs=pl.BlockSpec((None, NSA_GROUPS, rows, LANE), per_b),
        out_shape=jax.ShapeDtypeStruct((bd, NSA_GROUPS, rows, LANE), F32),
        compiler_params=_cparams("parallel"), name="win_decode",
    )(qd, kst, vst, knew, vnew)


def _pad_rows(a, rows):
    return jnp.pad(a, ((0, 0), (0, rows - a.shape[1]), (0, 0)))


def _sample(x_sample, caches, page_table, w):
    (c_ckv, c_krope, c_cmp_k, c_cmp_v, c_slc_k, c_slc_v, s_win_k, s_win_v, s_conv) = caches
    bd, td, d = x_sample.shape
    n = bd * td
    n_pages = page_table.shape[1]
    past = n_pages * PAGE_SIZE
    kv_rank = c_ckv.shape[-1]
    x = x_sample.reshape(n, d)
    pos = past + jnp.arange(td, dtype=jnp.int32)
    cosq, sinq = _rope_tables(jnp.tile(pos, bd))
    (qrot, qabs, ckv, krp, qn, kc, vc, ks, vs, kw, vw, gn, ga, gb) = _inproj(
        x, w, cosq, sinq, sample=True, tm=n, tab_blocks=1)
    pt = page_table.reshape(-1)
    krope = krp[:, MLA_D_NOPE:MLA_D_NOPE + MLA_D_ROPE]
    qrope = qrot.reshape(bd, td, MLA_HEADS, HEAD_PAD)[..., MLA_D_NOPE:MLA_D_NOPE + MLA_D_ROPE]
    qpad = LANE - MLA_D_ROPE
    qd_mla = jnp.concatenate([qabs.reshape(bd, td, MLA_HEADS, kv_rank), qrope,
                              jnp.zeros((bd, td, MLA_HEADS, qpad), F32)], axis=-1)
    qd_mla = qd_mla.reshape(bd, td * MLA_HEADS, kv_rank + LANE).astype(BF16)
    knew = jnp.concatenate([ckv, krope, jnp.zeros((n, qpad), F32)], axis=-1).reshape(bd, td, -1)
    knew = _pad_rows(knew, 8).astype(BF16)
    pos_minor = lambda c: jnp.moveaxis(c, 1, -1).reshape(c.shape[0], -1, c.shape[1])
    qg = qn.astype(F32).reshape(bd, td, NSA_GROUPS, NSA_HPG, NSA_DH).transpose(0, 2, 3, 1, 4)
    qg = qg.reshape(bd, NSA_GROUPS, NSA_HPG * td, NSA_DH)
    lane_g = (jnp.arange(LANE) // NSA_DH)[None, :] == jnp.arange(NSA_GROUPS)[:, None]
    qd = jnp.where(lane_g[None, :, None, :], jnp.tile(qg, (1, 1, 1, NSA_GROUPS)), 0.0).astype(BF16)
    n_sel = -(-(past + td) // SLC_BLOCK)
    n_sel_pad = -(-n_sel // LANE) * LANE
    o_lat, o_cmp, score = _mla_cmp_decode(pt, qd_mla, knew, qd, c_ckv, pos_minor(c_krope), pos_minor(c_cmp_k),
                                          pos_minor(c_cmp_v), w, td=td, pp=min(2 * PAGES_PER_STEP, n_pages), past=past,
                                          n_sel_pad=n_sel_pad)
    cur = jnp.broadcast_to((pos // SLC_BLOCK).astype(jnp.int32), (bd, NSA_GROUPS, td)).reshape(1, -1)
    n_sel8 = -(-n_sel // 8) * 8
    sel_t = _rank(score[:, :, :td, :n_sel8].reshape(-1, n_sel8).T, cur, n_sel=n_sel)
    sel = jnp.pad(sel_t.T, ((0, 0), (0, n_sel_pad - n_sel8))).reshape(bd, NSA_GROUPS, td, n_sel_pad)
    pp_slc = min(PAGES_PER_STEP, n_pages)
    bpp = PAGE_SIZE // SLC_BLOCK
    n_past_blk = past // SLC_BLOCK
    sel_pg = sel[..., :n_past_blk].reshape(bd, NSA_GROUPS * td, n_pages, bpp)
    need_page = sel_pg.max(axis=(1, 3)) > 0.5
    counts = need_page.sum(axis=-1).astype(jnp.int32)
    n_before = jnp.cumsum(need_page.astype(jnp.int32), axis=-1)
    u_before = jnp.cumsum(1 - need_page.astype(jnp.int32), axis=-1)
    slot_of = jnp.where(need_page, n_before - 1, counts[:, None] + u_before - 1)
    perm = slot_of[:, None, :] == jnp.arange(n_pages, dtype=jnp.int32)[None, :, None]
    order = jnp.sum(jnp.where(perm, jnp.arange(n_pages, dtype=jnp.int32), 0), axis=-1)
    pt_slc = jnp.sum(jnp.where(perm, page_table[:, None, :], 0), axis=-1).reshape(-1)
    sel_listed = jnp.einsum('bip,bxpc->bxic', perm.astype(BF16), sel_pg.astype(BF16), preferred_element_type=F32)
    bps = pp_slc * bpp
    sel_steps = sel_listed.reshape(bd, NSA_GROUPS, td, n_pages // pp_slc, bps).transpose(0, 3, 1, 2, 4)
    sel16 = jnp.tile(sel_steps, (1, 1, 1, NSA_HPG, 1)).astype(BF16)
    kpos = (order[:, :, None] * PAGE_SIZE + jnp.arange(PAGE_SIZE, dtype=jnp.int32)).astype(F32)
    kpos = kpos.reshape(bd, n_pages // pp_slc, 1, pp_slc * PAGE_SIZE)
    selnew = jnp.tile(jnp.broadcast_to(sel[..., n_past_blk:n_past_blk + 1], (bd, NSA_GROUPS, td, 8)), (1, 1, NSA_HPG, 1))
    new8 = lambda a: _pad_rows(a.reshape(bd, td, LANE), 8).astype(BF16)
    o_slc = _slc_decode(pt_slc, counts, qd, sel16, selnew, new8(ks), new8(vs), kpos, pos_minor(c_slc_k),
                        pos_minor(c_slc_v), td=td, pp=pp_slc, past=past)
    nbuf = s_win_k.shape[1]
    o_win = _win_decode(qd, pos_minor(s_win_k), pos_minor(s_win_v), new8(kw), new8(vw),
                        td=td, past=past)

    def heads_out(o):
        o = o.reshape(bd, NSA_GROUPS, NSA_HPG, td, NSA_GROUPS, NSA_DH)
        o = jnp.stack([o[:, g, :, :, g] for g in range(NSA_GROUPS)], axis=1)
        return o.transpose(0, 3, 1, 2, 4).reshape(n, NSA_HEADS * NSA_DH)

    dff = w['w_gate'].shape[1]
    assert td >= CONV_W - 1
    prev1 = jnp.concatenate([s_conv[:, 1:2], jnp.zeros((bd, td - 1, dff), F32)], axis=1)
    prev2 = jnp.concatenate([s_conv[:, 0:2], jnp.zeros((bd, td - 2, dff), F32)], axis=1)
    y, g = _finish(x, o_lat.reshape(n, MLA_HEADS * kv_rank), heads_out(o_cmp), heads_out(o_slc), heads_out(o_win),
                   gn, ga, gb, (prev1.reshape(n, dff), prev2.reshape(n, dff)), w, tm=n, period=td, full_g=True,
                   latent=True)
    kv4 = lambda a: a.reshape(1, bd, td, NSA_GROUPS, NSA_DH)
    win = lambda st, new: jnp.concatenate([st, new.reshape(bd, td, NSA_GROUPS, NSA_DH)], axis=1)[None, :, -nbuf:]
    conv_state = jnp.concatenate([s_conv, g.reshape(bd, td, dff)], axis=1)[None, :, -(CONV_W - 1):]
    states = (ckv.reshape(1, bd, td, kv_rank), krope.reshape(1, bd, td, MLA_D_ROPE), kv4(kc), kv4(vc), kv4(ks), kv4(vs),
              win(s_win_k, kw), win(s_win_v, vw), conv_state)
    return y.reshape(bd, td, d), states


def kernel(x_prompt, x_sample, cache_mla_ckv, cache_mla_krope, cache_nsa_cmp_k, cache_nsa_cmp_v, cache_nsa_slc_k, cache_nsa_slc_v, state_win_k, state_win_v, state_ffn_conv, page_table, norm1_g, w_in, q_norm_g, kv_norm_g, w_uq, w_uk, w_uv, cmp_pos_k, cmp_w1_k, cmp_w2_k, cmp_pos_v, cmp_w1_v, cmp_w2_v, w_proj_mla, w_proj_nsa, w_out, norm2_g, w_gate, w_up, conv_w, conv_b, w_down, norm_f_g):
    assert norm1_g.shape[0] == 1, "single-layer trunk"
    p = dict(norm1_g=norm1_g[0], w_in=w_in[0], q_norm_g=q_norm_g[0], kv_norm_g=kv_norm_g[0], w_uq=w_uq[0],
             w_uk=w_uk[0], w_uv=w_uv[0], cmp_pos_k=cmp_pos_k[0], cmp_w1_k=cmp_w1_k[0], cmp_w2_k=cmp_w2_k[0],
             cmp_pos_v=cmp_pos_v[0], cmp_w1_v=cmp_w1_v[0], cmp_w2_v=cmp_w2_v[0], w_proj_mla=w_proj_mla[0],
             w_proj_nsa=w_proj_nsa[0], w_out=w_out[0], norm2_g=norm2_g[0], w_gate=w_gate[0], w_up=w_up[0],
             conv_w=conv_w[0], conv_b=conv_b[0], w_down=w_down[0], norm_f_g=norm_f_g)
    w = _prep_weights(p)
    y_p, ps = _prompt(x_prompt, w)
    caches = (cache_mla_ckv[0], cache_mla_krope[0], cache_nsa_cmp_k[0], cache_nsa_cmp_v[0], cache_nsa_slc_k[0],
              cache_nsa_slc_v[0], state_win_k[0], state_win_v[0], state_ffn_conv[0])
    y_s, ss = _sample(x_sample, caches, page_table, w)
    out = [y_p, y_s]
    for a, b in zip(ps, ss):
        out += [a, b]
    return tuple(out)
```

```python
import functools

import numpy as np
import jax
import jax.numpy as jnp
from jax import lax
from jax.experimental import pallas as pl
from jax.experimental.pallas import tpu as pltpu

MLA_HEADS = 8
MLA_D_NOPE = 64
MLA_D_ROPE = 32
MLA_D_V = 64
ROPE_THETA = 10000.0
MLA_SCALE = (MLA_D_NOPE + MLA_D_ROPE) ** -0.5
NSA_HEADS = 8
NSA_GROUPS = 2
NSA_HPG = NSA_HEADS // NSA_GROUPS
NSA_DH = 64
NSA_SCALE = NSA_DH ** -0.5
CMP_BLOCK = 32
CMP_STRIDE = 16
SLC_BLOCK = 64
SLC_TOP_N = 16
WINDOW = 512
CONV_W = 3
PAGE_SIZE = 128
EPS = 1e-6
NEG = -1e30
FORCE = 1e9

LOG2E = 1.4426950408889634
LANE = 128
AUX_POS_HI, AUX_POS_LO, AUX_BLK0 = 64, 65, 72
HEAD_PAD = 128
CHUNK_FEATS = CMP_STRIDE * NSA_GROUPS * NSA_DH
FFN_BLOCK = 1536
SLOPES =tuple(float(2.0 ** (-8.0 * (h + 1) / NSA_HEADS)) for h in range(NSA_HEADS))
VMEM_LIMIT = 56 * 1024 * 1024
ROW_TILE = 256
MLA_TILES = (512, 512)
SLC_TILES = (256, 512)
CMP_TQ = 128
PAGES_PER_STEP = 32

F32 = jnp.float32
BF16 = jnp.bfloat16
_NT = (((1,), (1,)), ((), ()))


def _cparams(*sem):
    return pltpu.CompilerParams(dimension_semantics=sem, vmem_limit_bytes=VMEM_LIMIT)


def _rms(x, g):
    return x * lax.rsqrt(jnp.mean(x * x, axis=-1, keepdims=True) + EPS) * g


def _dot(a, b):
    return jnp.dot(a, b, preferred_element_type=F32)


def _dot_nt(a, b):
    return lax.dot_general(a, b, _NT, preferred_element_type=F32)


def _dot_exact(a, b):
    return jnp.dot(a, b, preferred_element_type=F32, precision=lax.Precision.HIGHEST)


def _iota(shape, dim):
    return lax.broadcasted_iota(jnp.int32, shape, dim)


def _log2(n):
    assert n > 0 and n & (n - 1) == 0, n
    return n.bit_length() - 1


def _vdiv(x, n):
    return lax.shift_right_logical(x, jnp.full(x.shape, _log2(n), jnp.int32))


def _vmod(x, n):
    assert n & (n - 1) == 0, n
    return x & (n - 1)


_O_CQ, _O_CKV, _O_QN, _O_K6, _O_GA = 0, 384, 640, 1152, 1920


def _inproj_kernel(x_ref, g1_ref, w_ref, qg_ref, kvg_ref, wuq_ref, wk_ref, wv_ref, cos_ref, sin_ref,
                   *outs, sample, q_rank, kv_rank, d_model, tiles_per_seq, q_scale):
    o_gb = _O_GA + d_model
    o_kr = o_gb + d_model
    x = x_ref[...]
    hn = _rms(x, g1_ref[...])
    y = _dot(hn.astype(BF16), w_ref[...])
    cosq = cos_ref[...]
    sinq = sin_ref[...]
    nq = MLA_HEADS * HEAD_PAD
    cqn = _rms(y[:, _O_CQ:_O_CQ + q_rank], qg_ref[...])
    q2 = _dot(cqn.astype(BF16), wuq_ref[...])
    half = MLA_D_ROPE // 2
    lane_h = _iota((x.shape[0], HEAD_PAD), 1)

    def rotated(qh):
        partner = jnp.where(lane_h < MLA_D_NOPE + half, -pltpu.roll(qh, HEAD_PAD - half, 1), pltpu.roll(qh, half, 1))
        return (qh * cosq + partner * sinq) * q_scale

    ckv = _rms(y[:, _O_CKV:_O_CKV + kv_rank], kvg_ref[...])
    kr = y[:, o_kr:o_kr + LANE] * cosq + y[:, o_kr + LANE:o_kr + 2 * LANE] * sinq
    ckv_b = ckv.astype(BF16)
    it = iter(outs)
    if sample:
        qrot_ref, qabs_ref = next(it), next(it)
        for h in range(MLA_HEADS):
            sl = slice(h * HEAD_PAD, (h + 1) * HEAD_PAD)
            qh = rotated(q2[:, sl])
            qrot_ref[:, sl] = qh
            qabs_ref[:, h * kv_rank:(h + 1) * kv_rank] = _dot(qh.astype(BF16), wk_ref[h])
    else:
        q_ref, k_ref, v_ref = next(it), next(it), next(it)
        knp = _dot(ckv_b, wk_ref[...])
        for h in range(MLA_HEADS):
            sl = slice(h * HEAD_PAD, (h + 1) * HEAD_PAD)
            qh = rotated(q2[:, sl])
            q_ref[:, sl] = qh.astype(BF16)
            k_ref[:, sl] = (knp[:, sl] + kr).astype(BF16)
        ones_lane = (_vmod(_iota((1, nq), 1), HEAD_PAD) == MLA_D_V).astype(F32)
        v_ref[...] = (_dot(ckv_b, wv_ref[...]) + ones_lane).astype(BF16)
    ckv_ref, kr_ref, qn_ref = next(it), next(it), next(it)
    ckv_ref[...] = ckv
    kr_ref[...] = kr
    yq = y[:, _O_QN:_O_QN + NSA_HEADS * NSA_DH] * NSA_SCALE
    if sample:
        qn_ref[...] = yq.astype(BF16)
    else:
        lane_q = _iota((x.shape[0], LANE), 1)
        for h in range(NSA_HEADS):
            pair = yq[:, (h // 2) * LANE:(h // 2 + 1) * LANE]
            if h % 2:
                pair = pltpu.roll(pair, NSA_DH, 1)
            aux_q = jnp.where((lane_q == AUX_POS_HI) | (lane_q == AUX_POS_LO), SLOPES[h], 0.0)
            qn_ref[:, h * LANE:(h + 1) * LANE] = jnp.where(lane_q < NSA_DH, pair, aux_q).astype(BF16)
    for j in range(6):
        yj = y[:, _O_K6 + j * LANE:_O_K6 + (j + 1) * LANE]
        if sample:
            next(it)[...] = yj
        else:
            next(it)[...] = yj.T
    if not sample:
        tm = x.shape[0]
        pos = (pl.program_id(0) % tiles_per_seq) * tm + _iota((tm, 1), 0)
        lane = _iota((tm, LANE), 1)
        onehot = ((lane >= AUX_BLK0) & (_vdiv(pos, SLC_BLOCK) == lane - AUX_BLK0)).astype(F32)
        aux = jnp.where(lane == AUX_POS_HI, (pos - _vmod(pos, 256)).astype(F32),
                        jnp.where(lane == AUX_POS_LO, _vmod(pos, 256).astype(F32), onehot))
        for j in range(6):
            yj = y[:, _O_K6 + j * LANE:_O_K6 + (j + 1) * LANE]
            ref = next(it)
            if j in (2, 3, 4, 5):
                tail = aux if j in (2, 4) else (lane == NSA_DH).astype(F32)
                ref[:, :LANE] = jnp.where(lane < NSA_DH, yj, tail).astype(BF16)
                ref[:, LANE:] = jnp.where(lane < NSA_DH, pltpu.roll(yj, NSA_DH, 1), tail).astype(BF16)
            else:
                ref[...] = yj.astype(BF16)
    gn_ref, ga_ref, gb_ref = next(it), next(it), next(it)
    gn_ref[...] = jax.nn.sigmoid(y[:, o_kr + 2 * LANE:o_kr + 3 * LANE])
    ga_ref[...] = jax.nn.sigmoid(y[:, _O_GA:_O_GA + d_model])
    gb_ref[...] = jax.nn.sigmoid(y[:, o_gb:o_gb + d_model])


def _inproj(x, wts, cosq, sinq, *, sample, tm, tab_blocks):
    n, d = x.shape
    q_rank, kv_rank = wts['q_norm_g'].shape[1], wts['kv_norm_g'].shape[1]
    nq = MLA_HEADS * HEAD_PAD
    wk = wts['w_ukT'] if sample else wts['w_ukp']
    row = lambda i: (i, 0)
    const2 = lambda i: (0, 0)
    tab_map = (lambda i: (i % tab_blocks, 0))
    in_specs = [
        pl.BlockSpec((tm, d), row),
        pl.BlockSpec((1, d), const2),
        pl.BlockSpec(wts['w_in'].shape, const2),
        pl.BlockSpec((1, q_rank), const2),
        pl.BlockSpec((1, kv_rank), const2),
        pl.BlockSpec(wts['w_uq2'].shape, const2),
        pl.BlockSpec(wk.shape, (lambda i: (0, 0, 0)) if sample else const2),
        pl.BlockSpec(wts['w_uvf'].shape, const2),
        pl.BlockSpec((tm, LANE), tab_map),
        pl.BlockSpec((tm, LANE), tab_map),
    ]
    shapes = []
    if sample:
        shapes += [(nq, F32), (MLA_HEADS * kv_rank, F32)]
    else:
        shapes += [(nq, BF16), (nq, BF16), (nq, BF16)]
    shapes += [(kv_rank, F32), (LANE, F32), (NSA_HEADS * (NSA_DH if sample else LANE), BF16)]
    n_lead = len(shapes)
    shapes += [(LANE, F32)] * 6
    if not sample:
        shapes += [(LANE, BF16), (LANE, BF16)] + [(2 * LANE, BF16)] * 4
        assert AUX_BLK0 + -(-tab_blocks * tm // SLC_BLOCK) <= LANE, "block one-hot must fit the aux lanes"
    shapes += [(LANE, F32), (d, F32), (d, F32)]
    out_shape = [jax.ShapeDtypeStruct((n, w), dt) for w, dt in shapes]
    out_specs = [pl.BlockSpec((tm, w), row) for w, _ in shapes]
    if not sample:
        tpb = tab_blocks
        for j in range(n_lead, n_lead + 6):
            out_shape[j] = jax.ShapeDtypeStruct((n // (tpb * tm), LANE, tpb * tm), F32)
            out_specs[j] = pl.BlockSpec((None, LANE, tm), lambda i: (i // tpb, 0, i % tpb))
    q_scale = MLA_SCALE if sample else MLA_SCALE * LOG2E
    kern = functools.partial(_inproj_kernel, sample=sample, q_rank=q_rank, kv_rank=kv_rank, d_model=d,
                             tiles_per_seq=tab_blocks, q_scale=q_scale)
    return pl.pallas_call(
        kern, grid=(n // tm,), in_specs=in_specs, out_specs=out_specs, out_shape=out_shape,
        compiler_params=_cparams("parallel"), name="inproj_sample" if sample else "inproj_prompt",
    )(x, wts['norm1_g'], wts['w_in'], wts['q_norm_g'], wts['kv_norm_g'], wts['w_uq2'], wk, wts['w_uvf'],
      cosq, sinq)


def _stack_heads(q_ref, heads, width):
    parts = [q_ref[:, h * width:(h + 1) * width] for h in heads]
    return parts[0] if len(parts) == 1 else jnp.concatenate(parts, axis=0)


def _flash_kernel(*refs, tq, tk, groups, par, dq, dk, dv, window, has_bias, base2):
    if has_bias:
        q_ref, k_ref, v_ref, bias_ref, o_ref = refs
    else:
        q_ref, k_ref, v_ref, o_ref = refs
    q_start = pl.program_id(1) * tq
    n_hi = (q_start + tq - 1) // tk + 1
    hi_full = (q_start + 1) // tk
    if window:
        n_lo = jnp.maximum(q_start - (window - 1), 0) // tk
        lo_full = (jnp.maximum(q_start + tq - window, 0) + tk - 1) // tk
    else:
        n_lo, lo_full = 0, 0
    e1 = jnp.clip(lo_full, n_lo, n_hi)
    e2 = jnp.clip(hi_full, e1, n_hi)
    ex = jnp.exp2 if base2 else jnp.exp
    for c0 in range(0, len(groups), par):
        chunk = groups[c0:c0 + par]
        qs = []
        for heads, kcol in chunk:
            qg = _stack_heads(q_ref, heads, dq)
            if has_bias:
                bias = bias_ref[:, kcol * LANE:(kcol + 1) * LANE]
                qg = qg + jnp.concatenate([bias] * len(heads), axis=0)
            qs.append(qg)
        rows = qs[0].shape[0]
        qpos = q_start + (_iota((rows, 1), 0) & (tq - 1))

        def step(j, carry, masked):
            k0 = pl.multiple_of(j * tk, tk)
            if masked:
                dist = qpos - (k0 + _iota((1, tk), 1))
                mask = dist >= 0
                if window:
                    mask = mask & (dist < window)
            out = []
            for (heads, kcol), qg, (m, acc) in zip(chunk, qs, carry):
                kt = k_ref[pl.ds(k0, tk), kcol * dk:(kcol + 1) * dk]
                vt = v_ref[pl.ds(k0, tk), kcol * LANE:(kcol + 1) * LANE]
                s = _dot_nt(qg, kt)
                if masked:
                    s = jnp.where(mask, s, NEG)
                m_new = jnp.maximum(m, jnp.max(s, axis=-1, keepdims=True))
                p = ex((s - m_new).astype(BF16))
                acc = ex(m - m_new) * acc + _dot(p, vt)
                out.append((m_new, acc))
            return tuple(out)

        carry = tuple((jnp.full((rows, 1), NEG, F32), jnp.zeros((rows, LANE), F32)) for _ in chunk)
        carry = lax.fori_loop(n_lo, e1, functools.partial(step, masked=True), carry)
        carry = lax.fori_loop(e1, e2, functools.partial(step, masked=False), carry)
        carry = lax.fori_loop(e2, n_hi, functools.partial(step, masked=True), carry)
        for (heads, kcol), (m, acc) in zip(chunk, carry):
            o = acc[:, :dv] * (1.0 / acc[:, dv:dv + 1])
            for hh, h in enumerate(heads):
                o_ref[:, h * dv:(h + 1) * dv] = o[hh * tq:(hh + 1) * tq].astype(o_ref.dtype)


def _flash(q, k, v, bias, *, batch, seq, tq, tk, groups, par, dq, dk, dv, window, base2, name):
    n = q.shape[0]
    has_bias = bias is not None
    n_heads = sum(len(g[0]) for g in groups)
    qrow = lambda b, i: (b * (seq // tq) + i, 0)
    kv = lambda b, i: (b, 0)
    in_specs = [pl.BlockSpec((tq, q.shape[1]), qrow), pl.BlockSpec((seq, k.shape[1]), kv),
                pl.BlockSpec((seq, v.shape[1]), kv)]
    args = [q, k, v]
    if has_bias:
        in_specs.append(pl.BlockSpec((tq, bias.shape[1]), qrow))
        args.append(bias)
    kern = functools.partial(_flash_kernel, tq=tq, tk=tk, groups=groups, par=par, dq=dq, dk=dk, dv=dv, window=window,
                             has_bias=has_bias, base2=base2)
    return pl.pallas_call(
        kern, grid=(batch, seq // tq), in_specs=in_specs,
        out_specs=pl.BlockSpec((tq, n_heads * dv), qrow),
        out_shape=jax.ShapeDtypeStruct((n, n_heads * dv), BF16),
        compiler_params=_cparams("parallel", "arbitrary"), name=name,
    )(*args)


def _win_kernel(q_ref, k_ref, v_ref, o_ref, *, tq, groups, dv):
    i = pl.program_id(1)
    rq = _iota((tq, tq), 0)
    ck = _iota((tq, tq), 1)
    bias_own = jnp.where(ck <= rq, 0.0, NEG)
    bias_far = jnp.where(ck > rq, 0.0, NEG) + jnp.where(i >= 2, 0.0, NEG)
    bias_mid = jnp.where(i >= 1, 0.0, NEG)
    starts = (jnp.maximum(i - 2, 0) * tq, jnp.maximum(i - 1, 0) * tq, i * tq)
    for heads, kcol in groups:
        qg = _stack_heads(q_ref, heads, LANE)
        nh = len(heads)
        ss, vs = [], []
        for k0, bias in zip(starts, (bias_far, None, bias_own)):
            k0 = pl.multiple_of(k0, tq)
            s = _dot_nt(qg, k_ref[pl.ds(k0, tq), kcol * LANE:(kcol + 1) * LANE])
            s = s + (bias_mid if bias is None else jnp.concatenate([bias] * nh, axis=0))
            ss.append(s)
            vs.append(v_ref[pl.ds(k0, tq), kcol * LANE:(kcol + 1) * LANE])
        m = functools.reduce(jnp.maximum, [jnp.max(s, axis=-1, keepdims=True) for s in ss])
        acc = functools.reduce(jnp.add, [_dot(jnp.exp((s - m).astype(BF16)), v) for s, v in zip(ss, vs)])
        o = acc[:, :dv] * (1.0 / acc[:, dv:dv + 1])
        for hh, h in enumerate(heads):
            o_ref[:, h * dv:(h + 1) * dv] = o[hh * tq:(hh + 1) * tq].astype(o_ref.dtype)


def _win_prompt(q, k, v, *, batch, seq, tq, groups, dv):
    assert WINDOW == 2 * tq and seq % tq == 0
    n = q.shape[0]
    n_heads = sum(len(g[0]) for g in groups)
    qrow = lambda b, i: (b * (seq // tq) + i, 0)
    kv = lambda b, i: (b, 0)
    return pl.pallas_call(
        functools.partial(_win_kernel, tq=tq, groups=groups, dv=dv), grid=(batch, seq // tq),
        in_specs=[pl.BlockSpec((tq, q.shape[1]), qrow), pl.BlockSpec((seq, k.shape[1]), kv),
                  pl.BlockSpec((seq, v.shape[1]), kv)],
        out_specs=pl.BlockSpec((tq, n_heads * dv), qrow),
        out_shape=jax.ShapeDtypeStruct((n, n_heads * dv), BF16),
        compiler_params=_cparams("parallel", "arbitrary"), name="win_prompt",
    )(q, k, v)


def _compress_finish(y, posy, w2_ref):
    rows = y.shape[0]
    a = y[:, :LANE]
    b = pltpu.roll(y[:, LANE:], rows - 1, 0)
    pos = posy[0:1, :LANE] + posy[1:2, LANE:]
    hid = jax.nn.gelu(a + b + pos)
    return _dot(hid.astype(BF16), w2_ref[...])


def _compress_prompt_kernel(xk_ref, xv_ref, w1k_ref, w1v_ref, pk_ref, pv_ref, w2k_ref, w2v_ref, ok_ref, ov_ref):
    yk = _dot(xk_ref[...], w1k_ref[...])
    yv = _dot(xv_ref[...], w1v_ref[...])
    ok_ref[...] = _compress_finish(yk, _dot(pk_ref[...], w1k_ref[...]), w2k_ref).astype(BF16)
    ov_ref[...] = _compress_finish(yv, _dot(pv_ref[...], w1v_ref[...]), w2v_ref).astype(BF16)


def _compress_prompt(kc, vc, wts, *, batch, seq):
    nch = seq // CMP_STRIDE
    xk = kc.reshape(batch * nch, CHUNK_FEATS)
    xv = vc.reshape(batch * nch, CHUNK_FEATS)
    row = lambda b: (b, 0)
    c2 = lambda b: (0, 0)
    wspec = pl.BlockSpec((CHUNK_FEATS, 2 * LANE), c2)
    pspec = pl.BlockSpec((8, CHUNK_FEATS), c2)
    w2spec = pl.BlockSpec((LANE, LANE), c2)
    return pl.pallas_call(
        _compress_prompt_kernel, grid=(batch,),
        in_specs=[pl.BlockSpec((nch, CHUNK_FEATS), row), pl.BlockSpec((nch, CHUNK_FEATS), row),
                  wspec, wspec, pspec, pspec, w2spec, w2spec],
        out_specs=[pl.BlockSpec((nch, LANE), row)] * 2,
        out_shape=[jax.ShapeDtypeStruct((batch * nch, LANE), BF16)] * 2,
        compiler_params=_cparams("parallel"), name="compress_prompt",
    )(xk, xv, wts['cmp_w1k'], wts['cmp_w1v'], wts['cmp_posk'], wts['cmp_posv'], wts['cmp_w2k'], wts['cmp_w2v'])


def _overlap(n_rows, n_sel):
    c = _iota((n_rows, n_sel), 0) * CMP_STRIDE
    j = _iota((n_rows, n_sel), 1) * SLC_BLOCK
    return ((c < j + SLC_BLOCK) & (c + CMP_BLOCK > j)).astype(F32)


def _force_scores(score, cur, jj):
    forced = (jj == 0) | (jj == cur) | (jj == cur - 1)
    score = jnp.where(forced, FORCE, score)
    return jnp.where(jj <= cur, score, NEG)


def _cmp_prompt_kernel(q_ref, k_ref, v_ref, o_ref, bias_ref, *, tq, n_cmp, n_sel):
    q_start = pl.program_id(1) * tq
    ncp = k_ref.shape[0]
    rows = NSA_HPG * tq
    qpos = q_start + (_iota((rows, 1), 0) & (tq - 1))
    cidx = _iota((1, ncp), 1)
    dist = qpos - (cidx * CMP_STRIDE + CMP_BLOCK - 1)
    mask = (dist >= 0) & (cidx < n_cmp)
    distf = dist.astype(F32)
    hrow = _vdiv(_iota((rows, 1), 0), tq)
    nsp = -(-n_sel // 8) * 8
    cur = _vdiv(q_start + _iota((1, tq), 1), SLC_BLOCK)
    jj = _iota((nsp, tq), 0)
    cb = _iota((nsp, ncp), 1) * CMP_STRIDE
    jb = _iota((nsp, ncp), 0) * SLC_BLOCK
    ov_t = ((cb < jb + SLC_BLOCK) & (cb + CMP_BLOCK > jb)).astype(F32)
    for g in range(NSA_GROUPS):
        heads = range(g * NSA_HPG, (g + 1) * NSA_HPG)
        qg = jnp.concatenate([q_ref[:, h * LANE:h * LANE + NSA_DH] for h in heads], axis=0)
        slope = jnp.zeros((rows, 1), F32)
        for hh, h in enumerate(heads):
            slope = jnp.where(hrow == hh, SLOPES[h], slope)
        s = _dot_nt(qg, k_ref[:, g * NSA_DH:(g + 1) * NSA_DH]) - slope * distf
        s = jnp.where(mask, s, NEG)
        m = jnp.max(s, axis=-1, keepdims=True)
        p = jnp.where(mask, jnp.exp(s - m), 0.0)
        l = jnp.sum(p, axis=-1, keepdims=True)
        p = p / jnp.where(l > 0.0, l, 1.0)
        o = _dot(p.astype(BF16), v_ref[:, g * NSA_DH:(g + 1) * NSA_DH])
        imp = p[0:tq]
        for hh in range(1, NSA_HPG):
            imp = imp + p[hh * tq:(hh + 1) * tq]
        for hh, h in enumerate(heads):
            o_ref[:, h * NSA_DH:(h + 1) * NSA_DH] = o[hh * tq:(hh + 1) * tq].astype(BF16)
        score = lax.dot_general(ov_t, imp, _NT, preferred_element_type=F32, precision=lax.Precision.HIGHEST)
        score = _force_scores(score, cur, jj)
        rank = jnp.zeros((nsp, tq), F32)
        for i in range(n_sel):
            ri = score[i:i + 1, :]
            beats = (ri > score) | ((ri == score) & (i < jj))
            rank = rank + beats.astype(F32)
        sel = (rank < float(min(SLC_TOP_N, n_sel))) & (jj <= cur)
        bias_t = jnp.where(sel | (jj >= n_sel), 0.0, NEG)
        bias_t = jnp.concatenate([jnp.zeros((AUX_BLK0, tq), F32), bias_t,
                                  jnp.zeros((LANE - AUX_BLK0 - nsp, tq), F32)], axis=0)
        bias_ref[:, g * LANE:(g + 1) * LANE] = bias_t.T.astype(BF16)


def _cmp_prompt(qn, kcc, vcc, *, batch, seq, tq):
    n = qn.shape[0]
    nch = seq // CMP_STRIDE
    n_cmp = nch - CMP_BLOCK // CMP_STRIDE + 1
    n_sel = -(-seq // SLC_BLOCK)
    qrow = lambda b, i: (b * (seq // tq) + i, 0)
    kv = lambda b, i: (b, 0)
    kern = functools.partial(_cmp_prompt_kernel, tq=tq, n_cmp=n_cmp, n_sel=n_sel)
    return pl.pallas_call(
        kern, grid=(batch, seq // tq),
        in_specs=[pl.BlockSpec((tq, qn.shape[1]), qrow), pl.BlockSpec((nch, LANE), kv), pl.BlockSpec((nch, LANE), kv)],
        out_specs=[pl.BlockSpec((tq, NSA_HEADS * NSA_DH), qrow), pl.BlockSpec((tq, NSA_GROUPS * LANE), qrow)],
        out_shape=[jax.ShapeDtypeStruct((n, NSA_HEADS * NSA_DH), BF16),
                   jax.ShapeDtypeStruct((n, NSA_GROUPS * LANE), BF16)],
        compiler_params=_cparams("parallel", "arbitrary"), name="cmp_prompt",
    )(qn, kcc, vcc)


def _finish_kernel(*refs, period, latent, has_state):
    it = iter(refs)
    x_ref, omla_ref, ocmp_ref, oslc_ref, owin_ref, gn_ref, ga_ref, gb_ref = (next(it) for _ in range(8))
    prev1_ref, prev2_ref = (next(it), next(it)) if has_state else (None, None)
    gx_ref = next(it)
    wuv_ref = next(it) if latent else None
    (wpm_ref, wpn_ref, wo_ref, g2_ref, wg_ref, wu_ref, cw_ref, cb_ref, wd_ref, gf_ref,
     y_ref, gout_ref, carry_ref) = it
    tm = x_ref.shape[0]
    gn = gn_ref[...]
    gn_hi = gn.astype(BF16)
    gn_lo = (gn - gn_hi.astype(F32)).astype(BF16)
    gexp = _dot(gn_hi, gx_ref[...]) + _dot(gn_lo, gx_ref[...])
    w = NSA_HEADS * NSA_DH
    o_nsa = (gexp[:, 0:w] * ocmp_ref[...].astype(F32) + gexp[:, w:2 * w] * oslc_ref[...].astype(F32)
             + gexp[:, 2 * w:3 * w] * owin_ref[...].astype(F32))
    o_mla = omla_ref[...].astype(BF16)
    if latent:
        o_mla = _dot(o_mla, wuv_ref[...]).astype(BF16)
    merged = (ga_ref[...] * _dot(o_mla, wpm_ref[...])
              + gb_ref[...] * _dot(o_nsa.astype(BF16), wpn_ref[...]))
    x1 = x_ref[...] + _dot(merged.astype(BF16), wo_ref[...])
    h2 = _rms(x1, g2_ref[...]).astype(BF16)
    row = _iota((tm, 1), 0)
    i = pl.program_id(0)
    t = _vmod(i * tm + row, period)
    carried = period > tm
    if carried:
        @pl.when(i == 0)
        def _():
            carry_ref[...] = jnp.zeros_like(carry_ref)
    dff = wg_ref.shape[1]
    x2 = x1
    for c0 in range(0, dff, FFN_BLOCK):
        cs = slice(c0, min(c0 + FFN_BLOCK, dff))
        g = _dot(h2, wg_ref[:, cs])
        u = _dot(h2, wu_ref[:, cs])
        g1 = pltpu.roll(g, 1, 0)
        g2 = pltpu.roll(g, 2, 0)
        if carried:
            c = carry_ref[:, cs]
            g1 = jnp.where(row == 0, c[7:8], g1)
            g2 = jnp.where(row == 0, c[6:7], jnp.where(row == 1, c[7:8], g2))
            carry_ref[:, cs] = g[tm - 8:tm]
        g1 = jnp.where(t >= 1, g1, prev1_ref[:, cs] if has_state else 0.0)
        g2 = jnp.where(t >= 2, g2, prev2_ref[:, cs] if has_state else 0.0)
        cw = cw_ref[:, cs]
        conv = cb_ref[:, cs] + cw[0:1] * g2 + cw[1:2] * g1 + cw[2:3] * g
        act = (jax.nn.silu(conv) * u).astype(BF16)
        x2 = x2 + _dot(act, wd_ref[cs, :])
        gout_ref[:, cs] = g[tm - 8:tm] if gout_ref.shape[0] == 8 else g
    y_ref[...] = _rms(x2, gf_ref[...])


def _finish(x, omla, ocmp, oslc, owin, gn, ga, gb, state_rows, wts, *, tm, period, full_g, latent):
    n, d = x.shape
    dff = wts['w_gate'].shape[1]
    row = lambda i: (i, 0)
    c2 = lambda i: (0, 0)
    acts = [x, omla, ocmp, oslc, owin, gn, ga, gb] + (list(state_rows) if state_rows is not None else [])
    consts = [wts['gate_expand']] + ([wts['w_uvbd']] if latent else []) + [
        wts['w_proj_mla'], wts['w_proj_nsa'], wts['w_out'], wts['norm2_g'],
        wts['w_gate'], wts['w_up'], wts['conv_w'], wts['conv_b'], wts['w_down'], wts['norm_f_g']]
    ins = acts + consts
    in_specs = [pl.BlockSpec((tm, a.shape[1]), row) for a in acts] + [pl.BlockSpec(a.shape, c2) for a in consts]
    g_rows = n if full_g else (n // tm) * 8
    g_blk = tm if full_g else 8
    kern = functools.partial(_finish_kernel, period=period, latent=latent, has_state=state_rows is not None)
    return pl.pallas_call(
        kern, grid=(n // tm,), in_specs=in_specs,
        out_specs=[pl.BlockSpec((tm, d), row), pl.BlockSpec((g_blk, dff), row)],
        out_shape=[jax.ShapeDtypeStruct((n, d), F32), jax.ShapeDtypeStruct((g_rows, dff), F32)],
        scratch_shapes=[pltpu.VMEM((8, dff), F32)],
        compiler_params=_cparams("arbitrary"), name="finish_full" if full_g else "finish_tiled",
    )(*ins)


def _swap_halves(w):
    hlf = w.shape[-1] // 2
    return jnp.concatenate([-w[..., hlf:], w[..., :hlf]], axis=-1)


def _prep_weights(p):
    d = p['w_in'].shape[0]
    q_rank, kv_rank = p['q_norm_g'].shape[-1], p['kv_norm_g'].shape[-1]
    sizes = [q_rank, kv_rank, MLA_D_ROPE, NSA_HEADS * NSA_DH] + [2 * NSA_GROUPS * NSA_DH] * 3 + [3 * NSA_HEADS, d, d]
    cuts = np.cumsum(sizes)[:-1].tolist()
    cq, ckv, kr, qn, kvc, kvs, kvw, gn, ga, gb = jnp.split(p['w_in'], cuts, axis=-1)
    assert _O_CKV == q_rank and _O_QN == q_rank + kv_rank
    lo, hi = MLA_D_NOPE, HEAD_PAD - MLA_D_NOPE - MLA_D_ROPE
    place = lambda w: jnp.pad(w, ((0, 0), (lo, hi)))
    gnp = jnp.pad(gn, ((0, 0), (0, LANE - gn.shape[1])))
    w_in = jnp.concatenate([cq, ckv, qn, kvc, kvs, kvw, ga, gb, place(kr), place(_swap_halves(kr)), gnp], axis=1)
    w = {'w_in': w_in.astype(BF16)}
    for k in ('norm1_g', 'q_norm_g', 'kv_norm_g', 'norm2_g', 'conv_b'):
        w[k] = p[k].reshape(1, -1)
    w['norm_f_g'] = p['norm_f_g'].reshape(1, -1)
    w['conv_w'] = jnp.pad(p['conv_w'], ((0, 8 - CONV_W), (0, 0)))
    uq = p['w_uq']
    w['w_uq2'] = jnp.pad(uq, ((0, 0), (0, 0), (0, HEAD_PAD - uq.shape[-1]))).reshape(q_rank, -1).astype(BF16)
    uk = p['w_uk']
    w['w_ukp'] = jnp.pad(uk, ((0, 0), (0, 0), (0, HEAD_PAD - MLA_D_NOPE))).reshape(kv_rank, -1).astype(BF16)
    w['w_ukT'] = jnp.pad(jnp.transpose(uk, (1, 2, 0)), ((0, 0), (0, HEAD_PAD - MLA_D_NOPE), (0, 0))).astype(BF16)
    w['w_uvf'] = jnp.pad(p['w_uv'], ((0, 0), (0, 0), (0, HEAD_PAD - MLA_D_V))).reshape(kv_rank, -1).astype(BF16)
    eye_h = jnp.eye(MLA_HEADS, dtype=F32)
    w['w_uvbd'] = jnp.einsum('rhv,hk->hrkv', p['w_uv'], eye_h).reshape(MLA_HEADS * kv_rank, -1).astype(BF16)
    eye_g = jnp.eye(NSA_GROUPS, dtype=F32)
    for nm in ('k', 'v'):
        w1 = p['cmp_w1_' + nm].reshape(2, CMP_STRIDE, NSA_DH, -1)
        big = jnp.einsum('ajdh,gk->jgdakh', w1, eye_g)
        w['cmp_w1' + nm] = big.reshape(CHUNK_FEATS, -1).astype(BF16)
        pos = p['cmp_pos_' + nm].reshape(2, CMP_STRIDE, 1, NSA_DH)
        pos = jnp.broadcast_to(pos, (2, CMP_STRIDE, NSA_GROUPS, NSA_DH)).reshape(2, CHUNK_FEATS)
        w['cmp_pos' + nm] = jnp.pad(pos, ((0, 6), (0, 0))).astype(BF16)
        w2 = p['cmp_w2_' + nm]
        w['cmp_w2' + nm] = jnp.einsum('hd,gk->ghkd', w2, eye_g).reshape(NSA_GROUPS * w2.shape[0], -1).astype(BF16)
    ge = np.zeros((LANE, 3 * NSA_HEADS * NSA_DH), np.float32)
    for h in range(NSA_HEADS):
        for i in range(3):
            ge[h * 3 + i, i * NSA_HEADS * NSA_DH + h * NSA_DH:i * NSA_HEADS * NSA_DH + (h + 1) * NSA_DH] = 1.0
    w['gate_expand'] = jnp.asarray(ge)
    for k in ('w_proj_mla', 'w_proj_nsa', 'w_out', 'w_gate', 'w_up', 'w_down'):
        w[k] = p[k].astype(BF16)
    return w


def _rope_tables(pos):
    inv = ROPE_THETA ** (-jnp.arange(0, MLA_D_ROPE, 2, dtype=F32) / MLA_D_ROPE)
    ang = pos.astype(F32)[:, None] * inv[None, :]
    cos, sin = jnp.cos(ang), jnp.sin(ang)
    n = pos.shape[0]
    pad = jnp.zeros((n, HEAD_PAD - MLA_D_NOPE - MLA_D_ROPE), F32)
    cosq = jnp.concatenate([jnp.ones((n, MLA_D_NOPE), F32), cos, cos, pad], axis=1)
    sinq = jnp.concatenate([jnp.zeros((n, MLA_D_NOPE), F32), sin, sin, pad], axis=1)
    return cosq, sinq


_NSA_GROUPS_SPEC = tuple((tuple(range(g * NSA_HPG, (g + 1) * NSA_HPG)), g) for g in range(NSA_GROUPS))
_MLA_GROUPS_SPEC = tuple(((h,), h) for h in range(MLA_HEADS))


def _prompt(x_prompt, w):
    b, t, d = x_prompt.shape
    n = b * t
    x = x_prompt.reshape(n, d)
    tm = ROW_TILE
    cosq, sinq = _rope_tables(jnp.arange(t, dtype=jnp.int32))
    (q_mla, k_mla, v_mla, ckv, krp, qn, kc, vc, ks, vs, kw, vw, kc_b, vc_b, ks_b, vs_b, kw_b, vw_b, gn, ga, gb) = _inproj(
        x, w, cosq, sinq, sample=False, tm=tm, tab_blocks=t // tm)
    o_mla = _flash(q_mla, k_mla, v_mla, None, batch=b, seq=t, tq=MLA_TILES[0], tk=MLA_TILES[1], groups=_MLA_GROUPS_SPEC, par=8,
                   dq=HEAD_PAD, dk=HEAD_PAD, dv=MLA_D_V, window=0, base2=True, name="mla_prompt")
    kcc, vcc = _compress_prompt(kc_b, vc_b, w, batch=b, seq=t)
    o_cmp, sel_bias = _cmp_prompt(qn, kcc, vcc, batch=b, seq=t, tq=CMP_TQ)
    o_slc = _flash(qn, ks_b, vs_b, sel_bias, batch=b, seq=t, tq=SLC_TILES[0], tk=SLC_TILES[1], groups=_NSA_GROUPS_SPEC, par=2,
                   dq=LANE, dk=LANE, dv=NSA_DH, window=0, base2=False, name="slc_prompt")
    o_win = _win_prompt(qn, kw_b, vw_b, batch=b, seq=t, tq=WINDOW // 2, groups=_NSA_GROUPS_SPEC, dv=NSA_DH)
    dff = w['w_gate'].shape[1]
    y, gtail = _finish(x, o_mla, o_cmp, o_slc, o_win, gn, ga, gb, None, w, tm=tm, period=t, full_g=False,
                       latent=False)
    kv4 = lambda a: a.reshape(1, b, NSA_GROUPS, NSA_DH, a.shape[-1]).transpose(0, 1, 4, 2, 3)
    n_keep = min(WINDOW, t)
    kw, vw = kw[:, :, t - n_keep:], vw[:, :, t - n_keep:]
    conv_state = gtail.reshape(b, t // tm, 8, dff)[:, -1, 8 - (CONV_W - 1):, :]
    states = (ckv.reshape(1, b, t, -1), krp[:, MLA_D_NOPE:MLA_D_NOPE + MLA_D_ROPE].reshape(1, b, t, MLA_D_ROPE),
              kv4(kc), kv4(vc), kv4(ks), kv4(vs), kv4(kw), kv4(vw), conv_state[None])
    return y.reshape(b, t, d), states


def _page_copies(pt_ref, pools, bufs, sems, step, slot, pp):
    copies = []
    for k in range(pp):
        page = pt_ref[step * pp + k]
        for pool, buf, sem in zip(pools, bufs, sems):
            copies.append(pltpu.make_async_copy(pool.at[page], buf.at[slot, k], sem.at[slot]))
    return copies


def _start_all(copies, n_pools):
    for i, c in enumerate(copies):
        c.start(priority=(i // n_pools) % 2)


def _stream_pages(pt_ref, pools, bufs, sems, pp):
    step = pl.program_id(0) * pl.num_programs(1) + pl.program_id(1)
    total = pl.num_programs(0) * pl.num_programs(1)
    slot = step % 2

    @pl.when(step == 0)
    def _():
        _start_all(_page_copies(pt_ref, pools, bufs, sems, step, slot, pp), len(pools))

    @pl.when(step + 1 < total)
    def _():
        _start_all(_page_copies(pt_ref, pools, bufs, sems, step + 1, 1 - slot, pp), len(pools))

    for c in _page_copies(pt_ref, pools, bufs, sems, step, slot, pp):
        c.wait()
    return slot


def _softmax_update(sc, v, m_scr, l_scr, acc_scr, v_transposed=False):
    m_old = m_scr[...]
    m_new = jnp.maximum(m_old, jnp.max(sc, axis=-1, keepdims=True))
    p = jnp.exp(sc - m_new)
    alpha = jnp.exp(m_old - m_new)
    l_scr[...] = alpha * l_scr[...] + jnp.sum(p, axis=-1, keepdims=True)
    pv = _dot_nt(p.astype(BF16), v) if v_transposed else _dot(p.astype(BF16), v)
    acc_scr[...] = alpha * acc_scr[...] + pv
    m_scr[...] = m_new


def _mla_decode_step(slot, q_ref, knew_ref, o_ref, cbuf, rbuf, kscr, krscr, m_scr, l_scr, acc_scr, *, pp, td, kv_rank):
    s = pl.program_id(1)
    q = q_ref[...]
    rows = q.shape[0]

    @pl.when(s == 0)
    def _():
        m_scr[...] = jnp.full_like(m_scr, NEG)
        l_scr[...] = jnp.zeros_like(l_scr)
        acc_scr[...] = jnp.zeros_like(acc_scr)
        kn = knew_ref[...]
        trow = _vdiv(_iota((rows, 1), 0), MLA_HEADS)
        col = _iota((1, kn.shape[0]), 1)
        sc = jnp.where((col <= trow) & (col < td), _dot_nt(q, kn), NEG)
        _softmax_update(sc, kn[:, :kv_rank], m_scr, l_scr, acc_scr)

    for k in range(pp):
        kscr[k * PAGE_SIZE:(k + 1) * PAGE_SIZE, :] = cbuf[slot, k].astype(BF16)
        krscr[:, k * PAGE_SIZE:(k + 1) * PAGE_SIZE] = rbuf[slot, k].astype(BF16)
    kt = kscr[...]
    sc = _dot_nt(q[:, :kv_rank], kt) + _dot(q[:, kv_rank:kv_rank + MLA_D_ROPE], krscr[...])
    _softmax_update(sc, kt, m_scr, l_scr, acc_scr)

    @pl.when(s == pl.num_programs(1) - 1)
    def _():
        o_ref[...] = acc_scr[...] / l_scr[...]


def _alibi_rows(rows, td):
    r = _iota((rows, 1), 0)
    return _vdiv(r, td), _vmod(r, td)


def _slope_rows(hrow, g):
    slope = jnp.zeros(hrow.shape, F32)
    for hh in range(NSA_HPG):
        slope = jnp.where(hrow == hh, SLOPES[g * NSA_HPG + hh], slope)
    return slope


def _cmp_decode_step(slot, q_ref, w1k_ref, w1v_ref, pk_ref, pv_ref, w2k_ref, w2v_ref, o_ref, score_ref,
                     kbuf, vbuf, kp0, kp1, vp0, vp1, yk_scr, yv_scr, *, pp, td, past, n_cmp, n_sel_pad):
    s = pl.program_id(1)
    cpp = PAGE_SIZE // CMP_STRIDE
    hp = pp // 2
    for half, (kp_scr, vp_scr) in enumerate(((kp0, vp0), (kp1, vp1))):
        for k in range(hp):
            kp_scr[k * PAGE_SIZE:(k + 1) * PAGE_SIZE, :] = kbuf[slot, half * hp + k].T
            vp_scr[k * PAGE_SIZE:(k + 1) * PAGE_SIZE, :] = vbuf[slot, half * hp + k].T
    chunk_rows = lambda scr: jnp.concatenate(
        [scr[pl.ds(j, hp * cpp, stride=CMP_STRIDE), :] for j in range(CMP_STRIDE)], axis=1).astype(BF16)
    for half, (kp_scr, vp_scr) in enumerate(((kp0, vp0), (kp1, vp1))):
        r0 = pl.multiple_of(s * (pp * cpp) + half * (hp * cpp), hp * cpp)
        yk_scr[pl.ds(r0, hp * cpp), :] = _dot(chunk_rows(kp_scr), w1k_ref[...])
        yv_scr[pl.ds(r0, hp * cpp), :] = _dot(chunk_rows(vp_scr), w1v_ref[...])

    @pl.when(s == pl.num_programs(1) - 1)
    def _():
        kcc = _compress_finish(yk_scr[...], _dot(pk_ref[...], w1k_ref[...]), w2k_ref).astype(BF16)
        vcc = _compress_finish(yv_scr[...], _dot(pv_ref[...], w1v_ref[...]), w2v_ref).astype(BF16)
        ncp = kcc.shape[0]
        rows = NSA_HPG * td
        hrow, trow = _alibi_rows(rows, td)
        cidx = _iota((1, ncp), 1)
        dist = (past + trow) - (cidx * CMP_STRIDE + CMP_BLOCK - 1)
        mask = (dist >= 0) & (cidx < n_cmp)
        distf = dist.astype(F32)
        tsum = (_vmod(_iota((8, rows), 1), td) == _iota((8, rows), 0)).astype(F32)
        ov = _overlap(ncp, n_sel_pad)
        t8 = _iota((8, 1), 0)
        cur = _vdiv(past + t8, SLC_BLOCK)
        jj = _iota((8, n_sel_pad), 1)
        for g in range(NSA_GROUPS):
            sc = _dot_nt(q_ref[g], kcc) - _slope_rows(hrow, g) * distf
            sc = jnp.where(mask, sc, NEG)
            m = jnp.max(sc, axis=-1, keepdims=True)
            p = jnp.where(mask, jnp.exp(sc - m), 0.0)
            l = jnp.sum(p, axis=-1, keepdims=True)
            p = p / jnp.where(l > 0.0, l, 1.0)
            o_ref[g] = _dot(p.astype(BF16), vcc)
            imp = _dot_exact(tsum, p)
            score_ref[g] = _force_scores(_dot_exact(imp, ov), cur, jj)


def _mla_cmp_decode_kernel(pt_ref, qm_ref, knew_ref, qc_ref, w1k_ref, w1v_ref, pk_ref, pv_ref, w2k_ref, w2v_ref,
                           ckv_hbm, kr_hbm, ck_hbm, cv_hbm, olat_ref, ocmp_ref, score_ref,
                           cbuf, rbuf, kbuf, vbuf, csem, rsem, ksem, vsem,
                           kscr, krscr, m_scr, l_scr, acc_scr, kp0, kp1, vp0, vp1, yk_scr, yv_scr,
                           *, pp, td, kv_rank, past, n_cmp, n_sel_pad):
    slot = _stream_pages(pt_ref, (ckv_hbm, kr_hbm, ck_hbm, cv_hbm), (cbuf, rbuf, kbuf, vbuf),
                         (csem, rsem, ksem, vsem), pp)
    _mla_decode_step(slot, qm_ref, knew_ref, olat_ref, cbuf, rbuf, kscr, krscr, m_scr, l_scr, acc_scr,
                     pp=pp, td=td, kv_rank=kv_rank)
    _cmp_decode_step(slot, qc_ref, w1k_ref, w1v_ref, pk_ref, pv_ref, w2k_ref, w2v_ref, ocmp_ref, score_ref,
                     kbuf, vbuf, kp0, kp1, vp0, vp1, yk_scr, yv_scr, pp=pp, td=td, past=past, n_cmp=n_cmp,
                     n_sel_pad=n_sel_pad)


def _mla_cmp_decode(pt, qd_mla, knew, qd, ckv_pool, kr_pool, k_pool, v_pool, wts, *, td, pp, past, n_sel_pad):
    bd, mrows, qw = qd_mla.shape
    kv_rank = ckv_pool.shape[-1]
    n_pages = pt.shape[0] // bd
    cpp = PAGE_SIZE // CMP_STRIDE
    nch = n_pages * cpp
    n_cmp = (past + td) // CMP_STRIDE - CMP_BLOCK // CMP_STRIDE + 1
    assert (past + td) // CMP_STRIDE == nch, "new rows must not complete a chunk"
    per_b3 = lambda b, s, pt: (b, 0, 0)
    per_b = lambda b, s, pt: (b, 0, 0, 0)
    c2 = lambda b, s, pt: (0, 0)
    rows = qd.shape[2]
    in_specs = ([pl.BlockSpec((None, mrows, qw), per_b3), pl.BlockSpec((None,) + knew.shape[1:], per_b3),
                 pl.BlockSpec((None,) + qd.shape[1:], per_b)]
                + [pl.BlockSpec((CHUNK_FEATS, 2 * LANE), c2) for _ in range(2)]
                + [pl.BlockSpec((8, CHUNK_FEATS), c2) for _ in range(2)]
                + [pl.BlockSpec((LANE, LANE), c2) for _ in range(2)]
                + [pl.BlockSpec(memory_space=pl.ANY) for _ in range(4)])
    kern = functools.partial(_mla_cmp_decode_kernel, pp=pp, td=td, kv_rank=kv_rank, past=past, n_cmp=n_cmp,
                             n_sel_pad=n_sel_pad)
    pools = (ckv_pool, kr_pool, k_pool, v_pool)
    return pl.pallas_call(
        kern,
        grid_spec=pltpu.PrefetchScalarGridSpec(
            num_scalar_prefetch=1, grid=(bd, n_pages // pp), in_specs=in_specs,
            out_specs=[pl.BlockSpec((None, mrows, kv_rank), per_b3),
                       pl.BlockSpec((None, NSA_GROUPS, rows, LANE), per_b),
                       pl.BlockSpec((None, NSA_GROUPS, 8, n_sel_pad), per_b)],
            scratch_shapes=[pltpu.VMEM((2, pp) + p.shape[1:], F32) for p in pools]
            + [pltpu.SemaphoreType.DMA((2,)) for _ in pools]
            + [pltpu.VMEM((pp * PAGE_SIZE, kv_rank), BF16), pltpu.VMEM((MLA_D_ROPE, pp * PAGE_SIZE), BF16),
               pltpu.VMEM((mrows, 1), F32), pltpu.VMEM((mrows, 1), F32), pltpu.VMEM((mrows, kv_rank), F32)]
            + [pltpu.VMEM((pp // 2 * PAGE_SIZE, LANE), F32) for _ in range(4)]
            + [pltpu.VMEM((nch, 2 * LANE), F32) for _ in range(2)]),
        out_shape=[jax.ShapeDtypeStruct((bd, mrows, kv_rank), F32),
                   jax.ShapeDtypeStruct((bd, NSA_GROUPS, rows, LANE), F32),
                   jax.ShapeDtypeStruct((bd, NSA_GROUPS, 8, n_sel_pad), F32)],
        compiler_params=_cparams("arbitrary", "arbitrary"), name="mla_cmp_decode",
    )(pt, qd_mla, knew, qd, wts['cmp_w1k'], wts['cmp_w1v'], wts['cmp_posk'], wts['cmp_posv'],
      wts['cmp_w2k'], wts['cmp_w2v'], *pools)


def _rank_kernel(score_ref, cur_ref, sel_ref, *, n_sel):
    sc = score_ref[...]
    jj = _iota(sc.shape, 0)

    def body(i, rank):
        ri = score_ref[pl.ds(i, 1), :]
        beats = (ri > sc) | ((ri == sc) & (i < jj))
        return rank + beats.astype(F32)

    rank = lax.fori_loop(0, n_sel, body, jnp.zeros(sc.shape, F32))
    sel = (rank < float(min(SLC_TOP_N, n_sel))) & (jj <= cur_ref[...])
    sel_ref[...] = sel.astype(F32)


def _rank(score_t, cur, *, n_sel):
    full = lambda a: pl.BlockSpec(a.shape, lambda: (0,) * a.ndim)
    return pl.pallas_call(
        functools.partial(_rank_kernel, n_sel=n_sel), in_specs=[full(score_t), full(cur)],
        out_specs=full(score_t), out_shape=jax.ShapeDtypeStruct(score_t.shape, F32), name="rank_decode",
    )(score_t, cur)


def _slc_decode_kernel(pt_ref, cnt_ref, q_ref, sel_ref, selnew_ref, knew_ref, vnew_ref, kpos_ref, expand_ref,
                       k_hbm, v_hbm, o_ref, kbuf, vbuf, ksem, vsem, kscr, vscr, m_scr, l_scr, acc_scr,
                       *, pp, td, past):
    b, s, ns = pl.program_id(0), pl.program_id(1), pl.num_programs(1)
    step = b * ns + s
    slot = step % 2
    pools, bufs, sems = (k_hbm, v_hbm), (kbuf, vbuf), (ksem, vsem)
    active = s * pp < cnt_ref[b]
    wrap = s + 1 == ns
    nb = jnp.minimum(jnp.where(wrap, b + 1, b), pl.num_programs(0) - 1)
    next_active = (step + 1 < pl.num_programs(0) * ns) & (jnp.where(wrap, 0, s + 1) * pp < cnt_ref[nb])

    @pl.when((step == 0) & active)
    def _():
        _start_all(_page_copies(pt_ref, pools, bufs, sems, step, slot, pp), len(pools))

    @pl.when(next_active)
    def _():
        _start_all(_page_copies(pt_ref, pools, bufs, sems, step + 1, 1 - slot, pp), len(pools))

    rows = q_ref.shape[1]
    hrow, trow = _alibi_rows(rows, td)

    @pl.when(s == 0)
    def _():
        m_scr[...] = jnp.full_like(m_scr, NEG)
        l_scr[...] = jnp.zeros_like(l_scr)
        acc_scr[...] = jnp.zeros_like(acc_scr)
        kn = knew_ref[...]
        vn = vnew_ref[...]
        col = _iota((1, kn.shape[0]), 1)
        dist = trow - col
        for g in range(NSA_GROUPS):
            sc = _dot_nt(q_ref[g], kn) - _slope_rows(hrow, g) * dist.astype(F32)
            mask = (dist >= 0) & (col < td) & (selnew_ref[g] > 0.5)
            _softmax_update(jnp.where(mask, sc, NEG), vn, m_scr.at[g], l_scr.at[g], acc_scr.at[g])

    @pl.when(active)
    def _():
        for c in _page_copies(pt_ref, pools, bufs, sems, step, slot, pp):
            c.wait()
        for k in range(pp):
            kscr[:, k * PAGE_SIZE:(k + 1) * PAGE_SIZE] = kbuf[slot, k].astype(BF16)
            vscr[:, k * PAGE_SIZE:(k + 1) * PAGE_SIZE] = vbuf[slot, k].astype(BF16)
        kt = kscr[...]
        vt = vscr[...]
        distf = (past + trow).astype(F32) - kpos_ref[...]
        for g in range(NSA_GROUPS):
            sc = _dot(q_ref[g], kt) - _slope_rows(hrow, g) * distf
            selx = _dot(sel_ref[g], expand_ref[...]) > 0.5
            _softmax_update(jnp.where(selx, sc, NEG), vt, m_scr.at[g], l_scr.at[g], acc_scr.at[g], v_transposed=True)

    @pl.when(s == ns - 1)
    def _():
        o_ref[...] = acc_scr[...] / l_scr[...]


def _slc_decode(pt, counts, qd, sel16, selnew, knew, vnew, kpos, k_pool, v_pool, *, td, pp, past):
    bd = qd.shape[0]
    rows = qd.shape[2]
    n_pages = pt.shape[0] // bd
    tk = pp * PAGE_SIZE
    expand = (jnp.arange(tk)[None, :] // SLC_BLOCK == jnp.arange(tk // SLC_BLOCK)[:, None]).astype(BF16)
    per_b = lambda b, s, pt, cnt: (b, 0, 0, 0)
    per_b3 = lambda b, s, pt, cnt: (b, 0, 0)
    in_specs = ([pl.BlockSpec((None,) + qd.shape[1:], per_b),
                 pl.BlockSpec((None, None) + sel16.shape[2:], lambda b, s, pt, cnt: (b, s, 0, 0, 0)),
                 pl.BlockSpec((None,) + selnew.shape[1:], per_b),
                 pl.BlockSpec((None,) + knew.shape[1:], per_b3), pl.BlockSpec((None,) + vnew.shape[1:], per_b3),
                 pl.BlockSpec((None, None, 1, tk), lambda b, s, pt, cnt: (b, s, 0, 0)),
                 pl.BlockSpec(expand.shape, lambda b, s, pt, cnt: (0, 0))]
                + [pl.BlockSpec(memory_space=pl.ANY) for _ in range(2)])
    kern = functools.partial(_slc_decode_kernel, pp=pp, td=td, past=past)
    return pl.pallas_call(
        kern,
        grid_spec=pltpu.PrefetchScalarGridSpec(
            num_scalar_prefetch=2, grid=(bd, n_pages // pp), in_specs=in_specs,
            out_specs=pl.BlockSpec((None, NSA_GROUPS, rows, LANE), per_b),
            scratch_shapes=[pltpu.VMEM((2, pp) + k_pool.shape[1:], F32) for _ in range(2)]
            + [pltpu.SemaphoreType.DMA((2,)) for _ in range(2)]
            + [pltpu.VMEM((LANE, pp * PAGE_SIZE), BF16) for _ in range(2)]
            + [pltpu.VMEM((NSA_GROUPS, rows, 1), F32) for _ in range(2)] + [pltpu.VMEM((NSA_GROUPS, rows, LANE), F32)]),
        out_shape=jax.ShapeDtypeStruct((bd, NSA_GROUPS, rows, LANE), F32),
        compiler_params=_cparams("arbitrary", "arbitrary"), name="slc_decode",
    )(pt, counts, qd, sel16, selnew, knew, vnew, kpos, expand, k_pool, v_pool)


def _win_decode_kernel(q_ref, kst_ref, vst_ref, knew_ref, vnew_ref, o_ref, *, td, past):
    rows = q_ref.shape[1]
    hrow, trow = _alibi_rows(rows, td)
    nbuf = kst_ref.shape[1]
    kst = kst_ref[...].astype(BF16)
    vst = vst_ref[...].astype(BF16)
    kn = knew_ref[...]
    vn = vnew_ref[...]
    kpos = past - nbuf + _iota((1, nbuf), 1)
    d1 = (past + trow) - kpos
    m1 = (d1 >= 0) & (d1 < WINDOW) & (kpos >= 0)
    col = _iota((1, kn.shape[0]), 1)
    d2 = trow - col
    m2 = (d2 >= 0) & (d2 < WINDOW) & (col < td)
    for g in range(NSA_GROUPS):
        slope = _slope_rows(hrow, g)
        s1 = jnp.where(m1, _dot(q_ref[g], kst) - slope * d1.astype(F32), NEG)
        s2 = jnp.where(m2, _dot_nt(q_ref[g], kn) - slope * d2.astype(F32), NEG)
        m = jnp.maximum(jnp.max(s1, axis=-1, keepdims=True), jnp.max(s2, axis=-1, keepdims=True))
        p1 = jnp.exp(s1 - m)
        p2 = jnp.exp(s2 - m)
        l = jnp.sum(p1, axis=-1, keepdims=True) + jnp.sum(p2, axis=-1, keepdims=True)
        o_ref[g] = (_dot_nt(p1.astype(BF16), vst) + _dot(p2.astype(BF16), vn)) / l


def _win_decode(qd, kst, vst, knew, vnew, *, td, past):
    bd = qd.shape[0]
    rows = qd.shape[2]
    per_b = lambda b: (b, 0, 0, 0)
    per_b3 = lambda b: (b, 0, 0)
    blk3 = lambda a: pl.BlockSpec((None,) + a.shape[1:], per_b3)
    return pl.pallas_call(
        functools.partial(_win_decode_kernel, td=td, past=past), grid=(bd,),
        in_specs=[pl.BlockSpec((None,) + qd.shape[1:], per_b), blk3(kst), blk3(vst), blk3(knew), blk3(vnew)],
        out_specs=pl.BlockSpec((None, NSA_GROUPS, rows, LANE), per_b),
        out_shape=jax.ShapeDtypeStruct((bd, NSA_GROUPS, rows, LANE), F32),
        compiler_params=_cparams("parallel"), name="win_decode",
    )(qd, kst, vst, knew, vnew)


def _pad_rows(a, rows):
    return jnp.pad(a, ((0, 0), (0, rows - a.shape[1]), (0, 0)))


def _sample(x_sample, caches, page_table, w):
    (c_ckv, c_krope, c_cmp_k, c_cmp_v, c_slc_k, c_slc_v, s_win_k, s_win_v, s_conv) = caches
    bd, td, d = x_sample.shape
    n = bd * td
    n_pages = page_table.shape[1]
    past = n_pages * PAGE_SIZE
    kv_rank = c_ckv.shape[-1]
    x = x_sample.reshape(n, d)
    pos = past + jnp.arange(td, dtype=jnp.int32)
    cosq, sinq = _rope_tables(jnp.tile(pos, bd))
    (qrot, qabs, ckv, krp, qn, kc, vc, ks, vs, kw, vw, gn, ga, gb) = _inproj(
        x, w, cosq, sinq, sample=True, tm=n, tab_blocks=1)
    pt = page_table.reshape(-1)
    krope = krp[:, MLA_D_NOPE:MLA_D_NOPE + MLA_D_ROPE]
    qrope = qrot.reshape(bd, td, MLA_HEADS, HEAD_PAD)[..., MLA_D_NOPE:MLA_D_NOPE + MLA_D_ROPE]
    qpad = LANE - MLA_D_ROPE
    qd_mla = jnp.concatenate([qabs.reshape(bd, td, MLA_HEADS, kv_rank), qrope,
                              jnp.zeros((bd, td, MLA_HEADS, qpad), F32)], axis=-1)
    qd_mla = qd_mla.reshape(bd, td * MLA_HEADS, kv_rank + LANE).astype(BF16)
    knew = jnp.concatenate([ckv, krope, jnp.zeros((n, qpad), F32)], axis=-1).reshape(bd, td, -1)
    knew = _pad_rows(knew, 8).astype(BF16)
    pos_minor = lambda c: jnp.moveaxis(c, 1, -1).reshape(c.shape[0], -1, c.shape[1])
    qg = qn.astype(F32).reshape(bd, td, NSA_GROUPS, NSA_HPG, NSA_DH).transpose(0, 2, 3, 1, 4)
    qg = qg.reshape(bd, NSA_GROUPS, NSA_HPG * td, NSA_DH)
    lane_g = (jnp.arange(LANE) // NSA_DH)[None, :] == jnp.arange(NSA_GROUPS)[:, None]
    qd = jnp.where(lane_g[None, :, None, :], jnp.tile(qg, (1, 1, 1, NSA_GROUPS)), 0.0).astype(BF16)
    n_sel = -(-(past + td) // SLC_BLOCK)
    n_sel_pad = -(-n_sel // LANE) * LANE
    o_lat, o_cmp, score = _mla_cmp_decode(pt, qd_mla, knew, qd, c_ckv, pos_minor(c_krope), pos_minor(c_cmp_k),
                                          pos_minor(c_cmp_v), w, td=td, pp=min(2 * PAGES_PER_STEP, n_pages), past=past,
                                          n_sel_pad=n_sel_pad)
    cur = jnp.broadcast_to((pos // SLC_BLOCK).astype(jnp.int32), (bd, NSA_GROUPS, td)).reshape(1, -1)
    n_sel8 = -(-n_sel // 8) * 8
    sel_t = _rank(score[:, :, :td, :n_sel8].reshape(-1, n_sel8).T, cur, n_sel=n_sel)
    sel = jnp.pad(sel_t.T, ((0, 0), (0, n_sel_pad - n_sel8))).reshape(bd, NSA_GROUPS, td, n_sel_pad)
    pp_slc = min(PAGES_PER_STEP // 2, n_pages)
    bpp = PAGE_SIZE // SLC_BLOCK
    n_past_blk = past // SLC_BLOCK
    sel_pg = sel[..., :n_past_blk].reshape(bd, NSA_GROUPS * td, n_pages, bpp)
    need_page = sel_pg.max(axis=(1, 3)) > 0.5
    counts = need_page.sum(axis=-1).astype(jnp.int32)
    n_before = jnp.cumsum(need_page.astype(jnp.int32), axis=-1)
    u_before = jnp.cumsum(1 - need_page.astype(jnp.int32), axis=-1)
    slot_of = jnp.where(need_page, n_before - 1, counts[:, None] + u_before - 1)
    perm = slot_of[:, None, :] == jnp.arange(n_pages, dtype=jnp.int32)[None, :, None]
    order = jnp.sum(jnp.where(perm, jnp.arange(n_pages, dtype=jnp.int32), 0), axis=-1)
    pt_slc = jnp.sum(jnp.where(perm, page_table[:, None, :], 0), axis=-1).reshape(-1)
    sel_listed = jnp.einsum('bip,bxpc->bxic', perm.astype(BF16), sel_pg.astype(BF16), preferred_element_type=F32)
    bps = pp_slc * bpp
    sel_steps = sel_listed.reshape(bd, NSA_GROUPS, td, n_pages // pp_slc, bps).transpose(0, 3, 1, 2, 4)
    sel16 = jnp.tile(sel_steps, (1, 1, 1, NSA_HPG, 1)).astype(BF16)
    kpos = (order[:, :, None] * PAGE_SIZE + jnp.arange(PAGE_SIZE, dtype=jnp.int32)).astype(F32)
    kpos = kpos.reshape(bd, n_pages // pp_slc, 1, pp_slc * PAGE_SIZE)
    selnew = jnp.tile(jnp.broadcast_to(sel[..., n_past_blk:n_past_blk + 1], (bd, NSA_GROUPS, td, 8)), (1, 1, NSA_HPG, 1))
    new8 = lambda a: _pad_rows(a.reshape(bd, td, LANE), 8).astype(BF16)
    o_slc = _slc_decode(pt_slc, counts, qd, sel16, selnew, new8(ks), new8(vs), kpos, pos_minor(c_slc_k),
                        pos_minor(c_slc_v), td=td, pp=pp_slc, past=past)
    nbuf = s_win_k.shape[1]
    o_win = _win_decode(qd, pos_minor(s_win_k), pos_minor(s_win_v), new8(kw), new8(vw),
                        td=td, past=past)

    def heads_out(o):
        o = o.reshape(bd, NSA_GROUPS, NSA_HPG, td, NSA_GROUPS, NSA_DH)
        o = jnp.stack([o[:, g, :, :, g] for g in range(NSA_GROUPS)], axis=1)
        return o.transpose(0, 3, 1, 2, 4).reshape(n, NSA_HEADS * NSA_DH)

    dff = w['w_gate'].shape[1]
    assert td >= CONV_W - 1
    prev1 = jnp.concatenate([s_conv[:, 1:2], jnp.zeros((bd, td - 1, dff), F32)], axis=1)
    prev2 = jnp.concatenate([s_conv[:, 0:2], jnp.zeros((bd, td - 2, dff), F32)], axis=1)
    y, g = _finish(x, o_lat.reshape(n, MLA_HEADS * kv_rank), heads_out(o_cmp), heads_out(o_slc), heads_out(o_win),
                   gn, ga, gb, (prev1.reshape(n, dff), prev2.reshape(n, dff)), w, tm=n, period=td, full_g=True,
                   latent=True)
    kv4 = lambda a: a.reshape(1, bd, td, NSA_GROUPS, NSA_DH)
    win = lambda st, new: jnp.concatenate([st, new.reshape(bd, td, NSA_GROUPS, NSA_DH)], axis=1)[None, :, -nbuf:]
    conv_state = jnp.concatenate([s_conv, g.reshape(bd, td, dff)], axis=1)[None, :, -(CONV_W - 1):]
    states = (ckv.reshape(1, bd, td, kv_rank), krope.reshape(1, bd, td, MLA_D_ROPE), kv4(kc), kv4(vc), kv4(ks), kv4(vs),
              win(s_win_k, kw), win(s_win_v, vw), conv_state)
    return y.reshape(bd, td, d), states


def kernel(x_prompt, x_sample, cache_mla_ckv, cache_mla_krope, cache_nsa_cmp_k, cache_nsa_cmp_v, cache_nsa_slc_k, cache_nsa_slc_v, state_win_k, state_win_v, state_ffn_conv, page_table, norm1_g, w_in, q_norm_g, kv_norm_g, w_uq, w_uk, w_uv, cmp_pos_k, cmp_w1_k, cmp_w2_k, cmp_pos_v, cmp_w1_v, cmp_w2_v, w_proj_mla, w_proj_nsa, w_out, norm2_g, w_gate, w_up, conv_w, conv_b, w_down, norm_f_g):
    assert norm1_g.shape[0] == 1, "single-layer trunk"
    p = dict(norm1_g=norm1_g[0], w_in=w_in[0], q_norm_g=q_norm_g[0], kv_norm_g=kv_norm_g[0], w_uq=w_uq[0],
             w_uk=w_uk[0], w_uv=w_uv[0], cmp_pos_k=cmp_pos_k[0], cmp_w1_k=cmp_w1_k[0], cmp_w2_k=cmp_w2_k[0],
             cmp_pos_v=cmp_pos_v[0], cmp_w1_v=cmp_w1_v[0], cmp_w2_v=cmp_w2_v[0], w_proj_mla=w_proj_mla[0],
             w_proj_nsa=w_proj_nsa[0], w_out=w_out[0], norm2_g=norm2_g[0], w_gate=w_gate[0], w_up=w_up[0],
             conv_w=conv_w[0], conv_b=conv_b[0], w_down=w_down[0], norm_f_g=norm_f_g)
    w = _prep_weights(p)
    y_p, ps = _prompt(x_prompt, w)
    caches = (cache_mla_ckv[0], cache_mla_krope[0], cache_nsa_cmp_k[0], cache_nsa_cmp_v[0], cache_nsa_slc_k[0],
              cache_nsa_slc_v[0], state_win_k[0], state_win_v[0], state_ffn_conv[0])
    y_s, ss = _sample(x_sample, caches, page_table, w)
    out = [y_p, y_s]
    for a, b in zip(ps, ss):
        out += [a, b]
    return tuple(out)
```
